```python
import jax, jax.numpy as jnp
from jax import lax
import numpy as np

D_MODEL = 1024
BATCH = 8
SEQ = 8192
DEPTH = 1

CONV_WIDTH = D_MODEL
CONV_SIZE = 31
SSM_WIDTH = D_MODEL // 2
SSM_GROUP = 16
SSM_GROUPS = SSM_WIDTH // SSM_GROUP
SSM_STATE = 64
DT_MIN = 0.001
DT_MAX = 0.1
RMS_EPS = 1e-6
LN_EPS = 1e-5
IN_SPLITS = (2 * CONV_WIDTH, CONV_WIDTH, SSM_WIDTH, SSM_WIDTH, D_MODEL, D_MODEL)
IN_WIDTH = sum(IN_SPLITS)
IN_OFFSETS = [int(v) for v in np.cumsum(IN_SPLITS)[:-1]]

kernel_name = "hybrid_conformer_conv_s5_gated_block"


def rms_norm(x, g):
    xf = x.astype(jnp.float32)
    y = xf * lax.rsqrt(jnp.mean(xf * xf, axis=-1, keepdims=True) + RMS_EPS)
    return (y * g.astype(jnp.float32)).astype(x.dtype)


def layer_norm(x, g, b):
    xf = x.astype(jnp.float32)
    mu = jnp.mean(xf, axis=-1, keepdims=True)
    xc = xf - mu
    var = jnp.mean(xc * xc, axis=-1, keepdims=True)
    y = xc * lax.rsqrt(var + LN_EPS) * g.astype(jnp.float32) + b.astype(jnp.float32)
    return y.astype(x.dtype)


def causal_depthwise_conv(u, w, b):
    c = u.shape[-1]
    y = lax.conv_general_dilated(
        u, w[:, None, :].astype(u.dtype), window_strides=(1,),
        padding=[(CONV_SIZE - 1, 0)],
        dimension_numbers=("NWC", "WIO", "NWC"),
        feature_group_count=c)
    return y + b


def s5_mimo(u, lam_re, lam_im, log_dt, b_re, b_im, c_re, c_im, d):
    bsz, length, _ = u.shape
    f32 = jnp.float32
    uf = u.astype(f32).reshape(bsz, length, SSM_GROUPS, SSM_GROUP)
    dt = jnp.exp(log_dt.astype(f32))[:, None]
    lr = lam_re.astype(f32)
    li = lam_im.astype(f32)
    mag = jnp.exp(lr * dt)
    ar = mag * jnp.cos(li * dt)
    ai = mag * jnp.sin(li * dt)
    den = lr * lr + li * li
    zr = ((ar - 1.0) * lr + ai * li) / den
    zi = (ai * lr - (ar - 1.0) * li) / den
    br = b_re.astype(f32)
    bi = b_im.astype(f32)
    bbar_re = zr[..., None] * br - zi[..., None] * bi
    bbar_im = zr[..., None] * bi + zi[..., None] * br
    bu_re = jnp.einsum("blgh,gph->blgp", uf, bbar_re)
    bu_im = jnp.einsum("blgh,gph->blgp", uf, bbar_im)
    a_re = jnp.broadcast_to(ar, bu_re.shape)
    a_im = jnp.broadcast_to(ai, bu_re.shape)

    def combine(e1, e2):
        a1r, a1i, b1r, b1i = e1
        a2r, a2i, b2r, b2i = e2
        return (a2r * a1r - a2i * a1i,
                a2r * a1i + a2i * a1r,
                a2r * b1r - a2i * b1i + b2r,
                a2r * b1i + a2i * b1r + b2i)

    _, _, s_re, s_im = lax.associative_scan(combine, (a_re, a_im, bu_re, bu_im), axis=1)
    y = (jnp.einsum("blgp,ghp->blgh", s_re, c_re.astype(f32))
         - jnp.einsum("blgp,ghp->blgh", s_im, c_im.astype(f32))
         + d.astype(f32) * uf)
    return y.reshape(bsz, length, SSM_WIDTH).astype(u.dtype)


def hybrid_layer(x, pre_g, w_in, conv_w, conv_b, conv_ln_g, conv_ln_b, w_conv_out,
                 lam_re, lam_im, log_dt, b_re, b_im, c_re, c_im, d,
                 w_glu, b_glu, w_ssm_out, w_out, post_g):
    h = rms_norm(x, pre_g)
    proj = jnp.einsum("bld,de->ble", h, w_in)
    conv_in, z_c, u_s, z_s, g_c, g_s = jnp.split(proj, IN_OFFSETS, axis=-1)
    ca, cb = jnp.split(conv_in, 2, axis=-1)
    cu = ca * jax.nn.sigmoid(cb)
    cu = causal_depthwise_conv(cu, conv_w, conv_b)
    cu = jax.nn.silu(layer_norm(cu, conv_ln_g, conv_ln_b))
    conv_out = jnp.einsum("blc,cd->bld", cu * jax.nn.silu(z_c), w_conv_out)
    y = jax.nn.gelu(s5_mimo(u_s, lam_re, lam_im, log_dt, b_re, b_im, c_re, c_im, d))
    y = y * jax.nn.sigmoid(jnp.einsum("blc,ce->ble", y, w_glu) + b_glu)
    ssm_out = jnp.einsum("blc,cd->bld", y * jax.nn.silu(z_s), w_ssm_out)
    merged = jax.nn.sigmoid(g_c) * conv_out + jax.nn.sigmoid(g_s) * ssm_out
    out = jnp.einsum("bld,de->ble", merged, w_out)
    return x + rms_norm(out, post_g)


def _fwd_setup_inputs(seed: int = 0) -> dict:
    key = jax.random.key(seed)
    ks = jax.random.split(key, 24)
    f32 = jnp.float32
    nrm = lambda k, shape, scale: jax.random.normal(k, shape, f32) * scale
    L, G, P, H = DEPTH, SSM_GROUPS, SSM_STATE, SSM_GROUP
    n = jnp.arange(P, dtype=f32)
    log_dt = (jnp.log(DT_MIN) + jax.random.uniform(ks[11], (L, G), f32)
              * (jnp.log(DT_MAX) - jnp.log(DT_MIN)))
    return {
        "x": jax.random.normal(ks[0], (BATCH, SEQ, D_MODEL), f32),
        "pre_norm_gain": 1.0 + nrm(ks[1], (L, D_MODEL), 0.05),
        "w_in": nrm(ks[2], (L, D_MODEL, IN_WIDTH), D_MODEL ** -0.5),
        "conv_w": nrm(ks[3], (L, CONV_SIZE, CONV_WIDTH), CONV_SIZE ** -0.5),
        "conv_b": nrm(ks[4], (L, CONV_WIDTH), 0.02),
        "conv_ln_gain": 1.0 + nrm(ks[5], (L, CONV_WIDTH), 0.05),
        "conv_ln_bias": nrm(ks[6], (L, CONV_WIDTH), 0.02),
        "w_conv_out": nrm(ks[7], (L, CONV_WIDTH, D_MODEL), CONV_WIDTH ** -0.5),
        "ssm_lambda_re": -0.5 + nrm(ks[8], (L, G, P), 0.01),
        "ssm_lambda_im": jnp.pi * n + nrm(ks[9], (L, G, P), 0.01),
        "ssm_log_dt": log_dt,
        "ssm_b_re": nrm(ks[12], (L, G, P, H), (2.0 * H) ** -0.5),
        "ssm_b_im": nrm(ks[13], (L, G, P, H), (2.0 * H) ** -0.5),
        "ssm_c_re": nrm(ks[14], (L, G, H, P), (2.0 * P) ** -0.5),
        "ssm_c_im": nrm(ks[15], (L, G, H, P), (2.0 * P) ** -0.5),
        "ssm_d": nrm(ks[16], (L, G, H), 1.0),
        "w_ssm_glu": nrm(ks[17], (L, SSM_WIDTH, SSM_WIDTH), SSM_WIDTH ** -0.5),
        "b_ssm_glu": nrm(ks[18], (L, SSM_WIDTH), 0.02),
        "w_ssm_out": nrm(ks[19], (L, SSM_WIDTH, D_MODEL), SSM_WIDTH ** -0.5),
        "w_out": nrm(ks[20], (L, D_MODEL, D_MODEL), D_MODEL ** -0.5),
        "post_norm_gain": 1.0 + nrm(ks[21], (L, D_MODEL), 0.05),
    }


def _fwd_reference(x, pre_norm_gain, w_in, conv_w, conv_b, conv_ln_gain, conv_ln_bias, w_conv_out,
              ssm_lambda_re, ssm_lambda_im, ssm_log_dt, ssm_b_re, ssm_b_im, ssm_c_re, ssm_c_im,
              ssm_d, w_ssm_glu, b_ssm_glu, w_ssm_out, w_out, post_norm_gain):
    for l in range(DEPTH):
        x = hybrid_layer(x, pre_norm_gain[l], w_in[l], conv_w[l], conv_b[l], conv_ln_gain[l],
                         conv_ln_bias[l], w_conv_out[l], ssm_lambda_re[l], ssm_lambda_im[l],
                         ssm_log_dt[l], ssm_b_re[l], ssm_b_im[l], ssm_c_re[l], ssm_c_im[l],
                         ssm_d[l], w_ssm_glu[l], b_ssm_glu[l], w_ssm_out[l], w_out[l],
                         post_norm_gain[l])
    return x


import jax as _jax
import jax.numpy as _jnp

TWIN_FORMAT = 'train_step'
FWD_PARAMS = ['x', 'pre_norm_gain', 'w_in', 'conv_w', 'conv_b', 'conv_ln_gain', 'conv_ln_bias', 'w_conv_out', 'ssm_lambda_re', 'ssm_lambda_im', 'ssm_log_dt', 'ssm_b_re', 'ssm_b_im', 'ssm_c_re', 'ssm_c_im', 'ssm_d', 'w_ssm_glu', 'b_ssm_glu', 'w_ssm_out', 'w_out', 'post_norm_gain']
TWIN_WEIGHTS = ['pre_norm_gain', 'w_in', 'conv_w', 'conv_b', 'conv_ln_gain', 'conv_ln_bias', 'w_conv_out', 'ssm_lambda_re', 'ssm_lambda_im', 'ssm_log_dt', 'ssm_b_re', 'ssm_b_im', 'ssm_c_re', 'ssm_c_im', 'ssm_d', 'w_ssm_glu', 'b_ssm_glu', 'w_ssm_out', 'w_out', 'post_norm_gain']
TWIN_DIFF_INPUT = 'x'
TWIN_INPUTS = ['x', 'pre_norm_gain', 'w_in', 'conv_w', 'conv_b', 'conv_ln_gain', 'conv_ln_bias', 'w_conv_out', 'ssm_lambda_re', 'ssm_lambda_im', 'ssm_log_dt', 'ssm_b_re', 'ssm_b_im', 'ssm_c_re', 'ssm_c_im', 'ssm_d', 'w_ssm_glu', 'b_ssm_glu', 'w_ssm_out', 'w_out', 'post_norm_gain', 'loss_target', 'm_pre_norm_gain', 'm_w_in', 'm_conv_w', 'm_conv_b', 'm_conv_ln_gain', 'm_conv_ln_bias', 'm_w_conv_out', 'm_ssm_lambda_re', 'm_ssm_lambda_im', 'm_ssm_log_dt', 'm_ssm_b_re', 'm_ssm_b_im', 'm_ssm_c_re', 'm_ssm_c_im', 'm_ssm_d', 'm_w_ssm_glu', 'm_b_ssm_glu', 'm_w_ssm_out', 'm_w_out', 'm_post_norm_gain', 'v_pre_norm_gain', 'v_w_in', 'v_conv_w', 'v_conv_b', 'v_conv_ln_gain', 'v_conv_ln_bias', 'v_w_conv_out', 'v_ssm_lambda_re', 'v_ssm_lambda_im', 'v_ssm_log_dt', 'v_ssm_b_re', 'v_ssm_b_im', 'v_ssm_c_re', 'v_ssm_c_im', 'v_ssm_d', 'v_w_ssm_glu', 'v_b_ssm_glu', 'v_w_ssm_out', 'v_w_out', 'v_post_norm_gain']
TWIN_OUTPUTS = ['loss', 'grad_x', 'grad_pre_norm_gain', 'grad_w_in', 'grad_conv_w', 'grad_conv_b', 'grad_conv_ln_gain', 'grad_conv_ln_bias', 'grad_w_conv_out', 'grad_ssm_lambda_re', 'grad_ssm_lambda_im', 'grad_ssm_log_dt', 'grad_ssm_b_re', 'grad_ssm_b_im', 'grad_ssm_c_re', 'grad_ssm_c_im', 'grad_ssm_d', 'grad_w_ssm_glu', 'grad_b_ssm_glu', 'grad_w_ssm_out', 'grad_w_out', 'grad_post_norm_gain', 'delta_pre_norm_gain', 'delta_w_in', 'delta_conv_w', 'delta_conv_b', 'delta_conv_ln_gain', 'delta_conv_ln_bias', 'delta_w_conv_out', 'delta_ssm_lambda_re', 'delta_ssm_lambda_im', 'delta_ssm_log_dt', 'delta_ssm_b_re', 'delta_ssm_b_im', 'delta_ssm_c_re', 'delta_ssm_c_im', 'delta_ssm_d', 'delta_w_ssm_glu', 'delta_b_ssm_glu', 'delta_w_ssm_out', 'delta_w_out', 'delta_post_norm_gain', 'new_m_pre_norm_gain', 'new_m_w_in', 'new_m_conv_w', 'new_m_conv_b', 'new_m_conv_ln_gain', 'new_m_conv_ln_bias', 'new_m_w_conv_out', 'new_m_ssm_lambda_re', 'new_m_ssm_lambda_im', 'new_m_ssm_log_dt', 'new_m_ssm_b_re', 'new_m_ssm_b_im', 'new_m_ssm_c_re', 'new_m_ssm_c_im', 'new_m_ssm_d', 'new_m_w_ssm_glu', 'new_m_b_ssm_glu', 'new_m_w_ssm_out', 'new_m_w_out', 'new_m_post_norm_gain', 'new_v_pre_norm_gain', 'new_v_w_in', 'new_v_conv_w', 'new_v_conv_b', 'new_v_conv_ln_gain', 'new_v_conv_ln_bias', 'new_v_w_conv_out', 'new_v_ssm_lambda_re', 'new_v_ssm_lambda_im', 'new_v_ssm_log_dt', 'new_v_ssm_b_re', 'new_v_ssm_b_im', 'new_v_ssm_c_re', 'new_v_ssm_c_im', 'new_v_ssm_d', 'new_v_w_ssm_glu', 'new_v_b_ssm_glu', 'new_v_w_ssm_out', 'new_v_w_out', 'new_v_post_norm_gain']
TWIN_LEAF_KINDS = {'loss': 'loss', 'grad_x': 'grad_x', 'grad_pre_norm_gain': 'grad_w', 'grad_w_in': 'grad_w', 'grad_conv_w': 'grad_w', 'grad_conv_b': 'grad_w', 'grad_conv_ln_gain': 'grad_w', 'grad_conv_ln_bias': 'grad_w', 'grad_w_conv_out': 'grad_w', 'grad_ssm_lambda_re': 'grad_w', 'grad_ssm_lambda_im': 'grad_w', 'grad_ssm_log_dt': 'grad_w', 'grad_ssm_b_re': 'grad_w', 'grad_ssm_b_im': 'grad_w', 'grad_ssm_c_re': 'grad_w', 'grad_ssm_c_im': 'grad_w', 'grad_ssm_d': 'grad_w', 'grad_w_ssm_glu': 'grad_w', 'grad_b_ssm_glu': 'grad_w', 'grad_w_ssm_out': 'grad_w', 'grad_w_out': 'grad_w', 'grad_post_norm_gain': 'grad_w', 'delta_pre_norm_gain': 'delta_w', 'delta_w_in': 'delta_w', 'delta_conv_w': 'delta_w', 'delta_conv_b': 'delta_w', 'delta_conv_ln_gain': 'delta_w', 'delta_conv_ln_bias': 'delta_w', 'delta_w_conv_out': 'delta_w', 'delta_ssm_lambda_re': 'delta_w', 'delta_ssm_lambda_im': 'delta_w', 'delta_ssm_log_dt': 'delta_w', 'delta_ssm_b_re': 'delta_w', 'delta_ssm_b_im': 'delta_w', 'delta_ssm_c_re': 'delta_w', 'delta_ssm_c_im': 'delta_w', 'delta_ssm_d': 'delta_w', 'delta_w_ssm_glu': 'delta_w', 'delta_b_ssm_glu': 'delta_w', 'delta_w_ssm_out': 'delta_w', 'delta_w_out': 'delta_w', 'delta_post_norm_gain': 'delta_w', 'new_m_pre_norm_gain': 'new_m', 'new_m_w_in': 'new_m', 'new_m_conv_w': 'new_m', 'new_m_conv_b': 'new_m', 'new_m_conv_ln_gain': 'new_m', 'new_m_conv_ln_bias': 'new_m', 'new_m_w_conv_out': 'new_m', 'new_m_ssm_lambda_re': 'new_m', 'new_m_ssm_lambda_im': 'new_m', 'new_m_ssm_log_dt': 'new_m', 'new_m_ssm_b_re': 'new_m', 'new_m_ssm_b_im': 'new_m', 'new_m_ssm_c_re': 'new_m', 'new_m_ssm_c_im': 'new_m', 'new_m_ssm_d': 'new_m', 'new_m_w_ssm_glu': 'new_m', 'new_m_b_ssm_glu': 'new_m', 'new_m_w_ssm_out': 'new_m', 'new_m_w_out': 'new_m', 'new_m_post_norm_gain': 'new_m', 'new_v_pre_norm_gain': 'new_v', 'new_v_w_in': 'new_v', 'new_v_conv_w': 'new_v', 'new_v_conv_b': 'new_v', 'new_v_conv_ln_gain': 'new_v', 'new_v_conv_ln_bias': 'new_v', 'new_v_w_conv_out': 'new_v', 'new_v_ssm_lambda_re': 'new_v', 'new_v_ssm_lambda_im': 'new_v', 'new_v_ssm_log_dt': 'new_v', 'new_v_ssm_b_re': 'new_v', 'new_v_ssm_b_im': 'new_v', 'new_v_ssm_c_re': 'new_v', 'new_v_ssm_c_im': 'new_v', 'new_v_ssm_d': 'new_v', 'new_v_w_ssm_glu': 'new_v', 'new_v_b_ssm_glu': 'new_v', 'new_v_w_ssm_out': 'new_v', 'new_v_w_out': 'new_v', 'new_v_post_norm_gain': 'new_v'}


def _forward(args):
    return _fwd_reference(*[args[k] for k in FWD_PARAMS])


def _output_shape():
    def fwd():
        inp = _fwd_setup_inputs(0)
        return _fwd_reference(*[inp[k] for k in FWD_PARAMS])
    out = _jax.eval_shape(fwd)
    return out.shape, out.dtype

N_MICROBATCH = 1
ADAM_LR = 0.001
ADAM_B1 = 0.9
ADAM_B2 = 0.999
ADAM_EPS = 1e-08
ADAM_WD = 0.01
ADAM_STEP = 10
PER_EXAMPLE_BATCH_AXIS = {'x': 0, 'loss_target': 0}
SHARED_INPUTS = []
_WEIGHT_DTYPES = {'pre_norm_gain': _jnp.float32, 'w_in': _jnp.float32, 'conv_w': _jnp.float32, 'conv_b': _jnp.float32, 'conv_ln_gain': _jnp.float32, 'conv_ln_bias': _jnp.float32, 'w_conv_out': _jnp.float32, 'ssm_lambda_re': _jnp.float32, 'ssm_lambda_im': _jnp.float32, 'ssm_log_dt': _jnp.float32, 'ssm_b_re': _jnp.float32, 'ssm_b_im': _jnp.float32, 'ssm_c_re': _jnp.float32, 'ssm_c_im': _jnp.float32, 'ssm_d': _jnp.float32, 'w_ssm_glu': _jnp.float32, 'b_ssm_glu': _jnp.float32, 'w_ssm_out': _jnp.float32, 'w_out': _jnp.float32, 'post_norm_gain': _jnp.float32}
MOMENT_SCALE = {'pre_norm_gain': 6.216872e-01, 'w_in': 2.687653e-01, 'conv_w': 3.340688e-01, 'conv_b': 1.464491e+00, 'conv_ln_gain': 6.190391e-01, 'conv_ln_bias': 8.820245e-01, 'w_conv_out': 4.239459e-01, 'ssm_lambda_re': 1.523848e-02, 'ssm_lambda_im': 1.763163e-02, 'ssm_log_dt': 7.449074e+00, 'ssm_b_re': 9.452957e-03, 'ssm_b_im': 9.172981e-03, 'ssm_c_re': 1.929858e-02, 'ssm_c_im': 1.899150e-02, 'ssm_d': 5.469718e-01, 'w_ssm_glu': 9.390613e-02, 'b_ssm_glu': 2.247951e-01, 'w_ssm_out': 3.100818e-01, 'w_out': 5.375949e-01, 'post_norm_gain': 6.438901e+01}


def _to_microbatches(a, axis):
    t = _jnp.moveaxis(a, axis, 0)
    t = t.reshape((N_MICROBATCH, t.shape[0] // N_MICROBATCH) + t.shape[1:])
    return _jnp.moveaxis(t, 1, axis + 1)


def setup_inputs(seed: int = 0) -> dict:
    inp = _fwd_setup_inputs(seed)
    key = _jax.random.fold_in(_jax.random.key(seed), 7919)
    shape, _ = _output_shape()
    out = dict(inp)
    out["loss_target"] = _jax.random.normal(_jax.random.fold_in(key, 0), shape, _jnp.float32)
    for i, name in enumerate(TWIN_WEIGHTS):
        w = inp[name].astype(_jnp.float32)
        if MOMENT_SCALE is None:
            s = _jnp.sqrt(_jnp.mean(_jnp.square(w)) + 1e-30)
        else:
            s = MOMENT_SCALE[name]
        km, kv = _jax.random.split(_jax.random.fold_in(key, i + 1))
        out[name] = w
        out["m_" + name] = s * _jax.random.normal(km, w.shape, _jnp.float32)
        out["v_" + name] = (s * s) * _jax.random.uniform(kv, w.shape, _jnp.float32, 0.5, 1.5)
    if N_MICROBATCH > 1:
        for name, axis in PER_EXAMPLE_BATCH_AXIS.items():
            out[name] = _to_microbatches(out[name], axis)
    return {'x': out['x'], 'pre_norm_gain': out['pre_norm_gain'], 'w_in': out['w_in'], 'conv_w': out['conv_w'], 'conv_b': out['conv_b'], 'conv_ln_gain': out['conv_ln_gain'], 'conv_ln_bias': out['conv_ln_bias'], 'w_conv_out': out['w_conv_out'], 'ssm_lambda_re': out['ssm_lambda_re'], 'ssm_lambda_im': out['ssm_lambda_im'], 'ssm_log_dt': out['ssm_log_dt'], 'ssm_b_re': out['ssm_b_re'], 'ssm_b_im': out['ssm_b_im'], 'ssm_c_re': out['ssm_c_re'], 'ssm_c_im': out['ssm_c_im'], 'ssm_d': out['ssm_d'], 'w_ssm_glu': out['w_ssm_glu'], 'b_ssm_glu': out['b_ssm_glu'], 'w_ssm_out': out['w_ssm_out'], 'w_out': out['w_out'], 'post_norm_gain': out['post_norm_gain'], 'loss_target': out['loss_target'], 'm_pre_norm_gain': out['m_pre_norm_gain'], 'm_w_in': out['m_w_in'], 'm_conv_w': out['m_conv_w'], 'm_conv_b': out['m_conv_b'], 'm_conv_ln_gain': out['m_conv_ln_gain'], 'm_conv_ln_bias': out['m_conv_ln_bias'], 'm_w_conv_out': out['m_w_conv_out'], 'm_ssm_lambda_re': out['m_ssm_lambda_re'], 'm_ssm_lambda_im': out['m_ssm_lambda_im'], 'm_ssm_log_dt': out['m_ssm_log_dt'], 'm_ssm_b_re': out['m_ssm_b_re'], 'm_ssm_b_im': out['m_ssm_b_im'], 'm_ssm_c_re': out['m_ssm_c_re'], 'm_ssm_c_im': out['m_ssm_c_im'], 'm_ssm_d': out['m_ssm_d'], 'm_w_ssm_glu': out['m_w_ssm_glu'], 'm_b_ssm_glu': out['m_b_ssm_glu'], 'm_w_ssm_out': out['m_w_ssm_out'], 'm_w_out': out['m_w_out'], 'm_post_norm_gain': out['m_post_norm_gain'], 'v_pre_norm_gain': out['v_pre_norm_gain'], 'v_w_in': out['v_w_in'], 'v_conv_w': out['v_conv_w'], 'v_conv_b': out['v_conv_b'], 'v_conv_ln_gain': out['v_conv_ln_gain'], 'v_conv_ln_bias': out['v_conv_ln_bias'], 'v_w_conv_out': out['v_w_conv_out'], 'v_ssm_lambda_re': out['v_ssm_lambda_re'], 'v_ssm_lambda_im': out['v_ssm_lambda_im'], 'v_ssm_log_dt': out['v_ssm_log_dt'], 'v_ssm_b_re': out['v_ssm_b_re'], 'v_ssm_b_im': out['v_ssm_b_im'], 'v_ssm_c_re': out['v_ssm_c_re'], 'v_ssm_c_im': out['v_ssm_c_im'], 'v_ssm_d': out['v_ssm_d'], 'v_w_ssm_glu': out['v_w_ssm_glu'], 'v_b_ssm_glu': out['v_b_ssm_glu'], 'v_w_ssm_out': out['v_w_ssm_out'], 'v_w_out': out['v_w_out'], 'v_post_norm_gain': out['v_post_norm_gain']}


def _loss(weights, diff, rest, loss_target):
    with _jax.named_scope("forward"):
        args = {**rest, TWIN_DIFF_INPUT: diff, **{k: w.astype(_WEIGHT_DTYPES[k]) for k, w in weights.items()}}
        y = _forward(args)
    with _jax.named_scope("loss_head"):
        err = _jnp.square(y.astype(_jnp.float32) - loss_target)
        return 0.5 * _jnp.sum(_jnp.mean(err, axis=-1)) if err.ndim else 0.5 * err


def _adamw(w, g, m, v):
    m = ADAM_B1 * m + (1.0 - ADAM_B1) * g
    v = ADAM_B2 * v + (1.0 - ADAM_B2) * _jnp.square(g)
    m_hat = m / (1.0 - ADAM_B1 ** ADAM_STEP)
    v_hat = v / (1.0 - ADAM_B2 ** ADAM_STEP)
    delta = -ADAM_LR * (m_hat / (_jnp.sqrt(v_hat) + ADAM_EPS) + ADAM_WD * w)
    return delta, m, v


def reference(x, pre_norm_gain, w_in, conv_w, conv_b, conv_ln_gain, conv_ln_bias, w_conv_out, ssm_lambda_re, ssm_lambda_im, ssm_log_dt, ssm_b_re, ssm_b_im, ssm_c_re, ssm_c_im, ssm_d, w_ssm_glu, b_ssm_glu, w_ssm_out, w_out, post_norm_gain, loss_target, m_pre_norm_gain, m_w_in, m_conv_w, m_conv_b, m_conv_ln_gain, m_conv_ln_bias, m_w_conv_out, m_ssm_lambda_re, m_ssm_lambda_im, m_ssm_log_dt, m_ssm_b_re, m_ssm_b_im, m_ssm_c_re, m_ssm_c_im, m_ssm_d, m_w_ssm_glu, m_b_ssm_glu, m_w_ssm_out, m_w_out, m_post_norm_gain, v_pre_norm_gain, v_w_in, v_conv_w, v_conv_b, v_conv_ln_gain, v_conv_ln_bias, v_w_conv_out, v_ssm_lambda_re, v_ssm_lambda_im, v_ssm_log_dt, v_ssm_b_re, v_ssm_b_im, v_ssm_c_re, v_ssm_c_im, v_ssm_d, v_w_ssm_glu, v_b_ssm_glu, v_w_ssm_out, v_w_out, v_post_norm_gain):
    given = dict(x=x, pre_norm_gain=pre_norm_gain, w_in=w_in, conv_w=conv_w, conv_b=conv_b, conv_ln_gain=conv_ln_gain, conv_ln_bias=conv_ln_bias, w_conv_out=w_conv_out, ssm_lambda_re=ssm_lambda_re, ssm_lambda_im=ssm_lambda_im, ssm_log_dt=ssm_log_dt, ssm_b_re=ssm_b_re, ssm_b_im=ssm_b_im, ssm_c_re=ssm_c_re, ssm_c_im=ssm_c_im, ssm_d=ssm_d, w_ssm_glu=w_ssm_glu, b_ssm_glu=b_ssm_glu, w_ssm_out=w_ssm_out, w_out=w_out, post_norm_gain=post_norm_gain, loss_target=loss_target, m_pre_norm_gain=m_pre_norm_gain, m_w_in=m_w_in, m_conv_w=m_conv_w, m_conv_b=m_conv_b, m_conv_ln_gain=m_conv_ln_gain, m_conv_ln_bias=m_conv_ln_bias, m_w_conv_out=m_w_conv_out, m_ssm_lambda_re=m_ssm_lambda_re, m_ssm_lambda_im=m_ssm_lambda_im, m_ssm_log_dt=m_ssm_log_dt, m_ssm_b_re=m_ssm_b_re, m_ssm_b_im=m_ssm_b_im, m_ssm_c_re=m_ssm_c_re, m_ssm_c_im=m_ssm_c_im, m_ssm_d=m_ssm_d, m_w_ssm_glu=m_w_ssm_glu, m_b_ssm_glu=m_b_ssm_glu, m_w_ssm_out=m_w_ssm_out, m_w_out=m_w_out, m_post_norm_gain=m_post_norm_gain, v_pre_norm_gain=v_pre_norm_gain, v_w_in=v_w_in, v_conv_w=v_conv_w, v_conv_b=v_conv_b, v_conv_ln_gain=v_conv_ln_gain, v_conv_ln_bias=v_conv_ln_bias, v_w_conv_out=v_w_conv_out, v_ssm_lambda_re=v_ssm_lambda_re, v_ssm_lambda_im=v_ssm_lambda_im, v_ssm_log_dt=v_ssm_log_dt, v_ssm_b_re=v_ssm_b_re, v_ssm_b_im=v_ssm_b_im, v_ssm_c_re=v_ssm_c_re, v_ssm_c_im=v_ssm_c_im, v_ssm_d=v_ssm_d, v_w_ssm_glu=v_w_ssm_glu, v_b_ssm_glu=v_b_ssm_glu, v_w_ssm_out=v_w_ssm_out, v_w_out=v_w_out, v_post_norm_gain=v_post_norm_gain)
    weights = {n: given[n] for n in TWIN_WEIGHTS}
    shared = {n: given[n] for n in SHARED_INPUTS}
    per_example = {n: given[n] for n in ['x']}
    grad_fn = _jax.value_and_grad(_loss, argnums=(0, 1))

    def one_microbatch(ex, loss_target):
        ex = dict(ex)
        diff = ex.pop(TWIN_DIFF_INPUT)
        return grad_fn(weights, diff, {**shared, **ex}, loss_target)

    if N_MICROBATCH == 1:
        loss, (grad_w, grad_x) = one_microbatch(per_example, given["loss_target"])
    else:
        def body(carry, xs):
            loss_sum, grad_sum = carry
            l_k, (gw_k, gx_k) = one_microbatch(xs[0], xs[1])
            with _jax.named_scope("update"):
                return (loss_sum + l_k, _jax.tree.map(_jnp.add, grad_sum, gw_k)), gx_k

        init = (_jnp.zeros((), _jnp.float32), _jax.tree.map(_jnp.zeros_like, weights))
        (loss, grad_w), grad_x = _jax.lax.scan(body, init, (per_example, given["loss_target"]))
    with _jax.named_scope("update"):
        delta_w, new_m, new_v = {}, {}, {}
        for n in TWIN_WEIGHTS:
            delta_w[n], new_m[n], new_v[n] = _adamw(weights[n], grad_w[n], given["m_" + n], given["v_" + n])
    return (loss, grad_x, *[grad_w[n] for n in TWIN_WEIGHTS], *[delta_w[n] for n in TWIN_WEIGHTS],
            *[new_m[n] for n in TWIN_WEIGHTS], *[new_v[n] for n in TWIN_WEIGHTS])
```

```python
import functools
import math

import numpy as np
import jax
import jax.numpy as jnp
from jax import lax
from jax.experimental import pallas as pl
from jax.experimental.pallas import tpu as pltpu

f32 = jnp.float32
bf16 = jnp.bfloat16

D = 1024
CW = 1024
SW = 512
G = 32
H = 16
PST = 64
NS = G * PST
KS = 31
IN_W = 6144
NCHIP = 4
SHARD_W = IN_W // NCHIP
RMS_EPS = 1e-6
LN_EPS = 1e-5
LR, B1, B2, EPS, WD, STEP = 0.001, 0.9, 0.999, 1e-08, 0.01, 10
GELU_K0 = math.sqrt(2.0 / math.pi)
GELU_K1 = 0.044715

TC = 512
R = TC // 8
LBW = 512
PACK_ROWS = 2304
HALF_ROWS = PACK_ROWS // 2
SMALL_ROWS = 1128
VMEM_LIMIT = 56 * 1024 * 1024
MESH = pl.DeviceIdType.MESH


def _cp(*sem):
    return pltpu.CompilerParams(dimension_semantics=tuple(sem), vmem_limit_bytes=VMEM_LIMIT)


def _sig(v):
    return 1.0 / (1.0 + jnp.exp(-v))


def _dot(a, b):
    return jnp.dot(a, b, preferred_element_type=f32)


def _dot_nt(a, b):
    return lax.dot_general(a, b, (((1,), (1,)), ((), ())), preferred_element_type=f32)


def _dot_tn(a, b):
    return lax.dot_general(a, b, (((0,), (0,)), ((), ())), preferred_element_type=f32)


def _full(shape):
    nd = len(shape)
    return pl.BlockSpec(shape, lambda *_: (0,) * nd)


def _prenorm(x, g_pre):
    L = x.shape[0]
    tm = 512
    def body(x_ref, g_ref, h_ref):
        xt = x_ref[...]
        r = lax.rsqrt(jnp.mean(xt * xt, axis=-1, keepdims=True) + RMS_EPS)
        h_ref[...] = (xt * r * g_ref[...]).astype(bf16)
    return pl.pallas_call(
        body, grid=(L // tm,),
        in_specs=[pl.BlockSpec((tm, D), lambda i: (i, 0)), _full((1, D))],
        out_specs=pl.BlockSpec((tm, D), lambda i: (i, 0)),
        out_shape=jax.ShapeDtypeStruct((L, D), bf16),
        name="prenorm", compiler_params=_cp("arbitrary"))(x, g_pre)


def _proj_fwd(h, w_in):
    L = h.shape[0]
    tm = 512
    def body(h_ref, w_ref, o_ref):
        o_ref[...] = _dot(h_ref[...], w_ref[...]).astype(bf16)
    return pl.pallas_call(
        body, grid=(NCHIP, L // tm),
        in_specs=[pl.BlockSpec((tm, D), lambda j, i: (i, 0)), pl.BlockSpec((D, SHARD_W), lambda j, i: (0, j))],
        out_specs=pl.BlockSpec((tm, SHARD_W), lambda j, i: (i, j)),
        out_shape=jax.ShapeDtypeStruct((L, IN_W), bf16),
        name="proj_fwd", compiler_params=_cp("arbitrary", "arbitrary"))(h, w_in)


def _conv_fwd(proj, cw, cbias, lng, lnb):
    L = proj.shape[0]
    tm = 256
    def body(ca_ref, cb_ref, zc_ref, w_ref, b_ref, g_ref, bb_ref, cu1_ref, ain_ref, buf, cacc):
        i = pl.program_id(0)
        @pl.when(i == 0)
        def _():
            buf[0:32, :] = jnp.zeros((32, CW), f32)
        @pl.when(i > 0)
        def _():
            buf[0:32, :] = buf[tm:tm + 32, :]
        ca = ca_ref[...].astype(f32)
        cb = cb_ref[...].astype(f32)
        buf[32:32 + tm, :] = ca * _sig(cb)

        def grp(j, carry):
            base = pl.multiple_of(j * 8, 8)
            for lb in range(CW // LBW):
                sl = slice(lb * LBW, (lb + 1) * LBW)
                win = buf[pl.ds(base, 40), sl]
                acc = jnp.broadcast_to(b_ref[:, sl], (8, LBW))
                for k in range(KS):
                    acc = acc + w_ref[k:k + 1, sl] * win[2 + k:2 + k + 8, :]
                cacc[pl.ds(base, 8), sl] = acc
            return carry
        lax.fori_loop(0, tm // 8, grp, 0)

        c1b = cacc[...].astype(bf16)
        cu1_ref[...] = c1b
        c1 = c1b.astype(f32)
        mu = jnp.mean(c1, axis=-1, keepdims=True)
        xc = c1 - mu
        var = jnp.mean(xc * xc, axis=-1, keepdims=True)
        ln = xc * lax.rsqrt(var + LN_EPS) * g_ref[...] + bb_ref[...]
        cu2 = ln * _sig(ln)
        zc = zc_ref[...].astype(f32)
        ain_ref[...] = (cu2 * (zc * _sig(zc))).astype(bf16)

    col = lambda c: pl.BlockSpec((tm, CW), lambda i, c=c: (i, c))
    return pl.pallas_call(
        body, grid=(L // tm,),
        in_specs=[col(0), col(1), col(2), _full((32, CW)), _full((1, CW)), _full((1, CW)), _full((1, CW))],
        out_specs=[pl.BlockSpec((tm, CW), lambda i: (i, 0)), pl.BlockSpec((tm, CW), lambda i: (i, 0))],
        out_shape=[jax.ShapeDtypeStruct((L, CW), bf16), jax.ShapeDtypeStruct((L, CW), bf16)],
        scratch_shapes=[pltpu.VMEM((tm + 32, CW), f32), pltpu.VMEM((tm, CW), f32)],
        name="conv_fwd", compiler_params=_cp("arbitrary"))(proj, proj, proj, cw, cbias, lng, lnb)


def _gelu_parts(y0):
    t = jnp.tanh(GELU_K0 * (y0 + GELU_K1 * y0 * y0 * y0))
    return t, 0.5 * y0 * (1.0 + t)


def _ssm_fwd(proj, perm, perm_t, bbt_re, bbt_im, ct_re, ct_im, a_re, a_im, apow_re, apow_im, dvec, wglu, bglu):
    L = proj.shape[0]
    nc = L // TC
    def body(u_ref, zs_ref, p_ref, pt_ref, bre_ref, bim_ref, cre_ref, cim_ref, are_ref, aim_ref, pwr_ref, pwi_ref,
             d_ref, wg_ref, bg_ref, y0_ref, bin_ref, sre, sim, cinr, cini, prev_re, prev_im):
        c = pl.program_id(0)
        @pl.when(c == 0)
        def _():
            prev_re[...] = jnp.zeros_like(prev_re)
            prev_im[...] = jnp.zeros_like(prev_im)
        u = u_ref[...]
        up = _dot(p_ref[...], u).astype(bf16)
        for blk in range(4):
            ub = up[:, 128 * blk:128 * (blk + 1)]
            sre[:, 512 * blk:512 * (blk + 1)] = _dot(ub, bre_ref[blk])
            sim[:, 512 * blk:512 * (blk + 1)] = _dot(ub, bim_ref[blk])
        for lb in range(NS // LBW):
            sl = slice(lb * LBW, (lb + 1) * LBW)
            ar = jnp.broadcast_to(are_ref[:, sl], (8, LBW))
            ai = jnp.broadcast_to(aim_ref[:, sl], (8, LBW))
            def step(r, carry, sl=sl, ar=ar, ai=ai):
                sr, si = carry
                row = pl.ds(pl.multiple_of(r * 8, 8), 8)
                nr = ar * sr - ai * si + sre[row, sl]
                ni = ar * si + ai * sr + sim[row, sl]
                sre[row, sl] = nr
                sim[row, sl] = ni
                return nr, ni
            lax.fori_loop(1, R, step, (sre[0:8, sl], sim[0:8, sl]))
        a_r = pwr_ref[R - 1:R, :]
        a_i = pwi_ref[R - 1:R, :]
        cr = prev_re[0:1, :]
        ci = prev_im[0:1, :]
        for seg in range(8):
            cinr[seg:seg + 1, :] = cr
            cini[seg:seg + 1, :] = ci
            er = sre[8 * (R - 1) + seg:8 * (R - 1) + seg + 1, :]
            ei = sim[8 * (R - 1) + seg:8 * (R - 1) + seg + 1, :]
            cr, ci = er + a_r * cr - a_i * ci, ei + a_r * ci + a_i * cr
        prev_re[0:1, :] = cr
        prev_im[0:1, :] = ci
        for lb in range(NS // LBW):
            sl = slice(lb * LBW, (lb + 1) * LBW)
            kr = cinr[:, sl]
            ki = cini[:, sl]
            def fix(r, carry, sl=sl, kr=kr, ki=ki):
                row = pl.ds(pl.multiple_of(r * 8, 8), 8)
                pr = jnp.broadcast_to(pwr_ref[pl.ds(r, 1), sl], (8, LBW))
                pi = jnp.broadcast_to(pwi_ref[pl.ds(r, 1), sl], (8, LBW))
                sre[row, sl] = sre[row, sl] + pr * kr - pi * ki
                sim[row, sl] = sim[row, sl] + pr * ki + pi * kr
                return carry
            lax.fori_loop(0, R, fix, 0)
        yp = []
        for blk in range(4):
            sr = sre[:, 512 * blk:512 * (blk + 1)].astype(bf16)
            si = sim[:, 512 * blk:512 * (blk + 1)].astype(bf16)
            yp.append(_dot(sr, cre_ref[blk]) - _dot(si, cim_ref[blk]))
        yperm = jnp.concatenate(yp, axis=1)
        hi = yperm.astype(bf16)
        lo = (yperm - hi.astype(f32)).astype(bf16)
        y0 = _dot(pt_ref[...], hi) + _dot(pt_ref[...], lo) + d_ref[...] * u.astype(f32)
        y0_ref[...] = y0
        _, y1 = _gelu_parts(y0)
        glu = _dot(y1.astype(bf16), wg_ref[...]) + bg_ref[...]
        y2 = y1 * _sig(glu)
        zs = zs_ref[...].astype(f32)
        bin_ref[...] = (y2 * (zs * _sig(zs))).astype(bf16)

    return pl.pallas_call(
        body, grid=(nc,),
        in_specs=[pl.BlockSpec((TC, SW), lambda c: (c, 6)), pl.BlockSpec((TC, SW), lambda c: (c, 7)),
                  _full((TC, TC)), _full((TC, TC)),
                  _full((4, 128, 512)), _full((4, 128, 512)), _full((4, 512, 128)), _full((4, 512, 128)),
                  _full((1, NS)), _full((1, NS)), _full((R, NS)), _full((R, NS)),
                  _full((1, SW)), _full((SW, SW)), _full((1, SW))],
        out_specs=[pl.BlockSpec((TC, SW), lambda c: (c, 0)), pl.BlockSpec((TC, SW), lambda c: (c, 0)),
                   pl.BlockSpec((TC, NS), lambda c: (c, 0)), pl.BlockSpec((TC, NS), lambda c: (c, 0)),
                   pl.BlockSpec((8, NS), lambda c: (c, 0)), pl.BlockSpec((8, NS), lambda c: (c, 0))],
        out_shape=[jax.ShapeDtypeStruct((L, SW), f32), jax.ShapeDtypeStruct((L, SW), bf16),
                   jax.ShapeDtypeStruct((L, NS), f32), jax.ShapeDtypeStruct((L, NS), f32),
                   jax.ShapeDtypeStruct((nc * 8, NS), f32), jax.ShapeDtypeStruct((nc * 8, NS), f32)],
        scratch_shapes=[pltpu.VMEM((8, NS), f32), pltpu.VMEM((8, NS), f32)],
        name="ssm_fwd", compiler_params=_cp("arbitrary"))(
            proj, proj, perm, perm_t, bbt_re, bbt_im, ct_re, ct_im, a_re, a_im, apow_re, apow_im, dvec, wglu, bglu)


def _tail(a_in, b_in, proj, x, tgt, wco, wso, wout, gpost):
    L = x.shape[0]
    tm = 256
    def body(a_ref, b_ref, gc_ref, gs_ref, x_ref, t_ref, wco_ref, wso_ref, wout_ref, gp_ref,
             gx_ref, dain_ref, dbin_ref, dgc_ref, dgs_ref, dwout_ref, dwco_ref, dwso_ref, dgp_ref, loss_ref):
        @pl.when(pl.program_id(0) == 0)
        def _():
            dwout_ref[...] = jnp.zeros_like(dwout_ref)
            dwco_ref[...] = jnp.zeros_like(dwco_ref)
            dwso_ref[...] = jnp.zeros_like(dwso_ref)
            dgp_ref[...] = jnp.zeros_like(dgp_ref)
            loss_ref[...] = jnp.zeros_like(loss_ref)
        a = a_ref[...]
        b = b_ref[...]
        co = _dot(a, wco_ref[...])
        so = _dot(b, wso_ref[...])
        sc = _sig(gc_ref[...].astype(f32))
        ss = _sig(gs_ref[...].astype(f32))
        mb = (sc * co + ss * so).astype(bf16)
        out = _dot(mb, wout_ref[...])
        r2 = lax.rsqrt(jnp.mean(out * out, axis=-1, keepdims=True) + RMS_EPS)
        on = out * r2
        gp = gp_ref[...]
        e = x_ref[...] + on * gp - t_ref[...]
        loss_ref[...] += (0.5 / D) * jnp.sum(e * e)
        dy = e * (1.0 / D)
        gx_ref[...] = dy
        dgp_ref[...] += jnp.sum(dy * on, axis=0, keepdims=True)
        dn = dy * gp
        dout = (r2 * (dn - on * jnp.mean(dn * on, axis=-1, keepdims=True))).astype(bf16)
        dwout_ref[...] += _dot_tn(mb, dout)
        dm = _dot_nt(dout, wout_ref[...])
        dgc_ref[...] = (dm * co * sc * (1.0 - sc)).astype(bf16)
        dgs_ref[...] = (dm * so * ss * (1.0 - ss)).astype(bf16)
        dco = (dm * sc).astype(bf16)
        dso = (dm * ss).astype(bf16)
        dwco_ref[...] += _dot_tn(a, dco)
        dwso_ref[...] += _dot_tn(b, dso)
        dain_ref[...] = _dot_nt(dco, wco_ref[...]).astype(bf16)
        dbin_ref[...] = _dot_nt(dso, wso_ref[...]).astype(bf16)

    row = lambda w: pl.BlockSpec((tm, w), lambda i: (i, 0))
    one = lambda shape: pl.BlockSpec(shape, lambda i: (0,) * len(shape), pipeline_mode=pl.Buffered(1))
    return pl.pallas_call(
        body, grid=(L // tm,),
        in_specs=[row(CW), row(SW), pl.BlockSpec((tm, D), lambda i: (i, 4)), pl.BlockSpec((tm, D), lambda i: (i, 5)),
                  row(D), row(D), one((CW, D)), one((SW, D)), one((D, D)), one((1, D))],
        out_specs=[row(D), row(CW), row(SW), row(D), row(D),
                   one((D, D)), one((CW, D)), one((SW, D)), one((1, D)), one((1, 128))],
        out_shape=[jax.ShapeDtypeStruct((L, D), f32), jax.ShapeDtypeStruct((L, CW), bf16),
                   jax.ShapeDtypeStruct((L, SW), bf16), jax.ShapeDtypeStruct((L, D), bf16),
                   jax.ShapeDtypeStruct((L, D), bf16),
                   jax.ShapeDtypeStruct((D, D), f32), jax.ShapeDtypeStruct((CW, D), f32),
                   jax.ShapeDtypeStruct((SW, D), f32), jax.ShapeDtypeStruct((1, D), f32),
                   jax.ShapeDtypeStruct((1, 128), f32)],
        name="tail", compiler_params=_cp("arbitrary"))(a_in, b_in, proj, proj, x, tgt, wco, wso, wout, gpost)


def _ssm_bwd(d_bin, y0, proj, sre, sim, cinr, cini, perm, perm_t, bbt_re, bbt_im, ct_re, ct_im,
             a_re, a_im, apow_re, apow_im, dvec, wglu, bglu):
    L = y0.shape[0]
    nc = L // TC
    def body(dbin_ref, y0_ref, u_ref, zs_ref, sre_ref, sim_ref, cinr_ref, cini_ref, p_ref, pt_ref,
             bre_ref, bim_ref, cre_ref, cim_ref, are_ref, aim_ref, pwr_ref, pwi_ref, d_ref, wg_ref, bg_ref,
             du_ref, dzs_ref, dbre_ref, dbim_ref, dcre_ref, dcim_ref, dd_ref, dar_ref, dai_ref, dwg_ref, dbg_ref,
             gre, gim, gcr, gci, nxt_re, nxt_im):
        @pl.when(pl.program_id(0) == 0)
        def _():
            for ref in (dbre_ref, dbim_ref, dcre_ref, dcim_ref, dd_ref, dar_ref, dai_ref, dwg_ref, dbg_ref,
                        nxt_re, nxt_im):
                ref[...] = jnp.zeros_like(ref)
        y0 = y0_ref[...]
        u = u_ref[...]
        zs = zs_ref[...].astype(f32)
        dbin = dbin_ref[...].astype(f32)
        t, y1 = _gelu_parts(y0)
        y1b = y1.astype(bf16)
        sg = _sig(_dot(y1b, wg_ref[...]) + bg_ref[...])
        sz = _sig(zs)
        d_y2 = dbin * (zs * sz)
        dzs_ref[...] = (dbin * (y1 * sg) * (sz * (1.0 + zs * (1.0 - sz)))).astype(bf16)
        d_glu = d_y2 * y1 * sg * (1.0 - sg)
        d_glub = d_glu.astype(bf16)
        d_y1 = d_y2 * sg + _dot_nt(d_glub, wg_ref[...])
        dwg_ref[...] += _dot_tn(y1b, d_glub)
        dbg_ref[...] += jnp.sum(d_glu, axis=0, keepdims=True)
        dgelu = 0.5 * (1.0 + t) + 0.5 * y0 * (1.0 - t * t) * GELU_K0 * (1.0 + 3.0 * GELU_K1 * y0 * y0)
        d_y0 = d_y1 * dgelu
        dd_ref[...] += jnp.sum(d_y0 * u.astype(f32), axis=0, keepdims=True)
        dyp = _dot(p_ref[...], d_y0.astype(bf16)).astype(bf16)
        up = _dot(p_ref[...], u).astype(bf16)
        for blk in range(4):
            dyb = dyp[:, 128 * blk:128 * (blk + 1)]
            gre[:, 512 * blk:512 * (blk + 1)] = _dot_nt(dyb, cre_ref[blk])
            gim[:, 512 * blk:512 * (blk + 1)] = -_dot_nt(dyb, cim_ref[blk])
        for lb in range(NS // LBW):
            sl = slice(lb * LBW, (lb + 1) * LBW)
            ar = jnp.broadcast_to(are_ref[:, sl], (8, LBW))
            ai = jnp.broadcast_to(aim_ref[:, sl], (8, LBW))
            def step(k, carry, sl=sl, ar=ar, ai=ai):
                gr, gi = carry
                row = pl.ds(pl.multiple_of((R - 2 - k) * 8, 8), 8)
                nr = ar * gr + ai * gi + gre[row, sl]
                ni = ar * gi - ai * gr + gim[row, sl]
                gre[row, sl] = nr
                gim[row, sl] = ni
                return nr, ni
            lax.fori_loop(0, R - 1, step, (gre[8 * (R - 1):8 * R, sl], gim[8 * (R - 1):8 * R, sl]))
        a_r = pwr_ref[R - 1:R, :]
        a_i = pwi_ref[R - 1:R, :]
        cr = nxt_re[0:1, :]
        ci = nxt_im[0:1, :]
        for seg in range(7, -1, -1):
            gcr[seg:seg + 1, :] = cr
            gci[seg:seg + 1, :] = ci
            er = gre[seg:seg + 1, :]
            ei = gim[seg:seg + 1, :]
            cr, ci = er + a_r * cr + a_i * ci, ei + a_r * ci - a_i * cr
        nxt_re[0:1, :] = cr
        nxt_im[0:1, :] = ci
        for lb in range(NS // LBW):
            sl = slice(lb * LBW, (lb + 1) * LBW)
            kr = gcr[:, sl]
            ki = gci[:, sl]
            def fix(r, carry, sl=sl, kr=kr, ki=ki):
                row = pl.ds(pl.multiple_of(r * 8, 8), 8)
                pr = jnp.broadcast_to(pwr_ref[pl.ds(R - 1 - r, 1), sl], (8, LBW))
                pi = jnp.broadcast_to(pwi_ref[pl.ds(R - 1 - r, 1), sl], (8, LBW))
                gre[row, sl] = gre[row, sl] + pr * kr + pi * ki
                gim[row, sl] = gim[row, sl] + pr * ki - pi * kr
                return carry
            lax.fori_loop(0, R, fix, 0)
        dup = []
        for blk in range(4):
            s4 = slice(512 * blk, 512 * (blk + 1))
            s1 = slice(128 * blk, 128 * (blk + 1))
            grb = gre[:, s4].astype(bf16)
            gib = gim[:, s4].astype(bf16)
            dup.append(_dot_nt(grb, bre_ref[blk]) + _dot_nt(gib, bim_ref[blk]))
            dbre_ref[blk] += _dot_tn(up[:, s1], grb)
            dbim_ref[blk] += _dot_tn(up[:, s1], gib)
            dcre_ref[blk] += _dot_tn(sre_ref[:, s4].astype(bf16), dyp[:, s1])
            dcim_ref[blk] -= _dot_tn(sim_ref[:, s4].astype(bf16), dyp[:, s1])
        duperm = jnp.concatenate(dup, axis=1)
        hi = duperm.astype(bf16)
        lo = (duperm - hi.astype(f32)).astype(bf16)
        du = _dot(pt_ref[...], hi) + _dot(pt_ref[...], lo) + d_ref[...] * d_y0
        du_ref[...] = du.astype(bf16)
        for lb in range(NS // LBW):
            sl = slice(lb * LBW, (lb + 1) * LBW)
            g0r, g0i = gre[0:8, sl], gim[0:8, sl]
            p0r, p0i = cinr_ref[:, sl], cini_ref[:, sl]
            acc0 = (g0r * p0r + g0i * p0i, g0i * p0r - g0r * p0i)
            def dacc(r, carry, sl=sl):
                xr, xi = carry
                row = pl.ds(pl.multiple_of(r * 8, 8), 8)
                prow = pl.ds(pl.multiple_of((r - 1) * 8, 8), 8)
                gr, gi = gre[row, sl], gim[row, sl]
                pr, pi = sre_ref[prow, sl], sim_ref[prow, sl]
                return xr + gr * pr + gi * pi, xi + gi * pr - gr * pi
            xr, xi = lax.fori_loop(1, R, dacc, acc0)
            dar_ref[:, sl] += xr
            dai_ref[:, sl] += xi

    rev = lambda w, cidx: pl.BlockSpec((TC, w), lambda i, cidx=cidx: (nc - 1 - i, cidx))
    one = lambda shape: pl.BlockSpec(shape, lambda i: (0,) * len(shape))
    return pl.pallas_call(
        body, grid=(nc,),
        in_specs=[rev(SW, 0), rev(SW, 0), rev(SW, 6), rev(SW, 7), rev(NS, 0), rev(NS, 0),
                  pl.BlockSpec((8, NS), lambda i: (nc - 1 - i, 0)), pl.BlockSpec((8, NS), lambda i: (nc - 1 - i, 0)),
                  one((TC, TC)), one((TC, TC)),
                  one((4, 128, 512)), one((4, 128, 512)), one((4, 512, 128)), one((4, 512, 128)),
                  one((1, NS)), one((1, NS)), one((R, NS)), one((R, NS)),
                  one((1, SW)), one((SW, SW)), one((1, SW))],
        out_specs=[rev(SW, 0), rev(SW, 0),
                   one((4, 128, 512)), one((4, 128, 512)), one((4, 512, 128)), one((4, 512, 128)),
                   one((1, SW)), one((8, NS)), one((8, NS)), one((SW, SW)), one((1, SW))],
        out_shape=[jax.ShapeDtypeStruct((L, SW), bf16), jax.ShapeDtypeStruct((L, SW), bf16),
                   jax.ShapeDtypeStruct((4, 128, 512), f32), jax.ShapeDtypeStruct((4, 128, 512), f32),
                   jax.ShapeDtypeStruct((4, 512, 128), f32), jax.ShapeDtypeStruct((4, 512, 128), f32),
                   jax.ShapeDtypeStruct((1, SW), f32), jax.ShapeDtypeStruct((8, NS), f32),
                   jax.ShapeDtypeStruct((8, NS), f32), jax.ShapeDtypeStruct((SW, SW), f32),
                   jax.ShapeDtypeStruct((1, SW), f32)],
        scratch_shapes=[pltpu.VMEM((TC, NS), f32), pltpu.VMEM((TC, NS), f32), pltpu.VMEM((8, NS), f32),
                        pltpu.VMEM((8, NS), f32), pltpu.VMEM((8, NS), f32), pltpu.VMEM((8, NS), f32)],
        name="ssm_bwd", compiler_params=_cp("arbitrary"))(
            d_bin, y0, proj, proj, sre, sim, cinr, cini, perm, perm_t, bbt_re, bbt_im, ct_re, ct_im,
            a_re, a_im, apow_re, apow_im, dvec, wglu, bglu)


def _conv_bwd(d_ain, cu1, proj, cw, lng, lnb):
    L = cu1.shape[0]
    tm = 256
    nt = L // tm
    hb = tm // 32
    def body(dain_ref, cu1_ref, ca_ref, cb_ref, zc_ref, cah_ref, cbh_ref, w_ref, g_ref, bb_ref,
             dca_ref, dcb_ref, dzc_ref, dw_ref, dbias_ref, dlng_ref, dlnb_ref, dbuf, cbuf, dcu0):
        i = pl.program_id(0)
        @pl.when(i == 0)
        def _():
            dw_ref[...] = jnp.zeros_like(dw_ref)
            dbias_ref[...] = jnp.zeros_like(dbias_ref)
            dlng_ref[...] = jnp.zeros_like(dlng_ref)
            dlnb_ref[...] = jnp.zeros_like(dlnb_ref)
            dbuf[tm:tm + 32, :] = jnp.zeros((32, CW), f32)
        @pl.when(i > 0)
        def _():
            dbuf[tm:tm + 32, :] = dbuf[0:32, :]
        dain = dain_ref[...].astype(f32)
        c1 = cu1_ref[...].astype(f32)
        zc = zc_ref[...].astype(f32)
        mu = jnp.mean(c1, axis=-1, keepdims=True)
        xc = c1 - mu
        var = jnp.mean(xc * xc, axis=-1, keepdims=True)
        rstd = lax.rsqrt(var + LN_EPS)
        xh = xc * rstd
        ln = xh * g_ref[...] + bb_ref[...]
        sl_ = _sig(ln)
        sz = _sig(zc)
        dzc_ref[...] = (dain * (ln * sl_) * (sz * (1.0 + zc * (1.0 - sz)))).astype(bf16)
        d_ln = dain * (zc * sz) * (sl_ * (1.0 + ln * (1.0 - sl_)))
        dlng_ref[...] += jnp.sum(d_ln * xh, axis=0, keepdims=True)
        dlnb_ref[...] += jnp.sum(d_ln, axis=0, keepdims=True)
        dxh = d_ln * g_ref[...]
        d_c1 = rstd * (dxh - jnp.mean(dxh, axis=-1, keepdims=True)
                       - xh * jnp.mean(dxh * xh, axis=-1, keepdims=True))
        dbias_ref[...] += jnp.sum(d_c1, axis=0, keepdims=True)
        dbuf[0:tm, :] = d_c1
        ca = ca_ref[...].astype(f32)
        sb = _sig(cb_ref[...].astype(f32))
        cbuf[32:32 + tm, :] = ca * sb
        halo = cah_ref[...].astype(f32) * _sig(cbh_ref[...].astype(f32))
        cbuf[0:32, :] = jnp.where(i == nt - 1, jnp.zeros_like(halo), halo)

        def grp(j, carry):
            base = pl.multiple_of(j * 8, 8)
            for lb in range(CW // LBW):
                sl = slice(lb * LBW, (lb + 1) * LBW)
                dwin = dbuf[pl.ds(base, 40), sl]
                cwin = cbuf[pl.ds(base, 40), sl]
                dv = dwin[0:8, :]
                acc = jnp.zeros((8, LBW), f32)
                for k in range(KS):
                    acc = acc + w_ref[k:k + 1, sl] * dwin[30 - k:38 - k, :]
                    dw_ref[k, :, sl] += dv * cwin[2 + k:10 + k, :]
                dcu0[pl.ds(base, 8), sl] = acc
            return carry
        lax.fori_loop(0, tm // 8, grp, 0)
        d0 = dcu0[...]
        dca_ref[...] = (d0 * sb).astype(bf16)
        dcb_ref[...] = (d0 * ca * sb * (1.0 - sb)).astype(bf16)

    rev = lambda cidx: pl.BlockSpec((tm, CW), lambda i, cidx=cidx: (nt - 1 - i, cidx))
    halo = lambda cidx: pl.BlockSpec((32, CW), lambda i, cidx=cidx: (jnp.maximum((nt - 1 - i) * hb - 1, 0), cidx))
    one = lambda shape: pl.BlockSpec(shape, lambda i: (0,) * len(shape))
    return pl.pallas_call(
        body, grid=(nt,),
        in_specs=[rev(0), rev(0), rev(0), rev(1), rev(2), halo(0), halo(1), one((32, CW)), one((1, CW)), one((1, CW))],
        out_specs=[rev(0), rev(0), rev(0), one((32, 8, CW)), one((1, CW)), one((1, CW)), one((1, CW))],
        out_shape=[jax.ShapeDtypeStruct((L, CW), bf16), jax.ShapeDtypeStruct((L, CW), bf16),
                   jax.ShapeDtypeStruct((L, CW), bf16), jax.ShapeDtypeStruct((32, 8, CW), f32),
                   jax.ShapeDtypeStruct((1, CW), f32), jax.ShapeDtypeStruct((1, CW), f32),
                   jax.ShapeDtypeStruct((1, CW), f32)],
        scratch_shapes=[pltpu.VMEM((tm + 32, CW), f32), pltpu.VMEM((tm + 32, CW), f32), pltpu.VMEM((tm, CW), f32)],
        name="conv_bwd", compiler_params=_cp("arbitrary"))(d_ain, cu1, proj, proj, proj, proj, proj, cw, lng, lnb)


def _win_grad(h, dproj):
    L = h.shape[0]
    tm = 512
    def body(h_ref, d_ref, o_ref):
        @pl.when(pl.program_id(1) == 0)
        def _():
            o_ref[...] = jnp.zeros_like(o_ref)
        o_ref[...] += _dot_tn(h_ref[...], d_ref[...])
    return pl.pallas_call(
        body, grid=(NCHIP, L // tm),
        in_specs=[pl.BlockSpec((tm, D), lambda j, i: (i, 0)), pl.BlockSpec((tm, SHARD_W), lambda j, i: (i, j))],
        out_specs=pl.BlockSpec((D, SHARD_W), lambda j, i: (0, j)),
        out_shape=jax.ShapeDtypeStruct((D, IN_W), f32),
        name="win_grad", compiler_params=_cp("arbitrary", "arbitrary"))(h, dproj)


def _x_grad(dproj, w_in, x, gx0, g_pre):
    L = x.shape[0]
    tm = 256
    def body(d_ref, w_ref, x_ref, gx_ref, g_ref, o_ref, dg_ref):
        @pl.when(pl.program_id(0) == 0)
        def _():
            dg_ref[...] = jnp.zeros_like(dg_ref)
        dh = _dot_nt(d_ref[...], w_ref[...])
        xt = x_ref[...]
        r = lax.rsqrt(jnp.mean(xt * xt, axis=-1, keepdims=True) + RMS_EPS)
        xn = xt * r
        dg_ref[...] += jnp.sum(dh * xn, axis=0, keepdims=True)
        dxn = dh * g_ref[...]
        o_ref[...] = gx_ref[...] + r * (dxn - xn * jnp.mean(dxn * xn, axis=-1, keepdims=True))
    return pl.pallas_call(
        body, grid=(L // tm,),
        in_specs=[pl.BlockSpec((tm, IN_W), lambda i: (i, 0)),
                  pl.BlockSpec((D, IN_W), lambda i: (0, 0), pipeline_mode=pl.Buffered(1)),
                  pl.BlockSpec((tm, D), lambda i: (i, 0)), pl.BlockSpec((tm, D), lambda i: (i, 0)), _full((1, D))],
        out_specs=[pl.BlockSpec((tm, D), lambda i: (i, 0)), _full((1, D))],
        out_shape=[jax.ShapeDtypeStruct((L, D), f32), jax.ShapeDtypeStruct((1, D), f32)],
        name="x_grad", compiler_params=_cp("arbitrary"))(dproj, w_in, x, gx0, g_pre)


def _sum8(parts):
    rows = parts.shape[1]
    tm = 128
    def body(p_ref, o_ref):
        acc = p_ref[0]
        for d in range(1, 8):
            acc = acc + p_ref[d]
        o_ref[...] = acc
    return pl.pallas_call(
        body, grid=(rows // tm,),
        in_specs=[pl.BlockSpec((8, tm, 1024), lambda i: (0, i, 0))],
        out_specs=pl.BlockSpec((tm, 1024), lambda i: (i, 0)),
        out_shape=jax.ShapeDtypeStruct((rows, 1024), f32), name="sum_pieces",
        compiler_params=_cp("arbitrary"))(parts)


def _adamw_math(w, g, m, v):
    m2 = B1 * m + (1.0 - B1) * g
    v2 = B2 * v + (1.0 - B2) * (g * g)
    m_hat = m2 / (1.0 - B1 ** STEP)
    v_hat = v2 / (1.0 - B2 ** STEP)
    delta = -LR * (m_hat / (jnp.sqrt(v_hat) + EPS) + WD * w)
    return delta, m2, v2


def _adamw(name, w, g, m, v):
    rows, cols = w.shape
    tm = rows if rows <= 256 else 256
    assert rows % tm == 0
    def body(w_ref, g_ref, m_ref, v_ref, d_ref, m2_ref, v2_ref):
        d, m2, v2 = _adamw_math(w_ref[...], g_ref[...], m_ref[...], v_ref[...])
        d_ref[...] = d
        m2_ref[...] = m2
        v2_ref[...] = v2
    spec = pl.BlockSpec((tm, cols), lambda i: (i, 0))
    shp = jax.ShapeDtypeStruct((rows, cols), f32)
    return pl.pallas_call(
        body, grid=(rows // tm,), in_specs=[spec] * 4, out_specs=[spec] * 3, out_shape=[shp] * 3,
        name=name, compiler_params=_cp("arbitrary"))(w, g, m, v)


def _adamw_small(parts, w, m, v):
    rows = w.shape[0]
    def body(p_ref, w_ref, m_ref, v_ref, g_ref, d_ref, m2_ref, v2_ref):
        g = p_ref[0]
        for dvc in range(1, 8):
            g = g + p_ref[dvc]
        g_ref[...] = g
        d, m2, v2 = _adamw_math(w_ref[...], g, m_ref[...], v_ref[...])
        d_ref[...] = d
        m2_ref[...] = m2
        v2_ref[...] = v2
    shp = jax.ShapeDtypeStruct((rows, 128), f32)
    return pl.pallas_call(body, out_shape=[shp] * 4, name="adamw_small",
                          compiler_params=pltpu.CompilerParams(vmem_limit_bytes=VMEM_LIMIT))(parts, w, m, v)


_ANY = pl.BlockSpec(memory_space=pl.ANY)


def _gather_weights(wp):
    def body(w_ref, o_ref, send_sems, recv_sems, local_sem):
        x, y, c = lax.axis_index("x"), lax.axis_index("y"), lax.axis_index("c")
        sibling = (x, y, 1 - c)
        chips = [(1 - x, y), (x, 1 - y), (1 - x, 1 - y)]

        def half(px, py, pc):
            return o_ref.at[2 * px + py, pl.ds(pc * HALF_ROWS, HALF_ROWS), :]

        def copy(k, block, to, src=None):
            return pltpu.make_async_remote_copy(
                src_ref=half(*block) if src is None else src, dst_ref=half(*block),
                send_sem=send_sems.at[k], recv_sem=recv_sems.at[k], device_id=to, device_id_type=MESH)

        mine = pltpu.make_async_copy(w_ref, o_ref.at[2 * x + y], local_sem)
        mine.start()
        my_half = w_ref.at[pl.ds(c * HALF_ROWS, HALF_ROWS), :]
        first = [copy(j, (x, y, c), (*chip, c), src=my_half) for j, chip in enumerate(chips)]
        for cp in first:
            cp.start()
        passed = [copy(3 + j, (*chip, c), sibling) for j, chip in enumerate(chips)]
        for j, chip in enumerate(chips):
            copy(j, (*chip, c), (x, y, c)).wait_recv()
            passed[j].start()
        for j, chip in enumerate(chips):
            copy(3 + j, (*chip, 1 - c), (x, y, c)).wait_recv()
        for cp in first + passed:
            cp.wait_send()
        mine.wait()

    return pl.pallas_call(
        body, in_specs=[_ANY], out_specs=_ANY,
        out_shape=jax.ShapeDtypeStruct((NCHIP, PACK_ROWS, 1024), bf16),
        scratch_shapes=[pltpu.SemaphoreType.DMA((6,)), pltpu.SemaphoreType.DMA((6,)), pltpu.SemaphoreType.DMA],
        name="gather_weights")(wp)


def _scatter_grads(gpack, gsmall):
    def body(g_ref, s_ref, og_ref, os_ref, send_sems, recv_sems, local_sems):
        x, y, c = lax.axis_index("x"), lax.axis_index("y"), lax.axis_index("c")
        me = 4 * x + 2 * y + c

        def piece(px, py, pc):
            return g_ref.at[2 * px + py, pl.ds(pc * HALF_ROWS, HALF_ROWS), :]

        own_g = pltpu.make_async_copy(piece(x, y, c), og_ref.at[me], local_sems.at[0])
        own_s = pltpu.make_async_copy(s_ref, os_ref.at[me], local_sems.at[1])
        own_g.start()
        own_s.start()
        copies = []
        k = 0
        for dx in range(2):
            for dy in range(2):
                for dc in range(2):
                    if dx + dy + dc == 0:
                        continue
                    px, py, pc = x ^ dx, y ^ dy, c ^ dc
                    copies.append(pltpu.make_async_remote_copy(
                        src_ref=piece(px, py, pc), dst_ref=og_ref.at[me],
                        send_sem=send_sems.at[k], recv_sem=recv_sems.at[k], device_id=(px, py, pc), device_id_type=MESH))
                    copies.append(pltpu.make_async_remote_copy(
                        src_ref=s_ref, dst_ref=os_ref.at[me],
                        send_sem=send_sems.at[7 + k], recv_sem=recv_sems.at[7 + k], device_id=(px, py, pc),
                        device_id_type=MESH))
                    k += 1
        for cp in copies:
            cp.start()
        for cp in copies:
            cp.wait_recv()
        for cp in copies:
            cp.wait_send()
        own_g.wait()
        own_s.wait()

    return pl.pallas_call(
        body, in_specs=[_ANY, _ANY], out_specs=[_ANY, _ANY],
        out_shape=[jax.ShapeDtypeStruct((8, HALF_ROWS, 1024), f32), jax.ShapeDtypeStruct((8, SMALL_ROWS, 128), f32)],
        scratch_shapes=[pltpu.SemaphoreType.DMA((14,)), pltpu.SemaphoreType.DMA((14,)), pltpu.SemaphoreType.DMA((2,))],
        name="scatter_grads")(gpack, gsmall)


def _sibling_exchange(half):
    def body(h_ref, o_ref, send_sem, recv_sem, local_sem):
        x, y, c = lax.axis_index("x"), lax.axis_index("y"), lax.axis_index("c")
        own = pltpu.make_async_copy(h_ref, o_ref.at[c], local_sem)
        own.start()
        cp = pltpu.make_async_remote_copy(src_ref=h_ref, dst_ref=o_ref.at[c], send_sem=send_sem, recv_sem=recv_sem,
                                          device_id=(x, y, 1 - c), device_id_type=MESH)
        cp.start()
        cp.wait_recv()
        cp.wait_send()
        own.wait()

    return pl.pallas_call(
        body, in_specs=[_ANY], out_specs=_ANY,
        out_shape=jax.ShapeDtypeStruct((2, HALF_ROWS, 1024), f32),
        scratch_shapes=[pltpu.SemaphoreType.DMA, pltpu.SemaphoreType.DMA, pltpu.SemaphoreType.DMA],
        name="sibling_exchange")(half)


_BIG_ROWS = (1536, 256, 256, 64, 128)
_CONV_PAD = 8192


def _pack_rows(mats, conv_rows, dtype):
    parts = [mats[0].reshape(1536, 1024), mats[1], mats[2], mats[3].reshape(64, 1024), mats[4].reshape(128, 1024)]
    parts = [p.astype(dtype) for p in parts] + [conv_rows]
    used = sum(p.shape[0] for p in parts)
    parts.append(jnp.zeros((PACK_ROWS - used, 1024), dtype))
    return jnp.concatenate(parts, axis=0)


def _pack_weights(mats, conv_w_s):
    flat = jnp.pad(conv_w_s.reshape(-1), (0, _CONV_PAD - KS * 256))
    bits = lax.bitcast_convert_type(flat, bf16).reshape(16, 1024)
    return _pack_rows(mats, bits, bf16)


def _pack_grads(mats, conv_w_s):
    flat = jnp.pad(conv_w_s.reshape(-1), (0, _CONV_PAD - KS * 256))
    return _pack_rows(mats, flat.reshape(8, 1024), f32)


def _split_rows(p, conv_rows):
    o = 0
    out = []
    for rows in _BIG_ROWS + (conv_rows,):
        out.append(p[..., o:o + rows, :])
        o += rows
    return out


def _unpack_grads(p):
    s = _split_rows(p, 8)
    return (s[0].reshape(1024, 1536), s[1], s[2], s[3].reshape(128, 512), s[4].reshape(512, 256),
            s[5].reshape(-1)[:KS * 256].reshape(KS, 256))


_SMALL = (("pre_norm_gain", (1, 1024)), ("conv_b", (1, 1024)), ("conv_ln_gain", (1, 1024)), ("conv_ln_bias", (1, 1024)),
          ("ssm_lambda_re", (1, 32, 64)), ("ssm_lambda_im", (1, 32, 64)), ("ssm_log_dt", (1, 32)),
          ("ssm_b_re", (1, 32, 64, 16)), ("ssm_b_im", (1, 32, 64, 16)), ("ssm_c_re", (1, 32, 16, 64)),
          ("ssm_c_im", (1, 32, 16, 64)), ("ssm_d", (1, 32, 16)), ("b_ssm_glu", (1, 512)), ("post_norm_gain", (1, 1024)))


def _pack_small(vals, extra=None):
    rows = []
    for v in list(vals) + ([extra] if extra is not None else []):
        flat = v.reshape(-1).astype(f32)
        n = -(-flat.shape[0] // 1024) * 1024
        rows.append(jnp.pad(flat, (0, n - flat.shape[0])).reshape(-1, 128))
    used = sum(r.shape[0] for r in rows)
    rows.append(jnp.zeros((SMALL_ROWS - used, 128), f32))
    return jnp.concatenate(rows, axis=0)


def _unpack_small(p):
    o = 0
    out = []
    for _, shape in _SMALL:
        n = int(np.prod(shape))
        nr = -(-n // 1024) * 8
        out.append(p[o:o + nr].reshape(-1)[:n].reshape(shape))
        o += nr
    return out, p[o, 0]


def _discretize(lam_re, lam_im, log_dt, b_re, b_im):
    dt = jnp.exp(log_dt)[:, None]
    mag = jnp.exp(lam_re * dt)
    ar = mag * jnp.cos(lam_im * dt)
    ai = mag * jnp.sin(lam_im * dt)
    den = lam_re * lam_re + lam_im * lam_im
    zr = ((ar - 1.0) * lam_re + ai * lam_im) / den
    zi = (ai * lam_re - (ar - 1.0) * lam_im) / den
    bbr = zr[..., None] * b_re - zi[..., None] * b_im
    bbi = zr[..., None] * b_im + zi[..., None] * b_re
    return ar, ai, bbr, bbi


_EYE8 = np.eye(8, dtype=np.float32)


def _bbt_blocks(bb):
    v = bb.reshape(4, 8, PST, H).transpose(0, 1, 3, 2)
    return jnp.einsum("bghp,gk->bghkp", v, _EYE8).reshape(4, 128, 512)


def _bbt_unblock(m):
    v = jnp.einsum("bghkp,gk->bghp", m.reshape(4, 8, H, 8, PST), _EYE8)
    return v.transpose(0, 1, 3, 2).reshape(G, PST, H)


def _ct_blocks(cc):
    v = cc.reshape(4, 8, H, PST)
    return jnp.einsum("bghp,gk->bgpkh", v, _EYE8).reshape(4, 512, 128)


def _ct_unblock(m):
    v = jnp.einsum("bgpkh,gk->bghp", m.reshape(4, 8, PST, 8, H), _EYE8)
    return v.reshape(G, H, PST)


def _perm_matrix():
    p = np.zeros((TC, TC), np.float32)
    for r in range(R):
        for seg in range(8):
            p[r * 8 + seg, seg * R + r] = 1.0
    return p


def _local_step(x, tgt, w_in, conv_w, w_co, w_glu, w_so, w_out, small):
    (g_pre, conv_b, ln_g, ln_b, lam_re, lam_im, log_dt, b_re, b_im, c_re, c_im, dvec, b_glu, g_post) = small
    lam_re, lam_im, log_dt = lam_re[0], lam_im[0], log_dt[0]
    b_re, b_im, c_re, c_im = b_re[0], b_im[0], c_re[0], c_im[0]
    (ar, ai, bbr, bbi), disc_vjp = jax.vjp(_discretize, lam_re, lam_im, log_dt, b_re, b_im)
    a_re = ar.reshape(1, NS)
    a_im = ai.reshape(1, NS)
    dt = jnp.exp(log_dt)[:, None]
    steps = jnp.arange(1, R + 1, dtype=f32)[:, None, None]
    apow_re = (jnp.exp(steps * (lam_re * dt)) * jnp.cos(steps * (lam_im * dt))).reshape(R, NS)
    apow_im = (jnp.exp(steps * (lam_re * dt)) * jnp.sin(steps * (lam_im * dt))).reshape(R, NS)
    bbt_re, bbt_im = _bbt_blocks(bbr).astype(bf16), _bbt_blocks(bbi).astype(bf16)
    ct_re, ct_im = _ct_blocks(c_re).astype(bf16), _ct_blocks(c_im).astype(bf16)
    perm_np = _perm_matrix()
    perm = jnp.asarray(perm_np, bf16)
    perm_t = jnp.asarray(perm_np.T, bf16)
    d_row = dvec.reshape(1, SW)
    cw32 = jnp.pad(conv_w, ((0, 1), (0, 0)))

    h = _prenorm(x, g_pre)
    proj = _proj_fwd(h, w_in)
    cu1, a_in = _conv_fwd(proj, cw32, conv_b, ln_g, ln_b)
    y0, b_in, sre, sim, cinr, cini = _ssm_fwd(proj, perm, perm_t, bbt_re, bbt_im, ct_re, ct_im, a_re, a_im,
                                              apow_re, apow_im, d_row, w_glu, b_glu)
    gx0, d_ain, d_bin, d_gc, d_gs, dw_out, dw_co, dw_so, dg_post, loss = _tail(
        a_in, b_in, proj, x, tgt, w_co, w_so, w_out, g_post)
    (d_u, d_zs, dbbt_re, dbbt_im, dct_re, dct_im, dd, dar8, dai8, dw_glu, db_glu) = _ssm_bwd(
        d_bin, y0, proj, sre, sim, cinr, cini, perm, perm_t, bbt_re, bbt_im, ct_re, ct_im,
        a_re, a_im, apow_re, apow_im, d_row, w_glu, b_glu)
    d_ca, d_cb, d_zc, dcw8, d_convb, d_lng, d_lnb = _conv_bwd(d_ain, cu1, proj, cw32, ln_g, ln_b)
    dproj = jnp.concatenate([d_ca, d_cb, d_zc, d_u, d_zs, d_gc, d_gs], axis=1)
    dw_in = _win_grad(h, dproj)
    grad_x, dg_pre = _x_grad(dproj, w_in, x, gx0, g_pre)

    d_ar = jnp.sum(dar8, axis=0).reshape(G, PST)
    d_ai = jnp.sum(dai8, axis=0).reshape(G, PST)
    d_lre, d_lim, d_ldt, d_bre, d_bim = disc_vjp((d_ar, d_ai, _bbt_unblock(dbbt_re), _bbt_unblock(dbbt_im)))
    d_conv_w = jnp.sum(dcw8, axis=1)[:KS]
    small_grads = [dg_pre, d_convb, d_lng, d_lnb, d_lre[None], d_lim[None], d_ldt[None], d_bre[None], d_bim[None],
                   _ct_unblock(dct_re)[None], _ct_unblock(dct_im)[None], dd.reshape(1, G, H), db_glu, dg_post]
    return loss[0, 0], grad_x, (dw_in, dw_co, dw_out, dw_glu, dw_so, d_conv_w), small_grads


def kernel(x, pre_norm_gain, w_in, conv_w, conv_b, conv_ln_gain, conv_ln_bias, w_conv_out, ssm_lambda_re, ssm_lambda_im, ssm_log_dt, ssm_b_re, ssm_b_im, ssm_c_re, ssm_c_im, ssm_d, w_ssm_glu, b_ssm_glu, w_ssm_out, w_out, post_norm_gain, loss_target, m_pre_norm_gain, m_w_in, m_conv_w, m_conv_b, m_conv_ln_gain, m_conv_ln_bias, m_w_conv_out, m_ssm_lambda_re, m_ssm_lambda_im, m_ssm_log_dt, m_ssm_b_re, m_ssm_b_im, m_ssm_c_re, m_ssm_c_im, m_ssm_d, m_w_ssm_glu, m_b_ssm_glu, m_w_ssm_out, m_w_out, m_post_norm_gain, v_pre_norm_gain, v_w_in, v_conv_w, v_conv_b, v_conv_ln_gain, v_conv_ln_bias, v_w_conv_out, v_ssm_lambda_re, v_ssm_lambda_im, v_ssm_log_dt, v_ssm_b_re, v_ssm_b_im, v_ssm_c_re, v_ssm_c_im, v_ssm_d, v_w_ssm_glu, v_b_ssm_glu, v_w_ssm_out, v_w_out, v_post_norm_gain):
    wp = _pack_weights((w_in[0], w_conv_out[0], w_out[0], w_ssm_glu[0], w_ssm_out[0]), conv_w[0])
    wg = _gather_weights(wp)
    secs = _split_rows(wg, 16)
    w_in_f = secs[0].reshape(NCHIP, D, SHARD_W).transpose(1, 0, 2).reshape(D, IN_W)
    w_co_f = secs[1].reshape(CW, D)
    w_out_f = secs[2].reshape(D, D)
    w_glu_f = secs[3].reshape(SW, SW)
    w_so_f = secs[4].reshape(NCHIP, SW, 256).transpose(1, 0, 2).reshape(SW, D)
    conv_w_f = lax.bitcast_convert_type(secs[5].reshape(NCHIP, _CONV_PAD, 2), f32)[:, :KS * 256]
    conv_w_f = conv_w_f.reshape(NCHIP, KS, 256).transpose(1, 0, 2).reshape(KS, CW)

    small = (pre_norm_gain, conv_b, conv_ln_gain, conv_ln_bias, ssm_lambda_re, ssm_lambda_im, ssm_log_dt, ssm_b_re,
             ssm_b_im, ssm_c_re, ssm_c_im, ssm_d, b_ssm_glu, post_norm_gain)
    loss_part, grad_x, big_grads, small_grads = _local_step(
        x[0], loss_target[0], w_in_f, conv_w_f, w_co_f, w_glu_f, w_so_f, w_out_f, small)

    dw_in, dw_co, dw_out, dw_glu, dw_so, d_conv_w = big_grads
    gpack = jnp.stack([
        _pack_grads((dw_in[:, j * SHARD_W:(j + 1) * SHARD_W], dw_co[j * 256:(j + 1) * 256],
                     dw_out[j * 256:(j + 1) * 256], dw_glu[j * 128:(j + 1) * 128], dw_so[:, j * 256:(j + 1) * 256]),
                    d_conv_w[:, j * 256:(j + 1) * 256])
        for j in range(NCHIP)])
    gsmall = _pack_small(small_grads, extra=loss_part)
    parts_big, parts_small = _scatter_grads(gpack, gsmall)
    gsum = _sibling_exchange(_sum8(parts_big)).reshape(PACK_ROWS, 1024)
    g_big = _unpack_grads(gsum)

    big_w = (w_in[0], w_conv_out[0], w_out[0], w_ssm_glu[0], w_ssm_out[0], conv_w[0])
    big_m = (m_w_in[0], m_w_conv_out[0], m_w_out[0], m_w_ssm_glu[0], m_w_ssm_out[0], m_conv_w[0])
    big_v = (v_w_in[0], v_w_conv_out[0], v_w_out[0], v_w_ssm_glu[0], v_w_ssm_out[0], v_conv_w[0])
    big_names = ("w_in", "w_conv_out", "w_out", "w_ssm_glu", "w_ssm_out", "conv_w")
    big_out = {}
    for n, w, g, m, v in zip(big_names, big_w, g_big, big_m, big_v):
        d, m2, v2 = _adamw("adamw_" + n, w, g, m, v)
        big_out[n] = (g[None], d[None], m2[None], v2[None])

    small_m = (m_pre_norm_gain, m_conv_b, m_conv_ln_gain, m_conv_ln_bias, m_ssm_lambda_re, m_ssm_lambda_im, m_ssm_log_dt,
               m_ssm_b_re, m_ssm_b_im, m_ssm_c_re, m_ssm_c_im, m_ssm_d, m_b_ssm_glu, m_post_norm_gain)
    small_v = (v_pre_norm_gain, v_conv_b, v_conv_ln_gain, v_conv_ln_bias, v_ssm_lambda_re, v_ssm_lambda_im, v_ssm_log_dt,
               v_ssm_b_re, v_ssm_b_im, v_ssm_c_re, v_ssm_c_im, v_ssm_d, v_b_ssm_glu, v_post_norm_gain)
    sg, sd, sm, sv = _adamw_small(parts_small, _pack_small(small), _pack_small(small_m), _pack_small(small_v))
    sg_l, loss = _unpack_small(sg)
    sd_l, _ = _unpack_small(sd)
    sm_l, _ = _unpack_small(sm)
    sv_l, _ = _unpack_small(sv)
    small_out = {n: (sg_l[i], sd_l[i], sm_l[i], sv_l[i]) for i, (n, _) in enumerate(_SMALL)}

    order = ("pre_norm_gain", "w_in", "conv_w", "conv_b", "conv_ln_gain", "conv_ln_bias", "w_conv_out", "ssm_lambda_re",
             "ssm_lambda_im", "ssm_log_dt", "ssm_b_re", "ssm_b_im", "ssm_c_re", "ssm_c_im", "ssm_d", "w_ssm_glu",
             "b_ssm_glu", "w_ssm_out", "w_out", "post_norm_gain")
    res = {**big_out, **small_out}
    outs = [loss, grad_x[None]]
    for k in range(4):
        outs.extend(res[n][k] for n in order)
    return tuple(outs)
```

```python
import math

import numpy as np
import jax
import jax.numpy as jnp
from jax import lax
from jax.experimental import pallas as pl
from jax.experimental.pallas import tpu as pltpu

f32 = jnp.float32
bf16 = jnp.bfloat16

D = 1024
CW = 1024
SW = 512
G = 32
H = 16
PST = 64
NS = G * PST
KS = 31
IN_W = 6144
NCHIP = 4
SHARD_W = IN_W // NCHIP
RMS_EPS = 1e-6
LN_EPS = 1e-5
LR, B1, B2, EPS, WD, STEP = 0.001, 0.9, 0.999, 1e-08, 0.01, 10
GELU_K0 = math.sqrt(2.0 / math.pi)
GELU_K1 = 0.044715

TC = 512
R = TC // 8
NH = 32
LBW = 512
REST_ROWS = 768
SMALL_ROWS = 1152
VMEM_LIMIT = 56 * 1024 * 1024
MESH = pl.DeviceIdType.MESH


def _cp(*sem):
    return pltpu.CompilerParams(dimension_semantics=tuple(sem), vmem_limit_bytes=VMEM_LIMIT)


def _sig(v):
    return 1.0 / (1.0 + jnp.exp(-v))


def _dot(a, b):
    return jnp.dot(a, b, preferred_element_type=f32)


def _dot_nt(a, b):
    return lax.dot_general(a, b, (((1,), (1,)), ((), ())), preferred_element_type=f32)


def _dot_tn(a, b):
    return lax.dot_general(a, b, (((0,), (0,)), ((), ())), preferred_element_type=f32)


def _full(shape):
    nd = len(shape)
    return pl.BlockSpec(shape, lambda *_: (0,) * nd)


def _rows8(i):
    return pl.ds(pl.multiple_of(i * 8, 8), 8)


def _prenorm(x, g_pre):
    L = x.shape[0]
    tm = 512
    def body(x_ref, g_ref, h_ref):
        xt = x_ref[...]
        r = lax.rsqrt(jnp.mean(xt * xt, axis=-1, keepdims=True) + RMS_EPS)
        h_ref[...] = (xt * r * g_ref[...]).astype(bf16)
    return pl.pallas_call(
        body, grid=(L // tm,),
        in_specs=[pl.BlockSpec((tm, D), lambda i: (i, 0)), _full((1, D))],
        out_specs=pl.BlockSpec((tm, D), lambda i: (i, 0)),
        out_shape=jax.ShapeDtypeStruct((L, D), bf16),
        name="prenorm", compiler_params=_cp("arbitrary"))(x, g_pre)


def _proj_fwd(h, w_in):
    L = h.shape[0]
    tm = 512
    def body(h_ref, w_ref, o_ref):
        o_ref[...] = _dot(h_ref[...], w_ref[0]).astype(bf16)
    return pl.pallas_call(
        body, grid=(NCHIP, L // tm),
        in_specs=[pl.BlockSpec((tm, D), lambda j, i: (i, 0)), pl.BlockSpec((1, D, SHARD_W), lambda j, i: (j, 0, 0))],
        out_specs=pl.BlockSpec((tm, SHARD_W), lambda j, i: (i, j)),
        out_shape=jax.ShapeDtypeStruct((L, IN_W), bf16),
        name="proj_fwd", compiler_params=_cp("arbitrary", "arbitrary"))(h, w_in)


NLB = CW // 128
RPI = 4


def _put_blocked(buf, row0, nrows, v):
    for lb in range(NLB):
        buf[lb, pl.ds(row0, nrows), :] = v[:, lb * 128:(lb + 1) * 128]


def _get_blocked(buf, row0, nrows):
    return jnp.concatenate([buf[lb, pl.ds(row0, nrows), :] for lb in range(NLB)], axis=1)


def _fill_before(ebuf, prev):
    sub = lax.broadcasted_iota(jnp.int32, (8, 128), 0)
    def halo(p, carry):
        for lb in range(NLB):
            cur = ebuf[lb, _rows8(R + p), :]
            ebuf[lb, _rows8(p), :] = jnp.where(sub == 0, pltpu.roll(prev[lb, _rows8(p), :], 1, 0),
                                               pltpu.roll(cur, 1, 0))
        return carry
    lax.fori_loop(0, NH, halo, 0)


def _fir(buf, lb, r, coef, first, flip):
    win = buf[lb, pl.ds(pl.multiple_of(r * 8, 8), (KS + RPI - 1) * 8), :]
    outs = []
    for i in range(RPI):
        acc = [first, None, None, None]
        for k in range(KS):
            o = i + ((KS - 1 - k) if flip else k)
            t = coef[k] * win[8 * o:8 * o + 8, :]
            acc[k % 4] = t if acc[k % 4] is None else acc[k % 4] + t
        outs.append((acc[0] + acc[1]) + (acc[2] + acc[3]))
    return outs


def _conv_fwd(proj, cw, cbias, lng, lnb):
    L = proj.shape[0]
    nc = L // TC
    def body(ca_ref, cb_ref, zc_ref, w_ref, b_ref, g_ref, bb_ref, cu1_ref, ain_ref, ebuf, prev, cacc):
        @pl.when(pl.program_id(0) == 0)
        def _():
            prev[...] = jnp.zeros_like(prev)
        def glu(s, carry):
            rows = pl.ds(pl.multiple_of(s * 64, 64), 64)
            _put_blocked(ebuf, pl.multiple_of(NH * 8 + s * 64, 64), 64,
                         ca_ref[rows, :].astype(f32) * _sig(cb_ref[rows, :].astype(f32)))
            return carry
        lax.fori_loop(0, TC // 64, glu, 0)
        _fill_before(ebuf, prev)
        prev[...] = ebuf[:, R * 8:(NH + R) * 8, :]
        for lb in range(NLB):
            sl = slice(lb * 128, (lb + 1) * 128)
            wk = [jnp.broadcast_to(w_ref[k:k + 1, sl], (8, 128)) for k in range(KS)]
            bias = jnp.broadcast_to(b_ref[:, sl], (8, 128))
            def tap(q, carry, lb=lb, wk=wk, bias=bias):
                r = q * RPI
                for i, o in enumerate(_fir(ebuf, lb, r + (NH - KS + 1), wk, bias, False)):
                    cacc[lb, _rows8(r + i), :] = o
                return carry
            lax.fori_loop(0, R // RPI, tap, 0)
        def norm(s, carry):
            rows = pl.ds(pl.multiple_of(s * 64, 64), 64)
            c1b = _get_blocked(cacc, pl.multiple_of(s * 64, 64), 64).astype(bf16)
            cu1_ref[rows, :] = c1b
            c1 = c1b.astype(f32)
            xc = c1 - jnp.mean(c1, axis=-1, keepdims=True)
            var = jnp.mean(xc * xc, axis=-1, keepdims=True)
            ln = xc * lax.rsqrt(var + LN_EPS) * g_ref[...] + bb_ref[...]
            zc = zc_ref[rows, :].astype(f32)
            ain_ref[rows, :] = ((ln * _sig(ln)) * (zc * _sig(zc))).astype(bf16)
            return carry
        lax.fori_loop(0, TC // 64, norm, 0)

    col = lambda c: pl.BlockSpec((TC, CW), lambda i, c=c: (i, c))
    return pl.pallas_call(
        body, grid=(nc,),
        in_specs=[col(0), col(1), col(2), _full((32, CW)), _full((1, CW)), _full((1, CW)), _full((1, CW))],
        out_specs=[pl.BlockSpec((TC, CW), lambda i: (i, 0)), pl.BlockSpec((TC, CW), lambda i: (i, 0))],
        out_shape=[jax.ShapeDtypeStruct((L, CW), bf16), jax.ShapeDtypeStruct((L, CW), bf16)],
        scratch_shapes=[pltpu.VMEM((NLB, (NH + R) * 8, 128), f32), pltpu.VMEM((NLB, NH * 8, 128), f32),
                        pltpu.VMEM((NLB, TC, 128), f32)],
        name="conv_fwd", compiler_params=_cp("arbitrary"))(proj, proj, proj, cw, cbias, lng, lnb)


def _gelu_parts(y0):
    t = jnp.tanh(GELU_K0 * (y0 + GELU_K1 * y0 * y0 * y0))
    return t, 0.5 * y0 * (1.0 + t)


def _ssm_fwd(proj, bbt_re, bbt_im, ct_re, ct_im, a_re, a_im, apow_re, apow_im, dvec, wglu, bglu):
    L = proj.shape[0]
    nc = L // TC
    def body(u_ref, zs_ref, bre_ref, bim_ref, cre_ref, cim_ref, are_ref, aim_ref, pwr_ref, pwi_ref,
             d_ref, wg_ref, bg_ref, y0_ref, bin_ref, sre, sim, cinr, cini, prev_re, prev_im):
        c = pl.program_id(0)
        @pl.when(c == 0)
        def _():
            prev_re[...] = jnp.zeros_like(prev_re)
            prev_im[...] = jnp.zeros_like(prev_im)
        u = u_ref[...]
        for blk in range(4):
            ub = u[:, 128 * blk:128 * (blk + 1)]
            sre[:, 512 * blk:512 * (blk + 1)] = _dot(ub, bre_ref[blk])
            sim[:, 512 * blk:512 * (blk + 1)] = _dot(ub, bim_ref[blk])
        for lb in range(NS // LBW):
            sl = slice(lb * LBW, (lb + 1) * LBW)
            ar = jnp.broadcast_to(are_ref[:, sl], (8, LBW))
            ai = jnp.broadcast_to(aim_ref[:, sl], (8, LBW))
            def step(r, carry, sl=sl, ar=ar, ai=ai):
                sr, si = carry
                nr = ar * sr - ai * si + sre[_rows8(r), sl]
                ni = ar * si + ai * sr + sim[_rows8(r), sl]
                sre[_rows8(r), sl] = nr
                sim[_rows8(r), sl] = ni
                return nr, ni
            lax.fori_loop(1, R, step, (sre[0:8, sl], sim[0:8, sl]))
        a_r = pwr_ref[R - 1:R, :]
        a_i = pwi_ref[R - 1:R, :]
        cr = prev_re[0:1, :]
        ci = prev_im[0:1, :]
        for seg in range(8):
            cinr[seg:seg + 1, :] = cr
            cini[seg:seg + 1, :] = ci
            er = sre[8 * (R - 1) + seg:8 * (R - 1) + seg + 1, :]
            ei = sim[8 * (R - 1) + seg:8 * (R - 1) + seg + 1, :]
            cr, ci = er + a_r * cr - a_i * ci, ei + a_r * ci + a_i * cr
        prev_re[0:1, :] = cr
        prev_im[0:1, :] = ci
        for lb in range(NS // LBW):
            sl = slice(lb * LBW, (lb + 1) * LBW)
            kr = cinr[:, sl]
            ki = cini[:, sl]
            def fix(r, carry, sl=sl, kr=kr, ki=ki):
                pr = jnp.broadcast_to(pwr_ref[pl.ds(r, 1), sl], (8, LBW))
                pi = jnp.broadcast_to(pwi_ref[pl.ds(r, 1), sl], (8, LBW))
                sre[_rows8(r), sl] = sre[_rows8(r), sl] + pr * kr - pi * ki
                sim[_rows8(r), sl] = sim[_rows8(r), sl] + pr * ki + pi * kr
                return carry
            lax.fori_loop(0, R, fix, 0)
        yp = []
        for blk in range(4):
            sr = sre[:, 512 * blk:512 * (blk + 1)].astype(bf16)
            si = sim[:, 512 * blk:512 * (blk + 1)].astype(bf16)
            yp.append(_dot(sr, cre_ref[blk]) - _dot(si, cim_ref[blk]))
        y0 = jnp.concatenate(yp, axis=1) + d_ref[...] * u.astype(f32)
        y0_ref[...] = y0
        _, y1 = _gelu_parts(y0)
        glu = _dot(y1.astype(bf16), wg_ref[...]) + bg_ref[...]
        y2 = y1 * _sig(glu)
        zs = zs_ref[...].astype(f32)
        bin_ref[...] = (y2 * (zs * _sig(zs))).astype(bf16)

    return pl.pallas_call(
        body, grid=(nc,),
        in_specs=[pl.BlockSpec((TC, SW), lambda c: (c, 6)), pl.BlockSpec((TC, SW), lambda c: (c, 7)),
                  _full((4, 128, 512)), _full((4, 128, 512)), _full((4, 512, 128)), _full((4, 512, 128)),
                  _full((1, NS)), _full((1, NS)), _full((R, NS)), _full((R, NS)),
                  _full((1, SW)), _full((SW, SW)), _full((1, SW))],
        out_specs=[pl.BlockSpec((TC, SW), lambda c: (c, 0)), pl.BlockSpec((TC, SW), lambda c: (c, 0)),
                   pl.BlockSpec((TC, NS), lambda c: (c, 0)), pl.BlockSpec((TC, NS), lambda c: (c, 0)),
                   pl.BlockSpec((8, NS), lambda c: (c, 0)), pl.BlockSpec((8, NS), lambda c: (c, 0))],
        out_shape=[jax.ShapeDtypeStruct((L, SW), f32), jax.ShapeDtypeStruct((L, SW), bf16),
                   jax.ShapeDtypeStruct((L, NS), f32), jax.ShapeDtypeStruct((L, NS), f32),
                   jax.ShapeDtypeStruct((nc * 8, NS), f32), jax.ShapeDtypeStruct((nc * 8, NS), f32)],
        scratch_shapes=[pltpu.VMEM((8, NS), f32), pltpu.VMEM((8, NS), f32)],
        name="ssm_fwd", compiler_params=_cp("arbitrary"))(
            proj, proj, bbt_re, bbt_im, ct_re, ct_im, a_re, a_im, apow_re, apow_im, dvec, wglu, bglu)


def _tail(a_in, b_in, proj, x, tgt, wco, wso, wout, gpost):
    L = x.shape[0]
    tm = 256
    def body(a_ref, b_ref, gc_ref, gs_ref, x_ref, t_ref, wco_ref, wso_ref, wout_ref, gp_ref,
             gx_ref, dain_ref, dbin_ref, dgc_ref, dgs_ref, dwout_ref, dwco_ref, dwso_ref, dgp_ref, loss_ref):
        @pl.when(pl.program_id(0) == 0)
        def _():
            dwout_ref[...] = jnp.zeros_like(dwout_ref)
            dwco_ref[...] = jnp.zeros_like(dwco_ref)
            dwso_ref[...] = jnp.zeros_like(dwso_ref)
            dgp_ref[...] = jnp.zeros_like(dgp_ref)
            loss_ref[...] = jnp.zeros_like(loss_ref)
        a = a_ref[...]
        b = b_ref[...]
        co = _dot(a, wco_ref[...])
        so = _dot(b, wso_ref[...])
        sc = _sig(gc_ref[...].astype(f32))
        ss = _sig(gs_ref[...].astype(f32))
        mb = (sc * co + ss * so).astype(bf16)
        out = _dot(mb, wout_ref[...])
        r2 = lax.rsqrt(jnp.mean(out * out, axis=-1, keepdims=True) + RMS_EPS)
        on = out * r2
        gp = gp_ref[...]
        e = x_ref[...] + on * gp - t_ref[...]
        loss_ref[...] += (0.5 / D) * jnp.sum(e * e)
        dy = e * (1.0 / D)
        gx_ref[...] = dy
        dgp_ref[...] += jnp.sum(dy * on, axis=0, keepdims=True)
        dn = dy * gp
        dout = (r2 * (dn - on * jnp.mean(dn * on, axis=-1, keepdims=True))).astype(bf16)
        dwout_ref[...] += _dot_tn(mb, dout)
        dm = _dot_nt(dout, wout_ref[...])
        dgc_ref[...] = (dm * co * sc * (1.0 - sc)).astype(bf16)
        dgs_ref[...] = (dm * so * ss * (1.0 - ss)).astype(bf16)
        dco = (dm * sc).astype(bf16)
        dso = (dm * ss).astype(bf16)
        dwco_ref[...] += _dot_tn(a, dco)
        dwso_ref[...] += _dot_tn(b, dso)
        dain_ref[...] = _dot_nt(dco, wco_ref[...]).astype(bf16)
        dbin_ref[...] = _dot_nt(dso, wso_ref[...]).astype(bf16)

    row = lambda w: pl.BlockSpec((tm, w), lambda i: (i, 0))
    one = lambda shape: pl.BlockSpec(shape, lambda i: (0,) * len(shape), pipeline_mode=pl.Buffered(1))
    return pl.pallas_call(
        body, grid=(L // tm,),
        in_specs=[row(CW), row(SW), pl.BlockSpec((tm, D), lambda i: (i, 4)), pl.BlockSpec((tm, D), lambda i: (i, 5)),
                  row(D), row(D), one((CW, D)), one((SW, D)), one((D, D)), one((1, D))],
        out_specs=[row(D), row(CW), row(SW), row(D), row(D),
                   one((D, D)), one((CW, D)), one((SW, D)), one((1, D)), one((1, 128))],
        out_shape=[jax.ShapeDtypeStruct((L, D), f32), jax.ShapeDtypeStruct((L, CW), bf16),
                   jax.ShapeDtypeStruct((L, SW), bf16), jax.ShapeDtypeStruct((L, D), bf16),
                   jax.ShapeDtypeStruct((L, D), bf16),
                   jax.ShapeDtypeStruct((D, D), f32), jax.ShapeDtypeStruct((CW, D), f32),
                   jax.ShapeDtypeStruct((SW, D), f32), jax.ShapeDtypeStruct((1, D), f32),
                   jax.ShapeDtypeStruct((1, 128), f32)],
        name="tail", compiler_params=_cp("arbitrary"))(a_in, b_in, proj, proj, x, tgt, wco, wso, wout, gpost)


def _ssm_bwd(d_bin, y0, proj, sre, sim, cinr, cini, bbt_re, bbt_im, ct_re, ct_im,
             a_re, a_im, apow_re, apow_im, dvec, wglu, bglu):
    L = y0.shape[0]
    nc = L // TC
    def body(dbin_ref, y0_ref, u_ref, zs_ref, sre_ref, sim_ref, cinr_ref, cini_ref,
             bre_ref, bim_ref, cre_ref, cim_ref, are_ref, aim_ref, pwr_ref, pwi_ref, d_ref, wg_ref, bg_ref,
             du_ref, dzs_ref, dbre_ref, dbim_ref, dcre_ref, dcim_ref, dd_ref, dar_ref, dai_ref, dwg_ref, dbg_ref,
             gre, gim, gcr, gci, nxt_re, nxt_im):
        @pl.when(pl.program_id(0) == 0)
        def _():
            for ref in (dbre_ref, dbim_ref, dcre_ref, dcim_ref, dd_ref, dar_ref, dai_ref, dwg_ref, dbg_ref,
                        nxt_re, nxt_im):
                ref[...] = jnp.zeros_like(ref)
        y0 = y0_ref[...]
        u = u_ref[...]
        zs = zs_ref[...].astype(f32)
        dbin = dbin_ref[...].astype(f32)
        t, y1 = _gelu_parts(y0)
        y1b = y1.astype(bf16)
        sg = _sig(_dot(y1b, wg_ref[...]) + bg_ref[...])
        sz = _sig(zs)
        d_y2 = dbin * (zs * sz)
        dzs_ref[...] = (dbin * (y1 * sg) * (sz * (1.0 + zs * (1.0 - sz)))).astype(bf16)
        d_glu = d_y2 * y1 * sg * (1.0 - sg)
        d_glub = d_glu.astype(bf16)
        d_y1 = d_y2 * sg + _dot_nt(d_glub, wg_ref[...])
        dwg_ref[...] += _dot_tn(y1b, d_glub)
        dbg_ref[...] += jnp.sum(d_glu, axis=0, keepdims=True)
        dgelu = 0.5 * (1.0 + t) + 0.5 * y0 * (1.0 - t * t) * GELU_K0 * (1.0 + 3.0 * GELU_K1 * y0 * y0)
        d_y0 = d_y1 * dgelu
        dd_ref[...] += jnp.sum(d_y0 * u.astype(f32), axis=0, keepdims=True)
        dyb = d_y0.astype(bf16)
        for blk in range(4):
            dy1 = dyb[:, 128 * blk:128 * (blk + 1)]
            gre[:, 512 * blk:512 * (blk + 1)] = _dot_nt(dy1, cre_ref[blk])
            gim[:, 512 * blk:512 * (blk + 1)] = -_dot_nt(dy1, cim_ref[blk])
        for lb in range(NS // LBW):
            sl = slice(lb * LBW, (lb + 1) * LBW)
            ar = jnp.broadcast_to(are_ref[:, sl], (8, LBW))
            ai = jnp.broadcast_to(aim_ref[:, sl], (8, LBW))
            def step(k, carry, sl=sl, ar=ar, ai=ai):
                gr, gi = carry
                row = _rows8(R - 2 - k)
                nr = ar * gr + ai * gi + gre[row, sl]
                ni = ar * gi - ai * gr + gim[row, sl]
                gre[row, sl] = nr
                gim[row, sl] = ni
                return nr, ni
            lax.fori_loop(0, R - 1, step, (gre[8 * (R - 1):8 * R, sl], gim[8 * (R - 1):8 * R, sl]))
        a_r = pwr_ref[R - 1:R, :]
        a_i = pwi_ref[R - 1:R, :]
        cr = nxt_re[0:1, :]
        ci = nxt_im[0:1, :]
        for seg in range(7, -1, -1):
            gcr[seg:seg + 1, :] = cr
            gci[seg:seg + 1, :] = ci
            er = gre[seg:seg + 1, :]
            ei = gim[seg:seg + 1, :]
            cr, ci = er + a_r * cr + a_i * ci, ei + a_r * ci - a_i * cr
        nxt_re[0:1, :] = cr
        nxt_im[0:1, :] = ci
        for lb in range(NS // LBW):
            sl = slice(lb * LBW, (lb + 1) * LBW)
            kr = gcr[:, sl]
            ki = gci[:, sl]
            def fix(r, carry, sl=sl, kr=kr, ki=ki):
                pr = jnp.broadcast_to(pwr_ref[pl.ds(R - 1 - r, 1), sl], (8, LBW))
                pi = jnp.broadcast_to(pwi_ref[pl.ds(R - 1 - r, 1), sl], (8, LBW))
                gre[_rows8(r), sl] = gre[_rows8(r), sl] + pr * kr + pi * ki
                gim[_rows8(r), sl] = gim[_rows8(r), sl] + pr * ki - pi * kr
                return carry
            lax.fori_loop(0, R, fix, 0)
        dup = []
        for blk in range(4):
            s4 = slice(512 * blk, 512 * (blk + 1))
            s1 = slice(128 * blk, 128 * (blk + 1))
            grb = gre[:, s4].astype(bf16)
            gib = gim[:, s4].astype(bf16)
            dup.append(_dot_nt(grb, bre_ref[blk]) + _dot_nt(gib, bim_ref[blk]))
            dbre_ref[blk] += _dot_tn(u[:, s1], grb)
            dbim_ref[blk] += _dot_tn(u[:, s1], gib)
            dcre_ref[blk] += _dot_tn(sre_ref[:, s4].astype(bf16), dyb[:, s1])
            dcim_ref[blk] -= _dot_tn(sim_ref[:, s4].astype(bf16), dyb[:, s1])
        du_ref[...] = (jnp.concatenate(dup, axis=1) + d_ref[...] * d_y0).astype(bf16)
        for lb in range(NS // LBW):
            sl = slice(lb * LBW, (lb + 1) * LBW)
            g0r, g0i = gre[0:8, sl], gim[0:8, sl]
            p0r, p0i = cinr_ref[:, sl], cini_ref[:, sl]
            acc0 = (g0r * p0r + g0i * p0i, g0i * p0r - g0r * p0i)
            def dacc(r, carry, sl=sl):
                xr, xi = carry
                gr, gi = gre[_rows8(r), sl], gim[_rows8(r), sl]
                pr, pi = sre_ref[_rows8(r - 1), sl], sim_ref[_rows8(r - 1), sl]
                return xr + gr * pr + gi * pi, xi + gi * pr - gr * pi
            xr, xi = lax.fori_loop(1, R, dacc, acc0)
            dar_ref[:, sl] += xr
            dai_ref[:, sl] += xi

    rev = lambda w, cidx: pl.BlockSpec((TC, w), lambda i, cidx=cidx: (nc - 1 - i, cidx))
    one = lambda shape: pl.BlockSpec(shape, lambda i: (0,) * len(shape))
    return pl.pallas_call(
        body, grid=(nc,),
        in_specs=[rev(SW, 0), rev(SW, 0), rev(SW, 6), rev(SW, 7), rev(NS, 0), rev(NS, 0),
                  pl.BlockSpec((8, NS), lambda i: (nc - 1 - i, 0)), pl.BlockSpec((8, NS), lambda i: (nc - 1 - i, 0)),
                  one((4, 128, 512)), one((4, 128, 512)), one((4, 512, 128)), one((4, 512, 128)),
                  one((1, NS)), one((1, NS)), one((R, NS)), one((R, NS)),
                  one((1, SW)), one((SW, SW)), one((1, SW))],
        out_specs=[rev(SW, 0), rev(SW, 0),
                   one((4, 128, 512)), one((4, 128, 512)), one((4, 512, 128)), one((4, 512, 128)),
                   one((1, SW)), one((8, NS)), one((8, NS)), one((SW, SW)), one((1, SW))],
        out_shape=[jax.ShapeDtypeStruct((L, SW), bf16), jax.ShapeDtypeStruct((L, SW), bf16),
                   jax.ShapeDtypeStruct((4, 128, 512), f32), jax.ShapeDtypeStruct((4, 128, 512), f32),
                   jax.ShapeDtypeStruct((4, 512, 128), f32), jax.ShapeDtypeStruct((4, 512, 128), f32),
                   jax.ShapeDtypeStruct((1, SW), f32), jax.ShapeDtypeStruct((8, NS), f32),
                   jax.ShapeDtypeStruct((8, NS), f32), jax.ShapeDtypeStruct((SW, SW), f32),
                   jax.ShapeDtypeStruct((1, SW), f32)],
        scratch_shapes=[pltpu.VMEM((TC, NS), f32), pltpu.VMEM((TC, NS), f32), pltpu.VMEM((8, NS), f32),
                        pltpu.VMEM((8, NS), f32), pltpu.VMEM((8, NS), f32), pltpu.VMEM((8, NS), f32)],
        name="ssm_bwd", compiler_params=_cp("arbitrary"))(
            d_bin, y0, proj, proj, sre, sim, cinr, cini, bbt_re, bbt_im, ct_re, ct_im,
            a_re, a_im, apow_re, apow_im, dvec, wglu, bglu)


def _conv_bwd(d_ain, cu1, proj, cw, lng, lnb):
    L = cu1.shape[0]
    nc = L // TC
    def body(dain_ref, cu1_ref, ca_ref, cb_ref, zc_ref, cah_ref, cbh_ref, w_ref, g_ref, bb_ref,
             dca_ref, dcb_ref, dzc_ref, dw_ref, dbias_ref, dlng_ref, dlnb_ref, dbuf, ebuf, prev, nxt, dcu0):
        i = pl.program_id(0)
        @pl.when(i == 0)
        def _():
            dw_ref[...] = jnp.zeros_like(dw_ref)
            dbias_ref[...] = jnp.zeros_like(dbias_ref)
            dlng_ref[...] = jnp.zeros_like(dlng_ref)
            dlnb_ref[...] = jnp.zeros_like(dlnb_ref)
            nxt[...] = jnp.zeros_like(nxt)
        def lnb(s, carry):
            rows = pl.ds(pl.multiple_of(s * 32, 32), 32)
            dain = dain_ref[rows, :].astype(f32)
            c1 = cu1_ref[rows, :].astype(f32)
            zc = zc_ref[rows, :].astype(f32)
            xc = c1 - jnp.mean(c1, axis=-1, keepdims=True)
            var = jnp.mean(xc * xc, axis=-1, keepdims=True)
            rstd = lax.rsqrt(var + LN_EPS)
            xh = xc * rstd
            ln = xh * g_ref[...] + bb_ref[...]
            sl_ = _sig(ln)
            sz = _sig(zc)
            dzc_ref[rows, :] = (dain * (ln * sl_) * (sz * (1.0 + zc * (1.0 - sz)))).astype(bf16)
            d_ln = dain * (zc * sz) * (sl_ * (1.0 + ln * (1.0 - sl_)))
            dlng_ref[...] += jnp.sum(d_ln * xh, axis=0, keepdims=True)
            dlnb_ref[...] += jnp.sum(d_ln, axis=0, keepdims=True)
            dxh = d_ln * g_ref[...]
            d_c1 = rstd * (dxh - jnp.mean(dxh, axis=-1, keepdims=True)
                           - xh * jnp.mean(dxh * xh, axis=-1, keepdims=True))
            dbias_ref[...] += jnp.sum(d_c1, axis=0, keepdims=True)
            _put_blocked(dbuf, pl.multiple_of(s * 32, 32), 32, d_c1)
            _put_blocked(ebuf, pl.multiple_of(NH * 8 + s * 32, 32), 32,
                         ca_ref[rows, :].astype(f32) * _sig(cb_ref[rows, :].astype(f32)))
            return carry
        lax.fori_loop(0, TC // 32, lnb, 0)
        sub = lax.broadcasted_iota(jnp.int32, (8, 128), 0)
        def after(p, carry):
            for lb in range(NLB):
                cur = dbuf[lb, _rows8(p), :]
                dbuf[lb, _rows8(R + p), :] = jnp.where(sub == 7, pltpu.roll(nxt[lb, _rows8(p), :], 7, 0),
                                                       pltpu.roll(cur, 7, 0))
            return carry
        lax.fori_loop(0, NH, after, 0)
        nxt[...] = dbuf[:, 0:NH * 8, :]
        def before(s, carry):
            rows = pl.ds(pl.multiple_of(s * 64, 64), 64)
            v = cah_ref[rows, :].astype(f32) * _sig(cbh_ref[rows, :].astype(f32))
            _put_blocked(prev, pl.multiple_of(s * 64, 64), 64, jnp.where(i == nc - 1, jnp.zeros_like(v), v))
            return carry
        lax.fori_loop(0, NH * 8 // 64, before, 0)
        _fill_before(ebuf, prev)
        for lb in range(NLB):
            sl = slice(lb * 128, (lb + 1) * 128)
            wk = [jnp.broadcast_to(w_ref[k:k + 1, sl], (8, 128)) for k in range(KS)]
            def tap(q, carry, lb=lb, wk=wk):
                r = q * RPI
                for j, o in enumerate(_fir(dbuf, lb, r, wk, None, True)):
                    dcu0[lb, _rows8(r + j), :] = o
                return carry
            lax.fori_loop(0, R // RPI, tap, 0)
            def wgrad(q, accs, lb=lb):
                r = q * RPI
                dvs = dbuf[lb, pl.ds(pl.multiple_of(r * 8, 8), RPI * 8), :]
                win = ebuf[lb, pl.ds(pl.multiple_of((r + (NH - KS + 1)) * 8, 8), (KS + RPI - 1) * 8), :]
                accs = list(accs)
                for j in range(RPI):
                    dv = dvs[8 * j:8 * j + 8, :]
                    for k in range(KS):
                        accs[k] = accs[k] + dv * win[8 * (j + k):8 * (j + k) + 8, :]
                return tuple(accs)
            accs = lax.fori_loop(0, R // RPI, wgrad, tuple(jnp.zeros((8, 128), f32) for _ in range(KS)))
            for k in range(KS):
                dw_ref[k, :, sl] += accs[k]
        def glub(s, carry):
            rows = pl.ds(pl.multiple_of(s * 64, 64), 64)
            d0 = _get_blocked(dcu0, pl.multiple_of(s * 64, 64), 64)
            ca = ca_ref[rows, :].astype(f32)
            sb = _sig(cb_ref[rows, :].astype(f32))
            dca_ref[rows, :] = (d0 * sb).astype(bf16)
            dcb_ref[rows, :] = (d0 * ca * sb * (1.0 - sb)).astype(bf16)
            return carry
        lax.fori_loop(0, TC // 64, glub, 0)

    hrows = NH * 8
    per = TC // hrows
    rev = lambda cidx: pl.BlockSpec((TC, CW), lambda i, cidx=cidx: (nc - 1 - i, cidx))
    halo = lambda cidx: pl.BlockSpec((hrows, CW), lambda i, cidx=cidx: (jnp.maximum((nc - 1 - i) * per - 1, 0), cidx))
    one = lambda shape: pl.BlockSpec(shape, lambda i: (0,) * len(shape))
    return pl.pallas_call(
        body, grid=(nc,),
        in_specs=[rev(0), rev(0), rev(0), rev(1), rev(2), halo(0), halo(1), one((32, CW)), one((1, CW)), one((1, CW))],
        out_specs=[rev(0), rev(0), rev(0), one((32, 8, CW)), one((1, CW)), one((1, CW)), one((1, CW))],
        out_shape=[jax.ShapeDtypeStruct((L, CW), bf16), jax.ShapeDtypeStruct((L, CW), bf16),
                   jax.ShapeDtypeStruct((L, CW), bf16), jax.ShapeDtypeStruct((32, 8, CW), f32),
                   jax.ShapeDtypeStruct((1, CW), f32), jax.ShapeDtypeStruct((1, CW), f32),
                   jax.ShapeDtypeStruct((1, CW), f32)],
        scratch_shapes=[pltpu.VMEM((NLB, (R + NH) * 8, 128), f32), pltpu.VMEM((NLB, (NH + R) * 8, 128), f32),
                        pltpu.VMEM((NLB, hrows, 128), f32), pltpu.VMEM((NLB, hrows, 128), f32),
                        pltpu.VMEM((NLB, TC, 128), f32)],
        name="conv_bwd", compiler_params=_cp("arbitrary"))(d_ain, cu1, proj, proj, proj, proj, proj, cw, lng, lnb)


def _win_grad(h, dproj):
    L = h.shape[0]
    tm = 512
    def body(h_ref, d_ref, o_ref):
        @pl.when(pl.program_id(1) == 0)
        def _():
            o_ref[...] = jnp.zeros_like(o_ref)
        o_ref[0] += _dot_tn(h_ref[...], d_ref[...])
    return pl.pallas_call(
        body, grid=(NCHIP, L // tm),
        in_specs=[pl.BlockSpec((tm, D), lambda j, i: (i, 0)), pl.BlockSpec((tm, SHARD_W), lambda j, i: (i, j))],
        out_specs=pl.BlockSpec((1, D, SHARD_W), lambda j, i: (j, 0, 0)),
        out_shape=jax.ShapeDtypeStruct((NCHIP, D, SHARD_W), f32),
        name="win_grad", compiler_params=_cp("arbitrary", "arbitrary"))(h, dproj)


def _x_grad(dproj, w_in, x, gx0, g_pre):
    L = x.shape[0]
    tm = 256
    def body(d_ref, w_ref, x_ref, gx_ref, g_ref, o_ref, dg_ref):
        @pl.when(pl.program_id(0) == 0)
        def _():
            dg_ref[...] = jnp.zeros_like(dg_ref)
        dh = _dot_nt(d_ref[:, 0:SHARD_W], w_ref[0])
        for j in range(1, NCHIP):
            dh = dh + _dot_nt(d_ref[:, j * SHARD_W:(j + 1) * SHARD_W], w_ref[j])
        xt = x_ref[...]
        r = lax.rsqrt(jnp.mean(xt * xt, axis=-1, keepdims=True) + RMS_EPS)
        xn = xt * r
        dg_ref[...] += jnp.sum(dh * xn, axis=0, keepdims=True)
        dxn = dh * g_ref[...]
        o_ref[...] = gx_ref[...] + r * (dxn - xn * jnp.mean(dxn * xn, axis=-1, keepdims=True))
    return pl.pallas_call(
        body, grid=(L // tm,),
        in_specs=[pl.BlockSpec((tm, IN_W), lambda i: (i, 0)),
                  pl.BlockSpec((NCHIP, D, SHARD_W), lambda i: (0, 0, 0), pipeline_mode=pl.Buffered(1)),
                  pl.BlockSpec((tm, D), lambda i: (i, 0)), pl.BlockSpec((tm, D), lambda i: (i, 0)), _full((1, D))],
        out_specs=[pl.BlockSpec((tm, D), lambda i: (i, 0)), _full((1, D))],
        out_shape=[jax.ShapeDtypeStruct((L, D), f32), jax.ShapeDtypeStruct((1, D), f32)],
        name="x_grad", compiler_params=_cp("arbitrary"))(dproj, w_in, x, gx0, g_pre)


def _pair_sum(c_arr, ga, ra, gb, rb, gs, rs):
    def body(c_ref, ga_ref, ra_ref, gb_ref, rb_ref, gs_ref, rs_ref, pa_ref, pb_ref, ps_ref):
        pa_ref[...] = (ga_ref[...] + ra_ref[...]).astype(bf16)
        pb_ref[...] = (gb_ref[...] + rb_ref[...]).astype(bf16)
        ps_ref[...] = gs_ref[...] + rs_ref[...]
    grid_spec = pltpu.PrefetchScalarGridSpec(
        num_scalar_prefetch=1, grid=(NCHIP,),
        in_specs=[pl.BlockSpec((1, D // 2, SHARD_W), lambda j, c: (j, c[0], 0)),
                  pl.BlockSpec((1, D // 2, SHARD_W), lambda j, c: (j, 0, 0)),
                  pl.BlockSpec((1, REST_ROWS // 2, 1024), lambda j, c: (j, c[0], 0)),
                  pl.BlockSpec((1, REST_ROWS // 2, 1024), lambda j, c: (j, 0, 0)),
                  pl.BlockSpec((SMALL_ROWS, 128), lambda j, c: (0, 0)),
                  pl.BlockSpec((SMALL_ROWS, 128), lambda j, c: (0, 0))],
        out_specs=[pl.BlockSpec((1, D // 2, SHARD_W), lambda j, c: (j, 0, 0)),
                   pl.BlockSpec((1, REST_ROWS // 2, 1024), lambda j, c: (j, 0, 0)),
                   pl.BlockSpec((SMALL_ROWS, 128), lambda j, c: (0, 0))])
    return pl.pallas_call(
        body, grid_spec=grid_spec,
        out_shape=[jax.ShapeDtypeStruct((NCHIP, D // 2, SHARD_W), bf16),
                   jax.ShapeDtypeStruct((NCHIP, REST_ROWS // 2, 1024), bf16),
                   jax.ShapeDtypeStruct((SMALL_ROWS, 128), f32)],
        name="pair_sum", compiler_params=_cp("arbitrary"))(c_arr, ga, ra, gb, rb, gs, rs)


def _chip_sum(qa, qb, qs):
    nt = 4
    def body(qa_ref, qb_ref, qs_ref, fa_ref, fb_ref, fs_ref):
        for q_ref, f_ref in ((qa_ref, fa_ref), (qb_ref, fb_ref), (qs_ref, fs_ref)):
            acc = q_ref[0].astype(f32)
            for j in range(1, NCHIP):
                acc = acc + q_ref[j].astype(f32)
            f_ref[...] = acc
    ra, rb, rs = D // 2 // nt, REST_ROWS // 2 // nt, SMALL_ROWS // nt
    return pl.pallas_call(
        body, grid=(nt,),
        in_specs=[pl.BlockSpec((NCHIP, ra, SHARD_W), lambda i: (0, i, 0)),
                  pl.BlockSpec((NCHIP, rb, 1024), lambda i: (0, i, 0)),
                  pl.BlockSpec((NCHIP, rs, 128), lambda i: (0, i, 0))],
        out_specs=[pl.BlockSpec((ra, SHARD_W), lambda i: (i, 0)), pl.BlockSpec((rb, 1024), lambda i: (i, 0)),
                   pl.BlockSpec((rs, 128), lambda i: (i, 0))],
        out_shape=[jax.ShapeDtypeStruct((D // 2, SHARD_W), f32), jax.ShapeDtypeStruct((REST_ROWS // 2, 1024), f32),
                   jax.ShapeDtypeStruct((SMALL_ROWS, 128), f32)],
        name="chip_sum", compiler_params=_cp("arbitrary"))(qa, qb, qs)


def _adamw_math(w, g, m, v):
    m2 = B1 * m + (1.0 - B1) * g
    v2 = B2 * v + (1.0 - B2) * (g * g)
    m_hat = m2 / (1.0 - B1 ** STEP)
    v_hat = v2 / (1.0 - B2 ** STEP)
    delta = -LR * (m_hat / (jnp.sqrt(v_hat) + EPS) + WD * w)
    return delta, m2, v2


def _adamw(name, w, g, m, v):
    rows, cols = w.shape
    tm = rows if rows <= 256 else (256 if rows % 256 == 0 else 128)
    assert rows % tm == 0
    def body(w_ref, g_ref, m_ref, v_ref, d_ref, m2_ref, v2_ref):
        d, m2, v2 = _adamw_math(w_ref[...], g_ref[...], m_ref[...], v_ref[...])
        d_ref[...] = d
        m2_ref[...] = m2
        v2_ref[...] = v2
    spec = pl.BlockSpec((tm, cols), lambda i: (i, 0))
    shp = jax.ShapeDtypeStruct((rows, cols), f32)
    return pl.pallas_call(
        body, grid=(rows // tm,), in_specs=[spec] * 4, out_specs=[spec] * 3, out_shape=[shp] * 3,
        name=name, compiler_params=_cp("arbitrary"))(w, g, m, v)


_ANY = pl.BlockSpec(memory_space=pl.ANY)


def _chunks(rows, parts):
    step = rows // parts
    assert step * parts == rows and step % 16 == 0
    return [(i * step, step) for i in range(parts)]


def _place():
    x, y, c = lax.axis_index("x"), lax.axis_index("y"), lax.axis_index("c")
    chips = [(1 - x, y), (x, 1 - y), (1 - x, 1 - y)]
    return x, y, c, chips


def _gather_weights(win_s, rest_s):
    segs = [(0, D // 2, r0, n) for r0, n in _chunks(D // 2, 4)] + \
           [(1, REST_ROWS // 2, r0, n) for r0, n in _chunks(REST_ROWS // 2, 2)]
    ns = len(segs)
    def body(a_ref, b_ref, oa_ref, ob_ref, send_sems, recv_sems, local_sems):
        x, y, c, chips = _place()
        k = 2 * x + y
        sibling = (x, y, 1 - c)
        ins, outs = (a_ref, b_ref), (oa_ref, ob_ref)

        def dst(which, half, chip, pc, r0, n):
            return outs[which].at[chip, pl.ds(pc * half + r0, n), :]

        def rcopy(i, src, dst_ref, to):
            return pltpu.make_async_remote_copy(src_ref=src, dst_ref=dst_ref, send_sem=send_sems.at[i],
                                                recv_sem=recv_sems.at[i], device_id=to, device_id_type=MESH)

        own = [pltpu.make_async_copy(ins[w], outs[w].at[k], local_sems.at[w]) for w in range(2)]
        for cp in own:
            cp.start()
        first = []
        for j, chip in enumerate(chips):
            for s, (w, half, r0, n) in enumerate(segs):
                first.append(rcopy(j * ns + s, ins[w].at[pl.ds(c * half + r0, n), :], dst(w, half, k, c, r0, n),
                                   (*chip, c)))
        for cp in first:
            cp.start()
        passed = []
        for j, chip in enumerate(chips):
            cj = 2 * chip[0] + chip[1]
            for s, (w, half, r0, n) in enumerate(segs):
                landed = dst(w, half, cj, c, r0, n)
                rcopy(j * ns + s, landed, landed, (x, y, c)).wait_recv()
                fwd = rcopy(3 * ns + j * ns + s, landed, landed, sibling)
                fwd.start()
                passed.append(fwd)
        for j, chip in enumerate(chips):
            cj = 2 * chip[0] + chip[1]
            for s, (w, half, r0, n) in enumerate(segs):
                theirs = dst(w, half, cj, 1 - c, r0, n)
                rcopy(3 * ns + j * ns + s, theirs, theirs, (x, y, c)).wait_recv()
        for cp in first + passed:
            cp.wait_send()
        for cp in own:
            cp.wait()

    return pl.pallas_call(
        body, in_specs=[_ANY, _ANY], out_specs=[_ANY, _ANY],
        out_shape=[jax.ShapeDtypeStruct((NCHIP, D, SHARD_W), bf16), jax.ShapeDtypeStruct((NCHIP, REST_ROWS, 1024), bf16)],
        scratch_shapes=[pltpu.SemaphoreType.DMA((6 * ns,)), pltpu.SemaphoreType.DMA((6 * ns,)),
                        pltpu.SemaphoreType.DMA((2,))],
        name="gather_weights")(win_s, rest_s)


def _pair_exchange(ga, gb, gs):
    ha, hb = D // 2, REST_ROWS // 2
    def body(a_ref, b_ref, s_ref, ra_ref, rb_ref, rs_ref, send_sems, recv_sems):
        x, y, c, _ = _place()
        sibling = (x, y, 1 - c)
        pieces = []
        for j in range(NCHIP):
            for r0, n in _chunks(ha, 4):
                pieces.append((a_ref.at[j, pl.ds((1 - c) * ha + r0, n), :], ra_ref.at[j, pl.ds(r0, n), :]))
            for r0, n in _chunks(hb, 2):
                pieces.append((b_ref.at[j, pl.ds((1 - c) * hb + r0, n), :], rb_ref.at[j, pl.ds(r0, n), :]))
        pieces.append((s_ref, rs_ref))
        copies = [pltpu.make_async_remote_copy(src_ref=s, dst_ref=d, send_sem=send_sems.at[i], recv_sem=recv_sems.at[i],
                                               device_id=sibling, device_id_type=MESH)
                  for i, (s, d) in enumerate(pieces)]
        for cp in copies:
            cp.start()
        for cp in copies:
            cp.wait_recv()
        for cp in copies:
            cp.wait_send()

    n = NCHIP * 6 + 1
    return pl.pallas_call(
        body, in_specs=[_ANY, _ANY, _ANY], out_specs=[_ANY, _ANY, _ANY],
        out_shape=[jax.ShapeDtypeStruct((NCHIP, ha, SHARD_W), f32), jax.ShapeDtypeStruct((NCHIP, hb, 1024), f32),
                   jax.ShapeDtypeStruct((SMALL_ROWS, 128), f32)],
        scratch_shapes=[pltpu.SemaphoreType.DMA((n,)), pltpu.SemaphoreType.DMA((n,))],
        name="pair_exchange")(ga, gb, gs)


def _chip_exchange(pa, pb, ps):
    ha, hb = D // 2, REST_ROWS // 2
    def body(a_ref, b_ref, s_ref, qa_ref, qb_ref, qs_ref, send_sems, recv_sems, local_sems):
        x, y, c, chips = _place()
        k = 2 * x + y
        own = [pltpu.make_async_copy(a_ref.at[k], qa_ref.at[k], local_sems.at[0]),
               pltpu.make_async_copy(b_ref.at[k], qb_ref.at[k], local_sems.at[1]),
               pltpu.make_async_copy(s_ref, qs_ref.at[k], local_sems.at[2])]
        for cp in own:
            cp.start()
        copies = []
        for j, chip in enumerate(chips):
            cj = 2 * chip[0] + chip[1]
            pieces = [(a_ref.at[cj, pl.ds(r0, n), :], qa_ref.at[k, pl.ds(r0, n), :]) for r0, n in _chunks(ha, 2)]
            pieces += [(b_ref.at[cj], qb_ref.at[k]), (s_ref, qs_ref.at[k])]
            for s, (src, dst_ref) in enumerate(pieces):
                copies.append(pltpu.make_async_remote_copy(
                    src_ref=src, dst_ref=dst_ref, send_sem=send_sems.at[4 * j + s], recv_sem=recv_sems.at[4 * j + s],
                    device_id=(*chip, c), device_id_type=MESH))
        for cp in copies:
            cp.start()
        for cp in copies:
            cp.wait_recv()
        for cp in copies:
            cp.wait_send()
        for cp in own:
            cp.wait()

    return pl.pallas_call(
        body, in_specs=[_ANY, _ANY, _ANY], out_specs=[_ANY, _ANY, _ANY],
        out_shape=[jax.ShapeDtypeStruct((NCHIP, ha, SHARD_W), bf16), jax.ShapeDtypeStruct((NCHIP, hb, 1024), bf16),
                   jax.ShapeDtypeStruct((NCHIP, SMALL_ROWS, 128), f32)],
        scratch_shapes=[pltpu.SemaphoreType.DMA((12,)), pltpu.SemaphoreType.DMA((12,)), pltpu.SemaphoreType.DMA((3,))],
        name="chip_exchange")(pa, pb, ps)


def _sibling_exchange(fa, fb):
    ha, hb = D // 2, REST_ROWS // 2
    def body(a_ref, b_ref, oa_ref, ob_ref, send_sems, recv_sems, local_sems):
        x, y, c, _ = _place()
        own = [pltpu.make_async_copy(a_ref, oa_ref.at[c], local_sems.at[0]),
               pltpu.make_async_copy(b_ref, ob_ref.at[c], local_sems.at[1])]
        for cp in own:
            cp.start()
        pieces = [(a_ref.at[pl.ds(r0, n), :], oa_ref.at[c, pl.ds(r0, n), :]) for r0, n in _chunks(ha, 4)]
        pieces += [(b_ref.at[pl.ds(r0, n), :], ob_ref.at[c, pl.ds(r0, n), :]) for r0, n in _chunks(hb, 2)]
        copies = [pltpu.make_async_remote_copy(src_ref=s, dst_ref=d, send_sem=send_sems.at[i], recv_sem=recv_sems.at[i],
                                               device_id=(x, y, 1 - c), device_id_type=MESH)
                  for i, (s, d) in enumerate(pieces)]
        for cp in copies:
            cp.start()
        for cp in copies:
            cp.wait_recv()
        for cp in copies:
            cp.wait_send()
        for cp in own:
            cp.wait()

    return pl.pallas_call(
        body, in_specs=[_ANY, _ANY], out_specs=[_ANY, _ANY],
        out_shape=[jax.ShapeDtypeStruct((2, ha, SHARD_W), f32), jax.ShapeDtypeStruct((2, hb, 1024), f32)],
        scratch_shapes=[pltpu.SemaphoreType.DMA((6,)), pltpu.SemaphoreType.DMA((6,)), pltpu.SemaphoreType.DMA((2,))],
        name="sibling_exchange")(fa, fb)


_REST_ROWS = (256, 256, 64, 128)
_CONV_PAD = 8192


def _pack_rest(mats, conv_rows, dtype):
    parts = [mats[0], mats[1], mats[2].reshape(64, 1024), mats[3].reshape(128, 1024)]
    parts = [p.astype(dtype) for p in parts] + [conv_rows]
    used = sum(p.shape[0] for p in parts)
    parts.append(jnp.zeros((REST_ROWS - used, 1024), dtype))
    return jnp.concatenate(parts, axis=0)


def _pack_rest_weights(mats, conv_w_s):
    flat = jnp.pad(conv_w_s.reshape(-1), (0, _CONV_PAD - KS * 256))
    return _pack_rest(mats, lax.bitcast_convert_type(flat, bf16).reshape(16, 1024), bf16)


def _pack_rest_grads(mats, conv_w_s):
    flat = jnp.pad(conv_w_s.reshape(-1), (0, _CONV_PAD - KS * 256))
    return _pack_rest(mats, flat.reshape(8, 1024), f32)


def _split_rest(p, conv_rows):
    o = 0
    out = []
    for rows in _REST_ROWS + (conv_rows,):
        out.append(p[..., o:o + rows, :])
        o += rows
    return out


_SMALL = (("pre_norm_gain", (1, 1024)), ("conv_b", (1, 1024)), ("conv_ln_gain", (1, 1024)), ("conv_ln_bias", (1, 1024)),
          ("ssm_lambda_re", (1, 32, 64)), ("ssm_lambda_im", (1, 32, 64)), ("ssm_log_dt", (1, 32)),
          ("ssm_b_re", (1, 32, 64, 16)), ("ssm_b_im", (1, 32, 64, 16)), ("ssm_c_re", (1, 32, 16, 64)),
          ("ssm_c_im", (1, 32, 16, 64)), ("ssm_d", (1, 32, 16)), ("b_ssm_glu", (1, 512)), ("post_norm_gain", (1, 1024)))


def _pack_small(vals, extra=None):
    rows = []
    for v in list(vals) + ([extra] if extra is not None else []):
        flat = v.reshape(-1).astype(f32)
        n = -(-flat.shape[0] // 1024) * 1024
        rows.append(jnp.pad(flat, (0, n - flat.shape[0])).reshape(-1, 128))
    used = sum(r.shape[0] for r in rows)
    rows.append(jnp.zeros((SMALL_ROWS - used, 128), f32))
    return jnp.concatenate(rows, axis=0)


def _unpack_small(p):
    o = 0
    out = []
    for _, shape in _SMALL:
        n = int(np.prod(shape))
        nr = -(-n // 1024) * 8
        out.append(p[o:o + nr].reshape(-1)[:n].reshape(shape))
        o += nr
    return out, p[o, 0]


def _discretize(lam_re, lam_im, log_dt, b_re, b_im):
    dt = jnp.exp(log_dt)[:, None]
    mag = jnp.exp(lam_re * dt)
    ar = mag * jnp.cos(lam_im * dt)
    ai = mag * jnp.sin(lam_im * dt)
    den = lam_re * lam_re + lam_im * lam_im
    zr = ((ar - 1.0) * lam_re + ai * lam_im) / den
    zi = (ai * lam_re - (ar - 1.0) * lam_im) / den
    bbr = zr[..., None] * b_re - zi[..., None] * b_im
    bbi = zr[..., None] * b_im + zi[..., None] * b_re
    return ar, ai, bbr, bbi


_EYE8 = np.eye(8, dtype=np.float32)


def _bbt_blocks(bb):
    v = bb.reshape(4, 8, PST, H).transpose(0, 1, 3, 2)
    return jnp.einsum("bghp,gk->bghkp", v, _EYE8).reshape(4, 128, 512)


def _bbt_unblock(m):
    v = jnp.einsum("bghkp,gk->bghp", m.reshape(4, 8, H, 8, PST), _EYE8)
    return v.transpose(0, 1, 3, 2).reshape(G, PST, H)


def _ct_blocks(cc):
    v = cc.reshape(4, 8, H, PST)
    return jnp.einsum("bghp,gk->bgpkh", v, _EYE8).reshape(4, 512, 128)


def _ct_unblock(m):
    v = jnp.einsum("bgpkh,gk->bghp", m.reshape(4, 8, PST, 8, H), _EYE8)
    return v.reshape(G, H, PST)


def _interleave(a):
    L, C = a.shape
    return a.reshape(L // TC, 8, R, C).transpose(0, 2, 1, 3).reshape(L, C)


def _deinterleave(a):
    L, C = a.shape
    return a.reshape(L // TC, R, 8, C).transpose(0, 2, 1, 3).reshape(L, C)


def _local_step(x, tgt, w_in, conv_w, w_co, w_glu, w_so, w_out, small):
    (g_pre, conv_b, ln_g, ln_b, lam_re, lam_im, log_dt, b_re, b_im, c_re, c_im, dvec, b_glu, g_post) = small
    lam_re, lam_im, log_dt = lam_re[0], lam_im[0], log_dt[0]
    b_re, b_im, c_re, c_im = b_re[0], b_im[0], c_re[0], c_im[0]
    (ar, ai, bbr, bbi), disc_vjp = jax.vjp(_discretize, lam_re, lam_im, log_dt, b_re, b_im)
    a_re = ar.reshape(1, NS)
    a_im = ai.reshape(1, NS)
    dt = jnp.exp(log_dt)[:, None]
    steps = jnp.arange(1, R + 1, dtype=f32)[:, None, None]
    apow_re = (jnp.exp(steps * (lam_re * dt)) * jnp.cos(steps * (lam_im * dt))).reshape(R, NS)
    apow_im = (jnp.exp(steps * (lam_re * dt)) * jnp.sin(steps * (lam_im * dt))).reshape(R, NS)
    bbt_re, bbt_im = _bbt_blocks(bbr).astype(bf16), _bbt_blocks(bbi).astype(bf16)
    ct_re, ct_im = _ct_blocks(c_re).astype(bf16), _ct_blocks(c_im).astype(bf16)
    d_row = dvec.reshape(1, SW)
    cw32 = jnp.pad(conv_w, ((0, 1), (0, 0)))
    xi = _interleave(x)
    ti = _interleave(tgt)

    h = _prenorm(xi, g_pre)
    proj = _proj_fwd(h, w_in)
    cu1, a_in = _conv_fwd(proj, cw32, conv_b, ln_g, ln_b)
    y0, b_in, sre, sim, cinr, cini = _ssm_fwd(proj, bbt_re, bbt_im, ct_re, ct_im, a_re, a_im,
                                              apow_re, apow_im, d_row, w_glu, b_glu)
    gx0, d_ain, d_bin, d_gc, d_gs, dw_out, dw_co, dw_so, dg_post, loss = _tail(
        a_in, b_in, proj, xi, ti, w_co, w_so, w_out, g_post)
    (d_u, d_zs, dbbt_re, dbbt_im, dct_re, dct_im, dd, dar8, dai8, dw_glu, db_glu) = _ssm_bwd(
        d_bin, y0, proj, sre, sim, cinr, cini, bbt_re, bbt_im, ct_re, ct_im,
        a_re, a_im, apow_re, apow_im, d_row, w_glu, b_glu)
    d_ca, d_cb, d_zc, dcw8, d_convb, d_lng, d_lnb = _conv_bwd(d_ain, cu1, proj, cw32, ln_g, ln_b)
    dproj = jnp.concatenate([d_ca, d_cb, d_zc, d_u, d_zs, d_gc, d_gs], axis=1)
    dw_in = _win_grad(h, dproj)
    gxi, dg_pre = _x_grad(dproj, w_in, xi, gx0, g_pre)
    grad_x = _deinterleave(gxi)

    d_ar = jnp.sum(dar8, axis=0).reshape(G, PST)
    d_ai = jnp.sum(dai8, axis=0).reshape(G, PST)
    d_lre, d_lim, d_ldt, d_bre, d_bim = disc_vjp((d_ar, d_ai, _bbt_unblock(dbbt_re), _bbt_unblock(dbbt_im)))
    d_conv_w = jnp.sum(dcw8, axis=1)[:KS]
    small_grads = [dg_pre, d_convb, d_lng, d_lnb, d_lre[None], d_lim[None], d_ldt[None], d_bre[None], d_bim[None],
                   _ct_unblock(dct_re)[None], _ct_unblock(dct_im)[None], dd.reshape(1, G, H), db_glu, dg_post]
    return loss[0, 0], grad_x, (dw_in, dw_co, dw_out, dw_glu, dw_so, d_conv_w), small_grads


def kernel(x, pre_norm_gain, w_in, conv_w, conv_b, conv_ln_gain, conv_ln_bias, w_conv_out, ssm_lambda_re, ssm_lambda_im, ssm_log_dt, ssm_b_re, ssm_b_im, ssm_c_re, ssm_c_im, ssm_d, w_ssm_glu, b_ssm_glu, w_ssm_out, w_out, post_norm_gain, loss_target, m_pre_norm_gain, m_w_in, m_conv_w, m_conv_b, m_conv_ln_gain, m_conv_ln_bias, m_w_conv_out, m_ssm_lambda_re, m_ssm_lambda_im, m_ssm_log_dt, m_ssm_b_re, m_ssm_b_im, m_ssm_c_re, m_ssm_c_im, m_ssm_d, m_w_ssm_glu, m_b_ssm_glu, m_w_ssm_out, m_w_out, m_post_norm_gain, v_pre_norm_gain, v_w_in, v_conv_w, v_conv_b, v_conv_ln_gain, v_conv_ln_bias, v_w_conv_out, v_ssm_lambda_re, v_ssm_lambda_im, v_ssm_log_dt, v_ssm_b_re, v_ssm_b_im, v_ssm_c_re, v_ssm_c_im, v_ssm_d, v_w_ssm_glu, v_b_ssm_glu, v_w_ssm_out, v_w_out, v_post_norm_gain):
    rest_s = _pack_rest_weights((w_conv_out[0], w_out[0], w_ssm_glu[0], w_ssm_out[0]), conv_w[0])
    w_in_g, rest_g = _gather_weights(w_in[0].astype(bf16), rest_s)
    secs = _split_rest(rest_g, 16)
    w_co_f = secs[0].reshape(CW, D)
    w_out_f = secs[1].reshape(D, D)
    w_glu_f = secs[2].reshape(SW, SW)
    w_so_f = secs[3].reshape(NCHIP, SW, 256).transpose(1, 0, 2).reshape(SW, D)
    conv_w_f = lax.bitcast_convert_type(secs[4].reshape(NCHIP, _CONV_PAD, 2), f32)[:, :KS * 256]
    conv_w_f = conv_w_f.reshape(NCHIP, KS, 256).transpose(1, 0, 2).reshape(KS, CW)

    small = (pre_norm_gain, conv_b, conv_ln_gain, conv_ln_bias, ssm_lambda_re, ssm_lambda_im, ssm_log_dt, ssm_b_re,
             ssm_b_im, ssm_c_re, ssm_c_im, ssm_d, b_ssm_glu, post_norm_gain)
    loss_part, grad_x, big_grads, small_grads = _local_step(
        x[0], loss_target[0], w_in_g, conv_w_f, w_co_f, w_glu_f, w_so_f, w_out_f, small)

    dw_in, dw_co, dw_out, dw_glu, dw_so, d_conv_w = big_grads
    gb = jnp.stack([
        _pack_rest_grads((dw_co[j * 256:(j + 1) * 256], dw_out[j * 256:(j + 1) * 256], dw_glu[j * 128:(j + 1) * 128],
                          dw_so[:, j * 256:(j + 1) * 256]), d_conv_w[:, j * 256:(j + 1) * 256])
        for j in range(NCHIP)])
    gs = _pack_small(small_grads, extra=loss_part)
    ra, rb, rs = _pair_exchange(dw_in, gb, gs)
    c_arr = lax.axis_index("c").astype(jnp.int32).reshape(1)
    pa, pb, ps = _pair_sum(c_arr, dw_in, ra, gb, rb, gs, rs)
    qa, qb, qs = _chip_exchange(pa, pb, ps)
    fa, fb, fs = _chip_sum(qa, qb, qs)
    ga2, gb2 = _sibling_exchange(fa, fb)
    g_rest = _split_rest(gb2.reshape(REST_ROWS, 1024), 8)
    g_big = (ga2.reshape(D, SHARD_W), g_rest[0], g_rest[1], g_rest[2].reshape(128, 512), g_rest[3].reshape(512, 256),
             g_rest[4].reshape(-1)[:KS * 256].reshape(KS, 256))

    big_w = (w_in[0], w_conv_out[0], w_out[0], w_ssm_glu[0], w_ssm_out[0], conv_w[0])
    big_m = (m_w_in[0], m_w_conv_out[0], m_w_out[0], m_w_ssm_glu[0], m_w_ssm_out[0], m_conv_w[0])
    big_v = (v_w_in[0], v_w_conv_out[0], v_w_out[0], v_w_ssm_glu[0], v_w_ssm_out[0], v_conv_w[0])
    big_names = ("w_in", "w_conv_out", "w_out", "w_ssm_glu", "w_ssm_out", "conv_w")
    big_out = {}
    for n, w, g, m, v in zip(big_names, big_w, g_big, big_m, big_v):
        d, m2, v2 = _adamw("adamw_" + n, w, g, m, v)
        big_out[n] = (g[None], d[None], m2[None], v2[None])

    small_m = (m_pre_norm_gain, m_conv_b, m_conv_ln_gain, m_conv_ln_bias, m_ssm_lambda_re, m_ssm_lambda_im, m_ssm_log_dt,
               m_ssm_b_re, m_ssm_b_im, m_ssm_c_re, m_ssm_c_im, m_ssm_d, m_b_ssm_glu, m_post_norm_gain)
    small_v = (v_pre_norm_gain, v_conv_b, v_conv_ln_gain, v_conv_ln_bias, v_ssm_lambda_re, v_ssm_lambda_im, v_ssm_log_dt,
               v_ssm_b_re, v_ssm_b_im, v_ssm_c_re, v_ssm_c_im, v_ssm_d, v_b_ssm_glu, v_post_norm_gain)
    sd, sm, sv = _adamw("adamw_small", _pack_small(small), fs, _pack_small(small_m), _pack_small(small_v))
    sg_l, loss = _unpack_small(fs)
    sd_l, _ = _unpack_small(sd)
    sm_l, _ = _unpack_small(sm)
    sv_l, _ = _unpack_small(sv)
    small_out = {n: (sg_l[i], sd_l[i], sm_l[i], sv_l[i]) for i, (n, _) in enumerate(_SMALL)}

    order = ("pre_norm_gain", "w_in", "conv_w", "conv_b", "conv_ln_gain", "conv_ln_bias", "w_conv_out", "ssm_lambda_re",
             "ssm_lambda_im", "ssm_log_dt", "ssm_b_re", "ssm_b_im", "ssm_c_re", "ssm_c_im", "ssm_d", "w_ssm_glu",
             "b_ssm_glu", "w_ssm_out", "w_out", "post_norm_gain")
    res = {**big_out, **small_out}
    outs = [loss, grad_x[None]]
    for k in range(4):
        outs.extend(res[n][k] for n in order)
    return tuple(outs)
```

```python
import math

import numpy as np
import jax
import jax.numpy as jnp
from jax import lax
from jax.experimental import pallas as pl
from jax.experimental.pallas import tpu as pltpu

f32 = jnp.float32
bf16 = jnp.bfloat16

D = 1024
CW = 1024
SW = 512
G = 32
H = 16
PST = 64
NS = G * PST
KS = 31
IN_W = 6144
NCHIP = 4
SHARD_W = IN_W // NCHIP
RMS_EPS = 1e-6
LN_EPS = 1e-5
LR, B1, B2, EPS, WD, STEP = 0.001, 0.9, 0.999, 1e-08, 0.01, 10
GELU_K0 = math.sqrt(2.0 / math.pi)
GELU_K1 = 0.044715

TC = 512
R = TC // 8
NH = 32
LBW = 512
REST_ROWS = 768
CONV_ROWS = 64
SMALL_ROWS = 1152
VMEM_LIMIT = 56 * 1024 * 1024
MESH = pl.DeviceIdType.MESH


def _cp(*sem):
    return pltpu.CompilerParams(dimension_semantics=tuple(sem), vmem_limit_bytes=VMEM_LIMIT)


def _sig(v):
    return 1.0 / (1.0 + jnp.exp(-v))


def _dot(a, b):
    return jnp.dot(a, b, preferred_element_type=f32)


def _dot_nt(a, b):
    return lax.dot_general(a, b, (((1,), (1,)), ((), ())), preferred_element_type=f32)


def _dot_tn(a, b):
    return lax.dot_general(a, b, (((0,), (0,)), ((), ())), preferred_element_type=f32)


def _full(shape):
    nd = len(shape)
    return pl.BlockSpec(shape, lambda *_: (0,) * nd)


def _rows8(i):
    return pl.ds(pl.multiple_of(i * 8, 8), 8)


def _prenorm(x, g_pre):
    L = x.shape[0]
    tm = 512
    def body(x_ref, g_ref, h_ref):
        xt = x_ref[...]
        r = lax.rsqrt(jnp.mean(xt * xt, axis=-1, keepdims=True) + RMS_EPS)
        h_ref[...] = (xt * r * g_ref[...]).astype(bf16)
    return pl.pallas_call(
        body, grid=(L // tm,),
        in_specs=[pl.BlockSpec((tm, D), lambda i: (i, 0)), _full((1, D))],
        out_specs=pl.BlockSpec((tm, D), lambda i: (i, 0)),
        out_shape=jax.ShapeDtypeStruct((L, D), bf16),
        name="prenorm", compiler_params=_cp("arbitrary"))(x, g_pre)


def _proj_fwd(h, w_in):
    L = h.shape[0]
    tm = 512
    def body(h_ref, w_ref, o_ref):
        o_ref[...] = _dot(h_ref[...], w_ref[0]).astype(bf16)
    return pl.pallas_call(
        body, grid=(NCHIP, L // tm),
        in_specs=[pl.BlockSpec((tm, D), lambda j, i: (i, 0)), pl.BlockSpec((1, D, SHARD_W), lambda j, i: (j, 0, 0))],
        out_specs=pl.BlockSpec((tm, SHARD_W), lambda j, i: (i, j)),
        out_shape=jax.ShapeDtypeStruct((L, IN_W), bf16),
        name="proj_fwd", compiler_params=_cp("arbitrary", "arbitrary"))(h, w_in)


NLB = CW // 128
RPI = 4


def _put_blocked(buf, row0, nrows, v):
    for lb in range(NLB):
        buf[lb, pl.ds(row0, nrows), :] = v[:, lb * 128:(lb + 1) * 128]


def _get_blocked(buf, row0, nrows):
    return jnp.concatenate([buf[lb, pl.ds(row0, nrows), :] for lb in range(NLB)], axis=1)


def _fill_before(ebuf, prev):
    sub = lax.broadcasted_iota(jnp.int32, (8, 128), 0)
    def halo(p, carry):
        for lb in range(NLB):
            cur = ebuf[lb, _rows8(R + p), :]
            ebuf[lb, _rows8(p), :] = jnp.where(sub == 0, pltpu.roll(prev[lb, _rows8(p), :], 1, 0),
                                               pltpu.roll(cur, 1, 0))
        return carry
    lax.fori_loop(0, NH, halo, 0)


def _fir(buf, lb, r, coef, first, flip):
    win = buf[lb, pl.ds(pl.multiple_of(r * 8, 8), (KS + RPI - 1) * 8), :]
    outs = []
    for i in range(RPI):
        acc = [first, None, None, None]
        for k in range(KS):
            o = i + ((KS - 1 - k) if flip else k)
            t = coef[k] * win[8 * o:8 * o + 8, :]
            acc[k % 4] = t if acc[k % 4] is None else acc[k % 4] + t
        outs.append((acc[0] + acc[1]) + (acc[2] + acc[3]))
    return outs


def _conv_fwd(proj, cw, cbias, lng, lnb):
    L = proj.shape[0]
    nc = L // TC
    def body(ca_ref, cb_ref, zc_ref, w_ref, b_ref, g_ref, bb_ref, cu1_ref, ain_ref, ebuf, prev, cacc):
        @pl.when(pl.program_id(0) == 0)
        def _():
            prev[...] = jnp.zeros_like(prev)
        def glu(s, carry):
            rows = pl.ds(pl.multiple_of(s * 64, 64), 64)
            _put_blocked(ebuf, pl.multiple_of(NH * 8 + s * 64, 64), 64,
                         ca_ref[rows, :].astype(f32) * _sig(cb_ref[rows, :].astype(f32)))
            return carry
        lax.fori_loop(0, TC // 64, glu, 0)
        _fill_before(ebuf, prev)
        prev[...] = ebuf[:, R * 8:(NH + R) * 8, :]
        for lb in range(NLB):
            sl = slice(lb * 128, (lb + 1) * 128)
            wk = [jnp.broadcast_to(w_ref[k:k + 1, sl], (8, 128)) for k in range(KS)]
            bias = jnp.broadcast_to(b_ref[:, sl], (8, 128))
            def tap(q, carry, lb=lb, wk=wk, bias=bias):
                r = q * RPI
                for i, o in enumerate(_fir(ebuf, lb, r + (NH - KS + 1), wk, bias, False)):
                    cacc[lb, _rows8(r + i), :] = o
                return carry
            lax.fori_loop(0, R // RPI, tap, 0)
        def norm(s, carry):
            rows = pl.ds(pl.multiple_of(s * 64, 64), 64)
            c1b = _get_blocked(cacc, pl.multiple_of(s * 64, 64), 64).astype(bf16)
            cu1_ref[rows, :] = c1b
            c1 = c1b.astype(f32)
            xc = c1 - jnp.mean(c1, axis=-1, keepdims=True)
            var = jnp.mean(xc * xc, axis=-1, keepdims=True)
            ln = xc * lax.rsqrt(var + LN_EPS) * g_ref[...] + bb_ref[...]
            zc = zc_ref[rows, :].astype(f32)
            ain_ref[rows, :] = ((ln * _sig(ln)) * (zc * _sig(zc))).astype(bf16)
            return carry
        lax.fori_loop(0, TC // 64, norm, 0)

    col = lambda c: pl.BlockSpec((TC, CW), lambda i, c=c: (i, c))
    return pl.pallas_call(
        body, grid=(nc,),
        in_specs=[col(0), col(1), col(2), _full((32, CW)), _full((1, CW)), _full((1, CW)), _full((1, CW))],
        out_specs=[pl.BlockSpec((TC, CW), lambda i: (i, 0)), pl.BlockSpec((TC, CW), lambda i: (i, 0))],
        out_shape=[jax.ShapeDtypeStruct((L, CW), bf16), jax.ShapeDtypeStruct((L, CW), bf16)],
        scratch_shapes=[pltpu.VMEM((NLB, (NH + R) * 8, 128), f32), pltpu.VMEM((NLB, NH * 8, 128), f32),
                        pltpu.VMEM((NLB, TC, 128), f32)],
        name="conv_fwd", compiler_params=_cp("arbitrary"))(proj, proj, proj, cw, cbias, lng, lnb)


def _gelu_parts(y0):
    t = jnp.tanh(GELU_K0 * (y0 + GELU_K1 * y0 * y0 * y0))
    return t, 0.5 * y0 * (1.0 + t)


def _ssm_fwd(proj, bbt_re, bbt_im, ct_re, ct_im, a_re, a_im, apow_re, apow_im, dvec, wglu, bglu):
    L = proj.shape[0]
    nc = L // TC
    def body(u_ref, zs_ref, bre_ref, bim_ref, cre_ref, cim_ref, are_ref, aim_ref, pwr_ref, pwi_ref,
             d_ref, wg_ref, bg_ref, y0_ref, bin_ref, sre, sim, cinr, cini, prev_re, prev_im):
        c = pl.program_id(0)
        @pl.when(c == 0)
        def _():
            prev_re[...] = jnp.zeros_like(prev_re)
            prev_im[...] = jnp.zeros_like(prev_im)
        u = u_ref[...]
        for blk in range(4):
            ub = u[:, 128 * blk:128 * (blk + 1)]
            sre[:, 512 * blk:512 * (blk + 1)] = _dot(ub, bre_ref[blk])
            sim[:, 512 * blk:512 * (blk + 1)] = _dot(ub, bim_ref[blk])
        for lb in range(NS // LBW):
            sl = slice(lb * LBW, (lb + 1) * LBW)
            ar = jnp.broadcast_to(are_ref[:, sl], (8, LBW))
            ai = jnp.broadcast_to(aim_ref[:, sl], (8, LBW))
            def step(r, carry, sl=sl, ar=ar, ai=ai):
                sr, si = carry
                nr = ar * sr - ai * si + sre[_rows8(r), sl]
                ni = ar * si + ai * sr + sim[_rows8(r), sl]
                sre[_rows8(r), sl] = nr
                sim[_rows8(r), sl] = ni
                return nr, ni
            lax.fori_loop(1, R, step, (sre[0:8, sl], sim[0:8, sl]))
        a_r = pwr_ref[R - 1:R, :]
        a_i = pwi_ref[R - 1:R, :]
        cr = prev_re[0:1, :]
        ci = prev_im[0:1, :]
        for seg in range(8):
            cinr[seg:seg + 1, :] = cr
            cini[seg:seg + 1, :] = ci
            er = sre[8 * (R - 1) + seg:8 * (R - 1) + seg + 1, :]
            ei = sim[8 * (R - 1) + seg:8 * (R - 1) + seg + 1, :]
            cr, ci = er + a_r * cr - a_i * ci, ei + a_r * ci + a_i * cr
        prev_re[0:1, :] = cr
        prev_im[0:1, :] = ci
        for lb in range(NS // LBW):
            sl = slice(lb * LBW, (lb + 1) * LBW)
            kr = cinr[:, sl]
            ki = cini[:, sl]
            def fix(r, carry, sl=sl, kr=kr, ki=ki):
                pr = jnp.broadcast_to(pwr_ref[pl.ds(r, 1), sl], (8, LBW))
                pi = jnp.broadcast_to(pwi_ref[pl.ds(r, 1), sl], (8, LBW))
                sre[_rows8(r), sl] = sre[_rows8(r), sl] + pr * kr - pi * ki
                sim[_rows8(r), sl] = sim[_rows8(r), sl] + pr * ki + pi * kr
                return carry
            lax.fori_loop(0, R, fix, 0)
        yp = []
        for blk in range(4):
            sr = sre[:, 512 * blk:512 * (blk + 1)].astype(bf16)
            si = sim[:, 512 * blk:512 * (blk + 1)].astype(bf16)
            yp.append(_dot(sr, cre_ref[blk]) - _dot(si, cim_ref[blk]))
        y0 = jnp.concatenate(yp, axis=1) + d_ref[...] * u.astype(f32)
        y0_ref[...] = y0
        _, y1 = _gelu_parts(y0)
        glu = _dot(y1.astype(bf16), wg_ref[...]) + bg_ref[...]
        y2 = y1 * _sig(glu)
        zs = zs_ref[...].astype(f32)
        bin_ref[...] = (y2 * (zs * _sig(zs))).astype(bf16)

    return pl.pallas_call(
        body, grid=(nc,),
        in_specs=[pl.BlockSpec((TC, SW), lambda c: (c, 6)), pl.BlockSpec((TC, SW), lambda c: (c, 7)),
                  _full((4, 128, 512)), _full((4, 128, 512)), _full((4, 512, 128)), _full((4, 512, 128)),
                  _full((1, NS)), _full((1, NS)), _full((R, NS)), _full((R, NS)),
                  _full((1, SW)), _full((SW, SW)), _full((1, SW))],
        out_specs=[pl.BlockSpec((TC, SW), lambda c: (c, 0)), pl.BlockSpec((TC, SW), lambda c: (c, 0)),
                   pl.BlockSpec((TC, NS), lambda c: (c, 0)), pl.BlockSpec((TC, NS), lambda c: (c, 0)),
                   pl.BlockSpec((8, NS), lambda c: (c, 0)), pl.BlockSpec((8, NS), lambda c: (c, 0))],
        out_shape=[jax.ShapeDtypeStruct((L, SW), f32), jax.ShapeDtypeStruct((L, SW), bf16),
                   jax.ShapeDtypeStruct((L, NS), f32), jax.ShapeDtypeStruct((L, NS), f32),
                   jax.ShapeDtypeStruct((nc * 8, NS), f32), jax.ShapeDtypeStruct((nc * 8, NS), f32)],
        scratch_shapes=[pltpu.VMEM((8, NS), f32), pltpu.VMEM((8, NS), f32)],
        name="ssm_fwd", compiler_params=_cp("arbitrary"))(
            proj, proj, bbt_re, bbt_im, ct_re, ct_im, a_re, a_im, apow_re, apow_im, dvec, wglu, bglu)


def _tail(a_in, b_in, proj, x, tgt, wco, wso, wout, gpost):
    L = x.shape[0]
    tm = 256
    def body(a_ref, b_ref, gc_ref, gs_ref, x_ref, t_ref, wco_ref, wso_ref, wout_ref, gp_ref,
             gx_ref, dain_ref, dbin_ref, dgc_ref, dgs_ref, dwout_ref, dwco_ref, dwso_ref, dgp_ref, loss_ref):
        @pl.when(pl.program_id(0) == 0)
        def _():
            dwout_ref[...] = jnp.zeros_like(dwout_ref)
            dwco_ref[...] = jnp.zeros_like(dwco_ref)
            dwso_ref[...] = jnp.zeros_like(dwso_ref)
            dgp_ref[...] = jnp.zeros_like(dgp_ref)
            loss_ref[...] = jnp.zeros_like(loss_ref)
        a = a_ref[...]
        b = b_ref[...]
        co = _dot(a, wco_ref[...])
        so = jnp.concatenate([_dot(b, wso_ref[j]) for j in range(NCHIP)], axis=1)
        sc = _sig(gc_ref[...].astype(f32))
        ss = _sig(gs_ref[...].astype(f32))
        mb = (sc * co + ss * so).astype(bf16)
        out = _dot(mb, wout_ref[...])
        r2 = lax.rsqrt(jnp.mean(out * out, axis=-1, keepdims=True) + RMS_EPS)
        on = out * r2
        gp = gp_ref[...]
        e = x_ref[...] + on * gp - t_ref[...]
        loss_ref[...] += (0.5 / D) * jnp.sum(e * e)
        dy = e * (1.0 / D)
        gx_ref[...] = dy
        dgp_ref[...] += jnp.sum(dy * on, axis=0, keepdims=True)
        dn = dy * gp
        dout = (r2 * (dn - on * jnp.mean(dn * on, axis=-1, keepdims=True))).astype(bf16)
        dwout_ref[...] += _dot_tn(mb, dout)
        dm = _dot_nt(dout, wout_ref[...])
        dgc_ref[...] = (dm * co * sc * (1.0 - sc)).astype(bf16)
        dgs_ref[...] = (dm * so * ss * (1.0 - ss)).astype(bf16)
        dco = (dm * sc).astype(bf16)
        dso = (dm * ss).astype(bf16)
        dwco_ref[...] += _dot_tn(a, dco)
        dbin = None
        for j in range(NCHIP):
            dso_j = dso[:, j * 256:(j + 1) * 256]
            dwso_ref[j] += _dot_tn(b, dso_j)
            t = _dot_nt(dso_j, wso_ref[j])
            dbin = t if dbin is None else dbin + t
        dain_ref[...] = _dot_nt(dco, wco_ref[...]).astype(bf16)
        dbin_ref[...] = dbin.astype(bf16)

    row = lambda w: pl.BlockSpec((tm, w), lambda i: (i, 0))
    one = lambda shape: pl.BlockSpec(shape, lambda i: (0,) * len(shape), pipeline_mode=pl.Buffered(1))
    return pl.pallas_call(
        body, grid=(L // tm,),
        in_specs=[row(CW), row(SW), pl.BlockSpec((tm, D), lambda i: (i, 4)), pl.BlockSpec((tm, D), lambda i: (i, 5)),
                  row(D), row(D), one((CW, D)), one((NCHIP, SW, 256)), one((D, D)), one((1, D))],
        out_specs=[row(D), row(CW), row(SW), row(D), row(D),
                   one((D, D)), one((CW, D)), one((NCHIP, SW, 256)), one((1, D)), one((1, 128))],
        out_shape=[jax.ShapeDtypeStruct((L, D), f32), jax.ShapeDtypeStruct((L, CW), bf16),
                   jax.ShapeDtypeStruct((L, SW), bf16), jax.ShapeDtypeStruct((L, D), bf16),
                   jax.ShapeDtypeStruct((L, D), bf16),
                   jax.ShapeDtypeStruct((D, D), f32), jax.ShapeDtypeStruct((CW, D), f32),
                   jax.ShapeDtypeStruct((NCHIP, SW, 256), f32), jax.ShapeDtypeStruct((1, D), f32),
                   jax.ShapeDtypeStruct((1, 128), f32)],
        name="tail", compiler_params=_cp("arbitrary"))(a_in, b_in, proj, proj, x, tgt, wco, wso, wout, gpost)


def _ssm_bwd(d_bin, y0, proj, sre, sim, cinr, cini, bbt_re, bbt_im, ct_re, ct_im,
             a_re, a_im, apow_re, apow_im, dvec, wglu, bglu):
    L = y0.shape[0]
    nc = L // TC
    def body(dbin_ref, y0_ref, u_ref, zs_ref, sre_ref, sim_ref, cinr_ref, cini_ref,
             bre_ref, bim_ref, cre_ref, cim_ref, are_ref, aim_ref, pwr_ref, pwi_ref, d_ref, wg_ref, bg_ref,
             du_ref, dzs_ref, dbre_ref, dbim_ref, dcre_ref, dcim_ref, dd_ref, dar_ref, dai_ref, dwg_ref, dbg_ref,
             gre, gim, gcr, gci, nxt_re, nxt_im):
        @pl.when(pl.program_id(0) == 0)
        def _():
            for ref in (dbre_ref, dbim_ref, dcre_ref, dcim_ref, dd_ref, dar_ref, dai_ref, dwg_ref, dbg_ref,
                        nxt_re, nxt_im):
                ref[...] = jnp.zeros_like(ref)
        y0 = y0_ref[...]
        u = u_ref[...]
        zs = zs_ref[...].astype(f32)
        dbin = dbin_ref[...].astype(f32)
        t, y1 = _gelu_parts(y0)
        y1b = y1.astype(bf16)
        sg = _sig(_dot(y1b, wg_ref[...]) + bg_ref[...])
        sz = _sig(zs)
        d_y2 = dbin * (zs * sz)
        dzs_ref[...] = (dbin * (y1 * sg) * (sz * (1.0 + zs * (1.0 - sz)))).astype(bf16)
        d_glu = d_y2 * y1 * sg * (1.0 - sg)
        d_glub = d_glu.astype(bf16)
        d_y1 = d_y2 * sg + _dot_nt(d_glub, wg_ref[...])
        dwg_ref[...] += _dot_tn(y1b, d_glub)
        dbg_ref[...] += jnp.sum(d_glu, axis=0, keepdims=True)
        dgelu = 0.5 * (1.0 + t) + 0.5 * y0 * (1.0 - t * t) * GELU_K0 * (1.0 + 3.0 * GELU_K1 * y0 * y0)
        d_y0 = d_y1 * dgelu
        dd_ref[...] += jnp.sum(d_y0 * u.astype(f32), axis=0, keepdims=True)
        dyb = d_y0.astype(bf16)
        for blk in range(4):
            dy1 = dyb[:, 128 * blk:128 * (blk + 1)]
            gre[:, 512 * blk:512 * (blk + 1)] = _dot_nt(dy1, cre_ref[blk])
            gim[:, 512 * blk:512 * (blk + 1)] = -_dot_nt(dy1, cim_ref[blk])
        for lb in range(NS // LBW):
            sl = slice(lb * LBW, (lb + 1) * LBW)
            ar = jnp.broadcast_to(are_ref[:, sl], (8, LBW))
            ai = jnp.broadcast_to(aim_ref[:, sl], (8, LBW))
            def step(k, carry, sl=sl, ar=ar, ai=ai):
                gr, gi = carry
                row = _rows8(R - 2 - k)
                nr = ar * gr + ai * gi + gre[row, sl]
                ni = ar * gi - ai * gr + gim[row, sl]
                gre[row, sl] = nr
                gim[row, sl] = ni
                return nr, ni
            lax.fori_loop(0, R - 1, step, (gre[8 * (R - 1):8 * R, sl], gim[8 * (R - 1):8 * R, sl]))
        a_r = pwr_ref[R - 1:R, :]
        a_i = pwi_ref[R - 1:R, :]
        cr = nxt_re[0:1, :]
        ci = nxt_im[0:1, :]
        for seg in range(7, -1, -1):
            gcr[seg:seg + 1, :] = cr
            gci[seg:seg + 1, :] = ci
            er = gre[seg:seg + 1, :]
            ei = gim[seg:seg + 1, :]
            cr, ci = er + a_r * cr + a_i * ci, ei + a_r * ci - a_i * cr
        nxt_re[0:1, :] = cr
        nxt_im[0:1, :] = ci
        for lb in range(NS // LBW):
            sl = slice(lb * LBW, (lb + 1) * LBW)
            kr = gcr[:, sl]
            ki = gci[:, sl]
            def fix(r, carry, sl=sl, kr=kr, ki=ki):
                pr = jnp.broadcast_to(pwr_ref[pl.ds(R - 1 - r, 1), sl], (8, LBW))
                pi = jnp.broadcast_to(pwi_ref[pl.ds(R - 1 - r, 1), sl], (8, LBW))
                gre[_rows8(r), sl] = gre[_rows8(r), sl] + pr * kr + pi * ki
                gim[_rows8(r), sl] = gim[_rows8(r), sl] + pr * ki - pi * kr
                return carry
            lax.fori_loop(0, R, fix, 0)
        dup = []
        for blk in range(4):
            s4 = slice(512 * blk, 512 * (blk + 1))
            s1 = slice(128 * blk, 128 * (blk + 1))
            grb = gre[:, s4].astype(bf16)
            gib = gim[:, s4].astype(bf16)
            dup.append(_dot_nt(grb, bre_ref[blk]) + _dot_nt(gib, bim_ref[blk]))
            dbre_ref[blk] += _dot_tn(u[:, s1], grb)
            dbim_ref[blk] += _dot_tn(u[:, s1], gib)
            dcre_ref[blk] += _dot_tn(sre_ref[:, s4].astype(bf16), dyb[:, s1])
            dcim_ref[blk] -= _dot_tn(sim_ref[:, s4].astype(bf16), dyb[:, s1])
        du_ref[...] = (jnp.concatenate(dup, axis=1) + d_ref[...] * d_y0).astype(bf16)
        for lb in range(NS // LBW):
            sl = slice(lb * LBW, (lb + 1) * LBW)
            g0r, g0i = gre[0:8, sl], gim[0:8, sl]
            p0r, p0i = cinr_ref[:, sl], cini_ref[:, sl]
            acc0 = (g0r * p0r + g0i * p0i, g0i * p0r - g0r * p0i)
            def dacc(r, carry, sl=sl):
                xr, xi = carry
                gr, gi = gre[_rows8(r), sl], gim[_rows8(r), sl]
                pr, pi = sre_ref[_rows8(r - 1), sl], sim_ref[_rows8(r - 1), sl]
                return xr + gr * pr + gi * pi, xi + gi * pr - gr * pi
            xr, xi = lax.fori_loop(1, R, dacc, acc0)
            dar_ref[:, sl] += xr
            dai_ref[:, sl] += xi

    rev = lambda w, cidx: pl.BlockSpec((TC, w), lambda i, cidx=cidx: (nc - 1 - i, cidx))
    one = lambda shape: pl.BlockSpec(shape, lambda i: (0,) * len(shape))
    return pl.pallas_call(
        body, grid=(nc,),
        in_specs=[rev(SW, 0), rev(SW, 0), rev(SW, 6), rev(SW, 7), rev(NS, 0), rev(NS, 0),
                  pl.BlockSpec((8, NS), lambda i: (nc - 1 - i, 0)), pl.BlockSpec((8, NS), lambda i: (nc - 1 - i, 0)),
                  one((4, 128, 512)), one((4, 128, 512)), one((4, 512, 128)), one((4, 512, 128)),
                  one((1, NS)), one((1, NS)), one((R, NS)), one((R, NS)),
                  one((1, SW)), one((SW, SW)), one((1, SW))],
        out_specs=[rev(SW, 0), rev(SW, 0),
                   one((4, 128, 512)), one((4, 128, 512)), one((4, 512, 128)), one((4, 512, 128)),
                   one((1, SW)), one((8, NS)), one((8, NS)), one((SW, SW)), one((1, SW))],
        out_shape=[jax.ShapeDtypeStruct((L, SW), bf16), jax.ShapeDtypeStruct((L, SW), bf16),
                   jax.ShapeDtypeStruct((4, 128, 512), f32), jax.ShapeDtypeStruct((4, 128, 512), f32),
                   jax.ShapeDtypeStruct((4, 512, 128), f32), jax.ShapeDtypeStruct((4, 512, 128), f32),
                   jax.ShapeDtypeStruct((1, SW), f32), jax.ShapeDtypeStruct((8, NS), f32),
                   jax.ShapeDtypeStruct((8, NS), f32), jax.ShapeDtypeStruct((SW, SW), f32),
                   jax.ShapeDtypeStruct((1, SW), f32)],
        scratch_shapes=[pltpu.VMEM((TC, NS), f32), pltpu.VMEM((TC, NS), f32), pltpu.VMEM((8, NS), f32),
                        pltpu.VMEM((8, NS), f32), pltpu.VMEM((8, NS), f32), pltpu.VMEM((8, NS), f32)],
        name="ssm_bwd", compiler_params=_cp("arbitrary"))(
            d_bin, y0, proj, proj, sre, sim, cinr, cini, bbt_re, bbt_im, ct_re, ct_im,
            a_re, a_im, apow_re, apow_im, dvec, wglu, bglu)


def _conv_bwd(d_ain, cu1, proj, cw, lng, lnb):
    L = cu1.shape[0]
    nc = L // TC
    def body(dain_ref, cu1_ref, ca_ref, cb_ref, zc_ref, cah_ref, cbh_ref, w_ref, g_ref, bb_ref,
             dca_ref, dcb_ref, dzc_ref, dw_ref, dbias_ref, dlng_ref, dlnb_ref, dbuf, ebuf, prev, nxt, dcu0):
        i = pl.program_id(0)
        @pl.when(i == 0)
        def _():
            dw_ref[...] = jnp.zeros_like(dw_ref)
            dbias_ref[...] = jnp.zeros_like(dbias_ref)
            dlng_ref[...] = jnp.zeros_like(dlng_ref)
            dlnb_ref[...] = jnp.zeros_like(dlnb_ref)
            nxt[...] = jnp.zeros_like(nxt)
        def lnb(s, carry):
            rows = pl.ds(pl.multiple_of(s * 32, 32), 32)
            dain = dain_ref[rows, :].astype(f32)
            c1 = cu1_ref[rows, :].astype(f32)
            zc = zc_ref[rows, :].astype(f32)
            xc = c1 - jnp.mean(c1, axis=-1, keepdims=True)
            var = jnp.mean(xc * xc, axis=-1, keepdims=True)
            rstd = lax.rsqrt(var + LN_EPS)
            xh = xc * rstd
            ln = xh * g_ref[...] + bb_ref[...]
            sl_ = _sig(ln)
            sz = _sig(zc)
            dzc_ref[rows, :] = (dain * (ln * sl_) * (sz * (1.0 + zc * (1.0 - sz)))).astype(bf16)
            d_ln = dain * (zc * sz) * (sl_ * (1.0 + ln * (1.0 - sl_)))
            dlng_ref[...] += jnp.sum(d_ln * xh, axis=0, keepdims=True)
            dlnb_ref[...] += jnp.sum(d_ln, axis=0, keepdims=True)
            dxh = d_ln * g_ref[...]
            d_c1 = rstd * (dxh - jnp.mean(dxh, axis=-1, keepdims=True)
                           - xh * jnp.mean(dxh * xh, axis=-1, keepdims=True))
            dbias_ref[...] += jnp.sum(d_c1, axis=0, keepdims=True)
            _put_blocked(dbuf, pl.multiple_of(s * 32, 32), 32, d_c1)
            _put_blocked(ebuf, pl.multiple_of(NH * 8 + s * 32, 32), 32,
                         ca_ref[rows, :].astype(f32) * _sig(cb_ref[rows, :].astype(f32)))
            return carry
        lax.fori_loop(0, TC // 32, lnb, 0)
        sub = lax.broadcasted_iota(jnp.int32, (8, 128), 0)
        def after(p, carry):
            for lb in range(NLB):
                cur = dbuf[lb, _rows8(p), :]
                dbuf[lb, _rows8(R + p), :] = jnp.where(sub == 7, pltpu.roll(nxt[lb, _rows8(p), :], 7, 0),
                                                       pltpu.roll(cur, 7, 0))
            return carry
        lax.fori_loop(0, NH, after, 0)
        nxt[...] = dbuf[:, 0:NH * 8, :]
        def before(s, carry):
            rows = pl.ds(pl.multiple_of(s * 64, 64), 64)
            v = cah_ref[rows, :].astype(f32) * _sig(cbh_ref[rows, :].astype(f32))
            _put_blocked(prev, pl.multiple_of(s * 64, 64), 64, jnp.where(i == nc - 1, jnp.zeros_like(v), v))
            return carry
        lax.fori_loop(0, NH * 8 // 64, before, 0)
        _fill_before(ebuf, prev)
        for lb in range(NLB):
            sl = slice(lb * 128, (lb + 1) * 128)
            wk = [jnp.broadcast_to(w_ref[k:k + 1, sl], (8, 128)) for k in range(KS)]
            def tap(q, carry, lb=lb, wk=wk):
                r = q * RPI
                for j, o in enumerate(_fir(dbuf, lb, r, wk, None, True)):
                    dcu0[lb, _rows8(r + j), :] = o
                return carry
            lax.fori_loop(0, R // RPI, tap, 0)
            def wgrad(q, accs, lb=lb):
                r = q * RPI
                dvs = dbuf[lb, pl.ds(pl.multiple_of(r * 8, 8), RPI * 8), :]
                win = ebuf[lb, pl.ds(pl.multiple_of((r + (NH - KS + 1)) * 8, 8), (KS + RPI - 1) * 8), :]
                accs = list(accs)
                for j in range(RPI):
                    dv = dvs[8 * j:8 * j + 8, :]
                    for k in range(KS):
                        accs[k] = accs[k] + dv * win[8 * (j + k):8 * (j + k) + 8, :]
                return tuple(accs)
            accs = lax.fori_loop(0, R // RPI, wgrad, tuple(jnp.zeros((8, 128), f32) for _ in range(KS)))
            for k in range(KS):
                dw_ref[k, :, sl] += accs[k]
        def glub(s, carry):
            rows = pl.ds(pl.multiple_of(s * 64, 64), 64)
            d0 = _get_blocked(dcu0, pl.multiple_of(s * 64, 64), 64)
            ca = ca_ref[rows, :].astype(f32)
            sb = _sig(cb_ref[rows, :].astype(f32))
            dca_ref[rows, :] = (d0 * sb).astype(bf16)
            dcb_ref[rows, :] = (d0 * ca * sb * (1.0 - sb)).astype(bf16)
            return carry
        lax.fori_loop(0, TC // 64, glub, 0)

    hrows = NH * 8
    per = TC // hrows
    rev = lambda cidx: pl.BlockSpec((TC, CW), lambda i, cidx=cidx: (nc - 1 - i, cidx))
    halo = lambda cidx: pl.BlockSpec((hrows, CW), lambda i, cidx=cidx: (jnp.maximum((nc - 1 - i) * per - 1, 0), cidx))
    one = lambda shape: pl.BlockSpec(shape, lambda i: (0,) * len(shape))
    return pl.pallas_call(
        body, grid=(nc,),
        in_specs=[rev(0), rev(0), rev(0), rev(1), rev(2), halo(0), halo(1), one((32, CW)), one((1, CW)), one((1, CW))],
        out_specs=[rev(0), rev(0), rev(0), one((32, 8, CW)), one((1, CW)), one((1, CW)), one((1, CW))],
        out_shape=[jax.ShapeDtypeStruct((L, CW), bf16), jax.ShapeDtypeStruct((L, CW), bf16),
                   jax.ShapeDtypeStruct((L, CW), bf16), jax.ShapeDtypeStruct((32, 8, CW), f32),
                   jax.ShapeDtypeStruct((1, CW), f32), jax.ShapeDtypeStruct((1, CW), f32),
                   jax.ShapeDtypeStruct((1, CW), f32)],
        scratch_shapes=[pltpu.VMEM((NLB, (R + NH) * 8, 128), f32), pltpu.VMEM((NLB, (NH + R) * 8, 128), f32),
                        pltpu.VMEM((NLB, hrows, 128), f32), pltpu.VMEM((NLB, hrows, 128), f32),
                        pltpu.VMEM((NLB, TC, 128), f32)],
        name="conv_bwd", compiler_params=_cp("arbitrary"))(d_ain, cu1, proj, proj, proj, proj, proj, cw, lng, lnb)


def _win_grad(h, dproj):
    L = h.shape[0]
    tm = 512
    def body(h_ref, d_ref, o_ref):
        @pl.when(pl.program_id(1) == 0)
        def _():
            o_ref[...] = jnp.zeros_like(o_ref)
        o_ref[0] += _dot_tn(h_ref[...], d_ref[...])
    return pl.pallas_call(
        body, grid=(NCHIP, L // tm),
        in_specs=[pl.BlockSpec((tm, D), lambda j, i: (i, 0)), pl.BlockSpec((tm, SHARD_W), lambda j, i: (i, j))],
        out_specs=pl.BlockSpec((1, D, SHARD_W), lambda j, i: (j, 0, 0)),
        out_shape=jax.ShapeDtypeStruct((NCHIP, D, SHARD_W), f32),
        name="win_grad", compiler_params=_cp("arbitrary", "arbitrary"))(h, dproj)


def _x_grad(dproj, w_in, x, gx0, g_pre):
    L = x.shape[0]
    tm = 256
    def body(d_ref, w_ref, x_ref, gx_ref, g_ref, o_ref, dg_ref):
        @pl.when(pl.program_id(0) == 0)
        def _():
            dg_ref[...] = jnp.zeros_like(dg_ref)
        dh = _dot_nt(d_ref[:, 0:SHARD_W], w_ref[0])
        for j in range(1, NCHIP):
            dh = dh + _dot_nt(d_ref[:, j * SHARD_W:(j + 1) * SHARD_W], w_ref[j])
        xt = x_ref[...]
        r = lax.rsqrt(jnp.mean(xt * xt, axis=-1, keepdims=True) + RMS_EPS)
        xn = xt * r
        dg_ref[...] += jnp.sum(dh * xn, axis=0, keepdims=True)
        dxn = dh * g_ref[...]
        o_ref[...] = gx_ref[...] + r * (dxn - xn * jnp.mean(dxn * xn, axis=-1, keepdims=True))
    return pl.pallas_call(
        body, grid=(L // tm,),
        in_specs=[pl.BlockSpec((tm, IN_W), lambda i: (i, 0)),
                  pl.BlockSpec((NCHIP, D, SHARD_W), lambda i: (0, 0, 0), pipeline_mode=pl.Buffered(1)),
                  pl.BlockSpec((tm, D), lambda i: (i, 0)), pl.BlockSpec((tm, D), lambda i: (i, 0)), _full((1, D))],
        out_specs=[pl.BlockSpec((tm, D), lambda i: (i, 0)), _full((1, D))],
        out_shape=[jax.ShapeDtypeStruct((L, D), f32), jax.ShapeDtypeStruct((1, D), f32)],
        name="x_grad", compiler_params=_cp("arbitrary"))(dproj, w_in, x, gx0, g_pre)


def _pair_sum(c_arr, ga, ra, gb, rb, gs, rs):
    def body(c_ref, ga_ref, ra_ref, gb_ref, rb_ref, gs_ref, rs_ref, pa_ref, pb_ref, ps_ref):
        pa_ref[...] = (ga_ref[...] + ra_ref[...]).astype(bf16)
        pb_ref[...] = (gb_ref[...] + rb_ref[...]).astype(bf16)
        ps_ref[...] = gs_ref[...] + rs_ref[...]
    grid_spec = pltpu.PrefetchScalarGridSpec(
        num_scalar_prefetch=1, grid=(NCHIP,),
        in_specs=[pl.BlockSpec((1, D // 2, SHARD_W), lambda j, c: (j, c[0], 0)),
                  pl.BlockSpec((1, D // 2, SHARD_W), lambda j, c: (j, 0, 0)),
                  pl.BlockSpec((1, REST_ROWS // 2, 1024), lambda j, c: (j, c[0], 0)),
                  pl.BlockSpec((1, REST_ROWS // 2, 1024), lambda j, c: (j, 0, 0)),
                  pl.BlockSpec((SMALL_ROWS, 128), lambda j, c: (0, 0)),
                  pl.BlockSpec((SMALL_ROWS, 128), lambda j, c: (0, 0))],
        out_specs=[pl.BlockSpec((1, D // 2, SHARD_W), lambda j, c: (j, 0, 0)),
                   pl.BlockSpec((1, REST_ROWS // 2, 1024), lambda j, c: (j, 0, 0)),
                   pl.BlockSpec((SMALL_ROWS, 128), lambda j, c: (0, 0))])
    return pl.pallas_call(
        body, grid_spec=grid_spec,
        out_shape=[jax.ShapeDtypeStruct((NCHIP, D // 2, SHARD_W), bf16),
                   jax.ShapeDtypeStruct((NCHIP, REST_ROWS // 2, 1024), bf16),
                   jax.ShapeDtypeStruct((SMALL_ROWS, 128), f32)],
        name="pair_sum", compiler_params=_cp("arbitrary"))(c_arr, ga, ra, gb, rb, gs, rs)


def _chip_sum(qa, qb, qs):
    nt = 4
    def body(qa_ref, qb_ref, qs_ref, fa_ref, fb_ref, fs_ref):
        for q_ref, f_ref in ((qa_ref, fa_ref), (qb_ref, fb_ref), (qs_ref, fs_ref)):
            acc = q_ref[0].astype(f32)
            for j in range(1, NCHIP):
                acc = acc + q_ref[j].astype(f32)
            f_ref[...] = acc
    ra, rb, rs = D // 2 // nt, REST_ROWS // 2 // nt, SMALL_ROWS // nt
    return pl.pallas_call(
        body, grid=(nt,),
        in_specs=[pl.BlockSpec((NCHIP, ra, SHARD_W), lambda i: (0, i, 0)),
                  pl.BlockSpec((NCHIP, rb, 1024), lambda i: (0, i, 0)),
                  pl.BlockSpec((NCHIP, rs, 128), lambda i: (0, i, 0))],
        out_specs=[pl.BlockSpec((ra, SHARD_W), lambda i: (i, 0)), pl.BlockSpec((rb, 1024), lambda i: (i, 0)),
                   pl.BlockSpec((rs, 128), lambda i: (i, 0))],
        out_shape=[jax.ShapeDtypeStruct((D // 2, SHARD_W), f32), jax.ShapeDtypeStruct((REST_ROWS // 2, 1024), f32),
                   jax.ShapeDtypeStruct((SMALL_ROWS, 128), f32)],
        name="chip_sum", compiler_params=_cp("arbitrary"))(qa, qb, qs)


def _adamw_math(w, g, m, v):
    m2 = B1 * m + (1.0 - B1) * g
    v2 = B2 * v + (1.0 - B2) * (g * g)
    m_hat = m2 / (1.0 - B1 ** STEP)
    v_hat = v2 / (1.0 - B2 ** STEP)
    delta = -LR * (m_hat / (jnp.sqrt(v_hat) + EPS) + WD * w)
    return delta, m2, v2


def _adamw(name, w, g, m, v):
    rows, cols = w.shape
    tm = rows if rows <= 256 else (256 if rows % 256 == 0 else 128)
    assert rows % tm == 0
    def body(w_ref, g_ref, m_ref, v_ref, d_ref, m2_ref, v2_ref):
        d, m2, v2 = _adamw_math(w_ref[...], g_ref[...], m_ref[...], v_ref[...])
        d_ref[...] = d
        m2_ref[...] = m2
        v2_ref[...] = v2
    spec = pl.BlockSpec((tm, cols), lambda i: (i, 0))
    shp = jax.ShapeDtypeStruct((rows, cols), f32)
    return pl.pallas_call(
        body, grid=(rows // tm,), in_specs=[spec] * 4, out_specs=[spec] * 3, out_shape=[shp] * 3,
        name=name, compiler_params=_cp("arbitrary"))(w, g, m, v)


_ANY = pl.BlockSpec(memory_space=pl.ANY)


def _chunks(rows, parts):
    step = rows // parts
    assert step * parts == rows and step % 16 == 0
    return [(i * step, step) for i in range(parts)]


def _place():
    x, y, c = lax.axis_index("x"), lax.axis_index("y"), lax.axis_index("c")
    chips = [(1 - x, y), (x, 1 - y), (1 - x, 1 - y)]
    return x, y, c, chips


def _gather_weights(win_s, rest_s):
    segs = [(0, D // 2, r0, n) for r0, n in _chunks(D // 2, 4)] + \
           [(1, REST_ROWS // 2, r0, n) for r0, n in _chunks(REST_ROWS // 2, 2)]
    ns = len(segs)
    def body(a_ref, b_ref, oa_ref, ob_ref, send_sems, recv_sems, local_sems):
        x, y, c, chips = _place()
        k = 2 * x + y
        sibling = (x, y, 1 - c)
        ins, outs = (a_ref, b_ref), (oa_ref, ob_ref)

        def dst(which, half, chip, pc, r0, n):
            return outs[which].at[chip, pl.ds(pc * half + r0, n), :]

        def rcopy(i, src, dst_ref, to):
            return pltpu.make_async_remote_copy(src_ref=src, dst_ref=dst_ref, send_sem=send_sems.at[i],
                                                recv_sem=recv_sems.at[i], device_id=to, device_id_type=MESH)

        own = [pltpu.make_async_copy(ins[w], outs[w].at[k], local_sems.at[w]) for w in range(2)]
        for cp in own:
            cp.start()
        first = []
        for j, chip in enumerate(chips):
            for s, (w, half, r0, n) in enumerate(segs):
                first.append(rcopy(j * ns + s, ins[w].at[pl.ds(c * half + r0, n), :], dst(w, half, k, c, r0, n),
                                   (*chip, c)))
        for cp in first:
            cp.start()
        passed = []
        for j, chip in enumerate(chips):
            cj = 2 * chip[0] + chip[1]
            for s, (w, half, r0, n) in enumerate(segs):
                landed = dst(w, half, cj, c, r0, n)
                rcopy(j * ns + s, landed, landed, (x, y, c)).wait_recv()
                fwd = rcopy(3 * ns + j * ns + s, landed, landed, sibling)
                fwd.start()
                passed.append(fwd)
        for j, chip in enumerate(chips):
            cj = 2 * chip[0] + chip[1]
            for s, (w, half, r0, n) in enumerate(segs):
                theirs = dst(w, half, cj, 1 - c, r0, n)
                rcopy(3 * ns + j * ns + s, theirs, theirs, (x, y, c)).wait_recv()
        for cp in first + passed:
            cp.wait_send()
        for cp in own:
            cp.wait()

    return pl.pallas_call(
        body, in_specs=[_ANY, _ANY], out_specs=[_ANY, _ANY],
        out_shape=[jax.ShapeDtypeStruct((NCHIP, D, SHARD_W), bf16), jax.ShapeDtypeStruct((NCHIP, REST_ROWS, 1024), bf16)],
        scratch_shapes=[pltpu.SemaphoreType.DMA((6 * ns,)), pltpu.SemaphoreType.DMA((6 * ns,)),
                        pltpu.SemaphoreType.DMA((2,))],
        name="gather_weights")(win_s, rest_s)


def _pair_exchange(ga, gb, gs):
    ha, hb = D // 2, REST_ROWS // 2
    def body(a_ref, b_ref, s_ref, ra_ref, rb_ref, rs_ref, send_sems, recv_sems):
        x, y, c, _ = _place()
        sibling = (x, y, 1 - c)
        pieces = []
        for j in range(NCHIP):
            for r0, n in _chunks(ha, 4):
                pieces.append((a_ref.at[j, pl.ds((1 - c) * ha + r0, n), :], ra_ref.at[j, pl.ds(r0, n), :]))
            for r0, n in _chunks(hb, 2):
                pieces.append((b_ref.at[j, pl.ds((1 - c) * hb + r0, n), :], rb_ref.at[j, pl.ds(r0, n), :]))
        pieces.append((s_ref, rs_ref))
        copies = [pltpu.make_async_remote_copy(src_ref=s, dst_ref=d, send_sem=send_sems.at[i], recv_sem=recv_sems.at[i],
                                               device_id=sibling, device_id_type=MESH)
                  for i, (s, d) in enumerate(pieces)]
        for cp in copies:
            cp.start()
        for cp in copies:
            cp.wait_recv()
        for cp in copies:
            cp.wait_send()

    n = NCHIP * 6 + 1
    return pl.pallas_call(
        body, in_specs=[_ANY, _ANY, _ANY], out_specs=[_ANY, _ANY, _ANY],
        out_shape=[jax.ShapeDtypeStruct((NCHIP, ha, SHARD_W), f32), jax.ShapeDtypeStruct((NCHIP, hb, 1024), f32),
                   jax.ShapeDtypeStruct((SMALL_ROWS, 128), f32)],
        scratch_shapes=[pltpu.SemaphoreType.DMA((n,)), pltpu.SemaphoreType.DMA((n,))],
        name="pair_exchange")(ga, gb, gs)


def _chip_exchange(pa, pb, ps):
    ha, hb = D // 2, REST_ROWS // 2
    def body(a_ref, b_ref, s_ref, qa_ref, qb_ref, qs_ref, send_sems, recv_sems, local_sems):
        x, y, c, chips = _place()
        k = 2 * x + y
        own = [pltpu.make_async_copy(a_ref.at[k], qa_ref.at[k], local_sems.at[0]),
               pltpu.make_async_copy(b_ref.at[k], qb_ref.at[k], local_sems.at[1]),
               pltpu.make_async_copy(s_ref, qs_ref.at[k], local_sems.at[2])]
        for cp in own:
            cp.start()
        copies = []
        for j, chip in enumerate(chips):
            cj = 2 * chip[0] + chip[1]
            pieces = [(a_ref.at[cj, pl.ds(r0, n), :], qa_ref.at[k, pl.ds(r0, n), :]) for r0, n in _chunks(ha, 2)]
            pieces += [(b_ref.at[cj], qb_ref.at[k]), (s_ref, qs_ref.at[k])]
            for s, (src, dst_ref) in enumerate(pieces):
                copies.append(pltpu.make_async_remote_copy(
                    src_ref=src, dst_ref=dst_ref, send_sem=send_sems.at[4 * j + s], recv_sem=recv_sems.at[4 * j + s],
                    device_id=(*chip, c), device_id_type=MESH))
        for cp in copies:
            cp.start()
        for cp in copies:
            cp.wait_recv()
        for cp in copies:
            cp.wait_send()
        for cp in own:
            cp.wait()

    return pl.pallas_call(
        body, in_specs=[_ANY, _ANY, _ANY], out_specs=[_ANY, _ANY, _ANY],
        out_shape=[jax.ShapeDtypeStruct((NCHIP, ha, SHARD_W), bf16), jax.ShapeDtypeStruct((NCHIP, hb, 1024), bf16),
                   jax.ShapeDtypeStruct((NCHIP, SMALL_ROWS, 128), f32)],
        scratch_shapes=[pltpu.SemaphoreType.DMA((12,)), pltpu.SemaphoreType.DMA((12,)), pltpu.SemaphoreType.DMA((3,))],
        name="chip_exchange")(pa, pb, ps)


def _sibling_exchange(fa, fb):
    ha, hb = D // 2, REST_ROWS // 2
    def body(a_ref, b_ref, oa_ref, ob_ref, send_sems, recv_sems, local_sems):
        x, y, c, _ = _place()
        own = [pltpu.make_async_copy(a_ref, oa_ref.at[c], local_sems.at[0]),
               pltpu.make_async_copy(b_ref, ob_ref.at[c], local_sems.at[1])]
        for cp in own:
            cp.start()
        pieces = [(a_ref.at[pl.ds(r0, n), :], oa_ref.at[c, pl.ds(r0, n), :]) for r0, n in _chunks(ha, 4)]
        pieces += [(b_ref.at[pl.ds(r0, n), :], ob_ref.at[c, pl.ds(r0, n), :]) for r0, n in _chunks(hb, 2)]
        copies = [pltpu.make_async_remote_copy(src_ref=s, dst_ref=d, send_sem=send_sems.at[i], recv_sem=recv_sems.at[i],
                                               device_id=(x, y, 1 - c), device_id_type=MESH)
                  for i, (s, d) in enumerate(pieces)]
        for cp in copies:
            cp.start()
        for cp in copies:
            cp.wait_recv()
        for cp in copies:
            cp.wait_send()
        for cp in own:
            cp.wait()

    return pl.pallas_call(
        body, in_specs=[_ANY, _ANY], out_specs=[_ANY, _ANY],
        out_shape=[jax.ShapeDtypeStruct((2, ha, SHARD_W), f32), jax.ShapeDtypeStruct((2, hb, 1024), f32)],
        scratch_shapes=[pltpu.SemaphoreType.DMA((6,)), pltpu.SemaphoreType.DMA((6,)), pltpu.SemaphoreType.DMA((2,))],
        name="sibling_exchange")(fa, fb)


def _nchunks(half, cols, itemsize):
    return 4 if half * cols * itemsize >= (1 << 20) else 1


def _segments(metas):
    segs = []
    for w, (half, cols, dt) in enumerate(metas):
        for r0, n in _chunks(half, _nchunks(half, cols, jnp.dtype(dt).itemsize)):
            segs.append((w, half, r0, n))
    return segs


def _rcopy(i, src, dst, send_sems, recv_sems, to):
    return pltpu.make_async_remote_copy(src_ref=src, dst_ref=dst, send_sem=send_sems.at[i], recv_sem=recv_sems.at[i],
                                        device_id=to, device_id_type=MESH)


def _gather_list(shards):
    na = len(shards)
    segs = _segments([(a.shape[0] // 2, a.shape[1], a.dtype) for a in shards])
    ns = len(segs)
    def body(*refs):
        ins, outs, (send_sems, recv_sems) = refs[:na], refs[na:2 * na], refs[2 * na:]
        x, y, c, chips = _place()
        k = 2 * x + y
        me, sibling = (x, y, c), (x, y, 1 - c)

        def dst(w, half, chip, pc, r0, n):
            return outs[w].at[chip, pl.ds(pc * half + r0, n), :]

        first = []
        for j, chip in enumerate(chips):
            for s, (w, half, r0, n) in enumerate(segs):
                first.append(_rcopy(j * ns + s, ins[w].at[pl.ds(c * half + r0, n), :], dst(w, half, k, c, r0, n),
                                    send_sems, recv_sems, (*chip, c)))
        for cp in first:
            cp.start()
        passed = []
        for j, chip in enumerate(chips):
            cj = 2 * chip[0] + chip[1]
            for s, (w, half, r0, n) in enumerate(segs):
                landed = dst(w, half, cj, c, r0, n)
                _rcopy(j * ns + s, landed, landed, send_sems, recv_sems, me).wait_recv()
                fwd = _rcopy(3 * ns + j * ns + s, landed, landed, send_sems, recv_sems, sibling)
                fwd.start()
                passed.append(fwd)
        for j, chip in enumerate(chips):
            cj = 2 * chip[0] + chip[1]
            for s, (w, half, r0, n) in enumerate(segs):
                theirs = dst(w, half, cj, 1 - c, r0, n)
                _rcopy(3 * ns + j * ns + s, theirs, theirs, send_sems, recv_sems, me).wait_recv()
        for cp in first + passed:
            cp.wait_send()

    return pl.pallas_call(
        body, in_specs=[_ANY] * na, out_specs=[_ANY] * na,
        out_shape=[jax.ShapeDtypeStruct((NCHIP,) + a.shape, a.dtype) for a in shards],
        scratch_shapes=[pltpu.SemaphoreType.DMA((6 * ns,)), pltpu.SemaphoreType.DMA((6 * ns,))],
        name="gather_weights")(*shards)


def _pair_exchange_list(grads, small):
    na = len(grads)
    segs = _segments([(g.shape[1] // 2, g.shape[2], g.dtype) for g in grads])
    n = NCHIP * len(segs) + 1
    def body(*refs):
        ins, s_ref, outs, rs_ref, (send_sems, recv_sems) = (refs[:na], refs[na], refs[na + 1:2 * na + 1],
                                                            refs[2 * na + 1], refs[2 * na + 2:])
        x, y, c, _ = _place()
        pieces = [(s_ref, rs_ref)]
        for j in range(NCHIP):
            for w, half, r0, rows in segs:
                pieces.append((ins[w].at[j, pl.ds((1 - c) * half + r0, rows), :], outs[w].at[j, pl.ds(r0, rows), :]))
        copies = [_rcopy(i, s, d, send_sems, recv_sems, (x, y, 1 - c)) for i, (s, d) in enumerate(pieces)]
        for cp in copies:
            cp.start()
        for cp in copies:
            cp.wait_recv()
        for cp in copies:
            cp.wait_send()

    return pl.pallas_call(
        body, in_specs=[_ANY] * (na + 1), out_specs=[_ANY] * (na + 1),
        out_shape=[jax.ShapeDtypeStruct((NCHIP, g.shape[1] // 2, g.shape[2]), f32) for g in grads]
        + [jax.ShapeDtypeStruct((SMALL_ROWS, 128), f32)],
        scratch_shapes=[pltpu.SemaphoreType.DMA((n,)), pltpu.SemaphoreType.DMA((n,))],
        name="pair_exchange")(*grads, small)


def _pair_sum_list(c_arr, grads, recvs, small, rsmall):
    na = len(grads)
    def body(c_ref, *refs):
        g_refs, r_refs, s_ref, rs_ref = refs[:na], refs[na:2 * na], refs[2 * na], refs[2 * na + 1]
        o_refs, os_ref = refs[2 * na + 2:3 * na + 2], refs[3 * na + 2]
        for g_ref, r_ref, o_ref in zip(g_refs, r_refs, o_refs):
            o_ref[...] = (g_ref[...] + r_ref[...]).astype(bf16)
        os_ref[...] = s_ref[...] + rs_ref[...]
    half = lambda g: pl.BlockSpec((1, g.shape[1] // 2, g.shape[2]), lambda j, c: (j, c[0], 0))
    low = lambda g: pl.BlockSpec((1, g.shape[1] // 2, g.shape[2]), lambda j, c: (j, 0, 0))
    sm = pl.BlockSpec((SMALL_ROWS, 128), lambda j, c: (0, 0))
    grid_spec = pltpu.PrefetchScalarGridSpec(
        num_scalar_prefetch=1, grid=(NCHIP,),
        in_specs=[half(g) for g in grads] + [low(g) for g in grads] + [sm, sm],
        out_specs=[low(g) for g in grads] + [sm])
    return pl.pallas_call(
        body, grid_spec=grid_spec,
        out_shape=[jax.ShapeDtypeStruct((NCHIP, g.shape[1] // 2, g.shape[2]), bf16) for g in grads]
        + [jax.ShapeDtypeStruct((SMALL_ROWS, 128), f32)],
        name="pair_sum", compiler_params=_cp("arbitrary"))(c_arr, *grads, *recvs, small, rsmall)


def _chip_exchange_list(parts, small):
    na = len(parts)
    segs = _segments([(p.shape[1], p.shape[2], p.dtype) for p in parts])
    ns = len(segs) + 1
    def body(*refs):
        ins, s_ref, outs, qs_ref, (send_sems, recv_sems) = (refs[:na], refs[na], refs[na + 1:2 * na + 1],
                                                            refs[2 * na + 1], refs[2 * na + 2:])
        x, y, c, chips = _place()
        k = 2 * x + y
        copies = []
        for j, chip in enumerate(chips):
            cj = 2 * chip[0] + chip[1]
            pieces = [(s_ref, qs_ref.at[k])]
            pieces += [(ins[w].at[cj, pl.ds(r0, n), :], outs[w].at[k, pl.ds(r0, n), :]) for w, _, r0, n in segs]
            copies += [_rcopy(ns * j + s, src, d, send_sems, recv_sems, (*chip, c)) for s, (src, d) in enumerate(pieces)]
        for cp in copies:
            cp.start()
        for cp in copies:
            cp.wait_recv()
        for cp in copies:
            cp.wait_send()

    return pl.pallas_call(
        body, in_specs=[_ANY] * (na + 1), out_specs=[_ANY] * (na + 1),
        out_shape=[jax.ShapeDtypeStruct(p.shape, bf16) for p in parts]
        + [jax.ShapeDtypeStruct((NCHIP, SMALL_ROWS, 128), f32)],
        scratch_shapes=[pltpu.SemaphoreType.DMA((3 * ns,)), pltpu.SemaphoreType.DMA((3 * ns,))],
        name="chip_exchange")(*parts, small)


def _chip_sum_list(parts, small):
    na = len(parts)
    nt = 2
    def body(*refs):
        for q_ref, f_ref in zip(refs[:na + 1], refs[na + 1:]):
            acc = q_ref[0].astype(f32)
            for j in range(1, NCHIP):
                acc = acc + q_ref[j].astype(f32)
            f_ref[...] = acc
    arrs = list(parts) + [small]
    return pl.pallas_call(
        body, grid=(nt,),
        in_specs=[pl.BlockSpec((NCHIP, a.shape[1] // nt, a.shape[2]), lambda i: (0, i, 0)) for a in arrs],
        out_specs=[pl.BlockSpec((a.shape[1] // nt, a.shape[2]), lambda i: (i, 0)) for a in arrs],
        out_shape=[jax.ShapeDtypeStruct(a.shape[1:], f32) for a in arrs],
        name="chip_sum", compiler_params=_cp("arbitrary"))(*arrs)


def _sibling_exchange_list(halves):
    na = len(halves)
    segs = _segments([(h.shape[0], h.shape[1], h.dtype) for h in halves])
    def body(*refs):
        ins, outs, (send_sems, recv_sems) = refs[:na], refs[na:2 * na], refs[2 * na:]
        x, y, c, _ = _place()
        copies = [_rcopy(i, ins[w].at[pl.ds(r0, n), :], outs[w].at[pl.ds(r0, n), :], send_sems, recv_sems, (x, y, 1 - c))
                  for i, (w, _, r0, n) in enumerate(segs)]
        for cp in copies:
            cp.start()
        for cp in copies:
            cp.wait_recv()
        for cp in copies:
            cp.wait_send()

    return pl.pallas_call(
        body, in_specs=[_ANY] * na, out_specs=[_ANY] * na,
        out_shape=[jax.ShapeDtypeStruct(h.shape, f32) for h in halves],
        scratch_shapes=[pltpu.SemaphoreType.DMA((len(segs),)), pltpu.SemaphoreType.DMA((len(segs),))],
        name="sibling_exchange")(*halves)


_REST_ROWS = (256, 256, 64, 128)
_CONV_PAD = 8192


def _pack_rest(mats, conv_rows, dtype):
    parts = [mats[0], mats[1], mats[2].reshape(64, 1024), mats[3].reshape(128, 1024)]
    parts = [p.astype(dtype) for p in parts] + [conv_rows]
    used = sum(p.shape[0] for p in parts)
    parts.append(jnp.zeros((REST_ROWS - used, 1024), dtype))
    return jnp.concatenate(parts, axis=0)


def _pack_rest_weights(mats, conv_w_s):
    flat = jnp.pad(conv_w_s.reshape(-1), (0, _CONV_PAD - KS * 256))
    return _pack_rest(mats, lax.bitcast_convert_type(flat, bf16).reshape(16, 1024), bf16)


def _pack_rest_grads(mats, conv_w_s):
    flat = jnp.pad(conv_w_s.reshape(-1), (0, _CONV_PAD - KS * 256))
    return _pack_rest(mats, flat.reshape(8, 1024), f32)


def _split_rest(p, conv_rows):
    o = 0
    out = []
    for rows in _REST_ROWS + (conv_rows,):
        out.append(p[..., o:o + rows, :])
        o += rows
    return out


_SMALL = (("pre_norm_gain", (1, 1024)), ("conv_b", (1, 1024)), ("conv_ln_gain", (1, 1024)), ("conv_ln_bias", (1, 1024)),
          ("ssm_lambda_re", (1, 32, 64)), ("ssm_lambda_im", (1, 32, 64)), ("ssm_log_dt", (1, 32)),
          ("ssm_b_re", (1, 32, 64, 16)), ("ssm_b_im", (1, 32, 64, 16)), ("ssm_c_re", (1, 32, 16, 64)),
          ("ssm_c_im", (1, 32, 16, 64)), ("ssm_d", (1, 32, 16)), ("b_ssm_glu", (1, 512)), ("post_norm_gain", (1, 1024)))


def _pack_small(vals, extra=None):
    rows = []
    for v in list(vals) + ([extra] if extra is not None else []):
        flat = v.reshape(-1).astype(f32)
        n = -(-flat.shape[0] // 1024) * 1024
        rows.append(jnp.pad(flat, (0, n - flat.shape[0])).reshape(-1, 128))
    used = sum(r.shape[0] for r in rows)
    rows.append(jnp.zeros((SMALL_ROWS - used, 128), f32))
    return jnp.concatenate(rows, axis=0)


def _unpack_small(p):
    o = 0
    out = []
    for _, shape in _SMALL:
        n = int(np.prod(shape))
        nr = -(-n // 1024) * 8
        out.append(p[o:o + nr].reshape(-1)[:n].reshape(shape))
        o += nr
    return out, p[o, 0]


def _discretize(lam_re, lam_im, log_dt, b_re, b_im):
    dt = jnp.exp(log_dt)[:, None]
    mag = jnp.exp(lam_re * dt)
    ar = mag * jnp.cos(lam_im * dt)
    ai = mag * jnp.sin(lam_im * dt)
    den = lam_re * lam_re + lam_im * lam_im
    zr = ((ar - 1.0) * lam_re + ai * lam_im) / den
    zi = (ai * lam_re - (ar - 1.0) * lam_im) / den
    bbr = zr[..., None] * b_re - zi[..., None] * b_im
    bbi = zr[..., None] * b_im + zi[..., None] * b_re
    return ar, ai, bbr, bbi


_EYE8 = np.eye(8, dtype=np.float32)


def _bbt_blocks(bb):
    v = bb.reshape(4, 8, PST, H).transpose(0, 1, 3, 2)
    return jnp.einsum("bghp,gk->bghkp", v, _EYE8).reshape(4, 128, 512)


def _bbt_unblock(m):
    v = jnp.einsum("bghkp,gk->bghp", m.reshape(4, 8, H, 8, PST), _EYE8)
    return v.transpose(0, 1, 3, 2).reshape(G, PST, H)


def _ct_blocks(cc):
    v = cc.reshape(4, 8, H, PST)
    return jnp.einsum("bghp,gk->bgpkh", v, _EYE8).reshape(4, 512, 128)


def _ct_unblock(m):
    v = jnp.einsum("bgpkh,gk->bghp", m.reshape(4, 8, PST, 8, H), _EYE8)
    return v.reshape(G, H, PST)


def _interleave(a):
    L, C = a.shape
    return a.reshape(L // TC, 8, R, C).transpose(0, 2, 1, 3).reshape(L, C)


def _deinterleave(a):
    L, C = a.shape
    return a.reshape(L // TC, R, 8, C).transpose(0, 2, 1, 3).reshape(L, C)


def _local_step(x, tgt, w_in, conv_w, w_co, w_glu, w_so, w_out, small):
    (g_pre, conv_b, ln_g, ln_b, lam_re, lam_im, log_dt, b_re, b_im, c_re, c_im, dvec, b_glu, g_post) = small
    lam_re, lam_im, log_dt = lam_re[0], lam_im[0], log_dt[0]
    b_re, b_im, c_re, c_im = b_re[0], b_im[0], c_re[0], c_im[0]
    (ar, ai, bbr, bbi), disc_vjp = jax.vjp(_discretize, lam_re, lam_im, log_dt, b_re, b_im)
    a_re = ar.reshape(1, NS)
    a_im = ai.reshape(1, NS)
    dt = jnp.exp(log_dt)[:, None]
    steps = jnp.arange(1, R + 1, dtype=f32)[:, None, None]
    apow_re = (jnp.exp(steps * (lam_re * dt)) * jnp.cos(steps * (lam_im * dt))).reshape(R, NS)
    apow_im = (jnp.exp(steps * (lam_re * dt)) * jnp.sin(steps * (lam_im * dt))).reshape(R, NS)
    bbt_re, bbt_im = _bbt_blocks(bbr).astype(bf16), _bbt_blocks(bbi).astype(bf16)
    ct_re, ct_im = _ct_blocks(c_re).astype(bf16), _ct_blocks(c_im).astype(bf16)
    d_row = dvec.reshape(1, SW)
    cw32 = jnp.pad(conv_w, ((0, 1), (0, 0)))
    xi = _interleave(x)
    ti = _interleave(tgt)

    h = _prenorm(xi, g_pre)
    proj = _proj_fwd(h, w_in)
    cu1, a_in = _conv_fwd(proj, cw32, conv_b, ln_g, ln_b)
    y0, b_in, sre, sim, cinr, cini = _ssm_fwd(proj, bbt_re, bbt_im, ct_re, ct_im, a_re, a_im,
                                              apow_re, apow_im, d_row, w_glu, b_glu)
    gx0, d_ain, d_bin, d_gc, d_gs, dw_out, dw_co, dw_so, dg_post, loss = _tail(
        a_in, b_in, proj, xi, ti, w_co, w_so, w_out, g_post)
    (d_u, d_zs, dbbt_re, dbbt_im, dct_re, dct_im, dd, dar8, dai8, dw_glu, db_glu) = _ssm_bwd(
        d_bin, y0, proj, sre, sim, cinr, cini, bbt_re, bbt_im, ct_re, ct_im,
        a_re, a_im, apow_re, apow_im, d_row, w_glu, b_glu)
    d_ca, d_cb, d_zc, dcw8, d_convb, d_lng, d_lnb = _conv_bwd(d_ain, cu1, proj, cw32, ln_g, ln_b)
    dproj = jnp.concatenate([d_ca, d_cb, d_zc, d_u, d_zs, d_gc, d_gs], axis=1)
    dw_in = _win_grad(h, dproj)
    gxi, dg_pre = _x_grad(dproj, w_in, xi, gx0, g_pre)
    grad_x = _deinterleave(gxi)

    d_ar = jnp.sum(dar8, axis=0).reshape(G, PST)
    d_ai = jnp.sum(dai8, axis=0).reshape(G, PST)
    d_lre, d_lim, d_ldt, d_bre, d_bim = disc_vjp((d_ar, d_ai, _bbt_unblock(dbbt_re), _bbt_unblock(dbbt_im)))
    d_conv_w = jnp.sum(dcw8, axis=1)[:KS]
    small_grads = [dg_pre, d_convb, d_lng, d_lnb, d_lre[None], d_lim[None], d_ldt[None], d_bre[None], d_bim[None],
                   _ct_unblock(dct_re)[None], _ct_unblock(dct_im)[None], dd.reshape(1, G, H), db_glu, dg_post]
    return loss[0, 0], grad_x, (dw_in, dw_co, dw_out, dw_glu, dw_so, d_conv_w), small_grads


def kernel(x, pre_norm_gain, w_in, conv_w, conv_b, conv_ln_gain, conv_ln_bias, w_conv_out, ssm_lambda_re, ssm_lambda_im, ssm_log_dt, ssm_b_re, ssm_b_im, ssm_c_re, ssm_c_im, ssm_d, w_ssm_glu, b_ssm_glu, w_ssm_out, w_out, post_norm_gain, loss_target, m_pre_norm_gain, m_w_in, m_conv_w, m_conv_b, m_conv_ln_gain, m_conv_ln_bias, m_w_conv_out, m_ssm_lambda_re, m_ssm_lambda_im, m_ssm_log_dt, m_ssm_b_re, m_ssm_b_im, m_ssm_c_re, m_ssm_c_im, m_ssm_d, m_w_ssm_glu, m_b_ssm_glu, m_w_ssm_out, m_w_out, m_post_norm_gain, v_pre_norm_gain, v_w_in, v_conv_w, v_conv_b, v_conv_ln_gain, v_conv_ln_bias, v_w_conv_out, v_ssm_lambda_re, v_ssm_lambda_im, v_ssm_log_dt, v_ssm_b_re, v_ssm_b_im, v_ssm_c_re, v_ssm_c_im, v_ssm_d, v_w_ssm_glu, v_b_ssm_glu, v_w_ssm_out, v_w_out, v_post_norm_gain):
    k = 2 * lax.axis_index("x") + lax.axis_index("y")
    c = lax.axis_index("c")
    own = lambda full, mine: lax.dynamic_update_slice(full, mine[None].astype(full.dtype), (k,) + (0,) * mine.ndim)
    shards = [w_in[0].astype(bf16), w_conv_out[0].astype(bf16), w_out[0].astype(bf16), w_ssm_glu[0].astype(bf16),
              w_ssm_out[0].astype(bf16), jnp.pad(conv_w[0], ((0, CONV_ROWS - KS), (0, 0)))]
    w_in_g, w_co_g, w_out_g, w_glu_g, w_so_g, conv_w_g = [own(g, s) for g, s in zip(_gather_list(shards), shards)]
    conv_w_f = conv_w_g[:, :KS].transpose(1, 0, 2).reshape(KS, CW)

    small = (pre_norm_gain, conv_b, conv_ln_gain, conv_ln_bias, ssm_lambda_re, ssm_lambda_im, ssm_log_dt, ssm_b_re,
             ssm_b_im, ssm_c_re, ssm_c_im, ssm_d, b_ssm_glu, post_norm_gain)
    loss_part, grad_x, big_grads, small_grads = _local_step(
        x[0], loss_target[0], w_in_g, conv_w_f, w_co_g.reshape(CW, D), w_glu_g.reshape(SW, SW), w_so_g,
        w_out_g.reshape(D, D), small)

    dw_in, dw_co, dw_out, dw_glu, dw_so, d_conv_w = big_grads
    d_conv_w = jnp.pad(d_conv_w, ((0, CONV_ROWS - KS), (0, 0))).reshape(CONV_ROWS, NCHIP, 256).transpose(1, 0, 2)
    grads = [dw_in, dw_co.reshape(NCHIP, 256, D), dw_out.reshape(NCHIP, 256, D), dw_glu.reshape(NCHIP, 128, SW),
             dw_so, d_conv_w]
    gs = _pack_small(small_grads, extra=loss_part)
    *recvs, rs = _pair_exchange_list(grads, gs)
    *parts, ps = _pair_sum_list(c.astype(jnp.int32).reshape(1), grads, recvs, gs, rs)
    *qparts, qs = _chip_exchange_list(parts, ps)
    qparts = [own(q, p[k]) for q, p in zip(qparts, parts)]
    *halves, fs = _chip_sum_list(qparts, own(qs, ps))
    theirs = _sibling_exchange_list(halves)
    g_big = [jnp.where(c == 0, jnp.concatenate([h, t]), jnp.concatenate([t, h])) for h, t in zip(halves, theirs)]
    g_big[5] = g_big[5][:KS]

    big_w = (w_in[0], w_conv_out[0], w_out[0], w_ssm_glu[0], w_ssm_out[0], conv_w[0])
    big_m = (m_w_in[0], m_w_conv_out[0], m_w_out[0], m_w_ssm_glu[0], m_w_ssm_out[0], m_conv_w[0])
    big_v = (v_w_in[0], v_w_conv_out[0], v_w_out[0], v_w_ssm_glu[0], v_w_ssm_out[0], v_conv_w[0])
    big_names = ("w_in", "w_conv_out", "w_out", "w_ssm_glu", "w_ssm_out", "conv_w")
    big_out = {}
    for n, w, g, m, v in zip(big_names, big_w, g_big, big_m, big_v):
        d, m2, v2 = _adamw("adamw_" + n, w, g, m, v)
        big_out[n] = (g[None], d[None], m2[None], v2[None])

    small_m = (m_pre_norm_gain, m_conv_b, m_conv_ln_gain, m_conv_ln_bias, m_ssm_lambda_re, m_ssm_lambda_im, m_ssm_log_dt,
               m_ssm_b_re, m_ssm_b_im, m_ssm_c_re, m_ssm_c_im, m_ssm_d, m_b_ssm_glu, m_post_norm_gain)
    small_v = (v_pre_norm_gain, v_conv_b, v_conv_ln_gain, v_conv_ln_bias, v_ssm_lambda_re, v_ssm_lambda_im, v_ssm_log_dt,
               v_ssm_b_re, v_ssm_b_im, v_ssm_c_re, v_ssm_c_im, v_ssm_d, v_b_ssm_glu, v_post_norm_gain)
    sd, sm, sv = _adamw("adamw_small", _pack_small(small), fs, _pack_small(small_m), _pack_small(small_v))
    sg_l, loss = _unpack_small(fs)
    sd_l, _ = _unpack_small(sd)
    sm_l, _ = _unpack_small(sm)
    sv_l, _ = _unpack_small(sv)
    small_out = {n: (sg_l[i], sd_l[i], sm_l[i], sv_l[i]) for i, (n, _) in enumerate(_SMALL)}

    order = ("pre_norm_gain", "w_in", "conv_w", "conv_b", "conv_ln_gain", "conv_ln_bias", "w_conv_out", "ssm_lambda_re",
             "ssm_lambda_im", "ssm_log_dt", "ssm_b_re", "ssm_b_im", "ssm_c_re", "ssm_c_im", "ssm_d", "w_ssm_glu",
             "b_ssm_glu", "w_ssm_out", "w_out", "post_norm_gain")
    res = {**big_out, **small_out}
    outs = [loss, grad_x[None]]
    for k in range(4):
        outs.extend(res[n][k] for n in order)
    return tuple(outs)
```

```python
import math

import numpy as np
import jax
import jax.numpy as jnp
from jax import lax
from jax.experimental import pallas as pl
from jax.experimental.pallas import tpu as pltpu

f32 = jnp.float32
bf16 = jnp.bfloat16

D = 1024
CW = 1024
SW = 512
G = 32
H = 16
PST = 64
NS = G * PST
KS = 31
IN_W = 6144
NCHIP = 4
SHARD_W = IN_W // NCHIP
RMS_EPS = 1e-6
LN_EPS = 1e-5
LR, B1, B2, EPS, WD, STEP = 0.001, 0.9, 0.999, 1e-08, 0.01, 10
GELU_K0 = math.sqrt(2.0 / math.pi)
GELU_K1 = 0.044715

TC = 512
R = TC // 8
NH = 32
LBW = 512
REST_ROWS = 768
CONV_ROWS = 64
SMALL_ROWS = 1152
VMEM_LIMIT = 56 * 1024 * 1024
MESH = pl.DeviceIdType.MESH


def _cp(*sem):
    return pltpu.CompilerParams(dimension_semantics=tuple(sem), vmem_limit_bytes=VMEM_LIMIT)


def _sig(v):
    return 1.0 / (1.0 + jnp.exp(-v))


def _dot(a, b):
    return jnp.dot(a, b, preferred_element_type=f32)


def _dot_nt(a, b):
    return lax.dot_general(a, b, (((1,), (1,)), ((), ())), preferred_element_type=f32)


def _dot_tn(a, b):
    return lax.dot_general(a, b, (((0,), (0,)), ((), ())), preferred_element_type=f32)


def _full(shape):
    nd = len(shape)
    return pl.BlockSpec(shape, lambda *_: (0,) * nd)


def _rows8(i):
    return pl.ds(pl.multiple_of(i * 8, 8), 8)


def _prenorm(x, g_pre):
    L = x.shape[0]
    tm = 512
    def body(x_ref, g_ref, h_ref):
        xt = x_ref[...]
        r = lax.rsqrt(jnp.mean(xt * xt, axis=-1, keepdims=True) + RMS_EPS)
        h_ref[...] = (xt * r * g_ref[...]).astype(bf16)
    return pl.pallas_call(
        body, grid=(L // tm,),
        in_specs=[pl.BlockSpec((tm, D), lambda i: (i, 0)), _full((1, D))],
        out_specs=pl.BlockSpec((tm, D), lambda i: (i, 0)),
        out_shape=jax.ShapeDtypeStruct((L, D), bf16),
        name="prenorm", compiler_params=_cp("arbitrary"))(x, g_pre)


def _proj_fwd(h, w_in):
    L = h.shape[0]
    tm = 512
    def body(h_ref, w_ref, o_ref):
        o_ref[...] = _dot(h_ref[...], w_ref[0]).astype(bf16)
    return pl.pallas_call(
        body, grid=(NCHIP, L // tm),
        in_specs=[pl.BlockSpec((tm, D), lambda j, i: (i, 0)), pl.BlockSpec((1, D, SHARD_W), lambda j, i: (j, 0, 0))],
        out_specs=pl.BlockSpec((tm, SHARD_W), lambda j, i: (i, j)),
        out_shape=jax.ShapeDtypeStruct((L, IN_W), bf16),
        name="proj_fwd", compiler_params=_cp("arbitrary", "arbitrary"))(h, w_in)


NLB = CW // 128
RPI = 4


def _put_blocked(buf, row0, nrows, v):
    for lb in range(NLB):
        buf[lb, pl.ds(row0, nrows), :] = v[:, lb * 128:(lb + 1) * 128]


def _get_blocked(buf, row0, nrows):
    return jnp.concatenate([buf[lb, pl.ds(row0, nrows), :] for lb in range(NLB)], axis=1)


def _fill_before(ebuf, prev):
    sub = lax.broadcasted_iota(jnp.int32, (8, 128), 0)
    def halo(p, carry):
        for lb in range(NLB):
            cur = ebuf[lb, _rows8(R + p), :]
            ebuf[lb, _rows8(p), :] = jnp.where(sub == 0, pltpu.roll(prev[lb, _rows8(p), :], 1, 0),
                                               pltpu.roll(cur, 1, 0))
        return carry
    lax.fori_loop(0, NH, halo, 0)


def _fir(buf, lb, r, coef, first, flip):
    win = buf[lb, pl.ds(pl.multiple_of(r * 8, 8), (KS + RPI - 1) * 8), :]
    outs = []
    for i in range(RPI):
        acc = [first, None, None, None]
        for k in range(KS):
            o = i + ((KS - 1 - k) if flip else k)
            t = coef[k] * win[8 * o:8 * o + 8, :]
            acc[k % 4] = t if acc[k % 4] is None else acc[k % 4] + t
        outs.append((acc[0] + acc[1]) + (acc[2] + acc[3]))
    return outs


def _conv_fwd(proj, cw, cbias, lng, lnb):
    L = proj.shape[0]
    nc = L // TC
    def body(ca_ref, cb_ref, zc_ref, w_ref, b_ref, g_ref, bb_ref, cu1_ref, ain_ref, ebuf, prev, cacc):
        @pl.when(pl.program_id(0) == 0)
        def _():
            prev[...] = jnp.zeros_like(prev)
        def glu(s, carry):
            rows = pl.ds(pl.multiple_of(s * 64, 64), 64)
            _put_blocked(ebuf, pl.multiple_of(NH * 8 + s * 64, 64), 64,
                         ca_ref[rows, :].astype(f32) * _sig(cb_ref[rows, :].astype(f32)))
            return carry
        lax.fori_loop(0, TC // 64, glu, 0)
        _fill_before(ebuf, prev)
        prev[...] = ebuf[:, R * 8:(NH + R) * 8, :]
        for lb in range(NLB):
            sl = slice(lb * 128, (lb + 1) * 128)
            wk = [jnp.broadcast_to(w_ref[k:k + 1, sl], (8, 128)) for k in range(KS)]
            bias = jnp.broadcast_to(b_ref[:, sl], (8, 128))
            def tap(q, carry, lb=lb, wk=wk, bias=bias):
                r = q * RPI
                for i, o in enumerate(_fir(ebuf, lb, r + (NH - KS + 1), wk, bias, False)):
                    cacc[lb, _rows8(r + i), :] = o
                return carry
            lax.fori_loop(0, R // RPI, tap, 0)
        def norm(s, carry):
            rows = pl.ds(pl.multiple_of(s * 64, 64), 64)
            c1b = _get_blocked(cacc, pl.multiple_of(s * 64, 64), 64).astype(bf16)
            cu1_ref[rows, :] = c1b
            c1 = c1b.astype(f32)
            xc = c1 - jnp.mean(c1, axis=-1, keepdims=True)
            var = jnp.mean(xc * xc, axis=-1, keepdims=True)
            ln = xc * lax.rsqrt(var + LN_EPS) * g_ref[...] + bb_ref[...]
            zc = zc_ref[rows, :].astype(f32)
            ain_ref[rows, :] = ((ln * _sig(ln)) * (zc * _sig(zc))).astype(bf16)
            return carry
        lax.fori_loop(0, TC // 64, norm, 0)

    col = lambda c: pl.BlockSpec((TC, CW), lambda i, c=c: (i, c))
    return pl.pallas_call(
        body, grid=(nc,),
        in_specs=[col(0), col(1), col(2), _full((32, CW)), _full((1, CW)), _full((1, CW)), _full((1, CW))],
        out_specs=[pl.BlockSpec((TC, CW), lambda i: (i, 0)), pl.BlockSpec((TC, CW), lambda i: (i, 0))],
        out_shape=[jax.ShapeDtypeStruct((L, CW), bf16), jax.ShapeDtypeStruct((L, CW), bf16)],
        scratch_shapes=[pltpu.VMEM((NLB, (NH + R) * 8, 128), f32), pltpu.VMEM((NLB, NH * 8, 128), f32),
                        pltpu.VMEM((NLB, TC, 128), f32)],
        name="conv_fwd", compiler_params=_cp("arbitrary"))(proj, proj, proj, cw, cbias, lng, lnb)


def _gelu_parts(y0):
    t = jnp.tanh(GELU_K0 * (y0 + GELU_K1 * y0 * y0 * y0))
    return t, 0.5 * y0 * (1.0 + t)


def _ssm_fwd(proj, bbt_re, bbt_im, ct_re, ct_im, a_re, a_im, apow_re, apow_im, dvec, wglu, bglu):
    L = proj.shape[0]
    nc = L // TC
    def body(u_ref, zs_ref, bre_ref, bim_ref, cre_ref, cim_ref, are_ref, aim_ref, pwr_ref, pwi_ref,
             d_ref, wg_ref, bg_ref, y0_ref, bin_ref, sre, sim, cinr, cini, prev_re, prev_im):
        c = pl.program_id(0)
        @pl.when(c == 0)
        def _():
            prev_re[...] = jnp.zeros_like(prev_re)
            prev_im[...] = jnp.zeros_like(prev_im)
        u = u_ref[...]
        for blk in range(4):
            ub = u[:, 128 * blk:128 * (blk + 1)]
            sre[:, 512 * blk:512 * (blk + 1)] = _dot(ub, bre_ref[blk])
            sim[:, 512 * blk:512 * (blk + 1)] = _dot(ub, bim_ref[blk])
        for lb in range(NS // LBW):
            sl = slice(lb * LBW, (lb + 1) * LBW)
            ar = jnp.broadcast_to(are_ref[:, sl], (8, LBW))
            ai = jnp.broadcast_to(aim_ref[:, sl], (8, LBW))
            def step(r, carry, sl=sl, ar=ar, ai=ai):
                sr, si = carry
                nr = ar * sr - ai * si + sre[_rows8(r), sl]
                ni = ar * si + ai * sr + sim[_rows8(r), sl]
                sre[_rows8(r), sl] = nr
                sim[_rows8(r), sl] = ni
                return nr, ni
            lax.fori_loop(1, R, step, (sre[0:8, sl], sim[0:8, sl]))
        a_r = pwr_ref[R - 1:R, :]
        a_i = pwi_ref[R - 1:R, :]
        cr = prev_re[0:1, :]
        ci = prev_im[0:1, :]
        for seg in range(8):
            cinr[seg:seg + 1, :] = cr
            cini[seg:seg + 1, :] = ci
            er = sre[8 * (R - 1) + seg:8 * (R - 1) + seg + 1, :]
            ei = sim[8 * (R - 1) + seg:8 * (R - 1) + seg + 1, :]
            cr, ci = er + a_r * cr - a_i * ci, ei + a_r * ci + a_i * cr
        prev_re[0:1, :] = cr
        prev_im[0:1, :] = ci
        for lb in range(NS // LBW):
            sl = slice(lb * LBW, (lb + 1) * LBW)
            kr = cinr[:, sl]
            ki = cini[:, sl]
            def fix(r, carry, sl=sl, kr=kr, ki=ki):
                pr = jnp.broadcast_to(pwr_ref[pl.ds(r, 1), sl], (8, LBW))
                pi = jnp.broadcast_to(pwi_ref[pl.ds(r, 1), sl], (8, LBW))
                sre[_rows8(r), sl] = sre[_rows8(r), sl] + pr * kr - pi * ki
                sim[_rows8(r), sl] = sim[_rows8(r), sl] + pr * ki + pi * kr
                return carry
            lax.fori_loop(0, R, fix, 0)
        yp = []
        for blk in range(4):
            sr = sre[:, 512 * blk:512 * (blk + 1)].astype(bf16)
            si = sim[:, 512 * blk:512 * (blk + 1)].astype(bf16)
            yp.append(_dot(sr, cre_ref[blk]) - _dot(si, cim_ref[blk]))
        y0 = jnp.concatenate(yp, axis=1) + d_ref[...] * u.astype(f32)
        y0_ref[...] = y0
        _, y1 = _gelu_parts(y0)
        glu = _dot(y1.astype(bf16), wg_ref[...]) + bg_ref[...]
        y2 = y1 * _sig(glu)
        zs = zs_ref[...].astype(f32)
        bin_ref[...] = (y2 * (zs * _sig(zs))).astype(bf16)

    return pl.pallas_call(
        body, grid=(nc,),
        in_specs=[pl.BlockSpec((TC, SW), lambda c: (c, 6)), pl.BlockSpec((TC, SW), lambda c: (c, 7)),
                  _full((4, 128, 512)), _full((4, 128, 512)), _full((4, 512, 128)), _full((4, 512, 128)),
                  _full((1, NS)), _full((1, NS)), _full((R, NS)), _full((R, NS)),
                  _full((1, SW)), _full((SW, SW)), _full((1, SW))],
        out_specs=[pl.BlockSpec((TC, SW), lambda c: (c, 0)), pl.BlockSpec((TC, SW), lambda c: (c, 0)),
                   pl.BlockSpec((TC, NS), lambda c: (c, 0)), pl.BlockSpec((TC, NS), lambda c: (c, 0)),
                   pl.BlockSpec((8, NS), lambda c: (c, 0)), pl.BlockSpec((8, NS), lambda c: (c, 0))],
        out_shape=[jax.ShapeDtypeStruct((L, SW), f32), jax.ShapeDtypeStruct((L, SW), bf16),
                   jax.ShapeDtypeStruct((L, NS), f32), jax.ShapeDtypeStruct((L, NS), f32),
                   jax.ShapeDtypeStruct((nc * 8, NS), f32), jax.ShapeDtypeStruct((nc * 8, NS), f32)],
        scratch_shapes=[pltpu.VMEM((8, NS), f32), pltpu.VMEM((8, NS), f32)],
        name="ssm_fwd", compiler_params=_cp("arbitrary"))(
            proj, proj, bbt_re, bbt_im, ct_re, ct_im, a_re, a_im, apow_re, apow_im, dvec, wglu, bglu)


def _tail(a_in, b_in, proj, x, tgt, wco, wso, wout, gpost):
    L = x.shape[0]
    tm = 256
    def body(a_ref, b_ref, gc_ref, gs_ref, x_ref, t_ref, wco_ref, wso_ref, wout_ref, gp_ref,
             gx_ref, dain_ref, dbin_ref, dp_ref, dwout_ref, dwco_ref, dwso_ref, dgp_ref, loss_ref):
        @pl.when(pl.program_id(0) == 0)
        def _():
            dwout_ref[...] = jnp.zeros_like(dwout_ref)
            dwco_ref[...] = jnp.zeros_like(dwco_ref)
            dwso_ref[...] = jnp.zeros_like(dwso_ref)
            dgp_ref[...] = jnp.zeros_like(dgp_ref)
            loss_ref[...] = jnp.zeros_like(loss_ref)
        a = a_ref[...]
        b = b_ref[...]
        co = _dot(a, wco_ref[...])
        so = jnp.concatenate([_dot(b, wso_ref[j]) for j in range(NCHIP)], axis=1)
        sc = _sig(gc_ref[...].astype(f32))
        ss = _sig(gs_ref[...].astype(f32))
        mb = (sc * co + ss * so).astype(bf16)
        out = _dot(mb, wout_ref[...])
        r2 = lax.rsqrt(jnp.mean(out * out, axis=-1, keepdims=True) + RMS_EPS)
        on = out * r2
        gp = gp_ref[...]
        e = x_ref[...] + on * gp - t_ref[...]
        loss_ref[...] += (0.5 / D) * jnp.sum(e * e)
        dy = e * (1.0 / D)
        gx_ref[...] = dy
        dgp_ref[...] += jnp.sum(dy * on, axis=0, keepdims=True)
        dn = dy * gp
        dout = (r2 * (dn - on * jnp.mean(dn * on, axis=-1, keepdims=True))).astype(bf16)
        dwout_ref[...] += _dot_tn(mb, dout)
        dm = _dot_nt(dout, wout_ref[...])
        dp_ref[:, 0:D] = (dm * co * sc * (1.0 - sc)).astype(bf16)
        dp_ref[:, D:2 * D] = (dm * so * ss * (1.0 - ss)).astype(bf16)
        dco = (dm * sc).astype(bf16)
        dso = (dm * ss).astype(bf16)
        dwco_ref[...] += _dot_tn(a, dco)
        dbin = None
        for j in range(NCHIP):
            dso_j = dso[:, j * 256:(j + 1) * 256]
            dwso_ref[j] += _dot_tn(b, dso_j)
            t = _dot_nt(dso_j, wso_ref[j])
            dbin = t if dbin is None else dbin + t
        dain_ref[...] = _dot_nt(dco, wco_ref[...]).astype(bf16)
        dbin_ref[...] = dbin.astype(bf16)

    row = lambda w: pl.BlockSpec((tm, w), lambda i: (i, 0))
    one = lambda shape: pl.BlockSpec(shape, lambda i: (0,) * len(shape), pipeline_mode=pl.Buffered(1))
    return pl.pallas_call(
        body, grid=(L // tm,),
        in_specs=[row(CW), row(SW), pl.BlockSpec((tm, D), lambda i: (i, 4)), pl.BlockSpec((tm, D), lambda i: (i, 5)),
                  row(D), row(D), one((CW, D)), one((NCHIP, SW, 256)), one((D, D)), one((1, D))],
        out_specs=[row(D), row(CW), row(SW), pl.BlockSpec((tm, 2 * D), lambda i: (i, 2)),
                   one((D, D)), one((CW, D)), one((NCHIP, SW, 256)), one((1, D)), one((1, 128))],
        out_shape=[jax.ShapeDtypeStruct((L, D), f32), jax.ShapeDtypeStruct((L, CW), bf16),
                   jax.ShapeDtypeStruct((L, SW), bf16), jax.ShapeDtypeStruct((L, IN_W), bf16),
                   jax.ShapeDtypeStruct((D, D), f32), jax.ShapeDtypeStruct((CW, D), f32),
                   jax.ShapeDtypeStruct((NCHIP, SW, 256), f32), jax.ShapeDtypeStruct((1, D), f32),
                   jax.ShapeDtypeStruct((1, 128), f32)],
        name="tail", compiler_params=_cp("arbitrary"))(a_in, b_in, proj, proj, x, tgt, wco, wso, wout, gpost)


def _ssm_bwd(d_bin, y0, proj, sre, sim, cinr, cini, bbt_re, bbt_im, ct_re, ct_im,
             a_re, a_im, apow_re, apow_im, dvec, wglu, bglu, dproj):
    L = y0.shape[0]
    nc = L // TC
    def body(dbin_ref, y0_ref, u_ref, zs_ref, sre_ref, sim_ref, cinr_ref, cini_ref,
             bre_ref, bim_ref, cre_ref, cim_ref, are_ref, aim_ref, pwr_ref, pwi_ref, d_ref, wg_ref, bg_ref, _,
             dp_ref, dbre_ref, dbim_ref, dcre_ref, dcim_ref, dd_ref, dar_ref, dai_ref, dwg_ref, dbg_ref,
             gre, gim, gcr, gci, nxt_re, nxt_im):
        @pl.when(pl.program_id(0) == 0)
        def _():
            for ref in (dbre_ref, dbim_ref, dcre_ref, dcim_ref, dd_ref, dar_ref, dai_ref, dwg_ref, dbg_ref,
                        nxt_re, nxt_im):
                ref[...] = jnp.zeros_like(ref)
        y0 = y0_ref[...]
        u = u_ref[...]
        zs = zs_ref[...].astype(f32)
        dbin = dbin_ref[...].astype(f32)
        t, y1 = _gelu_parts(y0)
        y1b = y1.astype(bf16)
        sg = _sig(_dot(y1b, wg_ref[...]) + bg_ref[...])
        sz = _sig(zs)
        d_y2 = dbin * (zs * sz)
        dp_ref[:, SW:2 * SW] = (dbin * (y1 * sg) * (sz * (1.0 + zs * (1.0 - sz)))).astype(bf16)
        d_glu = d_y2 * y1 * sg * (1.0 - sg)
        d_glub = d_glu.astype(bf16)
        d_y1 = d_y2 * sg + _dot_nt(d_glub, wg_ref[...])
        dwg_ref[...] += _dot_tn(y1b, d_glub)
        dbg_ref[...] += jnp.sum(d_glu, axis=0, keepdims=True)
        dgelu = 0.5 * (1.0 + t) + 0.5 * y0 * (1.0 - t * t) * GELU_K0 * (1.0 + 3.0 * GELU_K1 * y0 * y0)
        d_y0 = d_y1 * dgelu
        dd_ref[...] += jnp.sum(d_y0 * u.astype(f32), axis=0, keepdims=True)
        dyb = d_y0.astype(bf16)
        for blk in range(4):
            dy1 = dyb[:, 128 * blk:128 * (blk + 1)]
            gre[:, 512 * blk:512 * (blk + 1)] = _dot_nt(dy1, cre_ref[blk])
            gim[:, 512 * blk:512 * (blk + 1)] = -_dot_nt(dy1, cim_ref[blk])
        for lb in range(NS // LBW):
            sl = slice(lb * LBW, (lb + 1) * LBW)
            ar = jnp.broadcast_to(are_ref[:, sl], (8, LBW))
            ai = jnp.broadcast_to(aim_ref[:, sl], (8, LBW))
            def step(k, carry, sl=sl, ar=ar, ai=ai):
                gr, gi = carry
                row = _rows8(R - 2 - k)
                nr = ar * gr + ai * gi + gre[row, sl]
                ni = ar * gi - ai * gr + gim[row, sl]
                gre[row, sl] = nr
                gim[row, sl] = ni
                return nr, ni
            lax.fori_loop(0, R - 1, step, (gre[8 * (R - 1):8 * R, sl], gim[8 * (R - 1):8 * R, sl]))
        a_r = pwr_ref[R - 1:R, :]
        a_i = pwi_ref[R - 1:R, :]
        cr = nxt_re[0:1, :]
        ci = nxt_im[0:1, :]
        for seg in range(7, -1, -1):
            gcr[seg:seg + 1, :] = cr
            gci[seg:seg + 1, :] = ci
            er = gre[seg:seg + 1, :]
            ei = gim[seg:seg + 1, :]
            cr, ci = er + a_r * cr + a_i * ci, ei + a_r * ci - a_i * cr
        nxt_re[0:1, :] = cr
        nxt_im[0:1, :] = ci
        for lb in range(NS // LBW):
            sl = slice(lb * LBW, (lb + 1) * LBW)
            kr = gcr[:, sl]
            ki = gci[:, sl]
            def fix(r, carry, sl=sl, kr=kr, ki=ki):
                pr = jnp.broadcast_to(pwr_ref[pl.ds(R - 1 - r, 1), sl], (8, LBW))
                pi = jnp.broadcast_to(pwi_ref[pl.ds(R - 1 - r, 1), sl], (8, LBW))
                gre[_rows8(r), sl] = gre[_rows8(r), sl] + pr * kr + pi * ki
                gim[_rows8(r), sl] = gim[_rows8(r), sl] + pr * ki - pi * kr
                return carry
            lax.fori_loop(0, R, fix, 0)
        dup = []
        for blk in range(4):
            s4 = slice(512 * blk, 512 * (blk + 1))
            s1 = slice(128 * blk, 128 * (blk + 1))
            grb = gre[:, s4].astype(bf16)
            gib = gim[:, s4].astype(bf16)
            dup.append(_dot_nt(grb, bre_ref[blk]) + _dot_nt(gib, bim_ref[blk]))
            dbre_ref[blk] += _dot_tn(u[:, s1], grb)
            dbim_ref[blk] += _dot_tn(u[:, s1], gib)
            dcre_ref[blk] += _dot_tn(sre_ref[:, s4].astype(bf16), dyb[:, s1])
            dcim_ref[blk] -= _dot_tn(sim_ref[:, s4].astype(bf16), dyb[:, s1])
        dp_ref[:, 0:SW] = (jnp.concatenate(dup, axis=1) + d_ref[...] * d_y0).astype(bf16)
        for lb in range(NS // LBW):
            sl = slice(lb * LBW, (lb + 1) * LBW)
            g0r, g0i = gre[0:8, sl], gim[0:8, sl]
            p0r, p0i = cinr_ref[:, sl], cini_ref[:, sl]
            acc0 = (g0r * p0r + g0i * p0i, g0i * p0r - g0r * p0i)
            def dacc(r, carry, sl=sl):
                xr, xi = carry
                gr, gi = gre[_rows8(r), sl], gim[_rows8(r), sl]
                pr, pi = sre_ref[_rows8(r - 1), sl], sim_ref[_rows8(r - 1), sl]
                return xr + gr * pr + gi * pi, xi + gi * pr - gr * pi
            xr, xi = lax.fori_loop(1, R, dacc, acc0)
            dar_ref[:, sl] += xr
            dai_ref[:, sl] += xi

    rev = lambda w, cidx: pl.BlockSpec((TC, w), lambda i, cidx=cidx: (nc - 1 - i, cidx))
    one = lambda shape: pl.BlockSpec(shape, lambda i: (0,) * len(shape))
    return pl.pallas_call(
        body, grid=(nc,),
        in_specs=[rev(SW, 0), rev(SW, 0), rev(SW, 6), rev(SW, 7), rev(NS, 0), rev(NS, 0),
                  pl.BlockSpec((8, NS), lambda i: (nc - 1 - i, 0)), pl.BlockSpec((8, NS), lambda i: (nc - 1 - i, 0)),
                  one((4, 128, 512)), one((4, 128, 512)), one((4, 512, 128)), one((4, 512, 128)),
                  one((1, NS)), one((1, NS)), one((R, NS)), one((R, NS)),
                  one((1, SW)), one((SW, SW)), one((1, SW)), _ANY],
        out_specs=[pl.BlockSpec((TC, 2 * SW), lambda i: (nc - 1 - i, 3)),
                   one((4, 128, 512)), one((4, 128, 512)), one((4, 512, 128)), one((4, 512, 128)),
                   one((1, SW)), one((8, NS)), one((8, NS)), one((SW, SW)), one((1, SW))],
        out_shape=[jax.ShapeDtypeStruct((L, IN_W), bf16),
                   jax.ShapeDtypeStruct((4, 128, 512), f32), jax.ShapeDtypeStruct((4, 128, 512), f32),
                   jax.ShapeDtypeStruct((4, 512, 128), f32), jax.ShapeDtypeStruct((4, 512, 128), f32),
                   jax.ShapeDtypeStruct((1, SW), f32), jax.ShapeDtypeStruct((8, NS), f32),
                   jax.ShapeDtypeStruct((8, NS), f32), jax.ShapeDtypeStruct((SW, SW), f32),
                   jax.ShapeDtypeStruct((1, SW), f32)],
        scratch_shapes=[pltpu.VMEM((TC, NS), f32), pltpu.VMEM((TC, NS), f32), pltpu.VMEM((8, NS), f32),
                        pltpu.VMEM((8, NS), f32), pltpu.VMEM((8, NS), f32), pltpu.VMEM((8, NS), f32)],
        input_output_aliases={19: 0},
        name="ssm_bwd", compiler_params=_cp("arbitrary"))(
            d_bin, y0, proj, proj, sre, sim, cinr, cini, bbt_re, bbt_im, ct_re, ct_im,
            a_re, a_im, apow_re, apow_im, dvec, wglu, bglu, dproj)


def _conv_bwd(d_ain, cu1, proj, cw, lng, lnb, dproj):
    L = cu1.shape[0]
    nc = L // TC
    def body(dain_ref, cu1_ref, ca_ref, cb_ref, zc_ref, cah_ref, cbh_ref, w_ref, g_ref, bb_ref, _,
             dp_ref, dw_ref, dbias_ref, dlng_ref, dlnb_ref, dbuf, ebuf, prev, nxt, dcu0):
        i = pl.program_id(0)
        @pl.when(i == 0)
        def _():
            dw_ref[...] = jnp.zeros_like(dw_ref)
            dbias_ref[...] = jnp.zeros_like(dbias_ref)
            dlng_ref[...] = jnp.zeros_like(dlng_ref)
            dlnb_ref[...] = jnp.zeros_like(dlnb_ref)
            nxt[...] = jnp.zeros_like(nxt)
        def lnb(s, carry):
            rows = pl.ds(pl.multiple_of(s * 32, 32), 32)
            dain = dain_ref[rows, :].astype(f32)
            c1 = cu1_ref[rows, :].astype(f32)
            zc = zc_ref[rows, :].astype(f32)
            xc = c1 - jnp.mean(c1, axis=-1, keepdims=True)
            var = jnp.mean(xc * xc, axis=-1, keepdims=True)
            rstd = lax.rsqrt(var + LN_EPS)
            xh = xc * rstd
            ln = xh * g_ref[...] + bb_ref[...]
            sl_ = _sig(ln)
            sz = _sig(zc)
            dp_ref[rows, 2 * CW:3 * CW] = (dain * (ln * sl_) * (sz * (1.0 + zc * (1.0 - sz)))).astype(bf16)
            d_ln = dain * (zc * sz) * (sl_ * (1.0 + ln * (1.0 - sl_)))
            dlng_ref[...] += jnp.sum(d_ln * xh, axis=0, keepdims=True)
            dlnb_ref[...] += jnp.sum(d_ln, axis=0, keepdims=True)
            dxh = d_ln * g_ref[...]
            d_c1 = rstd * (dxh - jnp.mean(dxh, axis=-1, keepdims=True)
                           - xh * jnp.mean(dxh * xh, axis=-1, keepdims=True))
            dbias_ref[...] += jnp.sum(d_c1, axis=0, keepdims=True)
            _put_blocked(dbuf, pl.multiple_of(s * 32, 32), 32, d_c1)
            _put_blocked(ebuf, pl.multiple_of(NH * 8 + s * 32, 32), 32,
                         ca_ref[rows, :].astype(f32) * _sig(cb_ref[rows, :].astype(f32)))
            return carry
        lax.fori_loop(0, TC // 32, lnb, 0)
        sub = lax.broadcasted_iota(jnp.int32, (8, 128), 0)
        def after(p, carry):
            for lb in range(NLB):
                cur = dbuf[lb, _rows8(p), :]
                dbuf[lb, _rows8(R + p), :] = jnp.where(sub == 7, pltpu.roll(nxt[lb, _rows8(p), :], 7, 0),
                                                       pltpu.roll(cur, 7, 0))
            return carry
        lax.fori_loop(0, NH, after, 0)
        nxt[...] = dbuf[:, 0:NH * 8, :]
        def before(s, carry):
            rows = pl.ds(pl.multiple_of(s * 64, 64), 64)
            v = cah_ref[rows, :].astype(f32) * _sig(cbh_ref[rows, :].astype(f32))
            _put_blocked(prev, pl.multiple_of(s * 64, 64), 64, jnp.where(i == nc - 1, jnp.zeros_like(v), v))
            return carry
        lax.fori_loop(0, NH * 8 // 64, before, 0)
        _fill_before(ebuf, prev)
        for lb in range(NLB):
            sl = slice(lb * 128, (lb + 1) * 128)
            wk = [jnp.broadcast_to(w_ref[k:k + 1, sl], (8, 128)) for k in range(KS)]
            def tap(q, carry, lb=lb, wk=wk):
                r = q * RPI
                for j, o in enumerate(_fir(dbuf, lb, r, wk, None, True)):
                    dcu0[lb, _rows8(r + j), :] = o
                return carry
            lax.fori_loop(0, R // RPI, tap, 0)
            def wgrad(q, accs, lb=lb):
                r = q * RPI
                dvs = dbuf[lb, pl.ds(pl.multiple_of(r * 8, 8), RPI * 8), :]
                win = ebuf[lb, pl.ds(pl.multiple_of((r + (NH - KS + 1)) * 8, 8), (KS + RPI - 1) * 8), :]
                accs = list(accs)
                for j in range(RPI):
                    dv = dvs[8 * j:8 * j + 8, :]
                    for k in range(KS):
                        accs[k] = accs[k] + dv * win[8 * (j + k):8 * (j + k) + 8, :]
                return tuple(accs)
            accs = lax.fori_loop(0, R // RPI, wgrad, tuple(jnp.zeros((8, 128), f32) for _ in range(KS)))
            for k in range(KS):
                dw_ref[k, :, sl] += accs[k]
        def glub(s, carry):
            rows = pl.ds(pl.multiple_of(s * 64, 64), 64)
            d0 = _get_blocked(dcu0, pl.multiple_of(s * 64, 64), 64)
            ca = ca_ref[rows, :].astype(f32)
            sb = _sig(cb_ref[rows, :].astype(f32))
            dp_ref[rows, 0:CW] = (d0 * sb).astype(bf16)
            dp_ref[rows, CW:2 * CW] = (d0 * ca * sb * (1.0 - sb)).astype(bf16)
            return carry
        lax.fori_loop(0, TC // 64, glub, 0)

    hrows = NH * 8
    per = TC // hrows
    rev = lambda cidx: pl.BlockSpec((TC, CW), lambda i, cidx=cidx: (nc - 1 - i, cidx))
    halo = lambda cidx: pl.BlockSpec((hrows, CW), lambda i, cidx=cidx: (jnp.maximum((nc - 1 - i) * per - 1, 0), cidx))
    one = lambda shape: pl.BlockSpec(shape, lambda i: (0,) * len(shape))
    return pl.pallas_call(
        body, grid=(nc,),
        in_specs=[rev(0), rev(0), rev(0), rev(1), rev(2), halo(0), halo(1), one((32, CW)), one((1, CW)), one((1, CW)),
                  _ANY],
        out_specs=[pl.BlockSpec((TC, 3 * CW), lambda i: (nc - 1 - i, 0)), one((32, 8, CW)), one((1, CW)), one((1, CW)), one((1, CW))],
        out_shape=[jax.ShapeDtypeStruct((L, IN_W), bf16), jax.ShapeDtypeStruct((32, 8, CW), f32),
                   jax.ShapeDtypeStruct((1, CW), f32), jax.ShapeDtypeStruct((1, CW), f32),
                   jax.ShapeDtypeStruct((1, CW), f32)],
        scratch_shapes=[pltpu.VMEM((NLB, (R + NH) * 8, 128), f32), pltpu.VMEM((NLB, (NH + R) * 8, 128), f32),
                        pltpu.VMEM((NLB, hrows, 128), f32), pltpu.VMEM((NLB, hrows, 128), f32),
                        pltpu.VMEM((NLB, TC, 128), f32)],
        input_output_aliases={10: 0},
        name="conv_bwd", compiler_params=_cp("arbitrary"))(d_ain, cu1, proj, proj, proj, proj, proj, cw, lng, lnb, dproj)


def _win_grad(h, dproj):
    L = h.shape[0]
    tm = 512
    def body(h_ref, d_ref, o_ref):
        @pl.when(pl.program_id(1) == 0)
        def _():
            o_ref[...] = jnp.zeros_like(o_ref)
        o_ref[0] += _dot_tn(h_ref[...], d_ref[...])
    return pl.pallas_call(
        body, grid=(NCHIP, L // tm),
        in_specs=[pl.BlockSpec((tm, D), lambda j, i: (i, 0)), pl.BlockSpec((tm, SHARD_W), lambda j, i: (i, j))],
        out_specs=pl.BlockSpec((1, D, SHARD_W), lambda j, i: (j, 0, 0)),
        out_shape=jax.ShapeDtypeStruct((NCHIP, D, SHARD_W), f32),
        name="win_grad", compiler_params=_cp("arbitrary", "arbitrary"))(h, dproj)


def _x_grad(dproj, w_in, x, gx0, g_pre):
    L = x.shape[0]
    tm = 256
    def body(d_ref, w_ref, x_ref, gx_ref, g_ref, o_ref, dg_ref):
        @pl.when(pl.program_id(0) == 0)
        def _():
            dg_ref[...] = jnp.zeros_like(dg_ref)
        dh = _dot_nt(d_ref[:, 0:SHARD_W], w_ref[0])
        for j in range(1, NCHIP):
            dh = dh + _dot_nt(d_ref[:, j * SHARD_W:(j + 1) * SHARD_W], w_ref[j])
        xt = x_ref[...]
        r = lax.rsqrt(jnp.mean(xt * xt, axis=-1, keepdims=True) + RMS_EPS)
        xn = xt * r
        dg_ref[...] += jnp.sum(dh * xn, axis=0, keepdims=True)
        dxn = dh * g_ref[...]
        o_ref[...] = gx_ref[...] + r * (dxn - xn * jnp.mean(dxn * xn, axis=-1, keepdims=True))
    return pl.pallas_call(
        body, grid=(L // tm,),
        in_specs=[pl.BlockSpec((tm, IN_W), lambda i: (i, 0)),
                  pl.BlockSpec((NCHIP, D, SHARD_W), lambda i: (0, 0, 0), pipeline_mode=pl.Buffered(1)),
                  pl.BlockSpec((tm, D), lambda i: (i, 0)), pl.BlockSpec((tm, D), lambda i: (i, 0)), _full((1, D))],
        out_specs=[pl.BlockSpec((tm, D), lambda i: (i, 0)), _full((1, D))],
        out_shape=[jax.ShapeDtypeStruct((L, D), f32), jax.ShapeDtypeStruct((1, D), f32)],
        name="x_grad", compiler_params=_cp("arbitrary"))(dproj, w_in, x, gx0, g_pre)


def _pair_sum(c_arr, ga, ra, gb, rb, gs, rs):
    def body(c_ref, ga_ref, ra_ref, gb_ref, rb_ref, gs_ref, rs_ref, pa_ref, pb_ref, ps_ref):
        pa_ref[...] = (ga_ref[...] + ra_ref[...]).astype(bf16)
        pb_ref[...] = (gb_ref[...] + rb_ref[...]).astype(bf16)
        ps_ref[...] = gs_ref[...] + rs_ref[...]
    grid_spec = pltpu.PrefetchScalarGridSpec(
        num_scalar_prefetch=1, grid=(NCHIP,),
        in_specs=[pl.BlockSpec((1, D // 2, SHARD_W), lambda j, c: (j, c[0], 0)),
                  pl.BlockSpec((1, D // 2, SHARD_W), lambda j, c: (j, 0, 0)),
                  pl.BlockSpec((1, REST_ROWS // 2, 1024), lambda j, c: (j, c[0], 0)),
                  pl.BlockSpec((1, REST_ROWS // 2, 1024), lambda j, c: (j, 0, 0)),
                  pl.BlockSpec((SMALL_ROWS, 128), lambda j, c: (0, 0)),
                  pl.BlockSpec((SMALL_ROWS, 128), lambda j, c: (0, 0))],
        out_specs=[pl.BlockSpec((1, D // 2, SHARD_W), lambda j, c: (j, 0, 0)),
                   pl.BlockSpec((1, REST_ROWS // 2, 1024), lambda j, c: (j, 0, 0)),
                   pl.BlockSpec((SMALL_ROWS, 128), lambda j, c: (0, 0))])
    return pl.pallas_call(
        body, grid_spec=grid_spec,
        out_shape=[jax.ShapeDtypeStruct((NCHIP, D // 2, SHARD_W), bf16),
                   jax.ShapeDtypeStruct((NCHIP, REST_ROWS // 2, 1024), bf16),
                   jax.ShapeDtypeStruct((SMALL_ROWS, 128), f32)],
        name="pair_sum", compiler_params=_cp("arbitrary"))(c_arr, ga, ra, gb, rb, gs, rs)


def _chip_sum(qa, qb, qs):
    nt = 4
    def body(qa_ref, qb_ref, qs_ref, fa_ref, fb_ref, fs_ref):
        for q_ref, f_ref in ((qa_ref, fa_ref), (qb_ref, fb_ref), (qs_ref, fs_ref)):
            acc = q_ref[0].astype(f32)
            for j in range(1, NCHIP):
                acc = acc + q_ref[j].astype(f32)
            f_ref[...] = acc
    ra, rb, rs = D // 2 // nt, REST_ROWS // 2 // nt, SMALL_ROWS // nt
    return pl.pallas_call(
        body, grid=(nt,),
        in_specs=[pl.BlockSpec((NCHIP, ra, SHARD_W), lambda i: (0, i, 0)),
                  pl.BlockSpec((NCHIP, rb, 1024), lambda i: (0, i, 0)),
                  pl.BlockSpec((NCHIP, rs, 128), lambda i: (0, i, 0))],
        out_specs=[pl.BlockSpec((ra, SHARD_W), lambda i: (i, 0)), pl.BlockSpec((rb, 1024), lambda i: (i, 0)),
                   pl.BlockSpec((rs, 128), lambda i: (i, 0))],
        out_shape=[jax.ShapeDtypeStruct((D // 2, SHARD_W), f32), jax.ShapeDtypeStruct((REST_ROWS // 2, 1024), f32),
                   jax.ShapeDtypeStruct((SMALL_ROWS, 128), f32)],
        name="chip_sum", compiler_params=_cp("arbitrary"))(qa, qb, qs)


def _adamw_math(w, g, m, v):
    m2 = B1 * m + (1.0 - B1) * g
    v2 = B2 * v + (1.0 - B2) * (g * g)
    m_hat = m2 / (1.0 - B1 ** STEP)
    v_hat = v2 / (1.0 - B2 ** STEP)
    delta = -LR * (m_hat / (jnp.sqrt(v_hat) + EPS) + WD * w)
    return delta, m2, v2


def _adamw(name, w, g, m, v):
    rows, cols = w.shape
    tm = rows if rows <= 256 else (256 if rows % 256 == 0 else 128)
    assert rows % tm == 0
    def body(w_ref, g_ref, m_ref, v_ref, d_ref, m2_ref, v2_ref):
        d, m2, v2 = _adamw_math(w_ref[...], g_ref[...], m_ref[...], v_ref[...])
        d_ref[...] = d
        m2_ref[...] = m2
        v2_ref[...] = v2
    spec = pl.BlockSpec((tm, cols), lambda i: (i, 0))
    shp = jax.ShapeDtypeStruct((rows, cols), f32)
    return pl.pallas_call(
        body, grid=(rows // tm,), in_specs=[spec] * 4, out_specs=[spec] * 3, out_shape=[shp] * 3,
        name=name, compiler_params=_cp("arbitrary"))(w, g, m, v)


_ANY = pl.BlockSpec(memory_space=pl.ANY)


def _chunks(rows, parts):
    step = rows // parts
    assert step * parts == rows and step % 16 == 0
    return [(i * step, step) for i in range(parts)]


def _place():
    x, y, c = lax.axis_index("x"), lax.axis_index("y"), lax.axis_index("c")
    chips = [(1 - x, y), (x, 1 - y), (1 - x, 1 - y)]
    return x, y, c, chips


def _gather_weights(win_s, rest_s):
    segs = [(0, D // 2, r0, n) for r0, n in _chunks(D // 2, 4)] + \
           [(1, REST_ROWS // 2, r0, n) for r0, n in _chunks(REST_ROWS // 2, 2)]
    ns = len(segs)
    def body(a_ref, b_ref, oa_ref, ob_ref, send_sems, recv_sems, local_sems):
        x, y, c, chips = _place()
        k = 2 * x + y
        sibling = (x, y, 1 - c)
        ins, outs = (a_ref, b_ref), (oa_ref, ob_ref)

        def dst(which, half, chip, pc, r0, n):
            return outs[which].at[chip, pl.ds(pc * half + r0, n), :]

        def rcopy(i, src, dst_ref, to):
            return pltpu.make_async_remote_copy(src_ref=src, dst_ref=dst_ref, send_sem=send_sems.at[i],
                                                recv_sem=recv_sems.at[i], device_id=to, device_id_type=MESH)

        own = [pltpu.make_async_copy(ins[w], outs[w].at[k], local_sems.at[w]) for w in range(2)]
        for cp in own:
            cp.start()
        first = []
        for j, chip in enumerate(chips):
            for s, (w, half, r0, n) in enumerate(segs):
                first.append(rcopy(j * ns + s, ins[w].at[pl.ds(c * half + r0, n), :], dst(w, half, k, c, r0, n),
                                   (*chip, c)))
        for cp in first:
            cp.start()
        passed = []
        for j, chip in enumerate(chips):
            cj = 2 * chip[0] + chip[1]
            for s, (w, half, r0, n) in enumerate(segs):
                landed = dst(w, half, cj, c, r0, n)
                rcopy(j * ns + s, landed, landed, (x, y, c)).wait_recv()
                fwd = rcopy(3 * ns + j * ns + s, landed, landed, sibling)
                fwd.start()
                passed.append(fwd)
        for j, chip in enumerate(chips):
            cj = 2 * chip[0] + chip[1]
            for s, (w, half, r0, n) in enumerate(segs):
                theirs = dst(w, half, cj, 1 - c, r0, n)
                rcopy(3 * ns + j * ns + s, theirs, theirs, (x, y, c)).wait_recv()
        for cp in first + passed:
            cp.wait_send()
        for cp in own:
            cp.wait()

    return pl.pallas_call(
        body, in_specs=[_ANY, _ANY], out_specs=[_ANY, _ANY],
        out_shape=[jax.ShapeDtypeStruct((NCHIP, D, SHARD_W), bf16), jax.ShapeDtypeStruct((NCHIP, REST_ROWS, 1024), bf16)],
        scratch_shapes=[pltpu.SemaphoreType.DMA((6 * ns,)), pltpu.SemaphoreType.DMA((6 * ns,)),
                        pltpu.SemaphoreType.DMA((2,))],
        name="gather_weights")(win_s, rest_s)


def _pair_exchange(ga, gb, gs):
    ha, hb = D // 2, REST_ROWS // 2
    def body(a_ref, b_ref, s_ref, ra_ref, rb_ref, rs_ref, send_sems, recv_sems):
        x, y, c, _ = _place()
        sibling = (x, y, 1 - c)
        pieces = []
        for j in range(NCHIP):
            for r0, n in _chunks(ha, 4):
                pieces.append((a_ref.at[j, pl.ds((1 - c) * ha + r0, n), :], ra_ref.at[j, pl.ds(r0, n), :]))
            for r0, n in _chunks(hb, 2):
                pieces.append((b_ref.at[j, pl.ds((1 - c) * hb + r0, n), :], rb_ref.at[j, pl.ds(r0, n), :]))
        pieces.append((s_ref, rs_ref))
        copies = [pltpu.make_async_remote_copy(src_ref=s, dst_ref=d, send_sem=send_sems.at[i], recv_sem=recv_sems.at[i],
                                               device_id=sibling, device_id_type=MESH)
                  for i, (s, d) in enumerate(pieces)]
        for cp in copies:
            cp.start()
        for cp in copies:
            cp.wait_recv()
        for cp in copies:
            cp.wait_send()

    n = NCHIP * 6 + 1
    return pl.pallas_call(
        body, in_specs=[_ANY, _ANY, _ANY], out_specs=[_ANY, _ANY, _ANY],
        out_shape=[jax.ShapeDtypeStruct((NCHIP, ha, SHARD_W), f32), jax.ShapeDtypeStruct((NCHIP, hb, 1024), f32),
                   jax.ShapeDtypeStruct((SMALL_ROWS, 128), f32)],
        scratch_shapes=[pltpu.SemaphoreType.DMA((n,)), pltpu.SemaphoreType.DMA((n,))],
        name="pair_exchange")(ga, gb, gs)


def _chip_exchange(pa, pb, ps):
    ha, hb = D // 2, REST_ROWS // 2
    def body(a_ref, b_ref, s_ref, qa_ref, qb_ref, qs_ref, send_sems, recv_sems, local_sems):
        x, y, c, chips = _place()
        k = 2 * x + y
        own = [pltpu.make_async_copy(a_ref.at[k], qa_ref.at[k], local_sems.at[0]),
               pltpu.make_async_copy(b_ref.at[k], qb_ref.at[k], local_sems.at[1]),
               pltpu.make_async_copy(s_ref, qs_ref.at[k], local_sems.at[2])]
        for cp in own:
            cp.start()
        copies = []
        for j, chip in enumerate(chips):
            cj = 2 * chip[0] + chip[1]
            pieces = [(a_ref.at[cj, pl.ds(r0, n), :], qa_ref.at[k, pl.ds(r0, n), :]) for r0, n in _chunks(ha, 2)]
            pieces += [(b_ref.at[cj], qb_ref.at[k]), (s_ref, qs_ref.at[k])]
            for s, (src, dst_ref) in enumerate(pieces):
                copies.append(pltpu.make_async_remote_copy(
                    src_ref=src, dst_ref=dst_ref, send_sem=send_sems.at[4 * j + s], recv_sem=recv_sems.at[4 * j + s],
                    device_id=(*chip, c), device_id_type=MESH))
        for cp in copies:
            cp.start()
        for cp in copies:
            cp.wait_recv()
        for cp in copies:
            cp.wait_send()
        for cp in own:
            cp.wait()

    return pl.pallas_call(
        body, in_specs=[_ANY, _ANY, _ANY], out_specs=[_ANY, _ANY, _ANY],
        out_shape=[jax.ShapeDtypeStruct((NCHIP, ha, SHARD_W), bf16), jax.ShapeDtypeStruct((NCHIP, hb, 1024), bf16),
                   jax.ShapeDtypeStruct((NCHIP, SMALL_ROWS, 128), f32)],
        scratch_shapes=[pltpu.SemaphoreType.DMA((12,)), pltpu.SemaphoreType.DMA((12,)), pltpu.SemaphoreType.DMA((3,))],
        name="chip_exchange")(pa, pb, ps)


def _sibling_exchange(fa, fb):
    ha, hb = D // 2, REST_ROWS // 2
    def body(a_ref, b_ref, oa_ref, ob_ref, send_sems, recv_sems, local_sems):
        x, y, c, _ = _place()
        own = [pltpu.make_async_copy(a_ref, oa_ref.at[c], local_sems.at[0]),
               pltpu.make_async_copy(b_ref, ob_ref.at[c], local_sems.at[1])]
        for cp in own:
            cp.start()
        pieces = [(a_ref.at[pl.ds(r0, n), :], oa_ref.at[c, pl.ds(r0, n), :]) for r0, n in _chunks(ha, 4)]
        pieces += [(b_ref.at[pl.ds(r0, n), :], ob_ref.at[c, pl.ds(r0, n), :]) for r0, n in _chunks(hb, 2)]
        copies = [pltpu.make_async_remote_copy(src_ref=s, dst_ref=d, send_sem=send_sems.at[i], recv_sem=recv_sems.at[i],
                                               device_id=(x, y, 1 - c), device_id_type=MESH)
                  for i, (s, d) in enumerate(pieces)]
        for cp in copies:
            cp.start()
        for cp in copies:
            cp.wait_recv()
        for cp in copies:
            cp.wait_send()
        for cp in own:
            cp.wait()

    return pl.pallas_call(
        body, in_specs=[_ANY, _ANY], out_specs=[_ANY, _ANY],
        out_shape=[jax.ShapeDtypeStruct((2, ha, SHARD_W), f32), jax.ShapeDtypeStruct((2, hb, 1024), f32)],
        scratch_shapes=[pltpu.SemaphoreType.DMA((6,)), pltpu.SemaphoreType.DMA((6,)), pltpu.SemaphoreType.DMA((2,))],
        name="sibling_exchange")(fa, fb)


def _nchunks(half, cols, itemsize):
    return 4 if half * cols * itemsize >= (1 << 20) else 1


def _segments(metas):
    segs = []
    for w, (half, cols, dt) in enumerate(metas):
        for r0, n in _chunks(half, _nchunks(half, cols, jnp.dtype(dt).itemsize)):
            segs.append((w, half, r0, n))
    return segs


def _rcopy(i, src, dst, send_sems, recv_sems, to):
    return pltpu.make_async_remote_copy(src_ref=src, dst_ref=dst, send_sem=send_sems.at[i], recv_sem=recv_sems.at[i],
                                        device_id=to, device_id_type=MESH)


def _gather_list(shards):
    na = len(shards)
    segs = _segments([(a.shape[0] // 2, a.shape[1], a.dtype) for a in shards])
    ns = len(segs)
    def body(*refs):
        ins, outs, (send_sems, recv_sems) = refs[:na], refs[na:2 * na], refs[2 * na:]
        x, y, c, chips = _place()
        k = 2 * x + y
        me, sibling = (x, y, c), (x, y, 1 - c)

        def dst(w, half, chip, pc, r0, n):
            return outs[w].at[chip, pl.ds(pc * half + r0, n), :]

        first = []
        for j, chip in enumerate(chips):
            for s, (w, half, r0, n) in enumerate(segs):
                first.append(_rcopy(j * ns + s, ins[w].at[pl.ds(c * half + r0, n), :], dst(w, half, k, c, r0, n),
                                    send_sems, recv_sems, (*chip, c)))
        for cp in first:
            cp.start()
        passed = []
        for j, chip in enumerate(chips):
            cj = 2 * chip[0] + chip[1]
            for s, (w, half, r0, n) in enumerate(segs):
                landed = dst(w, half, cj, c, r0, n)
                _rcopy(j * ns + s, landed, landed, send_sems, recv_sems, me).wait_recv()
                fwd = _rcopy(3 * ns + j * ns + s, landed, landed, send_sems, recv_sems, sibling)
                fwd.start()
                passed.append(fwd)
        for j, chip in enumerate(chips):
            cj = 2 * chip[0] + chip[1]
            for s, (w, half, r0, n) in enumerate(segs):
                theirs = dst(w, half, cj, 1 - c, r0, n)
                _rcopy(3 * ns + j * ns + s, theirs, theirs, send_sems, recv_sems, me).wait_recv()
        for cp in first + passed:
            cp.wait_send()

    return pl.pallas_call(
        body, in_specs=[_ANY] * na, out_specs=[_ANY] * na,
        out_shape=[jax.ShapeDtypeStruct((NCHIP,) + a.shape, a.dtype) for a in shards],
        scratch_shapes=[pltpu.SemaphoreType.DMA((6 * ns,)), pltpu.SemaphoreType.DMA((6 * ns,))],
        name="gather_weights")(*shards)


def _gather_prep(shards, x, tgt, g_pre, perm):
    na = len(shards)
    L = x.shape[0]
    nc = L // TC
    segs = _segments([(a.shape[0] // 2, a.shape[1], a.dtype) for a in shards])
    ns = len(segs)
    def body(*refs):
        ins = refs[:na]
        x_ref, t_ref, g_ref, p_ref = refs[na:na + 4]
        outs = refs[na + 4:2 * na + 4]
        h_ref, xi_ref, ti_ref = refs[2 * na + 4:2 * na + 7]
        stages = refs[2 * na + 7:3 * na + 7]
        send_sems, recv_sems, local_sems = refs[3 * na + 7:]
        i = pl.program_id(0)
        x, y, c, chips = _place()
        k = 2 * x + y
        me, sibling = (x, y, c), (x, y, 1 - c)

        def dst(w, half, chip, pc, r0, n):
            return outs[w].at[chip, pl.ds(pc * half + r0, n), :]

        def firsts():
            return [_rcopy(j * ns + s, ins[w].at[pl.ds(c * half + r0, n), :], dst(w, half, k, c, r0, n),
                           send_sems, recv_sems, (*chip, c))
                    for j, chip in enumerate(chips) for s, (w, half, r0, n) in enumerate(segs)]

        def own_out(w):
            return pltpu.make_async_copy(stages[w], outs[w].at[k], local_sems.at[w])

        @pl.when(i == 0)
        def _():
            for cp in firsts():
                cp.start()
            for w in range(na):
                cin = pltpu.make_async_copy(ins[w], stages[w], local_sems.at[w])
                cin.start()
                cin.wait()
            for w in range(na):
                own_out(w).start()

        p = p_ref[...]
        def through(v):
            hi = v.astype(bf16)
            r1 = v - hi.astype(f32)
            mid = r1.astype(bf16)
            lo = (r1 - mid.astype(f32)).astype(bf16)
            return (_dot(p, hi) + _dot(p, mid)) + _dot(p, lo)
        xt = x_ref[...]
        r = lax.rsqrt(jnp.mean(xt * xt, axis=-1, keepdims=True) + RMS_EPS)
        h_ref[...] = _dot(p, (xt * r * g_ref[...]).astype(bf16)).astype(bf16)
        xi_ref[...] = through(xt)
        ti_ref[...] = through(t_ref[...])

        @pl.when(i == nc - 1)
        def _():
            passed = []
            for j, chip in enumerate(chips):
                cj = 2 * chip[0] + chip[1]
                for s, (w, half, r0, n) in enumerate(segs):
                    landed = dst(w, half, cj, c, r0, n)
                    _rcopy(j * ns + s, landed, landed, send_sems, recv_sems, me).wait_recv()
                    fwd = _rcopy(3 * ns + j * ns + s, landed, landed, send_sems, recv_sems, sibling)
                    fwd.start()
                    passed.append(fwd)
            for j, chip in enumerate(chips):
                cj = 2 * chip[0] + chip[1]
                for s, (w, half, r0, n) in enumerate(segs):
                    theirs = dst(w, half, cj, 1 - c, r0, n)
                    _rcopy(3 * ns + j * ns + s, theirs, theirs, send_sems, recv_sems, me).wait_recv()
            for cp in firsts() + passed:
                cp.wait_send()
            for w in range(na):
                own_out(w).wait()

    row = lambda: pl.BlockSpec((TC, D), lambda i: (i, 0))
    return pl.pallas_call(
        body, grid=(nc,),
        in_specs=[_ANY] * na + [row(), row(), _full((1, D)), _full((TC, TC))],
        out_specs=[_ANY] * na + [row(), row(), row()],
        out_shape=[jax.ShapeDtypeStruct((NCHIP,) + a.shape, a.dtype) for a in shards]
        + [jax.ShapeDtypeStruct((L, D), bf16), jax.ShapeDtypeStruct((L, D), f32), jax.ShapeDtypeStruct((L, D), f32)],
        scratch_shapes=[pltpu.VMEM(a.shape, a.dtype) for a in shards]
        + [pltpu.SemaphoreType.DMA((6 * ns,)), pltpu.SemaphoreType.DMA((6 * ns,)), pltpu.SemaphoreType.DMA((na,))],
        name="gather_prep", compiler_params=_cp("arbitrary"))(*shards, x, tgt, g_pre, perm)


def _x_grad_exchange(dproj, w_in, x, gx0, g_pre, parts, small):
    L = x.shape[0]
    tm = 256
    nt = L // tm
    na = len(parts)
    segs = _segments([(p.shape[1], p.shape[2], p.dtype) for p in parts])
    ns = len(segs) + 1
    def body(*refs):
        d_ref, w_ref, x_ref, gx_ref, g_ref = refs[:5]
        ins, s_ref = refs[5:5 + na], refs[5 + na]
        o_ref, dg_ref = refs[6 + na:8 + na]
        outs, qs_ref = refs[8 + na:8 + 2 * na], refs[8 + 2 * na]
        stages = refs[9 + 2 * na:10 + 3 * na]
        send_sems, recv_sems, local_sems = refs[10 + 3 * na:]
        i = pl.program_id(0)
        x, y, c, chips = _place()
        k = 2 * x + y

        def copies():
            out = []
            for j, chip in enumerate(chips):
                cj = 2 * chip[0] + chip[1]
                pieces = [(s_ref, qs_ref.at[k])]
                pieces += [(ins[w].at[cj, pl.ds(r0, n), :], outs[w].at[k, pl.ds(r0, n), :]) for w, _, r0, n in segs]
                out += [_rcopy(ns * j + s, src, d, send_sems, recv_sems, (*chip, c)) for s, (src, d) in enumerate(pieces)]
            return out

        def own_out(w):
            dst = qs_ref.at[k] if w == na else outs[w].at[k]
            return pltpu.make_async_copy(stages[w], dst, local_sems.at[w])

        @pl.when(i == 0)
        def _():
            dg_ref[...] = jnp.zeros_like(dg_ref)
            for cp in copies():
                cp.start()
            for w in range(na + 1):
                cin = pltpu.make_async_copy(s_ref if w == na else ins[w].at[k], stages[w], local_sems.at[w])
                cin.start()
                cin.wait()
            for w in range(na + 1):
                own_out(w).start()

        dh = _dot_nt(d_ref[:, 0:SHARD_W], w_ref[0])
        for j in range(1, NCHIP):
            dh = dh + _dot_nt(d_ref[:, j * SHARD_W:(j + 1) * SHARD_W], w_ref[j])
        xt = x_ref[...]
        r = lax.rsqrt(jnp.mean(xt * xt, axis=-1, keepdims=True) + RMS_EPS)
        xn = xt * r
        dg_ref[...] += jnp.sum(dh * xn, axis=0, keepdims=True)
        dxn = dh * g_ref[...]
        o_ref[...] = gx_ref[...] + r * (dxn - xn * jnp.mean(dxn * xn, axis=-1, keepdims=True))

        @pl.when(i == nt - 1)
        def _():
            for cp in copies():
                cp.wait_recv()
            for cp in copies():
                cp.wait_send()
            for w in range(na + 1):
                own_out(w).wait()

    return pl.pallas_call(
        body, grid=(nt,),
        in_specs=[pl.BlockSpec((tm, IN_W), lambda i: (i, 0)),
                  pl.BlockSpec((NCHIP, D, SHARD_W), lambda i: (0, 0, 0), pipeline_mode=pl.Buffered(1)),
                  pl.BlockSpec((tm, D), lambda i: (i, 0)), pl.BlockSpec((tm, D), lambda i: (i, 0)), _full((1, D))]
        + [_ANY] * (na + 1),
        out_specs=[pl.BlockSpec((tm, D), lambda i: (i, 0)), _full((1, D))] + [_ANY] * (na + 1),
        out_shape=[jax.ShapeDtypeStruct((L, D), f32), jax.ShapeDtypeStruct((1, D), f32)]
        + [jax.ShapeDtypeStruct(p.shape, bf16) for p in parts] + [jax.ShapeDtypeStruct((NCHIP, SMALL_ROWS, 128), f32)],
        scratch_shapes=[pltpu.VMEM(p.shape[1:], bf16) for p in parts] + [pltpu.VMEM((SMALL_ROWS, 128), f32)]
        + [pltpu.SemaphoreType.DMA((3 * ns,)), pltpu.SemaphoreType.DMA((3 * ns,)), pltpu.SemaphoreType.DMA((na + 1,))],
        name="x_grad_exchange", compiler_params=_cp("arbitrary"))(dproj, w_in, x, gx0, g_pre, *parts, small)


def _sibling_join_list(halves):
    na = len(halves)
    segs = _segments([(h.shape[0], h.shape[1], h.dtype) for h in halves])
    def body(*refs):
        ins, outs, stages = refs[:na], refs[na:2 * na], refs[2 * na:3 * na]
        send_sems, recv_sems, local_sems = refs[3 * na:]
        x, y, c, _ = _place()
        copies = [_rcopy(i, ins[w].at[pl.ds(r0, n), :], outs[w].at[pl.ds(c * half + r0, n), :], send_sems, recv_sems,
                         (x, y, 1 - c)) for i, (w, half, r0, n) in enumerate(segs)]
        for cp in copies:
            cp.start()
        own = []
        for w in range(na):
            cin = pltpu.make_async_copy(ins[w], stages[w], local_sems.at[w])
            cin.start()
            cin.wait()
            half = halves[w].shape[0]
            own.append(pltpu.make_async_copy(stages[w], outs[w].at[pl.ds(c * half, half), :], local_sems.at[w]))
            own[-1].start()
        for cp in copies:
            cp.wait_recv()
        for cp in copies:
            cp.wait_send()
        for cp in own:
            cp.wait()

    return pl.pallas_call(
        body, in_specs=[_ANY] * na, out_specs=[_ANY] * na,
        out_shape=[jax.ShapeDtypeStruct((2 * h.shape[0], h.shape[1]), f32) for h in halves],
        scratch_shapes=[pltpu.VMEM(h.shape, f32) for h in halves]
        + [pltpu.SemaphoreType.DMA((len(segs),)), pltpu.SemaphoreType.DMA((len(segs),)), pltpu.SemaphoreType.DMA((na,))],
        name="sibling_join")(*halves)


def _allgather_rows(v):
    def body(v_ref, o_ref, send_sems, recv_sems):
        x, y, c, _ = _place()
        me = 4 * x + 2 * y + c
        o_ref[me] = v_ref[...]
        copies = []
        i = 0
        for dx in range(2):
            for dy in range(2):
                for dc in range(2):
                    if dx + dy + dc:
                        copies.append(_rcopy(i, v_ref, o_ref.at[me], send_sems, recv_sems, (x ^ dx, y ^ dy, c ^ dc)))
                        i += 1
        for cp in copies:
            cp.start()
        for cp in copies:
            cp.wait_recv()
        for cp in copies:
            cp.wait_send()

    vm = pl.BlockSpec(memory_space=pltpu.VMEM)
    return pl.pallas_call(
        body, in_specs=[vm], out_specs=vm, out_shape=jax.ShapeDtypeStruct((8, 8, 128), f32),
        scratch_shapes=[pltpu.SemaphoreType.DMA((7,)), pltpu.SemaphoreType.DMA((7,))],
        name="allgather_rows")(v)


def _adamw_rows(parts, w, m, v):
    def body(p_ref, w_ref, m_ref, v_ref, g_ref, d_ref, m2_ref, v2_ref):
        g = p_ref[0]
        for dvc in range(1, 8):
            g = g + p_ref[dvc]
        g_ref[...] = g
        d, m2, v2 = _adamw_math(w_ref[...], g, m_ref[...], v_ref[...])
        d_ref[...] = d
        m2_ref[...] = m2
        v2_ref[...] = v2
    return pl.pallas_call(body, out_shape=[jax.ShapeDtypeStruct((8, 128), f32)] * 4, name="adamw_pre_norm_gain")(
        parts, w, m, v)


def _pair_exchange_list(grads, small):
    na = len(grads)
    segs = _segments([(g.shape[1] // 2, g.shape[2], g.dtype) for g in grads])
    n = NCHIP * len(segs) + 1
    def body(*refs):
        ins, s_ref, outs, rs_ref, (send_sems, recv_sems) = (refs[:na], refs[na], refs[na + 1:2 * na + 1],
                                                            refs[2 * na + 1], refs[2 * na + 2:])
        x, y, c, _ = _place()
        pieces = [(s_ref, rs_ref)]
        for j in range(NCHIP):
            for w, half, r0, rows in segs:
                pieces.append((ins[w].at[j, pl.ds((1 - c) * half + r0, rows), :], outs[w].at[j, pl.ds(r0, rows), :]))
        copies = [_rcopy(i, s, d, send_sems, recv_sems, (x, y, 1 - c)) for i, (s, d) in enumerate(pieces)]
        for cp in copies:
            cp.start()
        for cp in copies:
            cp.wait_recv()
        for cp in copies:
            cp.wait_send()

    return pl.pallas_call(
        body, in_specs=[_ANY] * (na + 1), out_specs=[_ANY] * (na + 1),
        out_shape=[jax.ShapeDtypeStruct((NCHIP, g.shape[1] // 2, g.shape[2]), f32) for g in grads]
        + [jax.ShapeDtypeStruct((SMALL_ROWS, 128), f32)],
        scratch_shapes=[pltpu.SemaphoreType.DMA((n,)), pltpu.SemaphoreType.DMA((n,))],
        name="pair_exchange")(*grads, small)


def _pair_sum_list(c_arr, grads, recvs, small, rsmall):
    na = len(grads)
    def body(c_ref, *refs):
        g_refs, r_refs, s_ref, rs_ref = refs[:na], refs[na:2 * na], refs[2 * na], refs[2 * na + 1]
        o_refs, os_ref = refs[2 * na + 2:3 * na + 2], refs[3 * na + 2]
        for g_ref, r_ref, o_ref in zip(g_refs, r_refs, o_refs):
            o_ref[...] = (g_ref[...] + r_ref[...]).astype(bf16)
        os_ref[...] = s_ref[...] + rs_ref[...]
    half = lambda g: pl.BlockSpec((1, g.shape[1] // 2, g.shape[2]), lambda j, c: (j, c[0], 0))
    low = lambda g: pl.BlockSpec((1, g.shape[1] // 2, g.shape[2]), lambda j, c: (j, 0, 0))
    sm = pl.BlockSpec((SMALL_ROWS, 128), lambda j, c: (0, 0))
    grid_spec = pltpu.PrefetchScalarGridSpec(
        num_scalar_prefetch=1, grid=(NCHIP,),
        in_specs=[half(g) for g in grads] + [low(g) for g in grads] + [sm, sm],
        out_specs=[low(g) for g in grads] + [sm])
    return pl.pallas_call(
        body, grid_spec=grid_spec,
        out_shape=[jax.ShapeDtypeStruct((NCHIP, g.shape[1] // 2, g.shape[2]), bf16) for g in grads]
        + [jax.ShapeDtypeStruct((SMALL_ROWS, 128), f32)],
        name="pair_sum", compiler_params=_cp("arbitrary"))(c_arr, *grads, *recvs, small, rsmall)


def _chip_exchange_list(parts, small):
    na = len(parts)
    segs = _segments([(p.shape[1], p.shape[2], p.dtype) for p in parts])
    ns = len(segs) + 1
    def body(*refs):
        ins, s_ref, outs, qs_ref, (send_sems, recv_sems) = (refs[:na], refs[na], refs[na + 1:2 * na + 1],
                                                            refs[2 * na + 1], refs[2 * na + 2:])
        x, y, c, chips = _place()
        k = 2 * x + y
        copies = []
        for j, chip in enumerate(chips):
            cj = 2 * chip[0] + chip[1]
            pieces = [(s_ref, qs_ref.at[k])]
            pieces += [(ins[w].at[cj, pl.ds(r0, n), :], outs[w].at[k, pl.ds(r0, n), :]) for w, _, r0, n in segs]
            copies += [_rcopy(ns * j + s, src, d, send_sems, recv_sems, (*chip, c)) for s, (src, d) in enumerate(pieces)]
        for cp in copies:
            cp.start()
        for cp in copies:
            cp.wait_recv()
        for cp in copies:
            cp.wait_send()

    return pl.pallas_call(
        body, in_specs=[_ANY] * (na + 1), out_specs=[_ANY] * (na + 1),
        out_shape=[jax.ShapeDtypeStruct(p.shape, bf16) for p in parts]
        + [jax.ShapeDtypeStruct((NCHIP, SMALL_ROWS, 128), f32)],
        scratch_shapes=[pltpu.SemaphoreType.DMA((3 * ns,)), pltpu.SemaphoreType.DMA((3 * ns,))],
        name="chip_exchange")(*parts, small)


def _chip_sum_list(parts, small):
    na = len(parts)
    nt = 2
    def body(*refs):
        for q_ref, f_ref in zip(refs[:na + 1], refs[na + 1:]):
            acc = q_ref[0].astype(f32)
            for j in range(1, NCHIP):
                acc = acc + q_ref[j].astype(f32)
            f_ref[...] = acc
    arrs = list(parts) + [small]
    return pl.pallas_call(
        body, grid=(nt,),
        in_specs=[pl.BlockSpec((NCHIP, a.shape[1] // nt, a.shape[2]), lambda i: (0, i, 0)) for a in arrs],
        out_specs=[pl.BlockSpec((a.shape[1] // nt, a.shape[2]), lambda i: (i, 0)) for a in arrs],
        out_shape=[jax.ShapeDtypeStruct(a.shape[1:], f32) for a in arrs],
        name="chip_sum", compiler_params=_cp("arbitrary"))(*arrs)


def _sibling_exchange_list(halves):
    na = len(halves)
    segs = _segments([(h.shape[0], h.shape[1], h.dtype) for h in halves])
    def body(*refs):
        ins, outs, (send_sems, recv_sems) = refs[:na], refs[na:2 * na], refs[2 * na:]
        x, y, c, _ = _place()
        copies = [_rcopy(i, ins[w].at[pl.ds(r0, n), :], outs[w].at[pl.ds(r0, n), :], send_sems, recv_sems, (x, y, 1 - c))
                  for i, (w, _, r0, n) in enumerate(segs)]
        for cp in copies:
            cp.start()
        for cp in copies:
            cp.wait_recv()
        for cp in copies:
            cp.wait_send()

    return pl.pallas_call(
        body, in_specs=[_ANY] * na, out_specs=[_ANY] * na,
        out_shape=[jax.ShapeDtypeStruct(h.shape, f32) for h in halves],
        scratch_shapes=[pltpu.SemaphoreType.DMA((len(segs),)), pltpu.SemaphoreType.DMA((len(segs),))],
        name="sibling_exchange")(*halves)


_REST_ROWS = (256, 256, 64, 128)
_CONV_PAD = 8192


def _pack_rest(mats, conv_rows, dtype):
    parts = [mats[0], mats[1], mats[2].reshape(64, 1024), mats[3].reshape(128, 1024)]
    parts = [p.astype(dtype) for p in parts] + [conv_rows]
    used = sum(p.shape[0] for p in parts)
    parts.append(jnp.zeros((REST_ROWS - used, 1024), dtype))
    return jnp.concatenate(parts, axis=0)


def _pack_rest_weights(mats, conv_w_s):
    flat = jnp.pad(conv_w_s.reshape(-1), (0, _CONV_PAD - KS * 256))
    return _pack_rest(mats, lax.bitcast_convert_type(flat, bf16).reshape(16, 1024), bf16)


def _pack_rest_grads(mats, conv_w_s):
    flat = jnp.pad(conv_w_s.reshape(-1), (0, _CONV_PAD - KS * 256))
    return _pack_rest(mats, flat.reshape(8, 1024), f32)


def _split_rest(p, conv_rows):
    o = 0
    out = []
    for rows in _REST_ROWS + (conv_rows,):
        out.append(p[..., o:o + rows, :])
        o += rows
    return out


_SMALL = (("conv_b", (1, 1024)), ("conv_ln_gain", (1, 1024)), ("conv_ln_bias", (1, 1024)),
          ("ssm_lambda_re", (1, 32, 64)), ("ssm_lambda_im", (1, 32, 64)), ("ssm_log_dt", (1, 32)),
          ("ssm_b_re", (1, 32, 64, 16)), ("ssm_b_im", (1, 32, 64, 16)), ("ssm_c_re", (1, 32, 16, 64)),
          ("ssm_c_im", (1, 32, 16, 64)), ("ssm_d", (1, 32, 16)), ("b_ssm_glu", (1, 512)), ("post_norm_gain", (1, 1024)))


def _pack_small(vals, extra=None):
    rows = []
    for v in list(vals) + ([extra] if extra is not None else []):
        flat = v.reshape(-1).astype(f32)
        n = -(-flat.shape[0] // 1024) * 1024
        rows.append(jnp.pad(flat, (0, n - flat.shape[0])).reshape(-1, 128))
    used = sum(r.shape[0] for r in rows)
    rows.append(jnp.zeros((SMALL_ROWS - used, 128), f32))
    return jnp.concatenate(rows, axis=0)


def _unpack_small(p):
    o = 0
    out = []
    for _, shape in _SMALL:
        n = int(np.prod(shape))
        nr = -(-n // 1024) * 8
        out.append(p[o:o + nr].reshape(-1)[:n].reshape(shape))
        o += nr
    return out, p[o, 0]


def _discretize(lam_re, lam_im, log_dt, b_re, b_im):
    dt = jnp.exp(log_dt)[:, None]
    mag = jnp.exp(lam_re * dt)
    ar = mag * jnp.cos(lam_im * dt)
    ai = mag * jnp.sin(lam_im * dt)
    den = lam_re * lam_re + lam_im * lam_im
    zr = ((ar - 1.0) * lam_re + ai * lam_im) / den
    zi = (ai * lam_re - (ar - 1.0) * lam_im) / den
    bbr = zr[..., None] * b_re - zi[..., None] * b_im
    bbi = zr[..., None] * b_im + zi[..., None] * b_re
    return ar, ai, bbr, bbi


_EYE8 = np.eye(8, dtype=np.float32)


def _bbt_blocks(bb):
    v = bb.reshape(4, 8, PST, H).transpose(0, 1, 3, 2)
    return jnp.einsum("bghp,gk->bghkp", v, _EYE8).reshape(4, 128, 512)


def _bbt_unblock(m):
    v = jnp.einsum("bghkp,gk->bghp", m.reshape(4, 8, H, 8, PST), _EYE8)
    return v.transpose(0, 1, 3, 2).reshape(G, PST, H)


def _ct_blocks(cc):
    v = cc.reshape(4, 8, H, PST)
    return jnp.einsum("bghp,gk->bgpkh", v, _EYE8).reshape(4, 512, 128)


def _ct_unblock(m):
    v = jnp.einsum("bgpkh,gk->bghp", m.reshape(4, 8, PST, 8, H), _EYE8)
    return v.reshape(G, H, PST)


def _perm_matrix():
    p = np.zeros((TC, TC), np.float32)
    for r in range(R):
        for seg in range(8):
            p[r * 8 + seg, seg * R + r] = 1.0
    return p


def _deinterleave(a):
    L, C = a.shape
    return a.reshape(L // TC, R, 8, C).transpose(0, 2, 1, 3).reshape(L, C)


def _fwd_bwd(h, xi, ti, w_in, conv_w, w_co, w_glu, w_so, w_out, small):
    (conv_b, ln_g, ln_b, lam_re, lam_im, log_dt, b_re, b_im, c_re, c_im, dvec, b_glu, g_post) = small
    lam_re, lam_im, log_dt = lam_re[0], lam_im[0], log_dt[0]
    b_re, b_im, c_re, c_im = b_re[0], b_im[0], c_re[0], c_im[0]
    (ar, ai, bbr, bbi), disc_vjp = jax.vjp(_discretize, lam_re, lam_im, log_dt, b_re, b_im)
    a_re = ar.reshape(1, NS)
    a_im = ai.reshape(1, NS)
    dt = jnp.exp(log_dt)[:, None]
    steps = jnp.arange(1, R + 1, dtype=f32)[:, None, None]
    apow_re = (jnp.exp(steps * (lam_re * dt)) * jnp.cos(steps * (lam_im * dt))).reshape(R, NS)
    apow_im = (jnp.exp(steps * (lam_re * dt)) * jnp.sin(steps * (lam_im * dt))).reshape(R, NS)
    bbt_re, bbt_im = _bbt_blocks(bbr).astype(bf16), _bbt_blocks(bbi).astype(bf16)
    ct_re, ct_im = _ct_blocks(c_re).astype(bf16), _ct_blocks(c_im).astype(bf16)
    d_row = dvec.reshape(1, SW)
    cw32 = jnp.pad(conv_w, ((0, 1), (0, 0)))

    proj = _proj_fwd(h, w_in)
    cu1, a_in = _conv_fwd(proj, cw32, conv_b, ln_g, ln_b)
    y0, b_in, sre, sim, cinr, cini = _ssm_fwd(proj, bbt_re, bbt_im, ct_re, ct_im, a_re, a_im,
                                              apow_re, apow_im, d_row, w_glu, b_glu)
    gx0, d_ain, d_bin, dproj, dw_out, dw_co, dw_so, dg_post, loss = _tail(
        a_in, b_in, proj, xi, ti, w_co, w_so, w_out, g_post)
    (dproj, dbbt_re, dbbt_im, dct_re, dct_im, dd, dar8, dai8, dw_glu, db_glu) = _ssm_bwd(
        d_bin, y0, proj, sre, sim, cinr, cini, bbt_re, bbt_im, ct_re, ct_im,
        a_re, a_im, apow_re, apow_im, d_row, w_glu, b_glu, dproj)
    dproj, dcw8, d_convb, d_lng, d_lnb = _conv_bwd(d_ain, cu1, proj, cw32, ln_g, ln_b, dproj)
    dw_in = _win_grad(h, dproj)

    d_ar = jnp.sum(dar8, axis=0).reshape(G, PST)
    d_ai = jnp.sum(dai8, axis=0).reshape(G, PST)
    d_lre, d_lim, d_ldt, d_bre, d_bim = disc_vjp((d_ar, d_ai, _bbt_unblock(dbbt_re), _bbt_unblock(dbbt_im)))
    d_conv_w = jnp.sum(dcw8, axis=1)[:KS]
    small_grads = [d_convb, d_lng, d_lnb, d_lre[None], d_lim[None], d_ldt[None], d_bre[None], d_bim[None],
                   _ct_unblock(dct_re)[None], _ct_unblock(dct_im)[None], dd.reshape(1, G, H), db_glu, dg_post]
    return loss[0, 0], gx0, dproj, (dw_in, dw_co, dw_out, dw_glu, dw_so, d_conv_w), small_grads


def kernel(x, pre_norm_gain, w_in, conv_w, conv_b, conv_ln_gain, conv_ln_bias, w_conv_out, ssm_lambda_re, ssm_lambda_im, ssm_log_dt, ssm_b_re, ssm_b_im, ssm_c_re, ssm_c_im, ssm_d, w_ssm_glu, b_ssm_glu, w_ssm_out, w_out, post_norm_gain, loss_target, m_pre_norm_gain, m_w_in, m_conv_w, m_conv_b, m_conv_ln_gain, m_conv_ln_bias, m_w_conv_out, m_ssm_lambda_re, m_ssm_lambda_im, m_ssm_log_dt, m_ssm_b_re, m_ssm_b_im, m_ssm_c_re, m_ssm_c_im, m_ssm_d, m_w_ssm_glu, m_b_ssm_glu, m_w_ssm_out, m_w_out, m_post_norm_gain, v_pre_norm_gain, v_w_in, v_conv_w, v_conv_b, v_conv_ln_gain, v_conv_ln_bias, v_w_conv_out, v_ssm_lambda_re, v_ssm_lambda_im, v_ssm_log_dt, v_ssm_b_re, v_ssm_b_im, v_ssm_c_re, v_ssm_c_im, v_ssm_d, v_w_ssm_glu, v_b_ssm_glu, v_w_ssm_out, v_w_out, v_post_norm_gain):
    c = lax.axis_index("c")
    shards = [w_in[0].astype(bf16), w_conv_out[0].astype(bf16), w_out[0].astype(bf16), w_ssm_glu[0].astype(bf16),
              w_ssm_out[0].astype(bf16), jnp.pad(conv_w[0], ((0, CONV_ROWS - KS), (0, 0)))]
    w_in_g, w_co_g, w_out_g, w_glu_g, w_so_g, conv_w_g, h, xi, ti = _gather_prep(
        shards, x[0], loss_target[0], pre_norm_gain, jnp.asarray(_perm_matrix(), bf16))
    conv_w_f = conv_w_g[:, :KS].transpose(1, 0, 2).reshape(KS, CW)

    small = (conv_b, conv_ln_gain, conv_ln_bias, ssm_lambda_re, ssm_lambda_im, ssm_log_dt, ssm_b_re,
             ssm_b_im, ssm_c_re, ssm_c_im, ssm_d, b_ssm_glu, post_norm_gain)
    loss_part, gx0, dproj, big_grads, small_grads = _fwd_bwd(
        h, xi, ti, w_in_g, conv_w_f, w_co_g.reshape(CW, D), w_glu_g.reshape(SW, SW), w_so_g, w_out_g.reshape(D, D), small)

    dw_in, dw_co, dw_out, dw_glu, dw_so, d_conv_w = big_grads
    d_conv_w = jnp.pad(d_conv_w, ((0, CONV_ROWS - KS), (0, 0))).reshape(CONV_ROWS, NCHIP, 256).transpose(1, 0, 2)
    grads = [dw_in, dw_co.reshape(NCHIP, 256, D), dw_out.reshape(NCHIP, 256, D), dw_glu.reshape(NCHIP, 128, SW),
             dw_so, d_conv_w]
    gs = _pack_small(small_grads, extra=loss_part)
    *recvs, rs = _pair_exchange_list(grads, gs)
    *parts, ps = _pair_sum_list(c.astype(jnp.int32).reshape(1), grads, recvs, gs, rs)
    gxi, dg_pre, *qparts, qs = _x_grad_exchange(dproj, w_in_g, xi, gx0, pre_norm_gain, parts, ps)
    grad_x = _deinterleave(gxi)
    *halves, fs = _chip_sum_list(qparts, qs)
    g_big = list(_sibling_join_list(halves))
    g_big[5] = g_big[5][:KS]

    big_w = (w_in[0], w_conv_out[0], w_out[0], w_ssm_glu[0], w_ssm_out[0], conv_w[0])
    big_m = (m_w_in[0], m_w_conv_out[0], m_w_out[0], m_w_ssm_glu[0], m_w_ssm_out[0], m_conv_w[0])
    big_v = (v_w_in[0], v_w_conv_out[0], v_w_out[0], v_w_ssm_glu[0], v_w_ssm_out[0], v_conv_w[0])
    big_names = ("w_in", "w_conv_out", "w_out", "w_ssm_glu", "w_ssm_out", "conv_w")
    res = {}
    for n, w, g, m, v in zip(big_names, big_w, g_big, big_m, big_v):
        d, m2, v2 = _adamw("adamw_" + n, w, g, m, v)
        res[n] = (g[None], d[None], m2[None], v2[None])

    small_m = (m_conv_b, m_conv_ln_gain, m_conv_ln_bias, m_ssm_lambda_re, m_ssm_lambda_im, m_ssm_log_dt,
               m_ssm_b_re, m_ssm_b_im, m_ssm_c_re, m_ssm_c_im, m_ssm_d, m_b_ssm_glu, m_post_norm_gain)
    small_v = (v_conv_b, v_conv_ln_gain, v_conv_ln_bias, v_ssm_lambda_re, v_ssm_lambda_im, v_ssm_log_dt,
               v_ssm_b_re, v_ssm_b_im, v_ssm_c_re, v_ssm_c_im, v_ssm_d, v_b_ssm_glu, v_post_norm_gain)
    sd, sm, sv = _adamw("adamw_small", _pack_small(small), fs, _pack_small(small_m), _pack_small(small_v))
    sg_l, loss = _unpack_small(fs)
    sd_l, _ = _unpack_small(sd)
    sm_l, _ = _unpack_small(sm)
    sv_l, _ = _unpack_small(sv)
    for i, (n, _) in enumerate(_SMALL):
        res[n] = (sg_l[i], sd_l[i], sm_l[i], sv_l[i])
    rows = lambda a: a.reshape(8, 128)
    pre = _adamw_rows(_allgather_rows(rows(dg_pre)), rows(pre_norm_gain), rows(m_pre_norm_gain), rows(v_pre_norm_gain))
    res["pre_norm_gain"] = tuple(a.reshape(1, D) for a in pre)

    order = ("pre_norm_gain", "w_in", "conv_w", "conv_b", "conv_ln_gain", "conv_ln_bias", "w_conv_out", "ssm_lambda_re",
             "ssm_lambda_im", "ssm_log_dt", "ssm_b_re", "ssm_b_im", "ssm_c_re", "ssm_c_im", "ssm_d", "w_ssm_glu",
             "b_ssm_glu", "w_ssm_out", "w_out", "post_norm_gain")
    outs = [loss, grad_x[None]]
    for q in range(4):
        outs.extend(res[n][q] for n in order)
    return tuple(outs)
```

```python
import math

import numpy as np
import jax
import jax.numpy as jnp
from jax import lax
from jax.experimental import pallas as pl
from jax.experimental.pallas import tpu as pltpu

f32 = jnp.float32
bf16 = jnp.bfloat16

D = 1024
CW = 1024
SW = 512
G = 32
H = 16
PST = 64
NS = G * PST
KS = 31
IN_W = 6144
NCHIP = 4
SHARD_W = IN_W // NCHIP
RMS_EPS = 1e-6
LN_EPS = 1e-5
LR, B1, B2, EPS, WD, STEP = 0.001, 0.9, 0.999, 1e-08, 0.01, 10
GELU_K0 = math.sqrt(2.0 / math.pi)
GELU_K1 = 0.044715

TC = 512
R = TC // 8
NH = 32
LBW = 1024
REST_ROWS = 768
CONV_ROWS = 64
SMALL_ROWS = 1152
VMEM_LIMIT = 56 * 1024 * 1024
MESH = pl.DeviceIdType.MESH


def _cp(*sem):
    return pltpu.CompilerParams(dimension_semantics=tuple(sem), vmem_limit_bytes=VMEM_LIMIT)


def _sig(v):
    return 0.5 * jnp.tanh(0.5 * v) + 0.5


def _dot(a, b):
    return jnp.dot(a, b, preferred_element_type=f32)


def _dot_nt(a, b):
    return lax.dot_general(a, b, (((1,), (1,)), ((), ())), preferred_element_type=f32)


def _dot_tn(a, b):
    return lax.dot_general(a, b, (((0,), (0,)), ((), ())), preferred_element_type=f32)


def _full(shape):
    nd = len(shape)
    return pl.BlockSpec(shape, lambda *_: (0,) * nd)


def _rows8(i):
    return pl.ds(pl.multiple_of(i * 8, 8), 8)


def _prenorm(x, g_pre):
    L = x.shape[0]
    tm = 512
    def body(x_ref, g_ref, h_ref):
        xt = x_ref[...]
        r = lax.rsqrt(jnp.mean(xt * xt, axis=-1, keepdims=True) + RMS_EPS)
        h_ref[...] = (xt * r * g_ref[...]).astype(bf16)
    return pl.pallas_call(
        body, grid=(L // tm,),
        in_specs=[pl.BlockSpec((tm, D), lambda i: (i, 0)), _full((1, D))],
        out_specs=pl.BlockSpec((tm, D), lambda i: (i, 0)),
        out_shape=jax.ShapeDtypeStruct((L, D), bf16),
        name="prenorm", compiler_params=_cp("arbitrary"))(x, g_pre)


def _proj_fwd(h, w_in):
    L = h.shape[0]
    tm = min(1024, L)
    def body(h_ref, w_ref, o_ref):
        o_ref[...] = _dot(h_ref[...], w_ref[0]).astype(bf16)
    return pl.pallas_call(
        body, grid=(NCHIP, L // tm),
        in_specs=[pl.BlockSpec((tm, D), lambda j, i: (i, 0)), pl.BlockSpec((1, D, SHARD_W), lambda j, i: (j, 0, 0))],
        out_specs=pl.BlockSpec((tm, SHARD_W), lambda j, i: (i, j)),
        out_shape=jax.ShapeDtypeStruct((L, IN_W), bf16),
        name="proj_fwd", compiler_params=_cp("arbitrary", "arbitrary"))(h, w_in)


NLB = CW // 128
RPI = 4


def _put_blocked(buf, row0, nrows, v):
    for lb in range(NLB):
        buf[lb, pl.ds(row0, nrows), :] = v[:, lb * 128:(lb + 1) * 128]


def _get_blocked(buf, row0, nrows):
    return jnp.concatenate([buf[lb, pl.ds(row0, nrows), :] for lb in range(NLB)], axis=1)


def _fill_before(ebuf, prev):
    sub = lax.broadcasted_iota(jnp.int32, (8, 128), 0)
    def halo(p, carry):
        for lb in range(NLB):
            cur = ebuf[lb, _rows8(R + p), :]
            ebuf[lb, _rows8(p), :] = jnp.where(sub == 0, pltpu.roll(prev[lb, _rows8(p), :], 1, 0),
                                               pltpu.roll(cur, 1, 0))
        return carry
    lax.fori_loop(0, NH, halo, 0)


def _fir(buf, lb, r, coef, first, flip):
    win = buf[lb, pl.ds(pl.multiple_of(r * 8, 8), (KS + RPI - 1) * 8), :]
    outs = []
    for i in range(RPI):
        acc = [first, None, None, None]
        for k in range(KS):
            o = i + ((KS - 1 - k) if flip else k)
            t = coef[k] * win[8 * o:8 * o + 8, :]
            acc[k % 4] = t if acc[k % 4] is None else acc[k % 4] + t
        outs.append((acc[0] + acc[1]) + (acc[2] + acc[3]))
    return outs


def _conv_fwd(proj, cw, cbias, lng, lnb):
    L = proj.shape[0]
    nc = L // TC
    def body(ca_ref, cb_ref, zc_ref, w_ref, b_ref, g_ref, bb_ref, cu1_ref, ain_ref, ebuf, prev, cacc):
        @pl.when(pl.program_id(0) == 0)
        def _():
            prev[...] = jnp.zeros_like(prev)
        def glu(s, carry):
            rows = pl.ds(pl.multiple_of(s * 64, 64), 64)
            _put_blocked(ebuf, pl.multiple_of(NH * 8 + s * 64, 64), 64,
                         ca_ref[rows, :].astype(f32) * _sig(cb_ref[rows, :].astype(f32)))
            return carry
        lax.fori_loop(0, TC // 64, glu, 0)
        _fill_before(ebuf, prev)
        prev[...] = ebuf[:, R * 8:(NH + R) * 8, :]
        for lb in range(NLB):
            sl = slice(lb * 128, (lb + 1) * 128)
            wk = [jnp.broadcast_to(w_ref[k:k + 1, sl], (8, 128)) for k in range(KS)]
            bias = jnp.broadcast_to(b_ref[:, sl], (8, 128))
            def tap(q, carry, lb=lb, wk=wk, bias=bias):
                r = q * RPI
                for i, o in enumerate(_fir(ebuf, lb, r + (NH - KS + 1), wk, bias, False)):
                    cacc[lb, _rows8(r + i), :] = o
                return carry
            lax.fori_loop(0, R // RPI, tap, 0)
        def norm(s, carry):
            rows = pl.ds(pl.multiple_of(s * 64, 64), 64)
            c1b = _get_blocked(cacc, pl.multiple_of(s * 64, 64), 64).astype(bf16)
            cu1_ref[rows, :] = c1b
            c1 = c1b.astype(f32)
            xc = c1 - jnp.mean(c1, axis=-1, keepdims=True)
            var = jnp.mean(xc * xc, axis=-1, keepdims=True)
            ln = xc * lax.rsqrt(var + LN_EPS) * g_ref[...] + bb_ref[...]
            zc = zc_ref[rows, :].astype(f32)
            ain_ref[rows, :] = ((ln * _sig(ln)) * (zc * _sig(zc))).astype(bf16)
            return carry
        lax.fori_loop(0, TC // 64, norm, 0)

    col = lambda c: pl.BlockSpec((TC, CW), lambda i, c=c: (i, c))
    return pl.pallas_call(
        body, grid=(nc,),
        in_specs=[col(0), col(1), col(2), _full((32, CW)), _full((1, CW)), _full((1, CW)), _full((1, CW))],
        out_specs=[pl.BlockSpec((TC, CW), lambda i: (i, 0)), pl.BlockSpec((TC, CW), lambda i: (i, 0))],
        out_shape=[jax.ShapeDtypeStruct((L, CW), bf16), jax.ShapeDtypeStruct((L, CW), bf16)],
        scratch_shapes=[pltpu.VMEM((NLB, (NH + R) * 8, 128), f32), pltpu.VMEM((NLB, NH * 8, 128), f32),
                        pltpu.VMEM((NLB, TC, 128), f32)],
        name="conv_fwd", compiler_params=_cp("arbitrary"))(proj, proj, proj, cw, cbias, lng, lnb)


def _gelu_parts(y0):
    t = jnp.tanh(GELU_K0 * (y0 + GELU_K1 * y0 * y0 * y0))
    return t, 0.5 * y0 * (1.0 + t)


def _ssm_fwd(proj, bbt_re, bbt_im, ct_re, ct_im, a_re, a_im, apow_re, apow_im, dvec, wglu, bglu):
    L = proj.shape[0]
    nc = L // TC
    def body(u_ref, zs_ref, bre_ref, bim_ref, cre_ref, cim_ref, are_ref, aim_ref, pwr_ref, pwi_ref,
             d_ref, wg_ref, bg_ref, y0_ref, bin_ref, sre, sim, cinr, cini, prev_re, prev_im):
        c = pl.program_id(0)
        @pl.when(c == 0)
        def _():
            prev_re[...] = jnp.zeros_like(prev_re)
            prev_im[...] = jnp.zeros_like(prev_im)
        u = u_ref[...]
        for blk in range(4):
            ub = u[:, 128 * blk:128 * (blk + 1)]
            sre[:, 512 * blk:512 * (blk + 1)] = _dot(ub, bre_ref[blk])
            sim[:, 512 * blk:512 * (blk + 1)] = _dot(ub, bim_ref[blk])
        for lb in range(NS // LBW):
            sl = slice(lb * LBW, (lb + 1) * LBW)
            ar = jnp.broadcast_to(are_ref[:, sl], (8, LBW))
            ai = jnp.broadcast_to(aim_ref[:, sl], (8, LBW))
            def step(r, carry, sl=sl, ar=ar, ai=ai):
                sr, si = carry
                nr = ar * sr - ai * si + sre[_rows8(r), sl]
                ni = ar * si + ai * sr + sim[_rows8(r), sl]
                sre[_rows8(r), sl] = nr
                sim[_rows8(r), sl] = ni
                return nr, ni
            lax.fori_loop(1, R, step, (sre[0:8, sl], sim[0:8, sl]))
        a_r = pwr_ref[R - 1:R, :]
        a_i = pwi_ref[R - 1:R, :]
        cr = prev_re[0:1, :]
        ci = prev_im[0:1, :]
        for seg in range(8):
            cinr[seg:seg + 1, :] = cr
            cini[seg:seg + 1, :] = ci
            er = sre[8 * (R - 1) + seg:8 * (R - 1) + seg + 1, :]
            ei = sim[8 * (R - 1) + seg:8 * (R - 1) + seg + 1, :]
            cr, ci = er + a_r * cr - a_i * ci, ei + a_r * ci + a_i * cr
        prev_re[0:1, :] = cr
        prev_im[0:1, :] = ci
        for lb in range(NS // LBW):
            sl = slice(lb * LBW, (lb + 1) * LBW)
            kr = cinr[:, sl]
            ki = cini[:, sl]
            def fix(r, carry, sl=sl, kr=kr, ki=ki):
                pr = jnp.broadcast_to(pwr_ref[pl.ds(r, 1), sl], (8, LBW))
                pi = jnp.broadcast_to(pwi_ref[pl.ds(r, 1), sl], (8, LBW))
                sre[_rows8(r), sl] = sre[_rows8(r), sl] + pr * kr - pi * ki
                sim[_rows8(r), sl] = sim[_rows8(r), sl] + pr * ki + pi * kr
                return carry
            lax.fori_loop(0, R, fix, 0)
        yp = []
        for blk in range(4):
            sr = sre[:, 512 * blk:512 * (blk + 1)].astype(bf16)
            si = sim[:, 512 * blk:512 * (blk + 1)].astype(bf16)
            yp.append(_dot(sr, cre_ref[blk]) - _dot(si, cim_ref[blk]))
        y0 = jnp.concatenate(yp, axis=1) + d_ref[...] * u.astype(f32)
        y0_ref[...] = y0
        _, y1 = _gelu_parts(y0)
        glu = _dot(y1.astype(bf16), wg_ref[...]) + bg_ref[...]
        y2 = y1 * _sig(glu)
        zs = zs_ref[...].astype(f32)
        bin_ref[...] = (y2 * (zs * _sig(zs))).astype(bf16)

    return pl.pallas_call(
        body, grid=(nc,),
        in_specs=[pl.BlockSpec((TC, SW), lambda c: (c, 6)), pl.BlockSpec((TC, SW), lambda c: (c, 7)),
                  _full((4, 128, 512)), _full((4, 128, 512)), _full((4, 512, 128)), _full((4, 512, 128)),
                  _full((1, NS)), _full((1, NS)), _full((R, NS)), _full((R, NS)),
                  _full((1, SW)), _full((SW, SW)), _full((1, SW))],
        out_specs=[pl.BlockSpec((TC, SW), lambda c: (c, 0)), pl.BlockSpec((TC, SW), lambda c: (c, 0)),
                   pl.BlockSpec((TC, NS), lambda c: (c, 0)), pl.BlockSpec((TC, NS), lambda c: (c, 0)),
                   pl.BlockSpec((8, NS), lambda c: (c, 0)), pl.BlockSpec((8, NS), lambda c: (c, 0))],
        out_shape=[jax.ShapeDtypeStruct((L, SW), f32), jax.ShapeDtypeStruct((L, SW), bf16),
                   jax.ShapeDtypeStruct((L, NS), f32), jax.ShapeDtypeStruct((L, NS), f32),
                   jax.ShapeDtypeStruct((nc * 8, NS), f32), jax.ShapeDtypeStruct((nc * 8, NS), f32)],
        scratch_shapes=[pltpu.VMEM((8, NS), f32), pltpu.VMEM((8, NS), f32)],
        name="ssm_fwd", compiler_params=_cp("arbitrary"))(
            proj, proj, bbt_re, bbt_im, ct_re, ct_im, a_re, a_im, apow_re, apow_im, dvec, wglu, bglu)


def _tail(a_in, b_in, proj, x, tgt, wco, wso, wout, gpost):
    L = x.shape[0]
    tm = 256
    def body(a_ref, b_ref, gc_ref, gs_ref, x_ref, t_ref, wco_ref, wso_ref, wout_ref, gp_ref,
             gx_ref, dain_ref, dbin_ref, dp_ref, dwout_ref, dwco_ref, dwso_ref, dgp_ref, loss_ref):
        @pl.when(pl.program_id(0) == 0)
        def _():
            dwout_ref[...] = jnp.zeros_like(dwout_ref)
            dwco_ref[...] = jnp.zeros_like(dwco_ref)
            dwso_ref[...] = jnp.zeros_like(dwso_ref)
            dgp_ref[...] = jnp.zeros_like(dgp_ref)
            loss_ref[...] = jnp.zeros_like(loss_ref)
        a = a_ref[...]
        b = b_ref[...]
        co = _dot(a, wco_ref[...])
        so = jnp.concatenate([_dot(b, wso_ref[j]) for j in range(NCHIP)], axis=1)
        sc = _sig(gc_ref[...].astype(f32))
        ss = _sig(gs_ref[...].astype(f32))
        mb = (sc * co + ss * so).astype(bf16)
        out = _dot(mb, wout_ref[...])
        r2 = lax.rsqrt(jnp.mean(out * out, axis=-1, keepdims=True) + RMS_EPS)
        on = out * r2
        gp = gp_ref[...]
        e = x_ref[...] + on * gp - t_ref[...]
        loss_ref[...] += (0.5 / D) * jnp.sum(e * e)
        dy = e * (1.0 / D)
        gx_ref[...] = dy
        dgp_ref[...] += jnp.sum(dy * on, axis=0, keepdims=True)
        dn = dy * gp
        dout = (r2 * (dn - on * jnp.mean(dn * on, axis=-1, keepdims=True))).astype(bf16)
        dwout_ref[...] += _dot_tn(mb, dout)
        dm = _dot_nt(dout, wout_ref[...])
        dp_ref[:, 0:D] = (dm * co * sc * (1.0 - sc)).astype(bf16)
        dp_ref[:, D:2 * D] = (dm * so * ss * (1.0 - ss)).astype(bf16)
        dco = (dm * sc).astype(bf16)
        dso = (dm * ss).astype(bf16)
        dwco_ref[...] += _dot_tn(a, dco)
        dbin = None
        for j in range(NCHIP):
            dso_j = dso[:, j * 256:(j + 1) * 256]
            dwso_ref[j] += _dot_tn(b, dso_j)
            t = _dot_nt(dso_j, wso_ref[j])
            dbin = t if dbin is None else dbin + t
        dain_ref[...] = _dot_nt(dco, wco_ref[...]).astype(bf16)
        dbin_ref[...] = dbin.astype(bf16)

    row = lambda w: pl.BlockSpec((tm, w), lambda i: (i, 0))
    one = lambda shape: pl.BlockSpec(shape, lambda i: (0,) * len(shape), pipeline_mode=pl.Buffered(1))
    return pl.pallas_call(
        body, grid=(L // tm,),
        in_specs=[row(CW), row(SW), pl.BlockSpec((tm, D), lambda i: (i, 4)), pl.BlockSpec((tm, D), lambda i: (i, 5)),
                  row(D), row(D), one((CW, D)), one((NCHIP, SW, 256)), one((D, D)), one((1, D))],
        out_specs=[row(D), row(CW), row(SW), pl.BlockSpec((tm, 2 * D), lambda i: (i, 2)),
                   one((D, D)), one((CW, D)), one((NCHIP, SW, 256)), one((1, D)), one((1, 128))],
        out_shape=[jax.ShapeDtypeStruct((L, D), f32), jax.ShapeDtypeStruct((L, CW), bf16),
                   jax.ShapeDtypeStruct((L, SW), bf16), jax.ShapeDtypeStruct((L, IN_W), bf16),
                   jax.ShapeDtypeStruct((D, D), f32), jax.ShapeDtypeStruct((CW, D), f32),
                   jax.ShapeDtypeStruct((NCHIP, SW, 256), f32), jax.ShapeDtypeStruct((1, D), f32),
                   jax.ShapeDtypeStruct((1, 128), f32)],
        name="tail", compiler_params=_cp("arbitrary"))(a_in, b_in, proj, proj, x, tgt, wco, wso, wout, gpost)


def _ssm_bwd(d_bin, y0, proj, sre, sim, cinr, cini, bbt_re, bbt_im, ct_re, ct_im,
             a_re, a_im, apow_re, apow_im, dvec, wglu, bglu, dproj):
    L = y0.shape[0]
    nc = L // TC
    def body(dbin_ref, y0_ref, u_ref, zs_ref, sre_ref, sim_ref, cinr_ref, cini_ref,
             bre_ref, bim_ref, cre_ref, cim_ref, are_ref, aim_ref, pwr_ref, pwi_ref, d_ref, wg_ref, bg_ref, _,
             dp_ref, dbre_ref, dbim_ref, dcre_ref, dcim_ref, dd_ref, dar_ref, dai_ref, dwg_ref, dbg_ref,
             gre, gim, gcr, gci, nxt_re, nxt_im):
        @pl.when(pl.program_id(0) == 0)
        def _():
            for ref in (dbre_ref, dbim_ref, dcre_ref, dcim_ref, dd_ref, dar_ref, dai_ref, dwg_ref, dbg_ref,
                        nxt_re, nxt_im):
                ref[...] = jnp.zeros_like(ref)
        y0 = y0_ref[...]
        u = u_ref[...]
        zs = zs_ref[...].astype(f32)
        dbin = dbin_ref[...].astype(f32)
        t, y1 = _gelu_parts(y0)
        y1b = y1.astype(bf16)
        sg = _sig(_dot(y1b, wg_ref[...]) + bg_ref[...])
        sz = _sig(zs)
        d_y2 = dbin * (zs * sz)
        dp_ref[:, SW:2 * SW] = (dbin * (y1 * sg) * (sz * (1.0 + zs * (1.0 - sz)))).astype(bf16)
        d_glu = d_y2 * y1 * sg * (1.0 - sg)
        d_glub = d_glu.astype(bf16)
        d_y1 = d_y2 * sg + _dot_nt(d_glub, wg_ref[...])
        dwg_ref[...] += _dot_tn(y1b, d_glub)
        dbg_ref[...] += jnp.sum(d_glu, axis=0, keepdims=True)
        dgelu = 0.5 * (1.0 + t) + 0.5 * y0 * (1.0 - t * t) * GELU_K0 * (1.0 + 3.0 * GELU_K1 * y0 * y0)
        d_y0 = d_y1 * dgelu
        dd_ref[...] += jnp.sum(d_y0 * u.astype(f32), axis=0, keepdims=True)
        dyb = d_y0.astype(bf16)
        for blk in range(4):
            dy1 = dyb[:, 128 * blk:128 * (blk + 1)]
            gre[:, 512 * blk:512 * (blk + 1)] = _dot_nt(dy1, cre_ref[blk])
            gim[:, 512 * blk:512 * (blk + 1)] = -_dot_nt(dy1, cim_ref[blk])
        for lb in range(NS // LBW):
            sl = slice(lb * LBW, (lb + 1) * LBW)
            ar = jnp.broadcast_to(are_ref[:, sl], (8, LBW))
            ai = jnp.broadcast_to(aim_ref[:, sl], (8, LBW))
            def step(k, carry, sl=sl, ar=ar, ai=ai):
                gr, gi = carry
                row = _rows8(R - 2 - k)
                nr = ar * gr + ai * gi + gre[row, sl]
                ni = ar * gi - ai * gr + gim[row, sl]
                gre[row, sl] = nr
                gim[row, sl] = ni
                return nr, ni
            lax.fori_loop(0, R - 1, step, (gre[8 * (R - 1):8 * R, sl], gim[8 * (R - 1):8 * R, sl]))
        a_r = pwr_ref[R - 1:R, :]
        a_i = pwi_ref[R - 1:R, :]
        cr = nxt_re[0:1, :]
        ci = nxt_im[0:1, :]
        for seg in range(7, -1, -1):
            gcr[seg:seg + 1, :] = cr
            gci[seg:seg + 1, :] = ci
            er = gre[seg:seg + 1, :]
            ei = gim[seg:seg + 1, :]
            cr, ci = er + a_r * cr + a_i * ci, ei + a_r * ci - a_i * cr
        nxt_re[0:1, :] = cr
        nxt_im[0:1, :] = ci
        for lb in range(NS // LBW):
            sl = slice(lb * LBW, (lb + 1) * LBW)
            kr = gcr[:, sl]
            ki = gci[:, sl]
            def fix(r, carry, sl=sl, kr=kr, ki=ki):
                pr = jnp.broadcast_to(pwr_ref[pl.ds(R - 1 - r, 1), sl], (8, LBW))
                pi = jnp.broadcast_to(pwi_ref[pl.ds(R - 1 - r, 1), sl], (8, LBW))
                gre[_rows8(r), sl] = gre[_rows8(r), sl] + pr * kr + pi * ki
                gim[_rows8(r), sl] = gim[_rows8(r), sl] + pr * ki - pi * kr
                return carry
            lax.fori_loop(0, R, fix, 0)
        dup = []
        for blk in range(4):
            s4 = slice(512 * blk, 512 * (blk + 1))
            s1 = slice(128 * blk, 128 * (blk + 1))
            grb = gre[:, s4].astype(bf16)
            gib = gim[:, s4].astype(bf16)
            dup.append(_dot_nt(grb, bre_ref[blk]) + _dot_nt(gib, bim_ref[blk]))
            dbre_ref[blk] += _dot_tn(u[:, s1], grb)
            dbim_ref[blk] += _dot_tn(u[:, s1], gib)
            dcre_ref[blk] += _dot_tn(sre_ref[:, s4].astype(bf16), dyb[:, s1])
            dcim_ref[blk] -= _dot_tn(sim_ref[:, s4].astype(bf16), dyb[:, s1])
        dp_ref[:, 0:SW] = (jnp.concatenate(dup, axis=1) + d_ref[...] * d_y0).astype(bf16)
        for lb in range(NS // LBW):
            sl = slice(lb * LBW, (lb + 1) * LBW)
            g0r, g0i = gre[0:8, sl], gim[0:8, sl]
            p0r, p0i = cinr_ref[:, sl], cini_ref[:, sl]
            acc0 = (g0r * p0r + g0i * p0i, g0i * p0r - g0r * p0i)
            def dacc(r, carry, sl=sl):
                xr, xi = carry
                gr, gi = gre[_rows8(r), sl], gim[_rows8(r), sl]
                pr, pi = sre_ref[_rows8(r - 1), sl], sim_ref[_rows8(r - 1), sl]
                return xr + gr * pr + gi * pi, xi + gi * pr - gr * pi
            xr, xi = lax.fori_loop(1, R, dacc, acc0)
            dar_ref[:, sl] += xr
            dai_ref[:, sl] += xi

    rev = lambda w, cidx: pl.BlockSpec((TC, w), lambda i, cidx=cidx: (nc - 1 - i, cidx))
    one = lambda shape: pl.BlockSpec(shape, lambda i: (0,) * len(shape))
    return pl.pallas_call(
        body, grid=(nc,),
        in_specs=[rev(SW, 0), rev(SW, 0), rev(SW, 6), rev(SW, 7), rev(NS, 0), rev(NS, 0),
                  pl.BlockSpec((8, NS), lambda i: (nc - 1 - i, 0)), pl.BlockSpec((8, NS), lambda i: (nc - 1 - i, 0)),
                  one((4, 128, 512)), one((4, 128, 512)), one((4, 512, 128)), one((4, 512, 128)),
                  one((1, NS)), one((1, NS)), one((R, NS)), one((R, NS)),
                  one((1, SW)), one((SW, SW)), one((1, SW)), _ANY],
        out_specs=[pl.BlockSpec((TC, 2 * SW), lambda i: (nc - 1 - i, 3)),
                   one((4, 128, 512)), one((4, 128, 512)), one((4, 512, 128)), one((4, 512, 128)),
                   one((1, SW)), one((8, NS)), one((8, NS)), one((SW, SW)), one((1, SW))],
        out_shape=[jax.ShapeDtypeStruct((L, IN_W), bf16),
                   jax.ShapeDtypeStruct((4, 128, 512), f32), jax.ShapeDtypeStruct((4, 128, 512), f32),
                   jax.ShapeDtypeStruct((4, 512, 128), f32), jax.ShapeDtypeStruct((4, 512, 128), f32),
                   jax.ShapeDtypeStruct((1, SW), f32), jax.ShapeDtypeStruct((8, NS), f32),
                   jax.ShapeDtypeStruct((8, NS), f32), jax.ShapeDtypeStruct((SW, SW), f32),
                   jax.ShapeDtypeStruct((1, SW), f32)],
        scratch_shapes=[pltpu.VMEM((TC, NS), f32), pltpu.VMEM((TC, NS), f32), pltpu.VMEM((8, NS), f32),
                        pltpu.VMEM((8, NS), f32), pltpu.VMEM((8, NS), f32), pltpu.VMEM((8, NS), f32)],
        input_output_aliases={19: 0},
        name="ssm_bwd", compiler_params=_cp("arbitrary"))(
            d_bin, y0, proj, proj, sre, sim, cinr, cini, bbt_re, bbt_im, ct_re, ct_im,
            a_re, a_im, apow_re, apow_im, dvec, wglu, bglu, dproj)


def _conv_bwd(d_ain, cu1, proj, cw, lng, lnb, dproj):
    L = cu1.shape[0]
    nc = L // TC
    def body(dain_ref, cu1_ref, ca_ref, cb_ref, zc_ref, cah_ref, cbh_ref, w_ref, g_ref, bb_ref, _,
             dp_ref, dw_ref, dbias_ref, dlng_ref, dlnb_ref, dbuf, ebuf, prev, nxt, dcu0):
        i = pl.program_id(0)
        @pl.when(i == 0)
        def _():
            dw_ref[...] = jnp.zeros_like(dw_ref)
            dbias_ref[...] = jnp.zeros_like(dbias_ref)
            dlng_ref[...] = jnp.zeros_like(dlng_ref)
            dlnb_ref[...] = jnp.zeros_like(dlnb_ref)
            nxt[...] = jnp.zeros_like(nxt)
        def lnb(s, carry):
            rows = pl.ds(pl.multiple_of(s * 32, 32), 32)
            dain = dain_ref[rows, :].astype(f32)
            c1 = cu1_ref[rows, :].astype(f32)
            zc = zc_ref[rows, :].astype(f32)
            xc = c1 - jnp.mean(c1, axis=-1, keepdims=True)
            var = jnp.mean(xc * xc, axis=-1, keepdims=True)
            rstd = lax.rsqrt(var + LN_EPS)
            xh = xc * rstd
            ln = xh * g_ref[...] + bb_ref[...]
            sl_ = _sig(ln)
            sz = _sig(zc)
            dp_ref[rows, 2 * CW:3 * CW] = (dain * (ln * sl_) * (sz * (1.0 + zc * (1.0 - sz)))).astype(bf16)
            d_ln = dain * (zc * sz) * (sl_ * (1.0 + ln * (1.0 - sl_)))
            dlng_ref[...] += jnp.sum(d_ln * xh, axis=0, keepdims=True)
            dlnb_ref[...] += jnp.sum(d_ln, axis=0, keepdims=True)
            dxh = d_ln * g_ref[...]
            d_c1 = rstd * (dxh - jnp.mean(dxh, axis=-1, keepdims=True)
                           - xh * jnp.mean(dxh * xh, axis=-1, keepdims=True))
            dbias_ref[...] += jnp.sum(d_c1, axis=0, keepdims=True)
            _put_blocked(dbuf, pl.multiple_of(s * 32, 32), 32, d_c1)
            _put_blocked(ebuf, pl.multiple_of(NH * 8 + s * 32, 32), 32,
                         ca_ref[rows, :].astype(f32) * _sig(cb_ref[rows, :].astype(f32)))
            return carry
        lax.fori_loop(0, TC // 32, lnb, 0)
        sub = lax.broadcasted_iota(jnp.int32, (8, 128), 0)
        def after(p, carry):
            for lb in range(NLB):
                cur = dbuf[lb, _rows8(p), :]
                dbuf[lb, _rows8(R + p), :] = jnp.where(sub == 7, pltpu.roll(nxt[lb, _rows8(p), :], 7, 0),
                                                       pltpu.roll(cur, 7, 0))
            return carry
        lax.fori_loop(0, NH, after, 0)
        nxt[...] = dbuf[:, 0:NH * 8, :]
        def before(s, carry):
            rows = pl.ds(pl.multiple_of(s * 64, 64), 64)
            v = cah_ref[rows, :].astype(f32) * _sig(cbh_ref[rows, :].astype(f32))
            _put_blocked(prev, pl.multiple_of(s * 64, 64), 64, jnp.where(i == nc - 1, jnp.zeros_like(v), v))
            return carry
        lax.fori_loop(0, NH * 8 // 64, before, 0)
        _fill_before(ebuf, prev)
        for lb in range(NLB):
            sl = slice(lb * 128, (lb + 1) * 128)
            wk = [jnp.broadcast_to(w_ref[k:k + 1, sl], (8, 128)) for k in range(KS)]
            def tap(q, carry, lb=lb, wk=wk):
                r = q * RPI
                for j, o in enumerate(_fir(dbuf, lb, r, wk, None, True)):
                    dcu0[lb, _rows8(r + j), :] = o
                return carry
            lax.fori_loop(0, R // RPI, tap, 0)
            def wgrad(q, accs, lb=lb):
                r = q * RPI
                dvs = dbuf[lb, pl.ds(pl.multiple_of(r * 8, 8), RPI * 8), :]
                win = ebuf[lb, pl.ds(pl.multiple_of((r + (NH - KS + 1)) * 8, 8), (KS + RPI - 1) * 8), :]
                accs = list(accs)
                for j in range(RPI):
                    dv = dvs[8 * j:8 * j + 8, :]
                    for k in range(KS):
                        accs[k] = accs[k] + dv * win[8 * (j + k):8 * (j + k) + 8, :]
                return tuple(accs)
            accs = lax.fori_loop(0, R // RPI, wgrad, tuple(jnp.zeros((8, 128), f32) for _ in range(KS)))
            for k in range(KS):
                dw_ref[k, :, sl] += accs[k]
        def glub(s, carry):
            rows = pl.ds(pl.multiple_of(s * 64, 64), 64)
            d0 = _get_blocked(dcu0, pl.multiple_of(s * 64, 64), 64)
            ca = ca_ref[rows, :].astype(f32)
            sb = _sig(cb_ref[rows, :].astype(f32))
            dp_ref[rows, 0:CW] = (d0 * sb).astype(bf16)
            dp_ref[rows, CW:2 * CW] = (d0 * ca * sb * (1.0 - sb)).astype(bf16)
            return carry
        lax.fori_loop(0, TC // 64, glub, 0)

    hrows = NH * 8
    per = TC // hrows
    rev = lambda cidx: pl.BlockSpec((TC, CW), lambda i, cidx=cidx: (nc - 1 - i, cidx))
    halo = lambda cidx: pl.BlockSpec((hrows, CW), lambda i, cidx=cidx: (jnp.maximum((nc - 1 - i) * per - 1, 0), cidx))
    one = lambda shape: pl.BlockSpec(shape, lambda i: (0,) * len(shape))
    return pl.pallas_call(
        body, grid=(nc,),
        in_specs=[rev(0), rev(0), rev(0), rev(1), rev(2), halo(0), halo(1), one((32, CW)), one((1, CW)), one((1, CW)),
                  _ANY],
        out_specs=[pl.BlockSpec((TC, 3 * CW), lambda i: (nc - 1 - i, 0)), one((32, 8, CW)), one((1, CW)), one((1, CW)), one((1, CW))],
        out_shape=[jax.ShapeDtypeStruct((L, IN_W), bf16), jax.ShapeDtypeStruct((32, 8, CW), f32),
                   jax.ShapeDtypeStruct((1, CW), f32), jax.ShapeDtypeStruct((1, CW), f32),
                   jax.ShapeDtypeStruct((1, CW), f32)],
        scratch_shapes=[pltpu.VMEM((NLB, (R + NH) * 8, 128), f32), pltpu.VMEM((NLB, (NH + R) * 8, 128), f32),
                        pltpu.VMEM((NLB, hrows, 128), f32), pltpu.VMEM((NLB, hrows, 128), f32),
                        pltpu.VMEM((NLB, TC, 128), f32)],
        input_output_aliases={10: 0},
        name="conv_bwd", compiler_params=_cp("arbitrary"))(d_ain, cu1, proj, proj, proj, proj, proj, cw, lng, lnb, dproj)


def _win_grad(h, dproj):
    L = h.shape[0]
    tm = min(1024, L)
    def body(h_ref, d_ref, o_ref):
        @pl.when(pl.program_id(1) == 0)
        def _():
            o_ref[...] = jnp.zeros_like(o_ref)
        o_ref[0] += _dot_tn(h_ref[...], d_ref[...])
    return pl.pallas_call(
        body, grid=(NCHIP, L // tm),
        in_specs=[pl.BlockSpec((tm, D), lambda j, i: (i, 0)), pl.BlockSpec((tm, SHARD_W), lambda j, i: (i, j))],
        out_specs=pl.BlockSpec((1, D, SHARD_W), lambda j, i: (j, 0, 0)),
        out_shape=jax.ShapeDtypeStruct((NCHIP, D, SHARD_W), f32),
        name="win_grad", compiler_params=_cp("arbitrary", "arbitrary"))(h, dproj)


def _x_grad(dproj, w_in, x, gx0, g_pre):
    L = x.shape[0]
    tm = 256
    def body(d_ref, w_ref, x_ref, gx_ref, g_ref, o_ref, dg_ref):
        @pl.when(pl.program_id(0) == 0)
        def _():
            dg_ref[...] = jnp.zeros_like(dg_ref)
        dh = _dot_nt(d_ref[:, 0:SHARD_W], w_ref[0])
        for j in range(1, NCHIP):
            dh = dh + _dot_nt(d_ref[:, j * SHARD_W:(j + 1) * SHARD_W], w_ref[j])
        xt = x_ref[...]
        r = lax.rsqrt(jnp.mean(xt * xt, axis=-1, keepdims=True) + RMS_EPS)
        xn = xt * r
        dg_ref[...] += jnp.sum(dh * xn, axis=0, keepdims=True)
        dxn = dh * g_ref[...]
        o_ref[...] = gx_ref[...] + r * (dxn - xn * jnp.mean(dxn * xn, axis=-1, keepdims=True))
    return pl.pallas_call(
        body, grid=(L // tm,),
        in_specs=[pl.BlockSpec((tm, IN_W), lambda i: (i, 0)),
                  pl.BlockSpec((NCHIP, D, SHARD_W), lambda i: (0, 0, 0), pipeline_mode=pl.Buffered(1)),
                  pl.BlockSpec((tm, D), lambda i: (i, 0)), pl.BlockSpec((tm, D), lambda i: (i, 0)), _full((1, D))],
        out_specs=[pl.BlockSpec((tm, D), lambda i: (i, 0)), _full((1, D))],
        out_shape=[jax.ShapeDtypeStruct((L, D), f32), jax.ShapeDtypeStruct((1, D), f32)],
        name="x_grad", compiler_params=_cp("arbitrary"))(dproj, w_in, x, gx0, g_pre)


def _pair_sum(c_arr, ga, ra, gb, rb, gs, rs):
    def body(c_ref, ga_ref, ra_ref, gb_ref, rb_ref, gs_ref, rs_ref, pa_ref, pb_ref, ps_ref):
        pa_ref[...] = (ga_ref[...] + ra_ref[...]).astype(bf16)
        pb_ref[...] = (gb_ref[...] + rb_ref[...]).astype(bf16)
        ps_ref[...] = gs_ref[...] + rs_ref[...]
    grid_spec = pltpu.PrefetchScalarGridSpec(
        num_scalar_prefetch=1, grid=(NCHIP,),
        in_specs=[pl.BlockSpec((1, D // 2, SHARD_W), lambda j, c: (j, c[0], 0)),
                  pl.BlockSpec((1, D // 2, SHARD_W), lambda j, c: (j, 0, 0)),
                  pl.BlockSpec((1, REST_ROWS // 2, 1024), lambda j, c: (j, c[0], 0)),
                  pl.BlockSpec((1, REST_ROWS // 2, 1024), lambda j, c: (j, 0, 0)),
                  pl.BlockSpec((SMALL_ROWS, 128), lambda j, c: (0, 0)),
                  pl.BlockSpec((SMALL_ROWS, 128), lambda j, c: (0, 0))],
        out_specs=[pl.BlockSpec((1, D // 2, SHARD_W), lambda j, c: (j, 0, 0)),
                   pl.BlockSpec((1, REST_ROWS // 2, 1024), lambda j, c: (j, 0, 0)),
                   pl.BlockSpec((SMALL_ROWS, 128), lambda j, c: (0, 0))])
    return pl.pallas_call(
        body, grid_spec=grid_spec,
        out_shape=[jax.ShapeDtypeStruct((NCHIP, D // 2, SHARD_W), bf16),
                   jax.ShapeDtypeStruct((NCHIP, REST_ROWS // 2, 1024), bf16),
                   jax.ShapeDtypeStruct((SMALL_ROWS, 128), f32)],
        name="pair_sum", compiler_params=_cp("arbitrary"))(c_arr, ga, ra, gb, rb, gs, rs)


def _chip_sum(qa, qb, qs):
    nt = 4
    def body(qa_ref, qb_ref, qs_ref, fa_ref, fb_ref, fs_ref):
        for q_ref, f_ref in ((qa_ref, fa_ref), (qb_ref, fb_ref), (qs_ref, fs_ref)):
            acc = q_ref[0].astype(f32)
            for j in range(1, NCHIP):
                acc = acc + q_ref[j].astype(f32)
            f_ref[...] = acc
    ra, rb, rs = D // 2 // nt, REST_ROWS // 2 // nt, SMALL_ROWS // nt
    return pl.pallas_call(
        body, grid=(nt,),
        in_specs=[pl.BlockSpec((NCHIP, ra, SHARD_W), lambda i: (0, i, 0)),
                  pl.BlockSpec((NCHIP, rb, 1024), lambda i: (0, i, 0)),
                  pl.BlockSpec((NCHIP, rs, 128), lambda i: (0, i, 0))],
        out_specs=[pl.BlockSpec((ra, SHARD_W), lambda i: (i, 0)), pl.BlockSpec((rb, 1024), lambda i: (i, 0)),
                   pl.BlockSpec((rs, 128), lambda i: (i, 0))],
        out_shape=[jax.ShapeDtypeStruct((D // 2, SHARD_W), f32), jax.ShapeDtypeStruct((REST_ROWS // 2, 1024), f32),
                   jax.ShapeDtypeStruct((SMALL_ROWS, 128), f32)],
        name="chip_sum", compiler_params=_cp("arbitrary"))(qa, qb, qs)


def _adamw_math(w, g, m, v):
    m2 = B1 * m + (1.0 - B1) * g
    v2 = B2 * v + (1.0 - B2) * (g * g)
    m_hat = m2 / (1.0 - B1 ** STEP)
    v_hat = v2 / (1.0 - B2 ** STEP)
    delta = -LR * (m_hat / (jnp.sqrt(v_hat) + EPS) + WD * w)
    return delta, m2, v2


def _adamw(name, w, g, m, v):
    rows, cols = w.shape
    tm = rows if rows <= 256 else (256 if rows % 256 == 0 else 128)
    assert rows % tm == 0
    def body(w_ref, g_ref, m_ref, v_ref, d_ref, m2_ref, v2_ref):
        d, m2, v2 = _adamw_math(w_ref[...], g_ref[...], m_ref[...], v_ref[...])
        d_ref[...] = d
        m2_ref[...] = m2
        v2_ref[...] = v2
    spec = pl.BlockSpec((tm, cols), lambda i: (i, 0))
    shp = jax.ShapeDtypeStruct((rows, cols), f32)
    return pl.pallas_call(
        body, grid=(rows // tm,), in_specs=[spec] * 4, out_specs=[spec] * 3, out_shape=[shp] * 3,
        name=name, compiler_params=_cp("arbitrary"))(w, g, m, v)


_ANY = pl.BlockSpec(memory_space=pl.ANY)


def _chunks(rows, parts):
    step = rows // parts
    assert step * parts == rows and step % 16 == 0
    return [(i * step, step) for i in range(parts)]


def _place():
    x, y, c = lax.axis_index("x"), lax.axis_index("y"), lax.axis_index("c")
    chips = [(1 - x, y), (x, 1 - y), (1 - x, 1 - y)]
    return x, y, c, chips


def _gather_weights(win_s, rest_s):
    segs = [(0, D // 2, r0, n) for r0, n in _chunks(D // 2, 4)] + \
           [(1, REST_ROWS // 2, r0, n) for r0, n in _chunks(REST_ROWS // 2, 2)]
    ns = len(segs)
    def body(a_ref, b_ref, oa_ref, ob_ref, send_sems, recv_sems, local_sems):
        x, y, c, chips = _place()
        k = 2 * x + y
        sibling = (x, y, 1 - c)
        ins, outs = (a_ref, b_ref), (oa_ref, ob_ref)

        def dst(which, half, chip, pc, r0, n):
            return outs[which].at[chip, pl.ds(pc * half + r0, n), :]

        def rcopy(i, src, dst_ref, to):
            return pltpu.make_async_remote_copy(src_ref=src, dst_ref=dst_ref, send_sem=send_sems.at[i],
                                                recv_sem=recv_sems.at[i], device_id=to, device_id_type=MESH)

        own = [pltpu.make_async_copy(ins[w], outs[w].at[k], local_sems.at[w]) for w in range(2)]
        for cp in own:
            cp.start()
        first = []
        for j, chip in enumerate(chips):
            for s, (w, half, r0, n) in enumerate(segs):
                first.append(rcopy(j * ns + s, ins[w].at[pl.ds(c * half + r0, n), :], dst(w, half, k, c, r0, n),
                                   (*chip, c)))
        for cp in first:
            cp.start()
        passed = []
        for j, chip in enumerate(chips):
            cj = 2 * chip[0] + chip[1]
            for s, (w, half, r0, n) in enumerate(segs):
                landed = dst(w, half, cj, c, r0, n)
                rcopy(j * ns + s, landed, landed, (x, y, c)).wait_recv()
                fwd = rcopy(3 * ns + j * ns + s, landed, landed, sibling)
                fwd.start()
                passed.append(fwd)
        for j, chip in enumerate(chips):
            cj = 2 * chip[0] + chip[1]
            for s, (w, half, r0, n) in enumerate(segs):
                theirs = dst(w, half, cj, 1 - c, r0, n)
                rcopy(3 * ns + j * ns + s, theirs, theirs, (x, y, c)).wait_recv()
        for cp in first + passed:
            cp.wait_send()
        for cp in own:
            cp.wait()

    return pl.pallas_call(
        body, in_specs=[_ANY, _ANY], out_specs=[_ANY, _ANY],
        out_shape=[jax.ShapeDtypeStruct((NCHIP, D, SHARD_W), bf16), jax.ShapeDtypeStruct((NCHIP, REST_ROWS, 1024), bf16)],
        scratch_shapes=[pltpu.SemaphoreType.DMA((6 * ns,)), pltpu.SemaphoreType.DMA((6 * ns,)),
                        pltpu.SemaphoreType.DMA((2,))],
        name="gather_weights")(win_s, rest_s)


def _pair_exchange(ga, gb, gs):
    ha, hb = D // 2, REST_ROWS // 2
    def body(a_ref, b_ref, s_ref, ra_ref, rb_ref, rs_ref, send_sems, recv_sems):
        x, y, c, _ = _place()
        sibling = (x, y, 1 - c)
        pieces = []
        for j in range(NCHIP):
            for r0, n in _chunks(ha, 4):
                pieces.append((a_ref.at[j, pl.ds((1 - c) * ha + r0, n), :], ra_ref.at[j, pl.ds(r0, n), :]))
            for r0, n in _chunks(hb, 2):
                pieces.append((b_ref.at[j, pl.ds((1 - c) * hb + r0, n), :], rb_ref.at[j, pl.ds(r0, n), :]))
        pieces.append((s_ref, rs_ref))
        copies = [pltpu.make_async_remote_copy(src_ref=s, dst_ref=d, send_sem=send_sems.at[i], recv_sem=recv_sems.at[i],
                                               device_id=sibling, device_id_type=MESH)
                  for i, (s, d) in enumerate(pieces)]
        for cp in copies:
            cp.start()
        for cp in copies:
            cp.wait_recv()
        for cp in copies:
            cp.wait_send()

    n = NCHIP * 6 + 1
    return pl.pallas_call(
        body, in_specs=[_ANY, _ANY, _ANY], out_specs=[_ANY, _ANY, _ANY],
        out_shape=[jax.ShapeDtypeStruct((NCHIP, ha, SHARD_W), f32), jax.ShapeDtypeStruct((NCHIP, hb, 1024), f32),
                   jax.ShapeDtypeStruct((SMALL_ROWS, 128), f32)],
        scratch_shapes=[pltpu.SemaphoreType.DMA((n,)), pltpu.SemaphoreType.DMA((n,))],
        name="pair_exchange")(ga, gb, gs)


def _chip_exchange(pa, pb, ps):
    ha, hb = D // 2, REST_ROWS // 2
    def body(a_ref, b_ref, s_ref, qa_ref, qb_ref, qs_ref, send_sems, recv_sems, local_sems):
        x, y, c, chips = _place()
        k = 2 * x + y
        own = [pltpu.make_async_copy(a_ref.at[k], qa_ref.at[k], local_sems.at[0]),
               pltpu.make_async_copy(b_ref.at[k], qb_ref.at[k], local_sems.at[1]),
               pltpu.make_async_copy(s_ref, qs_ref.at[k], local_sems.at[2])]
        for cp in own:
            cp.start()
        copies = []
        for j, chip in enumerate(chips):
            cj = 2 * chip[0] + chip[1]
            pieces = [(a_ref.at[cj, pl.ds(r0, n), :], qa_ref.at[k, pl.ds(r0, n), :]) for r0, n in _chunks(ha, 2)]
            pieces += [(b_ref.at[cj], qb_ref.at[k]), (s_ref, qs_ref.at[k])]
            for s, (src, dst_ref) in enumerate(pieces):
                copies.append(pltpu.make_async_remote_copy(
                    src_ref=src, dst_ref=dst_ref, send_sem=send_sems.at[4 * j + s], recv_sem=recv_sems.at[4 * j + s],
                    device_id=(*chip, c), device_id_type=MESH))
        for cp in copies:
            cp.start()
        for cp in copies:
            cp.wait_recv()
        for cp in copies:
            cp.wait_send()
        for cp in own:
            cp.wait()

    return pl.pallas_call(
        body, in_specs=[_ANY, _ANY, _ANY], out_specs=[_ANY, _ANY, _ANY],
        out_shape=[jax.ShapeDtypeStruct((NCHIP, ha, SHARD_W), bf16), jax.ShapeDtypeStruct((NCHIP, hb, 1024), bf16),
                   jax.ShapeDtypeStruct((NCHIP, SMALL_ROWS, 128), f32)],
        scratch_shapes=[pltpu.SemaphoreType.DMA((12,)), pltpu.SemaphoreType.DMA((12,)), pltpu.SemaphoreType.DMA((3,))],
        name="chip_exchange")(pa, pb, ps)


def _sibling_exchange(fa, fb):
    ha, hb = D // 2, REST_ROWS // 2
    def body(a_ref, b_ref, oa_ref, ob_ref, send_sems, recv_sems, local_sems):
        x, y, c, _ = _place()
        own = [pltpu.make_async_copy(a_ref, oa_ref.at[c], local_sems.at[0]),
               pltpu.make_async_copy(b_ref, ob_ref.at[c], local_sems.at[1])]
        for cp in own:
            cp.start()
        pieces = [(a_ref.at[pl.ds(r0, n), :], oa_ref.at[c, pl.ds(r0, n), :]) for r0, n in _chunks(ha, 4)]
        pieces += [(b_ref.at[pl.ds(r0, n), :], ob_ref.at[c, pl.ds(r0, n), :]) for r0, n in _chunks(hb, 2)]
        copies = [pltpu.make_async_remote_copy(src_ref=s, dst_ref=d, send_sem=send_sems.at[i], recv_sem=recv_sems.at[i],
                                               device_id=(x, y, 1 - c), device_id_type=MESH)
                  for i, (s, d) in enumerate(pieces)]
        for cp in copies:
            cp.start()
        for cp in copies:
            cp.wait_recv()
        for cp in copies:
            cp.wait_send()
        for cp in own:
            cp.wait()

    return pl.pallas_call(
        body, in_specs=[_ANY, _ANY], out_specs=[_ANY, _ANY],
        out_shape=[jax.ShapeDtypeStruct((2, ha, SHARD_W), f32), jax.ShapeDtypeStruct((2, hb, 1024), f32)],
        scratch_shapes=[pltpu.SemaphoreType.DMA((6,)), pltpu.SemaphoreType.DMA((6,)), pltpu.SemaphoreType.DMA((2,))],
        name="sibling_exchange")(fa, fb)


def _nchunks(half, cols, itemsize):
    return 4 if half * cols * itemsize >= (1 << 20) else 1


def _segments(metas):
    segs = []
    for w, (half, cols, dt) in enumerate(metas):
        for r0, n in _chunks(half, _nchunks(half, cols, jnp.dtype(dt).itemsize)):
            segs.append((w, half, r0, n))
    return segs


def _rcopy(i, src, dst, send_sems, recv_sems, to):
    return pltpu.make_async_remote_copy(src_ref=src, dst_ref=dst, send_sem=send_sems.at[i], recv_sem=recv_sems.at[i],
                                        device_id=to, device_id_type=MESH)


def _gather_list(shards):
    na = len(shards)
    segs = _segments([(a.shape[0] // 2, a.shape[1], a.dtype) for a in shards])
    ns = len(segs)
    def body(*refs):
        ins, outs, (send_sems, recv_sems) = refs[:na], refs[na:2 * na], refs[2 * na:]
        x, y, c, chips = _place()
        k = 2 * x + y
        me, sibling = (x, y, c), (x, y, 1 - c)

        def dst(w, half, chip, pc, r0, n):
            return outs[w].at[chip, pl.ds(pc * half + r0, n), :]

        first = []
        for j, chip in enumerate(chips):
            for s, (w, half, r0, n) in enumerate(segs):
                first.append(_rcopy(j * ns + s, ins[w].at[pl.ds(c * half + r0, n), :], dst(w, half, k, c, r0, n),
                                    send_sems, recv_sems, (*chip, c)))
        for cp in first:
            cp.start()
        passed = []
        for j, chip in enumerate(chips):
            cj = 2 * chip[0] + chip[1]
            for s, (w, half, r0, n) in enumerate(segs):
                landed = dst(w, half, cj, c, r0, n)
                _rcopy(j * ns + s, landed, landed, send_sems, recv_sems, me).wait_recv()
                fwd = _rcopy(3 * ns + j * ns + s, landed, landed, send_sems, recv_sems, sibling)
                fwd.start()
                passed.append(fwd)
        for j, chip in enumerate(chips):
            cj = 2 * chip[0] + chip[1]
            for s, (w, half, r0, n) in enumerate(segs):
                theirs = dst(w, half, cj, 1 - c, r0, n)
                _rcopy(3 * ns + j * ns + s, theirs, theirs, send_sems, recv_sems, me).wait_recv()
        for cp in first + passed:
            cp.wait_send()

    return pl.pallas_call(
        body, in_specs=[_ANY] * na, out_specs=[_ANY] * na,
        out_shape=[jax.ShapeDtypeStruct((NCHIP,) + a.shape, a.dtype) for a in shards],
        scratch_shapes=[pltpu.SemaphoreType.DMA((6 * ns,)), pltpu.SemaphoreType.DMA((6 * ns,))],
        name="gather_weights")(*shards)


def _gather_prep(shards, x, tgt, g_pre, perm):
    na = len(shards)
    L = x.shape[0]
    nc = L // TC
    segs = _segments([(a.shape[0] // 2, a.shape[1], a.dtype) for a in shards])
    ns = len(segs)
    def body(*refs):
        ins = refs[:na]
        x_ref, t_ref, g_ref, p_ref = refs[na:na + 4]
        outs = refs[na + 4:2 * na + 4]
        h_ref, xi_ref, ti_ref = refs[2 * na + 4:2 * na + 7]
        stages = refs[2 * na + 7:3 * na + 7]
        send_sems, recv_sems, local_sems = refs[3 * na + 7:]
        i = pl.program_id(0)
        x, y, c, chips = _place()
        k = 2 * x + y
        me, sibling = (x, y, c), (x, y, 1 - c)

        def dst(w, half, chip, pc, r0, n):
            return outs[w].at[chip, pl.ds(pc * half + r0, n), :]

        def firsts():
            return [_rcopy(j * ns + s, ins[w].at[pl.ds(c * half + r0, n), :], dst(w, half, k, c, r0, n),
                           send_sems, recv_sems, (*chip, c))
                    for j, chip in enumerate(chips) for s, (w, half, r0, n) in enumerate(segs)]

        def own_out(w):
            return pltpu.make_async_copy(stages[w], outs[w].at[k], local_sems.at[w])

        @pl.when(i == 0)
        def _():
            for cp in firsts():
                cp.start()
            for w in range(na):
                cin = pltpu.make_async_copy(ins[w], stages[w], local_sems.at[w])
                cin.start()
                cin.wait()
            for w in range(na):
                own_out(w).start()

        p = p_ref[...]
        def through(v):
            hi = v.astype(bf16)
            r1 = v - hi.astype(f32)
            mid = r1.astype(bf16)
            lo = (r1 - mid.astype(f32)).astype(bf16)
            return (_dot(p, hi) + _dot(p, mid)) + _dot(p, lo)
        xt = x_ref[...]
        r = lax.rsqrt(jnp.mean(xt * xt, axis=-1, keepdims=True) + RMS_EPS)
        h_ref[...] = _dot(p, (xt * r * g_ref[...]).astype(bf16)).astype(bf16)
        xi_ref[...] = through(xt)
        ti_ref[...] = through(t_ref[...])

        @pl.when(i == nc - 1)
        def _():
            passed = []
            for j, chip in enumerate(chips):
                cj = 2 * chip[0] + chip[1]
                for s, (w, half, r0, n) in enumerate(segs):
                    landed = dst(w, half, cj, c, r0, n)
                    _rcopy(j * ns + s, landed, landed, send_sems, recv_sems, me).wait_recv()
                    fwd = _rcopy(3 * ns + j * ns + s, landed, landed, send_sems, recv_sems, sibling)
                    fwd.start()
                    passed.append(fwd)
            for j, chip in enumerate(chips):
                cj = 2 * chip[0] + chip[1]
                for s, (w, half, r0, n) in enumerate(segs):
                    theirs = dst(w, half, cj, 1 - c, r0, n)
                    _rcopy(3 * ns + j * ns + s, theirs, theirs, send_sems, recv_sems, me).wait_recv()
            for cp in firsts() + passed:
                cp.wait_send()
            for w in range(na):
                own_out(w).wait()

    row = lambda: pl.BlockSpec((TC, D), lambda i: (i, 0))
    return pl.pallas_call(
        body, grid=(nc,),
        in_specs=[_ANY] * na + [row(), row(), _full((1, D)), _full((TC, TC))],
        out_specs=[_ANY] * na + [row(), row(), row()],
        out_shape=[jax.ShapeDtypeStruct((NCHIP,) + a.shape, a.dtype) for a in shards]
        + [jax.ShapeDtypeStruct((L, D), bf16), jax.ShapeDtypeStruct((L, D), f32), jax.ShapeDtypeStruct((L, D), f32)],
        scratch_shapes=[pltpu.VMEM(a.shape, a.dtype) for a in shards]
        + [pltpu.SemaphoreType.DMA((6 * ns,)), pltpu.SemaphoreType.DMA((6 * ns,)), pltpu.SemaphoreType.DMA((na,))],
        name="gather_prep", compiler_params=_cp("arbitrary"))(*shards, x, tgt, g_pre, perm)


def _x_grad_exchange(dproj, w_in, x, gx0, g_pre, parts, small):
    L = x.shape[0]
    tm = 256
    nt = L // tm
    na = len(parts)
    segs = _segments([(p.shape[1], p.shape[2], p.dtype) for p in parts])
    ns = len(segs) + 1
    def body(*refs):
        d_ref, w_ref, x_ref, gx_ref, g_ref = refs[:5]
        ins, s_ref = refs[5:5 + na], refs[5 + na]
        o_ref, dg_ref = refs[6 + na:8 + na]
        outs, qs_ref = refs[8 + na:8 + 2 * na], refs[8 + 2 * na]
        stages = refs[9 + 2 * na:10 + 3 * na]
        send_sems, recv_sems, local_sems = refs[10 + 3 * na:]
        i = pl.program_id(0)
        x, y, c, chips = _place()
        k = 2 * x + y

        def copies():
            out = []
            for j, chip in enumerate(chips):
                cj = 2 * chip[0] + chip[1]
                pieces = [(s_ref, qs_ref.at[k])]
                pieces += [(ins[w].at[cj, pl.ds(r0, n), :], outs[w].at[k, pl.ds(r0, n), :]) for w, _, r0, n in segs]
                out += [_rcopy(ns * j + s, src, d, send_sems, recv_sems, (*chip, c)) for s, (src, d) in enumerate(pieces)]
            return out

        def own_out(w):
            dst = qs_ref.at[k] if w == na else outs[w].at[k]
            return pltpu.make_async_copy(stages[w], dst, local_sems.at[w])

        @pl.when(i == 0)
        def _():
            dg_ref[...] = jnp.zeros_like(dg_ref)
            for cp in copies():
                cp.start()
            for w in range(na + 1):
                cin = pltpu.make_async_copy(s_ref if w == na else ins[w].at[k], stages[w], local_sems.at[w])
                cin.start()
                cin.wait()
            for w in range(na + 1):
                own_out(w).start()

        dh = _dot_nt(d_ref[:, 0:SHARD_W], w_ref[0])
        for j in range(1, NCHIP):
            dh = dh + _dot_nt(d_ref[:, j * SHARD_W:(j + 1) * SHARD_W], w_ref[j])
        xt = x_ref[...]
        r = lax.rsqrt(jnp.mean(xt * xt, axis=-1, keepdims=True) + RMS_EPS)
        xn = xt * r
        dg_ref[...] += jnp.sum(dh * xn, axis=0, keepdims=True)
        dxn = dh * g_ref[...]
        o_ref[...] = gx_ref[...] + r * (dxn - xn * jnp.mean(dxn * xn, axis=-1, keepdims=True))

        @pl.when(i == nt - 1)
        def _():
            for cp in copies():
                cp.wait_recv()
            for cp in copies():
                cp.wait_send()
            for w in range(na + 1):
                own_out(w).wait()

    return pl.pallas_call(
        body, grid=(nt,),
        in_specs=[pl.BlockSpec((tm, IN_W), lambda i: (i, 0)),
                  pl.BlockSpec((NCHIP, D, SHARD_W), lambda i: (0, 0, 0), pipeline_mode=pl.Buffered(1)),
                  pl.BlockSpec((tm, D), lambda i: (i, 0)), pl.BlockSpec((tm, D), lambda i: (i, 0)), _full((1, D))]
        + [_ANY] * (na + 1),
        out_specs=[pl.BlockSpec((tm, D), lambda i: (i, 0)), _full((1, D))] + [_ANY] * (na + 1),
        out_shape=[jax.ShapeDtypeStruct((L, D), f32), jax.ShapeDtypeStruct((1, D), f32)]
        + [jax.ShapeDtypeStruct(p.shape, bf16) for p in parts] + [jax.ShapeDtypeStruct((NCHIP, SMALL_ROWS, 128), f32)],
        scratch_shapes=[pltpu.VMEM(p.shape[1:], bf16) for p in parts] + [pltpu.VMEM((SMALL_ROWS, 128), f32)]
        + [pltpu.SemaphoreType.DMA((3 * ns,)), pltpu.SemaphoreType.DMA((3 * ns,)), pltpu.SemaphoreType.DMA((na + 1,))],
        name="x_grad_exchange", compiler_params=_cp("arbitrary"))(dproj, w_in, x, gx0, g_pre, *parts, small)


def _sibling_join_list(halves):
    na = len(halves)
    segs = _segments([(h.shape[0], h.shape[1], h.dtype) for h in halves])
    def body(*refs):
        ins, outs, stages = refs[:na], refs[na:2 * na], refs[2 * na:3 * na]
        send_sems, recv_sems, local_sems = refs[3 * na:]
        x, y, c, _ = _place()
        copies = [_rcopy(i, ins[w].at[pl.ds(r0, n), :], outs[w].at[pl.ds(c * half + r0, n), :], send_sems, recv_sems,
                         (x, y, 1 - c)) for i, (w, half, r0, n) in enumerate(segs)]
        for cp in copies:
            cp.start()
        own = []
        for w in range(na):
            cin = pltpu.make_async_copy(ins[w], stages[w], local_sems.at[w])
            cin.start()
            cin.wait()
            half = halves[w].shape[0]
            own.append(pltpu.make_async_copy(stages[w], outs[w].at[pl.ds(c * half, half), :], local_sems.at[w]))
            own[-1].start()
        for cp in copies:
            cp.wait_recv()
        for cp in copies:
            cp.wait_send()
        for cp in own:
            cp.wait()

    return pl.pallas_call(
        body, in_specs=[_ANY] * na, out_specs=[_ANY] * na,
        out_shape=[jax.ShapeDtypeStruct((2 * h.shape[0], h.shape[1]), f32) for h in halves],
        scratch_shapes=[pltpu.VMEM(h.shape, f32) for h in halves]
        + [pltpu.SemaphoreType.DMA((len(segs),)), pltpu.SemaphoreType.DMA((len(segs),)), pltpu.SemaphoreType.DMA((na,))],
        name="sibling_join")(*halves)


def _allgather_rows(v):
    def body(v_ref, o_ref, send_sems, recv_sems):
        x, y, c, _ = _place()
        me = 4 * x + 2 * y + c
        o_ref[me] = v_ref[...]
        copies = []
        i = 0
        for dx in range(2):
            for dy in range(2):
                for dc in range(2):
                    if dx + dy + dc:
                        copies.append(_rcopy(i, v_ref, o_ref.at[me], send_sems, recv_sems, (x ^ dx, y ^ dy, c ^ dc)))
                        i += 1
        for cp in copies:
            cp.start()
        for cp in copies:
            cp.wait_recv()
        for cp in copies:
            cp.wait_send()

    vm = pl.BlockSpec(memory_space=pltpu.VMEM)
    return pl.pallas_call(
        body, in_specs=[vm], out_specs=vm, out_shape=jax.ShapeDtypeStruct((8, 8, 128), f32),
        scratch_shapes=[pltpu.SemaphoreType.DMA((7,)), pltpu.SemaphoreType.DMA((7,))],
        name="allgather_rows")(v)


def _adamw_rows(parts, w, m, v):
    def body(p_ref, w_ref, m_ref, v_ref, g_ref, d_ref, m2_ref, v2_ref):
        g = p_ref[0]
        for dvc in range(1, 8):
            g = g + p_ref[dvc]
        g_ref[...] = g
        d, m2, v2 = _adamw_math(w_ref[...], g, m_ref[...], v_ref[...])
        d_ref[...] = d
        m2_ref[...] = m2
        v2_ref[...] = v2
    return pl.pallas_call(body, out_shape=[jax.ShapeDtypeStruct((8, 128), f32)] * 4, name="adamw_pre_norm_gain")(
        parts, w, m, v)


def _pair_exchange_list(grads, small):
    na = len(grads)
    segs = _segments([(g.shape[1] // 2, g.shape[2], g.dtype) for g in grads])
    n = NCHIP * len(segs) + 1
    def body(*refs):
        ins, s_ref, outs, rs_ref, (send_sems, recv_sems) = (refs[:na], refs[na], refs[na + 1:2 * na + 1],
                                                            refs[2 * na + 1], refs[2 * na + 2:])
        x, y, c, _ = _place()
        pieces = [(s_ref, rs_ref)]
        for j in range(NCHIP):
            for w, half, r0, rows in segs:
                pieces.append((ins[w].at[j, pl.ds((1 - c) * half + r0, rows), :], outs[w].at[j, pl.ds(r0, rows), :]))
        copies = [_rcopy(i, s, d, send_sems, recv_sems, (x, y, 1 - c)) for i, (s, d) in enumerate(pieces)]
        for cp in copies:
            cp.start()
        for cp in copies:
            cp.wait_recv()
        for cp in copies:
            cp.wait_send()

    return pl.pallas_call(
        body, in_specs=[_ANY] * (na + 1), out_specs=[_ANY] * (na + 1),
        out_shape=[jax.ShapeDtypeStruct((NCHIP, g.shape[1] // 2, g.shape[2]), f32) for g in grads]
        + [jax.ShapeDtypeStruct((SMALL_ROWS, 128), f32)],
        scratch_shapes=[pltpu.SemaphoreType.DMA((n,)), pltpu.SemaphoreType.DMA((n,))],
        name="pair_exchange")(*grads, small)


def _pair_sum_list(c_arr, grads, recvs, small, rsmall):
    na = len(grads)
    def body(c_ref, *refs):
        g_refs, r_refs, s_ref, rs_ref = refs[:na], refs[na:2 * na], refs[2 * na], refs[2 * na + 1]
        o_refs, os_ref = refs[2 * na + 2:3 * na + 2], refs[3 * na + 2]
        for g_ref, r_ref, o_ref in zip(g_refs, r_refs, o_refs):
            o_ref[...] = (g_ref[...] + r_ref[...]).astype(bf16)
        os_ref[...] = s_ref[...] + rs_ref[...]
    half = lambda g: pl.BlockSpec((1, g.shape[1] // 2, g.shape[2]), lambda j, c: (j, c[0], 0))
    low = lambda g: pl.BlockSpec((1, g.shape[1] // 2, g.shape[2]), lambda j, c: (j, 0, 0))
    sm = pl.BlockSpec((SMALL_ROWS, 128), lambda j, c: (0, 0))
    grid_spec = pltpu.PrefetchScalarGridSpec(
        num_scalar_prefetch=1, grid=(NCHIP,),
        in_specs=[half(g) for g in grads] + [low(g) for g in grads] + [sm, sm],
        out_specs=[low(g) for g in grads] + [sm])
    return pl.pallas_call(
        body, grid_spec=grid_spec,
        out_shape=[jax.ShapeDtypeStruct((NCHIP, g.shape[1] // 2, g.shape[2]), bf16) for g in grads]
        + [jax.ShapeDtypeStruct((SMALL_ROWS, 128), f32)],
        name="pair_sum", compiler_params=_cp("arbitrary"))(c_arr, *grads, *recvs, small, rsmall)


def _chip_exchange_list(parts, small):
    na = len(parts)
    segs = _segments([(p.shape[1], p.shape[2], p.dtype) for p in parts])
    ns = len(segs) + 1
    def body(*refs):
        ins, s_ref, outs, qs_ref, (send_sems, recv_sems) = (refs[:na], refs[na], refs[na + 1:2 * na + 1],
                                                            refs[2 * na + 1], refs[2 * na + 2:])
        x, y, c, chips = _place()
        k = 2 * x + y
        copies = []
        for j, chip in enumerate(chips):
            cj = 2 * chip[0] + chip[1]
            pieces = [(s_ref, qs_ref.at[k])]
            pieces += [(ins[w].at[cj, pl.ds(r0, n), :], outs[w].at[k, pl.ds(r0, n), :]) for w, _, r0, n in segs]
            copies += [_rcopy(ns * j + s, src, d, send_sems, recv_sems, (*chip, c)) for s, (src, d) in enumerate(pieces)]
        for cp in copies:
            cp.start()
        for cp in copies:
            cp.wait_recv()
        for cp in copies:
            cp.wait_send()

    return pl.pallas_call(
        body, in_specs=[_ANY] * (na + 1), out_specs=[_ANY] * (na + 1),
        out_shape=[jax.ShapeDtypeStruct(p.shape, bf16) for p in parts]
        + [jax.ShapeDtypeStruct((NCHIP, SMALL_ROWS, 128), f32)],
        scratch_shapes=[pltpu.SemaphoreType.DMA((3 * ns,)), pltpu.SemaphoreType.DMA((3 * ns,))],
        name="chip_exchange")(*parts, small)


def _chip_sum_list(parts, small):
    na = len(parts)
    nt = 2
    def body(*refs):
        for q_ref, f_ref in zip(refs[:na + 1], refs[na + 1:]):
            acc = q_ref[0].astype(f32)
            for j in range(1, NCHIP):
                acc = acc + q_ref[j].astype(f32)
            f_ref[...] = acc
    arrs = list(parts) + [small]
    return pl.pallas_call(
        body, grid=(nt,),
        in_specs=[pl.BlockSpec((NCHIP, a.shape[1] // nt, a.shape[2]), lambda i: (0, i, 0)) for a in arrs],
        out_specs=[pl.BlockSpec((a.shape[1] // nt, a.shape[2]), lambda i: (i, 0)) for a in arrs],
        out_shape=[jax.ShapeDtypeStruct(a.shape[1:], f32) for a in arrs],
        name="chip_sum", compiler_params=_cp("arbitrary"))(*arrs)


def _sibling_exchange_list(halves):
    na = len(halves)
    segs = _segments([(h.shape[0], h.shape[1], h.dtype) for h in halves])
    def body(*refs):
        ins, outs, (send_sems, recv_sems) = refs[:na], refs[na:2 * na], refs[2 * na:]
        x, y, c, _ = _place()
        copies = [_rcopy(i, ins[w].at[pl.ds(r0, n), :], outs[w].at[pl.ds(r0, n), :], send_sems, recv_sems, (x, y, 1 - c))
                  for i, (w, _, r0, n) in enumerate(segs)]
        for cp in copies:
            cp.start()
        for cp in copies:
            cp.wait_recv()
        for cp in copies:
            cp.wait_send()

    return pl.pallas_call(
        body, in_specs=[_ANY] * na, out_specs=[_ANY] * na,
        out_shape=[jax.ShapeDtypeStruct(h.shape, f32) for h in halves],
        scratch_shapes=[pltpu.SemaphoreType.DMA((len(segs),)), pltpu.SemaphoreType.DMA((len(segs),))],
        name="sibling_exchange")(*halves)


_REST_ROWS = (256, 256, 64, 128)
_CONV_PAD = 8192


def _pack_rest(mats, conv_rows, dtype):
    parts = [mats[0], mats[1], mats[2].reshape(64, 1024), mats[3].reshape(128, 1024)]
    parts = [p.astype(dtype) for p in parts] + [conv_rows]
    used = sum(p.shape[0] for p in parts)
    parts.append(jnp.zeros((REST_ROWS - used, 1024), dtype))
    return jnp.concatenate(parts, axis=0)


def _pack_rest_weights(mats, conv_w_s):
    flat = jnp.pad(conv_w_s.reshape(-1), (0, _CONV_PAD - KS * 256))
    return _pack_rest(mats, lax.bitcast_convert_type(flat, bf16).reshape(16, 1024), bf16)


def _pack_rest_grads(mats, conv_w_s):
    flat = jnp.pad(conv_w_s.reshape(-1), (0, _CONV_PAD - KS * 256))
    return _pack_rest(mats, flat.reshape(8, 1024), f32)


def _split_rest(p, conv_rows):
    o = 0
    out = []
    for rows in _REST_ROWS + (conv_rows,):
        out.append(p[..., o:o + rows, :])
        o += rows
    return out


_SMALL = (("conv_b", (1, 1024)), ("conv_ln_gain", (1, 1024)), ("conv_ln_bias", (1, 1024)),
          ("ssm_lambda_re", (1, 32, 64)), ("ssm_lambda_im", (1, 32, 64)), ("ssm_log_dt", (1, 32)),
          ("ssm_b_re", (1, 32, 64, 16)), ("ssm_b_im", (1, 32, 64, 16)), ("ssm_c_re", (1, 32, 16, 64)),
          ("ssm_c_im", (1, 32, 16, 64)), ("ssm_d", (1, 32, 16)), ("b_ssm_glu", (1, 512)), ("post_norm_gain", (1, 1024)))


def _pack_small(vals, extra=None):
    rows = []
    for v in list(vals) + ([extra] if extra is not None else []):
        flat = v.reshape(-1).astype(f32)
        n = -(-flat.shape[0] // 1024) * 1024
        rows.append(jnp.pad(flat, (0, n - flat.shape[0])).reshape(-1, 128))
    used = sum(r.shape[0] for r in rows)
    rows.append(jnp.zeros((SMALL_ROWS - used, 128), f32))
    return jnp.concatenate(rows, axis=0)


def _unpack_small(p):
    o = 0
    out = []
    for _, shape in _SMALL:
        n = int(np.prod(shape))
        nr = -(-n // 1024) * 8
        out.append(p[o:o + nr].reshape(-1)[:n].reshape(shape))
        o += nr
    return out, p[o, 0]


def _discretize(lam_re, lam_im, log_dt, b_re, b_im):
    dt = jnp.exp(log_dt)[:, None]
    mag = jnp.exp(lam_re * dt)
    ar = mag * jnp.cos(lam_im * dt)
    ai = mag * jnp.sin(lam_im * dt)
    den = lam_re * lam_re + lam_im * lam_im
    zr = ((ar - 1.0) * lam_re + ai * lam_im) / den
    zi = (ai * lam_re - (ar - 1.0) * lam_im) / den
    bbr = zr[..., None] * b_re - zi[..., None] * b_im
    bbi = zr[..., None] * b_im + zi[..., None] * b_re
    return ar, ai, bbr, bbi


_EYE8 = np.eye(8, dtype=np.float32)


def _bbt_blocks(bb):
    v = bb.reshape(4, 8, PST, H).transpose(0, 1, 3, 2)
    return jnp.einsum("bghp,gk->bghkp", v, _EYE8).reshape(4, 128, 512)


def _bbt_unblock(m):
    v = jnp.einsum("bghkp,gk->bghp", m.reshape(4, 8, H, 8, PST), _EYE8)
    return v.transpose(0, 1, 3, 2).reshape(G, PST, H)


def _ct_blocks(cc):
    v = cc.reshape(4, 8, H, PST)
    return jnp.einsum("bghp,gk->bgpkh", v, _EYE8).reshape(4, 512, 128)


def _ct_unblock(m):
    v = jnp.einsum("bgpkh,gk->bghp", m.reshape(4, 8, PST, 8, H), _EYE8)
    return v.reshape(G, H, PST)


def _perm_matrix():
    p = np.zeros((TC, TC), np.float32)
    for r in range(R):
        for seg in range(8):
            p[r * 8 + seg, seg * R + r] = 1.0
    return p


def _deinterleave(a):
    L, C = a.shape
    return a.reshape(L // TC, R, 8, C).transpose(0, 2, 1, 3).reshape(L, C)


def _fwd_bwd(h, xi, ti, w_in, conv_w, w_co, w_glu, w_so, w_out, small):
    (conv_b, ln_g, ln_b, lam_re, lam_im, log_dt, b_re, b_im, c_re, c_im, dvec, b_glu, g_post) = small
    lam_re, lam_im, log_dt = lam_re[0], lam_im[0], log_dt[0]
    b_re, b_im, c_re, c_im = b_re[0], b_im[0], c_re[0], c_im[0]
    (ar, ai, bbr, bbi), disc_vjp = jax.vjp(_discretize, lam_re, lam_im, log_dt, b_re, b_im)
    a_re = ar.reshape(1, NS)
    a_im = ai.reshape(1, NS)
    dt = jnp.exp(log_dt)[:, None]
    steps = jnp.arange(1, R + 1, dtype=f32)[:, None, None]
    apow_re = (jnp.exp(steps * (lam_re * dt)) * jnp.cos(steps * (lam_im * dt))).reshape(R, NS)
    apow_im = (jnp.exp(steps * (lam_re * dt)) * jnp.sin(steps * (lam_im * dt))).reshape(R, NS)
    bbt_re, bbt_im = _bbt_blocks(bbr).astype(bf16), _bbt_blocks(bbi).astype(bf16)
    ct_re, ct_im = _ct_blocks(c_re).astype(bf16), _ct_blocks(c_im).astype(bf16)
    d_row = dvec.reshape(1, SW)
    cw32 = jnp.pad(conv_w, ((0, 1), (0, 0)))

    proj = _proj_fwd(h, w_in)
    cu1, a_in = _conv_fwd(proj, cw32, conv_b, ln_g, ln_b)
    y0, b_in, sre, sim, cinr, cini = _ssm_fwd(proj, bbt_re, bbt_im, ct_re, ct_im, a_re, a_im,
                                              apow_re, apow_im, d_row, w_glu, b_glu)
    gx0, d_ain, d_bin, dproj, dw_out, dw_co, dw_so, dg_post, loss = _tail(
        a_in, b_in, proj, xi, ti, w_co, w_so, w_out, g_post)
    (dproj, dbbt_re, dbbt_im, dct_re, dct_im, dd, dar8, dai8, dw_glu, db_glu) = _ssm_bwd(
        d_bin, y0, proj, sre, sim, cinr, cini, bbt_re, bbt_im, ct_re, ct_im,
        a_re, a_im, apow_re, apow_im, d_row, w_glu, b_glu, dproj)
    dproj, dcw8, d_convb, d_lng, d_lnb = _conv_bwd(d_ain, cu1, proj, cw32, ln_g, ln_b, dproj)
    dw_in = _win_grad(h, dproj)

    d_ar = jnp.sum(dar8, axis=0).reshape(G, PST)
    d_ai = jnp.sum(dai8, axis=0).reshape(G, PST)
    d_lre, d_lim, d_ldt, d_bre, d_bim = disc_vjp((d_ar, d_ai, _bbt_unblock(dbbt_re), _bbt_unblock(dbbt_im)))
    d_conv_w = jnp.sum(dcw8, axis=1)[:KS]
    small_grads = [d_convb, d_lng, d_lnb, d_lre[None], d_lim[None], d_ldt[None], d_bre[None], d_bim[None],
                   _ct_unblock(dct_re)[None], _ct_unblock(dct_im)[None], dd.reshape(1, G, H), db_glu, dg_post]
    return loss[0, 0], gx0, dproj, (dw_in, dw_co, dw_out, dw_glu, dw_so, d_conv_w), small_grads


def kernel(x, pre_norm_gain, w_in, conv_w, conv_b, conv_ln_gain, conv_ln_bias, w_conv_out, ssm_lambda_re, ssm_lambda_im, ssm_log_dt, ssm_b_re, ssm_b_im, ssm_c_re, ssm_c_im, ssm_d, w_ssm_glu, b_ssm_glu, w_ssm_out, w_out, post_norm_gain, loss_target, m_pre_norm_gain, m_w_in, m_conv_w, m_conv_b, m_conv_ln_gain, m_conv_ln_bias, m_w_conv_out, m_ssm_lambda_re, m_ssm_lambda_im, m_ssm_log_dt, m_ssm_b_re, m_ssm_b_im, m_ssm_c_re, m_ssm_c_im, m_ssm_d, m_w_ssm_glu, m_b_ssm_glu, m_w_ssm_out, m_w_out, m_post_norm_gain, v_pre_norm_gain, v_w_in, v_conv_w, v_conv_b, v_conv_ln_gain, v_conv_ln_bias, v_w_conv_out, v_ssm_lambda_re, v_ssm_lambda_im, v_ssm_log_dt, v_ssm_b_re, v_ssm_b_im, v_ssm_c_re, v_ssm_c_im, v_ssm_d, v_w_ssm_glu, v_b_ssm_glu, v_w_ssm_out, v_w_out, v_post_norm_gain):
    c = lax.axis_index("c")
    shards = [w_in[0].astype(bf16), w_conv_out[0].astype(bf16), w_out[0].astype(bf16), w_ssm_glu[0].astype(bf16),
              w_ssm_out[0].astype(bf16), jnp.pad(conv_w[0], ((0, CONV_ROWS - KS), (0, 0)))]
    w_in_g, w_co_g, w_out_g, w_glu_g, w_so_g, conv_w_g, h, xi, ti = _gather_prep(
        shards, x[0], loss_target[0], pre_norm_gain, jnp.asarray(_perm_matrix(), bf16))
    conv_w_f = conv_w_g[:, :KS].transpose(1, 0, 2).reshape(KS, CW)

    small = (conv_b, conv_ln_gain, conv_ln_bias, ssm_lambda_re, ssm_lambda_im, ssm_log_dt, ssm_b_re,
             ssm_b_im, ssm_c_re, ssm_c_im, ssm_d, b_ssm_glu, post_norm_gain)
    loss_part, gx0, dproj, big_grads, small_grads = _fwd_bwd(
        h, xi, ti, w_in_g, conv_w_f, w_co_g.reshape(CW, D), w_glu_g.reshape(SW, SW), w_so_g, w_out_g.reshape(D, D), small)

    dw_in, dw_co, dw_out, dw_glu, dw_so, d_conv_w = big_grads
    d_conv_w = jnp.pad(d_conv_w, ((0, CONV_ROWS - KS), (0, 0))).reshape(CONV_ROWS, NCHIP, 256).transpose(1, 0, 2)
    grads = [dw_in, dw_co.reshape(NCHIP, 256, D), dw_out.reshape(NCHIP, 256, D), dw_glu.reshape(NCHIP, 128, SW),
             dw_so, d_conv_w]
    gs = _pack_small(small_grads, extra=loss_part)
    *recvs, rs = _pair_exchange_list(grads, gs)
    *parts, ps = _pair_sum_list(c.astype(jnp.int32).reshape(1), grads, recvs, gs, rs)
    gxi, dg_pre, *qparts, qs = _x_grad_exchange(dproj, w_in_g, xi, gx0, pre_norm_gain, parts, ps)
    grad_x = _deinterleave(gxi)
    *halves, fs = _chip_sum_list(qparts, qs)
    g_big = list(_sibling_join_list(halves))
    g_big[5] = g_big[5][:KS]

    big_w = (w_in[0], w_conv_out[0], w_out[0], w_ssm_glu[0], w_ssm_out[0], conv_w[0])
    big_m = (m_w_in[0], m_w_conv_out[0], m_w_out[0], m_w_ssm_glu[0], m_w_ssm_out[0], m_conv_w[0])
    big_v = (v_w_in[0], v_w_conv_out[0], v_w_out[0], v_w_ssm_glu[0], v_w_ssm_out[0], v_conv_w[0])
    big_names = ("w_in", "w_conv_out", "w_out", "w_ssm_glu", "w_ssm_out", "conv_w")
    res = {}
    for n, w, g, m, v in zip(big_names, big_w, g_big, big_m, big_v):
        d, m2, v2 = _adamw("adamw_" + n, w, g, m, v)
        res[n] = (g[None], d[None], m2[None], v2[None])

    small_m = (m_conv_b, m_conv_ln_gain, m_conv_ln_bias, m_ssm_lambda_re, m_ssm_lambda_im, m_ssm_log_dt,
               m_ssm_b_re, m_ssm_b_im, m_ssm_c_re, m_ssm_c_im, m_ssm_d, m_b_ssm_glu, m_post_norm_gain)
    small_v = (v_conv_b, v_conv_ln_gain, v_conv_ln_bias, v_ssm_lambda_re, v_ssm_lambda_im, v_ssm_log_dt,
               v_ssm_b_re, v_ssm_b_im, v_ssm_c_re, v_ssm_c_im, v_ssm_d, v_b_ssm_glu, v_post_norm_gain)
    sd, sm, sv = _adamw("adamw_small", _pack_small(small), fs, _pack_small(small_m), _pack_small(small_v))
    sg_l, loss = _unpack_small(fs)
    sd_l, _ = _unpack_small(sd)
    sm_l, _ = _unpack_small(sm)
    sv_l, _ = _unpack_small(sv)
    for i, (n, _) in enumerate(_SMALL):
        res[n] = (sg_l[i], sd_l[i], sm_l[i], sv_l[i])
    rows = lambda a: a.reshape(8, 128)
    pre = _adamw_rows(_allgather_rows(rows(dg_pre)), rows(pre_norm_gain), rows(m_pre_norm_gain), rows(v_pre_norm_gain))
    res["pre_norm_gain"] = tuple(a.reshape(1, D) for a in pre)

    order = ("pre_norm_gain", "w_in", "conv_w", "conv_b", "conv_ln_gain", "conv_ln_bias", "w_conv_out", "ssm_lambda_re",
             "ssm_lambda_im", "ssm_log_dt", "ssm_b_re", "ssm_b_im", "ssm_c_re", "ssm_c_im", "ssm_d", "w_ssm_glu",
             "b_ssm_glu", "w_ssm_out", "w_out", "post_norm_gain")
    outs = [loss, grad_x[None]]
    for q in range(4):
        outs.extend(res[n][q] for n in order)
    return tuple(outs)
```

```python
import math

import numpy as np
import jax
import jax.numpy as jnp
from jax import lax
from jax.experimental import pallas as pl
from jax.experimental.pallas import tpu as pltpu

f32 = jnp.float32
bf16 = jnp.bfloat16

D = 1024
CW = 1024
SW = 512
G = 32
H = 16
PST = 64
NS = G * PST
KS = 31
IN_W = 6144
NCHIP = 4
SHARD_W = IN_W // NCHIP
RMS_EPS = 1e-6
LN_EPS = 1e-5
LR, B1, B2, EPS, WD, STEP = 0.001, 0.9, 0.999, 1e-08, 0.01, 10
GELU_K0 = math.sqrt(2.0 / math.pi)
GELU_K1 = 0.044715

TC = 512
R = TC // 8
NH = 32
LBW = 1024
REST_ROWS = 768
CONV_ROWS = 64
SMALL_ROWS = 1152
VMEM_LIMIT = 56 * 1024 * 1024
MESH = pl.DeviceIdType.MESH


def _cp(*sem):
    return pltpu.CompilerParams(dimension_semantics=tuple(sem), vmem_limit_bytes=VMEM_LIMIT)


def _sig(v):
    return 0.5 * jnp.tanh(0.5 * v) + 0.5


def _dot(a, b):
    return jnp.dot(a, b, preferred_element_type=f32)


def _dot_nt(a, b):
    return lax.dot_general(a, b, (((1,), (1,)), ((), ())), preferred_element_type=f32)


def _dot_tn(a, b):
    return lax.dot_general(a, b, (((0,), (0,)), ((), ())), preferred_element_type=f32)


def _full(shape):
    nd = len(shape)
    return pl.BlockSpec(shape, lambda *_: (0,) * nd)


def _rows8(i):
    return pl.ds(pl.multiple_of(i * 8, 8), 8)


def _prenorm(x, g_pre):
    L = x.shape[0]
    tm = 512
    def body(x_ref, g_ref, h_ref):
        xt = x_ref[...]
        r = lax.rsqrt(jnp.mean(xt * xt, axis=-1, keepdims=True) + RMS_EPS)
        h_ref[...] = (xt * r * g_ref[...]).astype(bf16)
    return pl.pallas_call(
        body, grid=(L // tm,),
        in_specs=[pl.BlockSpec((tm, D), lambda i: (i, 0)), _full((1, D))],
        out_specs=pl.BlockSpec((tm, D), lambda i: (i, 0)),
        out_shape=jax.ShapeDtypeStruct((L, D), bf16),
        name="prenorm", compiler_params=_cp("arbitrary"))(x, g_pre)


def _proj_fwd(k_arr, h, w_in, proj):
    L = h.shape[0]
    tm = min(1024, L)
    def body(_, h_ref, w_ref, __, o_ref):
        o_ref[...] = _dot(h_ref[...], w_ref[0]).astype(bf16)
    shard = lambda j, k: (k[0] + 1 + j) % NCHIP
    grid_spec = pltpu.PrefetchScalarGridSpec(
        num_scalar_prefetch=1, grid=(NCHIP - 1, L // tm),
        in_specs=[pl.BlockSpec((tm, D), lambda j, i, k: (i, 0)),
                  pl.BlockSpec((1, D, SHARD_W), lambda j, i, k: (shard(j, k), 0, 0)), _ANY],
        out_specs=pl.BlockSpec((tm, SHARD_W), lambda j, i, k: (i, shard(j, k))))
    return pl.pallas_call(
        body, grid_spec=grid_spec, out_shape=jax.ShapeDtypeStruct((L, IN_W), bf16),
        input_output_aliases={3: 0},
        name="proj_fwd", compiler_params=_cp("arbitrary", "arbitrary"))(k_arr, h, w_in, proj)


NLB = CW // 128
RPI = 4


def _put_blocked(buf, row0, nrows, v):
    for lb in range(NLB):
        buf[lb, pl.ds(row0, nrows), :] = v[:, lb * 128:(lb + 1) * 128]


def _get_blocked(buf, row0, nrows):
    return jnp.concatenate([buf[lb, pl.ds(row0, nrows), :] for lb in range(NLB)], axis=1)


def _fill_before(ebuf, prev):
    sub = lax.broadcasted_iota(jnp.int32, (8, 128), 0)
    def halo(p, carry):
        for lb in range(NLB):
            cur = ebuf[lb, _rows8(R + p), :]
            ebuf[lb, _rows8(p), :] = jnp.where(sub == 0, pltpu.roll(prev[lb, _rows8(p), :], 1, 0),
                                               pltpu.roll(cur, 1, 0))
        return carry
    lax.fori_loop(0, NH, halo, 0)


def _fir(buf, lb, r, coef, first, flip):
    win = buf[lb, pl.ds(pl.multiple_of(r * 8, 8), (KS + RPI - 1) * 8), :]
    outs = []
    for i in range(RPI):
        acc = [first, None, None, None]
        for k in range(KS):
            o = i + ((KS - 1 - k) if flip else k)
            t = coef[k] * win[8 * o:8 * o + 8, :]
            acc[k % 4] = t if acc[k % 4] is None else acc[k % 4] + t
        outs.append((acc[0] + acc[1]) + (acc[2] + acc[3]))
    return outs


def _conv_fwd(proj, cw, cbias, lng, lnb):
    L = proj.shape[0]
    nc = L // TC
    def body(ca_ref, cb_ref, zc_ref, w_ref, b_ref, g_ref, bb_ref, cu1_ref, ain_ref, ebuf, prev, cacc):
        @pl.when(pl.program_id(0) == 0)
        def _():
            prev[...] = jnp.zeros_like(prev)
        def glu(s, carry):
            rows = pl.ds(pl.multiple_of(s * 64, 64), 64)
            _put_blocked(ebuf, pl.multiple_of(NH * 8 + s * 64, 64), 64,
                         ca_ref[rows, :].astype(f32) * _sig(cb_ref[rows, :].astype(f32)))
            return carry
        lax.fori_loop(0, TC // 64, glu, 0)
        _fill_before(ebuf, prev)
        prev[...] = ebuf[:, R * 8:(NH + R) * 8, :]
        for lb in range(NLB):
            sl = slice(lb * 128, (lb + 1) * 128)
            wk = [jnp.broadcast_to(w_ref[k:k + 1, sl], (8, 128)) for k in range(KS)]
            bias = jnp.broadcast_to(b_ref[:, sl], (8, 128))
            def tap(q, carry, lb=lb, wk=wk, bias=bias):
                r = q * RPI
                for i, o in enumerate(_fir(ebuf, lb, r + (NH - KS + 1), wk, bias, False)):
                    cacc[lb, _rows8(r + i), :] = o
                return carry
            lax.fori_loop(0, R // RPI, tap, 0)
        def norm(s, carry):
            rows = pl.ds(pl.multiple_of(s * 64, 64), 64)
            c1b = _get_blocked(cacc, pl.multiple_of(s * 64, 64), 64).astype(bf16)
            cu1_ref[rows, :] = c1b
            c1 = c1b.astype(f32)
            xc = c1 - jnp.mean(c1, axis=-1, keepdims=True)
            var = jnp.mean(xc * xc, axis=-1, keepdims=True)
            ln = xc * lax.rsqrt(var + LN_EPS) * g_ref[...] + bb_ref[...]
            zc = zc_ref[rows, :].astype(f32)
            ain_ref[rows, :] = ((ln * _sig(ln)) * (zc * _sig(zc))).astype(bf16)
            return carry
        lax.fori_loop(0, TC // 64, norm, 0)

    col = lambda c: pl.BlockSpec((TC, CW), lambda i, c=c: (i, c))
    return pl.pallas_call(
        body, grid=(nc,),
        in_specs=[col(0), col(1), col(2), _full((32, CW)), _full((1, CW)), _full((1, CW)), _full((1, CW))],
        out_specs=[pl.BlockSpec((TC, CW), lambda i: (i, 0)), pl.BlockSpec((TC, CW), lambda i: (i, 0))],
        out_shape=[jax.ShapeDtypeStruct((L, CW), bf16), jax.ShapeDtypeStruct((L, CW), bf16)],
        scratch_shapes=[pltpu.VMEM((NLB, (NH + R) * 8, 128), f32), pltpu.VMEM((NLB, NH * 8, 128), f32),
                        pltpu.VMEM((NLB, TC, 128), f32)],
        name="conv_fwd", compiler_params=_cp("arbitrary"))(proj, proj, proj, cw, cbias, lng, lnb)


def _gelu_parts(y0):
    t = jnp.tanh(GELU_K0 * (y0 + GELU_K1 * y0 * y0 * y0))
    return t, 0.5 * y0 * (1.0 + t)


def _ssm_fwd(proj, bbt_re, bbt_im, ct_re, ct_im, a_re, a_im, apow_re, apow_im, dvec, wglu, bglu):
    L = proj.shape[0]
    nc = L // TC
    def body(u_ref, zs_ref, bre_ref, bim_ref, cre_ref, cim_ref, are_ref, aim_ref, pwr_ref, pwi_ref,
             d_ref, wg_ref, bg_ref, y0_ref, bin_ref, sre, sim, cinr, cini, prev_re, prev_im):
        c = pl.program_id(0)
        @pl.when(c == 0)
        def _():
            prev_re[...] = jnp.zeros_like(prev_re)
            prev_im[...] = jnp.zeros_like(prev_im)
        u = u_ref[...]
        for blk in range(4):
            ub = u[:, 128 * blk:128 * (blk + 1)]
            sre[:, 512 * blk:512 * (blk + 1)] = _dot(ub, bre_ref[blk])
            sim[:, 512 * blk:512 * (blk + 1)] = _dot(ub, bim_ref[blk])
        for lb in range(NS // LBW):
            sl = slice(lb * LBW, (lb + 1) * LBW)
            ar = jnp.broadcast_to(are_ref[:, sl], (8, LBW))
            ai = jnp.broadcast_to(aim_ref[:, sl], (8, LBW))
            def step(r, carry, sl=sl, ar=ar, ai=ai):
                sr, si = carry
                nr = ar * sr - ai * si + sre[_rows8(r), sl]
                ni = ar * si + ai * sr + sim[_rows8(r), sl]
                sre[_rows8(r), sl] = nr
                sim[_rows8(r), sl] = ni
                return nr, ni
            lax.fori_loop(1, R, step, (sre[0:8, sl], sim[0:8, sl]))
        a_r = pwr_ref[R - 1:R, :]
        a_i = pwi_ref[R - 1:R, :]
        cr = prev_re[0:1, :]
        ci = prev_im[0:1, :]
        for seg in range(8):
            cinr[seg:seg + 1, :] = cr
            cini[seg:seg + 1, :] = ci
            er = sre[8 * (R - 1) + seg:8 * (R - 1) + seg + 1, :]
            ei = sim[8 * (R - 1) + seg:8 * (R - 1) + seg + 1, :]
            cr, ci = er + a_r * cr - a_i * ci, ei + a_r * ci + a_i * cr
        prev_re[0:1, :] = cr
        prev_im[0:1, :] = ci
        for lb in range(NS // LBW):
            sl = slice(lb * LBW, (lb + 1) * LBW)
            kr = cinr[:, sl]
            ki = cini[:, sl]
            def fix(r, carry, sl=sl, kr=kr, ki=ki):
                pr = jnp.broadcast_to(pwr_ref[pl.ds(r, 1), sl], (8, LBW))
                pi = jnp.broadcast_to(pwi_ref[pl.ds(r, 1), sl], (8, LBW))
                sre[_rows8(r), sl] = sre[_rows8(r), sl] + pr * kr - pi * ki
                sim[_rows8(r), sl] = sim[_rows8(r), sl] + pr * ki + pi * kr
                return carry
            lax.fori_loop(0, R, fix, 0)
        yp = []
        for blk in range(4):
            sr = sre[:, 512 * blk:512 * (blk + 1)].astype(bf16)
            si = sim[:, 512 * blk:512 * (blk + 1)].astype(bf16)
            yp.append(_dot(sr, cre_ref[blk]) - _dot(si, cim_ref[blk]))
        y0 = jnp.concatenate(yp, axis=1) + d_ref[...] * u.astype(f32)
        y0_ref[...] = y0
        _, y1 = _gelu_parts(y0)
        glu = _dot(y1.astype(bf16), wg_ref[...]) + bg_ref[...]
        y2 = y1 * _sig(glu)
        zs = zs_ref[...].astype(f32)
        bin_ref[...] = (y2 * (zs * _sig(zs))).astype(bf16)

    return pl.pallas_call(
        body, grid=(nc,),
        in_specs=[pl.BlockSpec((TC, SW), lambda c: (c, 6)), pl.BlockSpec((TC, SW), lambda c: (c, 7)),
                  _full((4, 128, 512)), _full((4, 128, 512)), _full((4, 512, 128)), _full((4, 512, 128)),
                  _full((1, NS)), _full((1, NS)), _full((R, NS)), _full((R, NS)),
                  _full((1, SW)), _full((SW, SW)), _full((1, SW))],
        out_specs=[pl.BlockSpec((TC, SW), lambda c: (c, 0)), pl.BlockSpec((TC, SW), lambda c: (c, 0)),
                   pl.BlockSpec((TC, NS), lambda c: (c, 0)), pl.BlockSpec((TC, NS), lambda c: (c, 0)),
                   pl.BlockSpec((8, NS), lambda c: (c, 0)), pl.BlockSpec((8, NS), lambda c: (c, 0))],
        out_shape=[jax.ShapeDtypeStruct((L, SW), f32), jax.ShapeDtypeStruct((L, SW), bf16),
                   jax.ShapeDtypeStruct((L, NS), f32), jax.ShapeDtypeStruct((L, NS), f32),
                   jax.ShapeDtypeStruct((nc * 8, NS), f32), jax.ShapeDtypeStruct((nc * 8, NS), f32)],
        scratch_shapes=[pltpu.VMEM((8, NS), f32), pltpu.VMEM((8, NS), f32)],
        name="ssm_fwd", compiler_params=_cp("arbitrary"))(
            proj, proj, bbt_re, bbt_im, ct_re, ct_im, a_re, a_im, apow_re, apow_im, dvec, wglu, bglu)


def _tail(a_in, b_in, proj, x, tgt, wco, wso, wout, gpost):
    L = x.shape[0]
    tm = 256
    def body(a_ref, b_ref, gc_ref, gs_ref, x_ref, t_ref, wco_ref, wso_ref, wout_ref, gp_ref,
             gx_ref, dain_ref, dbin_ref, dp_ref, dwout_ref, dwco_ref, dwso_ref, dgp_ref, loss_ref):
        @pl.when(pl.program_id(0) == 0)
        def _():
            dwout_ref[...] = jnp.zeros_like(dwout_ref)
            dwco_ref[...] = jnp.zeros_like(dwco_ref)
            dwso_ref[...] = jnp.zeros_like(dwso_ref)
            dgp_ref[...] = jnp.zeros_like(dgp_ref)
            loss_ref[...] = jnp.zeros_like(loss_ref)
        a = a_ref[...]
        b = b_ref[...]
        co = _dot(a, wco_ref[...])
        so = jnp.concatenate([_dot(b, wso_ref[j]) for j in range(NCHIP)], axis=1)
        sc = _sig(gc_ref[...].astype(f32))
        ss = _sig(gs_ref[...].astype(f32))
        mb = (sc * co + ss * so).astype(bf16)
        out = _dot(mb, wout_ref[...])
        r2 = lax.rsqrt(jnp.mean(out * out, axis=-1, keepdims=True) + RMS_EPS)
        on = out * r2
        gp = gp_ref[...]
        e = x_ref[...] + on * gp - t_ref[...]
        loss_ref[...] += (0.5 / D) * jnp.sum(e * e)
        dy = e * (1.0 / D)
        gx_ref[...] = dy
        dgp_ref[...] += jnp.sum(dy * on, axis=0, keepdims=True)
        dn = dy * gp
        dout = (r2 * (dn - on * jnp.mean(dn * on, axis=-1, keepdims=True))).astype(bf16)
        dwout_ref[...] += _dot_tn(mb, dout)
        dm = _dot_nt(dout, wout_ref[...])
        dp_ref[:, 0:D] = (dm * co * sc * (1.0 - sc)).astype(bf16)
        dp_ref[:, D:2 * D] = (dm * so * ss * (1.0 - ss)).astype(bf16)
        dco = (dm * sc).astype(bf16)
        dso = (dm * ss).astype(bf16)
        dwco_ref[...] += _dot_tn(a, dco)
        dbin = None
        for j in range(NCHIP):
            dso_j = dso[:, j * 256:(j + 1) * 256]
            dwso_ref[j] += _dot_tn(b, dso_j)
            t = _dot_nt(dso_j, wso_ref[j])
            dbin = t if dbin is None else dbin + t
        dain_ref[...] = _dot_nt(dco, wco_ref[...]).astype(bf16)
        dbin_ref[...] = dbin.astype(bf16)

    row = lambda w: pl.BlockSpec((tm, w), lambda i: (i, 0))
    one = lambda shape: pl.BlockSpec(shape, lambda i: (0,) * len(shape), pipeline_mode=pl.Buffered(1))
    return pl.pallas_call(
        body, grid=(L // tm,),
        in_specs=[row(CW), row(SW), pl.BlockSpec((tm, D), lambda i: (i, 4)), pl.BlockSpec((tm, D), lambda i: (i, 5)),
                  row(D), row(D), one((CW, D)), one((NCHIP, SW, 256)), one((D, D)), one((1, D))],
        out_specs=[row(D), row(CW), row(SW), pl.BlockSpec((tm, 2 * D), lambda i: (i, 2)),
                   one((D, D)), one((CW, D)), one((NCHIP, SW, 256)), one((1, D)), one((1, 128))],
        out_shape=[jax.ShapeDtypeStruct((L, D), f32), jax.ShapeDtypeStruct((L, CW), bf16),
                   jax.ShapeDtypeStruct((L, SW), bf16), jax.ShapeDtypeStruct((L, IN_W), bf16),
                   jax.ShapeDtypeStruct((D, D), f32), jax.ShapeDtypeStruct((CW, D), f32),
                   jax.ShapeDtypeStruct((NCHIP, SW, 256), f32), jax.ShapeDtypeStruct((1, D), f32),
                   jax.ShapeDtypeStruct((1, 128), f32)],
        name="tail", compiler_params=_cp("arbitrary"))(a_in, b_in, proj, proj, x, tgt, wco, wso, wout, gpost)


def _ssm_bwd(d_bin, y0, proj, sre, sim, cinr, cini, bbt_re, bbt_im, ct_re, ct_im,
             a_re, a_im, apow_re, apow_im, dvec, wglu, bglu, dproj):
    L = y0.shape[0]
    nc = L // TC
    def body(dbin_ref, y0_ref, u_ref, zs_ref, sre_ref, sim_ref, cinr_ref, cini_ref,
             bre_ref, bim_ref, cre_ref, cim_ref, are_ref, aim_ref, pwr_ref, pwi_ref, d_ref, wg_ref, bg_ref, _,
             dp_ref, dbre_ref, dbim_ref, dcre_ref, dcim_ref, dd_ref, dar_ref, dai_ref, dwg_ref, dbg_ref,
             gre, gim, gcr, gci, nxt_re, nxt_im):
        @pl.when(pl.program_id(0) == 0)
        def _():
            for ref in (dbre_ref, dbim_ref, dcre_ref, dcim_ref, dd_ref, dar_ref, dai_ref, dwg_ref, dbg_ref,
                        nxt_re, nxt_im):
                ref[...] = jnp.zeros_like(ref)
        y0 = y0_ref[...]
        u = u_ref[...]
        zs = zs_ref[...].astype(f32)
        dbin = dbin_ref[...].astype(f32)
        t, y1 = _gelu_parts(y0)
        y1b = y1.astype(bf16)
        sg = _sig(_dot(y1b, wg_ref[...]) + bg_ref[...])
        sz = _sig(zs)
        d_y2 = dbin * (zs * sz)
        dp_ref[:, SW:2 * SW] = (dbin * (y1 * sg) * (sz * (1.0 + zs * (1.0 - sz)))).astype(bf16)
        d_glu = d_y2 * y1 * sg * (1.0 - sg)
        d_glub = d_glu.astype(bf16)
        d_y1 = d_y2 * sg + _dot_nt(d_glub, wg_ref[...])
        dwg_ref[...] += _dot_tn(y1b, d_glub)
        dbg_ref[...] += jnp.sum(d_glu, axis=0, keepdims=True)
        dgelu = 0.5 * (1.0 + t) + 0.5 * y0 * (1.0 - t * t) * GELU_K0 * (1.0 + 3.0 * GELU_K1 * y0 * y0)
        d_y0 = d_y1 * dgelu
        dd_ref[...] += jnp.sum(d_y0 * u.astype(f32), axis=0, keepdims=True)
        dyb = d_y0.astype(bf16)
        for blk in range(4):
            dy1 = dyb[:, 128 * blk:128 * (blk + 1)]
            gre[:, 512 * blk:512 * (blk + 1)] = _dot_nt(dy1, cre_ref[blk])
            gim[:, 512 * blk:512 * (blk + 1)] = -_dot_nt(dy1, cim_ref[blk])
        for lb in range(NS // LBW):
            sl = slice(lb * LBW, (lb + 1) * LBW)
            ar = jnp.broadcast_to(are_ref[:, sl], (8, LBW))
            ai = jnp.broadcast_to(aim_ref[:, sl], (8, LBW))
            def step(k, carry, sl=sl, ar=ar, ai=ai):
                gr, gi = carry
                row = _rows8(R - 2 - k)
                nr = ar * gr + ai * gi + gre[row, sl]
                ni = ar * gi - ai * gr + gim[row, sl]
                gre[row, sl] = nr
                gim[row, sl] = ni
                return nr, ni
            lax.fori_loop(0, R - 1, step, (gre[8 * (R - 1):8 * R, sl], gim[8 * (R - 1):8 * R, sl]))
        a_r = pwr_ref[R - 1:R, :]
        a_i = pwi_ref[R - 1:R, :]
        cr = nxt_re[0:1, :]
        ci = nxt_im[0:1, :]
        for seg in range(7, -1, -1):
            gcr[seg:seg + 1, :] = cr
            gci[seg:seg + 1, :] = ci
            er = gre[seg:seg + 1, :]
            ei = gim[seg:seg + 1, :]
            cr, ci = er + a_r * cr + a_i * ci, ei + a_r * ci - a_i * cr
        nxt_re[0:1, :] = cr
        nxt_im[0:1, :] = ci
        for lb in range(NS // LBW):
            sl = slice(lb * LBW, (lb + 1) * LBW)
            kr = gcr[:, sl]
            ki = gci[:, sl]
            def fix(r, carry, sl=sl, kr=kr, ki=ki):
                pr = jnp.broadcast_to(pwr_ref[pl.ds(R - 1 - r, 1), sl], (8, LBW))
                pi = jnp.broadcast_to(pwi_ref[pl.ds(R - 1 - r, 1), sl], (8, LBW))
                gre[_rows8(r), sl] = gre[_rows8(r), sl] + pr * kr + pi * ki
                gim[_rows8(r), sl] = gim[_rows8(r), sl] + pr * ki - pi * kr
                return carry
            lax.fori_loop(0, R, fix, 0)
        dup = []
        for blk in range(4):
            s4 = slice(512 * blk, 512 * (blk + 1))
            s1 = slice(128 * blk, 128 * (blk + 1))
            grb = gre[:, s4].astype(bf16)
            gib = gim[:, s4].astype(bf16)
            dup.append(_dot_nt(grb, bre_ref[blk]) + _dot_nt(gib, bim_ref[blk]))
            dbre_ref[blk] += _dot_tn(u[:, s1], grb)
            dbim_ref[blk] += _dot_tn(u[:, s1], gib)
            dcre_ref[blk] += _dot_tn(sre_ref[:, s4].astype(bf16), dyb[:, s1])
            dcim_ref[blk] -= _dot_tn(sim_ref[:, s4].astype(bf16), dyb[:, s1])
        dp_ref[:, 0:SW] = (jnp.concatenate(dup, axis=1) + d_ref[...] * d_y0).astype(bf16)
        for lb in range(NS // LBW):
            sl = slice(lb * LBW, (lb + 1) * LBW)
            g0r, g0i = gre[0:8, sl], gim[0:8, sl]
            p0r, p0i = cinr_ref[:, sl], cini_ref[:, sl]
            acc0 = (g0r * p0r + g0i * p0i, g0i * p0r - g0r * p0i)
            def dacc(r, carry, sl=sl):
                xr, xi = carry
                gr, gi = gre[_rows8(r), sl], gim[_rows8(r), sl]
                pr, pi = sre_ref[_rows8(r - 1), sl], sim_ref[_rows8(r - 1), sl]
                return xr + gr * pr + gi * pi, xi + gi * pr - gr * pi
            xr, xi = lax.fori_loop(1, R, dacc, acc0)
            dar_ref[:, sl] += xr
            dai_ref[:, sl] += xi

    rev = lambda w, cidx: pl.BlockSpec((TC, w), lambda i, cidx=cidx: (nc - 1 - i, cidx))
    one = lambda shape: pl.BlockSpec(shape, lambda i: (0,) * len(shape))
    return pl.pallas_call(
        body, grid=(nc,),
        in_specs=[rev(SW, 0), rev(SW, 0), rev(SW, 6), rev(SW, 7), rev(NS, 0), rev(NS, 0),
                  pl.BlockSpec((8, NS), lambda i: (nc - 1 - i, 0)), pl.BlockSpec((8, NS), lambda i: (nc - 1 - i, 0)),
                  one((4, 128, 512)), one((4, 128, 512)), one((4, 512, 128)), one((4, 512, 128)),
                  one((1, NS)), one((1, NS)), one((R, NS)), one((R, NS)),
                  one((1, SW)), one((SW, SW)), one((1, SW)), _ANY],
        out_specs=[pl.BlockSpec((TC, 2 * SW), lambda i: (nc - 1 - i, 3)),
                   one((4, 128, 512)), one((4, 128, 512)), one((4, 512, 128)), one((4, 512, 128)),
                   one((1, SW)), one((8, NS)), one((8, NS)), one((SW, SW)), one((1, SW))],
        out_shape=[jax.ShapeDtypeStruct((L, IN_W), bf16),
                   jax.ShapeDtypeStruct((4, 128, 512), f32), jax.ShapeDtypeStruct((4, 128, 512), f32),
                   jax.ShapeDtypeStruct((4, 512, 128), f32), jax.ShapeDtypeStruct((4, 512, 128), f32),
                   jax.ShapeDtypeStruct((1, SW), f32), jax.ShapeDtypeStruct((8, NS), f32),
                   jax.ShapeDtypeStruct((8, NS), f32), jax.ShapeDtypeStruct((SW, SW), f32),
                   jax.ShapeDtypeStruct((1, SW), f32)],
        scratch_shapes=[pltpu.VMEM((TC, NS), f32), pltpu.VMEM((TC, NS), f32), pltpu.VMEM((8, NS), f32),
                        pltpu.VMEM((8, NS), f32), pltpu.VMEM((8, NS), f32), pltpu.VMEM((8, NS), f32)],
        input_output_aliases={19: 0},
        name="ssm_bwd", compiler_params=_cp("arbitrary"))(
            d_bin, y0, proj, proj, sre, sim, cinr, cini, bbt_re, bbt_im, ct_re, ct_im,
            a_re, a_im, apow_re, apow_im, dvec, wglu, bglu, dproj)


def _conv_bwd(d_ain, cu1, proj, cw, lng, lnb, dproj):
    L = cu1.shape[0]
    nc = L // TC
    def body(dain_ref, cu1_ref, ca_ref, cb_ref, zc_ref, cah_ref, cbh_ref, w_ref, g_ref, bb_ref, _,
             dp_ref, dw_ref, dbias_ref, dlng_ref, dlnb_ref, dbuf, ebuf, prev, nxt, dcu0):
        i = pl.program_id(0)
        @pl.when(i == 0)
        def _():
            dw_ref[...] = jnp.zeros_like(dw_ref)
            dbias_ref[...] = jnp.zeros_like(dbias_ref)
            dlng_ref[...] = jnp.zeros_like(dlng_ref)
            dlnb_ref[...] = jnp.zeros_like(dlnb_ref)
            nxt[...] = jnp.zeros_like(nxt)
        def lnb(s, carry):
            rows = pl.ds(pl.multiple_of(s * 32, 32), 32)
            dain = dain_ref[rows, :].astype(f32)
            c1 = cu1_ref[rows, :].astype(f32)
            zc = zc_ref[rows, :].astype(f32)
            xc = c1 - jnp.mean(c1, axis=-1, keepdims=True)
            var = jnp.mean(xc * xc, axis=-1, keepdims=True)
            rstd = lax.rsqrt(var + LN_EPS)
            xh = xc * rstd
            ln = xh * g_ref[...] + bb_ref[...]
            sl_ = _sig(ln)
            sz = _sig(zc)
            dp_ref[rows, 2 * CW:3 * CW] = (dain * (ln * sl_) * (sz * (1.0 + zc * (1.0 - sz)))).astype(bf16)
            d_ln = dain * (zc * sz) * (sl_ * (1.0 + ln * (1.0 - sl_)))
            dlng_ref[...] += jnp.sum(d_ln * xh, axis=0, keepdims=True)
            dlnb_ref[...] += jnp.sum(d_ln, axis=0, keepdims=True)
            dxh = d_ln * g_ref[...]
            d_c1 = rstd * (dxh - jnp.mean(dxh, axis=-1, keepdims=True)
                           - xh * jnp.mean(dxh * xh, axis=-1, keepdims=True))
            dbias_ref[...] += jnp.sum(d_c1, axis=0, keepdims=True)
            _put_blocked(dbuf, pl.multiple_of(s * 32, 32), 32, d_c1)
            _put_blocked(ebuf, pl.multiple_of(NH * 8 + s * 32, 32), 32,
                         ca_ref[rows, :].astype(f32) * _sig(cb_ref[rows, :].astype(f32)))
            return carry
        lax.fori_loop(0, TC // 32, lnb, 0)
        sub = lax.broadcasted_iota(jnp.int32, (8, 128), 0)
        def after(p, carry):
            for lb in range(NLB):
                cur = dbuf[lb, _rows8(p), :]
                dbuf[lb, _rows8(R + p), :] = jnp.where(sub == 7, pltpu.roll(nxt[lb, _rows8(p), :], 7, 0),
                                                       pltpu.roll(cur, 7, 0))
            return carry
        lax.fori_loop(0, NH, after, 0)
        nxt[...] = dbuf[:, 0:NH * 8, :]
        def before(s, carry):
            rows = pl.ds(pl.multiple_of(s * 64, 64), 64)
            v = cah_ref[rows, :].astype(f32) * _sig(cbh_ref[rows, :].astype(f32))
            _put_blocked(prev, pl.multiple_of(s * 64, 64), 64, jnp.where(i == nc - 1, jnp.zeros_like(v), v))
            return carry
        lax.fori_loop(0, NH * 8 // 64, before, 0)
        _fill_before(ebuf, prev)
        for lb in range(NLB):
            sl = slice(lb * 128, (lb + 1) * 128)
            wk = [jnp.broadcast_to(w_ref[k:k + 1, sl], (8, 128)) for k in range(KS)]
            def tap(q, carry, lb=lb, wk=wk):
                r = q * RPI
                for j, o in enumerate(_fir(dbuf, lb, r, wk, None, True)):
                    dcu0[lb, _rows8(r + j), :] = o
                return carry
            lax.fori_loop(0, R // RPI, tap, 0)
            def wgrad(q, accs, lb=lb):
                r = q * RPI
                dvs = dbuf[lb, pl.ds(pl.multiple_of(r * 8, 8), RPI * 8), :]
                win = ebuf[lb, pl.ds(pl.multiple_of((r + (NH - KS + 1)) * 8, 8), (KS + RPI - 1) * 8), :]
                accs = list(accs)
                for j in range(RPI):
                    dv = dvs[8 * j:8 * j + 8, :]
                    for k in range(KS):
                        accs[k] = accs[k] + dv * win[8 * (j + k):8 * (j + k) + 8, :]
                return tuple(accs)
            accs = lax.fori_loop(0, R // RPI, wgrad, tuple(jnp.zeros((8, 128), f32) for _ in range(KS)))
            for k in range(KS):
                dw_ref[k, :, sl] += accs[k]
        def glub(s, carry):
            rows = pl.ds(pl.multiple_of(s * 64, 64), 64)
            d0 = _get_blocked(dcu0, pl.multiple_of(s * 64, 64), 64)
            ca = ca_ref[rows, :].astype(f32)
            sb = _sig(cb_ref[rows, :].astype(f32))
            dp_ref[rows, 0:CW] = (d0 * sb).astype(bf16)
            dp_ref[rows, CW:2 * CW] = (d0 * ca * sb * (1.0 - sb)).astype(bf16)
            return carry
        lax.fori_loop(0, TC // 64, glub, 0)

    hrows = NH * 8
    per = TC // hrows
    rev = lambda cidx: pl.BlockSpec((TC, CW), lambda i, cidx=cidx: (nc - 1 - i, cidx))
    halo = lambda cidx: pl.BlockSpec((hrows, CW), lambda i, cidx=cidx: (jnp.maximum((nc - 1 - i) * per - 1, 0), cidx))
    one = lambda shape: pl.BlockSpec(shape, lambda i: (0,) * len(shape))
    return pl.pallas_call(
        body, grid=(nc,),
        in_specs=[rev(0), rev(0), rev(0), rev(1), rev(2), halo(0), halo(1), one((32, CW)), one((1, CW)), one((1, CW)),
                  _ANY],
        out_specs=[pl.BlockSpec((TC, 3 * CW), lambda i: (nc - 1 - i, 0)), one((32, 8, CW)), one((1, CW)), one((1, CW)), one((1, CW))],
        out_shape=[jax.ShapeDtypeStruct((L, IN_W), bf16), jax.ShapeDtypeStruct((32, 8, CW), f32),
                   jax.ShapeDtypeStruct((1, CW), f32), jax.ShapeDtypeStruct((1, CW), f32),
                   jax.ShapeDtypeStruct((1, CW), f32)],
        scratch_shapes=[pltpu.VMEM((NLB, (R + NH) * 8, 128), f32), pltpu.VMEM((NLB, (NH + R) * 8, 128), f32),
                        pltpu.VMEM((NLB, hrows, 128), f32), pltpu.VMEM((NLB, hrows, 128), f32),
                        pltpu.VMEM((NLB, TC, 128), f32)],
        input_output_aliases={10: 0},
        name="conv_bwd", compiler_params=_cp("arbitrary"))(d_ain, cu1, proj, proj, proj, proj, proj, cw, lng, lnb, dproj)


def _win_grad(h, dproj):
    L = h.shape[0]
    tm = min(1024, L)
    nt = L // tm
    def body(h_ref, d_ref, o_ref, acc):
        i = pl.program_id(1)
        part = _dot_tn(h_ref[...], d_ref[...])
        @pl.when(i == 0)
        def _():
            acc[...] = part
        @pl.when(i > 0)
        def _():
            acc[...] += part
        @pl.when(i == nt - 1)
        def _():
            o_ref[0] = acc[...].astype(bf16)
    return pl.pallas_call(
        body, grid=(NCHIP, nt),
        in_specs=[pl.BlockSpec((tm, D), lambda j, i: (i, 0)), pl.BlockSpec((tm, SHARD_W), lambda j, i: (i, j))],
        out_specs=pl.BlockSpec((1, D, SHARD_W), lambda j, i: (j, 0, 0)),
        out_shape=jax.ShapeDtypeStruct((NCHIP, D, SHARD_W), bf16),
        scratch_shapes=[pltpu.VMEM((D, SHARD_W), f32)],
        name="win_grad", compiler_params=_cp("arbitrary", "arbitrary"))(h, dproj)


def _x_grad(dproj, w_in, x, gx0, g_pre):
    L = x.shape[0]
    tm = 256
    def body(d_ref, w_ref, x_ref, gx_ref, g_ref, o_ref, dg_ref):
        @pl.when(pl.program_id(0) == 0)
        def _():
            dg_ref[...] = jnp.zeros_like(dg_ref)
        dh = _dot_nt(d_ref[:, 0:SHARD_W], w_ref[0])
        for j in range(1, NCHIP):
            dh = dh + _dot_nt(d_ref[:, j * SHARD_W:(j + 1) * SHARD_W], w_ref[j])
        xt = x_ref[...]
        r = lax.rsqrt(jnp.mean(xt * xt, axis=-1, keepdims=True) + RMS_EPS)
        xn = xt * r
        dg_ref[...] += jnp.sum(dh * xn, axis=0, keepdims=True)
        dxn = dh * g_ref[...]
        o_ref[...] = gx_ref[...] + r * (dxn - xn * jnp.mean(dxn * xn, axis=-1, keepdims=True))
    return pl.pallas_call(
        body, grid=(L // tm,),
        in_specs=[pl.BlockSpec((tm, IN_W), lambda i: (i, 0)),
                  pl.BlockSpec((NCHIP, D, SHARD_W), lambda i: (0, 0, 0), pipeline_mode=pl.Buffered(1)),
                  pl.BlockSpec((tm, D), lambda i: (i, 0)), pl.BlockSpec((tm, D), lambda i: (i, 0)), _full((1, D))],
        out_specs=[pl.BlockSpec((tm, D), lambda i: (i, 0)), _full((1, D))],
        out_shape=[jax.ShapeDtypeStruct((L, D), f32), jax.ShapeDtypeStruct((1, D), f32)],
        name="x_grad", compiler_params=_cp("arbitrary"))(dproj, w_in, x, gx0, g_pre)


def _pair_sum(c_arr, ga, ra, gb, rb, gs, rs):
    def body(c_ref, ga_ref, ra_ref, gb_ref, rb_ref, gs_ref, rs_ref, pa_ref, pb_ref, ps_ref):
        pa_ref[...] = (ga_ref[...] + ra_ref[...]).astype(bf16)
        pb_ref[...] = (gb_ref[...] + rb_ref[...]).astype(bf16)
        ps_ref[...] = gs_ref[...] + rs_ref[...]
    grid_spec = pltpu.PrefetchScalarGridSpec(
        num_scalar_prefetch=1, grid=(NCHIP,),
        in_specs=[pl.BlockSpec((1, D // 2, SHARD_W), lambda j, c: (j, c[0], 0)),
                  pl.BlockSpec((1, D // 2, SHARD_W), lambda j, c: (j, 0, 0)),
                  pl.BlockSpec((1, REST_ROWS // 2, 1024), lambda j, c: (j, c[0], 0)),
                  pl.BlockSpec((1, REST_ROWS // 2, 1024), lambda j, c: (j, 0, 0)),
                  pl.BlockSpec((SMALL_ROWS, 128), lambda j, c: (0, 0)),
                  pl.BlockSpec((SMALL_ROWS, 128), lambda j, c: (0, 0))],
        out_specs=[pl.BlockSpec((1, D // 2, SHARD_W), lambda j, c: (j, 0, 0)),
                   pl.BlockSpec((1, REST_ROWS // 2, 1024), lambda j, c: (j, 0, 0)),
                   pl.BlockSpec((SMALL_ROWS, 128), lambda j, c: (0, 0))])
    return pl.pallas_call(
        body, grid_spec=grid_spec,
        out_shape=[jax.ShapeDtypeStruct((NCHIP, D // 2, SHARD_W), bf16),
                   jax.ShapeDtypeStruct((NCHIP, REST_ROWS // 2, 1024), bf16),
                   jax.ShapeDtypeStruct((SMALL_ROWS, 128), f32)],
        name="pair_sum", compiler_params=_cp("arbitrary"))(c_arr, ga, ra, gb, rb, gs, rs)


def _chip_sum(qa, qb, qs):
    nt = 4
    def body(qa_ref, qb_ref, qs_ref, fa_ref, fb_ref, fs_ref):
        for q_ref, f_ref in ((qa_ref, fa_ref), (qb_ref, fb_ref), (qs_ref, fs_ref)):
            acc = q_ref[0].astype(f32)
            for j in range(1, NCHIP):
                acc = acc + q_ref[j].astype(f32)
            f_ref[...] = acc
    ra, rb, rs = D // 2 // nt, REST_ROWS // 2 // nt, SMALL_ROWS // nt
    return pl.pallas_call(
        body, grid=(nt,),
        in_specs=[pl.BlockSpec((NCHIP, ra, SHARD_W), lambda i: (0, i, 0)),
                  pl.BlockSpec((NCHIP, rb, 1024), lambda i: (0, i, 0)),
                  pl.BlockSpec((NCHIP, rs, 128), lambda i: (0, i, 0))],
        out_specs=[pl.BlockSpec((ra, SHARD_W), lambda i: (i, 0)), pl.BlockSpec((rb, 1024), lambda i: (i, 0)),
                   pl.BlockSpec((rs, 128), lambda i: (i, 0))],
        out_shape=[jax.ShapeDtypeStruct((D // 2, SHARD_W), f32), jax.ShapeDtypeStruct((REST_ROWS // 2, 1024), f32),
                   jax.ShapeDtypeStruct((SMALL_ROWS, 128), f32)],
        name="chip_sum", compiler_params=_cp("arbitrary"))(qa, qb, qs)


def _adamw_math(w, g, m, v):
    m2 = B1 * m + (1.0 - B1) * g
    v2 = B2 * v + (1.0 - B2) * (g * g)
    m_hat = m2 / (1.0 - B1 ** STEP)
    v_hat = v2 / (1.0 - B2 ** STEP)
    delta = -LR * (m_hat / (jnp.sqrt(v_hat) + EPS) + WD * w)
    return delta, m2, v2


def _adamw(name, w, g, m, v):
    rows, cols = w.shape
    tm = rows if rows <= 256 else (256 if rows % 256 == 0 else 128)
    assert rows % tm == 0
    def body(w_ref, g_ref, m_ref, v_ref, d_ref, m2_ref, v2_ref):
        d, m2, v2 = _adamw_math(w_ref[...], g_ref[...], m_ref[...], v_ref[...])
        d_ref[...] = d
        m2_ref[...] = m2
        v2_ref[...] = v2
    spec = pl.BlockSpec((tm, cols), lambda i: (i, 0))
    shp = jax.ShapeDtypeStruct((rows, cols), f32)
    return pl.pallas_call(
        body, grid=(rows // tm,), in_specs=[spec] * 4, out_specs=[spec] * 3, out_shape=[shp] * 3,
        name=name, compiler_params=_cp("arbitrary"))(w, g, m, v)


_ANY = pl.BlockSpec(memory_space=pl.ANY)


def _chunks(rows, parts):
    step = rows // parts
    assert step * parts == rows and step % 16 == 0
    return [(i * step, step) for i in range(parts)]


def _place():
    x, y, c = lax.axis_index("x"), lax.axis_index("y"), lax.axis_index("c")
    chips = [(1 - x, y), (x, 1 - y), (1 - x, 1 - y)]
    return x, y, c, chips


def _gather_weights(win_s, rest_s):
    segs = [(0, D // 2, r0, n) for r0, n in _chunks(D // 2, 4)] + \
           [(1, REST_ROWS // 2, r0, n) for r0, n in _chunks(REST_ROWS // 2, 2)]
    ns = len(segs)
    def body(a_ref, b_ref, oa_ref, ob_ref, send_sems, recv_sems, local_sems):
        x, y, c, chips = _place()
        k = 2 * x + y
        sibling = (x, y, 1 - c)
        ins, outs = (a_ref, b_ref), (oa_ref, ob_ref)

        def dst(which, half, chip, pc, r0, n):
            return outs[which].at[chip, pl.ds(pc * half + r0, n), :]

        def rcopy(i, src, dst_ref, to):
            return pltpu.make_async_remote_copy(src_ref=src, dst_ref=dst_ref, send_sem=send_sems.at[i],
                                                recv_sem=recv_sems.at[i], device_id=to, device_id_type=MESH)

        own = [pltpu.make_async_copy(ins[w], outs[w].at[k], local_sems.at[w]) for w in range(2)]
        for cp in own:
            cp.start()
        first = []
        for j, chip in enumerate(chips):
            for s, (w, half, r0, n) in enumerate(segs):
                first.append(rcopy(j * ns + s, ins[w].at[pl.ds(c * half + r0, n), :], dst(w, half, k, c, r0, n),
                                   (*chip, c)))
        for cp in first:
            cp.start()
        passed = []
        for j, chip in enumerate(chips):
            cj = 2 * chip[0] + chip[1]
            for s, (w, half, r0, n) in enumerate(segs):
                landed = dst(w, half, cj, c, r0, n)
                rcopy(j * ns + s, landed, landed, (x, y, c)).wait_recv()
                fwd = rcopy(3 * ns + j * ns + s, landed, landed, sibling)
                fwd.start()
                passed.append(fwd)
        for j, chip in enumerate(chips):
            cj = 2 * chip[0] + chip[1]
            for s, (w, half, r0, n) in enumerate(segs):
                theirs = dst(w, half, cj, 1 - c, r0, n)
                rcopy(3 * ns + j * ns + s, theirs, theirs, (x, y, c)).wait_recv()
        for cp in first + passed:
            cp.wait_send()
        for cp in own:
            cp.wait()

    return pl.pallas_call(
        body, in_specs=[_ANY, _ANY], out_specs=[_ANY, _ANY],
        out_shape=[jax.ShapeDtypeStruct((NCHIP, D, SHARD_W), bf16), jax.ShapeDtypeStruct((NCHIP, REST_ROWS, 1024), bf16)],
        scratch_shapes=[pltpu.SemaphoreType.DMA((6 * ns,)), pltpu.SemaphoreType.DMA((6 * ns,)),
                        pltpu.SemaphoreType.DMA((2,))],
        name="gather_weights")(win_s, rest_s)


def _pair_exchange(ga, gb, gs):
    ha, hb = D // 2, REST_ROWS // 2
    def body(a_ref, b_ref, s_ref, ra_ref, rb_ref, rs_ref, send_sems, recv_sems):
        x, y, c, _ = _place()
        sibling = (x, y, 1 - c)
        pieces = []
        for j in range(NCHIP):
            for r0, n in _chunks(ha, 4):
                pieces.append((a_ref.at[j, pl.ds((1 - c) * ha + r0, n), :], ra_ref.at[j, pl.ds(r0, n), :]))
            for r0, n in _chunks(hb, 2):
                pieces.append((b_ref.at[j, pl.ds((1 - c) * hb + r0, n), :], rb_ref.at[j, pl.ds(r0, n), :]))
        pieces.append((s_ref, rs_ref))
        copies = [pltpu.make_async_remote_copy(src_ref=s, dst_ref=d, send_sem=send_sems.at[i], recv_sem=recv_sems.at[i],
                                               device_id=sibling, device_id_type=MESH)
                  for i, (s, d) in enumerate(pieces)]
        for cp in copies:
            cp.start()
        for cp in copies:
            cp.wait_recv()
        for cp in copies:
            cp.wait_send()

    n = NCHIP * 6 + 1
    return pl.pallas_call(
        body, in_specs=[_ANY, _ANY, _ANY], out_specs=[_ANY, _ANY, _ANY],
        out_shape=[jax.ShapeDtypeStruct((NCHIP, ha, SHARD_W), f32), jax.ShapeDtypeStruct((NCHIP, hb, 1024), f32),
                   jax.ShapeDtypeStruct((SMALL_ROWS, 128), f32)],
        scratch_shapes=[pltpu.SemaphoreType.DMA((n,)), pltpu.SemaphoreType.DMA((n,))],
        name="pair_exchange")(ga, gb, gs)


def _chip_exchange(pa, pb, ps):
    ha, hb = D // 2, REST_ROWS // 2
    def body(a_ref, b_ref, s_ref, qa_ref, qb_ref, qs_ref, send_sems, recv_sems, local_sems):
        x, y, c, chips = _place()
        k = 2 * x + y
        own = [pltpu.make_async_copy(a_ref.at[k], qa_ref.at[k], local_sems.at[0]),
               pltpu.make_async_copy(b_ref.at[k], qb_ref.at[k], local_sems.at[1]),
               pltpu.make_async_copy(s_ref, qs_ref.at[k], local_sems.at[2])]
        for cp in own:
            cp.start()
        copies = []
        for j, chip in enumerate(chips):
            cj = 2 * chip[0] + chip[1]
            pieces = [(a_ref.at[cj, pl.ds(r0, n), :], qa_ref.at[k, pl.ds(r0, n), :]) for r0, n in _chunks(ha, 2)]
            pieces += [(b_ref.at[cj], qb_ref.at[k]), (s_ref, qs_ref.at[k])]
            for s, (src, dst_ref) in enumerate(pieces):
                copies.append(pltpu.make_async_remote_copy(
                    src_ref=src, dst_ref=dst_ref, send_sem=send_sems.at[4 * j + s], recv_sem=recv_sems.at[4 * j + s],
                    device_id=(*chip, c), device_id_type=MESH))
        for cp in copies:
            cp.start()
        for cp in copies:
            cp.wait_recv()
        for cp in copies:
            cp.wait_send()
        for cp in own:
            cp.wait()

    return pl.pallas_call(
        body, in_specs=[_ANY, _ANY, _ANY], out_specs=[_ANY, _ANY, _ANY],
        out_shape=[jax.ShapeDtypeStruct((NCHIP, ha, SHARD_W), bf16), jax.ShapeDtypeStruct((NCHIP, hb, 1024), bf16),
                   jax.ShapeDtypeStruct((NCHIP, SMALL_ROWS, 128), f32)],
        scratch_shapes=[pltpu.SemaphoreType.DMA((12,)), pltpu.SemaphoreType.DMA((12,)), pltpu.SemaphoreType.DMA((3,))],
        name="chip_exchange")(pa, pb, ps)


def _sibling_exchange(fa, fb):
    ha, hb = D // 2, REST_ROWS // 2
    def body(a_ref, b_ref, oa_ref, ob_ref, send_sems, recv_sems, local_sems):
        x, y, c, _ = _place()
        own = [pltpu.make_async_copy(a_ref, oa_ref.at[c], local_sems.at[0]),
               pltpu.make_async_copy(b_ref, ob_ref.at[c], local_sems.at[1])]
        for cp in own:
            cp.start()
        pieces = [(a_ref.at[pl.ds(r0, n), :], oa_ref.at[c, pl.ds(r0, n), :]) for r0, n in _chunks(ha, 4)]
        pieces += [(b_ref.at[pl.ds(r0, n), :], ob_ref.at[c, pl.ds(r0, n), :]) for r0, n in _chunks(hb, 2)]
        copies = [pltpu.make_async_remote_copy(src_ref=s, dst_ref=d, send_sem=send_sems.at[i], recv_sem=recv_sems.at[i],
                                               device_id=(x, y, 1 - c), device_id_type=MESH)
                  for i, (s, d) in enumerate(pieces)]
        for cp in copies:
            cp.start()
        for cp in copies:
            cp.wait_recv()
        for cp in copies:
            cp.wait_send()
        for cp in own:
            cp.wait()

    return pl.pallas_call(
        body, in_specs=[_ANY, _ANY], out_specs=[_ANY, _ANY],
        out_shape=[jax.ShapeDtypeStruct((2, ha, SHARD_W), f32), jax.ShapeDtypeStruct((2, hb, 1024), f32)],
        scratch_shapes=[pltpu.SemaphoreType.DMA((6,)), pltpu.SemaphoreType.DMA((6,)), pltpu.SemaphoreType.DMA((2,))],
        name="sibling_exchange")(fa, fb)


def _nchunks(half, cols, itemsize):
    return 4 if half * cols * itemsize >= (1 << 20) else 1


def _segments(metas):
    segs = []
    for w, (half, cols, dt) in enumerate(metas):
        for r0, n in _chunks(half, _nchunks(half, cols, jnp.dtype(dt).itemsize)):
            segs.append((w, half, r0, n))
    return segs


def _rcopy(i, src, dst, send_sems, recv_sems, to):
    return pltpu.make_async_remote_copy(src_ref=src, dst_ref=dst, send_sem=send_sems.at[i], recv_sem=recv_sems.at[i],
                                        device_id=to, device_id_type=MESH)


def _gather_list(shards):
    na = len(shards)
    segs = _segments([(a.shape[0] // 2, a.shape[1], a.dtype) for a in shards])
    ns = len(segs)
    def body(*refs):
        ins, outs, (send_sems, recv_sems) = refs[:na], refs[na:2 * na], refs[2 * na:]
        x, y, c, chips = _place()
        k = 2 * x + y
        me, sibling = (x, y, c), (x, y, 1 - c)

        def dst(w, half, chip, pc, r0, n):
            return outs[w].at[chip, pl.ds(pc * half + r0, n), :]

        first = []
        for j, chip in enumerate(chips):
            for s, (w, half, r0, n) in enumerate(segs):
                first.append(_rcopy(j * ns + s, ins[w].at[pl.ds(c * half + r0, n), :], dst(w, half, k, c, r0, n),
                                    send_sems, recv_sems, (*chip, c)))
        for cp in first:
            cp.start()
        passed = []
        for j, chip in enumerate(chips):
            cj = 2 * chip[0] + chip[1]
            for s, (w, half, r0, n) in enumerate(segs):
                landed = dst(w, half, cj, c, r0, n)
                _rcopy(j * ns + s, landed, landed, send_sems, recv_sems, me).wait_recv()
                fwd = _rcopy(3 * ns + j * ns + s, landed, landed, send_sems, recv_sems, sibling)
                fwd.start()
                passed.append(fwd)
        for j, chip in enumerate(chips):
            cj = 2 * chip[0] + chip[1]
            for s, (w, half, r0, n) in enumerate(segs):
                theirs = dst(w, half, cj, 1 - c, r0, n)
                _rcopy(3 * ns + j * ns + s, theirs, theirs, send_sems, recv_sems, me).wait_recv()
        for cp in first + passed:
            cp.wait_send()

    return pl.pallas_call(
        body, in_specs=[_ANY] * na, out_specs=[_ANY] * na,
        out_shape=[jax.ShapeDtypeStruct((NCHIP,) + a.shape, a.dtype) for a in shards],
        scratch_shapes=[pltpu.SemaphoreType.DMA((6 * ns,)), pltpu.SemaphoreType.DMA((6 * ns,))],
        name="gather_weights")(*shards)


def _gather_prep(k_arr, shards, x, tgt, g_pre, perm):
    na = len(shards)
    L = x.shape[0]
    nc = L // TC
    segs = _segments([(a.shape[0] // 2, a.shape[1], a.dtype) for a in shards])
    ns = len(segs)
    def body(_, *refs):
        ins = refs[:na]
        x_ref, t_ref, g_ref, p_ref = refs[na:na + 4]
        outs = refs[na + 4:2 * na + 4]
        h_ref, xi_ref, ti_ref, proj_ref = refs[2 * na + 4:2 * na + 8]
        stages = refs[2 * na + 8:3 * na + 8]
        send_sems, recv_sems, local_sems = refs[3 * na + 8:]
        i = pl.program_id(0)
        x, y, c, chips = _place()
        k = 2 * x + y
        me, sibling = (x, y, c), (x, y, 1 - c)

        def dst(w, half, chip, pc, r0, n):
            return outs[w].at[chip, pl.ds(pc * half + r0, n), :]

        def firsts():
            return [_rcopy(j * ns + s, ins[w].at[pl.ds(c * half + r0, n), :], dst(w, half, k, c, r0, n),
                           send_sems, recv_sems, (*chip, c))
                    for j, chip in enumerate(chips) for s, (w, half, r0, n) in enumerate(segs)]

        def own_out(w):
            return pltpu.make_async_copy(stages[w], outs[w].at[k], local_sems.at[w])

        @pl.when(i == 0)
        def _():
            for cp in firsts():
                cp.start()
            for w in range(na):
                cin = pltpu.make_async_copy(ins[w], stages[w], local_sems.at[w])
                cin.start()
                cin.wait()
            for w in range(na):
                own_out(w).start()

        p = p_ref[...]
        def through(v):
            hi = v.astype(bf16)
            r1 = v - hi.astype(f32)
            mid = r1.astype(bf16)
            lo = (r1 - mid.astype(f32)).astype(bf16)
            return (_dot(p, hi) + _dot(p, mid)) + _dot(p, lo)
        xt = x_ref[...]
        r = lax.rsqrt(jnp.mean(xt * xt, axis=-1, keepdims=True) + RMS_EPS)
        hp = _dot(p, (xt * r * g_ref[...]).astype(bf16)).astype(bf16)
        h_ref[...] = hp
        proj_ref[...] = _dot(hp, stages[0][...]).astype(bf16)
        xi_ref[...] = through(xt)
        ti_ref[...] = through(t_ref[...])

        @pl.when(i == nc - 1)
        def _():
            passed = []
            for j, chip in enumerate(chips):
                cj = 2 * chip[0] + chip[1]
                for s, (w, half, r0, n) in enumerate(segs):
                    landed = dst(w, half, cj, c, r0, n)
                    _rcopy(j * ns + s, landed, landed, send_sems, recv_sems, me).wait_recv()
                    fwd = _rcopy(3 * ns + j * ns + s, landed, landed, send_sems, recv_sems, sibling)
                    fwd.start()
                    passed.append(fwd)
            for j, chip in enumerate(chips):
                cj = 2 * chip[0] + chip[1]
                for s, (w, half, r0, n) in enumerate(segs):
                    theirs = dst(w, half, cj, 1 - c, r0, n)
                    _rcopy(3 * ns + j * ns + s, theirs, theirs, send_sems, recv_sems, me).wait_recv()
            for cp in firsts() + passed:
                cp.wait_send()
            for w in range(na):
                own_out(w).wait()

    row = lambda: pl.BlockSpec((TC, D), lambda i, k: (i, 0))
    grid_spec = pltpu.PrefetchScalarGridSpec(
        num_scalar_prefetch=1, grid=(nc,),
        in_specs=[_ANY] * na + [row(), row(), pl.BlockSpec((1, D), lambda i, k: (0, 0)),
                                pl.BlockSpec((TC, TC), lambda i, k: (0, 0))],
        out_specs=[_ANY] * na + [row(), row(), row(), pl.BlockSpec((TC, SHARD_W), lambda i, k: (i, k[0]))],
        scratch_shapes=[pltpu.VMEM(a.shape, a.dtype) for a in shards]
        + [pltpu.SemaphoreType.DMA((6 * ns,)), pltpu.SemaphoreType.DMA((6 * ns,)), pltpu.SemaphoreType.DMA((na,))])
    return pl.pallas_call(
        body, grid_spec=grid_spec,
        out_shape=[jax.ShapeDtypeStruct((NCHIP,) + a.shape, a.dtype) for a in shards]
        + [jax.ShapeDtypeStruct((L, D), bf16), jax.ShapeDtypeStruct((L, D), f32), jax.ShapeDtypeStruct((L, D), f32),
           jax.ShapeDtypeStruct((L, IN_W), bf16)],
        name="gather_prep", compiler_params=_cp("arbitrary"))(k_arr, *shards, x, tgt, g_pre, perm)


def _x_grad_exchange(dproj, w_in, x, gx0, g_pre, parts, small):
    L = x.shape[0]
    tm = 512
    nt = L // tm
    na = len(parts)
    segs = _segments([(p.shape[1], p.shape[2], p.dtype) for p in parts])
    ns = len(segs) + 1
    def body(*refs):
        d_ref, w_ref, x_ref, gx_ref, g_ref = refs[:5]
        ins, s_ref = refs[5:5 + na], refs[5 + na]
        o_ref, dg_ref = refs[6 + na:8 + na]
        outs, qs_ref = refs[8 + na:8 + 2 * na], refs[8 + 2 * na]
        stages = refs[9 + 2 * na:10 + 3 * na]
        send_sems, recv_sems, local_sems = refs[10 + 3 * na:]
        i = pl.program_id(0)
        x, y, c, chips = _place()
        k = 2 * x + y

        def copies():
            out = []
            for j, chip in enumerate(chips):
                cj = 2 * chip[0] + chip[1]
                pieces = [(s_ref, qs_ref.at[k])]
                pieces += [(ins[w].at[cj, pl.ds(r0, n), :], outs[w].at[k, pl.ds(r0, n), :]) for w, _, r0, n in segs]
                out += [_rcopy(ns * j + s, src, d, send_sems, recv_sems, (*chip, c)) for s, (src, d) in enumerate(pieces)]
            return out

        def own_out(w):
            dst = qs_ref.at[k] if w == na else outs[w].at[k]
            return pltpu.make_async_copy(stages[w], dst, local_sems.at[w])

        @pl.when(i == 0)
        def _():
            dg_ref[...] = jnp.zeros_like(dg_ref)
            for cp in copies():
                cp.start()
            for w in range(na + 1):
                cin = pltpu.make_async_copy(s_ref if w == na else ins[w].at[k], stages[w], local_sems.at[w])
                cin.start()
                cin.wait()
            for w in range(na + 1):
                own_out(w).start()

        dh = _dot_nt(d_ref[:, 0:SHARD_W], w_ref[0])
        for j in range(1, NCHIP):
            dh = dh + _dot_nt(d_ref[:, j * SHARD_W:(j + 1) * SHARD_W], w_ref[j])
        xt = x_ref[...]
        r = lax.rsqrt(jnp.mean(xt * xt, axis=-1, keepdims=True) + RMS_EPS)
        xn = xt * r
        dg_ref[...] += jnp.sum(dh * xn, axis=0, keepdims=True)
        dxn = dh * g_ref[...]
        o_ref[...] = gx_ref[...] + r * (dxn - xn * jnp.mean(dxn * xn, axis=-1, keepdims=True))

        @pl.when(i == nt - 1)
        def _():
            for cp in copies():
                cp.wait_recv()
            for cp in copies():
                cp.wait_send()
            for w in range(na + 1):
                own_out(w).wait()

    return pl.pallas_call(
        body, grid=(nt,),
        in_specs=[pl.BlockSpec((tm, IN_W), lambda i: (i, 0)),
                  pl.BlockSpec((NCHIP, D, SHARD_W), lambda i: (0, 0, 0), pipeline_mode=pl.Buffered(1)),
                  pl.BlockSpec((tm, D), lambda i: (i, 0)), pl.BlockSpec((tm, D), lambda i: (i, 0)), _full((1, D))]
        + [_ANY] * (na + 1),
        out_specs=[pl.BlockSpec((tm, D), lambda i: (i, 0)), _full((1, D))] + [_ANY] * (na + 1),
        out_shape=[jax.ShapeDtypeStruct((L, D), f32), jax.ShapeDtypeStruct((1, D), f32)]
        + [jax.ShapeDtypeStruct(p.shape, bf16) for p in parts] + [jax.ShapeDtypeStruct((NCHIP, SMALL_ROWS, 128), f32)],
        scratch_shapes=[pltpu.VMEM(p.shape[1:], bf16) for p in parts] + [pltpu.VMEM((SMALL_ROWS, 128), f32)]
        + [pltpu.SemaphoreType.DMA((3 * ns,)), pltpu.SemaphoreType.DMA((3 * ns,)), pltpu.SemaphoreType.DMA((na + 1,))],
        name="x_grad_exchange", compiler_params=_cp("arbitrary"))(dproj, w_in, x, gx0, g_pre, *parts, small)


def _sibling_join_list(halves):
    na = len(halves)
    segs = _segments([(h.shape[0], h.shape[1], h.dtype) for h in halves])
    def body(*refs):
        ins, outs, stages = refs[:na], refs[na:2 * na], refs[2 * na:3 * na]
        send_sems, recv_sems, local_sems = refs[3 * na:]
        x, y, c, _ = _place()
        copies = [_rcopy(i, ins[w].at[pl.ds(r0, n), :], outs[w].at[pl.ds(c * half + r0, n), :], send_sems, recv_sems,
                         (x, y, 1 - c)) for i, (w, half, r0, n) in enumerate(segs)]
        for cp in copies:
            cp.start()
        own = []
        for w in range(na):
            cin = pltpu.make_async_copy(ins[w], stages[w], local_sems.at[w])
            cin.start()
            cin.wait()
            half = halves[w].shape[0]
            own.append(pltpu.make_async_copy(stages[w], outs[w].at[pl.ds(c * half, half), :], local_sems.at[w]))
            own[-1].start()
        for cp in copies:
            cp.wait_recv()
        for cp in copies:
            cp.wait_send()
        for cp in own:
            cp.wait()

    return pl.pallas_call(
        body, in_specs=[_ANY] * na, out_specs=[_ANY] * na,
        out_shape=[jax.ShapeDtypeStruct((2 * h.shape[0], h.shape[1]), f32) for h in halves],
        scratch_shapes=[pltpu.VMEM(h.shape, f32) for h in halves]
        + [pltpu.SemaphoreType.DMA((len(segs),)), pltpu.SemaphoreType.DMA((len(segs),)), pltpu.SemaphoreType.DMA((na,))],
        name="sibling_join")(*halves)


def _allgather_rows(v):
    def body(v_ref, o_ref, send_sems, recv_sems):
        x, y, c, _ = _place()
        me = 4 * x + 2 * y + c
        o_ref[me] = v_ref[...]
        copies = []
        i = 0
        for dx in range(2):
            for dy in range(2):
                for dc in range(2):
                    if dx + dy + dc:
                        copies.append(_rcopy(i, v_ref, o_ref.at[me], send_sems, recv_sems, (x ^ dx, y ^ dy, c ^ dc)))
                        i += 1
        for cp in copies:
            cp.start()
        for cp in copies:
            cp.wait_recv()
        for cp in copies:
            cp.wait_send()

    vm = pl.BlockSpec(memory_space=pltpu.VMEM)
    return pl.pallas_call(
        body, in_specs=[vm], out_specs=vm, out_shape=jax.ShapeDtypeStruct((8, 8, 128), f32),
        scratch_shapes=[pltpu.SemaphoreType.DMA((7,)), pltpu.SemaphoreType.DMA((7,))],
        name="allgather_rows")(v)


def _adamw_rows(parts, w, m, v):
    def body(p_ref, w_ref, m_ref, v_ref, g_ref, d_ref, m2_ref, v2_ref):
        g = p_ref[0]
        for dvc in range(1, 8):
            g = g + p_ref[dvc]
        g_ref[...] = g
        d, m2, v2 = _adamw_math(w_ref[...], g, m_ref[...], v_ref[...])
        d_ref[...] = d
        m2_ref[...] = m2
        v2_ref[...] = v2
    return pl.pallas_call(body, out_shape=[jax.ShapeDtypeStruct((8, 128), f32)] * 4, name="adamw_pre_norm_gain")(
        parts, w, m, v)


def _pair_exchange_list(grads, small):
    na = len(grads)
    segs = _segments([(g.shape[1] // 2, g.shape[2], g.dtype) for g in grads])
    n = NCHIP * len(segs) + 1
    def body(*refs):
        ins, s_ref, outs, rs_ref, (send_sems, recv_sems) = (refs[:na], refs[na], refs[na + 1:2 * na + 1],
                                                            refs[2 * na + 1], refs[2 * na + 2:])
        x, y, c, _ = _place()
        pieces = [(s_ref, rs_ref)]
        for j in range(NCHIP):
            for w, half, r0, rows in segs:
                pieces.append((ins[w].at[j, pl.ds((1 - c) * half + r0, rows), :], outs[w].at[j, pl.ds(r0, rows), :]))
        copies = [_rcopy(i, s, d, send_sems, recv_sems, (x, y, 1 - c)) for i, (s, d) in enumerate(pieces)]
        for cp in copies:
            cp.start()
        for cp in copies:
            cp.wait_recv()
        for cp in copies:
            cp.wait_send()

    return pl.pallas_call(
        body, in_specs=[_ANY] * (na + 1), out_specs=[_ANY] * (na + 1),
        out_shape=[jax.ShapeDtypeStruct((NCHIP, g.shape[1] // 2, g.shape[2]), g.dtype) for g in grads]
        + [jax.ShapeDtypeStruct((SMALL_ROWS, 128), f32)],
        scratch_shapes=[pltpu.SemaphoreType.DMA((n,)), pltpu.SemaphoreType.DMA((n,))],
        name="pair_exchange")(*grads, small)


def _pair_sum_list(c_arr, grads, recvs, small, rsmall):
    na = len(grads)
    def body(c_ref, *refs):
        g_refs, r_refs, s_ref, rs_ref = refs[:na], refs[na:2 * na], refs[2 * na], refs[2 * na + 1]
        o_refs, os_ref = refs[2 * na + 2:3 * na + 2], refs[3 * na + 2]
        for g_ref, r_ref, o_ref in zip(g_refs, r_refs, o_refs):
            o_ref[...] = (g_ref[...].astype(f32) + r_ref[...].astype(f32)).astype(bf16)
        os_ref[...] = s_ref[...] + rs_ref[...]
    half = lambda g: pl.BlockSpec((1, g.shape[1] // 2, g.shape[2]), lambda j, c: (j, c[0], 0))
    low = lambda g: pl.BlockSpec((1, g.shape[1] // 2, g.shape[2]), lambda j, c: (j, 0, 0))
    sm = pl.BlockSpec((SMALL_ROWS, 128), lambda j, c: (0, 0))
    grid_spec = pltpu.PrefetchScalarGridSpec(
        num_scalar_prefetch=1, grid=(NCHIP,),
        in_specs=[half(g) for g in grads] + [low(g) for g in grads] + [sm, sm],
        out_specs=[low(g) for g in grads] + [sm])
    return pl.pallas_call(
        body, grid_spec=grid_spec,
        out_shape=[jax.ShapeDtypeStruct((NCHIP, g.shape[1] // 2, g.shape[2]), bf16) for g in grads]
        + [jax.ShapeDtypeStruct((SMALL_ROWS, 128), f32)],
        name="pair_sum", compiler_params=_cp("arbitrary"))(c_arr, *grads, *recvs, small, rsmall)


def _chip_exchange_list(parts, small):
    na = len(parts)
    segs = _segments([(p.shape[1], p.shape[2], p.dtype) for p in parts])
    ns = len(segs) + 1
    def body(*refs):
        ins, s_ref, outs, qs_ref, (send_sems, recv_sems) = (refs[:na], refs[na], refs[na + 1:2 * na + 1],
                                                            refs[2 * na + 1], refs[2 * na + 2:])
        x, y, c, chips = _place()
        k = 2 * x + y
        copies = []
        for j, chip in enumerate(chips):
            cj = 2 * chip[0] + chip[1]
            pieces = [(s_ref, qs_ref.at[k])]
            pieces += [(ins[w].at[cj, pl.ds(r0, n), :], outs[w].at[k, pl.ds(r0, n), :]) for w, _, r0, n in segs]
            copies += [_rcopy(ns * j + s, src, d, send_sems, recv_sems, (*chip, c)) for s, (src, d) in enumerate(pieces)]
        for cp in copies:
            cp.start()
        for cp in copies:
            cp.wait_recv()
        for cp in copies:
            cp.wait_send()

    return pl.pallas_call(
        body, in_specs=[_ANY] * (na + 1), out_specs=[_ANY] * (na + 1),
        out_shape=[jax.ShapeDtypeStruct(p.shape, bf16) for p in parts]
        + [jax.ShapeDtypeStruct((NCHIP, SMALL_ROWS, 128), f32)],
        scratch_shapes=[pltpu.SemaphoreType.DMA((3 * ns,)), pltpu.SemaphoreType.DMA((3 * ns,))],
        name="chip_exchange")(*parts, small)


def _chip_sum_list(parts, small):
    na = len(parts)
    nt = 2
    def body(*refs):
        for q_ref, f_ref in zip(refs[:na + 1], refs[na + 1:]):
            acc = q_ref[0].astype(f32)
            for j in range(1, NCHIP):
                acc = acc + q_ref[j].astype(f32)
            f_ref[...] = acc
    arrs = list(parts) + [small]
    return pl.pallas_call(
        body, grid=(nt,),
        in_specs=[pl.BlockSpec((NCHIP, a.shape[1] // nt, a.shape[2]), lambda i: (0, i, 0)) for a in arrs],
        out_specs=[pl.BlockSpec((a.shape[1] // nt, a.shape[2]), lambda i: (i, 0)) for a in arrs],
        out_shape=[jax.ShapeDtypeStruct(a.shape[1:], f32) for a in arrs],
        name="chip_sum", compiler_params=_cp("arbitrary"))(*arrs)


def _sibling_exchange_list(halves):
    na = len(halves)
    segs = _segments([(h.shape[0], h.shape[1], h.dtype) for h in halves])
    def body(*refs):
        ins, outs, (send_sems, recv_sems) = refs[:na], refs[na:2 * na], refs[2 * na:]
        x, y, c, _ = _place()
        copies = [_rcopy(i, ins[w].at[pl.ds(r0, n), :], outs[w].at[pl.ds(r0, n), :], send_sems, recv_sems, (x, y, 1 - c))
                  for i, (w, _, r0, n) in enumerate(segs)]
        for cp in copies:
            cp.start()
        for cp in copies:
            cp.wait_recv()
        for cp in copies:
            cp.wait_send()

    return pl.pallas_call(
        body, in_specs=[_ANY] * na, out_specs=[_ANY] * na,
        out_shape=[jax.ShapeDtypeStruct(h.shape, f32) for h in halves],
        scratch_shapes=[pltpu.SemaphoreType.DMA((len(segs),)), pltpu.SemaphoreType.DMA((len(segs),))],
        name="sibling_exchange")(*halves)


_REST_ROWS = (256, 256, 64, 128)
_CONV_PAD = 8192


def _pack_rest(mats, conv_rows, dtype):
    parts = [mats[0], mats[1], mats[2].reshape(64, 1024), mats[3].reshape(128, 1024)]
    parts = [p.astype(dtype) for p in parts] + [conv_rows]
    used = sum(p.shape[0] for p in parts)
    parts.append(jnp.zeros((REST_ROWS - used, 1024), dtype))
    return jnp.concatenate(parts, axis=0)


def _pack_rest_weights(mats, conv_w_s):
    flat = jnp.pad(conv_w_s.reshape(-1), (0, _CONV_PAD - KS * 256))
    return _pack_rest(mats, lax.bitcast_convert_type(flat, bf16).reshape(16, 1024), bf16)


def _pack_rest_grads(mats, conv_w_s):
    flat = jnp.pad(conv_w_s.reshape(-1), (0, _CONV_PAD - KS * 256))
    return _pack_rest(mats, flat.reshape(8, 1024), f32)


def _split_rest(p, conv_rows):
    o = 0
    out = []
    for rows in _REST_ROWS + (conv_rows,):
        out.append(p[..., o:o + rows, :])
        o += rows
    return out


_SMALL = (("conv_b", (1, 1024)), ("conv_ln_gain", (1, 1024)), ("conv_ln_bias", (1, 1024)),
          ("ssm_lambda_re", (1, 32, 64)), ("ssm_lambda_im", (1, 32, 64)), ("ssm_log_dt", (1, 32)),
          ("ssm_b_re", (1, 32, 64, 16)), ("ssm_b_im", (1, 32, 64, 16)), ("ssm_c_re", (1, 32, 16, 64)),
          ("ssm_c_im", (1, 32, 16, 64)), ("ssm_d", (1, 32, 16)), ("b_ssm_glu", (1, 512)), ("post_norm_gain", (1, 1024)))


def _pack_small(vals, extra=None):
    rows = []
    for v in list(vals) + ([extra] if extra is not None else []):
        flat = v.reshape(-1).astype(f32)
        n = -(-flat.shape[0] // 1024) * 1024
        rows.append(jnp.pad(flat, (0, n - flat.shape[0])).reshape(-1, 128))
    used = sum(r.shape[0] for r in rows)
    rows.append(jnp.zeros((SMALL_ROWS - used, 128), f32))
    return jnp.concatenate(rows, axis=0)


def _unpack_small(p):
    o = 0
    out = []
    for _, shape in _SMALL:
        n = int(np.prod(shape))
        nr = -(-n // 1024) * 8
        out.append(p[o:o + nr].reshape(-1)[:n].reshape(shape))
        o += nr
    return out, p[o, 0]


def _discretize(lam_re, lam_im, log_dt, b_re, b_im):
    dt = jnp.exp(log_dt)[:, None]
    mag = jnp.exp(lam_re * dt)
    ar = mag * jnp.cos(lam_im * dt)
    ai = mag * jnp.sin(lam_im * dt)
    den = lam_re * lam_re + lam_im * lam_im
    zr = ((ar - 1.0) * lam_re + ai * lam_im) / den
    zi = (ai * lam_re - (ar - 1.0) * lam_im) / den
    bbr = zr[..., None] * b_re - zi[..., None] * b_im
    bbi = zr[..., None] * b_im + zi[..., None] * b_re
    return ar, ai, bbr, bbi


_EYE8 = np.eye(8, dtype=np.float32)


def _bbt_blocks(bb):
    v = bb.reshape(4, 8, PST, H).transpose(0, 1, 3, 2)
    return jnp.einsum("bghp,gk->bghkp", v, _EYE8).reshape(4, 128, 512)


def _bbt_unblock(m):
    v = jnp.einsum("bghkp,gk->bghp", m.reshape(4, 8, H, 8, PST), _EYE8)
    return v.transpose(0, 1, 3, 2).reshape(G, PST, H)


def _ct_blocks(cc):
    v = cc.reshape(4, 8, H, PST)
    return jnp.einsum("bghp,gk->bgpkh", v, _EYE8).reshape(4, 512, 128)


def _ct_unblock(m):
    v = jnp.einsum("bgpkh,gk->bghp", m.reshape(4, 8, PST, 8, H), _EYE8)
    return v.reshape(G, H, PST)


def _perm_matrix():
    p = np.zeros((TC, TC), np.float32)
    for r in range(R):
        for seg in range(8):
            p[r * 8 + seg, seg * R + r] = 1.0
    return p


def _deinterleave(a):
    L, C = a.shape
    return a.reshape(L // TC, R, 8, C).transpose(0, 2, 1, 3).reshape(L, C)


def _fwd_bwd(h, xi, ti, proj, conv_w, w_co, w_glu, w_so, w_out, small):
    (conv_b, ln_g, ln_b, lam_re, lam_im, log_dt, b_re, b_im, c_re, c_im, dvec, b_glu, g_post) = small
    lam_re, lam_im, log_dt = lam_re[0], lam_im[0], log_dt[0]
    b_re, b_im, c_re, c_im = b_re[0], b_im[0], c_re[0], c_im[0]
    (ar, ai, bbr, bbi), disc_vjp = jax.vjp(_discretize, lam_re, lam_im, log_dt, b_re, b_im)
    a_re = ar.reshape(1, NS)
    a_im = ai.reshape(1, NS)
    dt = jnp.exp(log_dt)[:, None]
    steps = jnp.arange(1, R + 1, dtype=f32)[:, None, None]
    apow_re = (jnp.exp(steps * (lam_re * dt)) * jnp.cos(steps * (lam_im * dt))).reshape(R, NS)
    apow_im = (jnp.exp(steps * (lam_re * dt)) * jnp.sin(steps * (lam_im * dt))).reshape(R, NS)
    bbt_re, bbt_im = _bbt_blocks(bbr).astype(bf16), _bbt_blocks(bbi).astype(bf16)
    ct_re, ct_im = _ct_blocks(c_re).astype(bf16), _ct_blocks(c_im).astype(bf16)
    d_row = dvec.reshape(1, SW)
    cw32 = jnp.pad(conv_w, ((0, 1), (0, 0)))

    cu1, a_in = _conv_fwd(proj, cw32, conv_b, ln_g, ln_b)
    y0, b_in, sre, sim, cinr, cini = _ssm_fwd(proj, bbt_re, bbt_im, ct_re, ct_im, a_re, a_im,
                                              apow_re, apow_im, d_row, w_glu, b_glu)
    gx0, d_ain, d_bin, dproj, dw_out, dw_co, dw_so, dg_post, loss = _tail(
        a_in, b_in, proj, xi, ti, w_co, w_so, w_out, g_post)
    (dproj, dbbt_re, dbbt_im, dct_re, dct_im, dd, dar8, dai8, dw_glu, db_glu) = _ssm_bwd(
        d_bin, y0, proj, sre, sim, cinr, cini, bbt_re, bbt_im, ct_re, ct_im,
        a_re, a_im, apow_re, apow_im, d_row, w_glu, b_glu, dproj)
    dproj, dcw8, d_convb, d_lng, d_lnb = _conv_bwd(d_ain, cu1, proj, cw32, ln_g, ln_b, dproj)
    dw_in = _win_grad(h, dproj)

    d_ar = jnp.sum(dar8, axis=0).reshape(G, PST)
    d_ai = jnp.sum(dai8, axis=0).reshape(G, PST)
    d_lre, d_lim, d_ldt, d_bre, d_bim = disc_vjp((d_ar, d_ai, _bbt_unblock(dbbt_re), _bbt_unblock(dbbt_im)))
    d_conv_w = jnp.sum(dcw8, axis=1)[:KS]
    small_grads = [d_convb, d_lng, d_lnb, d_lre[None], d_lim[None], d_ldt[None], d_bre[None], d_bim[None],
                   _ct_unblock(dct_re)[None], _ct_unblock(dct_im)[None], dd.reshape(1, G, H), db_glu, dg_post]
    return loss[0, 0], gx0, dproj, (dw_in, dw_co, dw_out, dw_glu, dw_so, d_conv_w), small_grads


def kernel(x, pre_norm_gain, w_in, conv_w, conv_b, conv_ln_gain, conv_ln_bias, w_conv_out, ssm_lambda_re, ssm_lambda_im, ssm_log_dt, ssm_b_re, ssm_b_im, ssm_c_re, ssm_c_im, ssm_d, w_ssm_glu, b_ssm_glu, w_ssm_out, w_out, post_norm_gain, loss_target, m_pre_norm_gain, m_w_in, m_conv_w, m_conv_b, m_conv_ln_gain, m_conv_ln_bias, m_w_conv_out, m_ssm_lambda_re, m_ssm_lambda_im, m_ssm_log_dt, m_ssm_b_re, m_ssm_b_im, m_ssm_c_re, m_ssm_c_im, m_ssm_d, m_w_ssm_glu, m_b_ssm_glu, m_w_ssm_out, m_w_out, m_post_norm_gain, v_pre_norm_gain, v_w_in, v_conv_w, v_conv_b, v_conv_ln_gain, v_conv_ln_bias, v_w_conv_out, v_ssm_lambda_re, v_ssm_lambda_im, v_ssm_log_dt, v_ssm_b_re, v_ssm_b_im, v_ssm_c_re, v_ssm_c_im, v_ssm_d, v_w_ssm_glu, v_b_ssm_glu, v_w_ssm_out, v_w_out, v_post_norm_gain):
    c = lax.axis_index("c")
    shards = [w_in[0].astype(bf16), w_conv_out[0].astype(bf16), w_out[0].astype(bf16), w_ssm_glu[0].astype(bf16),
              w_ssm_out[0].astype(bf16), jnp.pad(conv_w[0], ((0, CONV_ROWS - KS), (0, 0)))]
    k_arr = (2 * lax.axis_index("x") + lax.axis_index("y")).astype(jnp.int32).reshape(1)
    w_in_g, w_co_g, w_out_g, w_glu_g, w_so_g, conv_w_g, h, xi, ti, proj = _gather_prep(
        k_arr, shards, x[0], loss_target[0], pre_norm_gain, jnp.asarray(_perm_matrix(), bf16))
    conv_w_f = conv_w_g[:, :KS].transpose(1, 0, 2).reshape(KS, CW)

    small = (conv_b, conv_ln_gain, conv_ln_bias, ssm_lambda_re, ssm_lambda_im, ssm_log_dt, ssm_b_re,
             ssm_b_im, ssm_c_re, ssm_c_im, ssm_d, b_ssm_glu, post_norm_gain)
    loss_part, gx0, dproj, big_grads, small_grads = _fwd_bwd(
        h, xi, ti, _proj_fwd(k_arr, h, w_in_g, proj), conv_w_f, w_co_g.reshape(CW, D), w_glu_g.reshape(SW, SW), w_so_g,
        w_out_g.reshape(D, D), small)

    dw_in, dw_co, dw_out, dw_glu, dw_so, d_conv_w = big_grads
    d_conv_w = jnp.pad(d_conv_w, ((0, CONV_ROWS - KS), (0, 0))).reshape(CONV_ROWS, NCHIP, 256).transpose(1, 0, 2)
    grads = [dw_in] + [g.astype(bf16) for g in (dw_co.reshape(NCHIP, 256, D), dw_out.reshape(NCHIP, 256, D),
                                                  dw_glu.reshape(NCHIP, 128, SW), dw_so, d_conv_w)]
    gs = _pack_small(small_grads, extra=loss_part)
    *recvs, rs = _pair_exchange_list(grads, gs)
    *parts, ps = _pair_sum_list(c.astype(jnp.int32).reshape(1), grads, recvs, gs, rs)
    gxi, dg_pre, *qparts, qs = _x_grad_exchange(dproj, w_in_g, xi, gx0, pre_norm_gain, parts, ps)
    grad_x = _deinterleave(gxi)
    *halves, fs = _chip_sum_list(qparts, qs)
    g_big = list(_sibling_join_list(halves))
    g_big[5] = g_big[5][:KS]

    big_w = (w_in[0], w_conv_out[0], w_out[0], w_ssm_glu[0], w_ssm_out[0], conv_w[0])
    big_m = (m_w_in[0], m_w_conv_out[0], m_w_out[0], m_w_ssm_glu[0], m_w_ssm_out[0], m_conv_w[0])
    big_v = (v_w_in[0], v_w_conv_out[0], v_w_out[0], v_w_ssm_glu[0], v_w_ssm_out[0], v_conv_w[0])
    big_names = ("w_in", "w_conv_out", "w_out", "w_ssm_glu", "w_ssm_out", "conv_w")
    res = {}
    for n, w, g, m, v in zip(big_names, big_w, g_big, big_m, big_v):
        d, m2, v2 = _adamw("adamw_" + n, w, g, m, v)
        res[n] = (g[None], d[None], m2[None], v2[None])

    small_m = (m_conv_b, m_conv_ln_gain, m_conv_ln_bias, m_ssm_lambda_re, m_ssm_lambda_im, m_ssm_log_dt,
               m_ssm_b_re, m_ssm_b_im, m_ssm_c_re, m_ssm_c_im, m_ssm_d, m_b_ssm_glu, m_post_norm_gain)
    small_v = (v_conv_b, v_conv_ln_gain, v_conv_ln_bias, v_ssm_lambda_re, v_ssm_lambda_im, v_ssm_log_dt,
               v_ssm_b_re, v_ssm_b_im, v_ssm_c_re, v_ssm_c_im, v_ssm_d, v_b_ssm_glu, v_post_norm_gain)
    sd, sm, sv = _adamw("adamw_small", _pack_small(small), fs, _pack_small(small_m), _pack_small(small_v))
    sg_l, loss = _unpack_small(fs)
    sd_l, _ = _unpack_small(sd)
    sm_l, _ = _unpack_small(sm)
    sv_l, _ = _unpack_small(sv)
    for i, (n, _) in enumerate(_SMALL):
        res[n] = (sg_l[i], sd_l[i], sm_l[i], sv_l[i])
    rows = lambda a: a.reshape(8, 128)
    pre = _adamw_rows(_allgather_rows(rows(dg_pre)), rows(pre_norm_gain), rows(m_pre_norm_gain), rows(v_pre_norm_gain))
    res["pre_norm_gain"] = tuple(a.reshape(1, D) for a in pre)

    order = ("pre_norm_gain", "w_in", "conv_w", "conv_b", "conv_ln_gain", "conv_ln_bias", "w_conv_out", "ssm_lambda_re",
             "ssm_lambda_im", "ssm_log_dt", "ssm_b_re", "ssm_b_im", "ssm_c_re", "ssm_c_im", "ssm_d", "w_ssm_glu",
             "b_ssm_glu", "w_ssm_out", "w_out", "post_norm_gain")
    outs = [loss, grad_x[None]]
    for q in range(4):
        outs.extend(res[n][q] for n in order)
    return tuple(outs)
```

```python
import math

import numpy as np
import jax
import jax.numpy as jnp
from jax import lax
from jax.experimental import pallas as pl
from jax.experimental.pallas import tpu as pltpu

f32 = jnp.float32
bf16 = jnp.bfloat16

D = 1024
CW = 1024
SW = 512
G = 32
H = 16
PST = 64
NS = G * PST
KS = 31
IN_W = 6144
NCHIP = 4
SHARD_W = IN_W // NCHIP
RMS_EPS = 1e-6
LN_EPS = 1e-5
LR, B1, B2, EPS, WD, STEP = 0.001, 0.9, 0.999, 1e-08, 0.01, 10
GELU_K0 = math.sqrt(2.0 / math.pi)
GELU_K1 = 0.044715

TC = 512
R = TC // 8
NH = 32
LBW = 1024
REST_ROWS = 768
CONV_ROWS = 64
SMALL_ROWS = 1152
VMEM_LIMIT = 56 * 1024 * 1024
MESH = pl.DeviceIdType.MESH


def _cp(*sem):
    return pltpu.CompilerParams(dimension_semantics=tuple(sem), vmem_limit_bytes=VMEM_LIMIT)


def _sig(v):
    return 0.5 * jnp.tanh(0.5 * v) + 0.5


def _dot(a, b):
    return jnp.dot(a, b, preferred_element_type=f32)


def _dot_nt(a, b):
    return lax.dot_general(a, b, (((1,), (1,)), ((), ())), preferred_element_type=f32)


def _dot_tn(a, b):
    return lax.dot_general(a, b, (((0,), (0,)), ((), ())), preferred_element_type=f32)


def _full(shape):
    nd = len(shape)
    return pl.BlockSpec(shape, lambda *_: (0,) * nd)


def _rows8(i):
    return pl.ds(pl.multiple_of(i * 8, 8), 8)


def _prenorm(x, g_pre):
    L = x.shape[0]
    tm = 512
    def body(x_ref, g_ref, h_ref):
        xt = x_ref[...]
        r = lax.rsqrt(jnp.mean(xt * xt, axis=-1, keepdims=True) + RMS_EPS)
        h_ref[...] = (xt * r * g_ref[...]).astype(bf16)
    return pl.pallas_call(
        body, grid=(L // tm,),
        in_specs=[pl.BlockSpec((tm, D), lambda i: (i, 0)), _full((1, D))],
        out_specs=pl.BlockSpec((tm, D), lambda i: (i, 0)),
        out_shape=jax.ShapeDtypeStruct((L, D), bf16),
        name="prenorm", compiler_params=_cp("arbitrary"))(x, g_pre)


def _proj_fwd(k_arr, h, w_in, proj):
    L = h.shape[0]
    tm = min(1024, L)
    def body(_, h_ref, w_ref, __, o_ref):
        o_ref[...] = _dot(h_ref[...], w_ref[0]).astype(bf16)
    shard = lambda j, k: (k[0] + 1 + j) % NCHIP
    grid_spec = pltpu.PrefetchScalarGridSpec(
        num_scalar_prefetch=1, grid=(NCHIP - 1, L // tm),
        in_specs=[pl.BlockSpec((tm, D), lambda j, i, k: (i, 0)),
                  pl.BlockSpec((1, D, SHARD_W), lambda j, i, k: (shard(j, k), 0, 0)), _ANY],
        out_specs=pl.BlockSpec((tm, SHARD_W), lambda j, i, k: (i, shard(j, k))))
    return pl.pallas_call(
        body, grid_spec=grid_spec, out_shape=jax.ShapeDtypeStruct((L, IN_W), bf16),
        input_output_aliases={3: 0},
        name="proj_fwd", compiler_params=_cp("arbitrary", "arbitrary"))(k_arr, h, w_in, proj)


NLB = CW // 128
RPI = 4


def _put_blocked(buf, row0, nrows, v):
    for lb in range(NLB):
        buf[lb, pl.ds(row0, nrows), :] = v[:, lb * 128:(lb + 1) * 128]


def _get_blocked(buf, row0, nrows):
    return jnp.concatenate([buf[lb, pl.ds(row0, nrows), :] for lb in range(NLB)], axis=1)


def _fill_before(ebuf, prev):
    sub = lax.broadcasted_iota(jnp.int32, (8, 128), 0)
    def halo(p, carry):
        for lb in range(NLB):
            cur = ebuf[lb, _rows8(R + p), :]
            ebuf[lb, _rows8(p), :] = jnp.where(sub == 0, pltpu.roll(prev[lb, _rows8(p), :], 1, 0),
                                               pltpu.roll(cur, 1, 0))
        return carry
    lax.fori_loop(0, NH, halo, 0)


def _fir(buf, lb, r, coef, first, flip):
    win = buf[lb, pl.ds(pl.multiple_of(r * 8, 8), (KS + RPI - 1) * 8), :]
    outs = []
    for i in range(RPI):
        acc = [first, None, None, None]
        for k in range(KS):
            o = i + ((KS - 1 - k) if flip else k)
            t = coef[k] * win[8 * o:8 * o + 8, :]
            acc[k % 4] = t if acc[k % 4] is None else acc[k % 4] + t
        outs.append((acc[0] + acc[1]) + (acc[2] + acc[3]))
    return outs


def _conv_fwd(proj, cw, cbias, lng, lnb):
    L = proj.shape[0]
    nc = L // TC
    def body(ca_ref, cb_ref, zc_ref, w_ref, b_ref, g_ref, bb_ref, cu1_ref, ain_ref, ebuf, prev, cacc):
        @pl.when(pl.program_id(0) == 0)
        def _():
            prev[...] = jnp.zeros_like(prev)
        def glu(s, carry):
            rows = pl.ds(pl.multiple_of(s * 64, 64), 64)
            _put_blocked(ebuf, pl.multiple_of(NH * 8 + s * 64, 64), 64,
                         ca_ref[rows, :].astype(f32) * _sig(cb_ref[rows, :].astype(f32)))
            return carry
        lax.fori_loop(0, TC // 64, glu, 0)
        _fill_before(ebuf, prev)
        prev[...] = ebuf[:, R * 8:(NH + R) * 8, :]
        for lb in range(NLB):
            sl = slice(lb * 128, (lb + 1) * 128)
            wk = [jnp.broadcast_to(w_ref[k:k + 1, sl], (8, 128)) for k in range(KS)]
            bias = jnp.broadcast_to(b_ref[:, sl], (8, 128))
            def tap(q, carry, lb=lb, wk=wk, bias=bias):
                r = q * RPI
                for i, o in enumerate(_fir(ebuf, lb, r + (NH - KS + 1), wk, bias, False)):
                    cacc[lb, _rows8(r + i), :] = o
                return carry
            lax.fori_loop(0, R // RPI, tap, 0)
        def norm(s, carry):
            rows = pl.ds(pl.multiple_of(s * 64, 64), 64)
            c1b = _get_blocked(cacc, pl.multiple_of(s * 64, 64), 64).astype(bf16)
            cu1_ref[rows, :] = c1b
            c1 = c1b.astype(f32)
            xc = c1 - jnp.mean(c1, axis=-1, keepdims=True)
            var = jnp.mean(xc * xc, axis=-1, keepdims=True)
            ln = xc * lax.rsqrt(var + LN_EPS) * g_ref[...] + bb_ref[...]
            zc = zc_ref[rows, :].astype(f32)
            ain_ref[rows, :] = ((ln * _sig(ln)) * (zc * _sig(zc))).astype(bf16)
            return carry
        lax.fori_loop(0, TC // 64, norm, 0, unroll=2)

    col = lambda c: pl.BlockSpec((TC, CW), lambda i, c=c: (i, c))
    return pl.pallas_call(
        body, grid=(nc,),
        in_specs=[col(0), col(1), col(2), _full((32, CW)), _full((1, CW)), _full((1, CW)), _full((1, CW))],
        out_specs=[pl.BlockSpec((TC, CW), lambda i: (i, 0)), pl.BlockSpec((TC, CW), lambda i: (i, 0))],
        out_shape=[jax.ShapeDtypeStruct((L, CW), bf16), jax.ShapeDtypeStruct((L, CW), bf16)],
        scratch_shapes=[pltpu.VMEM((NLB, (NH + R) * 8, 128), f32), pltpu.VMEM((NLB, NH * 8, 128), f32),
                        pltpu.VMEM((NLB, TC, 128), f32)],
        name="conv_fwd", compiler_params=_cp("arbitrary"))(proj, proj, proj, cw, cbias, lng, lnb)


def _gelu_parts(y0):
    t = jnp.tanh(GELU_K0 * (y0 + GELU_K1 * y0 * y0 * y0))
    return t, 0.5 * y0 * (1.0 + t)


def _ssm_fwd(proj, bbt_re, bbt_im, ct_re, ct_im, a_re, a_im, apow_re, apow_im, dvec, wglu, bglu):
    L = proj.shape[0]
    nc = L // TC
    def body(u_ref, zs_ref, bre_ref, bim_ref, cre_ref, cim_ref, are_ref, aim_ref, pwr_ref, pwi_ref,
             d_ref, wg_ref, bg_ref, y0_ref, bin_ref, sre, sim, cinr, cini, prev_re, prev_im):
        c = pl.program_id(0)
        @pl.when(c == 0)
        def _():
            prev_re[...] = jnp.zeros_like(prev_re)
            prev_im[...] = jnp.zeros_like(prev_im)
        u = u_ref[...]
        for blk in range(4):
            ub = u[:, 128 * blk:128 * (blk + 1)]
            sre[:, 512 * blk:512 * (blk + 1)] = _dot(ub, bre_ref[blk])
            sim[:, 512 * blk:512 * (blk + 1)] = _dot(ub, bim_ref[blk])
        for lb in range(NS // LBW):
            sl = slice(lb * LBW, (lb + 1) * LBW)
            ar = jnp.broadcast_to(are_ref[:, sl], (8, LBW))
            ai = jnp.broadcast_to(aim_ref[:, sl], (8, LBW))
            def step(r, carry, sl=sl, ar=ar, ai=ai):
                sr, si = carry
                nr = ar * sr - ai * si + sre[_rows8(r), sl]
                ni = ar * si + ai * sr + sim[_rows8(r), sl]
                sre[_rows8(r), sl] = nr
                sim[_rows8(r), sl] = ni
                return nr, ni
            lax.fori_loop(1, R, step, (sre[0:8, sl], sim[0:8, sl]))
        a_r = pwr_ref[R - 1:R, :]
        a_i = pwi_ref[R - 1:R, :]
        cr = prev_re[0:1, :]
        ci = prev_im[0:1, :]
        for seg in range(8):
            cinr[seg:seg + 1, :] = cr
            cini[seg:seg + 1, :] = ci
            er = sre[8 * (R - 1) + seg:8 * (R - 1) + seg + 1, :]
            ei = sim[8 * (R - 1) + seg:8 * (R - 1) + seg + 1, :]
            cr, ci = er + a_r * cr - a_i * ci, ei + a_r * ci + a_i * cr
        prev_re[0:1, :] = cr
        prev_im[0:1, :] = ci
        for lb in range(NS // LBW):
            sl = slice(lb * LBW, (lb + 1) * LBW)
            kr = cinr[:, sl]
            ki = cini[:, sl]
            def fix(r, carry, sl=sl, kr=kr, ki=ki):
                pr = jnp.broadcast_to(pwr_ref[pl.ds(r, 1), sl], (8, LBW))
                pi = jnp.broadcast_to(pwi_ref[pl.ds(r, 1), sl], (8, LBW))
                sre[_rows8(r), sl] = sre[_rows8(r), sl] + pr * kr - pi * ki
                sim[_rows8(r), sl] = sim[_rows8(r), sl] + pr * ki + pi * kr
                return carry
            lax.fori_loop(0, R, fix, 0)
        yp = []
        for blk in range(4):
            sr = sre[:, 512 * blk:512 * (blk + 1)].astype(bf16)
            si = sim[:, 512 * blk:512 * (blk + 1)].astype(bf16)
            yp.append(_dot(sr, cre_ref[blk]) - _dot(si, cim_ref[blk]))
        y0 = jnp.concatenate(yp, axis=1) + d_ref[...] * u.astype(f32)
        y0_ref[...] = y0
        _, y1 = _gelu_parts(y0)
        glu = _dot(y1.astype(bf16), wg_ref[...]) + bg_ref[...]
        y2 = y1 * _sig(glu)
        zs = zs_ref[...].astype(f32)
        bin_ref[...] = (y2 * (zs * _sig(zs))).astype(bf16)

    return pl.pallas_call(
        body, grid=(nc,),
        in_specs=[pl.BlockSpec((TC, SW), lambda c: (c, 6)), pl.BlockSpec((TC, SW), lambda c: (c, 7)),
                  _full((4, 128, 512)), _full((4, 128, 512)), _full((4, 512, 128)), _full((4, 512, 128)),
                  _full((1, NS)), _full((1, NS)), _full((R, NS)), _full((R, NS)),
                  _full((1, SW)), _full((SW, SW)), _full((1, SW))],
        out_specs=[pl.BlockSpec((TC, SW), lambda c: (c, 0)), pl.BlockSpec((TC, SW), lambda c: (c, 0)),
                   pl.BlockSpec((TC, NS), lambda c: (c, 0)), pl.BlockSpec((TC, NS), lambda c: (c, 0)),
                   pl.BlockSpec((8, NS), lambda c: (c, 0)), pl.BlockSpec((8, NS), lambda c: (c, 0))],
        out_shape=[jax.ShapeDtypeStruct((L, SW), f32), jax.ShapeDtypeStruct((L, SW), bf16),
                   jax.ShapeDtypeStruct((L, NS), f32), jax.ShapeDtypeStruct((L, NS), f32),
                   jax.ShapeDtypeStruct((nc * 8, NS), f32), jax.ShapeDtypeStruct((nc * 8, NS), f32)],
        scratch_shapes=[pltpu.VMEM((8, NS), f32), pltpu.VMEM((8, NS), f32)],
        name="ssm_fwd", compiler_params=_cp("arbitrary"))(
            proj, proj, bbt_re, bbt_im, ct_re, ct_im, a_re, a_im, apow_re, apow_im, dvec, wglu, bglu)


def _tail(a_in, b_in, proj, x, tgt, wco, wso, wout, gpost):
    L = x.shape[0]
    tm = 256
    def body(a_ref, b_ref, gc_ref, gs_ref, x_ref, t_ref, wco_ref, wso_ref, wout_ref, gp_ref,
             gx_ref, dain_ref, dbin_ref, dp_ref, dwout_ref, dwco_ref, dwso_ref, dgp_ref, loss_ref):
        @pl.when(pl.program_id(0) == 0)
        def _():
            dwout_ref[...] = jnp.zeros_like(dwout_ref)
            dwco_ref[...] = jnp.zeros_like(dwco_ref)
            dwso_ref[...] = jnp.zeros_like(dwso_ref)
            dgp_ref[...] = jnp.zeros_like(dgp_ref)
            loss_ref[...] = jnp.zeros_like(loss_ref)
        a = a_ref[...]
        b = b_ref[...]
        co = _dot(a, wco_ref[...])
        so = jnp.concatenate([_dot(b, wso_ref[j]) for j in range(NCHIP)], axis=1)
        sc = _sig(gc_ref[...].astype(f32))
        ss = _sig(gs_ref[...].astype(f32))
        mb = (sc * co + ss * so).astype(bf16)
        out = _dot(mb, wout_ref[...])
        r2 = lax.rsqrt(jnp.mean(out * out, axis=-1, keepdims=True) + RMS_EPS)
        on = out * r2
        gp = gp_ref[...]
        e = x_ref[...] + on * gp - t_ref[...]
        loss_ref[...] += (0.5 / D) * jnp.sum(e * e)
        dy = e * (1.0 / D)
        gx_ref[...] = dy
        dgp_ref[...] += jnp.sum(dy * on, axis=0, keepdims=True)
        dn = dy * gp
        dout = (r2 * (dn - on * jnp.mean(dn * on, axis=-1, keepdims=True))).astype(bf16)
        dwout_ref[...] += _dot_tn(mb, dout)
        dm = _dot_nt(dout, wout_ref[...])
        dp_ref[:, 0:D] = (dm * co * sc * (1.0 - sc)).astype(bf16)
        dp_ref[:, D:2 * D] = (dm * so * ss * (1.0 - ss)).astype(bf16)
        dco = (dm * sc).astype(bf16)
        dso = (dm * ss).astype(bf16)
        dwco_ref[...] += _dot_tn(a, dco)
        dbin = None
        for j in range(NCHIP):
            dso_j = dso[:, j * 256:(j + 1) * 256]
            dwso_ref[j] += _dot_tn(b, dso_j)
            t = _dot_nt(dso_j, wso_ref[j])
            dbin = t if dbin is None else dbin + t
        dain_ref[...] = _dot_nt(dco, wco_ref[...]).astype(bf16)
        dbin_ref[...] = dbin.astype(bf16)

    row = lambda w: pl.BlockSpec((tm, w), lambda i: (i, 0))
    one = lambda shape: pl.BlockSpec(shape, lambda i: (0,) * len(shape), pipeline_mode=pl.Buffered(1))
    return pl.pallas_call(
        body, grid=(L // tm,),
        in_specs=[row(CW), row(SW), pl.BlockSpec((tm, D), lambda i: (i, 4)), pl.BlockSpec((tm, D), lambda i: (i, 5)),
                  row(D), row(D), one((CW, D)), one((NCHIP, SW, 256)), one((D, D)), one((1, D))],
        out_specs=[row(D), row(CW), row(SW), pl.BlockSpec((tm, 2 * D), lambda i: (i, 2)),
                   one((D, D)), one((CW, D)), one((NCHIP, SW, 256)), one((1, D)), one((1, 128))],
        out_shape=[jax.ShapeDtypeStruct((L, D), f32), jax.ShapeDtypeStruct((L, CW), bf16),
                   jax.ShapeDtypeStruct((L, SW), bf16), jax.ShapeDtypeStruct((L, IN_W), bf16),
                   jax.ShapeDtypeStruct((D, D), f32), jax.ShapeDtypeStruct((CW, D), f32),
                   jax.ShapeDtypeStruct((NCHIP, SW, 256), f32), jax.ShapeDtypeStruct((1, D), f32),
                   jax.ShapeDtypeStruct((1, 128), f32)],
        name="tail", compiler_params=_cp("arbitrary"))(a_in, b_in, proj, proj, x, tgt, wco, wso, wout, gpost)


def _ssm_bwd(d_bin, y0, proj, sre, sim, cinr, cini, bbt_re, bbt_im, ct_re, ct_im,
             a_re, a_im, apow_re, apow_im, dvec, wglu, bglu, dproj):
    L = y0.shape[0]
    nc = L // TC
    def body(dbin_ref, y0_ref, u_ref, zs_ref, sre_ref, sim_ref, cinr_ref, cini_ref,
             bre_ref, bim_ref, cre_ref, cim_ref, are_ref, aim_ref, pwr_ref, pwi_ref, d_ref, wg_ref, bg_ref, _,
             dp_ref, dbre_ref, dbim_ref, dcre_ref, dcim_ref, dd_ref, dar_ref, dai_ref, dwg_ref, dbg_ref,
             gre, gim, gcr, gci, nxt_re, nxt_im):
        @pl.when(pl.program_id(0) == 0)
        def _():
            for ref in (dbre_ref, dbim_ref, dcre_ref, dcim_ref, dd_ref, dar_ref, dai_ref, dwg_ref, dbg_ref,
                        nxt_re, nxt_im):
                ref[...] = jnp.zeros_like(ref)
        y0 = y0_ref[...]
        u = u_ref[...]
        zs = zs_ref[...].astype(f32)
        dbin = dbin_ref[...].astype(f32)
        t, y1 = _gelu_parts(y0)
        y1b = y1.astype(bf16)
        sg = _sig(_dot(y1b, wg_ref[...]) + bg_ref[...])
        sz = _sig(zs)
        d_y2 = dbin * (zs * sz)
        dp_ref[:, SW:2 * SW] = (dbin * (y1 * sg) * (sz * (1.0 + zs * (1.0 - sz)))).astype(bf16)
        d_glu = d_y2 * y1 * sg * (1.0 - sg)
        d_glub = d_glu.astype(bf16)
        d_y1 = d_y2 * sg + _dot_nt(d_glub, wg_ref[...])
        dwg_ref[...] += _dot_tn(y1b, d_glub)
        dbg_ref[...] += jnp.sum(d_glu, axis=0, keepdims=True)
        dgelu = 0.5 * (1.0 + t) + 0.5 * y0 * (1.0 - t * t) * GELU_K0 * (1.0 + 3.0 * GELU_K1 * y0 * y0)
        d_y0 = d_y1 * dgelu
        dd_ref[...] += jnp.sum(d_y0 * u.astype(f32), axis=0, keepdims=True)
        dyb = d_y0.astype(bf16)
        for blk in range(4):
            dy1 = dyb[:, 128 * blk:128 * (blk + 1)]
            gre[:, 512 * blk:512 * (blk + 1)] = _dot_nt(dy1, cre_ref[blk])
            gim[:, 512 * blk:512 * (blk + 1)] = -_dot_nt(dy1, cim_ref[blk])
        for lb in range(NS // LBW):
            sl = slice(lb * LBW, (lb + 1) * LBW)
            ar = jnp.broadcast_to(are_ref[:, sl], (8, LBW))
            ai = jnp.broadcast_to(aim_ref[:, sl], (8, LBW))
            def step(k, carry, sl=sl, ar=ar, ai=ai):
                gr, gi = carry
                row = _rows8(R - 2 - k)
                nr = ar * gr + ai * gi + gre[row, sl]
                ni = ar * gi - ai * gr + gim[row, sl]
                gre[row, sl] = nr
                gim[row, sl] = ni
                return nr, ni
            lax.fori_loop(0, R - 1, step, (gre[8 * (R - 1):8 * R, sl], gim[8 * (R - 1):8 * R, sl]))
        a_r = pwr_ref[R - 1:R, :]
        a_i = pwi_ref[R - 1:R, :]
        cr = nxt_re[0:1, :]
        ci = nxt_im[0:1, :]
        for seg in range(7, -1, -1):
            gcr[seg:seg + 1, :] = cr
            gci[seg:seg + 1, :] = ci
            er = gre[seg:seg + 1, :]
            ei = gim[seg:seg + 1, :]
            cr, ci = er + a_r * cr + a_i * ci, ei + a_r * ci - a_i * cr
        nxt_re[0:1, :] = cr
        nxt_im[0:1, :] = ci
        for lb in range(NS // LBW):
            sl = slice(lb * LBW, (lb + 1) * LBW)
            kr = gcr[:, sl]
            ki = gci[:, sl]
            def fixed(rows, prow, sl=sl, kr=kr, ki=ki):
                pr = jnp.broadcast_to(pwr_ref[prow, sl], (8, LBW))
                pi = jnp.broadcast_to(pwi_ref[prow, sl], (8, LBW))
                gr = gre[rows, sl] + pr * kr + pi * ki
                gi = gim[rows, sl] + pr * ki - pi * kr
                gre[rows, sl] = gr
                gim[rows, sl] = gi
                return gr, gi
            g0r, g0i = fixed(slice(0, 8), slice(R - 1, R))
            p0r, p0i = cinr_ref[:, sl], cini_ref[:, sl]
            acc0 = (g0r * p0r + g0i * p0i, g0i * p0r - g0r * p0i)
            def dacc(r, carry, sl=sl, fixed=fixed):
                xr, xi = carry
                gr, gi = fixed(_rows8(r), pl.ds(R - 1 - r, 1))
                pr, pi = sre_ref[_rows8(r - 1), sl], sim_ref[_rows8(r - 1), sl]
                return xr + gr * pr + gi * pi, xi + gi * pr - gr * pi
            xr, xi = lax.fori_loop(1, R, dacc, acc0)
            dar_ref[:, sl] += xr
            dai_ref[:, sl] += xi
        dup = []
        for blk in range(4):
            s4 = slice(512 * blk, 512 * (blk + 1))
            s1 = slice(128 * blk, 128 * (blk + 1))
            grb = gre[:, s4].astype(bf16)
            gib = gim[:, s4].astype(bf16)
            dup.append(_dot_nt(grb, bre_ref[blk]) + _dot_nt(gib, bim_ref[blk]))
            dbre_ref[blk] += _dot_tn(u[:, s1], grb)
            dbim_ref[blk] += _dot_tn(u[:, s1], gib)
            dcre_ref[blk] += _dot_tn(sre_ref[:, s4].astype(bf16), dyb[:, s1])
            dcim_ref[blk] -= _dot_tn(sim_ref[:, s4].astype(bf16), dyb[:, s1])
        dp_ref[:, 0:SW] = (jnp.concatenate(dup, axis=1) + d_ref[...] * d_y0).astype(bf16)

    rev = lambda w, cidx: pl.BlockSpec((TC, w), lambda i, cidx=cidx: (nc - 1 - i, cidx))
    one = lambda shape: pl.BlockSpec(shape, lambda i: (0,) * len(shape))
    return pl.pallas_call(
        body, grid=(nc,),
        in_specs=[rev(SW, 0), rev(SW, 0), rev(SW, 6), rev(SW, 7), rev(NS, 0), rev(NS, 0),
                  pl.BlockSpec((8, NS), lambda i: (nc - 1 - i, 0)), pl.BlockSpec((8, NS), lambda i: (nc - 1 - i, 0)),
                  one((4, 128, 512)), one((4, 128, 512)), one((4, 512, 128)), one((4, 512, 128)),
                  one((1, NS)), one((1, NS)), one((R, NS)), one((R, NS)),
                  one((1, SW)), one((SW, SW)), one((1, SW)), _ANY],
        out_specs=[pl.BlockSpec((TC, 2 * SW), lambda i: (nc - 1 - i, 3)),
                   one((4, 128, 512)), one((4, 128, 512)), one((4, 512, 128)), one((4, 512, 128)),
                   one((1, SW)), one((8, NS)), one((8, NS)), one((SW, SW)), one((1, SW))],
        out_shape=[jax.ShapeDtypeStruct((L, IN_W), bf16),
                   jax.ShapeDtypeStruct((4, 128, 512), f32), jax.ShapeDtypeStruct((4, 128, 512), f32),
                   jax.ShapeDtypeStruct((4, 512, 128), f32), jax.ShapeDtypeStruct((4, 512, 128), f32),
                   jax.ShapeDtypeStruct((1, SW), f32), jax.ShapeDtypeStruct((8, NS), f32),
                   jax.ShapeDtypeStruct((8, NS), f32), jax.ShapeDtypeStruct((SW, SW), f32),
                   jax.ShapeDtypeStruct((1, SW), f32)],
        scratch_shapes=[pltpu.VMEM((TC, NS), f32), pltpu.VMEM((TC, NS), f32), pltpu.VMEM((8, NS), f32),
                        pltpu.VMEM((8, NS), f32), pltpu.VMEM((8, NS), f32), pltpu.VMEM((8, NS), f32)],
        input_output_aliases={19: 0},
        name="ssm_bwd", compiler_params=_cp("arbitrary"))(
            d_bin, y0, proj, proj, sre, sim, cinr, cini, bbt_re, bbt_im, ct_re, ct_im,
            a_re, a_im, apow_re, apow_im, dvec, wglu, bglu, dproj)


def _conv_bwd(d_ain, cu1, proj, cw, lng, lnb, dproj):
    L = cu1.shape[0]
    nc = L // TC
    def body(dain_ref, cu1_ref, ca_ref, cb_ref, zc_ref, cah_ref, cbh_ref, w_ref, g_ref, bb_ref, _,
             dp_ref, dw_ref, dbias_ref, dlng_ref, dlnb_ref, dbuf, ebuf, prev, nxt, dcu0):
        i = pl.program_id(0)
        @pl.when(i == 0)
        def _():
            dw_ref[...] = jnp.zeros_like(dw_ref)
            dbias_ref[...] = jnp.zeros_like(dbias_ref)
            dlng_ref[...] = jnp.zeros_like(dlng_ref)
            dlnb_ref[...] = jnp.zeros_like(dlnb_ref)
            nxt[...] = jnp.zeros_like(nxt)
        def lnb(s, carry):
            rows = pl.ds(pl.multiple_of(s * 32, 32), 32)
            dain = dain_ref[rows, :].astype(f32)
            c1 = cu1_ref[rows, :].astype(f32)
            zc = zc_ref[rows, :].astype(f32)
            xc = c1 - jnp.mean(c1, axis=-1, keepdims=True)
            var = jnp.mean(xc * xc, axis=-1, keepdims=True)
            rstd = lax.rsqrt(var + LN_EPS)
            xh = xc * rstd
            ln = xh * g_ref[...] + bb_ref[...]
            sl_ = _sig(ln)
            sz = _sig(zc)
            dp_ref[rows, 2 * CW:3 * CW] = (dain * (ln * sl_) * (sz * (1.0 + zc * (1.0 - sz)))).astype(bf16)
            d_ln = dain * (zc * sz) * (sl_ * (1.0 + ln * (1.0 - sl_)))
            dlng_ref[...] += jnp.sum(d_ln * xh, axis=0, keepdims=True)
            dlnb_ref[...] += jnp.sum(d_ln, axis=0, keepdims=True)
            dxh = d_ln * g_ref[...]
            d_c1 = rstd * (dxh - jnp.mean(dxh, axis=-1, keepdims=True)
                           - xh * jnp.mean(dxh * xh, axis=-1, keepdims=True))
            dbias_ref[...] += jnp.sum(d_c1, axis=0, keepdims=True)
            _put_blocked(dbuf, pl.multiple_of(s * 32, 32), 32, d_c1)
            _put_blocked(ebuf, pl.multiple_of(NH * 8 + s * 32, 32), 32,
                         ca_ref[rows, :].astype(f32) * _sig(cb_ref[rows, :].astype(f32)))
            return carry
        lax.fori_loop(0, TC // 32, lnb, 0, unroll=2)
        sub = lax.broadcasted_iota(jnp.int32, (8, 128), 0)
        def after(p, carry):
            for lb in range(NLB):
                cur = dbuf[lb, _rows8(p), :]
                dbuf[lb, _rows8(R + p), :] = jnp.where(sub == 7, pltpu.roll(nxt[lb, _rows8(p), :], 7, 0),
                                                       pltpu.roll(cur, 7, 0))
            return carry
        lax.fori_loop(0, NH, after, 0)
        nxt[...] = dbuf[:, 0:NH * 8, :]
        def before(s, carry):
            rows = pl.ds(pl.multiple_of(s * 64, 64), 64)
            v = cah_ref[rows, :].astype(f32) * _sig(cbh_ref[rows, :].astype(f32))
            _put_blocked(prev, pl.multiple_of(s * 64, 64), 64, jnp.where(i == nc - 1, jnp.zeros_like(v), v))
            return carry
        lax.fori_loop(0, NH * 8 // 64, before, 0)
        _fill_before(ebuf, prev)
        for lb in range(NLB):
            sl = slice(lb * 128, (lb + 1) * 128)
            wk = [jnp.broadcast_to(w_ref[k:k + 1, sl], (8, 128)) for k in range(KS)]
            def tap(q, carry, lb=lb, wk=wk):
                r = q * RPI
                for j, o in enumerate(_fir(dbuf, lb, r, wk, None, True)):
                    dcu0[lb, _rows8(r + j), :] = o
                return carry
            lax.fori_loop(0, R // RPI, tap, 0)
            def wgrad(q, accs, lb=lb):
                r = q * RPI
                dvs = dbuf[lb, pl.ds(pl.multiple_of(r * 8, 8), RPI * 8), :]
                win = ebuf[lb, pl.ds(pl.multiple_of((r + (NH - KS + 1)) * 8, 8), (KS + RPI - 1) * 8), :]
                accs = list(accs)
                for j in range(RPI):
                    dv = dvs[8 * j:8 * j + 8, :]
                    for k in range(KS):
                        accs[k] = accs[k] + dv * win[8 * (j + k):8 * (j + k) + 8, :]
                return tuple(accs)
            accs = lax.fori_loop(0, R // RPI, wgrad, tuple(jnp.zeros((8, 128), f32) for _ in range(KS)))
            for k in range(KS):
                dw_ref[k, :, sl] += accs[k]
        def glub(s, carry):
            rows = pl.ds(pl.multiple_of(s * 64, 64), 64)
            d0 = _get_blocked(dcu0, pl.multiple_of(s * 64, 64), 64)
            ca = ca_ref[rows, :].astype(f32)
            sb = _sig(cb_ref[rows, :].astype(f32))
            dp_ref[rows, 0:CW] = (d0 * sb).astype(bf16)
            dp_ref[rows, CW:2 * CW] = (d0 * ca * sb * (1.0 - sb)).astype(bf16)
            return carry
        lax.fori_loop(0, TC // 64, glub, 0)

    hrows = NH * 8
    per = TC // hrows
    rev = lambda cidx: pl.BlockSpec((TC, CW), lambda i, cidx=cidx: (nc - 1 - i, cidx))
    halo = lambda cidx: pl.BlockSpec((hrows, CW), lambda i, cidx=cidx: (jnp.maximum((nc - 1 - i) * per - 1, 0), cidx))
    one = lambda shape: pl.BlockSpec(shape, lambda i: (0,) * len(shape))
    return pl.pallas_call(
        body, grid=(nc,),
        in_specs=[rev(0), rev(0), rev(0), rev(1), rev(2), halo(0), halo(1), one((32, CW)), one((1, CW)), one((1, CW)),
                  _ANY],
        out_specs=[pl.BlockSpec((TC, 3 * CW), lambda i: (nc - 1 - i, 0)), one((32, 8, CW)), one((1, CW)), one((1, CW)), one((1, CW))],
        out_shape=[jax.ShapeDtypeStruct((L, IN_W), bf16), jax.ShapeDtypeStruct((32, 8, CW), f32),
                   jax.ShapeDtypeStruct((1, CW), f32), jax.ShapeDtypeStruct((1, CW), f32),
                   jax.ShapeDtypeStruct((1, CW), f32)],
        scratch_shapes=[pltpu.VMEM((NLB, (R + NH) * 8, 128), f32), pltpu.VMEM((NLB, (NH + R) * 8, 128), f32),
                        pltpu.VMEM((NLB, hrows, 128), f32), pltpu.VMEM((NLB, hrows, 128), f32),
                        pltpu.VMEM((NLB, TC, 128), f32)],
        input_output_aliases={10: 0},
        name="conv_bwd", compiler_params=_cp("arbitrary"))(d_ain, cu1, proj, proj, proj, proj, proj, cw, lng, lnb, dproj)


def _win_grad(h, dproj):
    L = h.shape[0]
    tm = min(1024, L)
    nt = L // tm
    def body(h_ref, d_ref, o_ref, acc):
        i = pl.program_id(1)
        @pl.when(i == 0)
        def _():
            acc[...] = jnp.zeros_like(acc)
        acc[...] += _dot_tn(h_ref[...], d_ref[...])
        @pl.when(i == nt - 1)
        def _():
            o_ref[0] = acc[...].astype(bf16)
    return pl.pallas_call(
        body, grid=(NCHIP, nt),
        in_specs=[pl.BlockSpec((tm, D), lambda j, i: (i, 0)), pl.BlockSpec((tm, SHARD_W), lambda j, i: (i, j))],
        out_specs=pl.BlockSpec((1, D, SHARD_W), lambda j, i: (j, 0, 0)),
        out_shape=jax.ShapeDtypeStruct((NCHIP, D, SHARD_W), bf16),
        scratch_shapes=[pltpu.VMEM((D, SHARD_W), f32)],
        name="win_grad", compiler_params=_cp("arbitrary", "arbitrary"))(h, dproj)


def _x_grad(dproj, w_in, x, gx0, g_pre):
    L = x.shape[0]
    tm = 256
    def body(d_ref, w_ref, x_ref, gx_ref, g_ref, o_ref, dg_ref):
        @pl.when(pl.program_id(0) == 0)
        def _():
            dg_ref[...] = jnp.zeros_like(dg_ref)
        dh = _dot_nt(d_ref[:, 0:SHARD_W], w_ref[0])
        for j in range(1, NCHIP):
            dh = dh + _dot_nt(d_ref[:, j * SHARD_W:(j + 1) * SHARD_W], w_ref[j])
        xt = x_ref[...]
        r = lax.rsqrt(jnp.mean(xt * xt, axis=-1, keepdims=True) + RMS_EPS)
        xn = xt * r
        dg_ref[...] += jnp.sum(dh * xn, axis=0, keepdims=True)
        dxn = dh * g_ref[...]
        o_ref[...] = gx_ref[...] + r * (dxn - xn * jnp.mean(dxn * xn, axis=-1, keepdims=True))
    return pl.pallas_call(
        body, grid=(L // tm,),
        in_specs=[pl.BlockSpec((tm, IN_W), lambda i: (i, 0)),
                  pl.BlockSpec((NCHIP, D, SHARD_W), lambda i: (0, 0, 0), pipeline_mode=pl.Buffered(1)),
                  pl.BlockSpec((tm, D), lambda i: (i, 0)), pl.BlockSpec((tm, D), lambda i: (i, 0)), _full((1, D))],
        out_specs=[pl.BlockSpec((tm, D), lambda i: (i, 0)), _full((1, D))],
        out_shape=[jax.ShapeDtypeStruct((L, D), f32), jax.ShapeDtypeStruct((1, D), f32)],
        name="x_grad", compiler_params=_cp("arbitrary"))(dproj, w_in, x, gx0, g_pre)


def _pair_sum(c_arr, ga, ra, gb, rb, gs, rs):
    def body(c_ref, ga_ref, ra_ref, gb_ref, rb_ref, gs_ref, rs_ref, pa_ref, pb_ref, ps_ref):
        pa_ref[...] = (ga_ref[...] + ra_ref[...]).astype(bf16)
        pb_ref[...] = (gb_ref[...] + rb_ref[...]).astype(bf16)
        ps_ref[...] = gs_ref[...] + rs_ref[...]
    grid_spec = pltpu.PrefetchScalarGridSpec(
        num_scalar_prefetch=1, grid=(NCHIP,),
        in_specs=[pl.BlockSpec((1, D // 2, SHARD_W), lambda j, c: (j, c[0], 0)),
                  pl.BlockSpec((1, D // 2, SHARD_W), lambda j, c: (j, 0, 0)),
                  pl.BlockSpec((1, REST_ROWS // 2, 1024), lambda j, c: (j, c[0], 0)),
                  pl.BlockSpec((1, REST_ROWS // 2, 1024), lambda j, c: (j, 0, 0)),
                  pl.BlockSpec((SMALL_ROWS, 128), lambda j, c: (0, 0)),
                  pl.BlockSpec((SMALL_ROWS, 128), lambda j, c: (0, 0))],
        out_specs=[pl.BlockSpec((1, D // 2, SHARD_W), lambda j, c: (j, 0, 0)),
                   pl.BlockSpec((1, REST_ROWS // 2, 1024), lambda j, c: (j, 0, 0)),
                   pl.BlockSpec((SMALL_ROWS, 128), lambda j, c: (0, 0))])
    return pl.pallas_call(
        body, grid_spec=grid_spec,
        out_shape=[jax.ShapeDtypeStruct((NCHIP, D // 2, SHARD_W), bf16),
                   jax.ShapeDtypeStruct((NCHIP, REST_ROWS // 2, 1024), bf16),
                   jax.ShapeDtypeStruct((SMALL_ROWS, 128), f32)],
        name="pair_sum", compiler_params=_cp("arbitrary"))(c_arr, ga, ra, gb, rb, gs, rs)


def _chip_sum(qa, qb, qs):
    nt = 4
    def body(qa_ref, qb_ref, qs_ref, fa_ref, fb_ref, fs_ref):
        for q_ref, f_ref in ((qa_ref, fa_ref), (qb_ref, fb_ref), (qs_ref, fs_ref)):
            acc = q_ref[0].astype(f32)
            for j in range(1, NCHIP):
                acc = acc + q_ref[j].astype(f32)
            f_ref[...] = acc
    ra, rb, rs = D // 2 // nt, REST_ROWS // 2 // nt, SMALL_ROWS // nt
    return pl.pallas_call(
        body, grid=(nt,),
        in_specs=[pl.BlockSpec((NCHIP, ra, SHARD_W), lambda i: (0, i, 0)),
                  pl.BlockSpec((NCHIP, rb, 1024), lambda i: (0, i, 0)),
                  pl.BlockSpec((NCHIP, rs, 128), lambda i: (0, i, 0))],
        out_specs=[pl.BlockSpec((ra, SHARD_W), lambda i: (i, 0)), pl.BlockSpec((rb, 1024), lambda i: (i, 0)),
                   pl.BlockSpec((rs, 128), lambda i: (i, 0))],
        out_shape=[jax.ShapeDtypeStruct((D // 2, SHARD_W), f32), jax.ShapeDtypeStruct((REST_ROWS // 2, 1024), f32),
                   jax.ShapeDtypeStruct((SMALL_ROWS, 128), f32)],
        name="chip_sum", compiler_params=_cp("arbitrary"))(qa, qb, qs)


def _adamw_math(w, g, m, v):
    m2 = B1 * m + (1.0 - B1) * g
    v2 = B2 * v + (1.0 - B2) * (g * g)
    m_hat = m2 / (1.0 - B1 ** STEP)
    v_hat = v2 / (1.0 - B2 ** STEP)
    delta = -LR * (m_hat / (jnp.sqrt(v_hat) + EPS) + WD * w)
    return delta, m2, v2


def _adamw(name, w, g, m, v):
    rows, cols = w.shape
    tm = rows if rows <= 256 else (256 if rows % 256 == 0 else 128)
    assert rows % tm == 0
    def body(w_ref, g_ref, m_ref, v_ref, d_ref, m2_ref, v2_ref):
        d, m2, v2 = _adamw_math(w_ref[...], g_ref[...], m_ref[...], v_ref[...])
        d_ref[...] = d
        m2_ref[...] = m2
        v2_ref[...] = v2
    spec = pl.BlockSpec((tm, cols), lambda i: (i, 0))
    shp = jax.ShapeDtypeStruct((rows, cols), f32)
    return pl.pallas_call(
        body, grid=(rows // tm,), in_specs=[spec] * 4, out_specs=[spec] * 3, out_shape=[shp] * 3,
        name=name, compiler_params=_cp("arbitrary"))(w, g, m, v)


_ANY = pl.BlockSpec(memory_space=pl.ANY)


def _chunks(rows, parts):
    step = rows // parts
    assert step * parts == rows and step % 16 == 0
    return [(i * step, step) for i in range(parts)]


def _place():
    x, y, c = lax.axis_index("x"), lax.axis_index("y"), lax.axis_index("c")
    chips = [(1 - x, y), (x, 1 - y), (1 - x, 1 - y)]
    return x, y, c, chips


def _gather_weights(win_s, rest_s):
    segs = [(0, D // 2, r0, n) for r0, n in _chunks(D // 2, 4)] + \
           [(1, REST_ROWS // 2, r0, n) for r0, n in _chunks(REST_ROWS // 2, 2)]
    ns = len(segs)
    def body(a_ref, b_ref, oa_ref, ob_ref, send_sems, recv_sems, local_sems):
        x, y, c, chips = _place()
        k = 2 * x + y
        sibling = (x, y, 1 - c)
        ins, outs = (a_ref, b_ref), (oa_ref, ob_ref)

        def dst(which, half, chip, pc, r0, n):
            return outs[which].at[chip, pl.ds(pc * half + r0, n), :]

        def rcopy(i, src, dst_ref, to):
            return pltpu.make_async_remote_copy(src_ref=src, dst_ref=dst_ref, send_sem=send_sems.at[i],
                                                recv_sem=recv_sems.at[i], device_id=to, device_id_type=MESH)

        own = [pltpu.make_async_copy(ins[w], outs[w].at[k], local_sems.at[w]) for w in range(2)]
        for cp in own:
            cp.start()
        first = []
        for j, chip in enumerate(chips):
            for s, (w, half, r0, n) in enumerate(segs):
                first.append(rcopy(j * ns + s, ins[w].at[pl.ds(c * half + r0, n), :], dst(w, half, k, c, r0, n),
                                   (*chip, c)))
        for cp in first:
            cp.start()
        passed = []
        for j, chip in enumerate(chips):
            cj = 2 * chip[0] + chip[1]
            for s, (w, half, r0, n) in enumerate(segs):
                landed = dst(w, half, cj, c, r0, n)
                rcopy(j * ns + s, landed, landed, (x, y, c)).wait_recv()
                fwd = rcopy(3 * ns + j * ns + s, landed, landed, sibling)
                fwd.start()
                passed.append(fwd)
        for j, chip in enumerate(chips):
            cj = 2 * chip[0] + chip[1]
            for s, (w, half, r0, n) in enumerate(segs):
                theirs = dst(w, half, cj, 1 - c, r0, n)
                rcopy(3 * ns + j * ns + s, theirs, theirs, (x, y, c)).wait_recv()
        for cp in first + passed:
            cp.wait_send()
        for cp in own:
            cp.wait()

    return pl.pallas_call(
        body, in_specs=[_ANY, _ANY], out_specs=[_ANY, _ANY],
        out_shape=[jax.ShapeDtypeStruct((NCHIP, D, SHARD_W), bf16), jax.ShapeDtypeStruct((NCHIP, REST_ROWS, 1024), bf16)],
        scratch_shapes=[pltpu.SemaphoreType.DMA((6 * ns,)), pltpu.SemaphoreType.DMA((6 * ns,)),
                        pltpu.SemaphoreType.DMA((2,))],
        name="gather_weights")(win_s, rest_s)


def _pair_exchange(ga, gb, gs):
    ha, hb = D // 2, REST_ROWS // 2
    def body(a_ref, b_ref, s_ref, ra_ref, rb_ref, rs_ref, send_sems, recv_sems):
        x, y, c, _ = _place()
        sibling = (x, y, 1 - c)
        pieces = []
        for j in range(NCHIP):
            for r0, n in _chunks(ha, 4):
                pieces.append((a_ref.at[j, pl.ds((1 - c) * ha + r0, n), :], ra_ref.at[j, pl.ds(r0, n), :]))
            for r0, n in _chunks(hb, 2):
                pieces.append((b_ref.at[j, pl.ds((1 - c) * hb + r0, n), :], rb_ref.at[j, pl.ds(r0, n), :]))
        pieces.append((s_ref, rs_ref))
        copies = [pltpu.make_async_remote_copy(src_ref=s, dst_ref=d, send_sem=send_sems.at[i], recv_sem=recv_sems.at[i],
                                               device_id=sibling, device_id_type=MESH)
                  for i, (s, d) in enumerate(pieces)]
        for cp in copies:
            cp.start()
        for cp in copies:
            cp.wait_recv()
        for cp in copies:
            cp.wait_send()

    n = NCHIP * 6 + 1
    return pl.pallas_call(
        body, in_specs=[_ANY, _ANY, _ANY], out_specs=[_ANY, _ANY, _ANY],
        out_shape=[jax.ShapeDtypeStruct((NCHIP, ha, SHARD_W), f32), jax.ShapeDtypeStruct((NCHIP, hb, 1024), f32),
                   jax.ShapeDtypeStruct((SMALL_ROWS, 128), f32)],
        scratch_shapes=[pltpu.SemaphoreType.DMA((n,)), pltpu.SemaphoreType.DMA((n,))],
        name="pair_exchange")(ga, gb, gs)


def _chip_exchange(pa, pb, ps):
    ha, hb = D // 2, REST_ROWS // 2
    def body(a_ref, b_ref, s_ref, qa_ref, qb_ref, qs_ref, send_sems, recv_sems, local_sems):
        x, y, c, chips = _place()
        k = 2 * x + y
        own = [pltpu.make_async_copy(a_ref.at[k], qa_ref.at[k], local_sems.at[0]),
               pltpu.make_async_copy(b_ref.at[k], qb_ref.at[k], local_sems.at[1]),
               pltpu.make_async_copy(s_ref, qs_ref.at[k], local_sems.at[2])]
        for cp in own:
            cp.start()
        copies = []
        for j, chip in enumerate(chips):
            cj = 2 * chip[0] + chip[1]
            pieces = [(a_ref.at[cj, pl.ds(r0, n), :], qa_ref.at[k, pl.ds(r0, n), :]) for r0, n in _chunks(ha, 2)]
            pieces += [(b_ref.at[cj], qb_ref.at[k]), (s_ref, qs_ref.at[k])]
            for s, (src, dst_ref) in enumerate(pieces):
                copies.append(pltpu.make_async_remote_copy(
                    src_ref=src, dst_ref=dst_ref, send_sem=send_sems.at[4 * j + s], recv_sem=recv_sems.at[4 * j + s],
                    device_id=(*chip, c), device_id_type=MESH))
        for cp in copies:
            cp.start()
        for cp in copies:
            cp.wait_recv()
        for cp in copies:
            cp.wait_send()
        for cp in own:
            cp.wait()

    return pl.pallas_call(
        body, in_specs=[_ANY, _ANY, _ANY], out_specs=[_ANY, _ANY, _ANY],
        out_shape=[jax.ShapeDtypeStruct((NCHIP, ha, SHARD_W), bf16), jax.ShapeDtypeStruct((NCHIP, hb, 1024), bf16),
                   jax.ShapeDtypeStruct((NCHIP, SMALL_ROWS, 128), f32)],
        scratch_shapes=[pltpu.SemaphoreType.DMA((12,)), pltpu.SemaphoreType.DMA((12,)), pltpu.SemaphoreType.DMA((3,))],
        name="chip_exchange")(pa, pb, ps)


def _sibling_exchange(fa, fb):
    ha, hb = D // 2, REST_ROWS // 2
    def body(a_ref, b_ref, oa_ref, ob_ref, send_sems, recv_sems, local_sems):
        x, y, c, _ = _place()
        own = [pltpu.make_async_copy(a_ref, oa_ref.at[c], local_sems.at[0]),
               pltpu.make_async_copy(b_ref, ob_ref.at[c], local_sems.at[1])]
        for cp in own:
            cp.start()
        pieces = [(a_ref.at[pl.ds(r0, n), :], oa_ref.at[c, pl.ds(r0, n), :]) for r0, n in _chunks(ha, 4)]
        pieces += [(b_ref.at[pl.ds(r0, n), :], ob_ref.at[c, pl.ds(r0, n), :]) for r0, n in _chunks(hb, 2)]
        copies = [pltpu.make_async_remote_copy(src_ref=s, dst_ref=d, send_sem=send_sems.at[i], recv_sem=recv_sems.at[i],
                                               device_id=(x, y, 1 - c), device_id_type=MESH)
                  for i, (s, d) in enumerate(pieces)]
        for cp in copies:
            cp.start()
        for cp in copies:
            cp.wait_recv()
        for cp in copies:
            cp.wait_send()
        for cp in own:
            cp.wait()

    return pl.pallas_call(
        body, in_specs=[_ANY, _ANY], out_specs=[_ANY, _ANY],
        out_shape=[jax.ShapeDtypeStruct((2, ha, SHARD_W), f32), jax.ShapeDtypeStruct((2, hb, 1024), f32)],
        scratch_shapes=[pltpu.SemaphoreType.DMA((6,)), pltpu.SemaphoreType.DMA((6,)), pltpu.SemaphoreType.DMA((2,))],
        name="sibling_exchange")(fa, fb)


def _nchunks(half, cols, itemsize):
    return 4 if half * cols * itemsize >= (1 << 20) else 1


def _segments(metas):
    segs = []
    for w, (half, cols, dt) in enumerate(metas):
        for r0, n in _chunks(half, _nchunks(half, cols, jnp.dtype(dt).itemsize)):
            segs.append((w, half, r0, n))
    return segs


def _rcopy(i, src, dst, send_sems, recv_sems, to):
    return pltpu.make_async_remote_copy(src_ref=src, dst_ref=dst, send_sem=send_sems.at[i], recv_sem=recv_sems.at[i],
                                        device_id=to, device_id_type=MESH)


def _gather_list(shards):
    na = len(shards)
    segs = _segments([(a.shape[0] // 2, a.shape[1], a.dtype) for a in shards])
    ns = len(segs)
    def body(*refs):
        ins, outs, (send_sems, recv_sems) = refs[:na], refs[na:2 * na], refs[2 * na:]
        x, y, c, chips = _place()
        k = 2 * x + y
        me, sibling = (x, y, c), (x, y, 1 - c)

        def dst(w, half, chip, pc, r0, n):
            return outs[w].at[chip, pl.ds(pc * half + r0, n), :]

        first = []
        for j, chip in enumerate(chips):
            for s, (w, half, r0, n) in enumerate(segs):
                first.append(_rcopy(j * ns + s, ins[w].at[pl.ds(c * half + r0, n), :], dst(w, half, k, c, r0, n),
                                    send_sems, recv_sems, (*chip, c)))
        for cp in first:
            cp.start()
        passed = []
        for j, chip in enumerate(chips):
            cj = 2 * chip[0] + chip[1]
            for s, (w, half, r0, n) in enumerate(segs):
                landed = dst(w, half, cj, c, r0, n)
                _rcopy(j * ns + s, landed, landed, send_sems, recv_sems, me).wait_recv()
                fwd = _rcopy(3 * ns + j * ns + s, landed, landed, send_sems, recv_sems, sibling)
                fwd.start()
                passed.append(fwd)
        for j, chip in enumerate(chips):
            cj = 2 * chip[0] + chip[1]
            for s, (w, half, r0, n) in enumerate(segs):
                theirs = dst(w, half, cj, 1 - c, r0, n)
                _rcopy(3 * ns + j * ns + s, theirs, theirs, send_sems, recv_sems, me).wait_recv()
        for cp in first + passed:
            cp.wait_send()

    return pl.pallas_call(
        body, in_specs=[_ANY] * na, out_specs=[_ANY] * na,
        out_shape=[jax.ShapeDtypeStruct((NCHIP,) + a.shape, a.dtype) for a in shards],
        scratch_shapes=[pltpu.SemaphoreType.DMA((6 * ns,)), pltpu.SemaphoreType.DMA((6 * ns,))],
        name="gather_weights")(*shards)


def _gather_prep(k_arr, shards, x, tgt, g_pre, perm):
    na = len(shards)
    L = x.shape[0]
    nc = L // TC
    segs = _segments([(a.shape[0] // 2, a.shape[1], a.dtype) for a in shards])
    ns = len(segs)
    def body(_, *refs):
        ins = refs[:na]
        x_ref, t_ref, g_ref, p_ref = refs[na:na + 4]
        outs = refs[na + 4:2 * na + 4]
        h_ref, xi_ref, ti_ref, proj_ref = refs[2 * na + 4:2 * na + 8]
        stages = refs[2 * na + 8:3 * na + 8]
        send_sems, recv_sems, local_sems = refs[3 * na + 8:]
        i = pl.program_id(0)
        x, y, c, chips = _place()
        k = 2 * x + y
        me, sibling = (x, y, c), (x, y, 1 - c)

        def dst(w, half, chip, pc, r0, n):
            return outs[w].at[chip, pl.ds(pc * half + r0, n), :]

        def firsts():
            return [_rcopy(j * ns + s, ins[w].at[pl.ds(c * half + r0, n), :], dst(w, half, k, c, r0, n),
                           send_sems, recv_sems, (*chip, c))
                    for j, chip in enumerate(chips) for s, (w, half, r0, n) in enumerate(segs)]

        def own_out(w):
            return pltpu.make_async_copy(stages[w], outs[w].at[k], local_sems.at[w])

        @pl.when(i == 0)
        def _():
            for cp in firsts():
                cp.start()
            for w in range(na):
                cin = pltpu.make_async_copy(ins[w], stages[w], local_sems.at[w])
                cin.start()
                cin.wait()
            for w in range(na):
                own_out(w).start()

        p = p_ref[...]
        def through(v):
            hi = v.astype(bf16)
            r1 = v - hi.astype(f32)
            mid = r1.astype(bf16)
            lo = (r1 - mid.astype(f32)).astype(bf16)
            return (_dot(p, hi) + _dot(p, mid)) + _dot(p, lo)
        xt = x_ref[...]
        r = lax.rsqrt(jnp.mean(xt * xt, axis=-1, keepdims=True) + RMS_EPS)
        hp = _dot(p, (xt * r * g_ref[...]).astype(bf16)).astype(bf16)
        h_ref[...] = hp
        proj_ref[...] = _dot(hp, stages[0][...]).astype(bf16)
        xi_ref[...] = through(xt)
        ti_ref[...] = through(t_ref[...])

        @pl.when(i == nc - 1)
        def _():
            passed = []
            for j, chip in enumerate(chips):
                cj = 2 * chip[0] + chip[1]
                for s, (w, half, r0, n) in enumerate(segs):
                    landed = dst(w, half, cj, c, r0, n)
                    _rcopy(j * ns + s, landed, landed, send_sems, recv_sems, me).wait_recv()
                    fwd = _rcopy(3 * ns + j * ns + s, landed, landed, send_sems, recv_sems, sibling)
                    fwd.start()
                    passed.append(fwd)
            for j, chip in enumerate(chips):
                cj = 2 * chip[0] + chip[1]
                for s, (w, half, r0, n) in enumerate(segs):
                    theirs = dst(w, half, cj, 1 - c, r0, n)
                    _rcopy(3 * ns + j * ns + s, theirs, theirs, send_sems, recv_sems, me).wait_recv()
            for cp in firsts() + passed:
                cp.wait_send()
            for w in range(na):
                own_out(w).wait()

    row = lambda: pl.BlockSpec((TC, D), lambda i, k: (i, 0))
    grid_spec = pltpu.PrefetchScalarGridSpec(
        num_scalar_prefetch=1, grid=(nc,),
        in_specs=[_ANY] * na + [row(), row(), pl.BlockSpec((1, D), lambda i, k: (0, 0)),
                                pl.BlockSpec((TC, TC), lambda i, k: (0, 0))],
        out_specs=[_ANY] * na + [row(), row(), row(), pl.BlockSpec((TC, SHARD_W), lambda i, k: (i, k[0]))],
        scratch_shapes=[pltpu.VMEM(a.shape, a.dtype) for a in shards]
        + [pltpu.SemaphoreType.DMA((6 * ns,)), pltpu.SemaphoreType.DMA((6 * ns,)), pltpu.SemaphoreType.DMA((na,))])
    return pl.pallas_call(
        body, grid_spec=grid_spec,
        out_shape=[jax.ShapeDtypeStruct((NCHIP,) + a.shape, a.dtype) for a in shards]
        + [jax.ShapeDtypeStruct((L, D), bf16), jax.ShapeDtypeStruct((L, D), f32), jax.ShapeDtypeStruct((L, D), f32),
           jax.ShapeDtypeStruct((L, IN_W), bf16)],
        name="gather_prep", compiler_params=_cp("arbitrary"))(k_arr, *shards, x, tgt, g_pre, perm)


def _x_grad_exchange(dproj, w_in, x, gx0, g_pre, parts, small):
    L = x.shape[0]
    tm = 512
    nt = L // tm
    na = len(parts)
    segs = _segments([(p.shape[1], p.shape[2], p.dtype) for p in parts])
    ns = len(segs) + 1
    def body(*refs):
        d_ref, w_ref, x_ref, gx_ref, g_ref = refs[:5]
        ins, s_ref = refs[5:5 + na], refs[5 + na]
        o_ref, dg_ref = refs[6 + na:8 + na]
        outs, qs_ref = refs[8 + na:8 + 2 * na], refs[8 + 2 * na]
        stages = refs[9 + 2 * na:10 + 3 * na]
        send_sems, recv_sems, local_sems = refs[10 + 3 * na:]
        i = pl.program_id(0)
        x, y, c, chips = _place()
        k = 2 * x + y

        def copies():
            out = []
            for j, chip in enumerate(chips):
                cj = 2 * chip[0] + chip[1]
                pieces = [(s_ref, qs_ref.at[k])]
                pieces += [(ins[w].at[cj, pl.ds(r0, n), :], outs[w].at[k, pl.ds(r0, n), :]) for w, _, r0, n in segs]
                out += [_rcopy(ns * j + s, src, d, send_sems, recv_sems, (*chip, c)) for s, (src, d) in enumerate(pieces)]
            return out

        def own_out(w):
            dst = qs_ref.at[k] if w == na else outs[w].at[k]
            return pltpu.make_async_copy(stages[w], dst, local_sems.at[w])

        @pl.when(i == 0)
        def _():
            dg_ref[...] = jnp.zeros_like(dg_ref)
            for cp in copies():
                cp.start()
            for w in range(na + 1):
                cin = pltpu.make_async_copy(s_ref if w == na else ins[w].at[k], stages[w], local_sems.at[w])
                cin.start()
                cin.wait()
            for w in range(na + 1):
                own_out(w).start()

        dh = _dot_nt(d_ref[:, 0:SHARD_W], w_ref[0])
        for j in range(1, NCHIP):
            dh = dh + _dot_nt(d_ref[:, j * SHARD_W:(j + 1) * SHARD_W], w_ref[j])
        xt = x_ref[...]
        r = lax.rsqrt(jnp.mean(xt * xt, axis=-1, keepdims=True) + RMS_EPS)
        xn = xt * r
        dg_ref[...] += jnp.sum(dh * xn, axis=0, keepdims=True)
        dxn = dh * g_ref[...]
        o_ref[...] = gx_ref[...] + r * (dxn - xn * jnp.mean(dxn * xn, axis=-1, keepdims=True))

        @pl.when(i == nt - 1)
        def _():
            for cp in copies():
                cp.wait_recv()
            for cp in copies():
                cp.wait_send()
            for w in range(na + 1):
                own_out(w).wait()

    return pl.pallas_call(
        body, grid=(nt,),
        in_specs=[pl.BlockSpec((tm, IN_W), lambda i: (i, 0)),
                  pl.BlockSpec((NCHIP, D, SHARD_W), lambda i: (0, 0, 0), pipeline_mode=pl.Buffered(1)),
                  pl.BlockSpec((tm, D), lambda i: (i, 0)), pl.BlockSpec((tm, D), lambda i: (i, 0)), _full((1, D))]
        + [_ANY] * (na + 1),
        out_specs=[pl.BlockSpec((tm, D), lambda i: (i, 0)), _full((1, D))] + [_ANY] * (na + 1),
        out_shape=[jax.ShapeDtypeStruct((L, D), f32), jax.ShapeDtypeStruct((1, D), f32)]
        + [jax.ShapeDtypeStruct(p.shape, bf16) for p in parts] + [jax.ShapeDtypeStruct((NCHIP, SMALL_ROWS, 128), f32)],
        scratch_shapes=[pltpu.VMEM(p.shape[1:], bf16) for p in parts] + [pltpu.VMEM((SMALL_ROWS, 128), f32)]
        + [pltpu.SemaphoreType.DMA((3 * ns,)), pltpu.SemaphoreType.DMA((3 * ns,)), pltpu.SemaphoreType.DMA((na + 1,))],
        name="x_grad_exchange", compiler_params=_cp("arbitrary"))(dproj, w_in, x, gx0, g_pre, *parts, small)


def _sibling_join_list(halves):
    na = len(halves)
    segs = _segments([(h.shape[0], h.shape[1], h.dtype) for h in halves])
    def body(*refs):
        ins, outs, stages = refs[:na], refs[na:2 * na], refs[2 * na:3 * na]
        send_sems, recv_sems, local_sems = refs[3 * na:]
        x, y, c, _ = _place()
        copies = [_rcopy(i, ins[w].at[pl.ds(r0, n), :], outs[w].at[pl.ds(c * half + r0, n), :], send_sems, recv_sems,
                         (x, y, 1 - c)) for i, (w, half, r0, n) in enumerate(segs)]
        for cp in copies:
            cp.start()
        own = []
        for w in range(na):
            cin = pltpu.make_async_copy(ins[w], stages[w], local_sems.at[w])
            cin.start()
            cin.wait()
            half = halves[w].shape[0]
            own.append(pltpu.make_async_copy(stages[w], outs[w].at[pl.ds(c * half, half), :], local_sems.at[w]))
            own[-1].start()
        for cp in copies:
            cp.wait_recv()
        for cp in copies:
            cp.wait_send()
        for cp in own:
            cp.wait()

    return pl.pallas_call(
        body, in_specs=[_ANY] * na, out_specs=[_ANY] * na,
        out_shape=[jax.ShapeDtypeStruct((2 * h.shape[0], h.shape[1]), f32) for h in halves],
        scratch_shapes=[pltpu.VMEM(h.shape, f32) for h in halves]
        + [pltpu.SemaphoreType.DMA((len(segs),)), pltpu.SemaphoreType.DMA((len(segs),)), pltpu.SemaphoreType.DMA((na,))],
        name="sibling_join")(*halves)


def _allgather_rows(v):
    def body(v_ref, o_ref, send_sems, recv_sems):
        x, y, c, _ = _place()
        me = 4 * x + 2 * y + c
        o_ref[me] = v_ref[...]
        copies = []
        i = 0
        for dx in range(2):
            for dy in range(2):
                for dc in range(2):
                    if dx + dy + dc:
                        copies.append(_rcopy(i, v_ref, o_ref.at[me], send_sems, recv_sems, (x ^ dx, y ^ dy, c ^ dc)))
                        i += 1
        for cp in copies:
            cp.start()
        for cp in copies:
            cp.wait_recv()
        for cp in copies:
            cp.wait_send()

    vm = pl.BlockSpec(memory_space=pltpu.VMEM)
    return pl.pallas_call(
        body, in_specs=[vm], out_specs=vm, out_shape=jax.ShapeDtypeStruct((8, 8, 128), f32),
        scratch_shapes=[pltpu.SemaphoreType.DMA((7,)), pltpu.SemaphoreType.DMA((7,))],
        name="allgather_rows")(v)


def _adamw_rows(parts, w, m, v):
    def body(p_ref, w_ref, m_ref, v_ref, g_ref, d_ref, m2_ref, v2_ref):
        g = p_ref[0]
        for dvc in range(1, 8):
            g = g + p_ref[dvc]
        g_ref[...] = g
        d, m2, v2 = _adamw_math(w_ref[...], g, m_ref[...], v_ref[...])
        d_ref[...] = d
        m2_ref[...] = m2
        v2_ref[...] = v2
    return pl.pallas_call(body, out_shape=[jax.ShapeDtypeStruct((8, 128), f32)] * 4, name="adamw_pre_norm_gain")(
        parts, w, m, v)


def _pair_exchange_list(grads, small):
    na = len(grads)
    segs = _segments([(g.shape[1] // 2, g.shape[2], g.dtype) for g in grads])
    n = NCHIP * len(segs) + 1
    def body(*refs):
        ins, s_ref, outs, rs_ref, (send_sems, recv_sems) = (refs[:na], refs[na], refs[na + 1:2 * na + 1],
                                                            refs[2 * na + 1], refs[2 * na + 2:])
        x, y, c, _ = _place()
        pieces = [(s_ref, rs_ref)]
        for j in range(NCHIP):
            for w, half, r0, rows in segs:
                pieces.append((ins[w].at[j, pl.ds((1 - c) * half + r0, rows), :], outs[w].at[j, pl.ds(r0, rows), :]))
        copies = [_rcopy(i, s, d, send_sems, recv_sems, (x, y, 1 - c)) for i, (s, d) in enumerate(pieces)]
        for cp in copies:
            cp.start()
        for cp in copies:
            cp.wait_recv()
        for cp in copies:
            cp.wait_send()

    return pl.pallas_call(
        body, in_specs=[_ANY] * (na + 1), out_specs=[_ANY] * (na + 1),
        out_shape=[jax.ShapeDtypeStruct((NCHIP, g.shape[1] // 2, g.shape[2]), g.dtype) for g in grads]
        + [jax.ShapeDtypeStruct((SMALL_ROWS, 128), f32)],
        scratch_shapes=[pltpu.SemaphoreType.DMA((n,)), pltpu.SemaphoreType.DMA((n,))],
        name="pair_exchange")(*grads, small)


def _pair_sum_list(c_arr, grads, recvs, small, rsmall):
    na = len(grads)
    def body(c_ref, *refs):
        g_refs, r_refs, s_ref, rs_ref = refs[:na], refs[na:2 * na], refs[2 * na], refs[2 * na + 1]
        o_refs, os_ref = refs[2 * na + 2:3 * na + 2], refs[3 * na + 2]
        for g_ref, r_ref, o_ref in zip(g_refs, r_refs, o_refs):
            o_ref[...] = (g_ref[...].astype(f32) + r_ref[...].astype(f32)).astype(bf16)
        os_ref[...] = s_ref[...] + rs_ref[...]
    half = lambda g: pl.BlockSpec((1, g.shape[1] // 2, g.shape[2]), lambda j, c: (j, c[0], 0))
    low = lambda g: pl.BlockSpec((1, g.shape[1] // 2, g.shape[2]), lambda j, c: (j, 0, 0))
    sm = pl.BlockSpec((SMALL_ROWS, 128), lambda j, c: (0, 0))
    grid_spec = pltpu.PrefetchScalarGridSpec(
        num_scalar_prefetch=1, grid=(NCHIP,),
        in_specs=[half(g) for g in grads] + [low(g) for g in grads] + [sm, sm],
        out_specs=[low(g) for g in grads] + [sm])
    return pl.pallas_call(
        body, grid_spec=grid_spec,
        out_shape=[jax.ShapeDtypeStruct((NCHIP, g.shape[1] // 2, g.shape[2]), bf16) for g in grads]
        + [jax.ShapeDtypeStruct((SMALL_ROWS, 128), f32)],
        name="pair_sum", compiler_params=_cp("arbitrary"))(c_arr, *grads, *recvs, small, rsmall)


def _chip_exchange_list(parts, small):
    na = len(parts)
    segs = _segments([(p.shape[1], p.shape[2], p.dtype) for p in parts])
    ns = len(segs) + 1
    def body(*refs):
        ins, s_ref, outs, qs_ref, (send_sems, recv_sems) = (refs[:na], refs[na], refs[na + 1:2 * na + 1],
                                                            refs[2 * na + 1], refs[2 * na + 2:])
        x, y, c, chips = _place()
        k = 2 * x + y
        copies = []
        for j, chip in enumerate(chips):
            cj = 2 * chip[0] + chip[1]
            pieces = [(s_ref, qs_ref.at[k])]
            pieces += [(ins[w].at[cj, pl.ds(r0, n), :], outs[w].at[k, pl.ds(r0, n), :]) for w, _, r0, n in segs]
            copies += [_rcopy(ns * j + s, src, d, send_sems, recv_sems, (*chip, c)) for s, (src, d) in enumerate(pieces)]
        for cp in copies:
            cp.start()
        for cp in copies:
            cp.wait_recv()
        for cp in copies:
            cp.wait_send()

    return pl.pallas_call(
        body, in_specs=[_ANY] * (na + 1), out_specs=[_ANY] * (na + 1),
        out_shape=[jax.ShapeDtypeStruct(p.shape, bf16) for p in parts]
        + [jax.ShapeDtypeStruct((NCHIP, SMALL_ROWS, 128), f32)],
        scratch_shapes=[pltpu.SemaphoreType.DMA((3 * ns,)), pltpu.SemaphoreType.DMA((3 * ns,))],
        name="chip_exchange")(*parts, small)


def _chip_sum_list(parts, small):
    na = len(parts)
    nt = 2
    def body(*refs):
        for q_ref, f_ref in zip(refs[:na + 1], refs[na + 1:]):
            acc = q_ref[0].astype(f32)
            for j in range(1, NCHIP):
                acc = acc + q_ref[j].astype(f32)
            f_ref[...] = acc
    arrs = list(parts) + [small]
    return pl.pallas_call(
        body, grid=(nt,),
        in_specs=[pl.BlockSpec((NCHIP, a.shape[1] // nt, a.shape[2]), lambda i: (0, i, 0)) for a in arrs],
        out_specs=[pl.BlockSpec((a.shape[1] // nt, a.shape[2]), lambda i: (i, 0)) for a in arrs],
        out_shape=[jax.ShapeDtypeStruct(a.shape[1:], f32) for a in arrs],
        name="chip_sum", compiler_params=_cp("arbitrary"))(*arrs)


def _sibling_exchange_list(halves):
    na = len(halves)
    segs = _segments([(h.shape[0], h.shape[1], h.dtype) for h in halves])
    def body(*refs):
        ins, outs, (send_sems, recv_sems) = refs[:na], refs[na:2 * na], refs[2 * na:]
        x, y, c, _ = _place()
        copies = [_rcopy(i, ins[w].at[pl.ds(r0, n), :], outs[w].at[pl.ds(r0, n), :], send_sems, recv_sems, (x, y, 1 - c))
                  for i, (w, _, r0, n) in enumerate(segs)]
        for cp in copies:
            cp.start()
        for cp in copies:
            cp.wait_recv()
        for cp in copies:
            cp.wait_send()

    return pl.pallas_call(
        body, in_specs=[_ANY] * na, out_specs=[_ANY] * na,
        out_shape=[jax.ShapeDtypeStruct(h.shape, f32) for h in halves],
        scratch_shapes=[pltpu.SemaphoreType.DMA((len(segs),)), pltpu.SemaphoreType.DMA((len(segs),))],
        name="sibling_exchange")(*halves)


_REST_ROWS = (256, 256, 64, 128)
_CONV_PAD = 8192


def _pack_rest(mats, conv_rows, dtype):
    parts = [mats[0], mats[1], mats[2].reshape(64, 1024), mats[3].reshape(128, 1024)]
    parts = [p.astype(dtype) for p in parts] + [conv_rows]
    used = sum(p.shape[0] for p in parts)
    parts.append(jnp.zeros((REST_ROWS - used, 1024), dtype))
    return jnp.concatenate(parts, axis=0)


def _pack_rest_weights(mats, conv_w_s):
    flat = jnp.pad(conv_w_s.reshape(-1), (0, _CONV_PAD - KS * 256))
    return _pack_rest(mats, lax.bitcast_convert_type(flat, bf16).reshape(16, 1024), bf16)


def _pack_rest_grads(mats, conv_w_s):
    flat = jnp.pad(conv_w_s.reshape(-1), (0, _CONV_PAD - KS * 256))
    return _pack_rest(mats, flat.reshape(8, 1024), f32)


def _split_rest(p, conv_rows):
    o = 0
    out = []
    for rows in _REST_ROWS + (conv_rows,):
        out.append(p[..., o:o + rows, :])
        o += rows
    return out


_SMALL = (("conv_b", (1, 1024)), ("conv_ln_gain", (1, 1024)), ("conv_ln_bias", (1, 1024)),
          ("ssm_lambda_re", (1, 32, 64)), ("ssm_lambda_im", (1, 32, 64)), ("ssm_log_dt", (1, 32)),
          ("ssm_b_re", (1, 32, 64, 16)), ("ssm_b_im", (1, 32, 64, 16)), ("ssm_c_re", (1, 32, 16, 64)),
          ("ssm_c_im", (1, 32, 16, 64)), ("ssm_d", (1, 32, 16)), ("b_ssm_glu", (1, 512)), ("post_norm_gain", (1, 1024)))


def _pack_small(vals, extra=None):
    rows = []
    for v in list(vals) + ([extra] if extra is not None else []):
        flat = v.reshape(-1).astype(f32)
        n = -(-flat.shape[0] // 1024) * 1024
        rows.append(jnp.pad(flat, (0, n - flat.shape[0])).reshape(-1, 128))
    used = sum(r.shape[0] for r in rows)
    rows.append(jnp.zeros((SMALL_ROWS - used, 128), f32))
    return jnp.concatenate(rows, axis=0)


def _unpack_small(p):
    o = 0
    out = []
    for _, shape in _SMALL:
        n = int(np.prod(shape))
        nr = -(-n // 1024) * 8
        out.append(p[o:o + nr].reshape(-1)[:n].reshape(shape))
        o += nr
    return out, p[o, 0]


def _discretize(lam_re, lam_im, log_dt, b_re, b_im):
    dt = jnp.exp(log_dt)[:, None]
    mag = jnp.exp(lam_re * dt)
    ar = mag * jnp.cos(lam_im * dt)
    ai = mag * jnp.sin(lam_im * dt)
    den = lam_re * lam_re + lam_im * lam_im
    zr = ((ar - 1.0) * lam_re + ai * lam_im) / den
    zi = (ai * lam_re - (ar - 1.0) * lam_im) / den
    bbr = zr[..., None] * b_re - zi[..., None] * b_im
    bbi = zr[..., None] * b_im + zi[..., None] * b_re
    return ar, ai, bbr, bbi


_EYE8 = np.eye(8, dtype=np.float32)


def _bbt_blocks(bb):
    v = bb.reshape(4, 8, PST, H).transpose(0, 1, 3, 2)
    return jnp.einsum("bghp,gk->bghkp", v, _EYE8).reshape(4, 128, 512)


def _bbt_unblock(m):
    v = jnp.einsum("bghkp,gk->bghp", m.reshape(4, 8, H, 8, PST), _EYE8)
    return v.transpose(0, 1, 3, 2).reshape(G, PST, H)


def _ct_blocks(cc):
    v = cc.reshape(4, 8, H, PST)
    return jnp.einsum("bghp,gk->bgpkh", v, _EYE8).reshape(4, 512, 128)


def _ct_unblock(m):
    v = jnp.einsum("bgpkh,gk->bghp", m.reshape(4, 8, PST, 8, H), _EYE8)
    return v.reshape(G, H, PST)


def _perm_matrix():
    p = np.zeros((TC, TC), np.float32)
    for r in range(R):
        for seg in range(8):
            p[r * 8 + seg, seg * R + r] = 1.0
    return p


def _deinterleave(a):
    L, C = a.shape
    return a.reshape(L // TC, R, 8, C).transpose(0, 2, 1, 3).reshape(L, C)


def _fwd_bwd(h, xi, ti, proj, conv_w, w_co, w_glu, w_so, w_out, small):
    (conv_b, ln_g, ln_b, lam_re, lam_im, log_dt, b_re, b_im, c_re, c_im, dvec, b_glu, g_post) = small
    lam_re, lam_im, log_dt = lam_re[0], lam_im[0], log_dt[0]
    b_re, b_im, c_re, c_im = b_re[0], b_im[0], c_re[0], c_im[0]
    (ar, ai, bbr, bbi), disc_vjp = jax.vjp(_discretize, lam_re, lam_im, log_dt, b_re, b_im)
    a_re = ar.reshape(1, NS)
    a_im = ai.reshape(1, NS)
    dt = jnp.exp(log_dt)[:, None]
    steps = jnp.arange(1, R + 1, dtype=f32)[:, None, None]
    apow_re = (jnp.exp(steps * (lam_re * dt)) * jnp.cos(steps * (lam_im * dt))).reshape(R, NS)
    apow_im = (jnp.exp(steps * (lam_re * dt)) * jnp.sin(steps * (lam_im * dt))).reshape(R, NS)
    bbt_re, bbt_im = _bbt_blocks(bbr).astype(bf16), _bbt_blocks(bbi).astype(bf16)
    ct_re, ct_im = _ct_blocks(c_re).astype(bf16), _ct_blocks(c_im).astype(bf16)
    d_row = dvec.reshape(1, SW)
    cw32 = jnp.pad(conv_w, ((0, 1), (0, 0)))

    cu1, a_in = _conv_fwd(proj, cw32, conv_b, ln_g, ln_b)
    y0, b_in, sre, sim, cinr, cini = _ssm_fwd(proj, bbt_re, bbt_im, ct_re, ct_im, a_re, a_im,
                                              apow_re, apow_im, d_row, w_glu, b_glu)
    gx0, d_ain, d_bin, dproj, dw_out, dw_co, dw_so, dg_post, loss = _tail(
        a_in, b_in, proj, xi, ti, w_co, w_so, w_out, g_post)
    (dproj, dbbt_re, dbbt_im, dct_re, dct_im, dd, dar8, dai8, dw_glu, db_glu) = _ssm_bwd(
        d_bin, y0, proj, sre, sim, cinr, cini, bbt_re, bbt_im, ct_re, ct_im,
        a_re, a_im, apow_re, apow_im, d_row, w_glu, b_glu, dproj)
    dproj, dcw8, d_convb, d_lng, d_lnb = _conv_bwd(d_ain, cu1, proj, cw32, ln_g, ln_b, dproj)
    dw_in = _win_grad(h, dproj)

    d_ar = jnp.sum(dar8, axis=0).reshape(G, PST)
    d_ai = jnp.sum(dai8, axis=0).reshape(G, PST)
    d_lre, d_lim, d_ldt, d_bre, d_bim = disc_vjp((d_ar, d_ai, _bbt_unblock(dbbt_re), _bbt_unblock(dbbt_im)))
    d_conv_w = jnp.sum(dcw8, axis=1)[:KS]
    small_grads = [d_convb, d_lng, d_lnb, d_lre[None], d_lim[None], d_ldt[None], d_bre[None], d_bim[None],
                   _ct_unblock(dct_re)[None], _ct_unblock(dct_im)[None], dd.reshape(1, G, H), db_glu, dg_post]
    return loss[0, 0], gx0, dproj, (dw_in, dw_co, dw_out, dw_glu, dw_so, d_conv_w), small_grads


def kernel(x, pre_norm_gain, w_in, conv_w, conv_b, conv_ln_gain, conv_ln_bias, w_conv_out, ssm_lambda_re, ssm_lambda_im, ssm_log_dt, ssm_b_re, ssm_b_im, ssm_c_re, ssm_c_im, ssm_d, w_ssm_glu, b_ssm_glu, w_ssm_out, w_out, post_norm_gain, loss_target, m_pre_norm_gain, m_w_in, m_conv_w, m_conv_b, m_conv_ln_gain, m_conv_ln_bias, m_w_conv_out, m_ssm_lambda_re, m_ssm_lambda_im, m_ssm_log_dt, m_ssm_b_re, m_ssm_b_im, m_ssm_c_re, m_ssm_c_im, m_ssm_d, m_w_ssm_glu, m_b_ssm_glu, m_w_ssm_out, m_w_out, m_post_norm_gain, v_pre_norm_gain, v_w_in, v_conv_w, v_conv_b, v_conv_ln_gain, v_conv_ln_bias, v_w_conv_out, v_ssm_lambda_re, v_ssm_lambda_im, v_ssm_log_dt, v_ssm_b_re, v_ssm_b_im, v_ssm_c_re, v_ssm_c_im, v_ssm_d, v_w_ssm_glu, v_b_ssm_glu, v_w_ssm_out, v_w_out, v_post_norm_gain):
    c = lax.axis_index("c")
    shards = [w_in[0].astype(bf16), w_conv_out[0].astype(bf16), w_out[0].astype(bf16), w_ssm_glu[0].astype(bf16),
              w_ssm_out[0].astype(bf16), jnp.pad(conv_w[0], ((0, CONV_ROWS - KS), (0, 0)))]
    k_arr = (2 * lax.axis_index("x") + lax.axis_index("y")).astype(jnp.int32).reshape(1)
    w_in_g, w_co_g, w_out_g, w_glu_g, w_so_g, conv_w_g, h, xi, ti, proj = _gather_prep(
        k_arr, shards, x[0], loss_target[0], pre_norm_gain, jnp.asarray(_perm_matrix(), bf16))
    conv_w_f = conv_w_g[:, :KS].transpose(1, 0, 2).reshape(KS, CW)

    small = (conv_b, conv_ln_gain, conv_ln_bias, ssm_lambda_re, ssm_lambda_im, ssm_log_dt, ssm_b_re,
             ssm_b_im, ssm_c_re, ssm_c_im, ssm_d, b_ssm_glu, post_norm_gain)
    loss_part, gx0, dproj, big_grads, small_grads = _fwd_bwd(
        h, xi, ti, _proj_fwd(k_arr, h, w_in_g, proj), conv_w_f, w_co_g.reshape(CW, D), w_glu_g.reshape(SW, SW), w_so_g,
        w_out_g.reshape(D, D), small)

    dw_in, dw_co, dw_out, dw_glu, dw_so, d_conv_w = big_grads
    d_conv_w = jnp.pad(d_conv_w, ((0, CONV_ROWS - KS), (0, 0))).reshape(CONV_ROWS, NCHIP, 256).transpose(1, 0, 2)
    grads = [dw_in] + [g.astype(bf16) for g in (dw_co.reshape(NCHIP, 256, D), dw_out.reshape(NCHIP, 256, D),
                                                  dw_glu.reshape(NCHIP, 128, SW), dw_so, d_conv_w)]
    gs = _pack_small(small_grads, extra=loss_part)
    *recvs, rs = _pair_exchange_list(grads, gs)
    *parts, ps = _pair_sum_list(c.astype(jnp.int32).reshape(1), grads, recvs, gs, rs)
    gxi, dg_pre, *qparts, qs = _x_grad_exchange(dproj, w_in_g, xi, gx0, pre_norm_gain, parts, ps)
    grad_x = _deinterleave(gxi)
    *halves, fs = _chip_sum_list(qparts, qs)
    g_big = list(_sibling_join_list(halves))
    g_big[5] = g_big[5][:KS]

    big_w = (w_in[0], w_conv_out[0], w_out[0], w_ssm_glu[0], w_ssm_out[0], conv_w[0])
    big_m = (m_w_in[0], m_w_conv_out[0], m_w_out[0], m_w_ssm_glu[0], m_w_ssm_out[0], m_conv_w[0])
    big_v = (v_w_in[0], v_w_conv_out[0], v_w_out[0], v_w_ssm_glu[0], v_w_ssm_out[0], v_conv_w[0])
    big_names = ("w_in", "w_conv_out", "w_out", "w_ssm_glu", "w_ssm_out", "conv_w")
    res = {}
    for n, w, g, m, v in zip(big_names, big_w, g_big, big_m, big_v):
        d, m2, v2 = _adamw("adamw_" + n, w, g, m, v)
        res[n] = (g[None], d[None], m2[None], v2[None])

    small_m = (m_conv_b, m_conv_ln_gain, m_conv_ln_bias, m_ssm_lambda_re, m_ssm_lambda_im, m_ssm_log_dt,
               m_ssm_b_re, m_ssm_b_im, m_ssm_c_re, m_ssm_c_im, m_ssm_d, m_b_ssm_glu, m_post_norm_gain)
    small_v = (v_conv_b, v_conv_ln_gain, v_conv_ln_bias, v_ssm_lambda_re, v_ssm_lambda_im, v_ssm_log_dt,
               v_ssm_b_re, v_ssm_b_im, v_ssm_c_re, v_ssm_c_im, v_ssm_d, v_b_ssm_glu, v_post_norm_gain)
    sd, sm, sv = _adamw("adamw_small", _pack_small(small), fs, _pack_small(small_m), _pack_small(small_v))
    sg_l, loss = _unpack_small(fs)
    sd_l, _ = _unpack_small(sd)
    sm_l, _ = _unpack_small(sm)
    sv_l, _ = _unpack_small(sv)
    for i, (n, _) in enumerate(_SMALL):
        res[n] = (sg_l[i], sd_l[i], sm_l[i], sv_l[i])
    rows = lambda a: a.reshape(8, 128)
    pre = _adamw_rows(_allgather_rows(rows(dg_pre)), rows(pre_norm_gain), rows(m_pre_norm_gain), rows(v_pre_norm_gain))
    res["pre_norm_gain"] = tuple(a.reshape(1, D) for a in pre)

    order = ("pre_norm_gain", "w_in", "conv_w", "conv_b", "conv_ln_gain", "conv_ln_bias", "w_conv_out", "ssm_lambda_re",
             "ssm_lambda_im", "ssm_log_dt", "ssm_b_re", "ssm_b_im", "ssm_c_re", "ssm_c_im", "ssm_d", "w_ssm_glu",
             "b_ssm_glu", "w_ssm_out", "w_out", "post_norm_gain")
    outs = [loss, grad_x[None]]
    for q in range(4):
        outs.extend(res[n][q] for n in order)
    return tuple(outs)
```

```python
import math

import numpy as np
import jax
import jax.numpy as jnp
from jax import lax
from jax.experimental import pallas as pl
from jax.experimental.pallas import tpu as pltpu

f32 = jnp.float32
bf16 = jnp.bfloat16

D = 1024
CW = 1024
SW = 512
G = 32
H = 16
PST = 64
NS = G * PST
KS = 31
IN_W = 6144
NCHIP = 4
SHARD_W = IN_W // NCHIP
RMS_EPS = 1e-6
LN_EPS = 1e-5
LR, B1, B2, EPS, WD, STEP = 0.001, 0.9, 0.999, 1e-08, 0.01, 10
GELU_K0 = math.sqrt(2.0 / math.pi)
GELU_K1 = 0.044715

TC = 512
R = TC // 8
NH = 32
LBW = 1024
REST_ROWS = 768
CONV_ROWS = 64
SMALL_ROWS = 1152
VMEM_LIMIT = 56 * 1024 * 1024
MESH = pl.DeviceIdType.MESH


def _cp(*sem):
    return pltpu.CompilerParams(dimension_semantics=tuple(sem), vmem_limit_bytes=VMEM_LIMIT)


def _sig(v):
    return 0.5 * jnp.tanh(0.5 * v) + 0.5


def _dot(a, b):
    return jnp.dot(a, b, preferred_element_type=f32)


def _dot_nt(a, b):
    return lax.dot_general(a, b, (((1,), (1,)), ((), ())), preferred_element_type=f32)


def _dot_tn(a, b):
    return lax.dot_general(a, b, (((0,), (0,)), ((), ())), preferred_element_type=f32)


def _full(shape):
    nd = len(shape)
    return pl.BlockSpec(shape, lambda *_: (0,) * nd)


def _rows8(i):
    return pl.ds(pl.multiple_of(i * 8, 8), 8)


def _prenorm(x, g_pre):
    L = x.shape[0]
    tm = 512
    def body(x_ref, g_ref, h_ref):
        xt = x_ref[...]
        r = lax.rsqrt(jnp.mean(xt * xt, axis=-1, keepdims=True) + RMS_EPS)
        h_ref[...] = (xt * r * g_ref[...]).astype(bf16)
    return pl.pallas_call(
        body, grid=(L // tm,),
        in_specs=[pl.BlockSpec((tm, D), lambda i: (i, 0)), _full((1, D))],
        out_specs=pl.BlockSpec((tm, D), lambda i: (i, 0)),
        out_shape=jax.ShapeDtypeStruct((L, D), bf16),
        name="prenorm", compiler_params=_cp("arbitrary"))(x, g_pre)


def _proj_fwd(k_arr, h, w_in, proj):
    L = h.shape[0]
    tm = min(1024, L)
    def body(_, h_ref, w_ref, __, o_ref):
        o_ref[...] = _dot(h_ref[...], w_ref[0]).astype(bf16)
    shard = lambda j, k: (k[0] + 1 + j) % NCHIP
    grid_spec = pltpu.PrefetchScalarGridSpec(
        num_scalar_prefetch=1, grid=(NCHIP - 1, L // tm),
        in_specs=[pl.BlockSpec((tm, D), lambda j, i, k: (i, 0)),
                  pl.BlockSpec((1, D, SHARD_W), lambda j, i, k: (shard(j, k), 0, 0)), _ANY],
        out_specs=pl.BlockSpec((tm, SHARD_W), lambda j, i, k: (i, shard(j, k))))
    return pl.pallas_call(
        body, grid_spec=grid_spec, out_shape=jax.ShapeDtypeStruct((L, IN_W), bf16),
        input_output_aliases={3: 0},
        name="proj_fwd", compiler_params=_cp("arbitrary", "arbitrary"))(k_arr, h, w_in, proj)


NLB = CW // 128
RPI = 8


def _put_blocked(buf, row0, nrows, v):
    for lb in range(NLB):
        buf[lb, pl.ds(row0, nrows), :] = v[:, lb * 128:(lb + 1) * 128]


def _get_blocked(buf, row0, nrows):
    return jnp.concatenate([buf[lb, pl.ds(row0, nrows), :] for lb in range(NLB)], axis=1)


def _fill_before(ebuf, prev):
    sub = lax.broadcasted_iota(jnp.int32, (8, 128), 0)
    def halo(p, carry):
        for lb in range(NLB):
            cur = ebuf[lb, _rows8(R + p), :]
            ebuf[lb, _rows8(p), :] = jnp.where(sub == 0, pltpu.roll(prev[lb, _rows8(p), :], 1, 0),
                                               pltpu.roll(cur, 1, 0))
        return carry
    lax.fori_loop(0, NH, halo, 0)


def _fir(buf, lb, r, coef, first, flip):
    win = buf[lb, pl.ds(pl.multiple_of(r * 8, 8), (KS + RPI - 1) * 8), :]
    outs = []
    for i in range(RPI):
        acc = [first, None, None, None]
        for k in range(KS):
            o = i + ((KS - 1 - k) if flip else k)
            t = coef[k] * win[8 * o:8 * o + 8, :]
            acc[k % 4] = t if acc[k % 4] is None else acc[k % 4] + t
        outs.append((acc[0] + acc[1]) + (acc[2] + acc[3]))
    return outs


def _conv_fwd(proj, cw, cbias, lng, lnb):
    L = proj.shape[0]
    nc = L // TC
    def body(ca_ref, cb_ref, zc_ref, w_ref, b_ref, g_ref, bb_ref, cu1_ref, ain_ref, ebuf, prev, cacc):
        @pl.when(pl.program_id(0) == 0)
        def _():
            prev[...] = jnp.zeros_like(prev)
        def glu(s, carry):
            rows = pl.ds(pl.multiple_of(s * 64, 64), 64)
            _put_blocked(ebuf, pl.multiple_of(NH * 8 + s * 64, 64), 64,
                         ca_ref[rows, :].astype(f32) * _sig(cb_ref[rows, :].astype(f32)))
            return carry
        lax.fori_loop(0, TC // 64, glu, 0)
        _fill_before(ebuf, prev)
        prev[...] = ebuf[:, R * 8:(NH + R) * 8, :]
        for lb in range(NLB):
            sl = slice(lb * 128, (lb + 1) * 128)
            wk = [jnp.broadcast_to(w_ref[k:k + 1, sl], (8, 128)) for k in range(KS)]
            bias = jnp.broadcast_to(b_ref[:, sl], (8, 128))
            def tap(q, carry, lb=lb, wk=wk, bias=bias):
                r = q * RPI
                for i, o in enumerate(_fir(ebuf, lb, r + (NH - KS + 1), wk, bias, False)):
                    cacc[lb, _rows8(r + i), :] = o
                return carry
            lax.fori_loop(0, R // RPI, tap, 0)
        def norm(s, carry):
            rows = pl.ds(pl.multiple_of(s * 64, 64), 64)
            c1b = _get_blocked(cacc, pl.multiple_of(s * 64, 64), 64).astype(bf16)
            cu1_ref[rows, :] = c1b
            c1 = c1b.astype(f32)
            xc = c1 - jnp.mean(c1, axis=-1, keepdims=True)
            var = jnp.mean(xc * xc, axis=-1, keepdims=True)
            ln = xc * lax.rsqrt(var + LN_EPS) * g_ref[...] + bb_ref[...]
            zc = zc_ref[rows, :].astype(f32)
            ain_ref[rows, :] = ((ln * _sig(ln)) * (zc * _sig(zc))).astype(bf16)
            return carry
        lax.fori_loop(0, TC // 64, norm, 0, unroll=4)

    col = lambda c: pl.BlockSpec((TC, CW), lambda i, c=c: (i, c))
    return pl.pallas_call(
        body, grid=(nc,),
        in_specs=[col(0), col(1), col(2), _full((32, CW)), _full((1, CW)), _full((1, CW)), _full((1, CW))],
        out_specs=[pl.BlockSpec((TC, CW), lambda i: (i, 0)), pl.BlockSpec((TC, CW), lambda i: (i, 0))],
        out_shape=[jax.ShapeDtypeStruct((L, CW), bf16), jax.ShapeDtypeStruct((L, CW), bf16)],
        scratch_shapes=[pltpu.VMEM((NLB, (NH + R) * 8, 128), f32), pltpu.VMEM((NLB, NH * 8, 128), f32),
                        pltpu.VMEM((NLB, TC, 128), f32)],
        name="conv_fwd", compiler_params=_cp("arbitrary"))(proj, proj, proj, cw, cbias, lng, lnb)


def _gelu_parts(y0):
    t = jnp.tanh(GELU_K0 * (y0 + GELU_K1 * y0 * y0 * y0))
    return t, 0.5 * y0 * (1.0 + t)


def _ssm_fwd(proj, bbt_re, bbt_im, ct_re, ct_im, a_re, a_im, apow_re, apow_im, dvec, wglu, bglu):
    L = proj.shape[0]
    nc = L // TC
    def body(u_ref, zs_ref, bre_ref, bim_ref, cre_ref, cim_ref, are_ref, aim_ref, pwr_ref, pwi_ref,
             d_ref, wg_ref, bg_ref, y0_ref, bin_ref, sre, sim, cinr, cini, prev_re, prev_im):
        c = pl.program_id(0)
        @pl.when(c == 0)
        def _():
            prev_re[...] = jnp.zeros_like(prev_re)
            prev_im[...] = jnp.zeros_like(prev_im)
        u = u_ref[...]
        for blk in range(4):
            ub = u[:, 128 * blk:128 * (blk + 1)]
            sre[:, 512 * blk:512 * (blk + 1)] = _dot(ub, bre_ref[blk])
            sim[:, 512 * blk:512 * (blk + 1)] = _dot(ub, bim_ref[blk])
        for lb in range(NS // LBW):
            sl = slice(lb * LBW, (lb + 1) * LBW)
            ar = jnp.broadcast_to(are_ref[:, sl], (8, LBW))
            ai = jnp.broadcast_to(aim_ref[:, sl], (8, LBW))
            def step(r, carry, sl=sl, ar=ar, ai=ai):
                sr, si = carry
                nr = ar * sr - ai * si + sre[_rows8(r), sl]
                ni = ar * si + ai * sr + sim[_rows8(r), sl]
                sre[_rows8(r), sl] = nr
                sim[_rows8(r), sl] = ni
                return nr, ni
            lax.fori_loop(1, R, step, (sre[0:8, sl], sim[0:8, sl]))
        a_r = pwr_ref[R - 1:R, :]
        a_i = pwi_ref[R - 1:R, :]
        cr = prev_re[0:1, :]
        ci = prev_im[0:1, :]
        for seg in range(8):
            cinr[seg:seg + 1, :] = cr
            cini[seg:seg + 1, :] = ci
            er = sre[8 * (R - 1) + seg:8 * (R - 1) + seg + 1, :]
            ei = sim[8 * (R - 1) + seg:8 * (R - 1) + seg + 1, :]
            cr, ci = er + a_r * cr - a_i * ci, ei + a_r * ci + a_i * cr
        prev_re[0:1, :] = cr
        prev_im[0:1, :] = ci
        for lb in range(NS // LBW):
            sl = slice(lb * LBW, (lb + 1) * LBW)
            kr = cinr[:, sl]
            ki = cini[:, sl]
            def fix(r, carry, sl=sl, kr=kr, ki=ki):
                pr = jnp.broadcast_to(pwr_ref[pl.ds(r, 1), sl], (8, LBW))
                pi = jnp.broadcast_to(pwi_ref[pl.ds(r, 1), sl], (8, LBW))
                sre[_rows8(r), sl] = sre[_rows8(r), sl] + pr * kr - pi * ki
                sim[_rows8(r), sl] = sim[_rows8(r), sl] + pr * ki + pi * kr
                return carry
            lax.fori_loop(0, R, fix, 0)
        yp = []
        for blk in range(4):
            sr = sre[:, 512 * blk:512 * (blk + 1)].astype(bf16)
            si = sim[:, 512 * blk:512 * (blk + 1)].astype(bf16)
            yp.append(_dot(sr, cre_ref[blk]) - _dot(si, cim_ref[blk]))
        y0 = jnp.concatenate(yp, axis=1) + d_ref[...] * u.astype(f32)
        y0_ref[...] = y0
        _, y1 = _gelu_parts(y0)
        glu = _dot(y1.astype(bf16), wg_ref[...]) + bg_ref[...]
        y2 = y1 * _sig(glu)
        zs = zs_ref[...].astype(f32)
        bin_ref[...] = (y2 * (zs * _sig(zs))).astype(bf16)

    return pl.pallas_call(
        body, grid=(nc,),
        in_specs=[pl.BlockSpec((TC, SW), lambda c: (c, 6)), pl.BlockSpec((TC, SW), lambda c: (c, 7)),
                  _full((4, 128, 512)), _full((4, 128, 512)), _full((4, 512, 128)), _full((4, 512, 128)),
                  _full((1, NS)), _full((1, NS)), _full((R, NS)), _full((R, NS)),
                  _full((1, SW)), _full((SW, SW)), _full((1, SW))],
        out_specs=[pl.BlockSpec((TC, SW), lambda c: (c, 0)), pl.BlockSpec((TC, SW), lambda c: (c, 0)),
                   pl.BlockSpec((TC, NS), lambda c: (c, 0)), pl.BlockSpec((TC, NS), lambda c: (c, 0)),
                   pl.BlockSpec((8, NS), lambda c: (c, 0)), pl.BlockSpec((8, NS), lambda c: (c, 0))],
        out_shape=[jax.ShapeDtypeStruct((L, SW), f32), jax.ShapeDtypeStruct((L, SW), bf16),
                   jax.ShapeDtypeStruct((L, NS), f32), jax.ShapeDtypeStruct((L, NS), f32),
                   jax.ShapeDtypeStruct((nc * 8, NS), f32), jax.ShapeDtypeStruct((nc * 8, NS), f32)],
        scratch_shapes=[pltpu.VMEM((8, NS), f32), pltpu.VMEM((8, NS), f32)],
        name="ssm_fwd", compiler_params=_cp("arbitrary"))(
            proj, proj, bbt_re, bbt_im, ct_re, ct_im, a_re, a_im, apow_re, apow_im, dvec, wglu, bglu)


def _tail(a_in, b_in, proj, x, tgt, wco, wso, wout, gpost):
    L = x.shape[0]
    tm = 256
    def body(a_ref, b_ref, gc_ref, gs_ref, x_ref, t_ref, wco_ref, wso_ref, wout_ref, gp_ref,
             gx_ref, dain_ref, dbin_ref, dp_ref, dwout_ref, dwco_ref, dwso_ref, dgp_ref, loss_ref):
        @pl.when(pl.program_id(0) == 0)
        def _():
            dwout_ref[...] = jnp.zeros_like(dwout_ref)
            dwco_ref[...] = jnp.zeros_like(dwco_ref)
            dwso_ref[...] = jnp.zeros_like(dwso_ref)
            dgp_ref[...] = jnp.zeros_like(dgp_ref)
            loss_ref[...] = jnp.zeros_like(loss_ref)
        a = a_ref[...]
        b = b_ref[...]
        co = _dot(a, wco_ref[...])
        so = jnp.concatenate([_dot(b, wso_ref[j]) for j in range(NCHIP)], axis=1)
        sc = _sig(gc_ref[...].astype(f32))
        ss = _sig(gs_ref[...].astype(f32))
        mb = (sc * co + ss * so).astype(bf16)
        out = _dot(mb, wout_ref[...])
        r2 = lax.rsqrt(jnp.mean(out * out, axis=-1, keepdims=True) + RMS_EPS)
        on = out * r2
        gp = gp_ref[...]
        e = x_ref[...] + on * gp - t_ref[...]
        loss_ref[...] += (0.5 / D) * jnp.sum(e * e)
        dy = e * (1.0 / D)
        gx_ref[...] = dy
        dgp_ref[...] += jnp.sum(dy * on, axis=0, keepdims=True)
        dn = dy * gp
        dout = (r2 * (dn - on * jnp.mean(dn * on, axis=-1, keepdims=True))).astype(bf16)
        dwout_ref[...] += _dot_tn(mb, dout)
        dm = _dot_nt(dout, wout_ref[...])
        dp_ref[:, 0:D] = (dm * co * sc * (1.0 - sc)).astype(bf16)
        dp_ref[:, D:2 * D] = (dm * so * ss * (1.0 - ss)).astype(bf16)
        dco = (dm * sc).astype(bf16)
        dso = (dm * ss).astype(bf16)
        dwco_ref[...] += _dot_tn(a, dco)
        dbin = None
        for j in range(NCHIP):
            dso_j = dso[:, j * 256:(j + 1) * 256]
            dwso_ref[j] += _dot_tn(b, dso_j)
            t = _dot_nt(dso_j, wso_ref[j])
            dbin = t if dbin is None else dbin + t
        dain_ref[...] = _dot_nt(dco, wco_ref[...]).astype(bf16)
        dbin_ref[...] = dbin.astype(bf16)

    row = lambda w: pl.BlockSpec((tm, w), lambda i: (i, 0))
    one = lambda shape: pl.BlockSpec(shape, lambda i: (0,) * len(shape), pipeline_mode=pl.Buffered(1))
    return pl.pallas_call(
        body, grid=(L // tm,),
        in_specs=[row(CW), row(SW), pl.BlockSpec((tm, D), lambda i: (i, 4)), pl.BlockSpec((tm, D), lambda i: (i, 5)),
                  row(D), row(D), one((CW, D)), one((NCHIP, SW, 256)), one((D, D)), one((1, D))],
        out_specs=[row(D), row(CW), row(SW), pl.BlockSpec((tm, 2 * D), lambda i: (i, 2)),
                   one((D, D)), one((CW, D)), one((NCHIP, SW, 256)), one((1, D)), one((1, 128))],
        out_shape=[jax.ShapeDtypeStruct((L, D), f32), jax.ShapeDtypeStruct((L, CW), bf16),
                   jax.ShapeDtypeStruct((L, SW), bf16), jax.ShapeDtypeStruct((L, IN_W), bf16),
                   jax.ShapeDtypeStruct((D, D), f32), jax.ShapeDtypeStruct((CW, D), f32),
                   jax.ShapeDtypeStruct((NCHIP, SW, 256), f32), jax.ShapeDtypeStruct((1, D), f32),
                   jax.ShapeDtypeStruct((1, 128), f32)],
        name="tail", compiler_params=_cp("arbitrary"))(a_in, b_in, proj, proj, x, tgt, wco, wso, wout, gpost)


def _ssm_bwd(d_bin, y0, proj, sre, sim, cinr, cini, bbt_re, bbt_im, ct_re, ct_im,
             a_re, a_im, apow_re, apow_im, dvec, wglu, bglu, dproj):
    L = y0.shape[0]
    nc = L // TC
    def body(dbin_ref, y0_ref, u_ref, zs_ref, sre_ref, sim_ref, cinr_ref, cini_ref,
             bre_ref, bim_ref, cre_ref, cim_ref, are_ref, aim_ref, pwr_ref, pwi_ref, d_ref, wg_ref, bg_ref, _,
             dp_ref, dbre_ref, dbim_ref, dcre_ref, dcim_ref, dd_ref, dar_ref, dai_ref, dwg_ref, dbg_ref,
             gre, gim, gcr, gci, nxt_re, nxt_im):
        @pl.when(pl.program_id(0) == 0)
        def _():
            for ref in (dbre_ref, dbim_ref, dcre_ref, dcim_ref, dd_ref, dar_ref, dai_ref, dwg_ref, dbg_ref,
                        nxt_re, nxt_im):
                ref[...] = jnp.zeros_like(ref)
        y0 = y0_ref[...]
        u = u_ref[...]
        zs = zs_ref[...].astype(f32)
        dbin = dbin_ref[...].astype(f32)
        t, y1 = _gelu_parts(y0)
        y1b = y1.astype(bf16)
        sg = _sig(_dot(y1b, wg_ref[...]) + bg_ref[...])
        sz = _sig(zs)
        d_y2 = dbin * (zs * sz)
        dp_ref[:, SW:2 * SW] = (dbin * (y1 * sg) * (sz * (1.0 + zs * (1.0 - sz)))).astype(bf16)
        d_glu = d_y2 * y1 * sg * (1.0 - sg)
        d_glub = d_glu.astype(bf16)
        d_y1 = d_y2 * sg + _dot_nt(d_glub, wg_ref[...])
        dwg_ref[...] += _dot_tn(y1b, d_glub)
        dbg_ref[...] += jnp.sum(d_glu, axis=0, keepdims=True)
        dgelu = 0.5 * (1.0 + t) + 0.5 * y0 * (1.0 - t * t) * GELU_K0 * (1.0 + 3.0 * GELU_K1 * y0 * y0)
        d_y0 = d_y1 * dgelu
        dd_ref[...] += jnp.sum(d_y0 * u.astype(f32), axis=0, keepdims=True)
        dyb = d_y0.astype(bf16)
        for blk in range(4):
            dy1 = dyb[:, 128 * blk:128 * (blk + 1)]
            gre[:, 512 * blk:512 * (blk + 1)] = _dot_nt(dy1, cre_ref[blk])
            gim[:, 512 * blk:512 * (blk + 1)] = -_dot_nt(dy1, cim_ref[blk])
        for lb in range(NS // LBW):
            sl = slice(lb * LBW, (lb + 1) * LBW)
            ar = jnp.broadcast_to(are_ref[:, sl], (8, LBW))
            ai = jnp.broadcast_to(aim_ref[:, sl], (8, LBW))
            def step(k, carry, sl=sl, ar=ar, ai=ai):
                gr, gi = carry
                row = _rows8(R - 2 - k)
                nr = ar * gr + ai * gi + gre[row, sl]
                ni = ar * gi - ai * gr + gim[row, sl]
                gre[row, sl] = nr
                gim[row, sl] = ni
                return nr, ni
            lax.fori_loop(0, R - 1, step, (gre[8 * (R - 1):8 * R, sl], gim[8 * (R - 1):8 * R, sl]))
        a_r = pwr_ref[R - 1:R, :]
        a_i = pwi_ref[R - 1:R, :]
        cr = nxt_re[0:1, :]
        ci = nxt_im[0:1, :]
        for seg in range(7, -1, -1):
            gcr[seg:seg + 1, :] = cr
            gci[seg:seg + 1, :] = ci
            er = gre[seg:seg + 1, :]
            ei = gim[seg:seg + 1, :]
            cr, ci = er + a_r * cr + a_i * ci, ei + a_r * ci - a_i * cr
        nxt_re[0:1, :] = cr
        nxt_im[0:1, :] = ci
        for lb in range(NS // LBW):
            sl = slice(lb * LBW, (lb + 1) * LBW)
            kr = gcr[:, sl]
            ki = gci[:, sl]
            def fixed(rows, prow, sl=sl, kr=kr, ki=ki):
                pr = jnp.broadcast_to(pwr_ref[prow, sl], (8, LBW))
                pi = jnp.broadcast_to(pwi_ref[prow, sl], (8, LBW))
                gr = gre[rows, sl] + pr * kr + pi * ki
                gi = gim[rows, sl] + pr * ki - pi * kr
                gre[rows, sl] = gr
                gim[rows, sl] = gi
                return gr, gi
            g0r, g0i = fixed(slice(0, 8), slice(R - 1, R))
            p0r, p0i = cinr_ref[:, sl], cini_ref[:, sl]
            acc0 = (g0r * p0r + g0i * p0i, g0i * p0r - g0r * p0i)
            def dacc(r, carry, sl=sl, fixed=fixed):
                xr, xi = carry
                gr, gi = fixed(_rows8(r), pl.ds(R - 1 - r, 1))
                pr, pi = sre_ref[_rows8(r - 1), sl], sim_ref[_rows8(r - 1), sl]
                return xr + gr * pr + gi * pi, xi + gi * pr - gr * pi
            xr, xi = lax.fori_loop(1, R, dacc, acc0)
            dar_ref[:, sl] += xr
            dai_ref[:, sl] += xi
        dup = []
        for blk in range(4):
            s4 = slice(512 * blk, 512 * (blk + 1))
            s1 = slice(128 * blk, 128 * (blk + 1))
            grb = gre[:, s4].astype(bf16)
            gib = gim[:, s4].astype(bf16)
            dup.append(_dot_nt(grb, bre_ref[blk]) + _dot_nt(gib, bim_ref[blk]))
            dbre_ref[blk] += _dot_tn(u[:, s1], grb)
            dbim_ref[blk] += _dot_tn(u[:, s1], gib)
            dcre_ref[blk] += _dot_tn(sre_ref[:, s4].astype(bf16), dyb[:, s1])
            dcim_ref[blk] -= _dot_tn(sim_ref[:, s4].astype(bf16), dyb[:, s1])
        dp_ref[:, 0:SW] = (jnp.concatenate(dup, axis=1) + d_ref[...] * d_y0).astype(bf16)

    rev = lambda w, cidx: pl.BlockSpec((TC, w), lambda i, cidx=cidx: (nc - 1 - i, cidx))
    one = lambda shape: pl.BlockSpec(shape, lambda i: (0,) * len(shape))
    return pl.pallas_call(
        body, grid=(nc,),
        in_specs=[rev(SW, 0), rev(SW, 0), rev(SW, 6), rev(SW, 7), rev(NS, 0), rev(NS, 0),
                  pl.BlockSpec((8, NS), lambda i: (nc - 1 - i, 0)), pl.BlockSpec((8, NS), lambda i: (nc - 1 - i, 0)),
                  one((4, 128, 512)), one((4, 128, 512)), one((4, 512, 128)), one((4, 512, 128)),
                  one((1, NS)), one((1, NS)), one((R, NS)), one((R, NS)),
                  one((1, SW)), one((SW, SW)), one((1, SW)), _ANY],
        out_specs=[pl.BlockSpec((TC, 2 * SW), lambda i: (nc - 1 - i, 3)),
                   one((4, 128, 512)), one((4, 128, 512)), one((4, 512, 128)), one((4, 512, 128)),
                   one((1, SW)), one((8, NS)), one((8, NS)), one((SW, SW)), one((1, SW))],
        out_shape=[jax.ShapeDtypeStruct((L, IN_W), bf16),
                   jax.ShapeDtypeStruct((4, 128, 512), f32), jax.ShapeDtypeStruct((4, 128, 512), f32),
                   jax.ShapeDtypeStruct((4, 512, 128), f32), jax.ShapeDtypeStruct((4, 512, 128), f32),
                   jax.ShapeDtypeStruct((1, SW), f32), jax.ShapeDtypeStruct((8, NS), f32),
                   jax.ShapeDtypeStruct((8, NS), f32), jax.ShapeDtypeStruct((SW, SW), f32),
                   jax.ShapeDtypeStruct((1, SW), f32)],
        scratch_shapes=[pltpu.VMEM((TC, NS), f32), pltpu.VMEM((TC, NS), f32), pltpu.VMEM((8, NS), f32),
                        pltpu.VMEM((8, NS), f32), pltpu.VMEM((8, NS), f32), pltpu.VMEM((8, NS), f32)],
        input_output_aliases={19: 0},
        name="ssm_bwd", compiler_params=_cp("arbitrary"))(
            d_bin, y0, proj, proj, sre, sim, cinr, cini, bbt_re, bbt_im, ct_re, ct_im,
            a_re, a_im, apow_re, apow_im, dvec, wglu, bglu, dproj)


def _conv_bwd(d_ain, cu1, proj, cw, lng, lnb, dproj):
    L = cu1.shape[0]
    nc = L // TC
    def body(dain_ref, cu1_ref, ca_ref, cb_ref, zc_ref, cah_ref, cbh_ref, w_ref, g_ref, bb_ref, _,
             dp_ref, dw_ref, dbias_ref, dlng_ref, dlnb_ref, dbuf, ebuf, prev, nxt, dcu0):
        i = pl.program_id(0)
        @pl.when(i == 0)
        def _():
            dw_ref[...] = jnp.zeros_like(dw_ref)
            dbias_ref[...] = jnp.zeros_like(dbias_ref)
            dlng_ref[...] = jnp.zeros_like(dlng_ref)
            dlnb_ref[...] = jnp.zeros_like(dlnb_ref)
            nxt[...] = jnp.zeros_like(nxt)
        def lnb(s, carry):
            rows = pl.ds(pl.multiple_of(s * 32, 32), 32)
            dain = dain_ref[rows, :].astype(f32)
            c1 = cu1_ref[rows, :].astype(f32)
            zc = zc_ref[rows, :].astype(f32)
            xc = c1 - jnp.mean(c1, axis=-1, keepdims=True)
            var = jnp.mean(xc * xc, axis=-1, keepdims=True)
            rstd = lax.rsqrt(var + LN_EPS)
            xh = xc * rstd
            ln = xh * g_ref[...] + bb_ref[...]
            sl_ = _sig(ln)
            sz = _sig(zc)
            dp_ref[rows, 2 * CW:3 * CW] = (dain * (ln * sl_) * (sz * (1.0 + zc * (1.0 - sz)))).astype(bf16)
            d_ln = dain * (zc * sz) * (sl_ * (1.0 + ln * (1.0 - sl_)))
            dlng_ref[...] += jnp.sum(d_ln * xh, axis=0, keepdims=True)
            dlnb_ref[...] += jnp.sum(d_ln, axis=0, keepdims=True)
            dxh = d_ln * g_ref[...]
            d_c1 = rstd * (dxh - jnp.mean(dxh, axis=-1, keepdims=True)
                           - xh * jnp.mean(dxh * xh, axis=-1, keepdims=True))
            dbias_ref[...] += jnp.sum(d_c1, axis=0, keepdims=True)
            _put_blocked(dbuf, pl.multiple_of(s * 32, 32), 32, d_c1)
            _put_blocked(ebuf, pl.multiple_of(NH * 8 + s * 32, 32), 32,
                         ca_ref[rows, :].astype(f32) * _sig(cb_ref[rows, :].astype(f32)))
            return carry
        lax.fori_loop(0, TC // 32, lnb, 0, unroll=4)
        sub = lax.broadcasted_iota(jnp.int32, (8, 128), 0)
        def after(p, carry):
            for lb in range(NLB):
                cur = dbuf[lb, _rows8(p), :]
                dbuf[lb, _rows8(R + p), :] = jnp.where(sub == 7, pltpu.roll(nxt[lb, _rows8(p), :], 7, 0),
                                                       pltpu.roll(cur, 7, 0))
            return carry
        lax.fori_loop(0, NH, after, 0)
        nxt[...] = dbuf[:, 0:NH * 8, :]
        def before(s, carry):
            rows = pl.ds(pl.multiple_of(s * 64, 64), 64)
            v = cah_ref[rows, :].astype(f32) * _sig(cbh_ref[rows, :].astype(f32))
            _put_blocked(prev, pl.multiple_of(s * 64, 64), 64, jnp.where(i == nc - 1, jnp.zeros_like(v), v))
            return carry
        lax.fori_loop(0, NH * 8 // 64, before, 0)
        _fill_before(ebuf, prev)
        for lb in range(NLB):
            sl = slice(lb * 128, (lb + 1) * 128)
            wk = [jnp.broadcast_to(w_ref[k:k + 1, sl], (8, 128)) for k in range(KS)]
            def tap(q, carry, lb=lb, wk=wk):
                r = q * RPI
                for j, o in enumerate(_fir(dbuf, lb, r, wk, None, True)):
                    dcu0[lb, _rows8(r + j), :] = o
                return carry
            lax.fori_loop(0, R // RPI, tap, 0)
            def wgrad(q, accs, lb=lb):
                r = q * RPI
                dvs = dbuf[lb, pl.ds(pl.multiple_of(r * 8, 8), RPI * 8), :]
                win = ebuf[lb, pl.ds(pl.multiple_of((r + (NH - KS + 1)) * 8, 8), (KS + RPI - 1) * 8), :]
                accs = list(accs)
                for j in range(RPI):
                    dv = dvs[8 * j:8 * j + 8, :]
                    for k in range(KS):
                        accs[k] = accs[k] + dv * win[8 * (j + k):8 * (j + k) + 8, :]
                return tuple(accs)
            accs = lax.fori_loop(0, R // RPI, wgrad, tuple(jnp.zeros((8, 128), f32) for _ in range(KS)))
            for k in range(KS):
                dw_ref[k, :, sl] += accs[k]
        def glub(s, carry):
            rows = pl.ds(pl.multiple_of(s * 64, 64), 64)
            d0 = _get_blocked(dcu0, pl.multiple_of(s * 64, 64), 64)
            ca = ca_ref[rows, :].astype(f32)
            sb = _sig(cb_ref[rows, :].astype(f32))
            dp_ref[rows, 0:CW] = (d0 * sb).astype(bf16)
            dp_ref[rows, CW:2 * CW] = (d0 * ca * sb * (1.0 - sb)).astype(bf16)
            return carry
        lax.fori_loop(0, TC // 64, glub, 0)

    hrows = NH * 8
    per = TC // hrows
    rev = lambda cidx: pl.BlockSpec((TC, CW), lambda i, cidx=cidx: (nc - 1 - i, cidx))
    halo = lambda cidx: pl.BlockSpec((hrows, CW), lambda i, cidx=cidx: (jnp.maximum((nc - 1 - i) * per - 1, 0), cidx))
    one = lambda shape: pl.BlockSpec(shape, lambda i: (0,) * len(shape))
    return pl.pallas_call(
        body, grid=(nc,),
        in_specs=[rev(0), rev(0), rev(0), rev(1), rev(2), halo(0), halo(1), one((32, CW)), one((1, CW)), one((1, CW)),
                  _ANY],
        out_specs=[pl.BlockSpec((TC, 3 * CW), lambda i: (nc - 1 - i, 0)), one((32, 8, CW)), one((1, CW)), one((1, CW)), one((1, CW))],
        out_shape=[jax.ShapeDtypeStruct((L, IN_W), bf16), jax.ShapeDtypeStruct((32, 8, CW), f32),
                   jax.ShapeDtypeStruct((1, CW), f32), jax.ShapeDtypeStruct((1, CW), f32),
                   jax.ShapeDtypeStruct((1, CW), f32)],
        scratch_shapes=[pltpu.VMEM((NLB, (R + NH) * 8, 128), f32), pltpu.VMEM((NLB, (NH + R) * 8, 128), f32),
                        pltpu.VMEM((NLB, hrows, 128), f32), pltpu.VMEM((NLB, hrows, 128), f32),
                        pltpu.VMEM((NLB, TC, 128), f32)],
        input_output_aliases={10: 0},
        name="conv_bwd", compiler_params=_cp("arbitrary"))(d_ain, cu1, proj, proj, proj, proj, proj, cw, lng, lnb, dproj)


def _win_grad(h, dproj):
    L = h.shape[0]
    tm = min(1024, L)
    nt = L // tm
    def body(h_ref, d_ref, o_ref, acc):
        i = pl.program_id(1)
        @pl.when(i == 0)
        def _():
            acc[...] = jnp.zeros_like(acc)
        acc[...] += _dot_tn(h_ref[...], d_ref[...])
        @pl.when(i == nt - 1)
        def _():
            o_ref[0] = acc[...].astype(bf16)
    return pl.pallas_call(
        body, grid=(NCHIP, nt),
        in_specs=[pl.BlockSpec((tm, D), lambda j, i: (i, 0)), pl.BlockSpec((tm, SHARD_W), lambda j, i: (i, j))],
        out_specs=pl.BlockSpec((1, D, SHARD_W), lambda j, i: (j, 0, 0)),
        out_shape=jax.ShapeDtypeStruct((NCHIP, D, SHARD_W), bf16),
        scratch_shapes=[pltpu.VMEM((D, SHARD_W), f32)],
        name="win_grad", compiler_params=_cp("arbitrary", "arbitrary"))(h, dproj)


def _x_grad(dproj, w_in, x, gx0, g_pre):
    L = x.shape[0]
    tm = 256
    def body(d_ref, w_ref, x_ref, gx_ref, g_ref, o_ref, dg_ref):
        @pl.when(pl.program_id(0) == 0)
        def _():
            dg_ref[...] = jnp.zeros_like(dg_ref)
        dh = _dot_nt(d_ref[:, 0:SHARD_W], w_ref[0])
        for j in range(1, NCHIP):
            dh = dh + _dot_nt(d_ref[:, j * SHARD_W:(j + 1) * SHARD_W], w_ref[j])
        xt = x_ref[...]
        r = lax.rsqrt(jnp.mean(xt * xt, axis=-1, keepdims=True) + RMS_EPS)
        xn = xt * r
        dg_ref[...] += jnp.sum(dh * xn, axis=0, keepdims=True)
        dxn = dh * g_ref[...]
        o_ref[...] = gx_ref[...] + r * (dxn - xn * jnp.mean(dxn * xn, axis=-1, keepdims=True))
    return pl.pallas_call(
        body, grid=(L // tm,),
        in_specs=[pl.BlockSpec((tm, IN_W), lambda i: (i, 0)),
                  pl.BlockSpec((NCHIP, D, SHARD_W), lambda i: (0, 0, 0), pipeline_mode=pl.Buffered(1)),
                  pl.BlockSpec((tm, D), lambda i: (i, 0)), pl.BlockSpec((tm, D), lambda i: (i, 0)), _full((1, D))],
        out_specs=[pl.BlockSpec((tm, D), lambda i: (i, 0)), _full((1, D))],
        out_shape=[jax.ShapeDtypeStruct((L, D), f32), jax.ShapeDtypeStruct((1, D), f32)],
        name="x_grad", compiler_params=_cp("arbitrary"))(dproj, w_in, x, gx0, g_pre)


def _pair_sum(c_arr, ga, ra, gb, rb, gs, rs):
    def body(c_ref, ga_ref, ra_ref, gb_ref, rb_ref, gs_ref, rs_ref, pa_ref, pb_ref, ps_ref):
        pa_ref[...] = (ga_ref[...] + ra_ref[...]).astype(bf16)
        pb_ref[...] = (gb_ref[...] + rb_ref[...]).astype(bf16)
        ps_ref[...] = gs_ref[...] + rs_ref[...]
    grid_spec = pltpu.PrefetchScalarGridSpec(
        num_scalar_prefetch=1, grid=(NCHIP,),
        in_specs=[pl.BlockSpec((1, D // 2, SHARD_W), lambda j, c: (j, c[0], 0)),
                  pl.BlockSpec((1, D // 2, SHARD_W), lambda j, c: (j, 0, 0)),
                  pl.BlockSpec((1, REST_ROWS // 2, 1024), lambda j, c: (j, c[0], 0)),
                  pl.BlockSpec((1, REST_ROWS // 2, 1024), lambda j, c: (j, 0, 0)),
                  pl.BlockSpec((SMALL_ROWS, 128), lambda j, c: (0, 0)),
                  pl.BlockSpec((SMALL_ROWS, 128), lambda j, c: (0, 0))],
        out_specs=[pl.BlockSpec((1, D // 2, SHARD_W), lambda j, c: (j, 0, 0)),
                   pl.BlockSpec((1, REST_ROWS // 2, 1024), lambda j, c: (j, 0, 0)),
                   pl.BlockSpec((SMALL_ROWS, 128), lambda j, c: (0, 0))])
    return pl.pallas_call(
        body, grid_spec=grid_spec,
        out_shape=[jax.ShapeDtypeStruct((NCHIP, D // 2, SHARD_W), bf16),
                   jax.ShapeDtypeStruct((NCHIP, REST_ROWS // 2, 1024), bf16),
                   jax.ShapeDtypeStruct((SMALL_ROWS, 128), f32)],
        name="pair_sum", compiler_params=_cp("arbitrary"))(c_arr, ga, ra, gb, rb, gs, rs)


def _chip_sum(qa, qb, qs):
    nt = 4
    def body(qa_ref, qb_ref, qs_ref, fa_ref, fb_ref, fs_ref):
        for q_ref, f_ref in ((qa_ref, fa_ref), (qb_ref, fb_ref), (qs_ref, fs_ref)):
            acc = q_ref[0].astype(f32)
            for j in range(1, NCHIP):
                acc = acc + q_ref[j].astype(f32)
            f_ref[...] = acc
    ra, rb, rs = D // 2 // nt, REST_ROWS // 2 // nt, SMALL_ROWS // nt
    return pl.pallas_call(
        body, grid=(nt,),
        in_specs=[pl.BlockSpec((NCHIP, ra, SHARD_W), lambda i: (0, i, 0)),
                  pl.BlockSpec((NCHIP, rb, 1024), lambda i: (0, i, 0)),
                  pl.BlockSpec((NCHIP, rs, 128), lambda i: (0, i, 0))],
        out_specs=[pl.BlockSpec((ra, SHARD_W), lambda i: (i, 0)), pl.BlockSpec((rb, 1024), lambda i: (i, 0)),
                   pl.BlockSpec((rs, 128), lambda i: (i, 0))],
        out_shape=[jax.ShapeDtypeStruct((D // 2, SHARD_W), f32), jax.ShapeDtypeStruct((REST_ROWS // 2, 1024), f32),
                   jax.ShapeDtypeStruct((SMALL_ROWS, 128), f32)],
        name="chip_sum", compiler_params=_cp("arbitrary"))(qa, qb, qs)


def _adamw_math(w, g, m, v):
    m2 = B1 * m + (1.0 - B1) * g
    v2 = B2 * v + (1.0 - B2) * (g * g)
    m_hat = m2 / (1.0 - B1 ** STEP)
    v_hat = v2 / (1.0 - B2 ** STEP)
    delta = -LR * (m_hat / (jnp.sqrt(v_hat) + EPS) + WD * w)
    return delta, m2, v2


def _adamw(name, w, g, m, v):
    rows, cols = w.shape
    tm = rows if rows <= 256 else (256 if rows % 256 == 0 else 128)
    assert rows % tm == 0
    def body(w_ref, g_ref, m_ref, v_ref, d_ref, m2_ref, v2_ref):
        d, m2, v2 = _adamw_math(w_ref[...], g_ref[...], m_ref[...], v_ref[...])
        d_ref[...] = d
        m2_ref[...] = m2
        v2_ref[...] = v2
    spec = pl.BlockSpec((tm, cols), lambda i: (i, 0))
    shp = jax.ShapeDtypeStruct((rows, cols), f32)
    return pl.pallas_call(
        body, grid=(rows // tm,), in_specs=[spec] * 4, out_specs=[spec] * 3, out_shape=[shp] * 3,
        name=name, compiler_params=_cp("arbitrary"))(w, g, m, v)


_ANY = pl.BlockSpec(memory_space=pl.ANY)


def _chunks(rows, parts):
    step = rows // parts
    assert step * parts == rows and step % 16 == 0
    return [(i * step, step) for i in range(parts)]


def _place():
    x, y, c = lax.axis_index("x"), lax.axis_index("y"), lax.axis_index("c")
    chips = [(1 - x, y), (x, 1 - y), (1 - x, 1 - y)]
    return x, y, c, chips


def _gather_weights(win_s, rest_s):
    segs = [(0, D // 2, r0, n) for r0, n in _chunks(D // 2, 4)] + \
           [(1, REST_ROWS // 2, r0, n) for r0, n in _chunks(REST_ROWS // 2, 2)]
    ns = len(segs)
    def body(a_ref, b_ref, oa_ref, ob_ref, send_sems, recv_sems, local_sems):
        x, y, c, chips = _place()
        k = 2 * x + y
        sibling = (x, y, 1 - c)
        ins, outs = (a_ref, b_ref), (oa_ref, ob_ref)

        def dst(which, half, chip, pc, r0, n):
            return outs[which].at[chip, pl.ds(pc * half + r0, n), :]

        def rcopy(i, src, dst_ref, to):
            return pltpu.make_async_remote_copy(src_ref=src, dst_ref=dst_ref, send_sem=send_sems.at[i],
                                                recv_sem=recv_sems.at[i], device_id=to, device_id_type=MESH)

        own = [pltpu.make_async_copy(ins[w], outs[w].at[k], local_sems.at[w]) for w in range(2)]
        for cp in own:
            cp.start()
        first = []
        for j, chip in enumerate(chips):
            for s, (w, half, r0, n) in enumerate(segs):
                first.append(rcopy(j * ns + s, ins[w].at[pl.ds(c * half + r0, n), :], dst(w, half, k, c, r0, n),
                                   (*chip, c)))
        for cp in first:
            cp.start()
        passed = []
        for j, chip in enumerate(chips):
            cj = 2 * chip[0] + chip[1]
            for s, (w, half, r0, n) in enumerate(segs):
                landed = dst(w, half, cj, c, r0, n)
                rcopy(j * ns + s, landed, landed, (x, y, c)).wait_recv()
                fwd = rcopy(3 * ns + j * ns + s, landed, landed, sibling)
                fwd.start()
                passed.append(fwd)
        for j, chip in enumerate(chips):
            cj = 2 * chip[0] + chip[1]
            for s, (w, half, r0, n) in enumerate(segs):
                theirs = dst(w, half, cj, 1 - c, r0, n)
                rcopy(3 * ns + j * ns + s, theirs, theirs, (x, y, c)).wait_recv()
        for cp in first + passed:
            cp.wait_send()
        for cp in own:
            cp.wait()

    return pl.pallas_call(
        body, in_specs=[_ANY, _ANY], out_specs=[_ANY, _ANY],
        out_shape=[jax.ShapeDtypeStruct((NCHIP, D, SHARD_W), bf16), jax.ShapeDtypeStruct((NCHIP, REST_ROWS, 1024), bf16)],
        scratch_shapes=[pltpu.SemaphoreType.DMA((6 * ns,)), pltpu.SemaphoreType.DMA((6 * ns,)),
                        pltpu.SemaphoreType.DMA((2,))],
        name="gather_weights")(win_s, rest_s)


def _pair_exchange(ga, gb, gs):
    ha, hb = D // 2, REST_ROWS // 2
    def body(a_ref, b_ref, s_ref, ra_ref, rb_ref, rs_ref, send_sems, recv_sems):
        x, y, c, _ = _place()
        sibling = (x, y, 1 - c)
        pieces = []
        for j in range(NCHIP):
            for r0, n in _chunks(ha, 4):
                pieces.append((a_ref.at[j, pl.ds((1 - c) * ha + r0, n), :], ra_ref.at[j, pl.ds(r0, n), :]))
            for r0, n in _chunks(hb, 2):
                pieces.append((b_ref.at[j, pl.ds((1 - c) * hb + r0, n), :], rb_ref.at[j, pl.ds(r0, n), :]))
        pieces.append((s_ref, rs_ref))
        copies = [pltpu.make_async_remote_copy(src_ref=s, dst_ref=d, send_sem=send_sems.at[i], recv_sem=recv_sems.at[i],
                                               device_id=sibling, device_id_type=MESH)
                  for i, (s, d) in enumerate(pieces)]
        for cp in copies:
            cp.start()
        for cp in copies:
            cp.wait_recv()
        for cp in copies:
            cp.wait_send()

    n = NCHIP * 6 + 1
    return pl.pallas_call(
        body, in_specs=[_ANY, _ANY, _ANY], out_specs=[_ANY, _ANY, _ANY],
        out_shape=[jax.ShapeDtypeStruct((NCHIP, ha, SHARD_W), f32), jax.ShapeDtypeStruct((NCHIP, hb, 1024), f32),
                   jax.ShapeDtypeStruct((SMALL_ROWS, 128), f32)],
        scratch_shapes=[pltpu.SemaphoreType.DMA((n,)), pltpu.SemaphoreType.DMA((n,))],
        name="pair_exchange")(ga, gb, gs)


def _chip_exchange(pa, pb, ps):
    ha, hb = D // 2, REST_ROWS // 2
    def body(a_ref, b_ref, s_ref, qa_ref, qb_ref, qs_ref, send_sems, recv_sems, local_sems):
        x, y, c, chips = _place()
        k = 2 * x + y
        own = [pltpu.make_async_copy(a_ref.at[k], qa_ref.at[k], local_sems.at[0]),
               pltpu.make_async_copy(b_ref.at[k], qb_ref.at[k], local_sems.at[1]),
               pltpu.make_async_copy(s_ref, qs_ref.at[k], local_sems.at[2])]
        for cp in own:
            cp.start()
        copies = []
        for j, chip in enumerate(chips):
            cj = 2 * chip[0] + chip[1]
            pieces = [(a_ref.at[cj, pl.ds(r0, n), :], qa_ref.at[k, pl.ds(r0, n), :]) for r0, n in _chunks(ha, 2)]
            pieces += [(b_ref.at[cj], qb_ref.at[k]), (s_ref, qs_ref.at[k])]
            for s, (src, dst_ref) in enumerate(pieces):
                copies.append(pltpu.make_async_remote_copy(
                    src_ref=src, dst_ref=dst_ref, send_sem=send_sems.at[4 * j + s], recv_sem=recv_sems.at[4 * j + s],
                    device_id=(*chip, c), device_id_type=MESH))
        for cp in copies:
            cp.start()
        for cp in copies:
            cp.wait_recv()
        for cp in copies:
            cp.wait_send()
        for cp in own:
            cp.wait()

    return pl.pallas_call(
        body, in_specs=[_ANY, _ANY, _ANY], out_specs=[_ANY, _ANY, _ANY],
        out_shape=[jax.ShapeDtypeStruct((NCHIP, ha, SHARD_W), bf16), jax.ShapeDtypeStruct((NCHIP, hb, 1024), bf16),
                   jax.ShapeDtypeStruct((NCHIP, SMALL_ROWS, 128), f32)],
        scratch_shapes=[pltpu.SemaphoreType.DMA((12,)), pltpu.SemaphoreType.DMA((12,)), pltpu.SemaphoreType.DMA((3,))],
        name="chip_exchange")(pa, pb, ps)


def _sibling_exchange(fa, fb):
    ha, hb = D // 2, REST_ROWS // 2
    def body(a_ref, b_ref, oa_ref, ob_ref, send_sems, recv_sems, local_sems):
        x, y, c, _ = _place()
        own = [pltpu.make_async_copy(a_ref, oa_ref.at[c], local_sems.at[0]),
               pltpu.make_async_copy(b_ref, ob_ref.at[c], local_sems.at[1])]
        for cp in own:
            cp.start()
        pieces = [(a_ref.at[pl.ds(r0, n), :], oa_ref.at[c, pl.ds(r0, n), :]) for r0, n in _chunks(ha, 4)]
        pieces += [(b_ref.at[pl.ds(r0, n), :], ob_ref.at[c, pl.ds(r0, n), :]) for r0, n in _chunks(hb, 2)]
        copies = [pltpu.make_async_remote_copy(src_ref=s, dst_ref=d, send_sem=send_sems.at[i], recv_sem=recv_sems.at[i],
                                               device_id=(x, y, 1 - c), device_id_type=MESH)
                  for i, (s, d) in enumerate(pieces)]
        for cp in copies:
            cp.start()
        for cp in copies:
            cp.wait_recv()
        for cp in copies:
            cp.wait_send()
        for cp in own:
            cp.wait()

    return pl.pallas_call(
        body, in_specs=[_ANY, _ANY], out_specs=[_ANY, _ANY],
        out_shape=[jax.ShapeDtypeStruct((2, ha, SHARD_W), f32), jax.ShapeDtypeStruct((2, hb, 1024), f32)],
        scratch_shapes=[pltpu.SemaphoreType.DMA((6,)), pltpu.SemaphoreType.DMA((6,)), pltpu.SemaphoreType.DMA((2,))],
        name="sibling_exchange")(fa, fb)


def _nchunks(half, cols, itemsize):
    return 4 if half * cols * itemsize >= (1 << 20) else 1


def _segments(metas):
    segs = []
    for w, (half, cols, dt) in enumerate(metas):
        for r0, n in _chunks(half, _nchunks(half, cols, jnp.dtype(dt).itemsize)):
            segs.append((w, half, r0, n))
    return segs


def _rcopy(i, src, dst, send_sems, recv_sems, to):
    return pltpu.make_async_remote_copy(src_ref=src, dst_ref=dst, send_sem=send_sems.at[i], recv_sem=recv_sems.at[i],
                                        device_id=to, device_id_type=MESH)


def _gather_list(shards):
    na = len(shards)
    segs = _segments([(a.shape[0] // 2, a.shape[1], a.dtype) for a in shards])
    ns = len(segs)
    def body(*refs):
        ins, outs, (send_sems, recv_sems) = refs[:na], refs[na:2 * na], refs[2 * na:]
        x, y, c, chips = _place()
        k = 2 * x + y
        me, sibling = (x, y, c), (x, y, 1 - c)

        def dst(w, half, chip, pc, r0, n):
            return outs[w].at[chip, pl.ds(pc * half + r0, n), :]

        first = []
        for j, chip in enumerate(chips):
            for s, (w, half, r0, n) in enumerate(segs):
                first.append(_rcopy(j * ns + s, ins[w].at[pl.ds(c * half + r0, n), :], dst(w, half, k, c, r0, n),
                                    send_sems, recv_sems, (*chip, c)))
        for cp in first:
            cp.start()
        passed = []
        for j, chip in enumerate(chips):
            cj = 2 * chip[0] + chip[1]
            for s, (w, half, r0, n) in enumerate(segs):
                landed = dst(w, half, cj, c, r0, n)
                _rcopy(j * ns + s, landed, landed, send_sems, recv_sems, me).wait_recv()
                fwd = _rcopy(3 * ns + j * ns + s, landed, landed, send_sems, recv_sems, sibling)
                fwd.start()
                passed.append(fwd)
        for j, chip in enumerate(chips):
            cj = 2 * chip[0] + chip[1]
            for s, (w, half, r0, n) in enumerate(segs):
                theirs = dst(w, half, cj, 1 - c, r0, n)
                _rcopy(3 * ns + j * ns + s, theirs, theirs, send_sems, recv_sems, me).wait_recv()
        for cp in first + passed:
            cp.wait_send()

    return pl.pallas_call(
        body, in_specs=[_ANY] * na, out_specs=[_ANY] * na,
        out_shape=[jax.ShapeDtypeStruct((NCHIP,) + a.shape, a.dtype) for a in shards],
        scratch_shapes=[pltpu.SemaphoreType.DMA((6 * ns,)), pltpu.SemaphoreType.DMA((6 * ns,))],
        name="gather_weights")(*shards)


def _gather_prep(k_arr, shards, x, tgt, g_pre, perm):
    na = len(shards)
    L = x.shape[0]
    nc = L // TC
    segs = _segments([(a.shape[0] // 2, a.shape[1], a.dtype) for a in shards])
    ns = len(segs)
    def body(_, *refs):
        ins = refs[:na]
        x_ref, t_ref, g_ref, p_ref = refs[na:na + 4]
        outs = refs[na + 4:2 * na + 4]
        h_ref, xi_ref, ti_ref, proj_ref = refs[2 * na + 4:2 * na + 8]
        stages = refs[2 * na + 8:3 * na + 8]
        send_sems, recv_sems, local_sems = refs[3 * na + 8:]
        i = pl.program_id(0)
        x, y, c, chips = _place()
        k = 2 * x + y
        me, sibling = (x, y, c), (x, y, 1 - c)

        def dst(w, half, chip, pc, r0, n):
            return outs[w].at[chip, pl.ds(pc * half + r0, n), :]

        def firsts():
            return [_rcopy(j * ns + s, ins[w].at[pl.ds(c * half + r0, n), :], dst(w, half, k, c, r0, n),
                           send_sems, recv_sems, (*chip, c))
                    for j, chip in enumerate(chips) for s, (w, half, r0, n) in enumerate(segs)]

        def own_out(w):
            return pltpu.make_async_copy(stages[w], outs[w].at[k], local_sems.at[w])

        @pl.when(i == 0)
        def _():
            for cp in firsts():
                cp.start()
            for w in range(na):
                cin = pltpu.make_async_copy(ins[w], stages[w], local_sems.at[w])
                cin.start()
                cin.wait()
            for w in range(na):
                own_out(w).start()

        p = p_ref[...]
        def through(v):
            hi = v.astype(bf16)
            r1 = v - hi.astype(f32)
            mid = r1.astype(bf16)
            lo = (r1 - mid.astype(f32)).astype(bf16)
            return (_dot(p, hi) + _dot(p, mid)) + _dot(p, lo)
        xt = x_ref[...]
        r = lax.rsqrt(jnp.mean(xt * xt, axis=-1, keepdims=True) + RMS_EPS)
        hp = _dot(p, (xt * r * g_ref[...]).astype(bf16)).astype(bf16)
        h_ref[...] = hp
        proj_ref[...] = _dot(hp, stages[0][...]).astype(bf16)
        xi_ref[...] = through(xt)
        ti_ref[...] = through(t_ref[...])

        @pl.when(i == nc - 1)
        def _():
            passed = []
            for j, chip in enumerate(chips):
                cj = 2 * chip[0] + chip[1]
                for s, (w, half, r0, n) in enumerate(segs):
                    landed = dst(w, half, cj, c, r0, n)
                    _rcopy(j * ns + s, landed, landed, send_sems, recv_sems, me).wait_recv()
                    fwd = _rcopy(3 * ns + j * ns + s, landed, landed, send_sems, recv_sems, sibling)
                    fwd.start()
                    passed.append(fwd)
            for j, chip in enumerate(chips):
                cj = 2 * chip[0] + chip[1]
                for s, (w, half, r0, n) in enumerate(segs):
                    theirs = dst(w, half, cj, 1 - c, r0, n)
                    _rcopy(3 * ns + j * ns + s, theirs, theirs, send_sems, recv_sems, me).wait_recv()
            for cp in firsts() + passed:
                cp.wait_send()
            for w in range(na):
                own_out(w).wait()

    row = lambda: pl.BlockSpec((TC, D), lambda i, k: (i, 0))
    grid_spec = pltpu.PrefetchScalarGridSpec(
        num_scalar_prefetch=1, grid=(nc,),
        in_specs=[_ANY] * na + [row(), row(), pl.BlockSpec((1, D), lambda i, k: (0, 0)),
                                pl.BlockSpec((TC, TC), lambda i, k: (0, 0))],
        out_specs=[_ANY] * na + [row(), row(), row(), pl.BlockSpec((TC, SHARD_W), lambda i, k: (i, k[0]))],
        scratch_shapes=[pltpu.VMEM(a.shape, a.dtype) for a in shards]
        + [pltpu.SemaphoreType.DMA((6 * ns,)), pltpu.SemaphoreType.DMA((6 * ns,)), pltpu.SemaphoreType.DMA((na,))])
    return pl.pallas_call(
        body, grid_spec=grid_spec,
        out_shape=[jax.ShapeDtypeStruct((NCHIP,) + a.shape, a.dtype) for a in shards]
        + [jax.ShapeDtypeStruct((L, D), bf16), jax.ShapeDtypeStruct((L, D), f32), jax.ShapeDtypeStruct((L, D), f32),
           jax.ShapeDtypeStruct((L, IN_W), bf16)],
        name="gather_prep", compiler_params=_cp("arbitrary"))(k_arr, *shards, x, tgt, g_pre, perm)


def _x_grad_exchange(dproj, w_in, x, gx0, g_pre, parts, small):
    L = x.shape[0]
    tm = 512
    nt = L // tm
    na = len(parts)
    segs = _segments([(p.shape[1], p.shape[2], p.dtype) for p in parts])
    ns = len(segs) + 1
    def body(*refs):
        d_ref, w_ref, x_ref, gx_ref, g_ref = refs[:5]
        ins, s_ref = refs[5:5 + na], refs[5 + na]
        o_ref, dg_ref = refs[6 + na:8 + na]
        outs, qs_ref = refs[8 + na:8 + 2 * na], refs[8 + 2 * na]
        stages = refs[9 + 2 * na:10 + 3 * na]
        send_sems, recv_sems, local_sems = refs[10 + 3 * na:]
        i = pl.program_id(0)
        x, y, c, chips = _place()
        k = 2 * x + y

        def copies():
            out = []
            for j, chip in enumerate(chips):
                cj = 2 * chip[0] + chip[1]
                pieces = [(s_ref, qs_ref.at[k])]
                pieces += [(ins[w].at[cj, pl.ds(r0, n), :], outs[w].at[k, pl.ds(r0, n), :]) for w, _, r0, n in segs]
                out += [_rcopy(ns * j + s, src, d, send_sems, recv_sems, (*chip, c)) for s, (src, d) in enumerate(pieces)]
            return out

        def own_out(w):
            dst = qs_ref.at[k] if w == na else outs[w].at[k]
            return pltpu.make_async_copy(stages[w], dst, local_sems.at[w])

        @pl.when(i == 0)
        def _():
            dg_ref[...] = jnp.zeros_like(dg_ref)
            for cp in copies():
                cp.start()
            for w in range(na + 1):
                cin = pltpu.make_async_copy(s_ref if w == na else ins[w].at[k], stages[w], local_sems.at[w])
                cin.start()
                cin.wait()
            for w in range(na + 1):
                own_out(w).start()

        dh = _dot_nt(d_ref[:, 0:SHARD_W], w_ref[0])
        for j in range(1, NCHIP):
            dh = dh + _dot_nt(d_ref[:, j * SHARD_W:(j + 1) * SHARD_W], w_ref[j])
        xt = x_ref[...]
        r = lax.rsqrt(jnp.mean(xt * xt, axis=-1, keepdims=True) + RMS_EPS)
        xn = xt * r
        dg_ref[...] += jnp.sum(dh * xn, axis=0, keepdims=True)
        dxn = dh * g_ref[...]
        o_ref[...] = gx_ref[...] + r * (dxn - xn * jnp.mean(dxn * xn, axis=-1, keepdims=True))

        @pl.when(i == nt - 1)
        def _():
            for cp in copies():
                cp.wait_recv()
            for cp in copies():
                cp.wait_send()
            for w in range(na + 1):
                own_out(w).wait()

    return pl.pallas_call(
        body, grid=(nt,),
        in_specs=[pl.BlockSpec((tm, IN_W), lambda i: (i, 0)),
                  pl.BlockSpec((NCHIP, D, SHARD_W), lambda i: (0, 0, 0), pipeline_mode=pl.Buffered(1)),
                  pl.BlockSpec((tm, D), lambda i: (i, 0)), pl.BlockSpec((tm, D), lambda i: (i, 0)), _full((1, D))]
        + [_ANY] * (na + 1),
        out_specs=[pl.BlockSpec((tm, D), lambda i: (i, 0)), _full((1, D))] + [_ANY] * (na + 1),
        out_shape=[jax.ShapeDtypeStruct((L, D), f32), jax.ShapeDtypeStruct((1, D), f32)]
        + [jax.ShapeDtypeStruct(p.shape, bf16) for p in parts] + [jax.ShapeDtypeStruct((NCHIP, SMALL_ROWS, 128), f32)],
        scratch_shapes=[pltpu.VMEM(p.shape[1:], bf16) for p in parts] + [pltpu.VMEM((SMALL_ROWS, 128), f32)]
        + [pltpu.SemaphoreType.DMA((3 * ns,)), pltpu.SemaphoreType.DMA((3 * ns,)), pltpu.SemaphoreType.DMA((na + 1,))],
        name="x_grad_exchange", compiler_params=_cp("arbitrary"))(dproj, w_in, x, gx0, g_pre, *parts, small)


def _sibling_join_list(halves):
    na = len(halves)
    segs = _segments([(h.shape[0], h.shape[1], h.dtype) for h in halves])
    def body(*refs):
        ins, outs, stages = refs[:na], refs[na:2 * na], refs[2 * na:3 * na]
        send_sems, recv_sems, local_sems = refs[3 * na:]
        x, y, c, _ = _place()
        copies = [_rcopy(i, ins[w].at[pl.ds(r0, n), :], outs[w].at[pl.ds(c * half + r0, n), :], send_sems, recv_sems,
                         (x, y, 1 - c)) for i, (w, half, r0, n) in enumerate(segs)]
        for cp in copies:
            cp.start()
        own = []
        for w in range(na):
            cin = pltpu.make_async_copy(ins[w], stages[w], local_sems.at[w])
            cin.start()
            cin.wait()
            half = halves[w].shape[0]
            own.append(pltpu.make_async_copy(stages[w], outs[w].at[pl.ds(c * half, half), :], local_sems.at[w]))
            own[-1].start()
        for cp in copies:
            cp.wait_recv()
        for cp in copies:
            cp.wait_send()
        for cp in own:
            cp.wait()

    return pl.pallas_call(
        body, in_specs=[_ANY] * na, out_specs=[_ANY] * na,
        out_shape=[jax.ShapeDtypeStruct((2 * h.shape[0], h.shape[1]), f32) for h in halves],
        scratch_shapes=[pltpu.VMEM(h.shape, f32) for h in halves]
        + [pltpu.SemaphoreType.DMA((len(segs),)), pltpu.SemaphoreType.DMA((len(segs),)), pltpu.SemaphoreType.DMA((na,))],
        name="sibling_join")(*halves)


def _allgather_rows(v):
    def body(v_ref, o_ref, send_sems, recv_sems):
        x, y, c, _ = _place()
        me = 4 * x + 2 * y + c
        o_ref[me] = v_ref[...]
        copies = []
        i = 0
        for dx in range(2):
            for dy in range(2):
                for dc in range(2):
                    if dx + dy + dc:
                        copies.append(_rcopy(i, v_ref, o_ref.at[me], send_sems, recv_sems, (x ^ dx, y ^ dy, c ^ dc)))
                        i += 1
        for cp in copies:
            cp.start()
        for cp in copies:
            cp.wait_recv()
        for cp in copies:
            cp.wait_send()

    vm = pl.BlockSpec(memory_space=pltpu.VMEM)
    return pl.pallas_call(
        body, in_specs=[vm], out_specs=vm, out_shape=jax.ShapeDtypeStruct((8, 8, 128), f32),
        scratch_shapes=[pltpu.SemaphoreType.DMA((7,)), pltpu.SemaphoreType.DMA((7,))],
        name="allgather_rows")(v)


def _adamw_rows(parts, w, m, v):
    def body(p_ref, w_ref, m_ref, v_ref, g_ref, d_ref, m2_ref, v2_ref):
        g = p_ref[0]
        for dvc in range(1, 8):
            g = g + p_ref[dvc]
        g_ref[...] = g
        d, m2, v2 = _adamw_math(w_ref[...], g, m_ref[...], v_ref[...])
        d_ref[...] = d
        m2_ref[...] = m2
        v2_ref[...] = v2
    return pl.pallas_call(body, out_shape=[jax.ShapeDtypeStruct((8, 128), f32)] * 4, name="adamw_pre_norm_gain")(
        parts, w, m, v)


def _pair_exchange_list(grads, small):
    na = len(grads)
    segs = _segments([(g.shape[1] // 2, g.shape[2], g.dtype) for g in grads])
    n = NCHIP * len(segs) + 1
    def body(*refs):
        ins, s_ref, outs, rs_ref, (send_sems, recv_sems) = (refs[:na], refs[na], refs[na + 1:2 * na + 1],
                                                            refs[2 * na + 1], refs[2 * na + 2:])
        x, y, c, _ = _place()
        pieces = [(s_ref, rs_ref)]
        for j in range(NCHIP):
            for w, half, r0, rows in segs:
                pieces.append((ins[w].at[j, pl.ds((1 - c) * half + r0, rows), :], outs[w].at[j, pl.ds(r0, rows), :]))
        copies = [_rcopy(i, s, d, send_sems, recv_sems, (x, y, 1 - c)) for i, (s, d) in enumerate(pieces)]
        for cp in copies:
            cp.start()
        for cp in copies:
            cp.wait_recv()
        for cp in copies:
            cp.wait_send()

    return pl.pallas_call(
        body, in_specs=[_ANY] * (na + 1), out_specs=[_ANY] * (na + 1),
        out_shape=[jax.ShapeDtypeStruct((NCHIP, g.shape[1] // 2, g.shape[2]), g.dtype) for g in grads]
        + [jax.ShapeDtypeStruct((SMALL_ROWS, 128), f32)],
        scratch_shapes=[pltpu.SemaphoreType.DMA((n,)), pltpu.SemaphoreType.DMA((n,))],
        name="pair_exchange")(*grads, small)


def _pair_sum_list(c_arr, grads, recvs, small, rsmall):
    na = len(grads)
    def body(c_ref, *refs):
        g_refs, r_refs, s_ref, rs_ref = refs[:na], refs[na:2 * na], refs[2 * na], refs[2 * na + 1]
        o_refs, os_ref = refs[2 * na + 2:3 * na + 2], refs[3 * na + 2]
        for g_ref, r_ref, o_ref in zip(g_refs, r_refs, o_refs):
            o_ref[...] = (g_ref[...].astype(f32) + r_ref[...].astype(f32)).astype(bf16)
        os_ref[...] = s_ref[...] + rs_ref[...]
    half = lambda g: pl.BlockSpec((1, g.shape[1] // 2, g.shape[2]), lambda j, c: (j, c[0], 0))
    low = lambda g: pl.BlockSpec((1, g.shape[1] // 2, g.shape[2]), lambda j, c: (j, 0, 0))
    sm = pl.BlockSpec((SMALL_ROWS, 128), lambda j, c: (0, 0))
    grid_spec = pltpu.PrefetchScalarGridSpec(
        num_scalar_prefetch=1, grid=(NCHIP,),
        in_specs=[half(g) for g in grads] + [low(g) for g in grads] + [sm, sm],
        out_specs=[low(g) for g in grads] + [sm])
    return pl.pallas_call(
        body, grid_spec=grid_spec,
        out_shape=[jax.ShapeDtypeStruct((NCHIP, g.shape[1] // 2, g.shape[2]), bf16) for g in grads]
        + [jax.ShapeDtypeStruct((SMALL_ROWS, 128), f32)],
        name="pair_sum", compiler_params=_cp("arbitrary"))(c_arr, *grads, *recvs, small, rsmall)


def _chip_exchange_list(parts, small):
    na = len(parts)
    segs = _segments([(p.shape[1], p.shape[2], p.dtype) for p in parts])
    ns = len(segs) + 1
    def body(*refs):
        ins, s_ref, outs, qs_ref, (send_sems, recv_sems) = (refs[:na], refs[na], refs[na + 1:2 * na + 1],
                                                            refs[2 * na + 1], refs[2 * na + 2:])
        x, y, c, chips = _place()
        k = 2 * x + y
        copies = []
        for j, chip in enumerate(chips):
            cj = 2 * chip[0] + chip[1]
            pieces = [(s_ref, qs_ref.at[k])]
            pieces += [(ins[w].at[cj, pl.ds(r0, n), :], outs[w].at[k, pl.ds(r0, n), :]) for w, _, r0, n in segs]
            copies += [_rcopy(ns * j + s, src, d, send_sems, recv_sems, (*chip, c)) for s, (src, d) in enumerate(pieces)]
        for cp in copies:
            cp.start()
        for cp in copies:
            cp.wait_recv()
        for cp in copies:
            cp.wait_send()

    return pl.pallas_call(
        body, in_specs=[_ANY] * (na + 1), out_specs=[_ANY] * (na + 1),
        out_shape=[jax.ShapeDtypeStruct(p.shape, bf16) for p in parts]
        + [jax.ShapeDtypeStruct((NCHIP, SMALL_ROWS, 128), f32)],
        scratch_shapes=[pltpu.SemaphoreType.DMA((3 * ns,)), pltpu.SemaphoreType.DMA((3 * ns,))],
        name="chip_exchange")(*parts, small)


def _chip_sum_list(parts, small):
    na = len(parts)
    nt = 2
    def body(*refs):
        for q_ref, f_ref in zip(refs[:na + 1], refs[na + 1:]):
            acc = q_ref[0].astype(f32)
            for j in range(1, NCHIP):
                acc = acc + q_ref[j].astype(f32)
            f_ref[...] = acc
    arrs = list(parts) + [small]
    return pl.pallas_call(
        body, grid=(nt,),
        in_specs=[pl.BlockSpec((NCHIP, a.shape[1] // nt, a.shape[2]), lambda i: (0, i, 0)) for a in arrs],
        out_specs=[pl.BlockSpec((a.shape[1] // nt, a.shape[2]), lambda i: (i, 0)) for a in arrs],
        out_shape=[jax.ShapeDtypeStruct(a.shape[1:], f32) for a in arrs],
        name="chip_sum", compiler_params=_cp("arbitrary"))(*arrs)


def _sibling_exchange_list(halves):
    na = len(halves)
    segs = _segments([(h.shape[0], h.shape[1], h.dtype) for h in halves])
    def body(*refs):
        ins, outs, (send_sems, recv_sems) = refs[:na], refs[na:2 * na], refs[2 * na:]
        x, y, c, _ = _place()
        copies = [_rcopy(i, ins[w].at[pl.ds(r0, n), :], outs[w].at[pl.ds(r0, n), :], send_sems, recv_sems, (x, y, 1 - c))
                  for i, (w, _, r0, n) in enumerate(segs)]
        for cp in copies:
            cp.start()
        for cp in copies:
            cp.wait_recv()
        for cp in copies:
            cp.wait_send()

    return pl.pallas_call(
        body, in_specs=[_ANY] * na, out_specs=[_ANY] * na,
        out_shape=[jax.ShapeDtypeStruct(h.shape, f32) for h in halves],
        scratch_shapes=[pltpu.SemaphoreType.DMA((len(segs),)), pltpu.SemaphoreType.DMA((len(segs),))],
        name="sibling_exchange")(*halves)


_REST_ROWS = (256, 256, 64, 128)
_CONV_PAD = 8192


def _pack_rest(mats, conv_rows, dtype):
    parts = [mats[0], mats[1], mats[2].reshape(64, 1024), mats[3].reshape(128, 1024)]
    parts = [p.astype(dtype) for p in parts] + [conv_rows]
    used = sum(p.shape[0] for p in parts)
    parts.append(jnp.zeros((REST_ROWS - used, 1024), dtype))
    return jnp.concatenate(parts, axis=0)


def _pack_rest_weights(mats, conv_w_s):
    flat = jnp.pad(conv_w_s.reshape(-1), (0, _CONV_PAD - KS * 256))
    return _pack_rest(mats, lax.bitcast_convert_type(flat, bf16).reshape(16, 1024), bf16)


def _pack_rest_grads(mats, conv_w_s):
    flat = jnp.pad(conv_w_s.reshape(-1), (0, _CONV_PAD - KS * 256))
    return _pack_rest(mats, flat.reshape(8, 1024), f32)


def _split_rest(p, conv_rows):
    o = 0
    out = []
    for rows in _REST_ROWS + (conv_rows,):
        out.append(p[..., o:o + rows, :])
        o += rows
    return out


_SMALL = (("conv_b", (1, 1024)), ("conv_ln_gain", (1, 1024)), ("conv_ln_bias", (1, 1024)),
          ("ssm_lambda_re", (1, 32, 64)), ("ssm_lambda_im", (1, 32, 64)), ("ssm_log_dt", (1, 32)),
          ("ssm_b_re", (1, 32, 64, 16)), ("ssm_b_im", (1, 32, 64, 16)), ("ssm_c_re", (1, 32, 16, 64)),
          ("ssm_c_im", (1, 32, 16, 64)), ("ssm_d", (1, 32, 16)), ("b_ssm_glu", (1, 512)), ("post_norm_gain", (1, 1024)))


def _pack_small(vals, extra=None):
    rows = []
    for v in list(vals) + ([extra] if extra is not None else []):
        flat = v.reshape(-1).astype(f32)
        n = -(-flat.shape[0] // 1024) * 1024
        rows.append(jnp.pad(flat, (0, n - flat.shape[0])).reshape(-1, 128))
    used = sum(r.shape[0] for r in rows)
    rows.append(jnp.zeros((SMALL_ROWS - used, 128), f32))
    return jnp.concatenate(rows, axis=0)


def _unpack_small(p):
    o = 0
    out = []
    for _, shape in _SMALL:
        n = int(np.prod(shape))
        nr = -(-n // 1024) * 8
        out.append(p[o:o + nr].reshape(-1)[:n].reshape(shape))
        o += nr
    return out, p[o, 0]


def _discretize(lam_re, lam_im, log_dt, b_re, b_im):
    dt = jnp.exp(log_dt)[:, None]
    mag = jnp.exp(lam_re * dt)
    ar = mag * jnp.cos(lam_im * dt)
    ai = mag * jnp.sin(lam_im * dt)
    den = lam_re * lam_re + lam_im * lam_im
    zr = ((ar - 1.0) * lam_re + ai * lam_im) / den
    zi = (ai * lam_re - (ar - 1.0) * lam_im) / den
    bbr = zr[..., None] * b_re - zi[..., None] * b_im
    bbi = zr[..., None] * b_im + zi[..., None] * b_re
    return ar, ai, bbr, bbi


_EYE8 = np.eye(8, dtype=np.float32)


def _bbt_blocks(bb):
    v = bb.reshape(4, 8, PST, H).transpose(0, 1, 3, 2)
    return jnp.einsum("bghp,gk->bghkp", v, _EYE8).reshape(4, 128, 512)


def _bbt_unblock(m):
    v = jnp.einsum("bghkp,gk->bghp", m.reshape(4, 8, H, 8, PST), _EYE8)
    return v.transpose(0, 1, 3, 2).reshape(G, PST, H)


def _ct_blocks(cc):
    v = cc.reshape(4, 8, H, PST)
    return jnp.einsum("bghp,gk->bgpkh", v, _EYE8).reshape(4, 512, 128)


def _ct_unblock(m):
    v = jnp.einsum("bgpkh,gk->bghp", m.reshape(4, 8, PST, 8, H), _EYE8)
    return v.reshape(G, H, PST)


def _perm_matrix():
    p = np.zeros((TC, TC), np.float32)
    for r in range(R):
        for seg in range(8):
            p[r * 8 + seg, seg * R + r] = 1.0
    return p


def _deinterleave(a):
    L, C = a.shape
    return a.reshape(L // TC, R, 8, C).transpose(0, 2, 1, 3).reshape(L, C)


def _fwd_bwd(h, xi, ti, proj, conv_w, w_co, w_glu, w_so, w_out, small):
    (conv_b, ln_g, ln_b, lam_re, lam_im, log_dt, b_re, b_im, c_re, c_im, dvec, b_glu, g_post) = small
    lam_re, lam_im, log_dt = lam_re[0], lam_im[0], log_dt[0]
    b_re, b_im, c_re, c_im = b_re[0], b_im[0], c_re[0], c_im[0]
    (ar, ai, bbr, bbi), disc_vjp = jax.vjp(_discretize, lam_re, lam_im, log_dt, b_re, b_im)
    a_re = ar.reshape(1, NS)
    a_im = ai.reshape(1, NS)
    dt = jnp.exp(log_dt)[:, None]
    steps = jnp.arange(1, R + 1, dtype=f32)[:, None, None]
    apow_re = (jnp.exp(steps * (lam_re * dt)) * jnp.cos(steps * (lam_im * dt))).reshape(R, NS)
    apow_im = (jnp.exp(steps * (lam_re * dt)) * jnp.sin(steps * (lam_im * dt))).reshape(R, NS)
    bbt_re, bbt_im = _bbt_blocks(bbr).astype(bf16), _bbt_blocks(bbi).astype(bf16)
    ct_re, ct_im = _ct_blocks(c_re).astype(bf16), _ct_blocks(c_im).astype(bf16)
    d_row = dvec.reshape(1, SW)
    cw32 = jnp.pad(conv_w, ((0, 1), (0, 0)))

    cu1, a_in = _conv_fwd(proj, cw32, conv_b, ln_g, ln_b)
    y0, b_in, sre, sim, cinr, cini = _ssm_fwd(proj, bbt_re, bbt_im, ct_re, ct_im, a_re, a_im,
                                              apow_re, apow_im, d_row, w_glu, b_glu)
    gx0, d_ain, d_bin, dproj, dw_out, dw_co, dw_so, dg_post, loss = _tail(
        a_in, b_in, proj, xi, ti, w_co, w_so, w_out, g_post)
    (dproj, dbbt_re, dbbt_im, dct_re, dct_im, dd, dar8, dai8, dw_glu, db_glu) = _ssm_bwd(
        d_bin, y0, proj, sre, sim, cinr, cini, bbt_re, bbt_im, ct_re, ct_im,
        a_re, a_im, apow_re, apow_im, d_row, w_glu, b_glu, dproj)
    dproj, dcw8, d_convb, d_lng, d_lnb = _conv_bwd(d_ain, cu1, proj, cw32, ln_g, ln_b, dproj)
    dw_in = _win_grad(h, dproj)

    d_ar = jnp.sum(dar8, axis=0).reshape(G, PST)
    d_ai = jnp.sum(dai8, axis=0).reshape(G, PST)
    d_lre, d_lim, d_ldt, d_bre, d_bim = disc_vjp((d_ar, d_ai, _bbt_unblock(dbbt_re), _bbt_unblock(dbbt_im)))
    d_conv_w = jnp.sum(dcw8, axis=1)[:KS]
    small_grads = [d_convb, d_lng, d_lnb, d_lre[None], d_lim[None], d_ldt[None], d_bre[None], d_bim[None],
                   _ct_unblock(dct_re)[None], _ct_unblock(dct_im)[None], dd.reshape(1, G, H), db_glu, dg_post]
    return loss[0, 0], gx0, dproj, (dw_in, dw_co, dw_out, dw_glu, dw_so, d_conv_w), small_grads


def kernel(x, pre_norm_gain, w_in, conv_w, conv_b, conv_ln_gain, conv_ln_bias, w_conv_out, ssm_lambda_re, ssm_lambda_im, ssm_log_dt, ssm_b_re, ssm_b_im, ssm_c_re, ssm_c_im, ssm_d, w_ssm_glu, b_ssm_glu, w_ssm_out, w_out, post_norm_gain, loss_target, m_pre_norm_gain, m_w_in, m_conv_w, m_conv_b, m_conv_ln_gain, m_conv_ln_bias, m_w_conv_out, m_ssm_lambda_re, m_ssm_lambda_im, m_ssm_log_dt, m_ssm_b_re, m_ssm_b_im, m_ssm_c_re, m_ssm_c_im, m_ssm_d, m_w_ssm_glu, m_b_ssm_glu, m_w_ssm_out, m_w_out, m_post_norm_gain, v_pre_norm_gain, v_w_in, v_conv_w, v_conv_b, v_conv_ln_gain, v_conv_ln_bias, v_w_conv_out, v_ssm_lambda_re, v_ssm_lambda_im, v_ssm_log_dt, v_ssm_b_re, v_ssm_b_im, v_ssm_c_re, v_ssm_c_im, v_ssm_d, v_w_ssm_glu, v_b_ssm_glu, v_w_ssm_out, v_w_out, v_post_norm_gain):
    c = lax.axis_index("c")
    shards = [w_in[0].astype(bf16), w_conv_out[0].astype(bf16), w_out[0].astype(bf16), w_ssm_glu[0].astype(bf16),
              w_ssm_out[0].astype(bf16), jnp.pad(conv_w[0], ((0, CONV_ROWS - KS), (0, 0)))]
    k_arr = (2 * lax.axis_index("x") + lax.axis_index("y")).astype(jnp.int32).reshape(1)
    w_in_g, w_co_g, w_out_g, w_glu_g, w_so_g, conv_w_g, h, xi, ti, proj = _gather_prep(
        k_arr, shards, x[0], loss_target[0], pre_norm_gain, jnp.asarray(_perm_matrix(), bf16))
    conv_w_f = conv_w_g[:, :KS].transpose(1, 0, 2).reshape(KS, CW)

    small = (conv_b, conv_ln_gain, conv_ln_bias, ssm_lambda_re, ssm_lambda_im, ssm_log_dt, ssm_b_re,
             ssm_b_im, ssm_c_re, ssm_c_im, ssm_d, b_ssm_glu, post_norm_gain)
    loss_part, gx0, dproj, big_grads, small_grads = _fwd_bwd(
        h, xi, ti, _proj_fwd(k_arr, h, w_in_g, proj), conv_w_f, w_co_g.reshape(CW, D), w_glu_g.reshape(SW, SW), w_so_g,
        w_out_g.reshape(D, D), small)

    dw_in, dw_co, dw_out, dw_glu, dw_so, d_conv_w = big_grads
    d_conv_w = jnp.pad(d_conv_w, ((0, CONV_ROWS - KS), (0, 0))).reshape(CONV_ROWS, NCHIP, 256).transpose(1, 0, 2)
    grads = [dw_in] + [g.astype(bf16) for g in (dw_co.reshape(NCHIP, 256, D), dw_out.reshape(NCHIP, 256, D),
                                                  dw_glu.reshape(NCHIP, 128, SW), dw_so, d_conv_w)]
    gs = _pack_small(small_grads, extra=loss_part)
    *recvs, rs = _pair_exchange_list(grads, gs)
    *parts, ps = _pair_sum_list(c.astype(jnp.int32).reshape(1), grads, recvs, gs, rs)
    gxi, dg_pre, *qparts, qs = _x_grad_exchange(dproj, w_in_g, xi, gx0, pre_norm_gain, parts, ps)
    grad_x = _deinterleave(gxi)
    *halves, fs = _chip_sum_list(qparts, qs)
    g_big = list(_sibling_join_list(halves))
    g_big[5] = g_big[5][:KS]

    big_w = (w_in[0], w_conv_out[0], w_out[0], w_ssm_glu[0], w_ssm_out[0], conv_w[0])
    big_m = (m_w_in[0], m_w_conv_out[0], m_w_out[0], m_w_ssm_glu[0], m_w_ssm_out[0], m_conv_w[0])
    big_v = (v_w_in[0], v_w_conv_out[0], v_w_out[0], v_w_ssm_glu[0], v_w_ssm_out[0], v_conv_w[0])
    big_names = ("w_in", "w_conv_out", "w_out", "w_ssm_glu", "w_ssm_out", "conv_w")
    res = {}
    for n, w, g, m, v in zip(big_names, big_w, g_big, big_m, big_v):
        d, m2, v2 = _adamw("adamw_" + n, w, g, m, v)
        res[n] = (g[None], d[None], m2[None], v2[None])

    small_m = (m_conv_b, m_conv_ln_gain, m_conv_ln_bias, m_ssm_lambda_re, m_ssm_lambda_im, m_ssm_log_dt,
               m_ssm_b_re, m_ssm_b_im, m_ssm_c_re, m_ssm_c_im, m_ssm_d, m_b_ssm_glu, m_post_norm_gain)
    small_v = (v_conv_b, v_conv_ln_gain, v_conv_ln_bias, v_ssm_lambda_re, v_ssm_lambda_im, v_ssm_log_dt,
               v_ssm_b_re, v_ssm_b_im, v_ssm_c_re, v_ssm_c_im, v_ssm_d, v_b_ssm_glu, v_post_norm_gain)
    sd, sm, sv = _adamw("adamw_small", _pack_small(small), fs, _pack_small(small_m), _pack_small(small_v))
    sg_l, loss = _unpack_small(fs)
    sd_l, _ = _unpack_small(sd)
    sm_l, _ = _unpack_small(sm)
    sv_l, _ = _unpack_small(sv)
    for i, (n, _) in enumerate(_SMALL):
        res[n] = (sg_l[i], sd_l[i], sm_l[i], sv_l[i])
    rows = lambda a: a.reshape(8, 128)
    pre = _adamw_rows(_allgather_rows(rows(dg_pre)), rows(pre_norm_gain), rows(m_pre_norm_gain), rows(v_pre_norm_gain))
    res["pre_norm_gain"] = tuple(a.reshape(1, D) for a in pre)

    order = ("pre_norm_gain", "w_in", "conv_w", "conv_b", "conv_ln_gain", "conv_ln_bias", "w_conv_out", "ssm_lambda_re",
             "ssm_lambda_im", "ssm_log_dt", "ssm_b_re", "ssm_b_im", "ssm_c_re", "ssm_c_im", "ssm_d", "w_ssm_glu",
             "b_ssm_glu", "w_ssm_out", "w_out", "post_norm_gain")
    outs = [loss, grad_x[None]]
    for q in range(4):
        outs.extend(res[n][q] for n in order)
    return tuple(outs)
```

```python
import math

import numpy as np
import jax
import jax.numpy as jnp
from jax import lax
from jax.experimental import pallas as pl
from jax.experimental.pallas import tpu as pltpu

f32 = jnp.float32
bf16 = jnp.bfloat16

D = 1024
CW = 1024
SW = 512
G = 32
H = 16
PST = 64
NS = G * PST
KS = 31
IN_W = 6144
NCHIP = 4
SHARD_W = IN_W // NCHIP
RMS_EPS = 1e-6
LN_EPS = 1e-5
LR, B1, B2, EPS, WD, STEP = 0.001, 0.9, 0.999, 1e-08, 0.01, 10
GELU_K0 = math.sqrt(2.0 / math.pi)
GELU_K1 = 0.044715

TC = 512
R = TC // 8
NH = 32
LBW = 1024
CONV_ROWS = 64
SMALL_ROWS = 1152
VMEM_LIMIT = 56 * 1024 * 1024
MESH = pl.DeviceIdType.MESH


def _cp(*sem):
    return pltpu.CompilerParams(dimension_semantics=tuple(sem), vmem_limit_bytes=VMEM_LIMIT)


def _sig(v):
    return 0.5 * jnp.tanh(0.5 * v) + 0.5


def _dot(a, b):
    return jnp.dot(a, b, preferred_element_type=f32)


def _dot_nt(a, b):
    return lax.dot_general(a, b, (((1,), (1,)), ((), ())), preferred_element_type=f32)


def _dot_tn(a, b):
    return lax.dot_general(a, b, (((0,), (0,)), ((), ())), preferred_element_type=f32)


def _full(shape):
    nd = len(shape)
    return pl.BlockSpec(shape, lambda *_: (0,) * nd)


def _rows8(i):
    return pl.ds(pl.multiple_of(i * 8, 8), 8)


def _proj_fwd(k_arr, h, w_in, proj):
    L = h.shape[0]
    tm = min(1024, L)
    def body(_, h_ref, w_ref, __, o_ref):
        o_ref[...] = _dot(h_ref[...], w_ref[0]).astype(bf16)
    shard = lambda j, k: (k[0] + 1 + j) % NCHIP
    grid_spec = pltpu.PrefetchScalarGridSpec(
        num_scalar_prefetch=1, grid=(NCHIP - 1, L // tm),
        in_specs=[pl.BlockSpec((tm, D), lambda j, i, k: (i, 0)),
                  pl.BlockSpec((1, D, SHARD_W), lambda j, i, k: (shard(j, k), 0, 0)), _ANY],
        out_specs=pl.BlockSpec((tm, SHARD_W), lambda j, i, k: (i, shard(j, k))))
    return pl.pallas_call(
        body, grid_spec=grid_spec, out_shape=jax.ShapeDtypeStruct((L, IN_W), bf16),
        input_output_aliases={3: 0},
        name="proj_fwd", compiler_params=_cp("arbitrary", "arbitrary"))(k_arr, h, w_in, proj)


NLB = CW // 128
RPI = 8


def _put_blocked(buf, row0, nrows, v):
    for lb in range(NLB):
        buf[lb, pl.ds(row0, nrows), :] = v[:, lb * 128:(lb + 1) * 128]


def _get_blocked(buf, row0, nrows):
    return jnp.concatenate([buf[lb, pl.ds(row0, nrows), :] for lb in range(NLB)], axis=1)


def _fill_before(ebuf, prev):
    sub = lax.broadcasted_iota(jnp.int32, (8, 128), 0)
    def halo(p, carry):
        for lb in range(NLB):
            cur = ebuf[lb, _rows8(R + p), :]
            ebuf[lb, _rows8(p), :] = jnp.where(sub == 0, pltpu.roll(prev[lb, _rows8(p), :], 1, 0),
                                               pltpu.roll(cur, 1, 0))
        return carry
    lax.fori_loop(0, NH, halo, 0)


def _fir(buf, lb, r, coef, first, flip):
    win = buf[lb, pl.ds(pl.multiple_of(r * 8, 8), (KS + RPI - 1) * 8), :]
    outs = []
    for i in range(RPI):
        acc = [first, None, None, None]
        for k in range(KS):
            o = i + ((KS - 1 - k) if flip else k)
            t = coef[k] * win[8 * o:8 * o + 8, :]
            acc[k % 4] = t if acc[k % 4] is None else acc[k % 4] + t
        outs.append((acc[0] + acc[1]) + (acc[2] + acc[3]))
    return outs


def _conv_fwd(proj, cw, cbias, lng, lnb):
    L = proj.shape[0]
    nc = L // TC
    def body(ca_ref, cb_ref, zc_ref, w_ref, b_ref, g_ref, bb_ref, cu1_ref, ain_ref, ebuf, prev, cacc):
        @pl.when(pl.program_id(0) == 0)
        def _():
            prev[...] = jnp.zeros_like(prev)
        def glu(s, carry):
            rows = pl.ds(pl.multiple_of(s * 64, 64), 64)
            _put_blocked(ebuf, pl.multiple_of(NH * 8 + s * 64, 64), 64,
                         ca_ref[rows, :].astype(f32) * _sig(cb_ref[rows, :].astype(f32)))
            return carry
        lax.fori_loop(0, TC // 64, glu, 0)
        _fill_before(ebuf, prev)
        prev[...] = ebuf[:, R * 8:(NH + R) * 8, :]
        for lb in range(NLB):
            sl = slice(lb * 128, (lb + 1) * 128)
            wk = [jnp.broadcast_to(w_ref[k:k + 1, sl], (8, 128)) for k in range(KS)]
            bias = jnp.broadcast_to(b_ref[:, sl], (8, 128))
            def tap(q, carry, lb=lb, wk=wk, bias=bias):
                r = q * RPI
                for i, o in enumerate(_fir(ebuf, lb, r + (NH - KS + 1), wk, bias, False)):
                    cacc[lb, _rows8(r + i), :] = o
                return carry
            lax.fori_loop(0, R // RPI, tap, 0)
        def norm(s, carry):
            rows = pl.ds(pl.multiple_of(s * 64, 64), 64)
            c1b = _get_blocked(cacc, pl.multiple_of(s * 64, 64), 64).astype(bf16)
            cu1_ref[rows, :] = c1b
            c1 = c1b.astype(f32)
            xc = c1 - jnp.mean(c1, axis=-1, keepdims=True)
            var = jnp.mean(xc * xc, axis=-1, keepdims=True)
            ln = xc * lax.rsqrt(var + LN_EPS) * g_ref[...] + bb_ref[...]
            zc = zc_ref[rows, :].astype(f32)
            ain_ref[rows, :] = ((ln * _sig(ln)) * (zc * _sig(zc))).astype(bf16)
            return carry
        lax.fori_loop(0, TC // 64, norm, 0, unroll=4)

    col = lambda c: pl.BlockSpec((TC, CW), lambda i, c=c: (i, c))
    return pl.pallas_call(
        body, grid=(nc,),
        in_specs=[col(0), col(1), col(2), _full((32, CW)), _full((1, CW)), _full((1, CW)), _full((1, CW))],
        out_specs=[pl.BlockSpec((TC, CW), lambda i: (i, 0)), pl.BlockSpec((TC, CW), lambda i: (i, 0))],
        out_shape=[jax.ShapeDtypeStruct((L, CW), bf16), jax.ShapeDtypeStruct((L, CW), bf16)],
        scratch_shapes=[pltpu.VMEM((NLB, (NH + R) * 8, 128), f32), pltpu.VMEM((NLB, NH * 8, 128), f32),
                        pltpu.VMEM((NLB, TC, 128), f32)],
        name="conv_fwd", compiler_params=_cp("arbitrary"))(proj, proj, proj, cw, cbias, lng, lnb)


def _gelu_parts(y0):
    t = jnp.tanh(GELU_K0 * (y0 + GELU_K1 * y0 * y0 * y0))
    return t, 0.5 * y0 * (1.0 + t)


def _ssm_fwd(proj, bbt_re, bbt_im, ct_re, ct_im, a_re, a_im, apow_re, apow_im, dvec, wglu, bglu):
    L = proj.shape[0]
    nc = L // TC
    def body(u_ref, zs_ref, bre_ref, bim_ref, cre_ref, cim_ref, are_ref, aim_ref, pwr_ref, pwi_ref,
             d_ref, wg_ref, bg_ref, y0_ref, bin_ref, sre, sim, cinr, cini, prev_re, prev_im):
        c = pl.program_id(0)
        @pl.when(c == 0)
        def _():
            prev_re[...] = jnp.zeros_like(prev_re)
            prev_im[...] = jnp.zeros_like(prev_im)
        u = u_ref[...]
        for blk in range(4):
            ub = u[:, 128 * blk:128 * (blk + 1)]
            sre[:, 512 * blk:512 * (blk + 1)] = _dot(ub, bre_ref[blk])
            sim[:, 512 * blk:512 * (blk + 1)] = _dot(ub, bim_ref[blk])
        for lb in range(NS // LBW):
            sl = slice(lb * LBW, (lb + 1) * LBW)
            ar = jnp.broadcast_to(are_ref[:, sl], (8, LBW))
            ai = jnp.broadcast_to(aim_ref[:, sl], (8, LBW))
            def step(r, carry, sl=sl, ar=ar, ai=ai):
                sr, si = carry
                nr = ar * sr - ai * si + sre[_rows8(r), sl]
                ni = ar * si + ai * sr + sim[_rows8(r), sl]
                sre[_rows8(r), sl] = nr
                sim[_rows8(r), sl] = ni
                return nr, ni
            lax.fori_loop(1, R, step, (sre[0:8, sl], sim[0:8, sl]))
        a_r = pwr_ref[R - 1:R, :]
        a_i = pwi_ref[R - 1:R, :]
        cr = prev_re[0:1, :]
        ci = prev_im[0:1, :]
        for seg in range(8):
            cinr[seg:seg + 1, :] = cr
            cini[seg:seg + 1, :] = ci
            er = sre[8 * (R - 1) + seg:8 * (R - 1) + seg + 1, :]
            ei = sim[8 * (R - 1) + seg:8 * (R - 1) + seg + 1, :]
            cr, ci = er + a_r * cr - a_i * ci, ei + a_r * ci + a_i * cr
        prev_re[0:1, :] = cr
        prev_im[0:1, :] = ci
        for lb in range(NS // LBW):
            sl = slice(lb * LBW, (lb + 1) * LBW)
            kr = cinr[:, sl]
            ki = cini[:, sl]
            def fix(r, carry, sl=sl, kr=kr, ki=ki):
                pr = jnp.broadcast_to(pwr_ref[pl.ds(r, 1), sl], (8, LBW))
                pi = jnp.broadcast_to(pwi_ref[pl.ds(r, 1), sl], (8, LBW))
                sre[_rows8(r), sl] = sre[_rows8(r), sl] + pr * kr - pi * ki
                sim[_rows8(r), sl] = sim[_rows8(r), sl] + pr * ki + pi * kr
                return carry
            lax.fori_loop(0, R, fix, 0, unroll=2)
        yp = []
        for blk in range(4):
            sr = sre[:, 512 * blk:512 * (blk + 1)].astype(bf16)
            si = sim[:, 512 * blk:512 * (blk + 1)].astype(bf16)
            yp.append(_dot(sr, cre_ref[blk]) - _dot(si, cim_ref[blk]))
        y0 = jnp.concatenate(yp, axis=1) + d_ref[...] * u.astype(f32)
        y0_ref[...] = y0
        _, y1 = _gelu_parts(y0)
        glu = _dot(y1.astype(bf16), wg_ref[...]) + bg_ref[...]
        y2 = y1 * _sig(glu)
        zs = zs_ref[...].astype(f32)
        bin_ref[...] = (y2 * (zs * _sig(zs))).astype(bf16)

    return pl.pallas_call(
        body, grid=(nc,),
        in_specs=[pl.BlockSpec((TC, SW), lambda c: (c, 6)), pl.BlockSpec((TC, SW), lambda c: (c, 7)),
                  _full((4, 128, 512)), _full((4, 128, 512)), _full((4, 512, 128)), _full((4, 512, 128)),
                  _full((1, NS)), _full((1, NS)), _full((R, NS)), _full((R, NS)),
                  _full((1, SW)), _full((SW, SW)), _full((1, SW))],
        out_specs=[pl.BlockSpec((TC, SW), lambda c: (c, 0)), pl.BlockSpec((TC, SW), lambda c: (c, 0)),
                   pl.BlockSpec((TC, NS), lambda c: (c, 0)), pl.BlockSpec((TC, NS), lambda c: (c, 0)),
                   pl.BlockSpec((8, NS), lambda c: (c, 0)), pl.BlockSpec((8, NS), lambda c: (c, 0))],
        out_shape=[jax.ShapeDtypeStruct((L, SW), f32), jax.ShapeDtypeStruct((L, SW), bf16),
                   jax.ShapeDtypeStruct((L, NS), f32), jax.ShapeDtypeStruct((L, NS), f32),
                   jax.ShapeDtypeStruct((nc * 8, NS), f32), jax.ShapeDtypeStruct((nc * 8, NS), f32)],
        scratch_shapes=[pltpu.VMEM((8, NS), f32), pltpu.VMEM((8, NS), f32)],
        name="ssm_fwd", compiler_params=_cp("arbitrary"))(
            proj, proj, bbt_re, bbt_im, ct_re, ct_im, a_re, a_im, apow_re, apow_im, dvec, wglu, bglu)


def _tail(a_in, b_in, proj, x, tgt, wco, wso, wout, gpost):
    L = x.shape[0]
    tm = 512
    def body(a_ref, b_ref, gc_ref, gs_ref, x_ref, t_ref, wco_ref, wso_ref, wout_ref, gp_ref,
             gx_ref, dain_ref, dbin_ref, dp_ref, dwout_ref, dwco_ref, dwso_ref, dgp_ref, loss_ref):
        @pl.when(pl.program_id(0) == 0)
        def _():
            dwout_ref[...] = jnp.zeros_like(dwout_ref)
            dwco_ref[...] = jnp.zeros_like(dwco_ref)
            dwso_ref[...] = jnp.zeros_like(dwso_ref)
            dgp_ref[...] = jnp.zeros_like(dgp_ref)
            loss_ref[...] = jnp.zeros_like(loss_ref)
        a = a_ref[...]
        b = b_ref[...]
        co = _dot(a, wco_ref[...])
        so = jnp.concatenate([_dot(b, wso_ref[j]) for j in range(NCHIP)], axis=1)
        sc = _sig(gc_ref[...].astype(f32))
        ss = _sig(gs_ref[...].astype(f32))
        mb = (sc * co + ss * so).astype(bf16)
        out = _dot(mb, wout_ref[...])
        r2 = lax.rsqrt(jnp.mean(out * out, axis=-1, keepdims=True) + RMS_EPS)
        on = out * r2
        gp = gp_ref[...]
        e = x_ref[...] + on * gp - t_ref[...]
        loss_ref[...] += (0.5 / D) * jnp.sum(e * e)
        dy = e * (1.0 / D)
        gx_ref[...] = dy
        dgp_ref[...] += jnp.sum(dy * on, axis=0, keepdims=True)
        dn = dy * gp
        dout = (r2 * (dn - on * jnp.mean(dn * on, axis=-1, keepdims=True))).astype(bf16)
        dwout_ref[...] += _dot_tn(mb, dout)
        dm = _dot_nt(dout, wout_ref[...])
        dp_ref[:, 0:D] = (dm * co * sc * (1.0 - sc)).astype(bf16)
        dp_ref[:, D:2 * D] = (dm * so * ss * (1.0 - ss)).astype(bf16)
        dco = (dm * sc).astype(bf16)
        dso = (dm * ss).astype(bf16)
        dwco_ref[...] += _dot_tn(a, dco)
        dbin = None
        for j in range(NCHIP):
            dso_j = dso[:, j * 256:(j + 1) * 256]
            dwso_ref[j] += _dot_tn(b, dso_j)
            t = _dot_nt(dso_j, wso_ref[j])
            dbin = t if dbin is None else dbin + t
        dain_ref[...] = _dot_nt(dco, wco_ref[...]).astype(bf16)
        dbin_ref[...] = dbin.astype(bf16)

    row = lambda w: pl.BlockSpec((tm, w), lambda i: (i, 0))
    one = lambda shape: pl.BlockSpec(shape, lambda i: (0,) * len(shape), pipeline_mode=pl.Buffered(1))
    return pl.pallas_call(
        body, grid=(L // tm,),
        in_specs=[row(CW), row(SW), pl.BlockSpec((tm, D), lambda i: (i, 4)), pl.BlockSpec((tm, D), lambda i: (i, 5)),
                  row(D), row(D), one((CW, D)), one((NCHIP, SW, 256)), one((D, D)), one((1, D))],
        out_specs=[row(D), row(CW), row(SW), pl.BlockSpec((tm, 2 * D), lambda i: (i, 2)),
                   one((D, D)), one((CW, D)), one((NCHIP, SW, 256)), one((1, D)), one((1, 128))],
        out_shape=[jax.ShapeDtypeStruct((L, D), f32), jax.ShapeDtypeStruct((L, CW), bf16),
                   jax.ShapeDtypeStruct((L, SW), bf16), jax.ShapeDtypeStruct((L, IN_W), bf16),
                   jax.ShapeDtypeStruct((D, D), f32), jax.ShapeDtypeStruct((CW, D), f32),
                   jax.ShapeDtypeStruct((NCHIP, SW, 256), f32), jax.ShapeDtypeStruct((1, D), f32),
                   jax.ShapeDtypeStruct((1, 128), f32)],
        name="tail", compiler_params=_cp("arbitrary"))(a_in, b_in, proj, proj, x, tgt, wco, wso, wout, gpost)


def _ssm_bwd(d_bin, y0, proj, sre, sim, cinr, cini, bbt_re, bbt_im, ct_re, ct_im,
             a_re, a_im, apow_re, apow_im, dvec, wglu, bglu, dproj):
    L = y0.shape[0]
    nc = L // TC
    def body(dbin_ref, y0_ref, u_ref, zs_ref, sre_ref, sim_ref, cinr_ref, cini_ref,
             bre_ref, bim_ref, cre_ref, cim_ref, are_ref, aim_ref, pwr_ref, pwi_ref, d_ref, wg_ref, bg_ref, _,
             dp_ref, dbre_ref, dbim_ref, dcre_ref, dcim_ref, dd_ref, dar_ref, dai_ref, dwg_ref, dbg_ref,
             gre, gim, gcr, gci, nxt_re, nxt_im):
        @pl.when(pl.program_id(0) == 0)
        def _():
            for ref in (dbre_ref, dbim_ref, dcre_ref, dcim_ref, dd_ref, dar_ref, dai_ref, dwg_ref, dbg_ref,
                        nxt_re, nxt_im):
                ref[...] = jnp.zeros_like(ref)
        y0 = y0_ref[...]
        u = u_ref[...]
        zs = zs_ref[...].astype(f32)
        dbin = dbin_ref[...].astype(f32)
        t, y1 = _gelu_parts(y0)
        y1b = y1.astype(bf16)
        sg = _sig(_dot(y1b, wg_ref[...]) + bg_ref[...])
        sz = _sig(zs)
        d_y2 = dbin * (zs * sz)
        dp_ref[:, SW:2 * SW] = (dbin * (y1 * sg) * (sz * (1.0 + zs * (1.0 - sz)))).astype(bf16)
        d_glu = d_y2 * y1 * sg * (1.0 - sg)
        d_glub = d_glu.astype(bf16)
        d_y1 = d_y2 * sg + _dot_nt(d_glub, wg_ref[...])
        dwg_ref[...] += _dot_tn(y1b, d_glub)
        dbg_ref[...] += jnp.sum(d_glu, axis=0, keepdims=True)
        dgelu = 0.5 * (1.0 + t) + 0.5 * y0 * (1.0 - t * t) * GELU_K0 * (1.0 + 3.0 * GELU_K1 * y0 * y0)
        d_y0 = d_y1 * dgelu
        dd_ref[...] += jnp.sum(d_y0 * u.astype(f32), axis=0, keepdims=True)
        dyb = d_y0.astype(bf16)
        for blk in range(4):
            dy1 = dyb[:, 128 * blk:128 * (blk + 1)]
            gre[:, 512 * blk:512 * (blk + 1)] = _dot_nt(dy1, cre_ref[blk])
            gim[:, 512 * blk:512 * (blk + 1)] = -_dot_nt(dy1, cim_ref[blk])
        for lb in range(NS // LBW):
            sl = slice(lb * LBW, (lb + 1) * LBW)
            ar = jnp.broadcast_to(are_ref[:, sl], (8, LBW))
            ai = jnp.broadcast_to(aim_ref[:, sl], (8, LBW))
            def step(k, carry, sl=sl, ar=ar, ai=ai):
                gr, gi = carry
                row = _rows8(R - 2 - k)
                nr = ar * gr + ai * gi + gre[row, sl]
                ni = ar * gi - ai * gr + gim[row, sl]
                gre[row, sl] = nr
                gim[row, sl] = ni
                return nr, ni
            lax.fori_loop(0, R - 1, step, (gre[8 * (R - 1):8 * R, sl], gim[8 * (R - 1):8 * R, sl]))
        a_r = pwr_ref[R - 1:R, :]
        a_i = pwi_ref[R - 1:R, :]
        cr = nxt_re[0:1, :]
        ci = nxt_im[0:1, :]
        for seg in range(7, -1, -1):
            gcr[seg:seg + 1, :] = cr
            gci[seg:seg + 1, :] = ci
            er = gre[seg:seg + 1, :]
            ei = gim[seg:seg + 1, :]
            cr, ci = er + a_r * cr + a_i * ci, ei + a_r * ci - a_i * cr
        nxt_re[0:1, :] = cr
        nxt_im[0:1, :] = ci
        for lb in range(NS // LBW):
            sl = slice(lb * LBW, (lb + 1) * LBW)
            kr = gcr[:, sl]
            ki = gci[:, sl]
            def fixed(rows, prow, sl=sl, kr=kr, ki=ki):
                pr = jnp.broadcast_to(pwr_ref[prow, sl], (8, LBW))
                pi = jnp.broadcast_to(pwi_ref[prow, sl], (8, LBW))
                gr = gre[rows, sl] + pr * kr + pi * ki
                gi = gim[rows, sl] + pr * ki - pi * kr
                gre[rows, sl] = gr
                gim[rows, sl] = gi
                return gr, gi
            g0r, g0i = fixed(slice(0, 8), slice(R - 1, R))
            p0r, p0i = cinr_ref[:, sl], cini_ref[:, sl]
            acc0 = (g0r * p0r + g0i * p0i, g0i * p0r - g0r * p0i)
            def dacc(r, carry, sl=sl, fixed=fixed):
                xr, xi = carry
                gr, gi = fixed(_rows8(r), pl.ds(R - 1 - r, 1))
                pr, pi = sre_ref[_rows8(r - 1), sl], sim_ref[_rows8(r - 1), sl]
                return xr + gr * pr + gi * pi, xi + gi * pr - gr * pi
            xr, xi = lax.fori_loop(1, R, dacc, acc0)
            dar_ref[:, sl] += xr
            dai_ref[:, sl] += xi
        dup = []
        for blk in range(4):
            s4 = slice(512 * blk, 512 * (blk + 1))
            s1 = slice(128 * blk, 128 * (blk + 1))
            grb = gre[:, s4].astype(bf16)
            gib = gim[:, s4].astype(bf16)
            dup.append(_dot_nt(grb, bre_ref[blk]) + _dot_nt(gib, bim_ref[blk]))
            dbre_ref[blk] += _dot_tn(u[:, s1], grb)
            dbim_ref[blk] += _dot_tn(u[:, s1], gib)
            dcre_ref[blk] += _dot_tn(sre_ref[:, s4].astype(bf16), dyb[:, s1])
            dcim_ref[blk] -= _dot_tn(sim_ref[:, s4].astype(bf16), dyb[:, s1])
        dp_ref[:, 0:SW] = (jnp.concatenate(dup, axis=1) + d_ref[...] * d_y0).astype(bf16)

    rev = lambda w, cidx: pl.BlockSpec((TC, w), lambda i, cidx=cidx: (nc - 1 - i, cidx))
    one = lambda shape: pl.BlockSpec(shape, lambda i: (0,) * len(shape))
    return pl.pallas_call(
        body, grid=(nc,),
        in_specs=[rev(SW, 0), rev(SW, 0), rev(SW, 6), rev(SW, 7), rev(NS, 0), rev(NS, 0),
                  pl.BlockSpec((8, NS), lambda i: (nc - 1 - i, 0)), pl.BlockSpec((8, NS), lambda i: (nc - 1 - i, 0)),
                  one((4, 128, 512)), one((4, 128, 512)), one((4, 512, 128)), one((4, 512, 128)),
                  one((1, NS)), one((1, NS)), one((R, NS)), one((R, NS)),
                  one((1, SW)), one((SW, SW)), one((1, SW)), _ANY],
        out_specs=[pl.BlockSpec((TC, 2 * SW), lambda i: (nc - 1 - i, 3)),
                   one((4, 128, 512)), one((4, 128, 512)), one((4, 512, 128)), one((4, 512, 128)),
                   one((1, SW)), one((8, NS)), one((8, NS)), one((SW, SW)), one((1, SW))],
        out_shape=[jax.ShapeDtypeStruct((L, IN_W), bf16),
                   jax.ShapeDtypeStruct((4, 128, 512), f32), jax.ShapeDtypeStruct((4, 128, 512), f32),
                   jax.ShapeDtypeStruct((4, 512, 128), f32), jax.ShapeDtypeStruct((4, 512, 128), f32),
                   jax.ShapeDtypeStruct((1, SW), f32), jax.ShapeDtypeStruct((8, NS), f32),
                   jax.ShapeDtypeStruct((8, NS), f32), jax.ShapeDtypeStruct((SW, SW), f32),
                   jax.ShapeDtypeStruct((1, SW), f32)],
        scratch_shapes=[pltpu.VMEM((TC, NS), f32), pltpu.VMEM((TC, NS), f32), pltpu.VMEM((8, NS), f32),
                        pltpu.VMEM((8, NS), f32), pltpu.VMEM((8, NS), f32), pltpu.VMEM((8, NS), f32)],
        input_output_aliases={19: 0},
        name="ssm_bwd", compiler_params=_cp("arbitrary"))(
            d_bin, y0, proj, proj, sre, sim, cinr, cini, bbt_re, bbt_im, ct_re, ct_im,
            a_re, a_im, apow_re, apow_im, dvec, wglu, bglu, dproj)


def _conv_bwd(d_ain, cu1, proj, cw, lng, lnb, dproj):
    L = cu1.shape[0]
    nc = L // TC
    def body(dain_ref, cu1_ref, ca_ref, cb_ref, zc_ref, cah_ref, cbh_ref, w_ref, g_ref, bb_ref, _,
             dp_ref, dw_ref, dbias_ref, dlng_ref, dlnb_ref, dbuf, ebuf, prev, nxt, dcu0):
        i = pl.program_id(0)
        @pl.when(i == 0)
        def _():
            dw_ref[...] = jnp.zeros_like(dw_ref)
            dbias_ref[...] = jnp.zeros_like(dbias_ref)
            dlng_ref[...] = jnp.zeros_like(dlng_ref)
            dlnb_ref[...] = jnp.zeros_like(dlnb_ref)
            nxt[...] = jnp.zeros_like(nxt)
        def lnb(s, carry):
            rows = pl.ds(pl.multiple_of(s * 32, 32), 32)
            dain = dain_ref[rows, :].astype(f32)
            c1 = cu1_ref[rows, :].astype(f32)
            zc = zc_ref[rows, :].astype(f32)
            xc = c1 - jnp.mean(c1, axis=-1, keepdims=True)
            var = jnp.mean(xc * xc, axis=-1, keepdims=True)
            rstd = lax.rsqrt(var + LN_EPS)
            xh = xc * rstd
            ln = xh * g_ref[...] + bb_ref[...]
            sl_ = _sig(ln)
            sz = _sig(zc)
            dp_ref[rows, 2 * CW:3 * CW] = (dain * (ln * sl_) * (sz * (1.0 + zc * (1.0 - sz)))).astype(bf16)
            d_ln = dain * (zc * sz) * (sl_ * (1.0 + ln * (1.0 - sl_)))
            dlng_ref[...] += jnp.sum(d_ln * xh, axis=0, keepdims=True)
            dlnb_ref[...] += jnp.sum(d_ln, axis=0, keepdims=True)
            dxh = d_ln * g_ref[...]
            d_c1 = rstd * (dxh - jnp.mean(dxh, axis=-1, keepdims=True)
                           - xh * jnp.mean(dxh * xh, axis=-1, keepdims=True))
            dbias_ref[...] += jnp.sum(d_c1, axis=0, keepdims=True)
            _put_blocked(dbuf, pl.multiple_of(s * 32, 32), 32, d_c1)
            _put_blocked(ebuf, pl.multiple_of(NH * 8 + s * 32, 32), 32,
                         ca_ref[rows, :].astype(f32) * _sig(cb_ref[rows, :].astype(f32)))
            return carry
        lax.fori_loop(0, TC // 32, lnb, 0, unroll=4)
        sub = lax.broadcasted_iota(jnp.int32, (8, 128), 0)
        def after(p, carry):
            for lb in range(NLB):
                cur = dbuf[lb, _rows8(p), :]
                dbuf[lb, _rows8(R + p), :] = jnp.where(sub == 7, pltpu.roll(nxt[lb, _rows8(p), :], 7, 0),
                                                       pltpu.roll(cur, 7, 0))
            return carry
        lax.fori_loop(0, NH, after, 0)
        nxt[...] = dbuf[:, 0:NH * 8, :]
        def before(s, carry):
            rows = pl.ds(pl.multiple_of(s * 64, 64), 64)
            v = cah_ref[rows, :].astype(f32) * _sig(cbh_ref[rows, :].astype(f32))
            _put_blocked(prev, pl.multiple_of(s * 64, 64), 64, jnp.where(i == nc - 1, jnp.zeros_like(v), v))
            return carry
        lax.fori_loop(0, NH * 8 // 64, before, 0)
        _fill_before(ebuf, prev)
        for lb in range(NLB):
            sl = slice(lb * 128, (lb + 1) * 128)
            wk = [jnp.broadcast_to(w_ref[k:k + 1, sl], (8, 128)) for k in range(KS)]
            def tap(q, carry, lb=lb, wk=wk):
                r = q * RPI
                for j, o in enumerate(_fir(dbuf, lb, r, wk, None, True)):
                    dcu0[lb, _rows8(r + j), :] = o
                return carry
            lax.fori_loop(0, R // RPI, tap, 0)
            def wgrad(q, accs, lb=lb):
                r = q * RPI
                dvs = dbuf[lb, pl.ds(pl.multiple_of(r * 8, 8), RPI * 8), :]
                win = ebuf[lb, pl.ds(pl.multiple_of((r + (NH - KS + 1)) * 8, 8), (KS + RPI - 1) * 8), :]
                accs = list(accs)
                for j in range(RPI):
                    dv = dvs[8 * j:8 * j + 8, :]
                    for k in range(KS):
                        accs[k] = accs[k] + dv * win[8 * (j + k):8 * (j + k) + 8, :]
                return tuple(accs)
            accs = lax.fori_loop(0, R // RPI, wgrad, tuple(jnp.zeros((8, 128), f32) for _ in range(KS)))
            for k in range(KS):
                dw_ref[k, :, sl] += accs[k]
        def glub(s, carry):
            rows = pl.ds(pl.multiple_of(s * 64, 64), 64)
            d0 = _get_blocked(dcu0, pl.multiple_of(s * 64, 64), 64)
            ca = ca_ref[rows, :].astype(f32)
            sb = _sig(cb_ref[rows, :].astype(f32))
            dp_ref[rows, 0:CW] = (d0 * sb).astype(bf16)
            dp_ref[rows, CW:2 * CW] = (d0 * ca * sb * (1.0 - sb)).astype(bf16)
            return carry
        lax.fori_loop(0, TC // 64, glub, 0)

    hrows = NH * 8
    per = TC // hrows
    rev = lambda cidx: pl.BlockSpec((TC, CW), lambda i, cidx=cidx: (nc - 1 - i, cidx))
    halo = lambda cidx: pl.BlockSpec((hrows, CW), lambda i, cidx=cidx: (jnp.maximum((nc - 1 - i) * per - 1, 0), cidx))
    one = lambda shape: pl.BlockSpec(shape, lambda i: (0,) * len(shape))
    return pl.pallas_call(
        body, grid=(nc,),
        in_specs=[rev(0), rev(0), rev(0), rev(1), rev(2), halo(0), halo(1), one((32, CW)), one((1, CW)), one((1, CW)),
                  _ANY],
        out_specs=[pl.BlockSpec((TC, 3 * CW), lambda i: (nc - 1 - i, 0)), one((32, 8, CW)), one((1, CW)), one((1, CW)), one((1, CW))],
        out_shape=[jax.ShapeDtypeStruct((L, IN_W), bf16), jax.ShapeDtypeStruct((32, 8, CW), f32),
                   jax.ShapeDtypeStruct((1, CW), f32), jax.ShapeDtypeStruct((1, CW), f32),
                   jax.ShapeDtypeStruct((1, CW), f32)],
        scratch_shapes=[pltpu.VMEM((NLB, (R + NH) * 8, 128), f32), pltpu.VMEM((NLB, (NH + R) * 8, 128), f32),
                        pltpu.VMEM((NLB, hrows, 128), f32), pltpu.VMEM((NLB, hrows, 128), f32),
                        pltpu.VMEM((NLB, TC, 128), f32)],
        input_output_aliases={10: 0},
        name="conv_bwd", compiler_params=_cp("arbitrary"))(d_ain, cu1, proj, proj, proj, proj, proj, cw, lng, lnb, dproj)


def _win_grad(h, dproj):
    L = h.shape[0]
    tm = min(1024, L)
    nt = L // tm
    def body(h_ref, d_ref, o_ref, acc):
        i = pl.program_id(1)
        @pl.when(i == 0)
        def _():
            acc[...] = jnp.zeros_like(acc)
        acc[...] += _dot_tn(h_ref[...], d_ref[...])
        @pl.when(i == nt - 1)
        def _():
            o_ref[0] = acc[...].astype(bf16)
    return pl.pallas_call(
        body, grid=(NCHIP, nt),
        in_specs=[pl.BlockSpec((tm, D), lambda j, i: (i, 0)), pl.BlockSpec((tm, SHARD_W), lambda j, i: (i, j))],
        out_specs=pl.BlockSpec((1, D, SHARD_W), lambda j, i: (j, 0, 0)),
        out_shape=jax.ShapeDtypeStruct((NCHIP, D, SHARD_W), bf16),
        scratch_shapes=[pltpu.VMEM((D, SHARD_W), f32)],
        name="win_grad", compiler_params=_cp("arbitrary", "arbitrary"))(h, dproj)


def _adamw_math(w, g, m, v):
    m2 = B1 * m + (1.0 - B1) * g
    v2 = B2 * v + (1.0 - B2) * (g * g)
    m_hat = m2 / (1.0 - B1 ** STEP)
    v_hat = v2 / (1.0 - B2 ** STEP)
    delta = -LR * (m_hat / (jnp.sqrt(v_hat) + EPS) + WD * w)
    return delta, m2, v2


def _adamw(name, w, g, m, v):
    rows, cols = w.shape
    tm = rows if rows <= 256 else (256 if rows % 256 == 0 else 128)
    assert rows % tm == 0
    def body(w_ref, g_ref, m_ref, v_ref, d_ref, m2_ref, v2_ref):
        d, m2, v2 = _adamw_math(w_ref[...], g_ref[...], m_ref[...], v_ref[...])
        d_ref[...] = d
        m2_ref[...] = m2
        v2_ref[...] = v2
    spec = pl.BlockSpec((tm, cols), lambda i: (i, 0))
    shp = jax.ShapeDtypeStruct((rows, cols), f32)
    return pl.pallas_call(
        body, grid=(rows // tm,), in_specs=[spec] * 4, out_specs=[spec] * 3, out_shape=[shp] * 3,
        name=name, compiler_params=_cp("arbitrary"))(w, g, m, v)


_ANY = pl.BlockSpec(memory_space=pl.ANY)


def _chunks(rows, parts):
    step = rows // parts
    assert step * parts == rows and step % 16 == 0
    return [(i * step, step) for i in range(parts)]


def _place():
    x, y, c = lax.axis_index("x"), lax.axis_index("y"), lax.axis_index("c")
    chips = [(1 - x, y), (x, 1 - y), (1 - x, 1 - y)]
    return x, y, c, chips


def _nchunks(half, cols, itemsize):
    return 4 if half * cols * itemsize >= (1 << 20) else 1


def _segments(metas):
    segs = []
    for w, (half, cols, dt) in enumerate(metas):
        for r0, n in _chunks(half, _nchunks(half, cols, jnp.dtype(dt).itemsize)):
            segs.append((w, half, r0, n))
    return segs


def _rcopy(i, src, dst, send_sems, recv_sems, to):
    return pltpu.make_async_remote_copy(src_ref=src, dst_ref=dst, send_sem=send_sems.at[i], recv_sem=recv_sems.at[i],
                                        device_id=to, device_id_type=MESH)


def _gather_prep(k_arr, shards, x, tgt, g_pre, perm):
    na = len(shards)
    L = x.shape[0]
    nc = L // TC
    segs = _segments([(a.shape[0] // 2, a.shape[1], a.dtype) for a in shards])
    ns = len(segs)
    def body(_, *refs):
        ins = refs[:na]
        x_ref, t_ref, g_ref, p_ref = refs[na:na + 4]
        outs = refs[na + 4:2 * na + 4]
        h_ref, xi_ref, ti_ref, proj_ref = refs[2 * na + 4:2 * na + 8]
        stages = refs[2 * na + 8:3 * na + 8]
        send_sems, recv_sems, local_sems = refs[3 * na + 8:]
        i = pl.program_id(0)
        x, y, c, chips = _place()
        k = 2 * x + y
        me, sibling = (x, y, c), (x, y, 1 - c)

        def dst(w, half, chip, pc, r0, n):
            return outs[w].at[chip, pl.ds(pc * half + r0, n), :]

        def firsts():
            return [_rcopy(j * ns + s, ins[w].at[pl.ds(c * half + r0, n), :], dst(w, half, k, c, r0, n),
                           send_sems, recv_sems, (*chip, c))
                    for j, chip in enumerate(chips) for s, (w, half, r0, n) in enumerate(segs)]

        def own_out(w):
            return pltpu.make_async_copy(stages[w], outs[w].at[k], local_sems.at[w])

        @pl.when(i == 0)
        def _():
            for cp in firsts():
                cp.start()
            for w in range(na):
                cin = pltpu.make_async_copy(ins[w], stages[w], local_sems.at[w])
                cin.start()
                cin.wait()
            for w in range(na):
                own_out(w).start()

        p = p_ref[...]
        def through(v):
            hi = v.astype(bf16)
            r1 = v - hi.astype(f32)
            mid = r1.astype(bf16)
            lo = (r1 - mid.astype(f32)).astype(bf16)
            return (_dot(p, hi) + _dot(p, mid)) + _dot(p, lo)
        xt = x_ref[...]
        r = lax.rsqrt(jnp.mean(xt * xt, axis=-1, keepdims=True) + RMS_EPS)
        hp = _dot(p, (xt * r * g_ref[...]).astype(bf16)).astype(bf16)
        h_ref[...] = hp
        proj_ref[...] = _dot(hp, stages[0][...]).astype(bf16)
        xi_ref[...] = through(xt)
        ti_ref[...] = through(t_ref[...])

        @pl.when(i == nc - 1)
        def _():
            passed = []
            for j, chip in enumerate(chips):
                cj = 2 * chip[0] + chip[1]
                for s, (w, half, r0, n) in enumerate(segs):
                    landed = dst(w, half, cj, c, r0, n)
                    _rcopy(j * ns + s, landed, landed, send_sems, recv_sems, me).wait_recv()
                    fwd = _rcopy(3 * ns + j * ns + s, landed, landed, send_sems, recv_sems, sibling)
                    fwd.start()
                    passed.append(fwd)
            for j, chip in enumerate(chips):
                cj = 2 * chip[0] + chip[1]
                for s, (w, half, r0, n) in enumerate(segs):
                    theirs = dst(w, half, cj, 1 - c, r0, n)
                    _rcopy(3 * ns + j * ns + s, theirs, theirs, send_sems, recv_sems, me).wait_recv()
            for cp in firsts() + passed:
                cp.wait_send()
            for w in range(na):
                own_out(w).wait()

    row = lambda: pl.BlockSpec((TC, D), lambda i, k: (i, 0))
    grid_spec = pltpu.PrefetchScalarGridSpec(
        num_scalar_prefetch=1, grid=(nc,),
        in_specs=[_ANY] * na + [row(), row(), pl.BlockSpec((1, D), lambda i, k: (0, 0)),
                                pl.BlockSpec((TC, TC), lambda i, k: (0, 0))],
        out_specs=[_ANY] * na + [row(), row(), row(), pl.BlockSpec((TC, SHARD_W), lambda i, k: (i, k[0]))],
        scratch_shapes=[pltpu.VMEM(a.shape, a.dtype) for a in shards]
        + [pltpu.SemaphoreType.DMA((6 * ns,)), pltpu.SemaphoreType.DMA((6 * ns,)), pltpu.SemaphoreType.DMA((na,))])
    return pl.pallas_call(
        body, grid_spec=grid_spec,
        out_shape=[jax.ShapeDtypeStruct((NCHIP,) + a.shape, a.dtype) for a in shards]
        + [jax.ShapeDtypeStruct((L, D), bf16), jax.ShapeDtypeStruct((L, D), f32), jax.ShapeDtypeStruct((L, D), f32),
           jax.ShapeDtypeStruct((L, IN_W), bf16)],
        name="gather_prep", compiler_params=_cp("arbitrary"))(k_arr, *shards, x, tgt, g_pre, perm)


def _x_grad_exchange(dproj, w_in, x, gx0, g_pre, parts, small):
    L = x.shape[0]
    tm = 512
    nt = L // tm
    na = len(parts)
    segs = _segments([(p.shape[1], p.shape[2], p.dtype) for p in parts])
    ns = len(segs) + 1
    def body(*refs):
        d_ref, w_ref, x_ref, gx_ref, g_ref = refs[:5]
        ins, s_ref = refs[5:5 + na], refs[5 + na]
        o_ref, dg_ref = refs[6 + na:8 + na]
        outs, qs_ref = refs[8 + na:8 + 2 * na], refs[8 + 2 * na]
        stages = refs[9 + 2 * na:10 + 3 * na]
        send_sems, recv_sems, local_sems = refs[10 + 3 * na:]
        i = pl.program_id(0)
        x, y, c, chips = _place()
        k = 2 * x + y

        def copies():
            out = []
            for j, chip in enumerate(chips):
                cj = 2 * chip[0] + chip[1]
                pieces = [(s_ref, qs_ref.at[k])]
                pieces += [(ins[w].at[cj, pl.ds(r0, n), :], outs[w].at[k, pl.ds(r0, n), :]) for w, _, r0, n in segs]
                out += [_rcopy(ns * j + s, src, d, send_sems, recv_sems, (*chip, c)) for s, (src, d) in enumerate(pieces)]
            return out

        def own_out(w):
            dst = qs_ref.at[k] if w == na else outs[w].at[k]
            return pltpu.make_async_copy(stages[w], dst, local_sems.at[w])

        @pl.when(i == 0)
        def _():
            dg_ref[...] = jnp.zeros_like(dg_ref)
            for cp in copies():
                cp.start()
            for w in range(na + 1):
                cin = pltpu.make_async_copy(s_ref if w == na else ins[w].at[k], stages[w], local_sems.at[w])
                cin.start()
                cin.wait()
            for w in range(na + 1):
                own_out(w).start()

        dh = _dot_nt(d_ref[:, 0:SHARD_W], w_ref[0])
        for j in range(1, NCHIP):
            dh = dh + _dot_nt(d_ref[:, j * SHARD_W:(j + 1) * SHARD_W], w_ref[j])
        xt = x_ref[...]
        r = lax.rsqrt(jnp.mean(xt * xt, axis=-1, keepdims=True) + RMS_EPS)
        xn = xt * r
        dg_ref[...] += jnp.sum(dh * xn, axis=0, keepdims=True)
        dxn = dh * g_ref[...]
        o_ref[...] = gx_ref[...] + r * (dxn - xn * jnp.mean(dxn * xn, axis=-1, keepdims=True))

        @pl.when(i == nt - 1)
        def _():
            for cp in copies():
                cp.wait_recv()
            for cp in copies():
                cp.wait_send()
            for w in range(na + 1):
                own_out(w).wait()

    return pl.pallas_call(
        body, grid=(nt,),
        in_specs=[pl.BlockSpec((tm, IN_W), lambda i: (i, 0)),
                  pl.BlockSpec((NCHIP, D, SHARD_W), lambda i: (0, 0, 0), pipeline_mode=pl.Buffered(1)),
                  pl.BlockSpec((tm, D), lambda i: (i, 0)), pl.BlockSpec((tm, D), lambda i: (i, 0)), _full((1, D))]
        + [_ANY] * (na + 1),
        out_specs=[pl.BlockSpec((tm, D), lambda i: (i, 0)), _full((1, D))] + [_ANY] * (na + 1),
        out_shape=[jax.ShapeDtypeStruct((L, D), f32), jax.ShapeDtypeStruct((1, D), f32)]
        + [jax.ShapeDtypeStruct(p.shape, bf16) for p in parts] + [jax.ShapeDtypeStruct((NCHIP, SMALL_ROWS, 128), f32)],
        scratch_shapes=[pltpu.VMEM(p.shape[1:], bf16) for p in parts] + [pltpu.VMEM((SMALL_ROWS, 128), f32)]
        + [pltpu.SemaphoreType.DMA((3 * ns,)), pltpu.SemaphoreType.DMA((3 * ns,)), pltpu.SemaphoreType.DMA((na + 1,))],
        name="x_grad_exchange", compiler_params=_cp("arbitrary"))(dproj, w_in, x, gx0, g_pre, *parts, small)


def _sibling_join_list(halves):
    na = len(halves)
    segs = _segments([(h.shape[0], h.shape[1], h.dtype) for h in halves])
    def body(*refs):
        ins, outs, stages = refs[:na], refs[na:2 * na], refs[2 * na:3 * na]
        send_sems, recv_sems, local_sems = refs[3 * na:]
        x, y, c, _ = _place()
        copies = [_rcopy(i, ins[w].at[pl.ds(r0, n), :], outs[w].at[pl.ds(c * half + r0, n), :], send_sems, recv_sems,
                         (x, y, 1 - c)) for i, (w, half, r0, n) in enumerate(segs)]
        for cp in copies:
            cp.start()
        own = []
        for w in range(na):
            cin = pltpu.make_async_copy(ins[w], stages[w], local_sems.at[w])
            cin.start()
            cin.wait()
            half = halves[w].shape[0]
            own.append(pltpu.make_async_copy(stages[w], outs[w].at[pl.ds(c * half, half), :], local_sems.at[w]))
            own[-1].start()
        for cp in copies:
            cp.wait_recv()
        for cp in copies:
            cp.wait_send()
        for cp in own:
            cp.wait()

    return pl.pallas_call(
        body, in_specs=[_ANY] * na, out_specs=[_ANY] * na,
        out_shape=[jax.ShapeDtypeStruct((2 * h.shape[0], h.shape[1]), f32) for h in halves],
        scratch_shapes=[pltpu.VMEM(h.shape, f32) for h in halves]
        + [pltpu.SemaphoreType.DMA((len(segs),)), pltpu.SemaphoreType.DMA((len(segs),)), pltpu.SemaphoreType.DMA((na,))],
        name="sibling_join")(*halves)


def _allgather_rows(v):
    def body(v_ref, o_ref, send_sems, recv_sems):
        x, y, c, _ = _place()
        me = 4 * x + 2 * y + c
        o_ref[me] = v_ref[...]
        copies = []
        i = 0
        for dx in range(2):
            for dy in range(2):
                for dc in range(2):
                    if dx + dy + dc:
                        copies.append(_rcopy(i, v_ref, o_ref.at[me], send_sems, recv_sems, (x ^ dx, y ^ dy, c ^ dc)))
                        i += 1
        for cp in copies:
            cp.start()
        for cp in copies:
            cp.wait_recv()
        for cp in copies:
            cp.wait_send()

    vm = pl.BlockSpec(memory_space=pltpu.VMEM)
    return pl.pallas_call(
        body, in_specs=[vm], out_specs=vm, out_shape=jax.ShapeDtypeStruct((8, 8, 128), f32),
        scratch_shapes=[pltpu.SemaphoreType.DMA((7,)), pltpu.SemaphoreType.DMA((7,))],
        name="allgather_rows")(v)


def _adamw_rows(parts, w, m, v):
    def body(p_ref, w_ref, m_ref, v_ref, g_ref, d_ref, m2_ref, v2_ref):
        g = p_ref[0]
        for dvc in range(1, 8):
            g = g + p_ref[dvc]
        g_ref[...] = g
        d, m2, v2 = _adamw_math(w_ref[...], g, m_ref[...], v_ref[...])
        d_ref[...] = d
        m2_ref[...] = m2
        v2_ref[...] = v2
    return pl.pallas_call(body, out_shape=[jax.ShapeDtypeStruct((8, 128), f32)] * 4, name="adamw_pre_norm_gain")(
        parts, w, m, v)


def _pair_exchange_list(grads, small):
    na = len(grads)
    segs = _segments([(g.shape[1] // 2, g.shape[2], g.dtype) for g in grads])
    n = NCHIP * len(segs) + 1
    def body(*refs):
        ins, s_ref, outs, rs_ref, (send_sems, recv_sems) = (refs[:na], refs[na], refs[na + 1:2 * na + 1],
                                                            refs[2 * na + 1], refs[2 * na + 2:])
        x, y, c, _ = _place()
        pieces = [(s_ref, rs_ref)]
        for j in range(NCHIP):
            for w, half, r0, rows in segs:
                pieces.append((ins[w].at[j, pl.ds((1 - c) * half + r0, rows), :], outs[w].at[j, pl.ds(r0, rows), :]))
        copies = [_rcopy(i, s, d, send_sems, recv_sems, (x, y, 1 - c)) for i, (s, d) in enumerate(pieces)]
        for cp in copies:
            cp.start()
        for cp in copies:
            cp.wait_recv()
        for cp in copies:
            cp.wait_send()

    return pl.pallas_call(
        body, in_specs=[_ANY] * (na + 1), out_specs=[_ANY] * (na + 1),
        out_shape=[jax.ShapeDtypeStruct((NCHIP, g.shape[1] // 2, g.shape[2]), g.dtype) for g in grads]
        + [jax.ShapeDtypeStruct((SMALL_ROWS, 128), f32)],
        scratch_shapes=[pltpu.SemaphoreType.DMA((n,)), pltpu.SemaphoreType.DMA((n,))],
        name="pair_exchange")(*grads, small)


def _pair_sum_list(c_arr, grads, recvs, small, rsmall):
    na = len(grads)
    def body(c_ref, *refs):
        g_refs, r_refs, s_ref, rs_ref = refs[:na], refs[na:2 * na], refs[2 * na], refs[2 * na + 1]
        o_refs, os_ref = refs[2 * na + 2:3 * na + 2], refs[3 * na + 2]
        for g_ref, r_ref, o_ref in zip(g_refs, r_refs, o_refs):
            o_ref[...] = (g_ref[...].astype(f32) + r_ref[...].astype(f32)).astype(bf16)
        os_ref[...] = s_ref[...] + rs_ref[...]
    half = lambda g: pl.BlockSpec((1, g.shape[1] // 2, g.shape[2]), lambda j, c: (j, c[0], 0))
    low = lambda g: pl.BlockSpec((1, g.shape[1] // 2, g.shape[2]), lambda j, c: (j, 0, 0))
    sm = pl.BlockSpec((SMALL_ROWS, 128), lambda j, c: (0, 0))
    grid_spec = pltpu.PrefetchScalarGridSpec(
        num_scalar_prefetch=1, grid=(NCHIP,),
        in_specs=[half(g) for g in grads] + [low(g) for g in grads] + [sm, sm],
        out_specs=[low(g) for g in grads] + [sm])
    return pl.pallas_call(
        body, grid_spec=grid_spec,
        out_shape=[jax.ShapeDtypeStruct((NCHIP, g.shape[1] // 2, g.shape[2]), bf16) for g in grads]
        + [jax.ShapeDtypeStruct((SMALL_ROWS, 128), f32)],
        name="pair_sum", compiler_params=_cp("arbitrary"))(c_arr, *grads, *recvs, small, rsmall)


def _chip_sum_list(parts, small):
    na = len(parts)
    nt = 2
    def body(*refs):
        for q_ref, f_ref in zip(refs[:na + 1], refs[na + 1:]):
            acc = q_ref[0].astype(f32)
            for j in range(1, NCHIP):
                acc = acc + q_ref[j].astype(f32)
            f_ref[...] = acc
    arrs = list(parts) + [small]
    return pl.pallas_call(
        body, grid=(nt,),
        in_specs=[pl.BlockSpec((NCHIP, a.shape[1] // nt, a.shape[2]), lambda i: (0, i, 0)) for a in arrs],
        out_specs=[pl.BlockSpec((a.shape[1] // nt, a.shape[2]), lambda i: (i, 0)) for a in arrs],
        out_shape=[jax.ShapeDtypeStruct(a.shape[1:], f32) for a in arrs],
        name="chip_sum", compiler_params=_cp("arbitrary"))(*arrs)


_SMALL =(("conv_b", (1, 1024)), ("conv_ln_gain", (1, 1024)), ("conv_ln_bias", (1, 1024)),
          ("ssm_lambda_re", (1, 32, 64)), ("ssm_lambda_im", (1, 32, 64)), ("ssm_log_dt", (1, 32)),
          ("ssm_b_re", (1, 32, 64, 16)), ("ssm_b_im", (1, 32, 64, 16)), ("ssm_c_re", (1, 32, 16, 64)),
          ("ssm_c_im", (1, 32, 16, 64)), ("ssm_d", (1, 32, 16)), ("b_ssm_glu", (1, 512)), ("post_norm_gain", (1, 1024)))


def _pack_small(vals, extra=None):
    rows = []
    for v in list(vals) + ([extra] if extra is not None else []):
        flat = v.reshape(-1).astype(f32)
        n = -(-flat.shape[0] // 1024) * 1024
        rows.append(jnp.pad(flat, (0, n - flat.shape[0])).reshape(-1, 128))
    used = sum(r.shape[0] for r in rows)
    rows.append(jnp.zeros((SMALL_ROWS - used, 128), f32))
    return jnp.concatenate(rows, axis=0)


def _unpack_small(p):
    o = 0
    out = []
    for _, shape in _SMALL:
        n = int(np.prod(shape))
        nr = -(-n // 1024) * 8
        out.append(p[o:o + nr].reshape(-1)[:n].reshape(shape))
        o += nr
    return out, p[o, 0]


def _discretize(lam_re, lam_im, log_dt, b_re, b_im):
    dt = jnp.exp(log_dt)[:, None]
    mag = jnp.exp(lam_re * dt)
    ar = mag * jnp.cos(lam_im * dt)
    ai = mag * jnp.sin(lam_im * dt)
    den = lam_re * lam_re + lam_im * lam_im
    zr = ((ar - 1.0) * lam_re + ai * lam_im) / den
    zi = (ai * lam_re - (ar - 1.0) * lam_im) / den
    bbr = zr[..., None] * b_re - zi[..., None] * b_im
    bbi = zr[..., None] * b_im + zi[..., None] * b_re
    return ar, ai, bbr, bbi


_EYE8 = np.eye(8, dtype=np.float32)


def _bbt_blocks(bb):
    v = bb.reshape(4, 8, PST, H).transpose(0, 1, 3, 2)
    return jnp.einsum("bghp,gk->bghkp", v, _EYE8).reshape(4, 128, 512)


def _bbt_unblock(m):
    v = jnp.einsum("bghkp,gk->bghp", m.reshape(4, 8, H, 8, PST), _EYE8)
    return v.transpose(0, 1, 3, 2).reshape(G, PST, H)


def _ct_blocks(cc):
    v = cc.reshape(4, 8, H, PST)
    return jnp.einsum("bghp,gk->bgpkh", v, _EYE8).reshape(4, 512, 128)


def _ct_unblock(m):
    v = jnp.einsum("bgpkh,gk->bghp", m.reshape(4, 8, PST, 8, H), _EYE8)
    return v.reshape(G, H, PST)


def _perm_matrix():
    p = np.zeros((TC, TC), np.float32)
    for r in range(R):
        for seg in range(8):
            p[r * 8 + seg, seg * R + r] = 1.0
    return p


def _deinterleave(a):
    L, C = a.shape
    return a.reshape(L // TC, R, 8, C).transpose(0, 2, 1, 3).reshape(L, C)


def _fwd_bwd(h, xi, ti, proj, conv_w, w_co, w_glu, w_so, w_out, small):
    (conv_b, ln_g, ln_b, lam_re, lam_im, log_dt, b_re, b_im, c_re, c_im, dvec, b_glu, g_post) = small
    lam_re, lam_im, log_dt = lam_re[0], lam_im[0], log_dt[0]
    b_re, b_im, c_re, c_im = b_re[0], b_im[0], c_re[0], c_im[0]
    (ar, ai, bbr, bbi), disc_vjp = jax.vjp(_discretize, lam_re, lam_im, log_dt, b_re, b_im)
    a_re = ar.reshape(1, NS)
    a_im = ai.reshape(1, NS)
    dt = jnp.exp(log_dt)[:, None]
    steps = jnp.arange(1, R + 1, dtype=f32)[:, None, None]
    apow_re = (jnp.exp(steps * (lam_re * dt)) * jnp.cos(steps * (lam_im * dt))).reshape(R, NS)
    apow_im = (jnp.exp(steps * (lam_re * dt)) * jnp.sin(steps * (lam_im * dt))).reshape(R, NS)
    bbt_re, bbt_im = _bbt_blocks(bbr).astype(bf16), _bbt_blocks(bbi).astype(bf16)
    ct_re, ct_im = _ct_blocks(c_re).astype(bf16), _ct_blocks(c_im).astype(bf16)
    d_row = dvec.reshape(1, SW)
    cw32 = jnp.pad(conv_w, ((0, 1), (0, 0)))

    cu1, a_in = _conv_fwd(proj, cw32, conv_b, ln_g, ln_b)
    y0, b_in, sre, sim, cinr, cini = _ssm_fwd(proj, bbt_re, bbt_im, ct_re, ct_im, a_re, a_im,
                                              apow_re, apow_im, d_row, w_glu, b_glu)
    gx0, d_ain, d_bin, dproj, dw_out, dw_co, dw_so, dg_post, loss = _tail(
        a_in, b_in, proj, xi, ti, w_co, w_so, w_out, g_post)
    (dproj, dbbt_re, dbbt_im, dct_re, dct_im, dd, dar8, dai8, dw_glu, db_glu) = _ssm_bwd(
        d_bin, y0, proj, sre, sim, cinr, cini, bbt_re, bbt_im, ct_re, ct_im,
        a_re, a_im, apow_re, apow_im, d_row, w_glu, b_glu, dproj)
    dproj, dcw8, d_convb, d_lng, d_lnb = _conv_bwd(d_ain, cu1, proj, cw32, ln_g, ln_b, dproj)
    dw_in = _win_grad(h, dproj)

    d_ar = jnp.sum(dar8, axis=0).reshape(G, PST)
    d_ai = jnp.sum(dai8, axis=0).reshape(G, PST)
    d_lre, d_lim, d_ldt, d_bre, d_bim = disc_vjp((d_ar, d_ai, _bbt_unblock(dbbt_re), _bbt_unblock(dbbt_im)))
    d_conv_w = jnp.sum(dcw8, axis=1)[:KS]
    small_grads = [d_convb, d_lng, d_lnb, d_lre[None], d_lim[None], d_ldt[None], d_bre[None], d_bim[None],
                   _ct_unblock(dct_re)[None], _ct_unblock(dct_im)[None], dd.reshape(1, G, H), db_glu, dg_post]
    return loss[0, 0], gx0, dproj, (dw_in, dw_co, dw_out, dw_glu, dw_so, d_conv_w), small_grads


def kernel(x, pre_norm_gain, w_in, conv_w, conv_b, conv_ln_gain, conv_ln_bias, w_conv_out, ssm_lambda_re, ssm_lambda_im, ssm_log_dt, ssm_b_re, ssm_b_im, ssm_c_re, ssm_c_im, ssm_d, w_ssm_glu, b_ssm_glu, w_ssm_out, w_out, post_norm_gain, loss_target, m_pre_norm_gain, m_w_in, m_conv_w, m_conv_b, m_conv_ln_gain, m_conv_ln_bias, m_w_conv_out, m_ssm_lambda_re, m_ssm_lambda_im, m_ssm_log_dt, m_ssm_b_re, m_ssm_b_im, m_ssm_c_re, m_ssm_c_im, m_ssm_d, m_w_ssm_glu, m_b_ssm_glu, m_w_ssm_out, m_w_out, m_post_norm_gain, v_pre_norm_gain, v_w_in, v_conv_w, v_conv_b, v_conv_ln_gain, v_conv_ln_bias, v_w_conv_out, v_ssm_lambda_re, v_ssm_lambda_im, v_ssm_log_dt, v_ssm_b_re, v_ssm_b_im, v_ssm_c_re, v_ssm_c_im, v_ssm_d, v_w_ssm_glu, v_b_ssm_glu, v_w_ssm_out, v_w_out, v_post_norm_gain):
    c = lax.axis_index("c")
    shards = [w_in[0].astype(bf16), w_conv_out[0].astype(bf16), w_out[0].astype(bf16), w_ssm_glu[0].astype(bf16),
              w_ssm_out[0].astype(bf16), jnp.pad(conv_w[0], ((0, CONV_ROWS - KS), (0, 0)))]
    k_arr = (2 * lax.axis_index("x") + lax.axis_index("y")).astype(jnp.int32).reshape(1)
    w_in_g, w_co_g, w_out_g, w_glu_g, w_so_g, conv_w_g, h, xi, ti, proj = _gather_prep(
        k_arr, shards, x[0], loss_target[0], pre_norm_gain, jnp.asarray(_perm_matrix(), bf16))
    conv_w_f = conv_w_g[:, :KS].transpose(1, 0, 2).reshape(KS, CW)

    small = (conv_b, conv_ln_gain, conv_ln_bias, ssm_lambda_re, ssm_lambda_im, ssm_log_dt, ssm_b_re,
             ssm_b_im, ssm_c_re, ssm_c_im, ssm_d, b_ssm_glu, post_norm_gain)
    loss_part, gx0, dproj, big_grads, small_grads = _fwd_bwd(
        h, xi, ti, _proj_fwd(k_arr, h, w_in_g, proj), conv_w_f, w_co_g.reshape(CW, D), w_glu_g.reshape(SW, SW), w_so_g,
        w_out_g.reshape(D, D), small)

    dw_in, dw_co, dw_out, dw_glu, dw_so, d_conv_w = big_grads
    d_conv_w = jnp.pad(d_conv_w, ((0, CONV_ROWS - KS), (0, 0))).reshape(CONV_ROWS, NCHIP, 256).transpose(1, 0, 2)
    grads = [dw_in] + [g.astype(bf16) for g in (dw_co.reshape(NCHIP, 256, D), dw_out.reshape(NCHIP, 256, D),
                                                  dw_glu.reshape(NCHIP, 128, SW), dw_so, d_conv_w)]
    gs = _pack_small(small_grads, extra=loss_part)
    *recvs, rs = _pair_exchange_list(grads, gs)
    *parts, ps = _pair_sum_list(c.astype(jnp.int32).reshape(1), grads, recvs, gs, rs)
    gxi, dg_pre, *qparts, qs = _x_grad_exchange(dproj, w_in_g, xi, gx0, pre_norm_gain, parts, ps)
    grad_x = _deinterleave(gxi)
    *halves, fs = _chip_sum_list(qparts, qs)
    g_big = list(_sibling_join_list(halves))
    g_big[5] = g_big[5][:KS]

    big_w = (w_in[0], w_conv_out[0], w_out[0], w_ssm_glu[0], w_ssm_out[0], conv_w[0])
    big_m = (m_w_in[0], m_w_conv_out[0], m_w_out[0], m_w_ssm_glu[0], m_w_ssm_out[0], m_conv_w[0])
    big_v = (v_w_in[0], v_w_conv_out[0], v_w_out[0], v_w_ssm_glu[0], v_w_ssm_out[0], v_conv_w[0])
    big_names = ("w_in", "w_conv_out", "w_out", "w_ssm_glu", "w_ssm_out", "conv_w")
    res = {}
    for n, w, g, m, v in zip(big_names, big_w, g_big, big_m, big_v):
        d, m2, v2 = _adamw("adamw_" + n, w, g, m, v)
        res[n] = (g[None], d[None], m2[None], v2[None])

    small_m = (m_conv_b, m_conv_ln_gain, m_conv_ln_bias, m_ssm_lambda_re, m_ssm_lambda_im, m_ssm_log_dt,
               m_ssm_b_re, m_ssm_b_im, m_ssm_c_re, m_ssm_c_im, m_ssm_d, m_b_ssm_glu, m_post_norm_gain)
    small_v = (v_conv_b, v_conv_ln_gain, v_conv_ln_bias, v_ssm_lambda_re, v_ssm_lambda_im, v_ssm_log_dt,
               v_ssm_b_re, v_ssm_b_im, v_ssm_c_re, v_ssm_c_im, v_ssm_d, v_b_ssm_glu, v_post_norm_gain)
    sd, sm, sv = _adamw("adamw_small", _pack_small(small), fs, _pack_small(small_m), _pack_small(small_v))
    sg_l, loss = _unpack_small(fs)
    sd_l, _ = _unpack_small(sd)
    sm_l, _ = _unpack_small(sm)
    sv_l, _ = _unpack_small(sv)
    for i, (n, _) in enumerate(_SMALL):
        res[n] = (sg_l[i], sd_l[i], sm_l[i], sv_l[i])
    rows = lambda a: a.reshape(8, 128)
    pre = _adamw_rows(_allgather_rows(rows(dg_pre)), rows(pre_norm_gain), rows(m_pre_norm_gain), rows(v_pre_norm_gain))
    res["pre_norm_gain"] = tuple(a.reshape(1, D) for a in pre)

    order = ("pre_norm_gain", "w_in", "conv_w", "conv_b", "conv_ln_gain", "conv_ln_bias", "w_conv_out", "ssm_lambda_re",
             "ssm_lambda_im", "ssm_log_dt", "ssm_b_re", "ssm_b_im", "ssm_c_re", "ssm_c_im", "ssm_d", "w_ssm_glu",
             "b_ssm_glu", "w_ssm_out", "w_out", "post_norm_gain")
    outs = [loss, grad_x[None]]
    for q in range(4):
        outs.extend(res[n][q] for n in order)
    return tuple(outs)
```

```python
import math

import numpy as np
import jax
import jax.numpy as jnp
from jax import lax
from jax.experimental import pallas as pl
from jax.experimental.pallas import tpu as pltpu

f32 = jnp.float32
bf16 = jnp.bfloat16

D = 1024
CW = 1024
SW = 512
G = 32
H = 16
PST = 64
NS = G * PST
KS = 31
IN_W = 6144
NCHIP = 4
SHARD_W = IN_W // NCHIP
RMS_EPS = 1e-6
LN_EPS = 1e-5
LR, B1, B2, EPS, WD, STEP = 0.001, 0.9, 0.999, 1e-08, 0.01, 10
GELU_K0 = math.sqrt(2.0 / math.pi)
GELU_K1 = 0.044715

TC = 512
R = TC // 8
NH = 32
LBW = 1024
CONV_ROWS = 64
SMALL_ROWS = 1152
VMEM_LIMIT = 56 * 1024 * 1024
MESH = pl.DeviceIdType.MESH


def _cp(*sem):
    return pltpu.CompilerParams(dimension_semantics=tuple(sem), vmem_limit_bytes=VMEM_LIMIT)


def _sig(v):
    return 0.5 * jnp.tanh(0.5 * v) + 0.5


def _dot(a, b):
    return jnp.dot(a, b, preferred_element_type=f32)


def _dot_nt(a, b):
    return lax.dot_general(a, b, (((1,), (1,)), ((), ())), preferred_element_type=f32)


def _dot_tn(a, b):
    return lax.dot_general(a, b, (((0,), (0,)), ((), ())), preferred_element_type=f32)


def _full(shape):
    nd = len(shape)
    return pl.BlockSpec(shape, lambda *_: (0,) * nd)


def _rows8(i):
    return pl.ds(pl.multiple_of(i * 8, 8), 8)


def _proj_fwd(k_arr, h, w_in, proj):
    L = h.shape[0]
    tm = min(1024, L)
    def body(_, h_ref, w_ref, __, o_ref):
        o_ref[...] = _dot(h_ref[...], w_ref[0]).astype(bf16)
    shard = lambda j, k: (k[0] + 1 + j) % NCHIP
    grid_spec = pltpu.PrefetchScalarGridSpec(
        num_scalar_prefetch=1, grid=(NCHIP - 1, L // tm),
        in_specs=[pl.BlockSpec((tm, D), lambda j, i, k: (i, 0)),
                  pl.BlockSpec((1, D, SHARD_W), lambda j, i, k: (shard(j, k), 0, 0)), _ANY],
        out_specs=pl.BlockSpec((tm, SHARD_W), lambda j, i, k: (i, shard(j, k))))
    return pl.pallas_call(
        body, grid_spec=grid_spec, out_shape=jax.ShapeDtypeStruct((L, IN_W), bf16),
        input_output_aliases={3: 0},
        name="proj_fwd", compiler_params=_cp("arbitrary", "arbitrary"))(k_arr, h, w_in, proj)


NLB = CW // 128
RPI = 8


def _put_blocked(buf, row0, nrows, v):
    for lb in range(NLB):
        buf[lb, pl.ds(row0, nrows), :] = v[:, lb * 128:(lb + 1) * 128]


def _get_blocked(buf, row0, nrows):
    return jnp.concatenate([buf[lb, pl.ds(row0, nrows), :] for lb in range(NLB)], axis=1)


def _fill_before(ebuf, prev):
    sub = lax.broadcasted_iota(jnp.int32, (8, 128), 0)
    def halo(p, carry):
        for lb in range(NLB):
            cur = ebuf[lb, _rows8(R + p), :]
            ebuf[lb, _rows8(p), :] = jnp.where(sub == 0, pltpu.roll(prev[lb, _rows8(p), :], 1, 0),
                                               pltpu.roll(cur, 1, 0))
        return carry
    lax.fori_loop(0, NH, halo, 0)


def _fir(buf, lb, r, coef, first, flip):
    win = buf[lb, pl.ds(pl.multiple_of(r * 8, 8), (KS + RPI - 1) * 8), :]
    outs = []
    for i in range(RPI):
        acc = [first, None, None, None]
        for k in range(KS):
            o = i + ((KS - 1 - k) if flip else k)
            t = coef[k] * win[8 * o:8 * o + 8, :]
            acc[k % 4] = t if acc[k % 4] is None else acc[k % 4] + t
        outs.append((acc[0] + acc[1]) + (acc[2] + acc[3]))
    return outs


def _conv_fwd(proj, cw, cbias, lng, lnb):
    L = proj.shape[0]
    nc = L // TC
    def body(ca_ref, cb_ref, zc_ref, w_ref, b_ref, g_ref, bb_ref, cu1_ref, ain_ref, ebuf, prev, cacc):
        @pl.when(pl.program_id(0) == 0)
        def _():
            prev[...] = jnp.zeros_like(prev)
        def glu(s, carry):
            rows = pl.ds(pl.multiple_of(s * 64, 64), 64)
            _put_blocked(ebuf, pl.multiple_of(NH * 8 + s * 64, 64), 64,
                         ca_ref[rows, :].astype(f32) * _sig(cb_ref[rows, :].astype(f32)))
            return carry
        lax.fori_loop(0, TC // 64, glu, 0)
        _fill_before(ebuf, prev)
        prev[...] = ebuf[:, R * 8:(NH + R) * 8, :]
        for lb in range(NLB):
            sl = slice(lb * 128, (lb + 1) * 128)
            wk = [jnp.broadcast_to(w_ref[k:k + 1, sl], (8, 128)) for k in range(KS)]
            bias = jnp.broadcast_to(b_ref[:, sl], (8, 128))
            def tap(q, carry, lb=lb, wk=wk, bias=bias):
                r = q * RPI
                for i, o in enumerate(_fir(ebuf, lb, r + (NH - KS + 1), wk, bias, False)):
                    cacc[lb, _rows8(r + i), :] = o
                return carry
            lax.fori_loop(0, R // RPI, tap, 0)
        def norm(s, carry):
            rows = pl.ds(pl.multiple_of(s * 64, 64), 64)
            c1b = _get_blocked(cacc, pl.multiple_of(s * 64, 64), 64).astype(bf16)
            cu1_ref[rows, :] = c1b
            c1 = c1b.astype(f32)
            xc = c1 - jnp.mean(c1, axis=-1, keepdims=True)
            var = jnp.mean(xc * xc, axis=-1, keepdims=True)
            ln = xc * lax.rsqrt(var + LN_EPS) * g_ref[...] + bb_ref[...]
            zc = zc_ref[rows, :].astype(f32)
            ain_ref[rows, :] = ((ln * _sig(ln)) * (zc * _sig(zc))).astype(bf16)
            return carry
        lax.fori_loop(0, TC // 64, norm, 0, unroll=4)

    col = lambda c: pl.BlockSpec((TC, CW), lambda i, c=c: (i, c))
    return pl.pallas_call(
        body, grid=(nc,),
        in_specs=[col(0), col(1), col(2), _full((32, CW)), _full((1, CW)), _full((1, CW)), _full((1, CW))],
        out_specs=[pl.BlockSpec((TC, CW), lambda i: (i, 0)), pl.BlockSpec((TC, CW), lambda i: (i, 0))],
        out_shape=[jax.ShapeDtypeStruct((L, CW), bf16), jax.ShapeDtypeStruct((L, CW), bf16)],
        scratch_shapes=[pltpu.VMEM((NLB, (NH + R) * 8, 128), f32), pltpu.VMEM((NLB, NH * 8, 128), f32),
                        pltpu.VMEM((NLB, TC, 128), f32)],
        name="conv_fwd", compiler_params=_cp("arbitrary"))(proj, proj, proj, cw, cbias, lng, lnb)


def _gelu_parts(y0):
    t = jnp.tanh(GELU_K0 * (y0 + GELU_K1 * y0 * y0 * y0))
    return t, 0.5 * y0 * (1.0 + t)


def _ssm_fwd(proj, bbt_re, bbt_im, ct_re, ct_im, a_re, a_im, apow_re, apow_im, dvec, wglu, bglu):
    L = proj.shape[0]
    nc = L // TC
    def body(u_ref, zs_ref, bre_ref, bim_ref, cre_ref, cim_ref, are_ref, aim_ref, pwr_ref, pwi_ref,
             d_ref, wg_ref, bg_ref, y0_ref, bin_ref, sre, sim, cinr, cini, prev_re, prev_im):
        c = pl.program_id(0)
        @pl.when(c == 0)
        def _():
            prev_re[...] = jnp.zeros_like(prev_re)
            prev_im[...] = jnp.zeros_like(prev_im)
        u = u_ref[...]
        for blk in range(4):
            ub = u[:, 128 * blk:128 * (blk + 1)]
            sre[:, 512 * blk:512 * (blk + 1)] = _dot(ub, bre_ref[blk])
            sim[:, 512 * blk:512 * (blk + 1)] = _dot(ub, bim_ref[blk])
        for lb in range(NS // LBW):
            sl = slice(lb * LBW, (lb + 1) * LBW)
            ar = jnp.broadcast_to(are_ref[:, sl], (8, LBW))
            ai = jnp.broadcast_to(aim_ref[:, sl], (8, LBW))
            def step(r, carry, sl=sl, ar=ar, ai=ai):
                sr, si = carry
                nr = ar * sr - ai * si + sre[_rows8(r), sl]
                ni = ar * si + ai * sr + sim[_rows8(r), sl]
                sre[_rows8(r), sl] = nr
                sim[_rows8(r), sl] = ni
                return nr, ni
            lax.fori_loop(1, R, step, (sre[0:8, sl], sim[0:8, sl]))
        a_r = pwr_ref[R - 1:R, :]
        a_i = pwi_ref[R - 1:R, :]
        cr = prev_re[0:1, :]
        ci = prev_im[0:1, :]
        for seg in range(8):
            cinr[seg:seg + 1, :] = cr
            cini[seg:seg + 1, :] = ci
            er = sre[8 * (R - 1) + seg:8 * (R - 1) + seg + 1, :]
            ei = sim[8 * (R - 1) + seg:8 * (R - 1) + seg + 1, :]
            cr, ci = er + a_r * cr - a_i * ci, ei + a_r * ci + a_i * cr
        prev_re[0:1, :] = cr
        prev_im[0:1, :] = ci
        for lb in range(NS // LBW):
            sl = slice(lb * LBW, (lb + 1) * LBW)
            kr = cinr[:, sl]
            ki = cini[:, sl]
            def fix(r, carry, sl=sl, kr=kr, ki=ki):
                pr = jnp.broadcast_to(pwr_ref[pl.ds(r, 1), sl], (8, LBW))
                pi = jnp.broadcast_to(pwi_ref[pl.ds(r, 1), sl], (8, LBW))
                sre[_rows8(r), sl] = sre[_rows8(r), sl] + pr * kr - pi * ki
                sim[_rows8(r), sl] = sim[_rows8(r), sl] + pr * ki + pi * kr
                return carry
            lax.fori_loop(0, R, fix, 0, unroll=2)
        yp = []
        for blk in range(4):
            sr = sre[:, 512 * blk:512 * (blk + 1)].astype(bf16)
            si = sim[:, 512 * blk:512 * (blk + 1)].astype(bf16)
            yp.append(_dot(sr, cre_ref[blk]) - _dot(si, cim_ref[blk]))
        y0 = jnp.concatenate(yp, axis=1) + d_ref[...] * u.astype(f32)
        y0_ref[...] = y0
        _, y1 = _gelu_parts(y0)
        glu = _dot(y1.astype(bf16), wg_ref[...]) + bg_ref[...]
        y2 = y1 * _sig(glu)
        zs = zs_ref[...].astype(f32)
        bin_ref[...] = (y2 * (zs * _sig(zs))).astype(bf16)

    return pl.pallas_call(
        body, grid=(nc,),
        in_specs=[pl.BlockSpec((TC, SW), lambda c: (c, 6)), pl.BlockSpec((TC, SW), lambda c: (c, 7)),
                  _full((4, 128, 512)), _full((4, 128, 512)), _full((4, 512, 128)), _full((4, 512, 128)),
                  _full((1, NS)), _full((1, NS)), _full((R, NS)), _full((R, NS)),
                  _full((1, SW)), _full((SW, SW)), _full((1, SW))],
        out_specs=[pl.BlockSpec((TC, SW), lambda c: (c, 0)), pl.BlockSpec((TC, SW), lambda c: (c, 0)),
                   pl.BlockSpec((TC, NS), lambda c: (c, 0)), pl.BlockSpec((TC, NS), lambda c: (c, 0)),
                   pl.BlockSpec((8, NS), lambda c: (c, 0)), pl.BlockSpec((8, NS), lambda c: (c, 0))],
        out_shape=[jax.ShapeDtypeStruct((L, SW), f32), jax.ShapeDtypeStruct((L, SW), bf16),
                   jax.ShapeDtypeStruct((L, NS), f32), jax.ShapeDtypeStruct((L, NS), f32),
                   jax.ShapeDtypeStruct((nc * 8, NS), f32), jax.ShapeDtypeStruct((nc * 8, NS), f32)],
        scratch_shapes=[pltpu.VMEM((8, NS), f32), pltpu.VMEM((8, NS), f32)],
        name="ssm_fwd", compiler_params=_cp("arbitrary"))(
            proj, proj, bbt_re, bbt_im, ct_re, ct_im, a_re, a_im, apow_re, apow_im, dvec, wglu, bglu)


def _tail(a_in, b_in, proj, x, tgt, wco, wso, wout, gpost):
    L = x.shape[0]
    tm = 512
    def body(a_ref, b_ref, gc_ref, gs_ref, x_ref, t_ref, wco_ref, wso_ref, wout_ref, gp_ref,
             gx_ref, dain_ref, dbin_ref, dp_ref, dwout_ref, dwco_ref, dwso_ref, dgp_ref, loss_ref):
        @pl.when(pl.program_id(0) == 0)
        def _():
            dwout_ref[...] = jnp.zeros_like(dwout_ref)
            dwco_ref[...] = jnp.zeros_like(dwco_ref)
            dwso_ref[...] = jnp.zeros_like(dwso_ref)
            dgp_ref[...] = jnp.zeros_like(dgp_ref)
            loss_ref[...] = jnp.zeros_like(loss_ref)
        a = a_ref[...]
        b = b_ref[...]
        co = _dot(a, wco_ref[...])
        so = jnp.concatenate([_dot(b, wso_ref[j]) for j in range(NCHIP)], axis=1)
        sc = _sig(gc_ref[...].astype(f32))
        ss = _sig(gs_ref[...].astype(f32))
        mb = (sc * co + ss * so).astype(bf16)
        out = _dot(mb, wout_ref[...])
        r2 = lax.rsqrt(jnp.mean(out * out, axis=-1, keepdims=True) + RMS_EPS)
        on = out * r2
        gp = gp_ref[...]
        e = x_ref[...] + on * gp - t_ref[...]
        loss_ref[...] += (0.5 / D) * jnp.sum(e * e)
        dy = e * (1.0 / D)
        gx_ref[...] = dy
        dgp_ref[...] += jnp.sum(dy * on, axis=0, keepdims=True)
        dn = dy * gp
        dout = (r2 * (dn - on * jnp.mean(dn * on, axis=-1, keepdims=True))).astype(bf16)
        dwout_ref[...] += _dot_tn(mb, dout)
        dm = _dot_nt(dout, wout_ref[...])
        dp_ref[:, 0:D] = (dm * co * sc * (1.0 - sc)).astype(bf16)
        dp_ref[:, D:2 * D] = (dm * so * ss * (1.0 - ss)).astype(bf16)
        dco = (dm * sc).astype(bf16)
        dso = (dm * ss).astype(bf16)
        dwco_ref[...] += _dot_tn(a, dco)
        dbin = None
        for j in range(NCHIP):
            dso_j = dso[:, j * 256:(j + 1) * 256]
            dwso_ref[j] += _dot_tn(b, dso_j)
            t = _dot_nt(dso_j, wso_ref[j])
            dbin = t if dbin is None else dbin + t
        dain_ref[...] = _dot_nt(dco, wco_ref[...]).astype(bf16)
        dbin_ref[...] = dbin.astype(bf16)

    row = lambda w: pl.BlockSpec((tm, w), lambda i: (i, 0))
    one = lambda shape: pl.BlockSpec(shape, lambda i: (0,) * len(shape), pipeline_mode=pl.Buffered(1))
    return pl.pallas_call(
        body, grid=(L // tm,),
        in_specs=[row(CW), row(SW), pl.BlockSpec((tm, D), lambda i: (i, 4)), pl.BlockSpec((tm, D), lambda i: (i, 5)),
                  row(D), row(D), one((CW, D)), one((NCHIP, SW, 256)), one((D, D)), one((1, D))],
        out_specs=[row(D), row(CW), row(SW), pl.BlockSpec((tm, 2 * D), lambda i: (i, 2)),
                   one((D, D)), one((CW, D)), one((NCHIP, SW, 256)), one((1, D)), one((1, 128))],
        out_shape=[jax.ShapeDtypeStruct((L, D), f32), jax.ShapeDtypeStruct((L, CW), bf16),
                   jax.ShapeDtypeStruct((L, SW), bf16), jax.ShapeDtypeStruct((L, IN_W), bf16),
                   jax.ShapeDtypeStruct((D, D), f32), jax.ShapeDtypeStruct((CW, D), f32),
                   jax.ShapeDtypeStruct((NCHIP, SW, 256), f32), jax.ShapeDtypeStruct((1, D), f32),
                   jax.ShapeDtypeStruct((1, 128), f32)],
        name="tail", compiler_params=_cp("arbitrary"))(a_in, b_in, proj, proj, x, tgt, wco, wso, wout, gpost)


def _ssm_bwd(d_bin, y0, proj, sre, sim, cinr, cini, bbt_re, bbt_im, ct_re, ct_im,
             a_re, a_im, apow_re, apow_im, dvec, wglu, bglu, dproj):
    L = y0.shape[0]
    nc = L // TC
    def body(dbin_ref, y0_ref, u_ref, zs_ref, sre_ref, sim_ref, cinr_ref, cini_ref,
             bre_ref, bim_ref, cre_ref, cim_ref, are_ref, aim_ref, pwr_ref, pwi_ref, d_ref, wg_ref, bg_ref, _,
             dp_ref, dbre_ref, dbim_ref, dcre_ref, dcim_ref, dd_ref, dar_ref, dai_ref, dwg_ref, dbg_ref,
             gre, gim, gcr, gci, nxt_re, nxt_im):
        @pl.when(pl.program_id(0) == 0)
        def _():
            for ref in (dbre_ref, dbim_ref, dcre_ref, dcim_ref, dd_ref, dar_ref, dai_ref, dwg_ref, dbg_ref,
                        nxt_re, nxt_im):
                ref[...] = jnp.zeros_like(ref)
        y0 = y0_ref[...]
        u = u_ref[...]
        zs = zs_ref[...].astype(f32)
        dbin = dbin_ref[...].astype(f32)
        t, y1 = _gelu_parts(y0)
        y1b = y1.astype(bf16)
        sg = _sig(_dot(y1b, wg_ref[...]) + bg_ref[...])
        sz = _sig(zs)
        d_y2 = dbin * (zs * sz)
        dp_ref[:, SW:2 * SW] = (dbin * (y1 * sg) * (sz * (1.0 + zs * (1.0 - sz)))).astype(bf16)
        d_glu = d_y2 * y1 * sg * (1.0 - sg)
        d_glub = d_glu.astype(bf16)
        d_y1 = d_y2 * sg + _dot_nt(d_glub, wg_ref[...])
        dwg_ref[...] += _dot_tn(y1b, d_glub)
        dbg_ref[...] += jnp.sum(d_glu, axis=0, keepdims=True)
        dgelu = 0.5 * (1.0 + t) + 0.5 * y0 * (1.0 - t * t) * GELU_K0 * (1.0 + 3.0 * GELU_K1 * y0 * y0)
        d_y0 = d_y1 * dgelu
        dd_ref[...] += jnp.sum(d_y0 * u.astype(f32), axis=0, keepdims=True)
        dyb = d_y0.astype(bf16)
        for blk in range(4):
            dy1 = dyb[:, 128 * blk:128 * (blk + 1)]
            gre[:, 512 * blk:512 * (blk + 1)] = _dot_nt(dy1, cre_ref[blk])
            gim[:, 512 * blk:512 * (blk + 1)] = -_dot_nt(dy1, cim_ref[blk])
        for lb in range(NS // LBW):
            sl = slice(lb * LBW, (lb + 1) * LBW)
            ar = jnp.broadcast_to(are_ref[:, sl], (8, LBW))
            ai = jnp.broadcast_to(aim_ref[:, sl], (8, LBW))
            def step(k, carry, sl=sl, ar=ar, ai=ai):
                gr, gi = carry
                row = _rows8(R - 2 - k)
                nr = ar * gr + ai * gi + gre[row, sl]
                ni = ar * gi - ai * gr + gim[row, sl]
                gre[row, sl] = nr
                gim[row, sl] = ni
                return nr, ni
            lax.fori_loop(0, R - 1, step, (gre[8 * (R - 1):8 * R, sl], gim[8 * (R - 1):8 * R, sl]))
        a_r = pwr_ref[R - 1:R, :]
        a_i = pwi_ref[R - 1:R, :]
        cr = nxt_re[0:1, :]
        ci = nxt_im[0:1, :]
        for seg in range(7, -1, -1):
            gcr[seg:seg + 1, :] = cr
            gci[seg:seg + 1, :] = ci
            er = gre[seg:seg + 1, :]
            ei = gim[seg:seg + 1, :]
            cr, ci = er + a_r * cr + a_i * ci, ei + a_r * ci - a_i * cr
        nxt_re[0:1, :] = cr
        nxt_im[0:1, :] = ci
        for lb in range(NS // LBW):
            sl = slice(lb * LBW, (lb + 1) * LBW)
            kr = gcr[:, sl]
            ki = gci[:, sl]
            def fixed(rows, prow, sl=sl, kr=kr, ki=ki):
                pr = jnp.broadcast_to(pwr_ref[prow, sl], (8, LBW))
                pi = jnp.broadcast_to(pwi_ref[prow, sl], (8, LBW))
                gr = gre[rows, sl] + pr * kr + pi * ki
                gi = gim[rows, sl] + pr * ki - pi * kr
                gre[rows, sl] = gr
                gim[rows, sl] = gi
                return gr, gi
            g0r, g0i = fixed(slice(0, 8), slice(R - 1, R))
            p0r, p0i = cinr_ref[:, sl], cini_ref[:, sl]
            acc0 = (g0r * p0r + g0i * p0i, g0i * p0r - g0r * p0i)
            def dacc(r, carry, sl=sl, fixed=fixed):
                xr, xi = carry
                gr, gi = fixed(_rows8(r), pl.ds(R - 1 - r, 1))
                pr, pi = sre_ref[_rows8(r - 1), sl], sim_ref[_rows8(r - 1), sl]
                return xr + gr * pr + gi * pi, xi + gi * pr - gr * pi
            xr, xi = lax.fori_loop(1, R, dacc, acc0)
            dar_ref[:, sl] += xr
            dai_ref[:, sl] += xi
        dup = []
        for blk in range(4):
            s4 = slice(512 * blk, 512 * (blk + 1))
            s1 = slice(128 * blk, 128 * (blk + 1))
            grb = gre[:, s4].astype(bf16)
            gib = gim[:, s4].astype(bf16)
            dup.append(_dot_nt(grb, bre_ref[blk]) + _dot_nt(gib, bim_ref[blk]))
            dbre_ref[blk] += _dot_tn(u[:, s1], grb)
            dbim_ref[blk] += _dot_tn(u[:, s1], gib)
            dcre_ref[blk] += _dot_tn(dyb[:, s1], sre_ref[:, s4].astype(bf16))
            dcim_ref[blk] -= _dot_tn(dyb[:, s1], sim_ref[:, s4].astype(bf16))
        dp_ref[:, 0:SW] = (jnp.concatenate(dup, axis=1) + d_ref[...] * d_y0).astype(bf16)

    rev = lambda w, cidx: pl.BlockSpec((TC, w), lambda i, cidx=cidx: (nc - 1 - i, cidx))
    one = lambda shape: pl.BlockSpec(shape, lambda i: (0,) * len(shape))
    return pl.pallas_call(
        body, grid=(nc,),
        in_specs=[rev(SW, 0), rev(SW, 0), rev(SW, 6), rev(SW, 7), rev(NS, 0), rev(NS, 0),
                  pl.BlockSpec((8, NS), lambda i: (nc - 1 - i, 0)), pl.BlockSpec((8, NS), lambda i: (nc - 1 - i, 0)),
                  one((4, 128, 512)), one((4, 128, 512)), one((4, 512, 128)), one((4, 512, 128)),
                  one((1, NS)), one((1, NS)), one((R, NS)), one((R, NS)),
                  one((1, SW)), one((SW, SW)), one((1, SW)), _ANY],
        out_specs=[pl.BlockSpec((TC, 2 * SW), lambda i: (nc - 1 - i, 3)),
                   one((4, 128, 512)), one((4, 128, 512)), one((4, 128, 512)), one((4, 128, 512)),
                   one((1, SW)), one((8, NS)), one((8, NS)), one((SW, SW)), one((1, SW))],
        out_shape=[jax.ShapeDtypeStruct((L, IN_W), bf16),
                   jax.ShapeDtypeStruct((4, 128, 512), f32), jax.ShapeDtypeStruct((4, 128, 512), f32),
                   jax.ShapeDtypeStruct((4, 128, 512), f32), jax.ShapeDtypeStruct((4, 128, 512), f32),
                   jax.ShapeDtypeStruct((1, SW), f32), jax.ShapeDtypeStruct((8, NS), f32),
                   jax.ShapeDtypeStruct((8, NS), f32), jax.ShapeDtypeStruct((SW, SW), f32),
                   jax.ShapeDtypeStruct((1, SW), f32)],
        scratch_shapes=[pltpu.VMEM((TC, NS), f32), pltpu.VMEM((TC, NS), f32), pltpu.VMEM((8, NS), f32),
                        pltpu.VMEM((8, NS), f32), pltpu.VMEM((8, NS), f32), pltpu.VMEM((8, NS), f32)],
        input_output_aliases={19: 0},
        name="ssm_bwd", compiler_params=_cp("arbitrary"))(
            d_bin, y0, proj, proj, sre, sim, cinr, cini, bbt_re, bbt_im, ct_re, ct_im,
            a_re, a_im, apow_re, apow_im, dvec, wglu, bglu, dproj)


def _conv_bwd(d_ain, cu1, proj, cw, lng, lnb, dproj):
    L = cu1.shape[0]
    nc = L // TC
    def body(dain_ref, cu1_ref, ca_ref, cb_ref, zc_ref, cah_ref, cbh_ref, w_ref, g_ref, bb_ref, _,
             dp_ref, dw_ref, dbias_ref, dlng_ref, dlnb_ref, dbuf, ebuf, prev, nxt, dcu0):
        i = pl.program_id(0)
        @pl.when(i == 0)
        def _():
            dw_ref[...] = jnp.zeros_like(dw_ref)
            dbias_ref[...] = jnp.zeros_like(dbias_ref)
            dlng_ref[...] = jnp.zeros_like(dlng_ref)
            dlnb_ref[...] = jnp.zeros_like(dlnb_ref)
            nxt[...] = jnp.zeros_like(nxt)
        def lnb(s, carry):
            rows = pl.ds(pl.multiple_of(s * 32, 32), 32)
            dain = dain_ref[rows, :].astype(f32)
            c1 = cu1_ref[rows, :].astype(f32)
            zc = zc_ref[rows, :].astype(f32)
            xc = c1 - jnp.mean(c1, axis=-1, keepdims=True)
            var = jnp.mean(xc * xc, axis=-1, keepdims=True)
            rstd = lax.rsqrt(var + LN_EPS)
            xh = xc * rstd
            ln = xh * g_ref[...] + bb_ref[...]
            sl_ = _sig(ln)
            sz = _sig(zc)
            dp_ref[rows, 2 * CW:3 * CW] = (dain * (ln * sl_) * (sz * (1.0 + zc * (1.0 - sz)))).astype(bf16)
            d_ln = dain * (zc * sz) * (sl_ * (1.0 + ln * (1.0 - sl_)))
            dlng_ref[...] += jnp.sum(d_ln * xh, axis=0, keepdims=True)
            dlnb_ref[...] += jnp.sum(d_ln, axis=0, keepdims=True)
            dxh = d_ln * g_ref[...]
            d_c1 = rstd * (dxh - jnp.mean(dxh, axis=-1, keepdims=True)
                           - xh * jnp.mean(dxh * xh, axis=-1, keepdims=True))
            dbias_ref[...] += jnp.sum(d_c1, axis=0, keepdims=True)
            _put_blocked(dbuf, pl.multiple_of(s * 32, 32), 32, d_c1)
            _put_blocked(ebuf, pl.multiple_of(NH * 8 + s * 32, 32), 32,
                         ca_ref[rows, :].astype(f32) * _sig(cb_ref[rows, :].astype(f32)))
            return carry
        lax.fori_loop(0, TC // 32, lnb, 0, unroll=4)
        sub = lax.broadcasted_iota(jnp.int32, (8, 128), 0)
        def after(p, carry):
            for lb in range(NLB):
                cur = dbuf[lb, _rows8(p), :]
                dbuf[lb, _rows8(R + p), :] = jnp.where(sub == 7, pltpu.roll(nxt[lb, _rows8(p), :], 7, 0),
                                                       pltpu.roll(cur, 7, 0))
            return carry
        lax.fori_loop(0, NH, after, 0)
        nxt[...] = dbuf[:, 0:NH * 8, :]
        def before(s, carry):
            rows = pl.ds(pl.multiple_of(s * 64, 64), 64)
            v = cah_ref[rows, :].astype(f32) * _sig(cbh_ref[rows, :].astype(f32))
            _put_blocked(prev, pl.multiple_of(s * 64, 64), 64, jnp.where(i == nc - 1, jnp.zeros_like(v), v))
            return carry
        lax.fori_loop(0, NH * 8 // 64, before, 0)
        _fill_before(ebuf, prev)
        for lb in range(NLB):
            sl = slice(lb * 128, (lb + 1) * 128)
            wk = [jnp.broadcast_to(w_ref[k:k + 1, sl], (8, 128)) for k in range(KS)]
            def tap(q, carry, lb=lb, wk=wk):
                r = q * RPI
                for j, o in enumerate(_fir(dbuf, lb, r, wk, None, True)):
                    dcu0[lb, _rows8(r + j), :] = o
                return carry
            lax.fori_loop(0, R // RPI, tap, 0)
            def wgrad(q, accs, lb=lb):
                r = q * RPI
                dvs = dbuf[lb, pl.ds(pl.multiple_of(r * 8, 8), RPI * 8), :]
                win = ebuf[lb, pl.ds(pl.multiple_of((r + (NH - KS + 1)) * 8, 8), (KS + RPI - 1) * 8), :]
                accs = list(accs)
                for j in range(RPI):
                    dv = dvs[8 * j:8 * j + 8, :]
                    for k in range(KS):
                        accs[k] = accs[k] + dv * win[8 * (j + k):8 * (j + k) + 8, :]
                return tuple(accs)
            accs = lax.fori_loop(0, R // RPI, wgrad, tuple(jnp.zeros((8, 128), f32) for _ in range(KS)))
            for k in range(KS):
                dw_ref[k, :, sl] += accs[k]
        def glub(s, carry):
            rows = pl.ds(pl.multiple_of(s * 64, 64), 64)
            d0 = _get_blocked(dcu0, pl.multiple_of(s * 64, 64), 64)
            ca = ca_ref[rows, :].astype(f32)
            sb = _sig(cb_ref[rows, :].astype(f32))
            dp_ref[rows, 0:CW] = (d0 * sb).astype(bf16)
            dp_ref[rows, CW:2 * CW] = (d0 * ca * sb * (1.0 - sb)).astype(bf16)
            return carry
        lax.fori_loop(0, TC // 64, glub, 0)

    hrows = NH * 8
    per = TC // hrows
    rev = lambda cidx: pl.BlockSpec((TC, CW), lambda i, cidx=cidx: (nc - 1 - i, cidx))
    halo = lambda cidx: pl.BlockSpec((hrows, CW), lambda i, cidx=cidx: (jnp.maximum((nc - 1 - i) * per - 1, 0), cidx))
    one = lambda shape: pl.BlockSpec(shape, lambda i: (0,) * len(shape))
    return pl.pallas_call(
        body, grid=(nc,),
        in_specs=[rev(0), rev(0), rev(0), rev(1), rev(2), halo(0), halo(1), one((32, CW)), one((1, CW)), one((1, CW)),
                  _ANY],
        out_specs=[pl.BlockSpec((TC, 3 * CW), lambda i: (nc - 1 - i, 0)), one((32, 8, CW)), one((1, CW)), one((1, CW)), one((1, CW))],
        out_shape=[jax.ShapeDtypeStruct((L, IN_W), bf16), jax.ShapeDtypeStruct((32, 8, CW), f32),
                   jax.ShapeDtypeStruct((1, CW), f32), jax.ShapeDtypeStruct((1, CW), f32),
                   jax.ShapeDtypeStruct((1, CW), f32)],
        scratch_shapes=[pltpu.VMEM((NLB, (R + NH) * 8, 128), f32), pltpu.VMEM((NLB, (NH + R) * 8, 128), f32),
                        pltpu.VMEM((NLB, hrows, 128), f32), pltpu.VMEM((NLB, hrows, 128), f32),
                        pltpu.VMEM((NLB, TC, 128), f32)],
        input_output_aliases={10: 0},
        name="conv_bwd", compiler_params=_cp("arbitrary"))(d_ain, cu1, proj, proj, proj, proj, proj, cw, lng, lnb, dproj)


def _win_grad(h, dproj):
    L = h.shape[0]
    tm = min(1024, L)
    nt = L // tm
    def body(h_ref, d_ref, o_ref, acc):
        i = pl.program_id(1)
        @pl.when(i == 0)
        def _():
            acc[...] = jnp.zeros_like(acc)
        acc[...] += _dot_tn(h_ref[...], d_ref[...])
        @pl.when(i == nt - 1)
        def _():
            o_ref[0] = acc[...].astype(bf16)
    return pl.pallas_call(
        body, grid=(NCHIP, nt),
        in_specs=[pl.BlockSpec((tm, D), lambda j, i: (i, 0)), pl.BlockSpec((tm, SHARD_W), lambda j, i: (i, j))],
        out_specs=pl.BlockSpec((1, D, SHARD_W), lambda j, i: (j, 0, 0)),
        out_shape=jax.ShapeDtypeStruct((NCHIP, D, SHARD_W), bf16),
        scratch_shapes=[pltpu.VMEM((D, SHARD_W), f32)],
        name="win_grad", compiler_params=_cp("arbitrary", "arbitrary"))(h, dproj)


def _adamw_math(w, g, m, v):
    m2 = B1 * m + (1.0 - B1) * g
    v2 = B2 * v + (1.0 - B2) * (g * g)
    m_hat = m2 / (1.0 - B1 ** STEP)
    v_hat = v2 / (1.0 - B2 ** STEP)
    delta = -LR * (m_hat / (jnp.sqrt(v_hat) + EPS) + WD * w)
    return delta, m2, v2


def _adamw(name, w, g, m, v):
    rows, cols = w.shape
    tm = rows if rows <= 256 else (256 if rows % 256 == 0 else 128)
    assert rows % tm == 0
    def body(w_ref, g_ref, m_ref, v_ref, d_ref, m2_ref, v2_ref):
        d, m2, v2 = _adamw_math(w_ref[...], g_ref[...], m_ref[...], v_ref[...])
        d_ref[...] = d
        m2_ref[...] = m2
        v2_ref[...] = v2
    spec = pl.BlockSpec((tm, cols), lambda i: (i, 0))
    shp = jax.ShapeDtypeStruct((rows, cols), f32)
    return pl.pallas_call(
        body, grid=(rows // tm,), in_specs=[spec] * 4, out_specs=[spec] * 3, out_shape=[shp] * 3,
        name=name, compiler_params=_cp("arbitrary"))(w, g, m, v)


def _adamw_group(name, ws, gs, ms, vs):
    n = len(ws)
    def body(*refs):
        for i in range(n):
            w_ref, g_ref, m_ref, v_ref = (refs[q * n + i] for q in range(4))
            d, m2, v2 = _adamw_math(w_ref[...], g_ref[...], m_ref[...], v_ref[...])
            for q, val in enumerate((d, m2, v2)):
                refs[(4 + q) * n + i][...] = val
    shapes = [jax.ShapeDtypeStruct(w.shape, f32) for w in ws]
    out = pl.pallas_call(body, out_shape=shapes * 3, name=name,
                         compiler_params=pltpu.CompilerParams(vmem_limit_bytes=VMEM_LIMIT))(*ws, *gs, *ms, *vs)
    return [(out[i], out[n + i], out[2 * n + i]) for i in range(n)]


_ANY = pl.BlockSpec(memory_space=pl.ANY)


def _chunks(rows, parts):
    step = rows // parts
    assert step * parts == rows and step % 16 == 0
    return [(i * step, step) for i in range(parts)]


def _place():
    x, y, c = lax.axis_index("x"), lax.axis_index("y"), lax.axis_index("c")
    chips = [(1 - x, y), (x, 1 - y), (1 - x, 1 - y)]
    return x, y, c, chips


def _nchunks(half, cols, itemsize):
    return 4 if half * cols * itemsize >= (1 << 20) else 1


def _segments(metas):
    segs = []
    for w, (half, cols, dt) in enumerate(metas):
        for r0, n in _chunks(half, _nchunks(half, cols, jnp.dtype(dt).itemsize)):
            segs.append((w, half, r0, n))
    return segs


def _rcopy(i, src, dst, send_sems, recv_sems, to):
    return pltpu.make_async_remote_copy(src_ref=src, dst_ref=dst, send_sem=send_sems.at[i], recv_sem=recv_sems.at[i],
                                        device_id=to, device_id_type=MESH)


def _gather_prep(k_arr, shards, x, tgt, g_pre, perm):
    na = len(shards)
    L = x.shape[0]
    nc = L // TC
    segs = _segments([(a.shape[0] // 2, a.shape[1], a.dtype) for a in shards])
    ns = len(segs)
    def body(_, *refs):
        ins = refs[:na]
        x_ref, t_ref, g_ref, p_ref = refs[na:na + 4]
        outs = refs[na + 4:2 * na + 4]
        h_ref, xi_ref, ti_ref, proj_ref = refs[2 * na + 4:2 * na + 8]
        stages = refs[2 * na + 8:3 * na + 8]
        send_sems, recv_sems, local_sems = refs[3 * na + 8:]
        i = pl.program_id(0)
        x, y, c, chips = _place()
        k = 2 * x + y
        me, sibling = (x, y, c), (x, y, 1 - c)

        def dst(w, half, chip, pc, r0, n):
            return outs[w].at[chip, pl.ds(pc * half + r0, n), :]

        def firsts():
            return [_rcopy(j * ns + s, ins[w].at[pl.ds(c * half + r0, n), :], dst(w, half, k, c, r0, n),
                           send_sems, recv_sems, (*chip, c))
                    for j, chip in enumerate(chips) for s, (w, half, r0, n) in enumerate(segs)]

        def own_out(w):
            return pltpu.make_async_copy(stages[w], outs[w].at[k], local_sems.at[w])

        @pl.when(i == 0)
        def _():
            for cp in firsts():
                cp.start()
            for w in range(na):
                cin = pltpu.make_async_copy(ins[w], stages[w], local_sems.at[w])
                cin.start()
                cin.wait()
            for w in range(na):
                own_out(w).start()

        p = p_ref[...]
        def through(v):
            hi = v.astype(bf16)
            r1 = v - hi.astype(f32)
            mid = r1.astype(bf16)
            lo = (r1 - mid.astype(f32)).astype(bf16)
            return (_dot(p, hi) + _dot(p, mid)) + _dot(p, lo)
        xt = x_ref[...]
        r = lax.rsqrt(jnp.mean(xt * xt, axis=-1, keepdims=True) + RMS_EPS)
        hp = _dot(p, (xt * r * g_ref[...]).astype(bf16)).astype(bf16)
        h_ref[...] = hp
        proj_ref[...] = _dot(hp, stages[0][...]).astype(bf16)
        xi_ref[...] = through(xt)
        ti_ref[...] = through(t_ref[...])

        @pl.when(i == nc - 1)
        def _():
            passed = []
            for j, chip in enumerate(chips):
                cj = 2 * chip[0] + chip[1]
                for s, (w, half, r0, n) in enumerate(segs):
                    landed = dst(w, half, cj, c, r0, n)
                    _rcopy(j * ns + s, landed, landed, send_sems, recv_sems, me).wait_recv()
                    fwd = _rcopy(3 * ns + j * ns + s, landed, landed, send_sems, recv_sems, sibling)
                    fwd.start()
                    passed.append(fwd)
            for j, chip in enumerate(chips):
                cj = 2 * chip[0] + chip[1]
                for s, (w, half, r0, n) in enumerate(segs):
                    theirs = dst(w, half, cj, 1 - c, r0, n)
                    _rcopy(3 * ns + j * ns + s, theirs, theirs, send_sems, recv_sems, me).wait_recv()
            for cp in firsts() + passed:
                cp.wait_send()
            for w in range(na):
                own_out(w).wait()

    row = lambda: pl.BlockSpec((TC, D), lambda i, k: (i, 0))
    grid_spec = pltpu.PrefetchScalarGridSpec(
        num_scalar_prefetch=1, grid=(nc,),
        in_specs=[_ANY] * na + [row(), row(), pl.BlockSpec((1, D), lambda i, k: (0, 0)),
                                pl.BlockSpec((TC, TC), lambda i, k: (0, 0))],
        out_specs=[_ANY] * na + [row(), row(), row(), pl.BlockSpec((TC, SHARD_W), lambda i, k: (i, k[0]))],
        scratch_shapes=[pltpu.VMEM(a.shape, a.dtype) for a in shards]
        + [pltpu.SemaphoreType.DMA((6 * ns,)), pltpu.SemaphoreType.DMA((6 * ns,)), pltpu.SemaphoreType.DMA((na,))])
    return pl.pallas_call(
        body, grid_spec=grid_spec,
        out_shape=[jax.ShapeDtypeStruct((NCHIP,) + a.shape, a.dtype) for a in shards]
        + [jax.ShapeDtypeStruct((L, D), bf16), jax.ShapeDtypeStruct((L, D), f32), jax.ShapeDtypeStruct((L, D), f32),
           jax.ShapeDtypeStruct((L, IN_W), bf16)],
        name="gather_prep", compiler_params=_cp("arbitrary"))(k_arr, *shards, x, tgt, g_pre, perm)


def _x_grad_exchange(dproj, w_in, x, gx0, g_pre, parts, small):
    L = x.shape[0]
    tm = 512
    nt = L // tm
    na = len(parts)
    segs = _segments([(p.shape[1], p.shape[2], p.dtype) for p in parts])
    ns = len(segs) + 1
    def body(*refs):
        d_ref, w_ref, x_ref, gx_ref, g_ref = refs[:5]
        ins, s_ref = refs[5:5 + na], refs[5 + na]
        o_ref, dg_ref = refs[6 + na:8 + na]
        outs, qs_ref = refs[8 + na:8 + 2 * na], refs[8 + 2 * na]
        stages = refs[9 + 2 * na:10 + 3 * na]
        send_sems, recv_sems, local_sems = refs[10 + 3 * na:]
        i = pl.program_id(0)
        x, y, c, chips = _place()
        k = 2 * x + y

        def copies():
            out = []
            for j, chip in enumerate(chips):
                cj = 2 * chip[0] + chip[1]
                pieces = [(s_ref, qs_ref.at[k])]
                pieces += [(ins[w].at[cj, pl.ds(r0, n), :], outs[w].at[k, pl.ds(r0, n), :]) for w, _, r0, n in segs]
                out += [_rcopy(ns * j + s, src, d, send_sems, recv_sems, (*chip, c)) for s, (src, d) in enumerate(pieces)]
            return out

        def own_out(w):
            dst = qs_ref.at[k] if w == na else outs[w].at[k]
            return pltpu.make_async_copy(stages[w], dst, local_sems.at[w])

        @pl.when(i == 0)
        def _():
            dg_ref[...] = jnp.zeros_like(dg_ref)
            for cp in copies():
                cp.start()
            for w in range(na + 1):
                cin = pltpu.make_async_copy(s_ref if w == na else ins[w].at[k], stages[w], local_sems.at[w])
                cin.start()
                cin.wait()
            for w in range(na + 1):
                own_out(w).start()

        dh = _dot_nt(d_ref[:, 0:SHARD_W], w_ref[0])
        for j in range(1, NCHIP):
            dh = dh + _dot_nt(d_ref[:, j * SHARD_W:(j + 1) * SHARD_W], w_ref[j])
        xt = x_ref[...]
        r = lax.rsqrt(jnp.mean(xt * xt, axis=-1, keepdims=True) + RMS_EPS)
        xn = xt * r
        dg_ref[...] += jnp.sum(dh * xn, axis=0, keepdims=True)
        dxn = dh * g_ref[...]
        o_ref[...] = gx_ref[...] + r * (dxn - xn * jnp.mean(dxn * xn, axis=-1, keepdims=True))

        @pl.when(i == nt - 1)
        def _():
            for cp in copies():
                cp.wait_recv()
            for cp in copies():
                cp.wait_send()
            for w in range(na + 1):
                own_out(w).wait()

    return pl.pallas_call(
        body, grid=(nt,),
        in_specs=[pl.BlockSpec((tm, IN_W), lambda i: (i, 0)),
                  pl.BlockSpec((NCHIP, D, SHARD_W), lambda i: (0, 0, 0), pipeline_mode=pl.Buffered(1)),
                  pl.BlockSpec((tm, D), lambda i: (i, 0)), pl.BlockSpec((tm, D), lambda i: (i, 0)), _full((1, D))]
        + [_ANY] * (na + 1),
        out_specs=[pl.BlockSpec((tm, D), lambda i: (i, 0)), _full((1, D))] + [_ANY] * (na + 1),
        out_shape=[jax.ShapeDtypeStruct((L, D), f32), jax.ShapeDtypeStruct((1, D), f32)]
        + [jax.ShapeDtypeStruct(p.shape, bf16) for p in parts] + [jax.ShapeDtypeStruct((NCHIP, SMALL_ROWS, 128), f32)],
        scratch_shapes=[pltpu.VMEM(p.shape[1:], bf16) for p in parts] + [pltpu.VMEM((SMALL_ROWS, 128), f32)]
        + [pltpu.SemaphoreType.DMA((3 * ns,)), pltpu.SemaphoreType.DMA((3 * ns,)), pltpu.SemaphoreType.DMA((na + 1,))],
        name="x_grad_exchange", compiler_params=_cp("arbitrary"))(dproj, w_in, x, gx0, g_pre, *parts, small)


def _sibling_join_list(halves):
    na = len(halves)
    segs = _segments([(h.shape[0], h.shape[1], h.dtype) for h in halves])
    def body(*refs):
        ins, outs, stages = refs[:na], refs[na:2 * na], refs[2 * na:3 * na]
        send_sems, recv_sems, local_sems = refs[3 * na:]
        x, y, c, _ = _place()
        copies = [_rcopy(i, ins[w].at[pl.ds(r0, n), :], outs[w].at[pl.ds(c * half + r0, n), :], send_sems, recv_sems,
                         (x, y, 1 - c)) for i, (w, half, r0, n) in enumerate(segs)]
        for cp in copies:
            cp.start()
        own = []
        for w in range(na):
            cin = pltpu.make_async_copy(ins[w], stages[w], local_sems.at[w])
            cin.start()
            cin.wait()
            half = halves[w].shape[0]
            own.append(pltpu.make_async_copy(stages[w], outs[w].at[pl.ds(c * half, half), :], local_sems.at[w]))
            own[-1].start()
        for cp in copies:
            cp.wait_recv()
        for cp in copies:
            cp.wait_send()
        for cp in own:
            cp.wait()

    return pl.pallas_call(
        body, in_specs=[_ANY] * na, out_specs=[_ANY] * na,
        out_shape=[jax.ShapeDtypeStruct((2 * h.shape[0], h.shape[1]), f32) for h in halves],
        scratch_shapes=[pltpu.VMEM(h.shape, f32) for h in halves]
        + [pltpu.SemaphoreType.DMA((len(segs),)), pltpu.SemaphoreType.DMA((len(segs),)), pltpu.SemaphoreType.DMA((na,))],
        name="sibling_join")(*halves)


def _allgather_rows(v):
    def body(v_ref, o_ref, send_sems, recv_sems):
        x, y, c, _ = _place()
        me = 4 * x + 2 * y + c
        o_ref[me] = v_ref[...]
        copies = []
        i = 0
        for dx in range(2):
            for dy in range(2):
                for dc in range(2):
                    if dx + dy + dc:
                        copies.append(_rcopy(i, v_ref, o_ref.at[me], send_sems, recv_sems, (x ^ dx, y ^ dy, c ^ dc)))
                        i += 1
        for cp in copies:
            cp.start()
        for cp in copies:
            cp.wait_recv()
        for cp in copies:
            cp.wait_send()

    vm = pl.BlockSpec(memory_space=pltpu.VMEM)
    return pl.pallas_call(
        body, in_specs=[vm], out_specs=vm, out_shape=jax.ShapeDtypeStruct((8, 8, 128), f32),
        scratch_shapes=[pltpu.SemaphoreType.DMA((7,)), pltpu.SemaphoreType.DMA((7,))],
        name="allgather_rows")(v)


def _adamw_rows(parts, w, m, v):
    def body(p_ref, w_ref, m_ref, v_ref, g_ref, d_ref, m2_ref, v2_ref):
        g = p_ref[0]
        for dvc in range(1, 8):
            g = g + p_ref[dvc]
        g_ref[...] = g
        d, m2, v2 = _adamw_math(w_ref[...], g, m_ref[...], v_ref[...])
        d_ref[...] = d
        m2_ref[...] = m2
        v2_ref[...] = v2
    return pl.pallas_call(body, out_shape=[jax.ShapeDtypeStruct((8, 128), f32)] * 4, name="adamw_pre_norm_gain")(
        parts, w, m, v)


def _pair_exchange_list(grads, small):
    na = len(grads)
    segs = _segments([(g.shape[1] // 2, g.shape[2], g.dtype) for g in grads])
    n = NCHIP * len(segs) + 1
    def body(*refs):
        ins, s_ref, outs, rs_ref, (send_sems, recv_sems) = (refs[:na], refs[na], refs[na + 1:2 * na + 1],
                                                            refs[2 * na + 1], refs[2 * na + 2:])
        x, y, c, _ = _place()
        pieces = [(s_ref, rs_ref)]
        for j in range(NCHIP):
            for w, half, r0, rows in segs:
                pieces.append((ins[w].at[j, pl.ds((1 - c) * half + r0, rows), :], outs[w].at[j, pl.ds(r0, rows), :]))
        copies = [_rcopy(i, s, d, send_sems, recv_sems, (x, y, 1 - c)) for i, (s, d) in enumerate(pieces)]
        for cp in copies:
            cp.start()
        for cp in copies:
            cp.wait_recv()
        for cp in copies:
            cp.wait_send()

    return pl.pallas_call(
        body, in_specs=[_ANY] * (na + 1), out_specs=[_ANY] * (na + 1),
        out_shape=[jax.ShapeDtypeStruct((NCHIP, g.shape[1] // 2, g.shape[2]), g.dtype) for g in grads]
        + [jax.ShapeDtypeStruct((SMALL_ROWS, 128), f32)],
        scratch_shapes=[pltpu.SemaphoreType.DMA((n,)), pltpu.SemaphoreType.DMA((n,))],
        name="pair_exchange")(*grads, small)


def _pair_sum_list(c_arr, grads, recvs, small, rsmall):
    na = len(grads)
    def body(c_ref, *refs):
        g_refs, r_refs, s_ref, rs_ref = refs[:na], refs[na:2 * na], refs[2 * na], refs[2 * na + 1]
        o_refs, os_ref = refs[2 * na + 2:3 * na + 2], refs[3 * na + 2]
        for g_ref, r_ref, o_ref in zip(g_refs, r_refs, o_refs):
            o_ref[...] = (g_ref[...].astype(f32) + r_ref[...].astype(f32)).astype(bf16)
        os_ref[...] = s_ref[...] + rs_ref[...]
    half = lambda g: pl.BlockSpec((1, g.shape[1] // 2, g.shape[2]), lambda j, c: (j, c[0], 0))
    low = lambda g: pl.BlockSpec((1, g.shape[1] // 2, g.shape[2]), lambda j, c: (j, 0, 0))
    sm = pl.BlockSpec((SMALL_ROWS, 128), lambda j, c: (0, 0))
    grid_spec = pltpu.PrefetchScalarGridSpec(
        num_scalar_prefetch=1, grid=(NCHIP,),
        in_specs=[half(g) for g in grads] + [low(g) for g in grads] + [sm, sm],
        out_specs=[low(g) for g in grads] + [sm])
    return pl.pallas_call(
        body, grid_spec=grid_spec,
        out_shape=[jax.ShapeDtypeStruct((NCHIP, g.shape[1] // 2, g.shape[2]), bf16) for g in grads]
        + [jax.ShapeDtypeStruct((SMALL_ROWS, 128), f32)],
        name="pair_sum", compiler_params=_cp("arbitrary"))(c_arr, *grads, *recvs, small, rsmall)


def _chip_sum_list(parts, small):
    na = len(parts)
    nt = 2
    def body(*refs):
        for q_ref, f_ref in zip(refs[:na + 1], refs[na + 1:]):
            acc = q_ref[0].astype(f32)
            for j in range(1, NCHIP):
                acc = acc + q_ref[j].astype(f32)
            f_ref[...] = acc
    arrs = list(parts) + [small]
    return pl.pallas_call(
        body, grid=(nt,),
        in_specs=[pl.BlockSpec((NCHIP, a.shape[1] // nt, a.shape[2]), lambda i: (0, i, 0)) for a in arrs],
        out_specs=[pl.BlockSpec((a.shape[1] // nt, a.shape[2]), lambda i: (i, 0)) for a in arrs],
        out_shape=[jax.ShapeDtypeStruct(a.shape[1:], f32) for a in arrs],
        name="chip_sum", compiler_params=_cp("arbitrary"))(*arrs)


_SMALL =(("conv_b", (1, 1024)), ("conv_ln_gain", (1, 1024)), ("conv_ln_bias", (1, 1024)),
          ("ssm_lambda_re", (1, 32, 64)), ("ssm_lambda_im", (1, 32, 64)), ("ssm_log_dt", (1, 32)),
          ("ssm_b_re", (1, 32, 64, 16)), ("ssm_b_im", (1, 32, 64, 16)), ("ssm_c_re", (1, 32, 16, 64)),
          ("ssm_c_im", (1, 32, 16, 64)), ("ssm_d", (1, 32, 16)), ("b_ssm_glu", (1, 512)), ("post_norm_gain", (1, 1024)))


def _pack_small(vals, extra=None):
    rows = []
    for v in list(vals) + ([extra] if extra is not None else []):
        flat = v.reshape(-1).astype(f32)
        n = -(-flat.shape[0] // 1024) * 1024
        rows.append(jnp.pad(flat, (0, n - flat.shape[0])).reshape(-1, 128))
    used = sum(r.shape[0] for r in rows)
    rows.append(jnp.zeros((SMALL_ROWS - used, 128), f32))
    return jnp.concatenate(rows, axis=0)


def _unpack_small(p):
    o = 0
    out = []
    for _, shape in _SMALL:
        n = int(np.prod(shape))
        nr = -(-n // 1024) * 8
        out.append(p[o:o + nr].reshape(-1)[:n].reshape(shape))
        o += nr
    return out, p[o, 0]


def _discretize(lam_re, lam_im, log_dt, b_re, b_im):
    dt = jnp.exp(log_dt)[:, None]
    mag = jnp.exp(lam_re * dt)
    ar = mag * jnp.cos(lam_im * dt)
    ai = mag * jnp.sin(lam_im * dt)
    den = lam_re * lam_re + lam_im * lam_im
    zr = ((ar - 1.0) * lam_re + ai * lam_im) / den
    zi = (ai * lam_re - (ar - 1.0) * lam_im) / den
    bbr = zr[..., None] * b_re - zi[..., None] * b_im
    bbi = zr[..., None] * b_im + zi[..., None] * b_re
    return ar, ai, bbr, bbi


_EYE8 = np.eye(8, dtype=np.float32)


def _bbt_blocks(bb):
    v = bb.reshape(4, 8, PST, H).transpose(0, 1, 3, 2)
    return jnp.einsum("bghp,gk->bghkp", v, _EYE8).reshape(4, 128, 512)


def _bbt_unblock(m):
    v = jnp.einsum("bghkp,gk->bghp", m.reshape(4, 8, H, 8, PST), _EYE8)
    return v.transpose(0, 1, 3, 2).reshape(G, PST, H)


def _ct_blocks(cc):
    v = cc.reshape(4, 8, H, PST)
    return jnp.einsum("bghp,gk->bgpkh", v, _EYE8).reshape(4, 512, 128)


def _ct_unblock(m):
    return jnp.einsum("bghkp,gk->bghp", m.reshape(4, 8, H, 8, PST), _EYE8).reshape(G, H, PST)


def _perm_matrix():
    p = np.zeros((TC, TC), np.float32)
    for r in range(R):
        for seg in range(8):
            p[r * 8 + seg, seg * R + r] = 1.0
    return p


def _deinterleave(a):
    L, C = a.shape
    return a.reshape(L // TC, R, 8, C).transpose(0, 2, 1, 3).reshape(L, C)


def _fwd_bwd(h, xi, ti, proj, conv_w, w_co, w_glu, w_so, w_out, small):
    (conv_b, ln_g, ln_b, lam_re, lam_im, log_dt, b_re, b_im, c_re, c_im, dvec, b_glu, g_post) = small
    lam_re, lam_im, log_dt = lam_re[0], lam_im[0], log_dt[0]
    b_re, b_im, c_re, c_im = b_re[0], b_im[0], c_re[0], c_im[0]
    (ar, ai, bbr, bbi), disc_vjp = jax.vjp(_discretize, lam_re, lam_im, log_dt, b_re, b_im)
    a_re = ar.reshape(1, NS)
    a_im = ai.reshape(1, NS)
    dt = jnp.exp(log_dt)[:, None]
    steps = jnp.arange(1, R + 1, dtype=f32)[:, None, None]
    apow_re = (jnp.exp(steps * (lam_re * dt)) * jnp.cos(steps * (lam_im * dt))).reshape(R, NS)
    apow_im = (jnp.exp(steps * (lam_re * dt)) * jnp.sin(steps * (lam_im * dt))).reshape(R, NS)
    bbt_re, bbt_im = _bbt_blocks(bbr).astype(bf16), _bbt_blocks(bbi).astype(bf16)
    ct_re, ct_im = _ct_blocks(c_re).astype(bf16), _ct_blocks(c_im).astype(bf16)
    d_row = dvec.reshape(1, SW)
    cw32 = jnp.pad(conv_w, ((0, 1), (0, 0)))

    cu1, a_in = _conv_fwd(proj, cw32, conv_b, ln_g, ln_b)
    y0, b_in, sre, sim, cinr, cini = _ssm_fwd(proj, bbt_re, bbt_im, ct_re, ct_im, a_re, a_im,
                                              apow_re, apow_im, d_row, w_glu, b_glu)
    gx0, d_ain, d_bin, dproj, dw_out, dw_co, dw_so, dg_post, loss = _tail(
        a_in, b_in, proj, xi, ti, w_co, w_so, w_out, g_post)
    (dproj, dbbt_re, dbbt_im, dct_re, dct_im, dd, dar8, dai8, dw_glu, db_glu) = _ssm_bwd(
        d_bin, y0, proj, sre, sim, cinr, cini, bbt_re, bbt_im, ct_re, ct_im,
        a_re, a_im, apow_re, apow_im, d_row, w_glu, b_glu, dproj)
    dproj, dcw8, d_convb, d_lng, d_lnb = _conv_bwd(d_ain, cu1, proj, cw32, ln_g, ln_b, dproj)
    dw_in = _win_grad(h, dproj)

    d_ar = jnp.sum(dar8, axis=0).reshape(G, PST)
    d_ai = jnp.sum(dai8, axis=0).reshape(G, PST)
    d_lre, d_lim, d_ldt, d_bre, d_bim = disc_vjp((d_ar, d_ai, _bbt_unblock(dbbt_re), _bbt_unblock(dbbt_im)))
    d_conv_w = jnp.sum(dcw8, axis=1)[:KS]
    small_grads = [d_convb, d_lng, d_lnb, d_lre[None], d_lim[None], d_ldt[None], d_bre[None], d_bim[None],
                   _ct_unblock(dct_re)[None], _ct_unblock(dct_im)[None], dd.reshape(1, G, H), db_glu, dg_post]
    return loss[0, 0], gx0, dproj, (dw_in, dw_co, dw_out, dw_glu, dw_so, d_conv_w), small_grads


def kernel(x, pre_norm_gain, w_in, conv_w, conv_b, conv_ln_gain, conv_ln_bias, w_conv_out, ssm_lambda_re, ssm_lambda_im, ssm_log_dt, ssm_b_re, ssm_b_im, ssm_c_re, ssm_c_im, ssm_d, w_ssm_glu, b_ssm_glu, w_ssm_out, w_out, post_norm_gain, loss_target, m_pre_norm_gain, m_w_in, m_conv_w, m_conv_b, m_conv_ln_gain, m_conv_ln_bias, m_w_conv_out, m_ssm_lambda_re, m_ssm_lambda_im, m_ssm_log_dt, m_ssm_b_re, m_ssm_b_im, m_ssm_c_re, m_ssm_c_im, m_ssm_d, m_w_ssm_glu, m_b_ssm_glu, m_w_ssm_out, m_w_out, m_post_norm_gain, v_pre_norm_gain, v_w_in, v_conv_w, v_conv_b, v_conv_ln_gain, v_conv_ln_bias, v_w_conv_out, v_ssm_lambda_re, v_ssm_lambda_im, v_ssm_log_dt, v_ssm_b_re, v_ssm_b_im, v_ssm_c_re, v_ssm_c_im, v_ssm_d, v_w_ssm_glu, v_b_ssm_glu, v_w_ssm_out, v_w_out, v_post_norm_gain):
    c = lax.axis_index("c")
    shards = [w_in[0].astype(bf16), w_conv_out[0].astype(bf16), w_out[0].astype(bf16), w_ssm_glu[0].astype(bf16),
              w_ssm_out[0].astype(bf16), jnp.pad(conv_w[0], ((0, CONV_ROWS - KS), (0, 0)))]
    k_arr = (2 * lax.axis_index("x") + lax.axis_index("y")).astype(jnp.int32).reshape(1)
    w_in_g, w_co_g, w_out_g, w_glu_g, w_so_g, conv_w_g, h, xi, ti, proj = _gather_prep(
        k_arr, shards, x[0], loss_target[0], pre_norm_gain, jnp.asarray(_perm_matrix(), bf16))
    conv_w_f = conv_w_g[:, :KS].transpose(1, 0, 2).reshape(KS, CW)

    small = (conv_b, conv_ln_gain, conv_ln_bias, ssm_lambda_re, ssm_lambda_im, ssm_log_dt, ssm_b_re,
             ssm_b_im, ssm_c_re, ssm_c_im, ssm_d, b_ssm_glu, post_norm_gain)
    loss_part, gx0, dproj, big_grads, small_grads = _fwd_bwd(
        h, xi, ti, _proj_fwd(k_arr, h, w_in_g, proj), conv_w_f, w_co_g.reshape(CW, D), w_glu_g.reshape(SW, SW), w_so_g,
        w_out_g.reshape(D, D), small)

    dw_in, dw_co, dw_out, dw_glu, dw_so, d_conv_w = big_grads
    d_conv_w = jnp.pad(d_conv_w, ((0, CONV_ROWS - KS), (0, 0))).reshape(CONV_ROWS, NCHIP, 256).transpose(1, 0, 2)
    grads = [dw_in] + [g.astype(bf16) for g in (dw_co.reshape(NCHIP, 256, D), dw_out.reshape(NCHIP, 256, D),
                                                  dw_glu.reshape(NCHIP, 128, SW), dw_so, d_conv_w)]
    gs = _pack_small(small_grads, extra=loss_part)
    *recvs, rs = _pair_exchange_list(grads, gs)
    *parts, ps = _pair_sum_list(c.astype(jnp.int32).reshape(1), grads, recvs, gs, rs)
    gxi, dg_pre, *qparts, qs = _x_grad_exchange(dproj, w_in_g, xi, gx0, pre_norm_gain, parts, ps)
    grad_x = _deinterleave(gxi)
    *halves, fs = _chip_sum_list(qparts, qs)
    g_big = list(_sibling_join_list(halves))
    g_big[5] = g_big[5][:KS]

    big_w = (w_in[0], w_conv_out[0], w_out[0], w_ssm_glu[0], w_ssm_out[0], conv_w[0])
    big_m = (m_w_in[0], m_w_conv_out[0], m_w_out[0], m_w_ssm_glu[0], m_w_ssm_out[0], m_conv_w[0])
    big_v = (v_w_in[0], v_w_conv_out[0], v_w_out[0], v_w_ssm_glu[0], v_w_ssm_out[0], v_conv_w[0])
    big_names = ("w_in", "w_conv_out", "w_out", "w_ssm_glu", "w_ssm_out", "conv_w")
    res = {}
    upd = [_adamw("adamw_w_in", big_w[0], g_big[0], big_m[0], big_v[0])]
    upd += _adamw_group("adamw_rest", big_w[1:], g_big[1:], big_m[1:], big_v[1:])
    for n, g, (d, m2, v2) in zip(big_names, g_big, upd):
        res[n] = (g[None], d[None], m2[None], v2[None])

    small_m = (m_conv_b, m_conv_ln_gain, m_conv_ln_bias, m_ssm_lambda_re, m_ssm_lambda_im, m_ssm_log_dt,
               m_ssm_b_re, m_ssm_b_im, m_ssm_c_re, m_ssm_c_im, m_ssm_d, m_b_ssm_glu, m_post_norm_gain)
    small_v = (v_conv_b, v_conv_ln_gain, v_conv_ln_bias, v_ssm_lambda_re, v_ssm_lambda_im, v_ssm_log_dt,
               v_ssm_b_re, v_ssm_b_im, v_ssm_c_re, v_ssm_c_im, v_ssm_d, v_b_ssm_glu, v_post_norm_gain)
    sd, sm, sv = _adamw("adamw_small", _pack_small(small), fs, _pack_small(small_m), _pack_small(small_v))
    sg_l, loss = _unpack_small(fs)
    sd_l, _ = _unpack_small(sd)
    sm_l, _ = _unpack_small(sm)
    sv_l, _ = _unpack_small(sv)
    for i, (n, _) in enumerate(_SMALL):
        res[n] = (sg_l[i], sd_l[i], sm_l[i], sv_l[i])
    rows = lambda a: a.reshape(8, 128)
    pre = _adamw_rows(_allgather_rows(rows(dg_pre)), rows(pre_norm_gain), rows(m_pre_norm_gain), rows(v_pre_norm_gain))
    res["pre_norm_gain"] = tuple(a.reshape(1, D) for a in pre)

    order = ("pre_norm_gain", "w_in", "conv_w", "conv_b", "conv_ln_gain", "conv_ln_bias", "w_conv_out", "ssm_lambda_re",
             "ssm_lambda_im", "ssm_log_dt", "ssm_b_re", "ssm_b_im", "ssm_c_re", "ssm_c_im", "ssm_d", "w_ssm_glu",
             "b_ssm_glu", "w_ssm_out", "w_out", "post_norm_gain")
    outs = [loss, grad_x[None]]
    for q in range(4):
        outs.extend(res[n][q] for n in order)
    return tuple(outs)
```

```python
import math

import numpy as np
import jax
import jax.numpy as jnp
from jax import lax
from jax.experimental import pallas as pl
from jax.experimental.pallas import tpu as pltpu

f32 = jnp.float32
bf16 = jnp.bfloat16

D = 1024
CW = 1024
SW = 512
G = 32
H = 16
PST = 64
NS = G * PST
KS = 31
IN_W = 6144
NCHIP = 4
SHARD_W = IN_W // NCHIP
RMS_EPS = 1e-6
LN_EPS = 1e-5
LR, B1, B2, EPS, WD, STEP = 0.001, 0.9, 0.999, 1e-08, 0.01, 10
GELU_K0 = math.sqrt(2.0 / math.pi)
GELU_K1 = 0.044715

TC = 512
R = TC // 8
NH = 32
LBW = 1024
CONV_ROWS = 64
SMALL_ROWS = 1152
VMEM_LIMIT = 56 * 1024 * 1024
MESH = pl.DeviceIdType.MESH


def _cp(*sem):
    return pltpu.CompilerParams(dimension_semantics=tuple(sem), vmem_limit_bytes=VMEM_LIMIT)


def _sig(v):
    return 0.5 * jnp.tanh(0.5 * v) + 0.5


def _dot(a, b):
    return jnp.dot(a, b, preferred_element_type=f32)


def _dot_nt(a, b):
    return lax.dot_general(a, b, (((1,), (1,)), ((), ())), preferred_element_type=f32)


def _dot_tn(a, b):
    return lax.dot_general(a, b, (((0,), (0,)), ((), ())), preferred_element_type=f32)


def _full(shape):
    nd = len(shape)
    return pl.BlockSpec(shape, lambda *_: (0,) * nd)


def _rows8(i):
    return pl.ds(pl.multiple_of(i * 8, 8), 8)


def _proj_fwd(k_arr, h, w_in, proj):
    L = h.shape[0]
    tm = min(1024, L)
    def body(_, h_ref, w_ref, __, o_ref):
        o_ref[...] = _dot(h_ref[...], w_ref[0]).astype(bf16)
    shard = lambda j, k: (k[0] + 1 + j) % NCHIP
    grid_spec = pltpu.PrefetchScalarGridSpec(
        num_scalar_prefetch=1, grid=(NCHIP - 1, L // tm),
        in_specs=[pl.BlockSpec((tm, D), lambda j, i, k: (i, 0)),
                  pl.BlockSpec((1, D, SHARD_W), lambda j, i, k: (shard(j, k), 0, 0)), _ANY],
        out_specs=pl.BlockSpec((tm, SHARD_W), lambda j, i, k: (i, shard(j, k))))
    return pl.pallas_call(
        body, grid_spec=grid_spec, out_shape=jax.ShapeDtypeStruct((L, IN_W), bf16),
        input_output_aliases={3: 0},
        name="proj_fwd", compiler_params=_cp("arbitrary", "arbitrary"))(k_arr, h, w_in, proj)


NLB = CW // 128
RPI = 8


def _put_blocked(buf, row0, nrows, v):
    for lb in range(NLB):
        buf[lb, pl.ds(row0, nrows), :] = v[:, lb * 128:(lb + 1) * 128]


def _get_blocked(buf, row0, nrows):
    return jnp.concatenate([buf[lb, pl.ds(row0, nrows), :] for lb in range(NLB)], axis=1)


def _fill_before(ebuf, prev):
    sub = lax.broadcasted_iota(jnp.int32, (8, 128), 0)
    def halo(p, carry):
        for lb in range(NLB):
            cur = ebuf[lb, _rows8(R + p), :]
            ebuf[lb, _rows8(p), :] = jnp.where(sub == 0, pltpu.roll(prev[lb, _rows8(p), :], 1, 0),
                                               pltpu.roll(cur, 1, 0))
        return carry
    lax.fori_loop(0, NH, halo, 0)


def _fir(buf, lb, r, coef, first, flip):
    win = buf[lb, pl.ds(pl.multiple_of(r * 8, 8), (KS + RPI - 1) * 8), :]
    outs = []
    for i in range(RPI):
        acc = [first, None, None, None]
        for k in range(KS):
            o = i + ((KS - 1 - k) if flip else k)
            t = coef[k] * win[8 * o:8 * o + 8, :]
            acc[k % 4] = t if acc[k % 4] is None else acc[k % 4] + t
        outs.append((acc[0] + acc[1]) + (acc[2] + acc[3]))
    return outs


def _conv_fwd(proj, cw, cbias, lng, lnb):
    L = proj.shape[0]
    nc = L // TC
    def body(ca_ref, cb_ref, zc_ref, w_ref, b_ref, g_ref, bb_ref, cu1_ref, ain_ref, ebuf, prev, cacc):
        @pl.when(pl.program_id(0) == 0)
        def _():
            prev[...] = jnp.zeros_like(prev)
        def glu(s, carry):
            rows = pl.ds(pl.multiple_of(s * 64, 64), 64)
            _put_blocked(ebuf, pl.multiple_of(NH * 8 + s * 64, 64), 64,
                         ca_ref[rows, :].astype(f32) * _sig(cb_ref[rows, :].astype(f32)))
            return carry
        lax.fori_loop(0, TC // 64, glu, 0)
        _fill_before(ebuf, prev)
        prev[...] = ebuf[:, R * 8:(NH + R) * 8, :]
        for lb in range(NLB):
            sl = slice(lb * 128, (lb + 1) * 128)
            wk = [jnp.broadcast_to(w_ref[k:k + 1, sl], (8, 128)) for k in range(KS)]
            bias = jnp.broadcast_to(b_ref[:, sl], (8, 128))
            def tap(q, carry, lb=lb, wk=wk, bias=bias):
                r = q * RPI
                for i, o in enumerate(_fir(ebuf, lb, r + (NH - KS + 1), wk, bias, False)):
                    cacc[lb, _rows8(r + i), :] = o
                return carry
            lax.fori_loop(0, R // RPI, tap, 0)
        def norm(s, carry):
            rows = pl.ds(pl.multiple_of(s * 64, 64), 64)
            c1b = _get_blocked(cacc, pl.multiple_of(s * 64, 64), 64).astype(bf16)
            cu1_ref[rows, :] = c1b
            c1 = c1b.astype(f32)
            xc = c1 - jnp.mean(c1, axis=-1, keepdims=True)
            var = jnp.mean(xc * xc, axis=-1, keepdims=True)
            ln = xc * lax.rsqrt(var + LN_EPS) * g_ref[...] + bb_ref[...]
            zc = zc_ref[rows, :].astype(f32)
            ain_ref[rows, :] = ((ln * _sig(ln)) * (zc * _sig(zc))).astype(bf16)
            return carry
        lax.fori_loop(0, TC // 64, norm, 0, unroll=4)

    col = lambda c: pl.BlockSpec((TC, CW), lambda i, c=c: (i, c))
    return pl.pallas_call(
        body, grid=(nc,),
        in_specs=[col(0), col(1), col(2), _full((32, CW)), _full((1, CW)), _full((1, CW)), _full((1, CW))],
        out_specs=[pl.BlockSpec((TC, CW), lambda i: (i, 0)), pl.BlockSpec((TC, CW), lambda i: (i, 0))],
        out_shape=[jax.ShapeDtypeStruct((L, CW), bf16), jax.ShapeDtypeStruct((L, CW), bf16)],
        scratch_shapes=[pltpu.VMEM((NLB, (NH + R) * 8, 128), f32), pltpu.VMEM((NLB, NH * 8, 128), f32),
                        pltpu.VMEM((NLB, TC, 128), f32)],
        name="conv_fwd", compiler_params=_cp("arbitrary"))(proj, proj, proj, cw, cbias, lng, lnb)


def _gelu_parts(y0):
    t = jnp.tanh(GELU_K0 * (y0 + GELU_K1 * y0 * y0 * y0))
    return t, 0.5 * y0 * (1.0 + t)


def _ssm_fwd(proj, bbt_re, bbt_im, ct_re, ct_im, a_re, a_im, apow_re, apow_im, dvec, wglu, bglu):
    L = proj.shape[0]
    nc = L // TC
    def body(u_ref, zs_ref, bre_ref, bim_ref, cre_ref, cim_ref, are_ref, aim_ref, pwr_ref, pwi_ref,
             d_ref, wg_ref, bg_ref, y0_ref, bin_ref, sre, sim, cinr, cini, prev_re, prev_im):
        c = pl.program_id(0)
        @pl.when(c == 0)
        def _():
            prev_re[...] = jnp.zeros_like(prev_re)
            prev_im[...] = jnp.zeros_like(prev_im)
        u = u_ref[...]
        for blk in range(4):
            ub = u[:, 128 * blk:128 * (blk + 1)]
            sre[:, 512 * blk:512 * (blk + 1)] = _dot(ub, bre_ref[blk])
            sim[:, 512 * blk:512 * (blk + 1)] = _dot(ub, bim_ref[blk])
        for lb in range(NS // LBW):
            sl = slice(lb * LBW, (lb + 1) * LBW)
            ar = jnp.broadcast_to(are_ref[:, sl], (8, LBW))
            ai = jnp.broadcast_to(aim_ref[:, sl], (8, LBW))
            def step(r, carry, sl=sl, ar=ar, ai=ai):
                sr, si = carry
                nr = ar * sr - ai * si + sre[_rows8(r), sl]
                ni = ar * si + ai * sr + sim[_rows8(r), sl]
                sre[_rows8(r), sl] = nr
                sim[_rows8(r), sl] = ni
                return nr, ni
            lax.fori_loop(1, R, step, (sre[0:8, sl], sim[0:8, sl]))
        a_r = pwr_ref[R - 1:R, :]
        a_i = pwi_ref[R - 1:R, :]
        cr = prev_re[0:1, :]
        ci = prev_im[0:1, :]
        for seg in range(8):
            cinr[seg:seg + 1, :] = cr
            cini[seg:seg + 1, :] = ci
            er = sre[8 * (R - 1) + seg:8 * (R - 1) + seg + 1, :]
            ei = sim[8 * (R - 1) + seg:8 * (R - 1) + seg + 1, :]
            cr, ci = er + a_r * cr - a_i * ci, ei + a_r * ci + a_i * cr
        prev_re[0:1, :] = cr
        prev_im[0:1, :] = ci
        for lb in range(NS // LBW):
            sl = slice(lb * LBW, (lb + 1) * LBW)
            kr = cinr[:, sl]
            ki = cini[:, sl]
            def fix(r, carry, sl=sl, kr=kr, ki=ki):
                pr = jnp.broadcast_to(pwr_ref[pl.ds(r, 1), sl], (8, LBW))
                pi = jnp.broadcast_to(pwi_ref[pl.ds(r, 1), sl], (8, LBW))
                sre[_rows8(r), sl] = sre[_rows8(r), sl] + pr * kr - pi * ki
                sim[_rows8(r), sl] = sim[_rows8(r), sl] + pr * ki + pi * kr
                return carry
            lax.fori_loop(0, R, fix, 0, unroll=2)
        yp = []
        for blk in range(4):
            sr = sre[:, 512 * blk:512 * (blk + 1)].astype(bf16)
            si = sim[:, 512 * blk:512 * (blk + 1)].astype(bf16)
            yp.append(_dot(sr, cre_ref[blk]) - _dot(si, cim_ref[blk]))
        y0 = jnp.concatenate(yp, axis=1) + d_ref[...] * u.astype(f32)
        _, y1 = _gelu_parts(y0)
        glu = _dot(y1.astype(bf16), wg_ref[...]) + bg_ref[...]
        y0_ref[:, 0:SW] = y0
        y0_ref[:, SW:2 * SW] = glu
        y2 = y1 * _sig(glu)
        zs = zs_ref[...].astype(f32)
        bin_ref[...] = (y2 * (zs * _sig(zs))).astype(bf16)

    return pl.pallas_call(
        body, grid=(nc,),
        in_specs=[pl.BlockSpec((TC, SW), lambda c: (c, 6)), pl.BlockSpec((TC, SW), lambda c: (c, 7)),
                  _full((4, 128, 512)), _full((4, 128, 512)), _full((4, 512, 128)), _full((4, 512, 128)),
                  _full((1, NS)), _full((1, NS)), _full((R, NS)), _full((R, NS)),
                  _full((1, SW)), _full((SW, SW)), _full((1, SW))],
        out_specs=[pl.BlockSpec((TC, 2 * SW), lambda c: (c, 0)), pl.BlockSpec((TC, SW), lambda c: (c, 0)),
                   pl.BlockSpec((TC, NS), lambda c: (c, 0)), pl.BlockSpec((TC, NS), lambda c: (c, 0)),
                   pl.BlockSpec((8, NS), lambda c: (c, 0)), pl.BlockSpec((8, NS), lambda c: (c, 0))],
        out_shape=[jax.ShapeDtypeStruct((L, 2 * SW), f32), jax.ShapeDtypeStruct((L, SW), bf16),
                   jax.ShapeDtypeStruct((L, NS), f32), jax.ShapeDtypeStruct((L, NS), f32),
                   jax.ShapeDtypeStruct((nc * 8, NS), f32), jax.ShapeDtypeStruct((nc * 8, NS), f32)],
        scratch_shapes=[pltpu.VMEM((8, NS), f32), pltpu.VMEM((8, NS), f32)],
        name="ssm_fwd", compiler_params=_cp("arbitrary"))(
            proj, proj, bbt_re, bbt_im, ct_re, ct_im, a_re, a_im, apow_re, apow_im, dvec, wglu, bglu)


def _tail(a_in, b_in, proj, x, tgt, wco, wso, wout, gpost):
    L = x.shape[0]
    tm = 512
    def body(a_ref, b_ref, gc_ref, gs_ref, x_ref, t_ref, wco_ref, wso_ref, wout_ref, gp_ref,
             gx_ref, dain_ref, dbin_ref, dp_ref, dwout_ref, dwco_ref, dwso_ref, dgp_ref, loss_ref):
        @pl.when(pl.program_id(0) == 0)
        def _():
            dwout_ref[...] = jnp.zeros_like(dwout_ref)
            dwco_ref[...] = jnp.zeros_like(dwco_ref)
            dwso_ref[...] = jnp.zeros_like(dwso_ref)
            dgp_ref[...] = jnp.zeros_like(dgp_ref)
            loss_ref[...] = jnp.zeros_like(loss_ref)
        a = a_ref[...]
        b = b_ref[...]
        co = _dot(a, wco_ref[...])
        so = jnp.concatenate([_dot(b, wso_ref[j]) for j in range(NCHIP)], axis=1)
        sc = _sig(gc_ref[...].astype(f32))
        ss = _sig(gs_ref[...].astype(f32))
        mb = (sc * co + ss * so).astype(bf16)
        out = _dot(mb, wout_ref[...])
        r2 = lax.rsqrt(jnp.mean(out * out, axis=-1, keepdims=True) + RMS_EPS)
        on = out * r2
        gp = gp_ref[...]
        e = x_ref[...] + on * gp - t_ref[...]
        loss_ref[...] += (0.5 / D) * jnp.sum(e * e)
        dy = e * (1.0 / D)
        gx_ref[...] = dy
        dgp_ref[...] += jnp.sum(dy * on, axis=0, keepdims=True)
        dn = dy * gp
        dout = (r2 * (dn - on * jnp.mean(dn * on, axis=-1, keepdims=True))).astype(bf16)
        dwout_ref[...] += _dot_tn(mb, dout)
        dm = _dot_nt(dout, wout_ref[...])
        dp_ref[:, 0:D] = (dm * co * sc * (1.0 - sc)).astype(bf16)
        dp_ref[:, D:2 * D] = (dm * so * ss * (1.0 - ss)).astype(bf16)
        dco = (dm * sc).astype(bf16)
        dso = (dm * ss).astype(bf16)
        dwco_ref[...] += _dot_tn(a, dco)
        dbin = None
        for j in range(NCHIP):
            dso_j = dso[:, j * 256:(j + 1) * 256]
            dwso_ref[j] += _dot_tn(b, dso_j)
            t = _dot_nt(dso_j, wso_ref[j])
            dbin = t if dbin is None else dbin + t
        dain_ref[...] = _dot_nt(dco, wco_ref[...]).astype(bf16)
        dbin_ref[...] = dbin.astype(bf16)

    row = lambda w: pl.BlockSpec((tm, w), lambda i: (i, 0))
    one = lambda shape: pl.BlockSpec(shape, lambda i: (0,) * len(shape), pipeline_mode=pl.Buffered(1))
    return pl.pallas_call(
        body, grid=(L // tm,),
        in_specs=[row(CW), row(SW), pl.BlockSpec((tm, D), lambda i: (i, 4)), pl.BlockSpec((tm, D), lambda i: (i, 5)),
                  row(D), row(D), one((CW, D)), one((NCHIP, SW, 256)), one((D, D)), one((1, D))],
        out_specs=[row(D), row(CW), row(SW), pl.BlockSpec((tm, 2 * D), lambda i: (i, 2)),
                   one((D, D)), one((CW, D)), one((NCHIP, SW, 256)), one((1, D)), one((1, 128))],
        out_shape=[jax.ShapeDtypeStruct((L, D), f32), jax.ShapeDtypeStruct((L, CW), bf16),
                   jax.ShapeDtypeStruct((L, SW), bf16), jax.ShapeDtypeStruct((L, IN_W), bf16),
                   jax.ShapeDtypeStruct((D, D), f32), jax.ShapeDtypeStruct((CW, D), f32),
                   jax.ShapeDtypeStruct((NCHIP, SW, 256), f32), jax.ShapeDtypeStruct((1, D), f32),
                   jax.ShapeDtypeStruct((1, 128), f32)],
        name="tail", compiler_params=_cp("arbitrary"))(a_in, b_in, proj, proj, x, tgt, wco, wso, wout, gpost)


def _ssm_bwd(d_bin, y0, proj, sre, sim, cinr, cini, bbt_re, bbt_im, ct_re, ct_im,
             a_re, a_im, apow_re, apow_im, dvec, wglu, bglu, dproj):
    L = y0.shape[0]
    nc = L // TC
    def body(dbin_ref, y0_ref, u_ref, zs_ref, sre_ref, sim_ref, cinr_ref, cini_ref,
             bre_ref, bim_ref, cre_ref, cim_ref, are_ref, aim_ref, pwr_ref, pwi_ref, d_ref, wg_ref, bg_ref, _,
             dp_ref, dbre_ref, dbim_ref, dcre_ref, dcim_ref, dd_ref, dar_ref, dai_ref, dwg_ref, dbg_ref,
             gre, gim, gcr, gci, nxt_re, nxt_im):
        @pl.when(pl.program_id(0) == 0)
        def _():
            for ref in (dbre_ref, dbim_ref, dcre_ref, dcim_ref, dd_ref, dar_ref, dai_ref, dwg_ref, dbg_ref,
                        nxt_re, nxt_im):
                ref[...] = jnp.zeros_like(ref)
        y0 = y0_ref[:, 0:SW]
        u = u_ref[...]
        zs = zs_ref[...].astype(f32)
        dbin = dbin_ref[...].astype(f32)
        t, y1 = _gelu_parts(y0)
        y1b = y1.astype(bf16)
        sg = _sig(y0_ref[:, SW:2 * SW])
        sz = _sig(zs)
        d_y2 = dbin * (zs * sz)
        dp_ref[:, SW:2 * SW] = (dbin * (y1 * sg) * (sz * (1.0 + zs * (1.0 - sz)))).astype(bf16)
        d_glu = d_y2 * y1 * sg * (1.0 - sg)
        d_glub = d_glu.astype(bf16)
        d_y1 = d_y2 * sg + _dot_nt(d_glub, wg_ref[...])
        dwg_ref[...] += _dot_tn(y1b, d_glub)
        dbg_ref[...] += jnp.sum(d_glu, axis=0, keepdims=True)
        dgelu = 0.5 * (1.0 + t) + 0.5 * y0 * (1.0 - t * t) * GELU_K0 * (1.0 + 3.0 * GELU_K1 * y0 * y0)
        d_y0 = d_y1 * dgelu
        dd_ref[...] += jnp.sum(d_y0 * u.astype(f32), axis=0, keepdims=True)
        dyb = d_y0.astype(bf16)
        for blk in range(4):
            dy1 = dyb[:, 128 * blk:128 * (blk + 1)]
            gre[:, 512 * blk:512 * (blk + 1)] = _dot_nt(dy1, cre_ref[blk])
            gim[:, 512 * blk:512 * (blk + 1)] = -_dot_nt(dy1, cim_ref[blk])
        for lb in range(NS // LBW):
            sl = slice(lb * LBW, (lb + 1) * LBW)
            ar = jnp.broadcast_to(are_ref[:, sl], (8, LBW))
            ai = jnp.broadcast_to(aim_ref[:, sl], (8, LBW))
            def step(k, carry, sl=sl, ar=ar, ai=ai):
                gr, gi = carry
                row = _rows8(R - 2 - k)
                nr = ar * gr + ai * gi + gre[row, sl]
                ni = ar * gi - ai * gr + gim[row, sl]
                gre[row, sl] = nr
                gim[row, sl] = ni
                return nr, ni
            lax.fori_loop(0, R - 1, step, (gre[8 * (R - 1):8 * R, sl], gim[8 * (R - 1):8 * R, sl]))
        a_r = pwr_ref[R - 1:R, :]
        a_i = pwi_ref[R - 1:R, :]
        cr = nxt_re[0:1, :]
        ci = nxt_im[0:1, :]
        for seg in range(7, -1, -1):
            gcr[seg:seg + 1, :] = cr
            gci[seg:seg + 1, :] = ci
            er = gre[seg:seg + 1, :]
            ei = gim[seg:seg + 1, :]
            cr, ci = er + a_r * cr + a_i * ci, ei + a_r * ci - a_i * cr
        nxt_re[0:1, :] = cr
        nxt_im[0:1, :] = ci
        for lb in range(NS // LBW):
            sl = slice(lb * LBW, (lb + 1) * LBW)
            kr = gcr[:, sl]
            ki = gci[:, sl]
            def fixed(rows, prow, sl=sl, kr=kr, ki=ki):
                pr = jnp.broadcast_to(pwr_ref[prow, sl], (8, LBW))
                pi = jnp.broadcast_to(pwi_ref[prow, sl], (8, LBW))
                gr = gre[rows, sl] + pr * kr + pi * ki
                gi = gim[rows, sl] + pr * ki - pi * kr
                gre[rows, sl] = gr
                gim[rows, sl] = gi
                return gr, gi
            g0r, g0i = fixed(slice(0, 8), slice(R - 1, R))
            p0r, p0i = cinr_ref[:, sl], cini_ref[:, sl]
            acc0 = (g0r * p0r + g0i * p0i, g0i * p0r - g0r * p0i)
            def dacc(r, carry, sl=sl, fixed=fixed):
                xr, xi = carry
                gr, gi = fixed(_rows8(r), pl.ds(R - 1 - r, 1))
                pr, pi = sre_ref[_rows8(r - 1), sl], sim_ref[_rows8(r - 1), sl]
                return xr + gr * pr + gi * pi, xi + gi * pr - gr * pi
            xr, xi = lax.fori_loop(1, R, dacc, acc0)
            dar_ref[:, sl] += xr
            dai_ref[:, sl] += xi
        dup = []
        for blk in range(4):
            s4 = slice(512 * blk, 512 * (blk + 1))
            s1 = slice(128 * blk, 128 * (blk + 1))
            grb = gre[:, s4].astype(bf16)
            gib = gim[:, s4].astype(bf16)
            dup.append(_dot_nt(grb, bre_ref[blk]) + _dot_nt(gib, bim_ref[blk]))
            dbre_ref[blk] += _dot_tn(u[:, s1], grb)
            dbim_ref[blk] += _dot_tn(u[:, s1], gib)
            dcre_ref[blk] += _dot_tn(dyb[:, s1], sre_ref[:, s4].astype(bf16))
            dcim_ref[blk] -= _dot_tn(dyb[:, s1], sim_ref[:, s4].astype(bf16))
        dp_ref[:, 0:SW] = (jnp.concatenate(dup, axis=1) + d_ref[...] * d_y0).astype(bf16)

    rev = lambda w, cidx: pl.BlockSpec((TC, w), lambda i, cidx=cidx: (nc - 1 - i, cidx))
    one = lambda shape: pl.BlockSpec(shape, lambda i: (0,) * len(shape))
    return pl.pallas_call(
        body, grid=(nc,),
        in_specs=[rev(SW, 0), rev(2 * SW, 0), rev(SW, 6), rev(SW, 7), rev(NS, 0), rev(NS, 0),
                  pl.BlockSpec((8, NS), lambda i: (nc - 1 - i, 0)), pl.BlockSpec((8, NS), lambda i: (nc - 1 - i, 0)),
                  one((4, 128, 512)), one((4, 128, 512)), one((4, 512, 128)), one((4, 512, 128)),
                  one((1, NS)), one((1, NS)), one((R, NS)), one((R, NS)),
                  one((1, SW)), one((SW, SW)), one((1, SW)), _ANY],
        out_specs=[pl.BlockSpec((TC, 2 * SW), lambda i: (nc - 1 - i, 3)),
                   one((4, 128, 512)), one((4, 128, 512)), one((4, 128, 512)), one((4, 128, 512)),
                   one((1, SW)), one((8, NS)), one((8, NS)), one((SW, SW)), one((1, SW))],
        out_shape=[jax.ShapeDtypeStruct((L, IN_W), bf16),
                   jax.ShapeDtypeStruct((4, 128, 512), f32), jax.ShapeDtypeStruct((4, 128, 512), f32),
                   jax.ShapeDtypeStruct((4, 128, 512), f32), jax.ShapeDtypeStruct((4, 128, 512), f32),
                   jax.ShapeDtypeStruct((1, SW), f32), jax.ShapeDtypeStruct((8, NS), f32),
                   jax.ShapeDtypeStruct((8, NS), f32), jax.ShapeDtypeStruct((SW, SW), f32),
                   jax.ShapeDtypeStruct((1, SW), f32)],
        scratch_shapes=[pltpu.VMEM((TC, NS), f32), pltpu.VMEM((TC, NS), f32), pltpu.VMEM((8, NS), f32),
                        pltpu.VMEM((8, NS), f32), pltpu.VMEM((8, NS), f32), pltpu.VMEM((8, NS), f32)],
        input_output_aliases={19: 0},
        name="ssm_bwd", compiler_params=_cp("arbitrary"))(
            d_bin, y0, proj, proj, sre, sim, cinr, cini, bbt_re, bbt_im, ct_re, ct_im,
            a_re, a_im, apow_re, apow_im, dvec, wglu, bglu, dproj)


def _conv_bwd(d_ain, cu1, proj, cw, lng, lnb, dproj):
    L = cu1.shape[0]
    nc = L // TC
    def body(dain_ref, cu1_ref, ca_ref, cb_ref, zc_ref, cah_ref, cbh_ref, w_ref, g_ref, bb_ref, _,
             dp_ref, dw_ref, dbias_ref, dlng_ref, dlnb_ref, dbuf, ebuf, prev, nxt, dcu0):
        i = pl.program_id(0)
        @pl.when(i == 0)
        def _():
            dw_ref[...] = jnp.zeros_like(dw_ref)
            dbias_ref[...] = jnp.zeros_like(dbias_ref)
            dlng_ref[...] = jnp.zeros_like(dlng_ref)
            dlnb_ref[...] = jnp.zeros_like(dlnb_ref)
            nxt[...] = jnp.zeros_like(nxt)
        def lnb(s, carry):
            rows = pl.ds(pl.multiple_of(s * 32, 32), 32)
            dain = dain_ref[rows, :].astype(f32)
            c1 = cu1_ref[rows, :].astype(f32)
            zc = zc_ref[rows, :].astype(f32)
            xc = c1 - jnp.mean(c1, axis=-1, keepdims=True)
            var = jnp.mean(xc * xc, axis=-1, keepdims=True)
            rstd = lax.rsqrt(var + LN_EPS)
            xh = xc * rstd
            ln = xh * g_ref[...] + bb_ref[...]
            sl_ = _sig(ln)
            sz = _sig(zc)
            dp_ref[rows, 2 * CW:3 * CW] = (dain * (ln * sl_) * (sz * (1.0 + zc * (1.0 - sz)))).astype(bf16)
            d_ln = dain * (zc * sz) * (sl_ * (1.0 + ln * (1.0 - sl_)))
            dlng_ref[...] += jnp.sum(d_ln * xh, axis=0, keepdims=True)
            dlnb_ref[...] += jnp.sum(d_ln, axis=0, keepdims=True)
            dxh = d_ln * g_ref[...]
            d_c1 = rstd * (dxh - jnp.mean(dxh, axis=-1, keepdims=True)
                           - xh * jnp.mean(dxh * xh, axis=-1, keepdims=True))
            dbias_ref[...] += jnp.sum(d_c1, axis=0, keepdims=True)
            _put_blocked(dbuf, pl.multiple_of(s * 32, 32), 32, d_c1)
            _put_blocked(ebuf, pl.multiple_of(NH * 8 + s * 32, 32), 32,
                         ca_ref[rows, :].astype(f32) * _sig(cb_ref[rows, :].astype(f32)))
            return carry
        lax.fori_loop(0, TC // 32, lnb, 0, unroll=4)
        sub = lax.broadcasted_iota(jnp.int32, (8, 128), 0)
        def after(p, carry):
            for lb in range(NLB):
                cur = dbuf[lb, _rows8(p), :]
                dbuf[lb, _rows8(R + p), :] = jnp.where(sub == 7, pltpu.roll(nxt[lb, _rows8(p), :], 7, 0),
                                                       pltpu.roll(cur, 7, 0))
            return carry
        lax.fori_loop(0, NH, after, 0)
        nxt[...] = dbuf[:, 0:NH * 8, :]
        def before(s, carry):
            rows = pl.ds(pl.multiple_of(s * 64, 64), 64)
            v = cah_ref[rows, :].astype(f32) * _sig(cbh_ref[rows, :].astype(f32))
            _put_blocked(prev, pl.multiple_of(s * 64, 64), 64, jnp.where(i == nc - 1, jnp.zeros_like(v), v))
            return carry
        lax.fori_loop(0, NH * 8 // 64, before, 0)
        _fill_before(ebuf, prev)
        for lb in range(NLB):
            sl = slice(lb * 128, (lb + 1) * 128)
            wk = [jnp.broadcast_to(w_ref[k:k + 1, sl], (8, 128)) for k in range(KS)]
            def tap(q, carry, lb=lb, wk=wk):
                r = q * RPI
                for j, o in enumerate(_fir(dbuf, lb, r, wk, None, True)):
                    dcu0[lb, _rows8(r + j), :] = o
                return carry
            lax.fori_loop(0, R // RPI, tap, 0)
            def wgrad(q, accs, lb=lb):
                r = q * RPI
                dvs = dbuf[lb, pl.ds(pl.multiple_of(r * 8, 8), RPI * 8), :]
                win = ebuf[lb, pl.ds(pl.multiple_of((r + (NH - KS + 1)) * 8, 8), (KS + RPI - 1) * 8), :]
                accs = list(accs)
                for j in range(RPI):
                    dv = dvs[8 * j:8 * j + 8, :]
                    for k in range(KS):
                        accs[k] = accs[k] + dv * win[8 * (j + k):8 * (j + k) + 8, :]
                return tuple(accs)
            accs = lax.fori_loop(0, R // RPI, wgrad, tuple(jnp.zeros((8, 128), f32) for _ in range(KS)))
            for k in range(KS):
                dw_ref[k, :, sl] += accs[k]
        def glub(s, carry):
            rows = pl.ds(pl.multiple_of(s * 64, 64), 64)
            d0 = _get_blocked(dcu0, pl.multiple_of(s * 64, 64), 64)
            ca = ca_ref[rows, :].astype(f32)
            sb = _sig(cb_ref[rows, :].astype(f32))
            dp_ref[rows, 0:CW] = (d0 * sb).astype(bf16)
            dp_ref[rows, CW:2 * CW] = (d0 * ca * sb * (1.0 - sb)).astype(bf16)
            return carry
        lax.fori_loop(0, TC // 64, glub, 0)

    hrows = NH * 8
    per = TC // hrows
    rev = lambda cidx: pl.BlockSpec((TC, CW), lambda i, cidx=cidx: (nc - 1 - i, cidx))
    halo = lambda cidx: pl.BlockSpec((hrows, CW), lambda i, cidx=cidx: (jnp.maximum((nc - 1 - i) * per - 1, 0), cidx))
    one = lambda shape: pl.BlockSpec(shape, lambda i: (0,) * len(shape))
    return pl.pallas_call(
        body, grid=(nc,),
        in_specs=[rev(0), rev(0), rev(0), rev(1), rev(2), halo(0), halo(1), one((32, CW)), one((1, CW)), one((1, CW)),
                  _ANY],
        out_specs=[pl.BlockSpec((TC, 3 * CW), lambda i: (nc - 1 - i, 0)), one((32, 8, CW)), one((1, CW)), one((1, CW)), one((1, CW))],
        out_shape=[jax.ShapeDtypeStruct((L, IN_W), bf16), jax.ShapeDtypeStruct((32, 8, CW), f32),
                   jax.ShapeDtypeStruct((1, CW), f32), jax.ShapeDtypeStruct((1, CW), f32),
                   jax.ShapeDtypeStruct((1, CW), f32)],
        scratch_shapes=[pltpu.VMEM((NLB, (R + NH) * 8, 128), f32), pltpu.VMEM((NLB, (NH + R) * 8, 128), f32),
                        pltpu.VMEM((NLB, hrows, 128), f32), pltpu.VMEM((NLB, hrows, 128), f32),
                        pltpu.VMEM((NLB, TC, 128), f32)],
        input_output_aliases={10: 0},
        name="conv_bwd", compiler_params=_cp("arbitrary"))(d_ain, cu1, proj, proj, proj, proj, proj, cw, lng, lnb, dproj)


def _win_grad(h, dproj):
    L = h.shape[0]
    tm = min(1024, L)
    nt = L // tm
    def body(h_ref, d_ref, o_ref, acc):
        i = pl.program_id(1)
        @pl.when(i == 0)
        def _():
            acc[...] = jnp.zeros_like(acc)
        acc[...] += _dot_tn(h_ref[...], d_ref[...])
        @pl.when(i == nt - 1)
        def _():
            o_ref[0] = acc[...].astype(bf16)
    return pl.pallas_call(
        body, grid=(NCHIP, nt),
        in_specs=[pl.BlockSpec((tm, D), lambda j, i: (i, 0)), pl.BlockSpec((tm, SHARD_W), lambda j, i: (i, j))],
        out_specs=pl.BlockSpec((1, D, SHARD_W), lambda j, i: (j, 0, 0)),
        out_shape=jax.ShapeDtypeStruct((NCHIP, D, SHARD_W), bf16),
        scratch_shapes=[pltpu.VMEM((D, SHARD_W), f32)],
        name="win_grad", compiler_params=_cp("arbitrary", "arbitrary"))(h, dproj)


def _adamw_math(w, g, m, v):
    m2 = B1 * m + (1.0 - B1) * g
    v2 = B2 * v + (1.0 - B2) * (g * g)
    m_hat = m2 / (1.0 - B1 ** STEP)
    v_hat = v2 / (1.0 - B2 ** STEP)
    delta = -LR * (m_hat / (jnp.sqrt(v_hat) + EPS) + WD * w)
    return delta, m2, v2


def _adamw(name, w, g, m, v):
    rows, cols = w.shape
    tm = rows if rows <= 256 else (256 if rows % 256 == 0 else 128)
    assert rows % tm == 0
    def body(w_ref, g_ref, m_ref, v_ref, d_ref, m2_ref, v2_ref):
        d, m2, v2 = _adamw_math(w_ref[...], g_ref[...], m_ref[...], v_ref[...])
        d_ref[...] = d
        m2_ref[...] = m2
        v2_ref[...] = v2
    spec = pl.BlockSpec((tm, cols), lambda i: (i, 0))
    shp = jax.ShapeDtypeStruct((rows, cols), f32)
    return pl.pallas_call(
        body, grid=(rows // tm,), in_specs=[spec] * 4, out_specs=[spec] * 3, out_shape=[shp] * 3,
        name=name, compiler_params=_cp("arbitrary"))(w, g, m, v)


def _adamw_group(name, ws, gs, ms, vs):
    n = len(ws)
    def body(*refs):
        for i in range(n):
            w_ref, g_ref, m_ref, v_ref = (refs[q * n + i] for q in range(4))
            d, m2, v2 = _adamw_math(w_ref[...], g_ref[...], m_ref[...], v_ref[...])
            for q, val in enumerate((d, m2, v2)):
                refs[(4 + q) * n + i][...] = val
    shapes = [jax.ShapeDtypeStruct(w.shape, f32) for w in ws]
    out = pl.pallas_call(body, out_shape=shapes * 3, name=name,
                         compiler_params=pltpu.CompilerParams(vmem_limit_bytes=VMEM_LIMIT))(*ws, *gs, *ms, *vs)
    return [(out[i], out[n + i], out[2 * n + i]) for i in range(n)]


_ANY = pl.BlockSpec(memory_space=pl.ANY)


def _chunks(rows, parts):
    step = rows // parts
    assert step * parts == rows and step % 16 == 0
    return [(i * step, step) for i in range(parts)]


def _place():
    x, y, c = lax.axis_index("x"), lax.axis_index("y"), lax.axis_index("c")
    chips = [(1 - x, y), (x, 1 - y), (1 - x, 1 - y)]
    return x, y, c, chips


def _nchunks(half, cols, itemsize):
    return 4 if half * cols * itemsize >= (1 << 20) else 1


def _segments(metas):
    segs = []
    for w, (half, cols, dt) in enumerate(metas):
        for r0, n in _chunks(half, _nchunks(half, cols, jnp.dtype(dt).itemsize)):
            segs.append((w, half, r0, n))
    return segs


def _rcopy(i, src, dst, send_sems, recv_sems, to):
    return pltpu.make_async_remote_copy(src_ref=src, dst_ref=dst, send_sem=send_sems.at[i], recv_sem=recv_sems.at[i],
                                        device_id=to, device_id_type=MESH)


def _gather_prep(k_arr, shards, x, tgt, g_pre, perm):
    na = len(shards)
    L = x.shape[0]
    nc = L // TC
    segs = _segments([(a.shape[0] // 2, a.shape[1], a.dtype) for a in shards])
    ns = len(segs)
    def body(_, *refs):
        ins = refs[:na]
        x_ref, t_ref, g_ref, p_ref = refs[na:na + 4]
        outs = refs[na + 4:2 * na + 4]
        h_ref, xi_ref, ti_ref, proj_ref = refs[2 * na + 4:2 * na + 8]
        stages = refs[2 * na + 8:3 * na + 8]
        send_sems, recv_sems, local_sems = refs[3 * na + 8:]
        i = pl.program_id(0)
        x, y, c, chips = _place()
        k = 2 * x + y
        me, sibling = (x, y, c), (x, y, 1 - c)

        def dst(w, half, chip, pc, r0, n):
            return outs[w].at[chip, pl.ds(pc * half + r0, n), :]

        def firsts():
            return [_rcopy(j * ns + s, ins[w].at[pl.ds(c * half + r0, n), :], dst(w, half, k, c, r0, n),
                           send_sems, recv_sems, (*chip, c))
                    for j, chip in enumerate(chips) for s, (w, half, r0, n) in enumerate(segs)]

        def own_out(w):
            return pltpu.make_async_copy(stages[w], outs[w].at[k], local_sems.at[w])

        @pl.when(i == 0)
        def _():
            for cp in firsts():
                cp.start()
            for w in range(na):
                cin = pltpu.make_async_copy(ins[w], stages[w], local_sems.at[w])
                cin.start()
                cin.wait()
            for w in range(na):
                own_out(w).start()

        p = p_ref[...]
        def through(v):
            hi = v.astype(bf16)
            r1 = v - hi.astype(f32)
            mid = r1.astype(bf16)
            lo = (r1 - mid.astype(f32)).astype(bf16)
            return (_dot(p, hi) + _dot(p, mid)) + _dot(p, lo)
        xt = x_ref[...]
        r = lax.rsqrt(jnp.mean(xt * xt, axis=-1, keepdims=True) + RMS_EPS)
        hp = _dot(p, (xt * r * g_ref[...]).astype(bf16)).astype(bf16)
        h_ref[...] = hp
        proj_ref[...] = _dot(hp, stages[0][...]).astype(bf16)
        xi_ref[...] = through(xt)
        ti_ref[...] = through(t_ref[...])

        @pl.when(i == nc - 1)
        def _():
            passed = []
            for j, chip in enumerate(chips):
                cj = 2 * chip[0] + chip[1]
                for s, (w, half, r0, n) in enumerate(segs):
                    landed = dst(w, half, cj, c, r0, n)
                    _rcopy(j * ns + s, landed, landed, send_sems, recv_sems, me).wait_recv()
                    fwd = _rcopy(3 * ns + j * ns + s, landed, landed, send_sems, recv_sems, sibling)
                    fwd.start()
                    passed.append(fwd)
            for j, chip in enumerate(chips):
                cj = 2 * chip[0] + chip[1]
                for s, (w, half, r0, n) in enumerate(segs):
                    theirs = dst(w, half, cj, 1 - c, r0, n)
                    _rcopy(3 * ns + j * ns + s, theirs, theirs, send_sems, recv_sems, me).wait_recv()
            for cp in firsts() + passed:
                cp.wait_send()
            for w in range(na):
                own_out(w).wait()

    row = lambda: pl.BlockSpec((TC, D), lambda i, k: (i, 0))
    grid_spec = pltpu.PrefetchScalarGridSpec(
        num_scalar_prefetch=1, grid=(nc,),
        in_specs=[_ANY] * na + [row(), row(), pl.BlockSpec((1, D), lambda i, k: (0, 0)),
                                pl.BlockSpec((TC, TC), lambda i, k: (0, 0))],
        out_specs=[_ANY] * na + [row(), row(), row(), pl.BlockSpec((TC, SHARD_W), lambda i, k: (i, k[0]))],
        scratch_shapes=[pltpu.VMEM(a.shape, a.dtype) for a in shards]
        + [pltpu.SemaphoreType.DMA((6 * ns,)), pltpu.SemaphoreType.DMA((6 * ns,)), pltpu.SemaphoreType.DMA((na,))])
    return pl.pallas_call(
        body, grid_spec=grid_spec,
        out_shape=[jax.ShapeDtypeStruct((NCHIP,) + a.shape, a.dtype) for a in shards]
        + [jax.ShapeDtypeStruct((L, D), bf16), jax.ShapeDtypeStruct((L, D), f32), jax.ShapeDtypeStruct((L, D), f32),
           jax.ShapeDtypeStruct((L, IN_W), bf16)],
        name="gather_prep", compiler_params=_cp("arbitrary"))(k_arr, *shards, x, tgt, g_pre, perm)


def _x_grad_exchange(dproj, w_in, x, gx0, g_pre, parts, small):
    L = x.shape[0]
    tm = 512
    nt = L // tm
    na = len(parts)
    hs = SMALL_ROWS // 2
    segs = _segments([(p.shape[1], p.shape[2], p.dtype) for p in parts])
    ns = len(segs) + 1
    def body(*refs):
        d_ref, w_ref, x_ref, gx_ref, g_ref = refs[:5]
        ins, s_ref = refs[5:5 + na], refs[5 + na]
        o_ref, dg_ref = refs[6 + na:8 + na]
        outs, qs_ref = refs[8 + na:8 + 2 * na], refs[8 + 2 * na]
        stages = refs[9 + 2 * na:10 + 3 * na]
        send_sems, recv_sems, local_sems = refs[10 + 3 * na:]
        i = pl.program_id(0)
        x, y, c, chips = _place()
        k = 2 * x + y

        def my_small():
            return s_ref.at[pl.ds(c * hs, hs), :]

        def copies():
            out = []
            for j, chip in enumerate(chips):
                cj = 2 * chip[0] + chip[1]
                pieces = [(my_small(), qs_ref.at[k])]
                pieces += [(ins[w].at[cj, pl.ds(r0, n), :], outs[w].at[k, pl.ds(r0, n), :]) for w, _, r0, n in segs]
                out += [_rcopy(ns * j + s, src, d, send_sems, recv_sems, (*chip, c)) for s, (src, d) in enumerate(pieces)]
            return out

        def own_out(w):
            dst = qs_ref.at[k] if w == na else outs[w].at[k]
            return pltpu.make_async_copy(stages[w], dst, local_sems.at[w])

        @pl.when(i == 0)
        def _():
            dg_ref[...] = jnp.zeros_like(dg_ref)
            for cp in copies():
                cp.start()
            for w in range(na + 1):
                cin = pltpu.make_async_copy(my_small() if w == na else ins[w].at[k], stages[w], local_sems.at[w])
                cin.start()
                cin.wait()
            for w in range(na + 1):
                own_out(w).start()

        dh = _dot_nt(d_ref[:, 0:SHARD_W], w_ref[0])
        for j in range(1, NCHIP):
            dh = dh + _dot_nt(d_ref[:, j * SHARD_W:(j + 1) * SHARD_W], w_ref[j])
        xt = x_ref[...]
        r = lax.rsqrt(jnp.mean(xt * xt, axis=-1, keepdims=True) + RMS_EPS)
        xn = xt * r
        dg_ref[...] += jnp.sum(dh * xn, axis=0, keepdims=True)
        dxn = dh * g_ref[...]
        o_ref[...] = gx_ref[...] + r * (dxn - xn * jnp.mean(dxn * xn, axis=-1, keepdims=True))

        @pl.when(i == nt - 1)
        def _():
            for cp in copies():
                cp.wait_recv()
            for cp in copies():
                cp.wait_send()
            for w in range(na + 1):
                own_out(w).wait()

    return pl.pallas_call(
        body, grid=(nt,),
        in_specs=[pl.BlockSpec((tm, IN_W), lambda i: (i, 0)),
                  pl.BlockSpec((NCHIP, D, SHARD_W), lambda i: (0, 0, 0), pipeline_mode=pl.Buffered(1)),
                  pl.BlockSpec((tm, D), lambda i: (i, 0)), pl.BlockSpec((tm, D), lambda i: (i, 0)), _full((1, D))]
        + [_ANY] * (na + 1),
        out_specs=[pl.BlockSpec((tm, D), lambda i: (i, 0)), _full((1, D))] + [_ANY] * (na + 1),
        out_shape=[jax.ShapeDtypeStruct((L, D), f32), jax.ShapeDtypeStruct((1, D), f32)]
        + [jax.ShapeDtypeStruct(p.shape, bf16) for p in parts] + [jax.ShapeDtypeStruct((NCHIP, hs, 128), f32)],
        scratch_shapes=[pltpu.VMEM(p.shape[1:], bf16) for p in parts] + [pltpu.VMEM((hs, 128), f32)]
        + [pltpu.SemaphoreType.DMA((3 * ns,)), pltpu.SemaphoreType.DMA((3 * ns,)), pltpu.SemaphoreType.DMA((na + 1,))],
        name="x_grad_exchange", compiler_params=_cp("arbitrary"))(dproj, w_in, x, gx0, g_pre, *parts, small)


def _sibling_join_list(halves):
    na = len(halves)
    segs = _segments([(h.shape[0], h.shape[1], h.dtype) for h in halves])
    def body(*refs):
        ins, outs, stages = refs[:na], refs[na:2 * na], refs[2 * na:3 * na]
        send_sems, recv_sems, local_sems = refs[3 * na:]
        x, y, c, _ = _place()
        copies = [_rcopy(i, ins[w].at[pl.ds(r0, n), :], outs[w].at[pl.ds(c * half + r0, n), :], send_sems, recv_sems,
                         (x, y, 1 - c)) for i, (w, half, r0, n) in enumerate(segs)]
        for cp in copies:
            cp.start()
        own = []
        for w in range(na):
            cin = pltpu.make_async_copy(ins[w], stages[w], local_sems.at[w])
            cin.start()
            cin.wait()
            half = halves[w].shape[0]
            own.append(pltpu.make_async_copy(stages[w], outs[w].at[pl.ds(c * half, half), :], local_sems.at[w]))
            own[-1].start()
        for cp in copies:
            cp.wait_recv()
        for cp in copies:
            cp.wait_send()
        for cp in own:
            cp.wait()

    return pl.pallas_call(
        body, in_specs=[_ANY] * na, out_specs=[_ANY] * na,
        out_shape=[jax.ShapeDtypeStruct((2 * h.shape[0], h.shape[1]), f32) for h in halves],
        scratch_shapes=[pltpu.VMEM(h.shape, f32) for h in halves]
        + [pltpu.SemaphoreType.DMA((len(segs),)), pltpu.SemaphoreType.DMA((len(segs),)), pltpu.SemaphoreType.DMA((na,))],
        name="sibling_join")(*halves)


def _allgather_rows(v):
    def body(v_ref, o_ref, send_sems, recv_sems):
        x, y, c, _ = _place()
        me = 4 * x + 2 * y + c
        o_ref[me] = v_ref[...]
        copies = []
        i = 0
        for dx in range(2):
            for dy in range(2):
                for dc in range(2):
                    if dx + dy + dc:
                        copies.append(_rcopy(i, v_ref, o_ref.at[me], send_sems, recv_sems, (x ^ dx, y ^ dy, c ^ dc)))
                        i += 1
        for cp in copies:
            cp.start()
        for cp in copies:
            cp.wait_recv()
        for cp in copies:
            cp.wait_send()

    vm = pl.BlockSpec(memory_space=pltpu.VMEM)
    return pl.pallas_call(
        body, in_specs=[vm], out_specs=vm, out_shape=jax.ShapeDtypeStruct((8, 8, 128), f32),
        scratch_shapes=[pltpu.SemaphoreType.DMA((7,)), pltpu.SemaphoreType.DMA((7,))],
        name="allgather_rows")(v)


def _adamw_rows(parts, w, m, v):
    def body(p_ref, w_ref, m_ref, v_ref, g_ref, d_ref, m2_ref, v2_ref):
        g = p_ref[0]
        for dvc in range(1, 8):
            g = g + p_ref[dvc]
        g_ref[...] = g
        d, m2, v2 = _adamw_math(w_ref[...], g, m_ref[...], v_ref[...])
        d_ref[...] = d
        m2_ref[...] = m2
        v2_ref[...] = v2
    return pl.pallas_call(body, out_shape=[jax.ShapeDtypeStruct((8, 128), f32)] * 4, name="adamw_pre_norm_gain")(
        parts, w, m, v)


def _pair_exchange_list(grads, small):
    na = len(grads)
    segs = _segments([(g.shape[1] // 2, g.shape[2], g.dtype) for g in grads])
    n = NCHIP * len(segs) + 1
    def body(*refs):
        ins, s_ref, outs, rs_ref, (send_sems, recv_sems) = (refs[:na], refs[na], refs[na + 1:2 * na + 1],
                                                            refs[2 * na + 1], refs[2 * na + 2:])
        x, y, c, _ = _place()
        pieces = [(s_ref, rs_ref)]
        for j in range(NCHIP):
            for w, half, r0, rows in segs:
                pieces.append((ins[w].at[j, pl.ds((1 - c) * half + r0, rows), :], outs[w].at[j, pl.ds(r0, rows), :]))
        copies = [_rcopy(i, s, d, send_sems, recv_sems, (x, y, 1 - c)) for i, (s, d) in enumerate(pieces)]
        for cp in copies:
            cp.start()
        for cp in copies:
            cp.wait_recv()
        for cp in copies:
            cp.wait_send()

    return pl.pallas_call(
        body, in_specs=[_ANY] * (na + 1), out_specs=[_ANY] * (na + 1),
        out_shape=[jax.ShapeDtypeStruct((NCHIP, g.shape[1] // 2, g.shape[2]), g.dtype) for g in grads]
        + [jax.ShapeDtypeStruct((SMALL_ROWS, 128), f32)],
        scratch_shapes=[pltpu.SemaphoreType.DMA((n,)), pltpu.SemaphoreType.DMA((n,))],
        name="pair_exchange")(*grads, small)


def _pair_sum_list(c_arr, grads, recvs, small, rsmall):
    na = len(grads)
    def body(c_ref, *refs):
        g_refs, r_refs, s_ref, rs_ref = refs[:na], refs[na:2 * na], refs[2 * na], refs[2 * na + 1]
        o_refs, os_ref = refs[2 * na + 2:3 * na + 2], refs[3 * na + 2]
        for g_ref, r_ref, o_ref in zip(g_refs, r_refs, o_refs):
            o_ref[...] = (g_ref[...].astype(f32) + r_ref[...].astype(f32)).astype(bf16)
        os_ref[...] = s_ref[...] + rs_ref[...]
    half = lambda g: pl.BlockSpec((1, g.shape[1] // 2, g.shape[2]), lambda j, c: (j, c[0], 0))
    low = lambda g: pl.BlockSpec((1, g.shape[1] // 2, g.shape[2]), lambda j, c: (j, 0, 0))
    sm = pl.BlockSpec((SMALL_ROWS, 128), lambda j, c: (0, 0))
    grid_spec = pltpu.PrefetchScalarGridSpec(
        num_scalar_prefetch=1, grid=(NCHIP,),
        in_specs=[half(g) for g in grads] + [low(g) for g in grads] + [sm, sm],
        out_specs=[low(g) for g in grads] + [sm])
    return pl.pallas_call(
        body, grid_spec=grid_spec,
        out_shape=[jax.ShapeDtypeStruct((NCHIP, g.shape[1] // 2, g.shape[2]), bf16) for g in grads]
        + [jax.ShapeDtypeStruct((SMALL_ROWS, 128), f32)],
        name="pair_sum", compiler_params=_cp("arbitrary"))(c_arr, *grads, *recvs, small, rsmall)


def _chip_sum_list(parts, small):
    na = len(parts)
    nt = 2
    def body(*refs):
        for q_ref, f_ref in zip(refs[:na + 1], refs[na + 1:]):
            acc = q_ref[0].astype(f32)
            for j in range(1, NCHIP):
                acc = acc + q_ref[j].astype(f32)
            f_ref[...] = acc
    arrs = list(parts) + [small]
    return pl.pallas_call(
        body, grid=(nt,),
        in_specs=[pl.BlockSpec((NCHIP, a.shape[1] // nt, a.shape[2]), lambda i: (0, i, 0)) for a in arrs],
        out_specs=[pl.BlockSpec((a.shape[1] // nt, a.shape[2]), lambda i: (i, 0)) for a in arrs],
        out_shape=[jax.ShapeDtypeStruct(a.shape[1:], f32) for a in arrs],
        name="chip_sum", compiler_params=_cp("arbitrary"))(*arrs)


_SMALL =(("conv_b", (1, 1024)), ("conv_ln_gain", (1, 1024)), ("conv_ln_bias", (1, 1024)),
          ("ssm_lambda_re", (1, 32, 64)), ("ssm_lambda_im", (1, 32, 64)), ("ssm_log_dt", (1, 32)),
          ("ssm_b_re", (1, 32, 64, 16)), ("ssm_b_im", (1, 32, 64, 16)), ("ssm_c_re", (1, 32, 16, 64)),
          ("ssm_c_im", (1, 32, 16, 64)), ("ssm_d", (1, 32, 16)), ("b_ssm_glu", (1, 512)), ("post_norm_gain", (1, 1024)))


def _pack_small(vals, extra=None):
    rows = []
    for v in list(vals) + ([extra] if extra is not None else []):
        flat = v.reshape(-1).astype(f32)
        n = -(-flat.shape[0] // 1024) * 1024
        rows.append(jnp.pad(flat, (0, n - flat.shape[0])).reshape(-1, 128))
    used = sum(r.shape[0] for r in rows)
    rows.append(jnp.zeros((SMALL_ROWS - used, 128), f32))
    return jnp.concatenate(rows, axis=0)


def _unpack_small(p):
    o = 0
    out = []
    for _, shape in _SMALL:
        n = int(np.prod(shape))
        nr = -(-n // 1024) * 8
        out.append(p[o:o + nr].reshape(-1)[:n].reshape(shape))
        o += nr
    return out, p[o, 0]


def _discretize(lam_re, lam_im, log_dt, b_re, b_im):
    dt = jnp.exp(log_dt)[:, None]
    mag = jnp.exp(lam_re * dt)
    ar = mag * jnp.cos(lam_im * dt)
    ai = mag * jnp.sin(lam_im * dt)
    den = lam_re * lam_re + lam_im * lam_im
    zr = ((ar - 1.0) * lam_re + ai * lam_im) / den
    zi = (ai * lam_re - (ar - 1.0) * lam_im) / den
    bbr = zr[..., None] * b_re - zi[..., None] * b_im
    bbi = zr[..., None] * b_im + zi[..., None] * b_re
    return ar, ai, bbr, bbi


_EYE8 = np.eye(8, dtype=np.float32)


def _bbt_blocks(bb):
    v = bb.reshape(4, 8, PST, H).transpose(0, 1, 3, 2)
    return jnp.einsum("bghp,gk->bghkp", v, _EYE8).reshape(4, 128, 512)


def _bbt_unblock(m):
    v = jnp.einsum("bghkp,gk->bghp", m.reshape(4, 8, H, 8, PST), _EYE8)
    return v.transpose(0, 1, 3, 2).reshape(G, PST, H)


def _ct_blocks(cc):
    v = cc.reshape(4, 8, H, PST)
    return jnp.einsum("bghp,gk->bgpkh", v, _EYE8).reshape(4, 512, 128)


def _ct_unblock(m):
    return jnp.einsum("bghkp,gk->bghp", m.reshape(4, 8, H, 8, PST), _EYE8).reshape(G, H, PST)


def _perm_matrix():
    p = np.zeros((TC, TC), np.float32)
    for r in range(R):
        for seg in range(8):
            p[r * 8 + seg, seg * R + r] = 1.0
    return p


def _deinterleave(a):
    L, C = a.shape
    return a.reshape(L // TC, R, 8, C).transpose(0, 2, 1, 3).reshape(L, C)


def _fwd_bwd(h, xi, ti, proj, conv_w, w_co, w_glu, w_so, w_out, small):
    (conv_b, ln_g, ln_b, lam_re, lam_im, log_dt, b_re, b_im, c_re, c_im, dvec, b_glu, g_post) = small
    lam_re, lam_im, log_dt = lam_re[0], lam_im[0], log_dt[0]
    b_re, b_im, c_re, c_im = b_re[0], b_im[0], c_re[0], c_im[0]
    (ar, ai, bbr, bbi), disc_vjp = jax.vjp(_discretize, lam_re, lam_im, log_dt, b_re, b_im)
    a_re = ar.reshape(1, NS)
    a_im = ai.reshape(1, NS)
    dt = jnp.exp(log_dt)[:, None]
    steps = jnp.arange(1, R + 1, dtype=f32)[:, None, None]
    apow_re = (jnp.exp(steps * (lam_re * dt)) * jnp.cos(steps * (lam_im * dt))).reshape(R, NS)
    apow_im = (jnp.exp(steps * (lam_re * dt)) * jnp.sin(steps * (lam_im * dt))).reshape(R, NS)
    bbt_re, bbt_im = _bbt_blocks(bbr).astype(bf16), _bbt_blocks(bbi).astype(bf16)
    ct_re, ct_im = _ct_blocks(c_re).astype(bf16), _ct_blocks(c_im).astype(bf16)
    d_row = dvec.reshape(1, SW)
    cw32 = jnp.pad(conv_w, ((0, 1), (0, 0)))

    cu1, a_in = _conv_fwd(proj, cw32, conv_b, ln_g, ln_b)
    y0, b_in, sre, sim, cinr, cini = _ssm_fwd(proj, bbt_re, bbt_im, ct_re, ct_im, a_re, a_im,
                                              apow_re, apow_im, d_row, w_glu, b_glu)
    gx0, d_ain, d_bin, dproj, dw_out, dw_co, dw_so, dg_post, loss = _tail(
        a_in, b_in, proj, xi, ti, w_co, w_so, w_out, g_post)
    (dproj, dbbt_re, dbbt_im, dct_re, dct_im, dd, dar8, dai8, dw_glu, db_glu) = _ssm_bwd(
        d_bin, y0, proj, sre, sim, cinr, cini, bbt_re, bbt_im, ct_re, ct_im,
        a_re, a_im, apow_re, apow_im, d_row, w_glu, b_glu, dproj)
    dproj, dcw8, d_convb, d_lng, d_lnb = _conv_bwd(d_ain, cu1, proj, cw32, ln_g, ln_b, dproj)
    dw_in = _win_grad(h, dproj)

    d_ar = jnp.sum(dar8, axis=0).reshape(G, PST)
    d_ai = jnp.sum(dai8, axis=0).reshape(G, PST)
    d_lre, d_lim, d_ldt, d_bre, d_bim = disc_vjp((d_ar, d_ai, _bbt_unblock(dbbt_re), _bbt_unblock(dbbt_im)))
    d_conv_w = jnp.sum(dcw8, axis=1)[:KS]
    small_grads = [d_convb, d_lng, d_lnb, d_lre[None], d_lim[None], d_ldt[None], d_bre[None], d_bim[None],
                   _ct_unblock(dct_re)[None], _ct_unblock(dct_im)[None], dd.reshape(1, G, H), db_glu, dg_post]
    return loss[0, 0], gx0, dproj, (dw_in, dw_co, dw_out, dw_glu, dw_so, d_conv_w), small_grads


def kernel(x, pre_norm_gain, w_in, conv_w, conv_b, conv_ln_gain, conv_ln_bias, w_conv_out, ssm_lambda_re, ssm_lambda_im, ssm_log_dt, ssm_b_re, ssm_b_im, ssm_c_re, ssm_c_im, ssm_d, w_ssm_glu, b_ssm_glu, w_ssm_out, w_out, post_norm_gain, loss_target, m_pre_norm_gain, m_w_in, m_conv_w, m_conv_b, m_conv_ln_gain, m_conv_ln_bias, m_w_conv_out, m_ssm_lambda_re, m_ssm_lambda_im, m_ssm_log_dt, m_ssm_b_re, m_ssm_b_im, m_ssm_c_re, m_ssm_c_im, m_ssm_d, m_w_ssm_glu, m_b_ssm_glu, m_w_ssm_out, m_w_out, m_post_norm_gain, v_pre_norm_gain, v_w_in, v_conv_w, v_conv_b, v_conv_ln_gain, v_conv_ln_bias, v_w_conv_out, v_ssm_lambda_re, v_ssm_lambda_im, v_ssm_log_dt, v_ssm_b_re, v_ssm_b_im, v_ssm_c_re, v_ssm_c_im, v_ssm_d, v_w_ssm_glu, v_b_ssm_glu, v_w_ssm_out, v_w_out, v_post_norm_gain):
    c = lax.axis_index("c")
    shards = [w_in[0].astype(bf16), w_conv_out[0].astype(bf16), w_out[0].astype(bf16), w_ssm_glu[0].astype(bf16),
              w_ssm_out[0].astype(bf16), jnp.pad(conv_w[0], ((0, CONV_ROWS - KS), (0, 0)))]
    k_arr = (2 * lax.axis_index("x") + lax.axis_index("y")).astype(jnp.int32).reshape(1)
    w_in_g, w_co_g, w_out_g, w_glu_g, w_so_g, conv_w_g, h, xi, ti, proj = _gather_prep(
        k_arr, shards, x[0], loss_target[0], pre_norm_gain, jnp.asarray(_perm_matrix(), bf16))
    conv_w_f = conv_w_g[:, :KS].transpose(1, 0, 2).reshape(KS, CW)

    small = (conv_b, conv_ln_gain, conv_ln_bias, ssm_lambda_re, ssm_lambda_im, ssm_log_dt, ssm_b_re,
             ssm_b_im, ssm_c_re, ssm_c_im, ssm_d, b_ssm_glu, post_norm_gain)
    loss_part, gx0, dproj, big_grads, small_grads = _fwd_bwd(
        h, xi, ti, _proj_fwd(k_arr, h, w_in_g, proj), conv_w_f, w_co_g.reshape(CW, D), w_glu_g.reshape(SW, SW), w_so_g,
        w_out_g.reshape(D, D), small)

    dw_in, dw_co, dw_out, dw_glu, dw_so, d_conv_w = big_grads
    d_conv_w = jnp.pad(d_conv_w, ((0, CONV_ROWS - KS), (0, 0))).reshape(CONV_ROWS, NCHIP, 256).transpose(1, 0, 2)
    grads = [dw_in] + [g.astype(bf16) for g in (dw_co.reshape(NCHIP, 256, D), dw_out.reshape(NCHIP, 256, D),
                                                  dw_glu.reshape(NCHIP, 128, SW), dw_so, d_conv_w)]
    gs = _pack_small(small_grads, extra=loss_part)
    *recvs, rs = _pair_exchange_list(grads, gs)
    *parts, ps = _pair_sum_list(c.astype(jnp.int32).reshape(1), grads, recvs, gs, rs)
    gxi, dg_pre, *qparts, qs = _x_grad_exchange(dproj, w_in_g, xi, gx0, pre_norm_gain, parts, ps)
    grad_x = _deinterleave(gxi)
    *g_big, fs = _sibling_join_list(_chip_sum_list(qparts, qs))
    g_big[5] = g_big[5][:KS]

    big_w = (w_in[0], w_conv_out[0], w_out[0], w_ssm_glu[0], w_ssm_out[0], conv_w[0])
    big_m = (m_w_in[0], m_w_conv_out[0], m_w_out[0], m_w_ssm_glu[0], m_w_ssm_out[0], m_conv_w[0])
    big_v = (v_w_in[0], v_w_conv_out[0], v_w_out[0], v_w_ssm_glu[0], v_w_ssm_out[0], v_conv_w[0])
    big_names = ("w_in", "w_conv_out", "w_out", "w_ssm_glu", "w_ssm_out", "conv_w")
    res = {}
    upd = [_adamw("adamw_w_in", big_w[0], g_big[0], big_m[0], big_v[0])]
    upd += _adamw_group("adamw_rest", big_w[1:], g_big[1:], big_m[1:], big_v[1:])
    for n, g, (d, m2, v2) in zip(big_names, g_big, upd):
        res[n] = (g[None], d[None], m2[None], v2[None])

    small_m = (m_conv_b, m_conv_ln_gain, m_conv_ln_bias, m_ssm_lambda_re, m_ssm_lambda_im, m_ssm_log_dt,
               m_ssm_b_re, m_ssm_b_im, m_ssm_c_re, m_ssm_c_im, m_ssm_d, m_b_ssm_glu, m_post_norm_gain)
    small_v = (v_conv_b, v_conv_ln_gain, v_conv_ln_bias, v_ssm_lambda_re, v_ssm_lambda_im, v_ssm_log_dt,
               v_ssm_b_re, v_ssm_b_im, v_ssm_c_re, v_ssm_c_im, v_ssm_d, v_b_ssm_glu, v_post_norm_gain)
    sd, sm, sv = _adamw("adamw_small", _pack_small(small), fs, _pack_small(small_m), _pack_small(small_v))
    sg_l, loss = _unpack_small(fs)
    sd_l, _ = _unpack_small(sd)
    sm_l, _ = _unpack_small(sm)
    sv_l, _ = _unpack_small(sv)
    for i, (n, _) in enumerate(_SMALL):
        res[n] = (sg_l[i], sd_l[i], sm_l[i], sv_l[i])
    rows = lambda a: a.reshape(8, 128)
    pre = _adamw_rows(_allgather_rows(rows(dg_pre)), rows(pre_norm_gain), rows(m_pre_norm_gain), rows(v_pre_norm_gain))
    res["pre_norm_gain"] = tuple(a.reshape(1, D) for a in pre)

    order = ("pre_norm_gain", "w_in", "conv_w", "conv_b", "conv_ln_gain", "conv_ln_bias", "w_conv_out", "ssm_lambda_re",
             "ssm_lambda_im", "ssm_log_dt", "ssm_b_re", "ssm_b_im", "ssm_c_re", "ssm_c_im", "ssm_d", "w_ssm_glu",
             "b_ssm_glu", "w_ssm_out", "w_out", "post_norm_gain")
    outs = [loss, grad_x[None]]
    for q in range(4):
        outs.extend(res[n][q] for n in order)
    return tuple(outs)
```

```python
import math

import numpy as np
import jax
import jax.numpy as jnp
from jax import lax
from jax.experimental import pallas as pl
from jax.experimental.pallas import tpu as pltpu

f32 = jnp.float32
bf16 = jnp.bfloat16

D = 1024
CW = 1024
SW = 512
G = 32
H = 16
PST = 64
NS = G * PST
KS = 31
IN_W = 6144
NCHIP = 4
SHARD_W = IN_W // NCHIP
RMS_EPS = 1e-6
LN_EPS = 1e-5
LR, B1, B2, EPS, WD, STEP = 0.001, 0.9, 0.999, 1e-08, 0.01, 10
GELU_K0 = math.sqrt(2.0 / math.pi)
GELU_K1 = 0.044715

TC = 512
R = TC // 8
NH = 32
LBW = 1024
CONV_ROWS = 64
SMALL_ROWS = 1152
VMEM_LIMIT = 56 * 1024 * 1024
MESH = pl.DeviceIdType.MESH


def _cp(*sem):
    return pltpu.CompilerParams(dimension_semantics=tuple(sem), vmem_limit_bytes=VMEM_LIMIT)


def _sig(v):
    return 0.5 * jnp.tanh(0.5 * v) + 0.5


def _dot(a, b):
    return jnp.dot(a, b, preferred_element_type=f32)


def _dot_nt(a, b):
    return lax.dot_general(a, b, (((1,), (1,)), ((), ())), preferred_element_type=f32)


def _dot_tn(a, b):
    return lax.dot_general(a, b, (((0,), (0,)), ((), ())), preferred_element_type=f32)


def _full(shape):
    nd = len(shape)
    return pl.BlockSpec(shape, lambda *_: (0,) * nd)


def _rows8(i):
    return pl.ds(pl.multiple_of(i * 8, 8), 8)


def _proj_fwd(k_arr, h, w_in, proj):
    L = h.shape[0]
    tm = min(1024, L)
    def body(_, h_ref, w_ref, __, o_ref):
        o_ref[...] = _dot(h_ref[...], w_ref[0]).astype(bf16)
    shard = lambda j, k: (k[0] + 1 + j) % NCHIP
    grid_spec = pltpu.PrefetchScalarGridSpec(
        num_scalar_prefetch=1, grid=(NCHIP - 1, L // tm),
        in_specs=[pl.BlockSpec((tm, D), lambda j, i, k: (i, 0)),
                  pl.BlockSpec((1, D, SHARD_W), lambda j, i, k: (shard(j, k), 0, 0)), _ANY],
        out_specs=pl.BlockSpec((tm, SHARD_W), lambda j, i, k: (i, shard(j, k))))
    return pl.pallas_call(
        body, grid_spec=grid_spec, out_shape=jax.ShapeDtypeStruct((L, IN_W), bf16),
        input_output_aliases={3: 0},
        name="proj_fwd", compiler_params=_cp("arbitrary", "arbitrary"))(k_arr, h, w_in, proj)


NLB = CW // 128
RPI = 8


def _put_blocked(buf, row0, nrows, v):
    for lb in range(NLB):
        buf[lb, pl.ds(row0, nrows), :] = v[:, lb * 128:(lb + 1) * 128]


def _get_blocked(buf, row0, nrows):
    return jnp.concatenate([buf[lb, pl.ds(row0, nrows), :] for lb in range(NLB)], axis=1)


def _fill_before(ebuf, prev):
    sub = lax.broadcasted_iota(jnp.int32, (8, 128), 0)
    def halo(p, carry):
        for lb in range(NLB):
            cur = ebuf[lb, _rows8(R + p), :]
            ebuf[lb, _rows8(p), :] = jnp.where(sub == 0, pltpu.roll(prev[lb, _rows8(p), :], 1, 0),
                                               pltpu.roll(cur, 1, 0))
        return carry
    lax.fori_loop(0, NH, halo, 0)


def _fir(buf, lb, r, coef, first, flip):
    win = buf[lb, pl.ds(pl.multiple_of(r * 8, 8), (KS + RPI - 1) * 8), :]
    outs = []
    for i in range(RPI):
        acc = [first, None, None, None]
        for k in range(KS):
            o = i + ((KS - 1 - k) if flip else k)
            t = coef[k] * win[8 * o:8 * o + 8, :]
            acc[k % 4] = t if acc[k % 4] is None else acc[k % 4] + t
        outs.append((acc[0] + acc[1]) + (acc[2] + acc[3]))
    return outs


def _conv_fwd(proj, cw, cbias, lng, lnb):
    L = proj.shape[0]
    nc = L // TC
    def body(ca_ref, cb_ref, zc_ref, w_ref, b_ref, g_ref, bb_ref, cu1_ref, ain_ref, ebuf, prev, cacc):
        @pl.when(pl.program_id(0) == 0)
        def _():
            prev[...] = jnp.zeros_like(prev)
        def glu(s, carry):
            rows = pl.ds(pl.multiple_of(s * 64, 64), 64)
            _put_blocked(ebuf, pl.multiple_of(NH * 8 + s * 64, 64), 64,
                         ca_ref[rows, :].astype(f32) * _sig(cb_ref[rows, :].astype(f32)))
            return carry
        lax.fori_loop(0, TC // 64, glu, 0)
        _fill_before(ebuf, prev)
        prev[...] = ebuf[:, R * 8:(NH + R) * 8, :]
        for lb in range(NLB):
            sl = slice(lb * 128, (lb + 1) * 128)
            wk = [jnp.broadcast_to(w_ref[k:k + 1, sl], (8, 128)) for k in range(KS)]
            bias = jnp.broadcast_to(b_ref[:, sl], (8, 128))
            def tap(q, carry, lb=lb, wk=wk, bias=bias):
                r = q * RPI
                for i, o in enumerate(_fir(ebuf, lb, r + (NH - KS + 1), wk, bias, False)):
                    cacc[lb, _rows8(r + i), :] = o
                return carry
            lax.fori_loop(0, R // RPI, tap, 0)
        def norm(s, carry):
            rows = pl.ds(pl.multiple_of(s * 64, 64), 64)
            c1b = _get_blocked(cacc, pl.multiple_of(s * 64, 64), 64).astype(bf16)
            cu1_ref[rows, :] = c1b
            c1 = c1b.astype(f32)
            xc = c1 - jnp.mean(c1, axis=-1, keepdims=True)
            var = jnp.mean(xc * xc, axis=-1, keepdims=True)
            ln = xc * lax.rsqrt(var + LN_EPS) * g_ref[...] + bb_ref[...]
            zc = zc_ref[rows, :].astype(f32)
            ain_ref[rows, :] = ((ln * _sig(ln)) * (zc * _sig(zc))).astype(bf16)
            return carry
        lax.fori_loop(0, TC // 64, norm, 0, unroll=4)

    col = lambda c: pl.BlockSpec((TC, CW), lambda i, c=c: (i, c))
    return pl.pallas_call(
        body, grid=(nc,),
        in_specs=[col(0), col(1), col(2), _full((32, CW)), _full((1, CW)), _full((1, CW)), _full((1, CW))],
        out_specs=[pl.BlockSpec((TC, CW), lambda i: (i, 0)), pl.BlockSpec((TC, CW), lambda i: (i, 0))],
        out_shape=[jax.ShapeDtypeStruct((L, CW), bf16), jax.ShapeDtypeStruct((L, CW), bf16)],
        scratch_shapes=[pltpu.VMEM((NLB, (NH + R) * 8, 128), f32), pltpu.VMEM((NLB, NH * 8, 128), f32),
                        pltpu.VMEM((NLB, TC, 128), f32)],
        name="conv_fwd", compiler_params=_cp("arbitrary"))(proj, proj, proj, cw, cbias, lng, lnb)


def _gelu_parts(y0):
    t = jnp.tanh(GELU_K0 * (y0 + GELU_K1 * y0 * y0 * y0))
    return t, 0.5 * y0 * (1.0 + t)


def _ssm_fwd(proj, bbt_re, bbt_im, ct_re, ct_im, a_re, a_im, apow_re, apow_im, dvec, wglu, bglu):
    L = proj.shape[0]
    nc = L // TC
    def body(u_ref, zs_ref, bre_ref, bim_ref, cre_ref, cim_ref, are_ref, aim_ref, pwr_ref, pwi_ref,
             d_ref, wg_ref, bg_ref, y0_ref, bin_ref, sre, sim, cinr, cini, prev_re, prev_im):
        c = pl.program_id(0)
        @pl.when(c == 0)
        def _():
            prev_re[...] = jnp.zeros_like(prev_re)
            prev_im[...] = jnp.zeros_like(prev_im)
        u = u_ref[...]
        for blk in range(4):
            ub = u[:, 128 * blk:128 * (blk + 1)]
            sre[:, 512 * blk:512 * (blk + 1)] = _dot(ub, bre_ref[blk])
            sim[:, 512 * blk:512 * (blk + 1)] = _dot(ub, bim_ref[blk])
        for lb in range(NS // LBW):
            sl = slice(lb * LBW, (lb + 1) * LBW)
            ar = jnp.broadcast_to(are_ref[:, sl], (8, LBW))
            ai = jnp.broadcast_to(aim_ref[:, sl], (8, LBW))
            def step(r, carry, sl=sl, ar=ar, ai=ai):
                sr, si = carry
                nr = ar * sr - ai * si + sre[_rows8(r), sl]
                ni = ar * si + ai * sr + sim[_rows8(r), sl]
                sre[_rows8(r), sl] = nr
                sim[_rows8(r), sl] = ni
                return nr, ni
            lax.fori_loop(1, R, step, (sre[0:8, sl], sim[0:8, sl]))
        a_r = pwr_ref[R - 1:R, :]
        a_i = pwi_ref[R - 1:R, :]
        cr = prev_re[0:1, :]
        ci = prev_im[0:1, :]
        for seg in range(8):
            cinr[seg:seg + 1, :] = cr
            cini[seg:seg + 1, :] = ci
            er = sre[8 * (R - 1) + seg:8 * (R - 1) + seg + 1, :]
            ei = sim[8 * (R - 1) + seg:8 * (R - 1) + seg + 1, :]
            cr, ci = er + a_r * cr - a_i * ci, ei + a_r * ci + a_i * cr
        prev_re[0:1, :] = cr
        prev_im[0:1, :] = ci
        for lb in range(NS // LBW):
            sl = slice(lb * LBW, (lb + 1) * LBW)
            kr = cinr[:, sl]
            ki = cini[:, sl]
            def fix(r, carry, sl=sl, kr=kr, ki=ki):
                pr = jnp.broadcast_to(pwr_ref[pl.ds(r, 1), sl], (8, LBW))
                pi = jnp.broadcast_to(pwi_ref[pl.ds(r, 1), sl], (8, LBW))
                sre[_rows8(r), sl] = sre[_rows8(r), sl] + pr * kr - pi * ki
                sim[_rows8(r), sl] = sim[_rows8(r), sl] + pr * ki + pi * kr
                return carry
            lax.fori_loop(0, R, fix, 0, unroll=2)
        yp = []
        for blk in range(4):
            sr = sre[:, 512 * blk:512 * (blk + 1)].astype(bf16)
            si = sim[:, 512 * blk:512 * (blk + 1)].astype(bf16)
            yp.append(_dot(sr, cre_ref[blk]) - _dot(si, cim_ref[blk]))
        y0 = jnp.concatenate(yp, axis=1) + d_ref[...] * u.astype(f32)
        y0_ref[...] = y0
        _, y1 = _gelu_parts(y0)
        glu = _dot(y1.astype(bf16), wg_ref[...]) + bg_ref[...]
        y2 = y1 * _sig(glu)
        zs = zs_ref[...].astype(f32)
        bin_ref[...] = (y2 * (zs * _sig(zs))).astype(bf16)

    return pl.pallas_call(
        body, grid=(nc,),
        in_specs=[pl.BlockSpec((TC, SW), lambda c: (c, 6)), pl.BlockSpec((TC, SW), lambda c: (c, 7)),
                  _full((4, 128, 512)), _full((4, 128, 512)), _full((4, 512, 128)), _full((4, 512, 128)),
                  _full((1, NS)), _full((1, NS)), _full((R, NS)), _full((R, NS)),
                  _full((1, SW)), _full((SW, SW)), _full((1, SW))],
        out_specs=[pl.BlockSpec((TC, SW), lambda c: (c, 0)), pl.BlockSpec((TC, SW), lambda c: (c, 0)),
                   pl.BlockSpec((TC, NS), lambda c: (c, 0)), pl.BlockSpec((TC, NS), lambda c: (c, 0)),
                   pl.BlockSpec((8, NS), lambda c: (c, 0)), pl.BlockSpec((8, NS), lambda c: (c, 0))],
        out_shape=[jax.ShapeDtypeStruct((L, SW), f32), jax.ShapeDtypeStruct((L, SW), bf16),
                   jax.ShapeDtypeStruct((L, NS), f32), jax.ShapeDtypeStruct((L, NS), f32),
                   jax.ShapeDtypeStruct((nc * 8, NS), f32), jax.ShapeDtypeStruct((nc * 8, NS), f32)],
        scratch_shapes=[pltpu.VMEM((8, NS), f32), pltpu.VMEM((8, NS), f32)],
        name="ssm_fwd", compiler_params=_cp("arbitrary"))(
            proj, proj, bbt_re, bbt_im, ct_re, ct_im, a_re, a_im, apow_re, apow_im, dvec, wglu, bglu)


def _tail(a_in, b_in, proj, x, tgt, wco, wso, wout, gpost):
    L = x.shape[0]
    tm = 512
    def body(a_ref, b_ref, gc_ref, gs_ref, x_ref, t_ref, wco_ref, wso_ref, wout_ref, gp_ref,
             gx_ref, dain_ref, dbin_ref, dp_ref, dwout_ref, dwco_ref, dwso_ref, dgp_ref, loss_ref):
        @pl.when(pl.program_id(0) == 0)
        def _():
            dwout_ref[...] = jnp.zeros_like(dwout_ref)
            dwco_ref[...] = jnp.zeros_like(dwco_ref)
            dwso_ref[...] = jnp.zeros_like(dwso_ref)
            dgp_ref[...] = jnp.zeros_like(dgp_ref)
            loss_ref[...] = jnp.zeros_like(loss_ref)
        a = a_ref[...]
        b = b_ref[...]
        co = _dot(a, wco_ref[...])
        so = jnp.concatenate([_dot(b, wso_ref[j]) for j in range(NCHIP)], axis=1)
        sc = _sig(gc_ref[...].astype(f32))
        ss = _sig(gs_ref[...].astype(f32))
        mb = (sc * co + ss * so).astype(bf16)
        out = _dot(mb, wout_ref[...])
        r2 = lax.rsqrt(jnp.mean(out * out, axis=-1, keepdims=True) + RMS_EPS)
        on = out * r2
        gp = gp_ref[...]
        e = x_ref[...] + on * gp - t_ref[...]
        loss_ref[...] += (0.5 / D) * jnp.sum(e * e)
        dy = e * (1.0 / D)
        gx_ref[...] = dy
        dgp_ref[...] += jnp.sum(dy * on, axis=0, keepdims=True)
        dn = dy * gp
        dout = (r2 * (dn - on * jnp.mean(dn * on, axis=-1, keepdims=True))).astype(bf16)
        dwout_ref[...] += _dot_tn(mb, dout)
        dm = _dot_nt(dout, wout_ref[...])
        dp_ref[:, 0:D] = (dm * co * sc * (1.0 - sc)).astype(bf16)
        dp_ref[:, D:2 * D] = (dm * so * ss * (1.0 - ss)).astype(bf16)
        dco = (dm * sc).astype(bf16)
        dso = (dm * ss).astype(bf16)
        dwco_ref[...] += _dot_tn(a, dco)
        dbin = None
        for j in range(NCHIP):
            dso_j = dso[:, j * 256:(j + 1) * 256]
            dwso_ref[j] += _dot_tn(b, dso_j)
            t = _dot_nt(dso_j, wso_ref[j])
            dbin = t if dbin is None else dbin + t
        dain_ref[...] = _dot_nt(dco, wco_ref[...]).astype(bf16)
        dbin_ref[...] = dbin.astype(bf16)

    row = lambda w: pl.BlockSpec((tm, w), lambda i: (i, 0))
    one = lambda shape: pl.BlockSpec(shape, lambda i: (0,) * len(shape), pipeline_mode=pl.Buffered(1))
    return pl.pallas_call(
        body, grid=(L // tm,),
        in_specs=[row(CW), row(SW), pl.BlockSpec((tm, D), lambda i: (i, 4)), pl.BlockSpec((tm, D), lambda i: (i, 5)),
                  row(D), row(D), one((CW, D)), one((NCHIP, SW, 256)), one((D, D)), one((1, D))],
        out_specs=[row(D), row(CW), row(SW), pl.BlockSpec((tm, 2 * D), lambda i: (i, 2)),
                   one((D, D)), one((CW, D)), one((NCHIP, SW, 256)), one((1, D)), one((1, 128))],
        out_shape=[jax.ShapeDtypeStruct((L, D), f32), jax.ShapeDtypeStruct((L, CW), bf16),
                   jax.ShapeDtypeStruct((L, SW), bf16), jax.ShapeDtypeStruct((L, IN_W), bf16),
                   jax.ShapeDtypeStruct((D, D), f32), jax.ShapeDtypeStruct((CW, D), f32),
                   jax.ShapeDtypeStruct((NCHIP, SW, 256), f32), jax.ShapeDtypeStruct((1, D), f32),
                   jax.ShapeDtypeStruct((1, 128), f32)],
        name="tail", compiler_params=_cp("arbitrary"))(a_in, b_in, proj, proj, x, tgt, wco, wso, wout, gpost)


def _ssm_bwd(d_bin, y0, proj, sre, sim, cinr, cini, bbt_re, bbt_im, ct_re, ct_im,
             a_re, a_im, apow_re, apow_im, dvec, wglu, bglu, dproj):
    L = y0.shape[0]
    nc = L // TC
    def body(dbin_ref, y0_ref, u_ref, zs_ref, sre_ref, sim_ref, cinr_ref, cini_ref,
             bre_ref, bim_ref, cre_ref, cim_ref, are_ref, aim_ref, pwr_ref, pwi_ref, d_ref, wg_ref, bg_ref, _,
             dp_ref, dbre_ref, dbim_ref, dcre_ref, dcim_ref, dd_ref, dar_ref, dai_ref, dwg_ref, dbg_ref,
             gre, gim, gcr, gci, nxt_re, nxt_im):
        @pl.when(pl.program_id(0) == 0)
        def _():
            for ref in (dbre_ref, dbim_ref, dcre_ref, dcim_ref, dd_ref, dar_ref, dai_ref, dwg_ref, dbg_ref,
                        nxt_re, nxt_im):
                ref[...] = jnp.zeros_like(ref)
        y0 = y0_ref[...]
        u = u_ref[...]
        zs = zs_ref[...].astype(f32)
        dbin = dbin_ref[...].astype(f32)
        t, y1 = _gelu_parts(y0)
        y1b = y1.astype(bf16)
        sg = _sig(_dot(y1b, wg_ref[...]) + bg_ref[...])
        sz = _sig(zs)
        d_y2 = dbin * (zs * sz)
        dp_ref[:, SW:2 * SW] = (dbin * (y1 * sg) * (sz * (1.0 + zs * (1.0 - sz)))).astype(bf16)
        d_glu = d_y2 * y1 * sg * (1.0 - sg)
        d_glub = d_glu.astype(bf16)
        d_y1 = d_y2 * sg + _dot_nt(d_glub, wg_ref[...])
        dwg_ref[...] += _dot_tn(y1b, d_glub)
        dbg_ref[...] += jnp.sum(d_glu, axis=0, keepdims=True)
        dgelu = 0.5 * (1.0 + t) + 0.5 * y0 * (1.0 - t * t) * GELU_K0 * (1.0 + 3.0 * GELU_K1 * y0 * y0)
        d_y0 = d_y1 * dgelu
        dd_ref[...] += jnp.sum(d_y0 * u.astype(f32), axis=0, keepdims=True)
        dyb = d_y0.astype(bf16)
        for blk in range(4):
            dy1 = dyb[:, 128 * blk:128 * (blk + 1)]
            gre[:, 512 * blk:512 * (blk + 1)] = _dot_nt(dy1, cre_ref[blk])
            gim[:, 512 * blk:512 * (blk + 1)] = -_dot_nt(dy1, cim_ref[blk])
        for lb in range(NS // LBW):
            sl = slice(lb * LBW, (lb + 1) * LBW)
            ar = jnp.broadcast_to(are_ref[:, sl], (8, LBW))
            ai = jnp.broadcast_to(aim_ref[:, sl], (8, LBW))
            def step(k, carry, sl=sl, ar=ar, ai=ai):
                gr, gi = carry
                row = _rows8(R - 2 - k)
                nr = ar * gr + ai * gi + gre[row, sl]
                ni = ar * gi - ai * gr + gim[row, sl]
                gre[row, sl] = nr
                gim[row, sl] = ni
                return nr, ni
            lax.fori_loop(0, R - 1, step, (gre[8 * (R - 1):8 * R, sl], gim[8 * (R - 1):8 * R, sl]))
        a_r = pwr_ref[R - 1:R, :]
        a_i = pwi_ref[R - 1:R, :]
        cr = nxt_re[0:1, :]
        ci = nxt_im[0:1, :]
        for seg in range(7, -1, -1):
            gcr[seg:seg + 1, :] = cr
            gci[seg:seg + 1, :] = ci
            er = gre[seg:seg + 1, :]
            ei = gim[seg:seg + 1, :]
            cr, ci = er + a_r * cr + a_i * ci, ei + a_r * ci - a_i * cr
        nxt_re[0:1, :] = cr
        nxt_im[0:1, :] = ci
        for lb in range(NS // LBW):
            sl = slice(lb * LBW, (lb + 1) * LBW)
            kr = gcr[:, sl]
            ki = gci[:, sl]
            def fixed(rows, prow, sl=sl, kr=kr, ki=ki):
                pr = jnp.broadcast_to(pwr_ref[prow, sl], (8, LBW))
                pi = jnp.broadcast_to(pwi_ref[prow, sl], (8, LBW))
                gr = gre[rows, sl] + pr * kr + pi * ki
                gi = gim[rows, sl] + pr * ki - pi * kr
                gre[rows, sl] = gr
                gim[rows, sl] = gi
                return gr, gi
            g0r, g0i = fixed(slice(0, 8), slice(R - 1, R))
            p0r, p0i = cinr_ref[:, sl], cini_ref[:, sl]
            acc0 = (g0r * p0r + g0i * p0i, g0i * p0r - g0r * p0i)
            def dacc(r, carry, sl=sl, fixed=fixed):
                xr, xi = carry
                gr, gi = fixed(_rows8(r), pl.ds(R - 1 - r, 1))
                pr, pi = sre_ref[_rows8(r - 1), sl], sim_ref[_rows8(r - 1), sl]
                return xr + gr * pr + gi * pi, xi + gi * pr - gr * pi
            xr, xi = lax.fori_loop(1, R, dacc, acc0)
            dar_ref[:, sl] += xr
            dai_ref[:, sl] += xi
        dup = []
        for blk in range(4):
            s4 = slice(512 * blk, 512 * (blk + 1))
            s1 = slice(128 * blk, 128 * (blk + 1))
            grb = gre[:, s4].astype(bf16)
            gib = gim[:, s4].astype(bf16)
            dup.append(_dot_nt(grb, bre_ref[blk]) + _dot_nt(gib, bim_ref[blk]))
            dbre_ref[blk] += _dot_tn(u[:, s1], grb)
            dbim_ref[blk] += _dot_tn(u[:, s1], gib)
            dcre_ref[blk] += _dot_tn(dyb[:, s1], sre_ref[:, s4].astype(bf16))
            dcim_ref[blk] -= _dot_tn(dyb[:, s1], sim_ref[:, s4].astype(bf16))
        dp_ref[:, 0:SW] = (jnp.concatenate(dup, axis=1) + d_ref[...] * d_y0).astype(bf16)

    rev = lambda w, cidx: pl.BlockSpec((TC, w), lambda i, cidx=cidx: (nc - 1 - i, cidx))
    one = lambda shape: pl.BlockSpec(shape, lambda i: (0,) * len(shape))
    return pl.pallas_call(
        body, grid=(nc,),
        in_specs=[rev(SW, 0), rev(SW, 0), rev(SW, 6), rev(SW, 7), rev(NS, 0), rev(NS, 0),
                  pl.BlockSpec((8, NS), lambda i: (nc - 1 - i, 0)), pl.BlockSpec((8, NS), lambda i: (nc - 1 - i, 0)),
                  one((4, 128, 512)), one((4, 128, 512)), one((4, 512, 128)), one((4, 512, 128)),
                  one((1, NS)), one((1, NS)), one((R, NS)), one((R, NS)),
                  one((1, SW)), one((SW, SW)), one((1, SW)), _ANY],
        out_specs=[pl.BlockSpec((TC, 2 * SW), lambda i: (nc - 1 - i, 3)),
                   one((4, 128, 512)), one((4, 128, 512)), one((4, 128, 512)), one((4, 128, 512)),
                   one((1, SW)), one((8, NS)), one((8, NS)), one((SW, SW)), one((1, SW))],
        out_shape=[jax.ShapeDtypeStruct((L, IN_W), bf16),
                   jax.ShapeDtypeStruct((4, 128, 512), f32), jax.ShapeDtypeStruct((4, 128, 512), f32),
                   jax.ShapeDtypeStruct((4, 128, 512), f32), jax.ShapeDtypeStruct((4, 128, 512), f32),
                   jax.ShapeDtypeStruct((1, SW), f32), jax.ShapeDtypeStruct((8, NS), f32),
                   jax.ShapeDtypeStruct((8, NS), f32), jax.ShapeDtypeStruct((SW, SW), f32),
                   jax.ShapeDtypeStruct((1, SW), f32)],
        scratch_shapes=[pltpu.VMEM((TC, NS), f32), pltpu.VMEM((TC, NS), f32), pltpu.VMEM((8, NS), f32),
                        pltpu.VMEM((8, NS), f32), pltpu.VMEM((8, NS), f32), pltpu.VMEM((8, NS), f32)],
        input_output_aliases={19: 0},
        name="ssm_bwd", compiler_params=_cp("arbitrary"))(
            d_bin, y0, proj, proj, sre, sim, cinr, cini, bbt_re, bbt_im, ct_re, ct_im,
            a_re, a_im, apow_re, apow_im, dvec, wglu, bglu, dproj)


def _conv_bwd(d_ain, cu1, proj, cw, lng, lnb, dproj):
    L = cu1.shape[0]
    nc = L // TC
    def body(dain_ref, cu1_ref, ca_ref, cb_ref, zc_ref, cah_ref, cbh_ref, w_ref, g_ref, bb_ref, _,
             dp_ref, dw_ref, dbias_ref, dlng_ref, dlnb_ref, dbuf, ebuf, prev, nxt, dcu0):
        i = pl.program_id(0)
        @pl.when(i == 0)
        def _():
            dw_ref[...] = jnp.zeros_like(dw_ref)
            dbias_ref[...] = jnp.zeros_like(dbias_ref)
            dlng_ref[...] = jnp.zeros_like(dlng_ref)
            dlnb_ref[...] = jnp.zeros_like(dlnb_ref)
            nxt[...] = jnp.zeros_like(nxt)
        def lnb(s, carry):
            rows = pl.ds(pl.multiple_of(s * 32, 32), 32)
            dain = dain_ref[rows, :].astype(f32)
            c1 = cu1_ref[rows, :].astype(f32)
            zc = zc_ref[rows, :].astype(f32)
            xc = c1 - jnp.mean(c1, axis=-1, keepdims=True)
            var = jnp.mean(xc * xc, axis=-1, keepdims=True)
            rstd = lax.rsqrt(var + LN_EPS)
            xh = xc * rstd
            ln = xh * g_ref[...] + bb_ref[...]
            sl_ = _sig(ln)
            sz = _sig(zc)
            dp_ref[rows, 2 * CW:3 * CW] = (dain * (ln * sl_) * (sz * (1.0 + zc * (1.0 - sz)))).astype(bf16)
            d_ln = dain * (zc * sz) * (sl_ * (1.0 + ln * (1.0 - sl_)))
            dlng_ref[...] += jnp.sum(d_ln * xh, axis=0, keepdims=True)
            dlnb_ref[...] += jnp.sum(d_ln, axis=0, keepdims=True)
            dxh = d_ln * g_ref[...]
            d_c1 = rstd * (dxh - jnp.mean(dxh, axis=-1, keepdims=True)
                           - xh * jnp.mean(dxh * xh, axis=-1, keepdims=True))
            dbias_ref[...] += jnp.sum(d_c1, axis=0, keepdims=True)
            _put_blocked(dbuf, pl.multiple_of(s * 32, 32), 32, d_c1)
            _put_blocked(ebuf, pl.multiple_of(NH * 8 + s * 32, 32), 32,
                         ca_ref[rows, :].astype(f32) * _sig(cb_ref[rows, :].astype(f32)))
            return carry
        lax.fori_loop(0, TC // 32, lnb, 0, unroll=4)
        sub = lax.broadcasted_iota(jnp.int32, (8, 128), 0)
        def after(p, carry):
            for lb in range(NLB):
                cur = dbuf[lb, _rows8(p), :]
                dbuf[lb, _rows8(R + p), :] = jnp.where(sub == 7, pltpu.roll(nxt[lb, _rows8(p), :], 7, 0),
                                                       pltpu.roll(cur, 7, 0))
            return carry
        lax.fori_loop(0, NH, after, 0)
        nxt[...] = dbuf[:, 0:NH * 8, :]
        def before(s, carry):
            rows = pl.ds(pl.multiple_of(s * 64, 64), 64)
            v = cah_ref[rows, :].astype(f32) * _sig(cbh_ref[rows, :].astype(f32))
            _put_blocked(prev, pl.multiple_of(s * 64, 64), 64, jnp.where(i == nc - 1, jnp.zeros_like(v), v))
            return carry
        lax.fori_loop(0, NH * 8 // 64, before, 0)
        _fill_before(ebuf, prev)
        for lb in range(NLB):
            sl = slice(lb * 128, (lb + 1) * 128)
            wk = [jnp.broadcast_to(w_ref[k:k + 1, sl], (8, 128)) for k in range(KS)]
            def tap(q, carry, lb=lb, wk=wk):
                r = q * RPI
                for j, o in enumerate(_fir(dbuf, lb, r, wk, None, True)):
                    dcu0[lb, _rows8(r + j), :] = o
                return carry
            lax.fori_loop(0, R // RPI, tap, 0)
            def wgrad(q, accs, lb=lb):
                r = q * RPI
                dvs = dbuf[lb, pl.ds(pl.multiple_of(r * 8, 8), RPI * 8), :]
                win = ebuf[lb, pl.ds(pl.multiple_of((r + (NH - KS + 1)) * 8, 8), (KS + RPI - 1) * 8), :]
                accs = list(accs)
                for j in range(RPI):
                    dv = dvs[8 * j:8 * j + 8, :]
                    for k in range(KS):
                        accs[k] = accs[k] + dv * win[8 * (j + k):8 * (j + k) + 8, :]
                return tuple(accs)
            accs = lax.fori_loop(0, R // RPI, wgrad, tuple(jnp.zeros((8, 128), f32) for _ in range(KS)))
            for k in range(KS):
                dw_ref[k, :, sl] += accs[k]
        def glub(s, carry):
            rows = pl.ds(pl.multiple_of(s * 64, 64), 64)
            d0 = _get_blocked(dcu0, pl.multiple_of(s * 64, 64), 64)
            ca = ca_ref[rows, :].astype(f32)
            sb = _sig(cb_ref[rows, :].astype(f32))
            dp_ref[rows, 0:CW] = (d0 * sb).astype(bf16)
            dp_ref[rows, CW:2 * CW] = (d0 * ca * sb * (1.0 - sb)).astype(bf16)
            return carry
        lax.fori_loop(0, TC // 64, glub, 0)

    hrows = NH * 8
    per = TC // hrows
    rev = lambda cidx: pl.BlockSpec((TC, CW), lambda i, cidx=cidx: (nc - 1 - i, cidx))
    halo = lambda cidx: pl.BlockSpec((hrows, CW), lambda i, cidx=cidx: (jnp.maximum((nc - 1 - i) * per - 1, 0), cidx))
    one = lambda shape: pl.BlockSpec(shape, lambda i: (0,) * len(shape))
    return pl.pallas_call(
        body, grid=(nc,),
        in_specs=[rev(0), rev(0), rev(0), rev(1), rev(2), halo(0), halo(1), one((32, CW)), one((1, CW)), one((1, CW)),
                  _ANY],
        out_specs=[pl.BlockSpec((TC, 3 * CW), lambda i: (nc - 1 - i, 0)), one((32, 8, CW)), one((1, CW)), one((1, CW)), one((1, CW))],
        out_shape=[jax.ShapeDtypeStruct((L, IN_W), bf16), jax.ShapeDtypeStruct((32, 8, CW), f32),
                   jax.ShapeDtypeStruct((1, CW), f32), jax.ShapeDtypeStruct((1, CW), f32),
                   jax.ShapeDtypeStruct((1, CW), f32)],
        scratch_shapes=[pltpu.VMEM((NLB, (R + NH) * 8, 128), f32), pltpu.VMEM((NLB, (NH + R) * 8, 128), f32),
                        pltpu.VMEM((NLB, hrows, 128), f32), pltpu.VMEM((NLB, hrows, 128), f32),
                        pltpu.VMEM((NLB, TC, 128), f32)],
        input_output_aliases={10: 0},
        name="conv_bwd", compiler_params=_cp("arbitrary"))(d_ain, cu1, proj, proj, proj, proj, proj, cw, lng, lnb, dproj)


def _win_grad(h, dproj):
    L = h.shape[0]
    tm = min(1024, L)
    nt = L // tm
    def body(h_ref, d_ref, o_ref, acc):
        i = pl.program_id(1)
        @pl.when(i == 0)
        def _():
            acc[...] = jnp.zeros_like(acc)
        acc[...] += _dot_tn(h_ref[...], d_ref[...])
        @pl.when(i == nt - 1)
        def _():
            o_ref[0] = acc[...].astype(bf16)
    return pl.pallas_call(
        body, grid=(NCHIP, nt),
        in_specs=[pl.BlockSpec((tm, D), lambda j, i: (i, 0)), pl.BlockSpec((tm, SHARD_W), lambda j, i: (i, j))],
        out_specs=pl.BlockSpec((1, D, SHARD_W), lambda j, i: (j, 0, 0)),
        out_shape=jax.ShapeDtypeStruct((NCHIP, D, SHARD_W), bf16),
        scratch_shapes=[pltpu.VMEM((D, SHARD_W), f32)],
        name="win_grad", compiler_params=_cp("arbitrary", "arbitrary"))(h, dproj)


def _adamw_math(w, g, m, v):
    m2 = B1 * m + (1.0 - B1) * g
    v2 = B2 * v + (1.0 - B2) * (g * g)
    m_hat = m2 / (1.0 - B1 ** STEP)
    v_hat = v2 / (1.0 - B2 ** STEP)
    delta = -LR * (m_hat / (jnp.sqrt(v_hat) + EPS) + WD * w)
    return delta, m2, v2


def _adamw(name, w, g, m, v):
    rows, cols = w.shape
    tm = rows if rows <= 256 else (256 if rows % 256 == 0 else 128)
    assert rows % tm == 0
    def body(w_ref, g_ref, m_ref, v_ref, d_ref, m2_ref, v2_ref):
        d, m2, v2 = _adamw_math(w_ref[...], g_ref[...], m_ref[...], v_ref[...])
        d_ref[...] = d
        m2_ref[...] = m2
        v2_ref[...] = v2
    spec = pl.BlockSpec((tm, cols), lambda i: (i, 0))
    shp = jax.ShapeDtypeStruct((rows, cols), f32)
    return pl.pallas_call(
        body, grid=(rows // tm,), in_specs=[spec] * 4, out_specs=[spec] * 3, out_shape=[shp] * 3,
        name=name, compiler_params=_cp("arbitrary"))(w, g, m, v)


def _adamw_group(name, ws, gs, ms, vs):
    n = len(ws)
    def body(*refs):
        for i in range(n):
            w_ref, g_ref, m_ref, v_ref = (refs[q * n + i] for q in range(4))
            d, m2, v2 = _adamw_math(w_ref[...], g_ref[...], m_ref[...], v_ref[...])
            for q, val in enumerate((d, m2, v2)):
                refs[(4 + q) * n + i][...] = val
    shapes = [jax.ShapeDtypeStruct(w.shape, f32) for w in ws]
    out = pl.pallas_call(body, out_shape=shapes * 3, name=name,
                         compiler_params=pltpu.CompilerParams(vmem_limit_bytes=VMEM_LIMIT))(*ws, *gs, *ms, *vs)
    return [(out[i], out[n + i], out[2 * n + i]) for i in range(n)]


_ANY = pl.BlockSpec(memory_space=pl.ANY)


def _chunks(rows, parts):
    step = rows // parts
    assert step * parts == rows and step % 16 == 0
    return [(i * step, step) for i in range(parts)]


def _place():
    x, y, c = lax.axis_index("x"), lax.axis_index("y"), lax.axis_index("c")
    chips = [(1 - x, y), (x, 1 - y), (1 - x, 1 - y)]
    return x, y, c, chips


def _nchunks(half, cols, itemsize):
    return 4 if half * cols * itemsize >= (1 << 20) else 1


def _segments(metas):
    segs = []
    for w, (half, cols, dt) in enumerate(metas):
        for r0, n in _chunks(half, _nchunks(half, cols, jnp.dtype(dt).itemsize)):
            segs.append((w, half, r0, n))
    return segs


def _rcopy(i, src, dst, send_sems, recv_sems, to):
    return pltpu.make_async_remote_copy(src_ref=src, dst_ref=dst, send_sem=send_sems.at[i], recv_sem=recv_sems.at[i],
                                        device_id=to, device_id_type=MESH)


def _gather_prep(k_arr, shards, x, tgt, g_pre, perm):
    na = len(shards)
    L = x.shape[0]
    nc = L // TC
    segs = _segments([(a.shape[0] // 2, a.shape[1], a.dtype) for a in shards])
    ns = len(segs)
    def body(_, *refs):
        ins = refs[:na]
        x_ref, t_ref, g_ref, p_ref = refs[na:na + 4]
        outs = refs[na + 4:2 * na + 4]
        h_ref, xi_ref, ti_ref, proj_ref = refs[2 * na + 4:2 * na + 8]
        stages = refs[2 * na + 8:3 * na + 8]
        send_sems, recv_sems, local_sems = refs[3 * na + 8:]
        i = pl.program_id(0)
        x, y, c, chips = _place()
        k = 2 * x + y
        me, sibling = (x, y, c), (x, y, 1 - c)

        def dst(w, half, chip, pc, r0, n):
            return outs[w].at[chip, pl.ds(pc * half + r0, n), :]

        def firsts():
            return [_rcopy(j * ns + s, ins[w].at[pl.ds(c * half + r0, n), :], dst(w, half, k, c, r0, n),
                           send_sems, recv_sems, (*chip, c))
                    for j, chip in enumerate(chips) for s, (w, half, r0, n) in enumerate(segs)]

        def own_out(w):
            return pltpu.make_async_copy(stages[w], outs[w].at[k], local_sems.at[w])

        @pl.when(i == 0)
        def _():
            for cp in firsts():
                cp.start()
            cins = [pltpu.make_async_copy(ins[w], stages[w], local_sems.at[w]) for w in range(na)]
            for cp in cins:
                cp.start()
            for w in range(na):
                cins[w].wait()
                own_out(w).start()

        p = p_ref[...]
        def through(v):
            hi = v.astype(bf16)
            r1 = v - hi.astype(f32)
            mid = r1.astype(bf16)
            lo = (r1 - mid.astype(f32)).astype(bf16)
            return (_dot(p, hi) + _dot(p, mid)) + _dot(p, lo)
        xt = x_ref[...]
        r = lax.rsqrt(jnp.mean(xt * xt, axis=-1, keepdims=True) + RMS_EPS)
        hp = _dot(p, (xt * r * g_ref[...]).astype(bf16)).astype(bf16)
        h_ref[...] = hp
        proj_ref[...] = _dot(hp, stages[0][...]).astype(bf16)
        xi_ref[...] = through(xt)
        ti_ref[...] = through(t_ref[...])

        @pl.when(i == nc - 1)
        def _():
            passed = []
            for j, chip in enumerate(chips):
                cj = 2 * chip[0] + chip[1]
                for s, (w, half, r0, n) in enumerate(segs):
                    landed = dst(w, half, cj, c, r0, n)
                    _rcopy(j * ns + s, landed, landed, send_sems, recv_sems, me).wait_recv()
                    fwd = _rcopy(3 * ns + j * ns + s, landed, landed, send_sems, recv_sems, sibling)
                    fwd.start()
                    passed.append(fwd)
            for j, chip in enumerate(chips):
                cj = 2 * chip[0] + chip[1]
                for s, (w, half, r0, n) in enumerate(segs):
                    theirs = dst(w, half, cj, 1 - c, r0, n)
                    _rcopy(3 * ns + j * ns + s, theirs, theirs, send_sems, recv_sems, me).wait_recv()
            for cp in firsts() + passed:
                cp.wait_send()
            for w in range(na):
                own_out(w).wait()

    row = lambda: pl.BlockSpec((TC, D), lambda i, k: (i, 0))
    grid_spec = pltpu.PrefetchScalarGridSpec(
        num_scalar_prefetch=1, grid=(nc,),
        in_specs=[_ANY] * na + [row(), row(), pl.BlockSpec((1, D), lambda i, k: (0, 0)),
                                pl.BlockSpec((TC, TC), lambda i, k: (0, 0))],
        out_specs=[_ANY] * na + [row(), row(), row(), pl.BlockSpec((TC, SHARD_W), lambda i, k: (i, k[0]))],
        scratch_shapes=[pltpu.VMEM(a.shape, a.dtype) for a in shards]
        + [pltpu.SemaphoreType.DMA((6 * ns,)), pltpu.SemaphoreType.DMA((6 * ns,)), pltpu.SemaphoreType.DMA((na,))])
    return pl.pallas_call(
        body, grid_spec=grid_spec,
        out_shape=[jax.ShapeDtypeStruct((NCHIP,) + a.shape, a.dtype) for a in shards]
        + [jax.ShapeDtypeStruct((L, D), bf16), jax.ShapeDtypeStruct((L, D), f32), jax.ShapeDtypeStruct((L, D), f32),
           jax.ShapeDtypeStruct((L, IN_W), bf16)],
        name="gather_prep", compiler_params=_cp("arbitrary"))(k_arr, *shards, x, tgt, g_pre, perm)


def _x_grad_exchange(dproj, w_in, x, gx0, g_pre, parts, small):
    L = x.shape[0]
    tm = 512
    nt = L // tm
    na = len(parts)
    segs = _segments([(p.shape[1], p.shape[2], p.dtype) for p in parts])
    ns = len(segs) + 1
    def body(*refs):
        d_ref, w_ref, x_ref, gx_ref, g_ref = refs[:5]
        ins, s_ref = refs[5:5 + na], refs[5 + na]
        o_ref, dg_ref = refs[6 + na:8 + na]
        outs, qs_ref = refs[8 + na:8 + 2 * na], refs[8 + 2 * na]
        stages = refs[9 + 2 * na:10 + 3 * na]
        send_sems, recv_sems, local_sems = refs[10 + 3 * na:]
        i = pl.program_id(0)
        x, y, c, chips = _place()
        k = 2 * x + y

        def copies():
            out = []
            for j, chip in enumerate(chips):
                cj = 2 * chip[0] + chip[1]
                pieces = [(s_ref, qs_ref.at[k])]
                pieces += [(ins[w].at[cj, pl.ds(r0, n), :], outs[w].at[k, pl.ds(r0, n), :]) for w, _, r0, n in segs]
                out += [_rcopy(ns * j + s, src, d, send_sems, recv_sems, (*chip, c)) for s, (src, d) in enumerate(pieces)]
            return out

        def own_out(w):
            dst = qs_ref.at[k] if w == na else outs[w].at[k]
            return pltpu.make_async_copy(stages[w], dst, local_sems.at[w])

        @pl.when(i == 0)
        def _():
            dg_ref[...] = jnp.zeros_like(dg_ref)
            for cp in copies():
                cp.start()
            cins = [pltpu.make_async_copy(s_ref if w == na else ins[w].at[k], stages[w], local_sems.at[w])
                    for w in range(na + 1)]
            for cp in cins:
                cp.start()
            for w in range(na + 1):
                cins[w].wait()
                own_out(w).start()

        dh = _dot_nt(d_ref[:, 0:SHARD_W], w_ref[0])
        for j in range(1, NCHIP):
            dh = dh + _dot_nt(d_ref[:, j * SHARD_W:(j + 1) * SHARD_W], w_ref[j])
        xt = x_ref[...]
        r = lax.rsqrt(jnp.mean(xt * xt, axis=-1, keepdims=True) + RMS_EPS)
        xn = xt * r
        dg_ref[...] += jnp.sum(dh * xn, axis=0, keepdims=True)
        dxn = dh * g_ref[...]
        o_ref[...] = gx_ref[...] + r * (dxn - xn * jnp.mean(dxn * xn, axis=-1, keepdims=True))

        @pl.when(i == nt - 1)
        def _():
            for cp in copies():
                cp.wait_recv()
            for cp in copies():
                cp.wait_send()
            for w in range(na + 1):
                own_out(w).wait()

    return pl.pallas_call(
        body, grid=(nt,),
        in_specs=[pl.BlockSpec((tm, IN_W), lambda i: (i, 0)),
                  pl.BlockSpec((NCHIP, D, SHARD_W), lambda i: (0, 0, 0), pipeline_mode=pl.Buffered(1)),
                  pl.BlockSpec((tm, D), lambda i: (i, 0)), pl.BlockSpec((tm, D), lambda i: (i, 0)), _full((1, D))]
        + [_ANY] * (na + 1),
        out_specs=[pl.BlockSpec((tm, D), lambda i: (i, 0)), _full((1, D))] + [_ANY] * (na + 1),
        out_shape=[jax.ShapeDtypeStruct((L, D), f32), jax.ShapeDtypeStruct((1, D), f32)]
        + [jax.ShapeDtypeStruct(p.shape, bf16) for p in parts] + [jax.ShapeDtypeStruct((NCHIP, SMALL_ROWS, 128), f32)],
        scratch_shapes=[pltpu.VMEM(p.shape[1:], bf16) for p in parts] + [pltpu.VMEM((SMALL_ROWS, 128), f32)]
        + [pltpu.SemaphoreType.DMA((3 * ns,)), pltpu.SemaphoreType.DMA((3 * ns,)), pltpu.SemaphoreType.DMA((na + 1,))],
        name="x_grad_exchange", compiler_params=_cp("arbitrary"))(dproj, w_in, x, gx0, g_pre, *parts, small)


def _sibling_join_list(halves):
    na = len(halves)
    segs = _segments([(h.shape[0], h.shape[1], h.dtype) for h in halves])
    def body(*refs):
        ins, outs, stages = refs[:na], refs[na:2 * na], refs[2 * na:3 * na]
        send_sems, recv_sems, local_sems = refs[3 * na:]
        x, y, c, _ = _place()
        copies = [_rcopy(i, ins[w].at[pl.ds(r0, n), :], outs[w].at[pl.ds(c * half + r0, n), :], send_sems, recv_sems,
                         (x, y, 1 - c)) for i, (w, half, r0, n) in enumerate(segs)]
        for cp in copies:
            cp.start()
        cins = [pltpu.make_async_copy(ins[w], stages[w], local_sems.at[w]) for w in range(na)]
        for cp in cins:
            cp.start()
        own = []
        for w in range(na):
            cins[w].wait()
            half = halves[w].shape[0]
            own.append(pltpu.make_async_copy(stages[w], outs[w].at[pl.ds(c * half, half), :], local_sems.at[w]))
            own[-1].start()
        for cp in copies:
            cp.wait_recv()
        for cp in copies:
            cp.wait_send()
        for cp in own:
            cp.wait()

    return pl.pallas_call(
        body, in_specs=[_ANY] * na, out_specs=[_ANY] * na,
        out_shape=[jax.ShapeDtypeStruct((2 * h.shape[0], h.shape[1]), f32) for h in halves],
        scratch_shapes=[pltpu.VMEM(h.shape, f32) for h in halves]
        + [pltpu.SemaphoreType.DMA((len(segs),)), pltpu.SemaphoreType.DMA((len(segs),)), pltpu.SemaphoreType.DMA((na,))],
        name="sibling_join")(*halves)


def _allgather_rows(v):
    def body(v_ref, o_ref, send_sems, recv_sems):
        x, y, c, _ = _place()
        me = 4 * x + 2 * y + c
        o_ref[me] = v_ref[...]
        copies = []
        i = 0
        for dx in range(2):
            for dy in range(2):
                for dc in range(2):
                    if dx + dy + dc:
                        copies.append(_rcopy(i, v_ref, o_ref.at[me], send_sems, recv_sems, (x ^ dx, y ^ dy, c ^ dc)))
                        i += 1
        for cp in copies:
            cp.start()
        for cp in copies:
            cp.wait_recv()
        for cp in copies:
            cp.wait_send()

    vm = pl.BlockSpec(memory_space=pltpu.VMEM)
    return pl.pallas_call(
        body, in_specs=[vm], out_specs=vm, out_shape=jax.ShapeDtypeStruct((8, 8, 128), f32),
        scratch_shapes=[pltpu.SemaphoreType.DMA((7,)), pltpu.SemaphoreType.DMA((7,))],
        name="allgather_rows")(v)


def _adamw_rows(parts, w, m, v):
    def body(p_ref, w_ref, m_ref, v_ref, g_ref, d_ref, m2_ref, v2_ref):
        g = p_ref[0]
        for dvc in range(1, 8):
            g = g + p_ref[dvc]
        g_ref[...] = g
        d, m2, v2 = _adamw_math(w_ref[...], g, m_ref[...], v_ref[...])
        d_ref[...] = d
        m2_ref[...] = m2
        v2_ref[...] = v2
    return pl.pallas_call(body, out_shape=[jax.ShapeDtypeStruct((8, 128), f32)] * 4, name="adamw_pre_norm_gain")(
        parts, w, m, v)


def _pair_exchange_list(grads, small):
    na = len(grads)
    segs = _segments([(g.shape[1] // 2, g.shape[2], g.dtype) for g in grads])
    n = NCHIP * len(segs) + 1
    def body(*refs):
        ins, s_ref, outs, rs_ref, (send_sems, recv_sems) = (refs[:na], refs[na], refs[na + 1:2 * na + 1],
                                                            refs[2 * na + 1], refs[2 * na + 2:])
        x, y, c, _ = _place()
        pieces = [(s_ref, rs_ref)]
        for j in range(NCHIP):
            for w, half, r0, rows in segs:
                pieces.append((ins[w].at[j, pl.ds((1 - c) * half + r0, rows), :], outs[w].at[j, pl.ds(r0, rows), :]))
        copies = [_rcopy(i, s, d, send_sems, recv_sems, (x, y, 1 - c)) for i, (s, d) in enumerate(pieces)]
        for cp in copies:
            cp.start()
        for cp in copies:
            cp.wait_recv()
        for cp in copies:
            cp.wait_send()

    return pl.pallas_call(
        body, in_specs=[_ANY] * (na + 1), out_specs=[_ANY] * (na + 1),
        out_shape=[jax.ShapeDtypeStruct((NCHIP, g.shape[1] // 2, g.shape[2]), g.dtype) for g in grads]
        + [jax.ShapeDtypeStruct((SMALL_ROWS, 128), f32)],
        scratch_shapes=[pltpu.SemaphoreType.DMA((n,)), pltpu.SemaphoreType.DMA((n,))],
        name="pair_exchange")(*grads, small)


def _pair_sum_list(c_arr, grads, recvs, small, rsmall):
    na = len(grads)
    def body(c_ref, *refs):
        g_refs, r_refs, s_ref, rs_ref = refs[:na], refs[na:2 * na], refs[2 * na], refs[2 * na + 1]
        o_refs, os_ref = refs[2 * na + 2:3 * na + 2], refs[3 * na + 2]
        for g_ref, r_ref, o_ref in zip(g_refs, r_refs, o_refs):
            o_ref[...] = (g_ref[...].astype(f32) + r_ref[...].astype(f32)).astype(bf16)
        os_ref[...] = s_ref[...] + rs_ref[...]
    half = lambda g: pl.BlockSpec((1, g.shape[1] // 2, g.shape[2]), lambda j, c: (j, c[0], 0))
    low = lambda g: pl.BlockSpec((1, g.shape[1] // 2, g.shape[2]), lambda j, c: (j, 0, 0))
    sm = pl.BlockSpec((SMALL_ROWS, 128), lambda j, c: (0, 0))
    grid_spec = pltpu.PrefetchScalarGridSpec(
        num_scalar_prefetch=1, grid=(NCHIP,),
        in_specs=[half(g) for g in grads] + [low(g) for g in grads] + [sm, sm],
        out_specs=[low(g) for g in grads] + [sm])
    return pl.pallas_call(
        body, grid_spec=grid_spec,
        out_shape=[jax.ShapeDtypeStruct((NCHIP, g.shape[1] // 2, g.shape[2]), bf16) for g in grads]
        + [jax.ShapeDtypeStruct((SMALL_ROWS, 128), f32)],
        name="pair_sum", compiler_params=_cp("arbitrary"))(c_arr, *grads, *recvs, small, rsmall)


def _chip_sum_list(parts, small):
    na = len(parts)
    nt = 2
    def body(*refs):
        for q_ref, f_ref in zip(refs[:na + 1], refs[na + 1:]):
            acc = q_ref[0].astype(f32)
            for j in range(1, NCHIP):
                acc = acc + q_ref[j].astype(f32)
            f_ref[...] = acc
    arrs = list(parts) + [small]
    return pl.pallas_call(
        body, grid=(nt,),
        in_specs=[pl.BlockSpec((NCHIP, a.shape[1] // nt, a.shape[2]), lambda i: (0, i, 0)) for a in arrs],
        out_specs=[pl.BlockSpec((a.shape[1] // nt, a.shape[2]), lambda i: (i, 0)) for a in arrs],
        out_shape=[jax.ShapeDtypeStruct(a.shape[1:], f32) for a in arrs],
        name="chip_sum", compiler_params=_cp("arbitrary"))(*arrs)


_SMALL =(("conv_b", (1, 1024)), ("conv_ln_gain", (1, 1024)), ("conv_ln_bias", (1, 1024)),
          ("ssm_lambda_re", (1, 32, 64)), ("ssm_lambda_im", (1, 32, 64)), ("ssm_log_dt", (1, 32)),
          ("ssm_b_re", (1, 32, 64, 16)), ("ssm_b_im", (1, 32, 64, 16)), ("ssm_c_re", (1, 32, 16, 64)),
          ("ssm_c_im", (1, 32, 16, 64)), ("ssm_d", (1, 32, 16)), ("b_ssm_glu", (1, 512)), ("post_norm_gain", (1, 1024)))


def _pack_small(vals, extra=None):
    rows = []
    for v in list(vals) + ([extra] if extra is not None else []):
        flat = v.reshape(-1).astype(f32)
        n = -(-flat.shape[0] // 1024) * 1024
        rows.append(jnp.pad(flat, (0, n - flat.shape[0])).reshape(-1, 128))
    used = sum(r.shape[0] for r in rows)
    rows.append(jnp.zeros((SMALL_ROWS - used, 128), f32))
    return jnp.concatenate(rows, axis=0)


def _unpack_small(p):
    o = 0
    out = []
    for _, shape in _SMALL:
        n = int(np.prod(shape))
        nr = -(-n // 1024) * 8
        out.append(p[o:o + nr].reshape(-1)[:n].reshape(shape))
        o += nr
    return out, p[o, 0]


def _discretize(lam_re, lam_im, log_dt, b_re, b_im):
    dt = jnp.exp(log_dt)[:, None]
    mag = jnp.exp(lam_re * dt)
    ar = mag * jnp.cos(lam_im * dt)
    ai = mag * jnp.sin(lam_im * dt)
    den = lam_re * lam_re + lam_im * lam_im
    zr = ((ar - 1.0) * lam_re + ai * lam_im) / den
    zi = (ai * lam_re - (ar - 1.0) * lam_im) / den
    bbr = zr[..., None] * b_re - zi[..., None] * b_im
    bbi = zr[..., None] * b_im + zi[..., None] * b_re
    return ar, ai, bbr, bbi


_EYE8 = np.eye(8, dtype=np.float32)


def _bbt_blocks(bb):
    v = bb.reshape(4, 8, PST, H).transpose(0, 1, 3, 2)
    return jnp.einsum("bghp,gk->bghkp", v, _EYE8).reshape(4, 128, 512)


def _bbt_unblock(m):
    v = jnp.einsum("bghkp,gk->bghp", m.reshape(4, 8, H, 8, PST), _EYE8)
    return v.transpose(0, 1, 3, 2).reshape(G, PST, H)


def _ct_blocks(cc):
    v = cc.reshape(4, 8, H, PST)
    return jnp.einsum("bghp,gk->bgpkh", v, _EYE8).reshape(4, 512, 128)


def _ct_unblock(m):
    return jnp.einsum("bghkp,gk->bghp", m.reshape(4, 8, H, 8, PST), _EYE8).reshape(G, H, PST)


def _perm_matrix():
    p = np.zeros((TC, TC), np.float32)
    for r in range(R):
        for seg in range(8):
            p[r * 8 + seg, seg * R + r] = 1.0
    return p


def _deinterleave(a):
    L, C = a.shape
    return a.reshape(L // TC, R, 8, C).transpose(0, 2, 1, 3).reshape(L, C)


def _fwd_bwd(h, xi, ti, proj, conv_w, w_co, w_glu, w_so, w_out, small):
    (conv_b, ln_g, ln_b, lam_re, lam_im, log_dt, b_re, b_im, c_re, c_im, dvec, b_glu, g_post) = small
    lam_re, lam_im, log_dt = lam_re[0], lam_im[0], log_dt[0]
    b_re, b_im, c_re, c_im = b_re[0], b_im[0], c_re[0], c_im[0]
    (ar, ai, bbr, bbi), disc_vjp = jax.vjp(_discretize, lam_re, lam_im, log_dt, b_re, b_im)
    a_re = ar.reshape(1, NS)
    a_im = ai.reshape(1, NS)
    dt = jnp.exp(log_dt)[:, None]
    steps = jnp.arange(1, R + 1, dtype=f32)[:, None, None]
    apow_re = (jnp.exp(steps * (lam_re * dt)) * jnp.cos(steps * (lam_im * dt))).reshape(R, NS)
    apow_im = (jnp.exp(steps * (lam_re * dt)) * jnp.sin(steps * (lam_im * dt))).reshape(R, NS)
    bbt_re, bbt_im = _bbt_blocks(bbr).astype(bf16), _bbt_blocks(bbi).astype(bf16)
    ct_re, ct_im = _ct_blocks(c_re).astype(bf16), _ct_blocks(c_im).astype(bf16)
    d_row = dvec.reshape(1, SW)
    cw32 = jnp.pad(conv_w, ((0, 1), (0, 0)))

    cu1, a_in = _conv_fwd(proj, cw32, conv_b, ln_g, ln_b)
    y0, b_in, sre, sim, cinr, cini = _ssm_fwd(proj, bbt_re, bbt_im, ct_re, ct_im, a_re, a_im,
                                              apow_re, apow_im, d_row, w_glu, b_glu)
    gx0, d_ain, d_bin, dproj, dw_out, dw_co, dw_so, dg_post, loss = _tail(
        a_in, b_in, proj, xi, ti, w_co, w_so, w_out, g_post)
    (dproj, dbbt_re, dbbt_im, dct_re, dct_im, dd, dar8, dai8, dw_glu, db_glu) = _ssm_bwd(
        d_bin, y0, proj, sre, sim, cinr, cini, bbt_re, bbt_im, ct_re, ct_im,
        a_re, a_im, apow_re, apow_im, d_row, w_glu, b_glu, dproj)
    dproj, dcw8, d_convb, d_lng, d_lnb = _conv_bwd(d_ain, cu1, proj, cw32, ln_g, ln_b, dproj)
    dw_in = _win_grad(h, dproj)

    d_ar = jnp.sum(dar8, axis=0).reshape(G, PST)
    d_ai = jnp.sum(dai8, axis=0).reshape(G, PST)
    d_lre, d_lim, d_ldt, d_bre, d_bim = disc_vjp((d_ar, d_ai, _bbt_unblock(dbbt_re), _bbt_unblock(dbbt_im)))
    d_conv_w = jnp.sum(dcw8, axis=1)[:KS]
    small_grads = [d_convb, d_lng, d_lnb, d_lre[None], d_lim[None], d_ldt[None], d_bre[None], d_bim[None],
                   _ct_unblock(dct_re)[None], _ct_unblock(dct_im)[None], dd.reshape(1, G, H), db_glu, dg_post]
    return loss[0, 0], gx0, dproj, (dw_in, dw_co, dw_out, dw_glu, dw_so, d_conv_w), small_grads


def kernel(x, pre_norm_gain, w_in, conv_w, conv_b, conv_ln_gain, conv_ln_bias, w_conv_out, ssm_lambda_re, ssm_lambda_im, ssm_log_dt, ssm_b_re, ssm_b_im, ssm_c_re, ssm_c_im, ssm_d, w_ssm_glu, b_ssm_glu, w_ssm_out, w_out, post_norm_gain, loss_target, m_pre_norm_gain, m_w_in, m_conv_w, m_conv_b, m_conv_ln_gain, m_conv_ln_bias, m_w_conv_out, m_ssm_lambda_re, m_ssm_lambda_im, m_ssm_log_dt, m_ssm_b_re, m_ssm_b_im, m_ssm_c_re, m_ssm_c_im, m_ssm_d, m_w_ssm_glu, m_b_ssm_glu, m_w_ssm_out, m_w_out, m_post_norm_gain, v_pre_norm_gain, v_w_in, v_conv_w, v_conv_b, v_conv_ln_gain, v_conv_ln_bias, v_w_conv_out, v_ssm_lambda_re, v_ssm_lambda_im, v_ssm_log_dt, v_ssm_b_re, v_ssm_b_im, v_ssm_c_re, v_ssm_c_im, v_ssm_d, v_w_ssm_glu, v_b_ssm_glu, v_w_ssm_out, v_w_out, v_post_norm_gain):
    c = lax.axis_index("c")
    shards = [w_in[0].astype(bf16), w_conv_out[0].astype(bf16), w_out[0].astype(bf16), w_ssm_glu[0].astype(bf16),
              w_ssm_out[0].astype(bf16), jnp.pad(conv_w[0], ((0, CONV_ROWS - KS), (0, 0)))]
    k_arr = (2 * lax.axis_index("x") + lax.axis_index("y")).astype(jnp.int32).reshape(1)
    w_in_g, w_co_g, w_out_g, w_glu_g, w_so_g, conv_w_g, h, xi, ti, proj = _gather_prep(
        k_arr, shards, x[0], loss_target[0], pre_norm_gain, jnp.asarray(_perm_matrix(), bf16))
    conv_w_f = conv_w_g[:, :KS].transpose(1, 0, 2).reshape(KS, CW)

    small = (conv_b, conv_ln_gain, conv_ln_bias, ssm_lambda_re, ssm_lambda_im, ssm_log_dt, ssm_b_re,
             ssm_b_im, ssm_c_re, ssm_c_im, ssm_d, b_ssm_glu, post_norm_gain)
    loss_part, gx0, dproj, big_grads, small_grads = _fwd_bwd(
        h, xi, ti, _proj_fwd(k_arr, h, w_in_g, proj), conv_w_f, w_co_g.reshape(CW, D), w_glu_g.reshape(SW, SW), w_so_g,
        w_out_g.reshape(D, D), small)

    dw_in, dw_co, dw_out, dw_glu, dw_so, d_conv_w = big_grads
    d_conv_w = jnp.pad(d_conv_w, ((0, CONV_ROWS - KS), (0, 0))).reshape(CONV_ROWS, NCHIP, 256).transpose(1, 0, 2)
    grads = [dw_in] + [g.astype(bf16) for g in (dw_co.reshape(NCHIP, 256, D), dw_out.reshape(NCHIP, 256, D),
                                                  dw_glu.reshape(NCHIP, 128, SW), dw_so, d_conv_w)]
    gs = _pack_small(small_grads, extra=loss_part)
    *recvs, rs = _pair_exchange_list(grads, gs)
    *parts, ps = _pair_sum_list(c.astype(jnp.int32).reshape(1), grads, recvs, gs, rs)
    gxi, dg_pre, *qparts, qs = _x_grad_exchange(dproj, w_in_g, xi, gx0, pre_norm_gain, parts, ps)
    grad_x = _deinterleave(gxi)
    *halves, fs = _chip_sum_list(qparts, qs)
    g_big = list(_sibling_join_list(halves))
    g_big[5] = g_big[5][:KS]

    big_w = (w_in[0], w_conv_out[0], w_out[0], w_ssm_glu[0], w_ssm_out[0], conv_w[0])
    big_m = (m_w_in[0], m_w_conv_out[0], m_w_out[0], m_w_ssm_glu[0], m_w_ssm_out[0], m_conv_w[0])
    big_v = (v_w_in[0], v_w_conv_out[0], v_w_out[0], v_w_ssm_glu[0], v_w_ssm_out[0], v_conv_w[0])
    big_names = ("w_in", "w_conv_out", "w_out", "w_ssm_glu", "w_ssm_out", "conv_w")
    res = {}
    upd = [_adamw("adamw_w_in", big_w[0], g_big[0], big_m[0], big_v[0])]
    upd += _adamw_group("adamw_rest", big_w[1:], g_big[1:], big_m[1:], big_v[1:])
    for n, g, (d, m2, v2) in zip(big_names, g_big, upd):
        res[n] = (g[None], d[None], m2[None], v2[None])

    small_m = (m_conv_b, m_conv_ln_gain, m_conv_ln_bias, m_ssm_lambda_re, m_ssm_lambda_im, m_ssm_log_dt,
               m_ssm_b_re, m_ssm_b_im, m_ssm_c_re, m_ssm_c_im, m_ssm_d, m_b_ssm_glu, m_post_norm_gain)
    small_v = (v_conv_b, v_conv_ln_gain, v_conv_ln_bias, v_ssm_lambda_re, v_ssm_lambda_im, v_ssm_log_dt,
               v_ssm_b_re, v_ssm_b_im, v_ssm_c_re, v_ssm_c_im, v_ssm_d, v_b_ssm_glu, v_post_norm_gain)
    sd, sm, sv = _adamw("adamw_small", _pack_small(small), fs, _pack_small(small_m), _pack_small(small_v))
    sg_l, loss = _unpack_small(fs)
    sd_l, _ = _unpack_small(sd)
    sm_l, _ = _unpack_small(sm)
    sv_l, _ = _unpack_small(sv)
    for i, (n, _) in enumerate(_SMALL):
        res[n] = (sg_l[i], sd_l[i], sm_l[i], sv_l[i])
    rows = lambda a: a.reshape(8, 128)
    pre = _adamw_rows(_allgather_rows(rows(dg_pre)), rows(pre_norm_gain), rows(m_pre_norm_gain), rows(v_pre_norm_gain))
    res["pre_norm_gain"] = tuple(a.reshape(1, D) for a in pre)

    order = ("pre_norm_gain", "w_in", "conv_w", "conv_b", "conv_ln_gain", "conv_ln_bias", "w_conv_out", "ssm_lambda_re",
             "ssm_lambda_im", "ssm_log_dt", "ssm_b_re", "ssm_b_im", "ssm_c_re", "ssm_c_im", "ssm_d", "w_ssm_glu",
             "b_ssm_glu", "w_ssm_out", "w_out", "post_norm_gain")
    outs = [loss, grad_x[None]]
    for q in range(4):
        outs.extend(res[n][q] for n in order)
    return tuple(outs)
```

```python
import math

import numpy as np
import jax
import jax.numpy as jnp
from jax import lax
from jax.experimental import pallas as pl
from jax.experimental.pallas import tpu as pltpu

f32 = jnp.float32
bf16 = jnp.bfloat16

D = 1024
CW = 1024
SW = 512
G = 32
H = 16
PST = 64
NS = G * PST
KS = 31
IN_W = 6144
NCHIP = 4
SHARD_W = IN_W // NCHIP
RMS_EPS = 1e-6
LN_EPS = 1e-5
LR, B1, B2, EPS, WD, STEP = 0.001, 0.9, 0.999, 1e-08, 0.01, 10
GELU_K0 = math.sqrt(2.0 / math.pi)
GELU_K1 = 0.044715

TC = 512
R = TC // 8
NH = 32
LBW = 1024
CONV_ROWS = 64
SMALL_ROWS = 1152
VMEM_LIMIT = 56 * 1024 * 1024
MESH = pl.DeviceIdType.MESH


def _cp(*sem):
    return pltpu.CompilerParams(dimension_semantics=tuple(sem), vmem_limit_bytes=VMEM_LIMIT)


def _sig(v):
    return 0.5 * jnp.tanh(0.5 * v) + 0.5


def _dot(a, b):
    return jnp.dot(a, b, preferred_element_type=f32)


def _dot_nt(a, b):
    return lax.dot_general(a, b, (((1,), (1,)), ((), ())), preferred_element_type=f32)


def _dot_tn(a, b):
    return lax.dot_general(a, b, (((0,), (0,)), ((), ())), preferred_element_type=f32)


def _full(shape):
    nd = len(shape)
    return pl.BlockSpec(shape, lambda *_: (0,) * nd)


def _rows8(i):
    return pl.ds(pl.multiple_of(i * 8, 8), 8)


def _proj_fwd(k_arr, h, w_in, proj):
    L = h.shape[0]
    tm = min(1024, L)
    def body(_, h_ref, w_ref, __, o_ref):
        o_ref[...] = _dot(h_ref[...], w_ref[0]).astype(bf16)
    shard = lambda j, k: (k[0] + 1 + j) % NCHIP
    grid_spec = pltpu.PrefetchScalarGridSpec(
        num_scalar_prefetch=1, grid=(NCHIP - 1, L // tm),
        in_specs=[pl.BlockSpec((tm, D), lambda j, i, k: (i, 0)),
                  pl.BlockSpec((1, D, SHARD_W), lambda j, i, k: (shard(j, k), 0, 0)), _ANY],
        out_specs=pl.BlockSpec((tm, SHARD_W), lambda j, i, k: (i, shard(j, k))))
    return pl.pallas_call(
        body, grid_spec=grid_spec, out_shape=jax.ShapeDtypeStruct((L, IN_W), bf16),
        input_output_aliases={3: 0},
        name="proj_fwd", compiler_params=_cp("arbitrary", "arbitrary"))(k_arr, h, w_in, proj)


NLB = CW // 128
RPI = 8


def _put_blocked(buf, row0, nrows, v):
    for lb in range(NLB):
        buf[lb, pl.ds(row0, nrows), :] = v[:, lb * 128:(lb + 1) * 128]


def _get_blocked(buf, row0, nrows):
    return jnp.concatenate([buf[lb, pl.ds(row0, nrows), :] for lb in range(NLB)], axis=1)


def _fill_before(ebuf, prev):
    sub = lax.broadcasted_iota(jnp.int32, (8, 128), 0)
    def halo(p, carry):
        for lb in range(NLB):
            cur = ebuf[lb, _rows8(R + p), :]
            ebuf[lb, _rows8(p), :] = jnp.where(sub == 0, pltpu.roll(prev[lb, _rows8(p), :], 1, 0),
                                               pltpu.roll(cur, 1, 0))
        return carry
    lax.fori_loop(0, NH, halo, 0)


def _fir(buf, lb, r, coef, first, flip):
    win = buf[lb, pl.ds(pl.multiple_of(r * 8, 8), (KS + RPI - 1) * 8), :]
    outs = []
    for i in range(RPI):
        acc = [first, None, None, None]
        for k in range(KS):
            o = i + ((KS - 1 - k) if flip else k)
            t = coef[k] * win[8 * o:8 * o + 8, :]
            acc[k % 4] = t if acc[k % 4] is None else acc[k % 4] + t
        outs.append((acc[0] + acc[1]) + (acc[2] + acc[3]))
    return outs


def _conv_fwd(proj, cw, cbias, lng, lnb):
    L = proj.shape[0]
    nc = L // TC
    def body(ca_ref, cb_ref, zc_ref, w_ref, b_ref, g_ref, bb_ref, cu1_ref, ain_ref, ebuf, prev, cacc):
        @pl.when(pl.program_id(0) == 0)
        def _():
            prev[...] = jnp.zeros_like(prev)
        def glu(s, carry):
            rows = pl.ds(pl.multiple_of(s * 64, 64), 64)
            _put_blocked(ebuf, pl.multiple_of(NH * 8 + s * 64, 64), 64,
                         ca_ref[rows, :].astype(f32) * _sig(cb_ref[rows, :].astype(f32)))
            return carry
        lax.fori_loop(0, TC // 64, glu, 0)
        _fill_before(ebuf, prev)
        prev[...] = ebuf[:, R * 8:(NH + R) * 8, :]
        for lb in range(NLB):
            sl = slice(lb * 128, (lb + 1) * 128)
            wk = [jnp.broadcast_to(w_ref[k:k + 1, sl], (8, 128)) for k in range(KS)]
            bias = jnp.broadcast_to(b_ref[:, sl], (8, 128))
            def tap(q, carry, lb=lb, wk=wk, bias=bias):
                r = q * RPI
                for i, o in enumerate(_fir(ebuf, lb, r + (NH - KS + 1), wk, bias, False)):
                    cacc[lb, _rows8(r + i), :] = o
                return carry
            lax.fori_loop(0, R // RPI, tap, 0)
        def norm(s, carry):
            rows = pl.ds(pl.multiple_of(s * 64, 64), 64)
            c1b = _get_blocked(cacc, pl.multiple_of(s * 64, 64), 64).astype(bf16)
            cu1_ref[rows, :] = c1b
            c1 = c1b.astype(f32)
            xc = c1 - jnp.mean(c1, axis=-1, keepdims=True)
            var = jnp.mean(xc * xc, axis=-1, keepdims=True)
            ln = xc * lax.rsqrt(var + LN_EPS) * g_ref[...] + bb_ref[...]
            zc = zc_ref[rows, :].astype(f32)
            ain_ref[rows, :] = ((ln * _sig(ln)) * (zc * _sig(zc))).astype(bf16)
            return carry
        lax.fori_loop(0, TC // 64, norm, 0, unroll=4)

    col = lambda c: pl.BlockSpec((TC, CW), lambda i, c=c: (i, c))
    return pl.pallas_call(
        body, grid=(nc,),
        in_specs=[col(0), col(1), col(2), _full((32, CW)), _full((1, CW)), _full((1, CW)), _full((1, CW))],
        out_specs=[pl.BlockSpec((TC, CW), lambda i: (i, 0)), pl.BlockSpec((TC, CW), lambda i: (i, 0))],
        out_shape=[jax.ShapeDtypeStruct((L, CW), bf16), jax.ShapeDtypeStruct((L, CW), bf16)],
        scratch_shapes=[pltpu.VMEM((NLB, (NH + R) * 8, 128), f32), pltpu.VMEM((NLB, NH * 8, 128), f32),
                        pltpu.VMEM((NLB, TC, 128), f32)],
        name="conv_fwd", compiler_params=_cp("arbitrary"))(proj, proj, proj, cw, cbias, lng, lnb)


def _gelu_parts(y0):
    t = jnp.tanh(GELU_K0 * (y0 + GELU_K1 * y0 * y0 * y0))
    return t, 0.5 * y0 * (1.0 + t)


def _ssm_fwd(proj, bbt_re, bbt_im, ct_re, ct_im, a_re, a_im, apow_re, apow_im, dvec, wglu, bglu):
    L = proj.shape[0]
    nc = L // TC
    def body(u_ref, zs_ref, bre_ref, bim_ref, cre_ref, cim_ref, are_ref, aim_ref, pwr_ref, pwi_ref,
             d_ref, wg_ref, bg_ref, y0_ref, bin_ref, sre, sim, cinr, cini, prev_re, prev_im):
        c = pl.program_id(0)
        @pl.when(c == 0)
        def _():
            prev_re[...] = jnp.zeros_like(prev_re)
            prev_im[...] = jnp.zeros_like(prev_im)
        u = u_ref[...]
        for blk in range(4):
            ub = u[:, 128 * blk:128 * (blk + 1)]
            sre[:, 512 * blk:512 * (blk + 1)] = _dot(ub, bre_ref[blk])
            sim[:, 512 * blk:512 * (blk + 1)] = _dot(ub, bim_ref[blk])
        for lb in range(NS // LBW):
            sl = slice(lb * LBW, (lb + 1) * LBW)
            ar = jnp.broadcast_to(are_ref[:, sl], (8, LBW))
            ai = jnp.broadcast_to(aim_ref[:, sl], (8, LBW))
            def step(r, carry, sl=sl, ar=ar, ai=ai):
                sr, si = carry
                nr = ar * sr - ai * si + sre[_rows8(r), sl]
                ni = ar * si + ai * sr + sim[_rows8(r), sl]
                sre[_rows8(r), sl] = nr
                sim[_rows8(r), sl] = ni
                return nr, ni
            lax.fori_loop(1, R, step, (sre[0:8, sl], sim[0:8, sl]))
        a_r = pwr_ref[R - 1:R, :]
        a_i = pwi_ref[R - 1:R, :]
        cr = prev_re[0:1, :]
        ci = prev_im[0:1, :]
        for seg in range(8):
            cinr[seg:seg + 1, :] = cr
            cini[seg:seg + 1, :] = ci
            er = sre[8 * (R - 1) + seg:8 * (R - 1) + seg + 1, :]
            ei = sim[8 * (R - 1) + seg:8 * (R - 1) + seg + 1, :]
            cr, ci = er + a_r * cr - a_i * ci, ei + a_r * ci + a_i * cr
        prev_re[0:1, :] = cr
        prev_im[0:1, :] = ci
        for lb in range(NS // LBW):
            sl = slice(lb * LBW, (lb + 1) * LBW)
            kr = cinr[:, sl]
            ki = cini[:, sl]
            def fix(r, carry, sl=sl, kr=kr, ki=ki):
                pr = jnp.broadcast_to(pwr_ref[pl.ds(r, 1), sl], (8, LBW))
                pi = jnp.broadcast_to(pwi_ref[pl.ds(r, 1), sl], (8, LBW))
                sre[_rows8(r), sl] = sre[_rows8(r), sl] + pr * kr - pi * ki
                sim[_rows8(r), sl] = sim[_rows8(r), sl] + pr * ki + pi * kr
                return carry
            lax.fori_loop(0, R, fix, 0, unroll=2)
        yp = []
        for blk in range(4):
            sr = sre[:, 512 * blk:512 * (blk + 1)].astype(bf16)
            si = sim[:, 512 * blk:512 * (blk + 1)].astype(bf16)
            yp.append(_dot(sr, cre_ref[blk]) - _dot(si, cim_ref[blk]))
        y0 = jnp.concatenate(yp, axis=1) + d_ref[...] * u.astype(f32)
        y0_ref[...] = y0
        _, y1 = _gelu_parts(y0)
        glu = _dot(y1.astype(bf16), wg_ref[...]) + bg_ref[...]
        y2 = y1 * _sig(glu)
        zs = zs_ref[...].astype(f32)
        bin_ref[...] = (y2 * (zs * _sig(zs))).astype(bf16)

    return pl.pallas_call(
        body, grid=(nc,),
        in_specs=[pl.BlockSpec((TC, SW), lambda c: (c, 6)), pl.BlockSpec((TC, SW), lambda c: (c, 7)),
                  _full((4, 128, 512)), _full((4, 128, 512)), _full((4, 512, 128)), _full((4, 512, 128)),
                  _full((1, NS)), _full((1, NS)), _full((R, NS)), _full((R, NS)),
                  _full((1, SW)), _full((SW, SW)), _full((1, SW))],
        out_specs=[pl.BlockSpec((TC, SW), lambda c: (c, 0)), pl.BlockSpec((TC, SW), lambda c: (c, 0)),
                   pl.BlockSpec((TC, NS), lambda c: (c, 0)), pl.BlockSpec((TC, NS), lambda c: (c, 0)),
                   pl.BlockSpec((8, NS), lambda c: (c, 0)), pl.BlockSpec((8, NS), lambda c: (c, 0))],
        out_shape=[jax.ShapeDtypeStruct((L, SW), f32), jax.ShapeDtypeStruct((L, SW), bf16),
                   jax.ShapeDtypeStruct((L, NS), f32), jax.ShapeDtypeStruct((L, NS), f32),
                   jax.ShapeDtypeStruct((nc * 8, NS), f32), jax.ShapeDtypeStruct((nc * 8, NS), f32)],
        scratch_shapes=[pltpu.VMEM((8, NS), f32), pltpu.VMEM((8, NS), f32)],
        name="ssm_fwd", compiler_params=_cp("arbitrary"))(
            proj, proj, bbt_re, bbt_im, ct_re, ct_im, a_re, a_im, apow_re, apow_im, dvec, wglu, bglu)


def _tail(a_in, b_in, proj, x, tgt, wco, wso, wout, gpost):
    L = x.shape[0]
    tm = 512
    def body(a_ref, b_ref, gc_ref, gs_ref, x_ref, t_ref, wco_ref, wso_ref, wout_ref, gp_ref,
             gx_ref, dain_ref, dbin_ref, dp_ref, dwout_ref, dwco_ref, dwso_ref, dgp_ref, loss_ref):
        @pl.when(pl.program_id(0) == 0)
        def _():
            dwout_ref[...] = jnp.zeros_like(dwout_ref)
            dwco_ref[...] = jnp.zeros_like(dwco_ref)
            dwso_ref[...] = jnp.zeros_like(dwso_ref)
            dgp_ref[...] = jnp.zeros_like(dgp_ref)
            loss_ref[...] = jnp.zeros_like(loss_ref)
        a = a_ref[...]
        b = b_ref[...]
        co = _dot(a, wco_ref[...])
        so = jnp.concatenate([_dot(b, wso_ref[j]) for j in range(NCHIP)], axis=1)
        sc = _sig(gc_ref[...].astype(f32))
        ss = _sig(gs_ref[...].astype(f32))
        mb = (sc * co + ss * so).astype(bf16)
        out = _dot(mb, wout_ref[...])
        r2 = lax.rsqrt(jnp.mean(out * out, axis=-1, keepdims=True) + RMS_EPS)
        on = out * r2
        gp = gp_ref[...]
        e = x_ref[...] + on * gp - t_ref[...]
        loss_ref[...] += (0.5 / D) * jnp.sum(e * e)
        dy = e * (1.0 / D)
        gx_ref[...] = dy
        dgp_ref[...] += jnp.sum(dy * on, axis=0, keepdims=True)
        dn = dy * gp
        dout = (r2 * (dn - on * jnp.mean(dn * on, axis=-1, keepdims=True))).astype(bf16)
        dwout_ref[...] += _dot_tn(mb, dout)
        dm = _dot_nt(dout, wout_ref[...])
        dp_ref[:, 0:D] = (dm * co * sc * (1.0 - sc)).astype(bf16)
        dp_ref[:, D:2 * D] = (dm * so * ss * (1.0 - ss)).astype(bf16)
        dco = (dm * sc).astype(bf16)
        dso = (dm * ss).astype(bf16)
        dwco_ref[...] += _dot_tn(a, dco)
        dbin = None
        for j in range(NCHIP):
            dso_j = dso[:, j * 256:(j + 1) * 256]
            dwso_ref[j] += _dot_tn(b, dso_j)
            t = _dot_nt(dso_j, wso_ref[j])
            dbin = t if dbin is None else dbin + t
        dain_ref[...] = _dot_nt(dco, wco_ref[...]).astype(bf16)
        dbin_ref[...] = dbin.astype(bf16)

    row = lambda w: pl.BlockSpec((tm, w), lambda i: (i, 0))
    one = lambda shape: pl.BlockSpec(shape, lambda i: (0,) * len(shape), pipeline_mode=pl.Buffered(1))
    return pl.pallas_call(
        body, grid=(L // tm,),
        in_specs=[row(CW), row(SW), pl.BlockSpec((tm, D), lambda i: (i, 4)), pl.BlockSpec((tm, D), lambda i: (i, 5)),
                  row(D), row(D), one((CW, D)), one((NCHIP, SW, 256)), one((D, D)), one((1, D))],
        out_specs=[row(D), row(CW), row(SW), pl.BlockSpec((tm, 2 * D), lambda i: (i, 2)),
                   one((D, D)), one((CW, D)), one((NCHIP, SW, 256)), one((1, D)), one((1, 128))],
        out_shape=[jax.ShapeDtypeStruct((L, D), f32), jax.ShapeDtypeStruct((L, CW), bf16),
                   jax.ShapeDtypeStruct((L, SW), bf16), jax.ShapeDtypeStruct((L, IN_W), bf16),
                   jax.ShapeDtypeStruct((D, D), f32), jax.ShapeDtypeStruct((CW, D), f32),
                   jax.ShapeDtypeStruct((NCHIP, SW, 256), f32), jax.ShapeDtypeStruct((1, D), f32),
                   jax.ShapeDtypeStruct((1, 128), f32)],
        name="tail", compiler_params=_cp("arbitrary"))(a_in, b_in, proj, proj, x, tgt, wco, wso, wout, gpost)


def _ssm_bwd(d_bin, y0, proj, sre, sim, cinr, cini, bbt_re, bbt_im, ct_re, ct_im,
             a_re, a_im, apow_re, apow_im, dvec, wglu, bglu, dproj):
    L = y0.shape[0]
    nc = L // TC
    def body(dbin_ref, y0_ref, u_ref, zs_ref, sre_ref, sim_ref, cinr_ref, cini_ref,
             bre_ref, bim_ref, cre_ref, cim_ref, are_ref, aim_ref, pwr_ref, pwi_ref, d_ref, wg_ref, bg_ref, _,
             dp_ref, dbre_ref, dbim_ref, dcre_ref, dcim_ref, dd_ref, dar_ref, dai_ref, dwg_ref, dbg_ref,
             gre, gim, gcr, gci, nxt_re, nxt_im):
        @pl.when(pl.program_id(0) == 0)
        def _():
            for ref in (dbre_ref, dbim_ref, dcre_ref, dcim_ref, dd_ref, dar_ref, dai_ref, dwg_ref, dbg_ref,
                        nxt_re, nxt_im):
                ref[...] = jnp.zeros_like(ref)
        y0 = y0_ref[...]
        u = u_ref[...]
        zs = zs_ref[...].astype(f32)
        dbin = dbin_ref[...].astype(f32)
        t, y1 = _gelu_parts(y0)
        y1b = y1.astype(bf16)
        sg = _sig(_dot(y1b, wg_ref[...]) + bg_ref[...])
        sz = _sig(zs)
        d_y2 = dbin * (zs * sz)
        dp_ref[:, SW:2 * SW] = (dbin * (y1 * sg) * (sz * (1.0 + zs * (1.0 - sz)))).astype(bf16)
        d_glu = d_y2 * y1 * sg * (1.0 - sg)
        d_glub = d_glu.astype(bf16)
        d_y1 = d_y2 * sg + _dot_nt(d_glub, wg_ref[...])
        dwg_ref[...] += _dot_tn(y1b, d_glub)
        dbg_ref[...] += jnp.sum(d_glu, axis=0, keepdims=True)
        dgelu = 0.5 * (1.0 + t) + 0.5 * y0 * (1.0 - t * t) * GELU_K0 * (1.0 + 3.0 * GELU_K1 * y0 * y0)
        d_y0 = d_y1 * dgelu
        dd_ref[...] += jnp.sum(d_y0 * u.astype(f32), axis=0, keepdims=True)
        dyb = d_y0.astype(bf16)
        for blk in range(4):
            dy1 = dyb[:, 128 * blk:128 * (blk + 1)]
            gre[:, 512 * blk:512 * (blk + 1)] = _dot_nt(dy1, cre_ref[blk])
            gim[:, 512 * blk:512 * (blk + 1)] = -_dot_nt(dy1, cim_ref[blk])
        for lb in range(NS // LBW):
            sl = slice(lb * LBW, (lb + 1) * LBW)
            ar = jnp.broadcast_to(are_ref[:, sl], (8, LBW))
            ai = jnp.broadcast_to(aim_ref[:, sl], (8, LBW))
            def step(k, carry, sl=sl, ar=ar, ai=ai):
                gr, gi = carry
                row = _rows8(R - 2 - k)
                nr = ar * gr + ai * gi + gre[row, sl]
                ni = ar * gi - ai * gr + gim[row, sl]
                gre[row, sl] = nr
                gim[row, sl] = ni
                return nr, ni
            lax.fori_loop(0, R - 1, step, (gre[8 * (R - 1):8 * R, sl], gim[8 * (R - 1):8 * R, sl]))
        a_r = pwr_ref[R - 1:R, :]
        a_i = pwi_ref[R - 1:R, :]
        cr = nxt_re[0:1, :]
        ci = nxt_im[0:1, :]
        for seg in range(7, -1, -1):
            gcr[seg:seg + 1, :] = cr
            gci[seg:seg + 1, :] = ci
            er = gre[seg:seg + 1, :]
            ei = gim[seg:seg + 1, :]
            cr, ci = er + a_r * cr + a_i * ci, ei + a_r * ci - a_i * cr
        nxt_re[0:1, :] = cr
        nxt_im[0:1, :] = ci
        for lb in range(NS // LBW):
            sl = slice(lb * LBW, (lb + 1) * LBW)
            kr = gcr[:, sl]
            ki = gci[:, sl]
            def fixed(rows, prow, sl=sl, kr=kr, ki=ki):
                pr = jnp.broadcast_to(pwr_ref[prow, sl], (8, LBW))
                pi = jnp.broadcast_to(pwi_ref[prow, sl], (8, LBW))
                gr = gre[rows, sl] + pr * kr + pi * ki
                gi = gim[rows, sl] + pr * ki - pi * kr
                gre[rows, sl] = gr
                gim[rows, sl] = gi
                return gr, gi
            g0r, g0i = fixed(slice(0, 8), slice(R - 1, R))
            p0r, p0i = cinr_ref[:, sl], cini_ref[:, sl]
            acc0 = (g0r * p0r + g0i * p0i, g0i * p0r - g0r * p0i)
            def dacc(r, carry, sl=sl, fixed=fixed):
                xr, xi = carry
                gr, gi = fixed(_rows8(r), pl.ds(R - 1 - r, 1))
                pr, pi = sre_ref[_rows8(r - 1), sl], sim_ref[_rows8(r - 1), sl]
                return xr + gr * pr + gi * pi, xi + gi * pr - gr * pi
            xr, xi = lax.fori_loop(1, R, dacc, acc0)
            dar_ref[:, sl] += xr
            dai_ref[:, sl] += xi
        dup = []
        for blk in range(4):
            s4 = slice(512 * blk, 512 * (blk + 1))
            s1 = slice(128 * blk, 128 * (blk + 1))
            grb = gre[:, s4].astype(bf16)
            gib = gim[:, s4].astype(bf16)
            dup.append(_dot_nt(grb, bre_ref[blk]) + _dot_nt(gib, bim_ref[blk]))
            dbre_ref[blk] += _dot_tn(u[:, s1], grb)
            dbim_ref[blk] += _dot_tn(u[:, s1], gib)
            dcre_ref[blk] += _dot_tn(dyb[:, s1], sre_ref[:, s4].astype(bf16))
            dcim_ref[blk] -= _dot_tn(dyb[:, s1], sim_ref[:, s4].astype(bf16))
        dp_ref[:, 0:SW] = (jnp.concatenate(dup, axis=1) + d_ref[...] * d_y0).astype(bf16)

    rev = lambda w, cidx: pl.BlockSpec((TC, w), lambda i, cidx=cidx: (nc - 1 - i, cidx))
    one = lambda shape: pl.BlockSpec(shape, lambda i: (0,) * len(shape))
    return pl.pallas_call(
        body, grid=(nc,),
        in_specs=[rev(SW, 0), rev(SW, 0), rev(SW, 6), rev(SW, 7), rev(NS, 0), rev(NS, 0),
                  pl.BlockSpec((8, NS), lambda i: (nc - 1 - i, 0)), pl.BlockSpec((8, NS), lambda i: (nc - 1 - i, 0)),
                  one((4, 128, 512)), one((4, 128, 512)), one((4, 512, 128)), one((4, 512, 128)),
                  one((1, NS)), one((1, NS)), one((R, NS)), one((R, NS)),
                  one((1, SW)), one((SW, SW)), one((1, SW)), _ANY],
        out_specs=[pl.BlockSpec((TC, 2 * SW), lambda i: (nc - 1 - i, 3)),
                   one((4, 128, 512)), one((4, 128, 512)), one((4, 128, 512)), one((4, 128, 512)),
                   one((1, SW)), one((8, NS)), one((8, NS)), one((SW, SW)), one((1, SW))],
        out_shape=[jax.ShapeDtypeStruct((L, IN_W), bf16),
                   jax.ShapeDtypeStruct((4, 128, 512), f32), jax.ShapeDtypeStruct((4, 128, 512), f32),
                   jax.ShapeDtypeStruct((4, 128, 512), f32), jax.ShapeDtypeStruct((4, 128, 512), f32),
                   jax.ShapeDtypeStruct((1, SW), f32), jax.ShapeDtypeStruct((8, NS), f32),
                   jax.ShapeDtypeStruct((8, NS), f32), jax.ShapeDtypeStruct((SW, SW), f32),
                   jax.ShapeDtypeStruct((1, SW), f32)],
        scratch_shapes=[pltpu.VMEM((TC, NS), f32), pltpu.VMEM((TC, NS), f32), pltpu.VMEM((8, NS), f32),
                        pltpu.VMEM((8, NS), f32), pltpu.VMEM((8, NS), f32), pltpu.VMEM((8, NS), f32)],
        input_output_aliases={19: 0},
        name="ssm_bwd", compiler_params=_cp("arbitrary"))(
            d_bin, y0, proj, proj, sre, sim, cinr, cini, bbt_re, bbt_im, ct_re, ct_im,
            a_re, a_im, apow_re, apow_im, dvec, wglu, bglu, dproj)


def _conv_bwd(d_ain, cu1, proj, cw, lng, lnb, dproj):
    L = cu1.shape[0]
    nc = L // TC
    def body(dain_ref, cu1_ref, ca_ref, cb_ref, zc_ref, cah_ref, cbh_ref, w_ref, g_ref, bb_ref, _,
             dp_ref, dw_ref, dbias_ref, dlng_ref, dlnb_ref, dbuf, ebuf, prev, nxt, dcu0):
        i = pl.program_id(0)
        @pl.when(i == 0)
        def _():
            dw_ref[...] = jnp.zeros_like(dw_ref)
            dbias_ref[...] = jnp.zeros_like(dbias_ref)
            dlng_ref[...] = jnp.zeros_like(dlng_ref)
            dlnb_ref[...] = jnp.zeros_like(dlnb_ref)
            nxt[...] = jnp.zeros_like(nxt)
        def lnb(s, carry):
            rows = pl.ds(pl.multiple_of(s * 32, 32), 32)
            dain = dain_ref[rows, :].astype(f32)
            c1 = cu1_ref[rows, :].astype(f32)
            zc = zc_ref[rows, :].astype(f32)
            xc = c1 - jnp.mean(c1, axis=-1, keepdims=True)
            var = jnp.mean(xc * xc, axis=-1, keepdims=True)
            rstd = lax.rsqrt(var + LN_EPS)
            xh = xc * rstd
            ln = xh * g_ref[...] + bb_ref[...]
            sl_ = _sig(ln)
            sz = _sig(zc)
            dp_ref[rows, 2 * CW:3 * CW] = (dain * (ln * sl_) * (sz * (1.0 + zc * (1.0 - sz)))).astype(bf16)
            d_ln = dain * (zc * sz) * (sl_ * (1.0 + ln * (1.0 - sl_)))
            dlng_ref[...] += jnp.sum(d_ln * xh, axis=0, keepdims=True)
            dlnb_ref[...] += jnp.sum(d_ln, axis=0, keepdims=True)
            dxh = d_ln * g_ref[...]
            d_c1 = rstd * (dxh - jnp.mean(dxh, axis=-1, keepdims=True)
                           - xh * jnp.mean(dxh * xh, axis=-1, keepdims=True))
            dbias_ref[...] += jnp.sum(d_c1, axis=0, keepdims=True)
            _put_blocked(dbuf, pl.multiple_of(s * 32, 32), 32, d_c1)
            _put_blocked(ebuf, pl.multiple_of(NH * 8 + s * 32, 32), 32,
                         ca_ref[rows, :].astype(f32) * _sig(cb_ref[rows, :].astype(f32)))
            return carry
        lax.fori_loop(0, TC // 32, lnb, 0, unroll=4)
        sub = lax.broadcasted_iota(jnp.int32, (8, 128), 0)
        def after(p, carry):
            for lb in range(NLB):
                cur = dbuf[lb, _rows8(p), :]
                dbuf[lb, _rows8(R + p), :] = jnp.where(sub == 7, pltpu.roll(nxt[lb, _rows8(p), :], 7, 0),
                                                       pltpu.roll(cur, 7, 0))
            return carry
        lax.fori_loop(0, NH, after, 0)
        nxt[...] = dbuf[:, 0:NH * 8, :]
        def before(s, carry):
            rows = pl.ds(pl.multiple_of(s * 64, 64), 64)
            v = cah_ref[rows, :].astype(f32) * _sig(cbh_ref[rows, :].astype(f32))
            _put_blocked(prev, pl.multiple_of(s * 64, 64), 64, jnp.where(i == nc - 1, jnp.zeros_like(v), v))
            return carry
        lax.fori_loop(0, NH * 8 // 64, before, 0)
        _fill_before(ebuf, prev)
        for lb in range(NLB):
            sl = slice(lb * 128, (lb + 1) * 128)
            wk = [jnp.broadcast_to(w_ref[k:k + 1, sl], (8, 128)) for k in range(KS)]
            def tap(q, carry, lb=lb, wk=wk):
                r = q * RPI
                for j, o in enumerate(_fir(dbuf, lb, r, wk, None, True)):
                    dcu0[lb, _rows8(r + j), :] = o
                return carry
            lax.fori_loop(0, R // RPI, tap, 0)
            def wgrad(q, accs, lb=lb):
                r = q * RPI
                dvs = dbuf[lb, pl.ds(pl.multiple_of(r * 8, 8), RPI * 8), :]
                win = ebuf[lb, pl.ds(pl.multiple_of((r + (NH - KS + 1)) * 8, 8), (KS + RPI - 1) * 8), :]
                accs = list(accs)
                for j in range(RPI):
                    dv = dvs[8 * j:8 * j + 8, :]
                    for k in range(KS):
                        accs[k] = accs[k] + dv * win[8 * (j + k):8 * (j + k) + 8, :]
                return tuple(accs)
            accs = lax.fori_loop(0, R // RPI, wgrad, tuple(jnp.zeros((8, 128), f32) for _ in range(KS)))
            for k in range(KS):
                dw_ref[k, :, sl] += accs[k]
        def glub(s, carry):
            rows = pl.ds(pl.multiple_of(s * 64, 64), 64)
            d0 = _get_blocked(dcu0, pl.multiple_of(s * 64, 64), 64)
            ca = ca_ref[rows, :].astype(f32)
            sb = _sig(cb_ref[rows, :].astype(f32))
            dp_ref[rows, 0:CW] = (d0 * sb).astype(bf16)
            dp_ref[rows, CW:2 * CW] = (d0 * ca * sb * (1.0 - sb)).astype(bf16)
            return carry
        lax.fori_loop(0, TC // 64, glub, 0)

    hrows = NH * 8
    per = TC // hrows
    rev = lambda cidx: pl.BlockSpec((TC, CW), lambda i, cidx=cidx: (nc - 1 - i, cidx))
    halo = lambda cidx: pl.BlockSpec((hrows, CW), lambda i, cidx=cidx: (jnp.maximum((nc - 1 - i) * per - 1, 0), cidx))
    one = lambda shape: pl.BlockSpec(shape, lambda i: (0,) * len(shape))
    return pl.pallas_call(
        body, grid=(nc,),
        in_specs=[rev(0), rev(0), rev(0), rev(1), rev(2), halo(0), halo(1), one((32, CW)), one((1, CW)), one((1, CW)),
                  _ANY],
        out_specs=[pl.BlockSpec((TC, 3 * CW), lambda i: (nc - 1 - i, 0)), one((32, 8, CW)), one((1, CW)), one((1, CW)), one((1, CW))],
        out_shape=[jax.ShapeDtypeStruct((L, IN_W), bf16), jax.ShapeDtypeStruct((32, 8, CW), f32),
                   jax.ShapeDtypeStruct((1, CW), f32), jax.ShapeDtypeStruct((1, CW), f32),
                   jax.ShapeDtypeStruct((1, CW), f32)],
        scratch_shapes=[pltpu.VMEM((NLB, (R + NH) * 8, 128), f32), pltpu.VMEM((NLB, (NH + R) * 8, 128), f32),
                        pltpu.VMEM((NLB, hrows, 128), f32), pltpu.VMEM((NLB, hrows, 128), f32),
                        pltpu.VMEM((NLB, TC, 128), f32)],
        input_output_aliases={10: 0},
        name="conv_bwd", compiler_params=_cp("arbitrary"))(d_ain, cu1, proj, proj, proj, proj, proj, cw, lng, lnb, dproj)


def _win_grad(h, dproj):
    L = h.shape[0]
    tm = min(1024, L)
    nt = L // tm
    def body(h_ref, d_ref, o_ref, acc):
        i = pl.program_id(1)
        @pl.when(i == 0)
        def _():
            acc[...] = jnp.zeros_like(acc)
        acc[...] += _dot_tn(h_ref[...], d_ref[...])
        @pl.when(i == nt - 1)
        def _():
            o_ref[0] = acc[...].astype(bf16)
    return pl.pallas_call(
        body, grid=(NCHIP, nt),
        in_specs=[pl.BlockSpec((tm, D), lambda j, i: (i, 0)), pl.BlockSpec((tm, SHARD_W), lambda j, i: (i, j))],
        out_specs=pl.BlockSpec((1, D, SHARD_W), lambda j, i: (j, 0, 0)),
        out_shape=jax.ShapeDtypeStruct((NCHIP, D, SHARD_W), bf16),
        scratch_shapes=[pltpu.VMEM((D, SHARD_W), f32)],
        name="win_grad", compiler_params=_cp("arbitrary", "arbitrary"))(h, dproj)


def _adamw_math(w, g, m, v):
    m2 = B1 * m + (1.0 - B1) * g
    v2 = B2 * v + (1.0 - B2) * (g * g)
    m_hat = m2 / (1.0 - B1 ** STEP)
    v_hat = v2 / (1.0 - B2 ** STEP)
    delta = -LR * (m_hat / (jnp.sqrt(v_hat) + EPS) + WD * w)
    return delta, m2, v2


def _adamw(name, w, g, m, v):
    rows, cols = w.shape
    tm = rows if rows <= 256 else (256 if rows % 256 == 0 else 128)
    assert rows % tm == 0
    def body(w_ref, g_ref, m_ref, v_ref, d_ref, m2_ref, v2_ref):
        d, m2, v2 = _adamw_math(w_ref[...], g_ref[...], m_ref[...], v_ref[...])
        d_ref[...] = d
        m2_ref[...] = m2
        v2_ref[...] = v2
    spec = pl.BlockSpec((tm, cols), lambda i: (i, 0))
    shp = jax.ShapeDtypeStruct((rows, cols), f32)
    return pl.pallas_call(
        body, grid=(rows // tm,), in_specs=[spec] * 4, out_specs=[spec] * 3, out_shape=[shp] * 3,
        name=name, compiler_params=_cp("arbitrary"))(w, g, m, v)


def _adamw_group(name, ws, gs, ms, vs):
    n = len(ws)
    def body(*refs):
        for i in range(n):
            w_ref, g_ref, m_ref, v_ref = (refs[q * n + i] for q in range(4))
            d, m2, v2 = _adamw_math(w_ref[...], g_ref[...], m_ref[...], v_ref[...])
            for q, val in enumerate((d, m2, v2)):
                refs[(4 + q) * n + i][...] = val
    shapes = [jax.ShapeDtypeStruct(w.shape, f32) for w in ws]
    out = pl.pallas_call(body, out_shape=shapes * 3, name=name,
                         compiler_params=pltpu.CompilerParams(vmem_limit_bytes=VMEM_LIMIT))(*ws, *gs, *ms, *vs)
    return [(out[i], out[n + i], out[2 * n + i]) for i in range(n)]


_ANY = pl.BlockSpec(memory_space=pl.ANY)


def _chunks(rows, parts):
    step = rows // parts
    assert step * parts == rows and step % 16 == 0
    return [(i * step, step) for i in range(parts)]


def _place():
    x, y, c = lax.axis_index("x"), lax.axis_index("y"), lax.axis_index("c")
    chips = [(1 - x, y), (x, 1 - y), (1 - x, 1 - y)]
    return x, y, c, chips


def _nchunks(half, cols, itemsize):
    return 4 if half * cols * itemsize >= (1 << 20) else 1


def _segments(metas):
    segs = []
    for w, (half, cols, dt) in enumerate(metas):
        for r0, n in _chunks(half, _nchunks(half, cols, jnp.dtype(dt).itemsize)):
            segs.append((w, half, r0, n))
    return segs


def _rcopy(i, src, dst, send_sems, recv_sems, to):
    return pltpu.make_async_remote_copy(src_ref=src, dst_ref=dst, send_sem=send_sems.at[i], recv_sem=recv_sems.at[i],
                                        device_id=to, device_id_type=MESH)


def _gather_prep(k_arr, shards, x, tgt, g_pre, perm):
    na = len(shards)
    L = x.shape[0]
    nc = L // TC
    segs = _segments([(a.shape[0] // 2, a.shape[1], a.dtype) for a in shards])
    ns = len(segs)
    def body(_, *refs):
        ins = refs[:na]
        x_ref, t_ref, g_ref, p_ref = refs[na:na + 4]
        outs = refs[na + 4:2 * na + 4]
        h_ref, xi_ref, ti_ref, proj_ref = refs[2 * na + 4:2 * na + 8]
        stages = refs[2 * na + 8:3 * na + 8]
        send_sems, recv_sems, local_sems = refs[3 * na + 8:]
        i = pl.program_id(0)
        x, y, c, chips = _place()
        k = 2 * x + y
        me, sibling = (x, y, c), (x, y, 1 - c)

        def dst(w, half, chip, pc, r0, n):
            return outs[w].at[chip, pl.ds(pc * half + r0, n), :]

        def firsts():
            return [_rcopy(j * ns + s, ins[w].at[pl.ds(c * half + r0, n), :], dst(w, half, k, c, r0, n),
                           send_sems, recv_sems, (*chip, c))
                    for j, chip in enumerate(chips) for s, (w, half, r0, n) in enumerate(segs)]

        def own_out(w):
            return pltpu.make_async_copy(stages[w], outs[w].at[k], local_sems.at[w])

        @pl.when(i == 0)
        def _():
            for cp in firsts():
                cp.start()
            cins = [pltpu.make_async_copy(ins[w], stages[w], local_sems.at[w]) for w in range(na)]
            for cp in cins:
                cp.start()
            for w in range(na):
                cins[w].wait()
                own_out(w).start()

        p = p_ref[...]
        def through(v):
            hi = v.astype(bf16)
            r1 = v - hi.astype(f32)
            mid = r1.astype(bf16)
            lo = (r1 - mid.astype(f32)).astype(bf16)
            return (_dot(p, hi) + _dot(p, mid)) + _dot(p, lo)
        xt = x_ref[...]
        r = lax.rsqrt(jnp.mean(xt * xt, axis=-1, keepdims=True) + RMS_EPS)
        hp = _dot(p, (xt * r * g_ref[...]).astype(bf16)).astype(bf16)
        h_ref[...] = hp
        proj_ref[...] = _dot(hp, stages[0][...]).astype(bf16)
        xi_ref[...] = through(xt)
        ti_ref[...] = through(t_ref[...])

        @pl.when(i == nc - 1)
        def _():
            passed = []
            for j, chip in enumerate(chips):
                cj = 2 * chip[0] + chip[1]
                for s, (w, half, r0, n) in enumerate(segs):
                    landed = dst(w, half, cj, c, r0, n)
                    _rcopy(j * ns + s, landed, landed, send_sems, recv_sems, me).wait_recv()
                    fwd = _rcopy(3 * ns + j * ns + s, landed, landed, send_sems, recv_sems, sibling)
                    fwd.start()
                    passed.append(fwd)
            for j, chip in enumerate(chips):
                cj = 2 * chip[0] + chip[1]
                for s, (w, half, r0, n) in enumerate(segs):
                    theirs = dst(w, half, cj, 1 - c, r0, n)
                    _rcopy(3 * ns + j * ns + s, theirs, theirs, send_sems, recv_sems, me).wait_recv()
            for cp in firsts() + passed:
                cp.wait_send()
            for w in range(na):
                own_out(w).wait()

    row = lambda: pl.BlockSpec((TC, D), lambda i, k: (i, 0))
    grid_spec = pltpu.PrefetchScalarGridSpec(
        num_scalar_prefetch=1, grid=(nc,),
        in_specs=[_ANY] * na + [row(), row(), pl.BlockSpec((1, D), lambda i, k: (0, 0)),
                                pl.BlockSpec((TC, TC), lambda i, k: (0, 0))],
        out_specs=[_ANY] * na + [row(), row(), row(), pl.BlockSpec((TC, SHARD_W), lambda i, k: (i, k[0]))],
        scratch_shapes=[pltpu.VMEM(a.shape, a.dtype) for a in shards]
        + [pltpu.SemaphoreType.DMA((6 * ns,)), pltpu.SemaphoreType.DMA((6 * ns,)), pltpu.SemaphoreType.DMA((na,))])
    return pl.pallas_call(
        body, grid_spec=grid_spec,
        out_shape=[jax.ShapeDtypeStruct((NCHIP,) + a.shape, a.dtype) for a in shards]
        + [jax.ShapeDtypeStruct((L, D), bf16), jax.ShapeDtypeStruct((L, D), f32), jax.ShapeDtypeStruct((L, D), f32),
           jax.ShapeDtypeStruct((L, IN_W), bf16)],
        name="gather_prep", compiler_params=_cp("arbitrary"))(k_arr, *shards, x, tgt, g_pre, perm)


def _x_grad_exchange(dproj, w_in, x, gx0, g_pre, parts, small):
    L = x.shape[0]
    tm = 512
    nt = L // tm
    na = len(parts)
    hs = SMALL_ROWS // 2
    segs = _segments([(p.shape[1], p.shape[2], p.dtype) for p in parts])
    ns = len(segs) + 1
    def body(*refs):
        d_ref, w_ref, x_ref, gx_ref, g_ref = refs[:5]
        ins, s_ref = refs[5:5 + na], refs[5 + na]
        o_ref, dg_ref = refs[6 + na:8 + na]
        outs, qs_ref = refs[8 + na:8 + 2 * na], refs[8 + 2 * na]
        stages = refs[9 + 2 * na:10 + 3 * na]
        send_sems, recv_sems, local_sems = refs[10 + 3 * na:]
        i = pl.program_id(0)
        x, y, c, chips = _place()
        k = 2 * x + y

        def my_small():
            return s_ref.at[pl.ds(c * hs, hs), :]

        def copies():
            out = []
            for j, chip in enumerate(chips):
                cj = 2 * chip[0] + chip[1]
                pieces = [(my_small(), qs_ref.at[k])]
                pieces += [(ins[w].at[cj, pl.ds(r0, n), :], outs[w].at[k, pl.ds(r0, n), :]) for w, _, r0, n in segs]
                out += [_rcopy(ns * j + s, src, d, send_sems, recv_sems, (*chip, c)) for s, (src, d) in enumerate(pieces)]
            return out

        def own_out(w):
            dst = qs_ref.at[k] if w == na else outs[w].at[k]
            return pltpu.make_async_copy(stages[w], dst, local_sems.at[w])

        @pl.when(i == 0)
        def _():
            dg_ref[...] = jnp.zeros_like(dg_ref)
            for cp in copies():
                cp.start()
            cins = [pltpu.make_async_copy(my_small() if w == na else ins[w].at[k], stages[w], local_sems.at[w])
                    for w in range(na + 1)]
            for cp in cins:
                cp.start()
            for w in range(na + 1):
                cins[w].wait()
                own_out(w).start()

        dh = _dot_nt(d_ref[:, 0:SHARD_W], w_ref[0])
        for j in range(1, NCHIP):
            dh = dh + _dot_nt(d_ref[:, j * SHARD_W:(j + 1) * SHARD_W], w_ref[j])
        xt = x_ref[...]
        r = lax.rsqrt(jnp.mean(xt * xt, axis=-1, keepdims=True) + RMS_EPS)
        xn = xt * r
        dg_ref[...] += jnp.sum(dh * xn, axis=0, keepdims=True)
        dxn = dh * g_ref[...]
        o_ref[...] = gx_ref[...] + r * (dxn - xn * jnp.mean(dxn * xn, axis=-1, keepdims=True))

        @pl.when(i == nt - 1)
        def _():
            for cp in copies():
                cp.wait_recv()
            for cp in copies():
                cp.wait_send()
            for w in range(na + 1):
                own_out(w).wait()

    return pl.pallas_call(
        body, grid=(nt,),
        in_specs=[pl.BlockSpec((tm, IN_W), lambda i: (i, 0)),
                  pl.BlockSpec((NCHIP, D, SHARD_W), lambda i: (0, 0, 0), pipeline_mode=pl.Buffered(1)),
                  pl.BlockSpec((tm, D), lambda i: (i, 0)), pl.BlockSpec((tm, D), lambda i: (i, 0)), _full((1, D))]
        + [_ANY] * (na + 1),
        out_specs=[pl.BlockSpec((tm, D), lambda i: (i, 0)), _full((1, D))] + [_ANY] * (na + 1),
        out_shape=[jax.ShapeDtypeStruct((L, D), f32), jax.ShapeDtypeStruct((1, D), f32)]
        + [jax.ShapeDtypeStruct(p.shape, bf16) for p in parts] + [jax.ShapeDtypeStruct((NCHIP, hs, 128), f32)],
        scratch_shapes=[pltpu.VMEM(p.shape[1:], bf16) for p in parts] + [pltpu.VMEM((hs, 128), f32)]
        + [pltpu.SemaphoreType.DMA((3 * ns,)), pltpu.SemaphoreType.DMA((3 * ns,)), pltpu.SemaphoreType.DMA((na + 1,))],
        name="x_grad_exchange", compiler_params=_cp("arbitrary"))(dproj, w_in, x, gx0, g_pre, *parts, small)


def _sibling_join_list(halves):
    na = len(halves)
    segs = _segments([(h.shape[0], h.shape[1], h.dtype) for h in halves])
    def body(*refs):
        ins, outs, stages = refs[:na], refs[na:2 * na], refs[2 * na:3 * na]
        send_sems, recv_sems, local_sems = refs[3 * na:]
        x, y, c, _ = _place()
        copies = [_rcopy(i, ins[w].at[pl.ds(r0, n), :], outs[w].at[pl.ds(c * half + r0, n), :], send_sems, recv_sems,
                         (x, y, 1 - c)) for i, (w, half, r0, n) in enumerate(segs)]
        for cp in copies:
            cp.start()
        cins = [pltpu.make_async_copy(ins[w], stages[w], local_sems.at[w]) for w in range(na)]
        for cp in cins:
            cp.start()
        own = []
        for w in range(na):
            cins[w].wait()
            half = halves[w].shape[0]
            own.append(pltpu.make_async_copy(stages[w], outs[w].at[pl.ds(c * half, half), :], local_sems.at[w]))
            own[-1].start()
        for cp in copies:
            cp.wait_recv()
        for cp in copies:
            cp.wait_send()
        for cp in own:
            cp.wait()

    return pl.pallas_call(
        body, in_specs=[_ANY] * na, out_specs=[_ANY] * na,
        out_shape=[jax.ShapeDtypeStruct((2 * h.shape[0], h.shape[1]), f32) for h in halves],
        scratch_shapes=[pltpu.VMEM(h.shape, f32) for h in halves]
        + [pltpu.SemaphoreType.DMA((len(segs),)), pltpu.SemaphoreType.DMA((len(segs),)), pltpu.SemaphoreType.DMA((na,))],
        name="sibling_join")(*halves)


def _small_join(v, fs_half):
    hs = fs_half.shape[0]
    def body(v_ref, h_ref, o_ref, fs_ref, send_sems, recv_sems):
        x, y, c, _ = _place()
        me = 4 * x + 2 * y + c
        o_ref[me] = v_ref[...]
        mine = pl.ds(pl.multiple_of(c * hs, 8), hs)
        fs_ref[mine, :] = h_ref[...]
        copies = [_rcopy(7, h_ref, fs_ref.at[mine, :], send_sems, recv_sems, (x, y, 1 - c))]
        i = 0
        for dx in range(2):
            for dy in range(2):
                for dc in range(2):
                    if dx + dy + dc:
                        copies.append(_rcopy(i, v_ref, o_ref.at[me], send_sems, recv_sems, (x ^ dx, y ^ dy, c ^ dc)))
                        i += 1
        for cp in copies:
            cp.start()
        for cp in copies:
            cp.wait_recv()
        for cp in copies:
            cp.wait_send()

    vm = pl.BlockSpec(memory_space=pltpu.VMEM)
    return pl.pallas_call(
        body, in_specs=[vm, vm], out_specs=[vm, vm],
        out_shape=[jax.ShapeDtypeStruct((8, 8, 128), f32), jax.ShapeDtypeStruct((2 * hs, 128), f32)],
        scratch_shapes=[pltpu.SemaphoreType.DMA((8,)), pltpu.SemaphoreType.DMA((8,))],
        name="small_join")(v, fs_half)


def _adamw_rows(parts, w, m, v):
    def body(p_ref, w_ref, m_ref, v_ref, g_ref, d_ref, m2_ref, v2_ref):
        g = p_ref[0]
        for dvc in range(1, 8):
            g = g + p_ref[dvc]
        g_ref[...] = g
        d, m2, v2 = _adamw_math(w_ref[...], g, m_ref[...], v_ref[...])
        d_ref[...] = d
        m2_ref[...] = m2
        v2_ref[...] = v2
    return pl.pallas_call(body, out_shape=[jax.ShapeDtypeStruct((8, 128), f32)] * 4, name="adamw_pre_norm_gain")(
        parts, w, m, v)


def _pair_exchange_list(grads, small):
    na = len(grads)
    segs = _segments([(g.shape[1] // 2, g.shape[2], g.dtype) for g in grads])
    n = NCHIP * len(segs) + 1
    def body(*refs):
        ins, s_ref, outs, rs_ref, (send_sems, recv_sems) = (refs[:na], refs[na], refs[na + 1:2 * na + 1],
                                                            refs[2 * na + 1], refs[2 * na + 2:])
        x, y, c, _ = _place()
        pieces = [(s_ref, rs_ref)]
        for j in range(NCHIP):
            for w, half, r0, rows in segs:
                pieces.append((ins[w].at[j, pl.ds((1 - c) * half + r0, rows), :], outs[w].at[j, pl.ds(r0, rows), :]))
        copies = [_rcopy(i, s, d, send_sems, recv_sems, (x, y, 1 - c)) for i, (s, d) in enumerate(pieces)]
        for cp in copies:
            cp.start()
        for cp in copies:
            cp.wait_recv()
        for cp in copies:
            cp.wait_send()

    return pl.pallas_call(
        body, in_specs=[_ANY] * (na + 1), out_specs=[_ANY] * (na + 1),
        out_shape=[jax.ShapeDtypeStruct((NCHIP, g.shape[1] // 2, g.shape[2]), g.dtype) for g in grads]
        + [jax.ShapeDtypeStruct((SMALL_ROWS, 128), f32)],
        scratch_shapes=[pltpu.SemaphoreType.DMA((n,)), pltpu.SemaphoreType.DMA((n,))],
        name="pair_exchange")(*grads, small)


def _pair_sum_list(c_arr, grads, recvs, small, rsmall):
    na = len(grads)
    def body(c_ref, *refs):
        g_refs, r_refs, s_ref, rs_ref = refs[:na], refs[na:2 * na], refs[2 * na], refs[2 * na + 1]
        o_refs, os_ref = refs[2 * na + 2:3 * na + 2], refs[3 * na + 2]
        for g_ref, r_ref, o_ref in zip(g_refs, r_refs, o_refs):
            o_ref[...] = (g_ref[...].astype(f32) + r_ref[...].astype(f32)).astype(bf16)
        os_ref[...] = s_ref[...] + rs_ref[...]
    half = lambda g: pl.BlockSpec((1, g.shape[1] // 2, g.shape[2]), lambda j, c: (j, c[0], 0))
    low = lambda g: pl.BlockSpec((1, g.shape[1] // 2, g.shape[2]), lambda j, c: (j, 0, 0))
    sm = pl.BlockSpec((SMALL_ROWS, 128), lambda j, c: (0, 0))
    grid_spec = pltpu.PrefetchScalarGridSpec(
        num_scalar_prefetch=1, grid=(NCHIP,),
        in_specs=[half(g) for g in grads] + [low(g) for g in grads] + [sm, sm],
        out_specs=[low(g) for g in grads] + [sm])
    return pl.pallas_call(
        body, grid_spec=grid_spec,
        out_shape=[jax.ShapeDtypeStruct((NCHIP, g.shape[1] // 2, g.shape[2]), bf16) for g in grads]
        + [jax.ShapeDtypeStruct((SMALL_ROWS, 128), f32)],
        name="pair_sum", compiler_params=_cp("arbitrary"))(c_arr, *grads, *recvs, small, rsmall)


def _chip_sum_list(parts, small):
    na = len(parts)
    nt = 2
    def body(*refs):
        for q_ref, f_ref in zip(refs[:na + 1], refs[na + 1:]):
            acc = q_ref[0].astype(f32)
            for j in range(1, NCHIP):
                acc = acc + q_ref[j].astype(f32)
            f_ref[...] = acc
    arrs = list(parts) + [small]
    return pl.pallas_call(
        body, grid=(nt,),
        in_specs=[pl.BlockSpec((NCHIP, a.shape[1] // nt, a.shape[2]), lambda i: (0, i, 0)) for a in arrs],
        out_specs=[pl.BlockSpec((a.shape[1] // nt, a.shape[2]), lambda i: (i, 0)) for a in arrs],
        out_shape=[jax.ShapeDtypeStruct(a.shape[1:], f32) for a in arrs],
        name="chip_sum", compiler_params=_cp("arbitrary"))(*arrs)


_SMALL =(("conv_b", (1, 1024)), ("conv_ln_gain", (1, 1024)), ("conv_ln_bias", (1, 1024)),
          ("ssm_lambda_re", (1, 32, 64)), ("ssm_lambda_im", (1, 32, 64)), ("ssm_log_dt", (1, 32)),
          ("ssm_b_re", (1, 32, 64, 16)), ("ssm_b_im", (1, 32, 64, 16)), ("ssm_c_re", (1, 32, 16, 64)),
          ("ssm_c_im", (1, 32, 16, 64)), ("ssm_d", (1, 32, 16)), ("b_ssm_glu", (1, 512)), ("post_norm_gain", (1, 1024)))


def _pack_small(vals, extra=None):
    rows = []
    for v in list(vals) + ([extra] if extra is not None else []):
        flat = v.reshape(-1).astype(f32)
        n = -(-flat.shape[0] // 1024) * 1024
        rows.append(jnp.pad(flat, (0, n - flat.shape[0])).reshape(-1, 128))
    used = sum(r.shape[0] for r in rows)
    rows.append(jnp.zeros((SMALL_ROWS - used, 128), f32))
    return jnp.concatenate(rows, axis=0)


def _unpack_small(p):
    o = 0
    out = []
    for _, shape in _SMALL:
        n = int(np.prod(shape))
        nr = -(-n // 1024) * 8
        out.append(p[o:o + nr].reshape(-1)[:n].reshape(shape))
        o += nr
    return out, p[o, 0]


def _discretize(lam_re, lam_im, log_dt, b_re, b_im):
    dt = jnp.exp(log_dt)[:, None]
    mag = jnp.exp(lam_re * dt)
    ar = mag * jnp.cos(lam_im * dt)
    ai = mag * jnp.sin(lam_im * dt)
    den = lam_re * lam_re + lam_im * lam_im
    zr = ((ar - 1.0) * lam_re + ai * lam_im) / den
    zi = (ai * lam_re - (ar - 1.0) * lam_im) / den
    bbr = zr[..., None] * b_re - zi[..., None] * b_im
    bbi = zr[..., None] * b_im + zi[..., None] * b_re
    return ar, ai, bbr, bbi


_EYE8 = np.eye(8, dtype=np.float32)


def _bbt_blocks(bb):
    v = bb.reshape(4, 8, PST, H).transpose(0, 1, 3, 2)
    return jnp.einsum("bghp,gk->bghkp", v, _EYE8).reshape(4, 128, 512)


def _bbt_unblock(m):
    v = jnp.einsum("bghkp,gk->bghp", m.reshape(4, 8, H, 8, PST), _EYE8)
    return v.transpose(0, 1, 3, 2).reshape(G, PST, H)


def _ct_blocks(cc):
    v = cc.reshape(4, 8, H, PST)
    return jnp.einsum("bghp,gk->bgpkh", v, _EYE8).reshape(4, 512, 128)


def _ct_unblock(m):
    return jnp.einsum("bghkp,gk->bghp", m.reshape(4, 8, H, 8, PST), _EYE8).reshape(G, H, PST)


def _perm_matrix():
    p = np.zeros((TC, TC), np.float32)
    for r in range(R):
        for seg in range(8):
            p[r * 8 + seg, seg * R + r] = 1.0
    return p


def _deinterleave(a):
    L, C = a.shape
    return a.reshape(L // TC, R, 8, C).transpose(0, 2, 1, 3).reshape(L, C)


def _fwd_bwd(h, xi, ti, proj, conv_w, w_co, w_glu, w_so, w_out, small):
    (conv_b, ln_g, ln_b, lam_re, lam_im, log_dt, b_re, b_im, c_re, c_im, dvec, b_glu, g_post) = small
    lam_re, lam_im, log_dt = lam_re[0], lam_im[0], log_dt[0]
    b_re, b_im, c_re, c_im = b_re[0], b_im[0], c_re[0], c_im[0]
    (ar, ai, bbr, bbi), disc_vjp = jax.vjp(_discretize, lam_re, lam_im, log_dt, b_re, b_im)
    a_re = ar.reshape(1, NS)
    a_im = ai.reshape(1, NS)
    dt = jnp.exp(log_dt)[:, None]
    steps = jnp.arange(1, R + 1, dtype=f32)[:, None, None]
    apow_re = (jnp.exp(steps * (lam_re * dt)) * jnp.cos(steps * (lam_im * dt))).reshape(R, NS)
    apow_im = (jnp.exp(steps * (lam_re * dt)) * jnp.sin(steps * (lam_im * dt))).reshape(R, NS)
    bbt_re, bbt_im = _bbt_blocks(bbr).astype(bf16), _bbt_blocks(bbi).astype(bf16)
    ct_re, ct_im = _ct_blocks(c_re).astype(bf16), _ct_blocks(c_im).astype(bf16)
    d_row = dvec.reshape(1, SW)
    cw32 = jnp.pad(conv_w, ((0, 1), (0, 0)))

    cu1, a_in = _conv_fwd(proj, cw32, conv_b, ln_g, ln_b)
    y0, b_in, sre, sim, cinr, cini = _ssm_fwd(proj, bbt_re, bbt_im, ct_re, ct_im, a_re, a_im,
                                              apow_re, apow_im, d_row, w_glu, b_glu)
    gx0, d_ain, d_bin, dproj, dw_out, dw_co, dw_so, dg_post, loss = _tail(
        a_in, b_in, proj, xi, ti, w_co, w_so, w_out, g_post)
    (dproj, dbbt_re, dbbt_im, dct_re, dct_im, dd, dar8, dai8, dw_glu, db_glu) = _ssm_bwd(
        d_bin, y0, proj, sre, sim, cinr, cini, bbt_re, bbt_im, ct_re, ct_im,
        a_re, a_im, apow_re, apow_im, d_row, w_glu, b_glu, dproj)
    dproj, dcw8, d_convb, d_lng, d_lnb = _conv_bwd(d_ain, cu1, proj, cw32, ln_g, ln_b, dproj)
    dw_in = _win_grad(h, dproj)

    d_ar = jnp.sum(dar8, axis=0).reshape(G, PST)
    d_ai = jnp.sum(dai8, axis=0).reshape(G, PST)
    d_lre, d_lim, d_ldt, d_bre, d_bim = disc_vjp((d_ar, d_ai, _bbt_unblock(dbbt_re), _bbt_unblock(dbbt_im)))
    d_conv_w = jnp.sum(dcw8, axis=1)[:KS]
    small_grads = [d_convb, d_lng, d_lnb, d_lre[None], d_lim[None], d_ldt[None], d_bre[None], d_bim[None],
                   _ct_unblock(dct_re)[None], _ct_unblock(dct_im)[None], dd.reshape(1, G, H), db_glu, dg_post]
    return loss[0, 0], gx0, dproj, (dw_in, dw_co, dw_out, dw_glu, dw_so, d_conv_w), small_grads


def kernel(x, pre_norm_gain, w_in, conv_w, conv_b, conv_ln_gain, conv_ln_bias, w_conv_out, ssm_lambda_re, ssm_lambda_im, ssm_log_dt, ssm_b_re, ssm_b_im, ssm_c_re, ssm_c_im, ssm_d, w_ssm_glu, b_ssm_glu, w_ssm_out, w_out, post_norm_gain, loss_target, m_pre_norm_gain, m_w_in, m_conv_w, m_conv_b, m_conv_ln_gain, m_conv_ln_bias, m_w_conv_out, m_ssm_lambda_re, m_ssm_lambda_im, m_ssm_log_dt, m_ssm_b_re, m_ssm_b_im, m_ssm_c_re, m_ssm_c_im, m_ssm_d, m_w_ssm_glu, m_b_ssm_glu, m_w_ssm_out, m_w_out, m_post_norm_gain, v_pre_norm_gain, v_w_in, v_conv_w, v_conv_b, v_conv_ln_gain, v_conv_ln_bias, v_w_conv_out, v_ssm_lambda_re, v_ssm_lambda_im, v_ssm_log_dt, v_ssm_b_re, v_ssm_b_im, v_ssm_c_re, v_ssm_c_im, v_ssm_d, v_w_ssm_glu, v_b_ssm_glu, v_w_ssm_out, v_w_out, v_post_norm_gain):
    c = lax.axis_index("c")
    shards = [w_in[0].astype(bf16), w_conv_out[0].astype(bf16), w_out[0].astype(bf16), w_ssm_glu[0].astype(bf16),
              w_ssm_out[0].astype(bf16), jnp.pad(conv_w[0], ((0, CONV_ROWS - KS), (0, 0)))]
    k_arr = (2 * lax.axis_index("x") + lax.axis_index("y")).astype(jnp.int32).reshape(1)
    w_in_g, w_co_g, w_out_g, w_glu_g, w_so_g, conv_w_g, h, xi, ti, proj = _gather_prep(
        k_arr, shards, x[0], loss_target[0], pre_norm_gain, jnp.asarray(_perm_matrix(), bf16))
    conv_w_f = conv_w_g[:, :KS].transpose(1, 0, 2).reshape(KS, CW)

    small = (conv_b, conv_ln_gain, conv_ln_bias, ssm_lambda_re, ssm_lambda_im, ssm_log_dt, ssm_b_re,
             ssm_b_im, ssm_c_re, ssm_c_im, ssm_d, b_ssm_glu, post_norm_gain)
    loss_part, gx0, dproj, big_grads, small_grads = _fwd_bwd(
        h, xi, ti, _proj_fwd(k_arr, h, w_in_g, proj), conv_w_f, w_co_g.reshape(CW, D), w_glu_g.reshape(SW, SW), w_so_g,
        w_out_g.reshape(D, D), small)

    dw_in, dw_co, dw_out, dw_glu, dw_so, d_conv_w = big_grads
    d_conv_w = jnp.pad(d_conv_w, ((0, CONV_ROWS - KS), (0, 0))).reshape(CONV_ROWS, NCHIP, 256).transpose(1, 0, 2)
    grads = [dw_in] + [g.astype(bf16) for g in (dw_co.reshape(NCHIP, 256, D), dw_out.reshape(NCHIP, 256, D),
                                                  dw_glu.reshape(NCHIP, 128, SW), dw_so, d_conv_w)]
    gs = _pack_small(small_grads, extra=loss_part)
    *recvs, rs = _pair_exchange_list(grads, gs)
    *parts, ps = _pair_sum_list(c.astype(jnp.int32).reshape(1), grads, recvs, gs, rs)
    gxi, dg_pre, *qparts, qs = _x_grad_exchange(dproj, w_in_g, xi, gx0, pre_norm_gain, parts, ps)
    grad_x = _deinterleave(gxi)
    *halves, fs_half = _chip_sum_list(qparts, qs)
    pre_parts, fs = _small_join(dg_pre.reshape(8, 128), fs_half)
    g_big = list(_sibling_join_list(halves))
    g_big[5] = g_big[5][:KS]

    big_w = (w_in[0], w_conv_out[0], w_out[0], w_ssm_glu[0], w_ssm_out[0], conv_w[0])
    big_m = (m_w_in[0], m_w_conv_out[0], m_w_out[0], m_w_ssm_glu[0], m_w_ssm_out[0], m_conv_w[0])
    big_v = (v_w_in[0], v_w_conv_out[0], v_w_out[0], v_w_ssm_glu[0], v_w_ssm_out[0], v_conv_w[0])
    big_names = ("w_in", "w_conv_out", "w_out", "w_ssm_glu", "w_ssm_out", "conv_w")
    res = {}
    upd = [_adamw("adamw_w_in", big_w[0], g_big[0], big_m[0], big_v[0])]
    upd += _adamw_group("adamw_rest", big_w[1:], g_big[1:], big_m[1:], big_v[1:])
    for n, g, (d, m2, v2) in zip(big_names, g_big, upd):
        res[n] = (g[None], d[None], m2[None], v2[None])

    small_m = (m_conv_b, m_conv_ln_gain, m_conv_ln_bias, m_ssm_lambda_re, m_ssm_lambda_im, m_ssm_log_dt,
               m_ssm_b_re, m_ssm_b_im, m_ssm_c_re, m_ssm_c_im, m_ssm_d, m_b_ssm_glu, m_post_norm_gain)
    small_v = (v_conv_b, v_conv_ln_gain, v_conv_ln_bias, v_ssm_lambda_re, v_ssm_lambda_im, v_ssm_log_dt,
               v_ssm_b_re, v_ssm_b_im, v_ssm_c_re, v_ssm_c_im, v_ssm_d, v_b_ssm_glu, v_post_norm_gain)
    sd, sm, sv = _adamw("adamw_small", _pack_small(small), fs, _pack_small(small_m), _pack_small(small_v))
    sg_l, loss = _unpack_small(fs)
    sd_l, _ = _unpack_small(sd)
    sm_l, _ = _unpack_small(sm)
    sv_l, _ = _unpack_small(sv)
    for i, (n, _) in enumerate(_SMALL):
        res[n] = (sg_l[i], sd_l[i], sm_l[i], sv_l[i])
    rows = lambda a: a.reshape(8, 128)
    pre = _adamw_rows(pre_parts, rows(pre_norm_gain), rows(m_pre_norm_gain), rows(v_pre_norm_gain))
    res["pre_norm_gain"] = tuple(a.reshape(1, D) for a in pre)

    order = ("pre_norm_gain", "w_in", "conv_w", "conv_b", "conv_ln_gain", "conv_ln_bias", "w_conv_out", "ssm_lambda_re",
             "ssm_lambda_im", "ssm_log_dt", "ssm_b_re", "ssm_b_im", "ssm_c_re", "ssm_c_im", "ssm_d", "w_ssm_glu",
             "b_ssm_glu", "w_ssm_out", "w_out", "post_norm_gain")
    outs = [loss, grad_x[None]]
    for q in range(4):
        outs.extend(res[n][q] for n in order)
    return tuple(outs)
```

```python
import math

import numpy as np
import jax
import jax.numpy as jnp
from jax import lax
from jax.experimental import pallas as pl
from jax.experimental.pallas import tpu as pltpu

f32 = jnp.float32
bf16 = jnp.bfloat16

D = 1024
CW = 1024
SW = 512
G = 32
H = 16
PST = 64
NS = G * PST
KS = 31
IN_W = 6144
NCHIP = 4
SHARD_W = IN_W // NCHIP
RMS_EPS = 1e-6
LN_EPS = 1e-5
LR, B1, B2, EPS, WD, STEP = 0.001, 0.9, 0.999, 1e-08, 0.01, 10
GELU_K0 = math.sqrt(2.0 / math.pi)
GELU_K1 = 0.044715

TC = 512
R = TC // 8
NH = 32
LBW = 1024
CONV_ROWS = 64
SMALL_ROWS = 1152
VMEM_LIMIT = 56 * 1024 * 1024
MESH = pl.DeviceIdType.MESH


def _cp(*sem):
    return pltpu.CompilerParams(dimension_semantics=tuple(sem), vmem_limit_bytes=VMEM_LIMIT)


def _sig(v):
    return 0.5 * jnp.tanh(0.5 * v) + 0.5


def _dot(a, b):
    return jnp.dot(a, b, preferred_element_type=f32)


def _dot_nt(a, b):
    return lax.dot_general(a, b, (((1,), (1,)), ((), ())), preferred_element_type=f32)


def _dot_tn(a, b):
    return lax.dot_general(a, b, (((0,), (0,)), ((), ())), preferred_element_type=f32)


def _full(shape):
    nd = len(shape)
    return pl.BlockSpec(shape, lambda *_: (0,) * nd)


def _rows8(i):
    return pl.ds(pl.multiple_of(i * 8, 8), 8)


def _proj_fwd(k_arr, h, w_in, proj):
    L = h.shape[0]
    tm = min(1024, L)
    def body(_, h_ref, w_ref, __, o_ref):
        o_ref[...] = _dot(h_ref[...], w_ref[0]).astype(bf16)
    shard = lambda j, k: (k[0] + 1 + j) % NCHIP
    grid_spec = pltpu.PrefetchScalarGridSpec(
        num_scalar_prefetch=1, grid=(NCHIP - 1, L // tm),
        in_specs=[pl.BlockSpec((tm, D), lambda j, i, k: (i, 0)),
                  pl.BlockSpec((1, D, SHARD_W), lambda j, i, k: (shard(j, k), 0, 0)), _ANY],
        out_specs=pl.BlockSpec((tm, SHARD_W), lambda j, i, k: (i, shard(j, k))))
    return pl.pallas_call(
        body, grid_spec=grid_spec, out_shape=jax.ShapeDtypeStruct((L, IN_W), bf16),
        input_output_aliases={3: 0},
        name="proj_fwd", compiler_params=_cp("arbitrary", "arbitrary"))(k_arr, h, w_in, proj)


NLB = CW // 128
RPI = 16


def _put_blocked(buf, row0, nrows, v):
    for lb in range(NLB):
        buf[lb, pl.ds(row0, nrows), :] = v[:, lb * 128:(lb + 1) * 128]


def _get_blocked(buf, row0, nrows):
    return jnp.concatenate([buf[lb, pl.ds(row0, nrows), :] for lb in range(NLB)], axis=1)


def _fill_before(ebuf, prev):
    sub = lax.broadcasted_iota(jnp.int32, (8, 128), 0)
    def halo(p, carry):
        for lb in range(NLB):
            cur = ebuf[lb, _rows8(R + p), :]
            ebuf[lb, _rows8(p), :] = jnp.where(sub == 0, pltpu.roll(prev[lb, _rows8(p), :], 1, 0),
                                               pltpu.roll(cur, 1, 0))
        return carry
    lax.fori_loop(0, NH, halo, 0)


def _fir(buf, lb, r, coef, first, flip):
    win = buf[lb, pl.ds(pl.multiple_of(r * 8, 8), (KS + RPI - 1) * 8), :]
    outs = []
    for i in range(RPI):
        acc = [first, None, None, None]
        for k in range(KS):
            o = i + ((KS - 1 - k) if flip else k)
            t = coef[k] * win[8 * o:8 * o + 8, :]
            acc[k % 4] = t if acc[k % 4] is None else acc[k % 4] + t
        outs.append((acc[0] + acc[1]) + (acc[2] + acc[3]))
    return outs


def _conv_fwd(proj, cw, cbias, lng, lnb):
    L = proj.shape[0]
    nc = L // TC
    def body(ca_ref, cb_ref, zc_ref, w_ref, b_ref, g_ref, bb_ref, cu1_ref, ain_ref, ebuf, prev, cacc):
        @pl.when(pl.program_id(0) == 0)
        def _():
            prev[...] = jnp.zeros_like(prev)
        def glu(s, carry):
            rows = pl.ds(pl.multiple_of(s * 64, 64), 64)
            _put_blocked(ebuf, pl.multiple_of(NH * 8 + s * 64, 64), 64,
                         ca_ref[rows, :].astype(f32) * _sig(cb_ref[rows, :].astype(f32)))
            return carry
        lax.fori_loop(0, TC // 64, glu, 0)
        _fill_before(ebuf, prev)
        prev[...] = ebuf[:, R * 8:(NH + R) * 8, :]
        for lb in range(NLB):
            sl = slice(lb * 128, (lb + 1) * 128)
            wk = [jnp.broadcast_to(w_ref[k:k + 1, sl], (8, 128)) for k in range(KS)]
            bias = jnp.broadcast_to(b_ref[:, sl], (8, 128))
            def tap(q, carry, lb=lb, wk=wk, bias=bias):
                r = q * RPI
                for i, o in enumerate(_fir(ebuf, lb, r + (NH - KS + 1), wk, bias, False)):
                    cacc[lb, _rows8(r + i), :] = o
                return carry
            lax.fori_loop(0, R // RPI, tap, 0)
        def norm(s, carry):
            rows = pl.ds(pl.multiple_of(s * 64, 64), 64)
            c1b = _get_blocked(cacc, pl.multiple_of(s * 64, 64), 64).astype(bf16)
            cu1_ref[rows, :] = c1b
            c1 = c1b.astype(f32)
            xc = c1 - jnp.mean(c1, axis=-1, keepdims=True)
            var = jnp.mean(xc * xc, axis=-1, keepdims=True)
            ln = xc * lax.rsqrt(var + LN_EPS) * g_ref[...] + bb_ref[...]
            zc = zc_ref[rows, :].astype(f32)
            ain_ref[rows, :] = ((ln * _sig(ln)) * (zc * _sig(zc))).astype(bf16)
            return carry
        lax.fori_loop(0, TC // 64, norm, 0, unroll=4)

    col = lambda c: pl.BlockSpec((TC, CW), lambda i, c=c: (i, c))
    return pl.pallas_call(
        body, grid=(nc,),
        in_specs=[col(0), col(1), col(2), _full((32, CW)), _full((1, CW)), _full((1, CW)), _full((1, CW))],
        out_specs=[pl.BlockSpec((TC, CW), lambda i: (i, 0)), pl.BlockSpec((TC, CW), lambda i: (i, 0))],
        out_shape=[jax.ShapeDtypeStruct((L, CW), bf16), jax.ShapeDtypeStruct((L, CW), bf16)],
        scratch_shapes=[pltpu.VMEM((NLB, (NH + R) * 8, 128), f32), pltpu.VMEM((NLB, NH * 8, 128), f32),
                        pltpu.VMEM((NLB, TC, 128), f32)],
        name="conv_fwd", compiler_params=_cp("arbitrary"))(proj, proj, proj, cw, cbias, lng, lnb)


def _gelu_parts(y0):
    t = jnp.tanh(GELU_K0 * (y0 + GELU_K1 * y0 * y0 * y0))
    return t, 0.5 * y0 * (1.0 + t)


def _ssm_fwd(proj, bbt_re, bbt_im, ct_re, ct_im, a_re, a_im, apow_re, apow_im, dvec, wglu, bglu):
    L = proj.shape[0]
    nc = L // TC
    def body(u_ref, zs_ref, bre_ref, bim_ref, cre_ref, cim_ref, are_ref, aim_ref, pwr_ref, pwi_ref,
             d_ref, wg_ref, bg_ref, y0_ref, bin_ref, sre, sim, cinr, cini, prev_re, prev_im):
        c = pl.program_id(0)
        @pl.when(c == 0)
        def _():
            prev_re[...] = jnp.zeros_like(prev_re)
            prev_im[...] = jnp.zeros_like(prev_im)
        u = u_ref[...]
        for blk in range(4):
            ub = u[:, 128 * blk:128 * (blk + 1)]
            sre[:, 512 * blk:512 * (blk + 1)] = _dot(ub, bre_ref[blk])
            sim[:, 512 * blk:512 * (blk + 1)] = _dot(ub, bim_ref[blk])
        for lb in range(NS // LBW):
            sl = slice(lb * LBW, (lb + 1) * LBW)
            ar = jnp.broadcast_to(are_ref[:, sl], (8, LBW))
            ai = jnp.broadcast_to(aim_ref[:, sl], (8, LBW))
            def step(r, carry, sl=sl, ar=ar, ai=ai):
                sr, si = carry
                nr = ar * sr - ai * si + sre[_rows8(r), sl]
                ni = ar * si + ai * sr + sim[_rows8(r), sl]
                sre[_rows8(r), sl] = nr
                sim[_rows8(r), sl] = ni
                return nr, ni
            lax.fori_loop(1, R, step, (sre[0:8, sl], sim[0:8, sl]))
        a_r = pwr_ref[R - 1:R, :]
        a_i = pwi_ref[R - 1:R, :]
        cr = prev_re[0:1, :]
        ci = prev_im[0:1, :]
        for seg in range(8):
            cinr[seg:seg + 1, :] = cr
            cini[seg:seg + 1, :] = ci
            er = sre[8 * (R - 1) + seg:8 * (R - 1) + seg + 1, :]
            ei = sim[8 * (R - 1) + seg:8 * (R - 1) + seg + 1, :]
            cr, ci = er + a_r * cr - a_i * ci, ei + a_r * ci + a_i * cr
        prev_re[0:1, :] = cr
        prev_im[0:1, :] = ci
        for lb in range(NS // LBW):
            sl = slice(lb * LBW, (lb + 1) * LBW)
            kr = cinr[:, sl]
            ki = cini[:, sl]
            def fix(r, carry, sl=sl, kr=kr, ki=ki):
                pr = jnp.broadcast_to(pwr_ref[pl.ds(r, 1), sl], (8, LBW))
                pi = jnp.broadcast_to(pwi_ref[pl.ds(r, 1), sl], (8, LBW))
                sre[_rows8(r), sl] = sre[_rows8(r), sl] + pr * kr - pi * ki
                sim[_rows8(r), sl] = sim[_rows8(r), sl] + pr * ki + pi * kr
                return carry
            lax.fori_loop(0, R, fix, 0, unroll=2)
        yp = []
        for blk in range(4):
            sr = sre[:, 512 * blk:512 * (blk + 1)].astype(bf16)
            si = sim[:, 512 * blk:512 * (blk + 1)].astype(bf16)
            yp.append(_dot(sr, cre_ref[blk]) - _dot(si, cim_ref[blk]))
        y0 = jnp.concatenate(yp, axis=1) + d_ref[...] * u.astype(f32)
        y0_ref[...] = y0
        _, y1 = _gelu_parts(y0)
        glu = _dot(y1.astype(bf16), wg_ref[...]) + bg_ref[...]
        y2 = y1 * _sig(glu)
        zs = zs_ref[...].astype(f32)
        bin_ref[...] = (y2 * (zs * _sig(zs))).astype(bf16)

    return pl.pallas_call(
        body, grid=(nc,),
        in_specs=[pl.BlockSpec((TC, SW), lambda c: (c, 6)), pl.BlockSpec((TC, SW), lambda c: (c, 7)),
                  _full((4, 128, 512)), _full((4, 128, 512)), _full((4, 512, 128)), _full((4, 512, 128)),
                  _full((1, NS)), _full((1, NS)), _full((R, NS)), _full((R, NS)),
                  _full((1, SW)), _full((SW, SW)), _full((1, SW))],
        out_specs=[pl.BlockSpec((TC, SW), lambda c: (c, 0)), pl.BlockSpec((TC, SW), lambda c: (c, 0)),
                   pl.BlockSpec((TC, NS), lambda c: (c, 0)), pl.BlockSpec((TC, NS), lambda c: (c, 0)),
                   pl.BlockSpec((8, NS), lambda c: (c, 0)), pl.BlockSpec((8, NS), lambda c: (c, 0))],
        out_shape=[jax.ShapeDtypeStruct((L, SW), f32), jax.ShapeDtypeStruct((L, SW), bf16),
                   jax.ShapeDtypeStruct((L, NS), f32), jax.ShapeDtypeStruct((L, NS), f32),
                   jax.ShapeDtypeStruct((nc * 8, NS), f32), jax.ShapeDtypeStruct((nc * 8, NS), f32)],
        scratch_shapes=[pltpu.VMEM((8, NS), f32), pltpu.VMEM((8, NS), f32)],
        name="ssm_fwd", compiler_params=_cp("arbitrary"))(
            proj, proj, bbt_re, bbt_im, ct_re, ct_im, a_re, a_im, apow_re, apow_im, dvec, wglu, bglu)


def _tail(a_in, b_in, proj, x, tgt, wco, wso, wout, gpost):
    L = x.shape[0]
    tm = 512
    def body(a_ref, b_ref, gc_ref, gs_ref, x_ref, t_ref, wco_ref, wso_ref, wout_ref, gp_ref,
             gx_ref, dain_ref, dbin_ref, dp_ref, dwout_ref, dwco_ref, dwso_ref, dgp_ref, loss_ref):
        @pl.when(pl.program_id(0) == 0)
        def _():
            dwout_ref[...] = jnp.zeros_like(dwout_ref)
            dwco_ref[...] = jnp.zeros_like(dwco_ref)
            dwso_ref[...] = jnp.zeros_like(dwso_ref)
            dgp_ref[...] = jnp.zeros_like(dgp_ref)
            loss_ref[...] = jnp.zeros_like(loss_ref)
        a = a_ref[...]
        b = b_ref[...]
        co = _dot(a, wco_ref[...])
        so = jnp.concatenate([_dot(b, wso_ref[j]) for j in range(NCHIP)], axis=1)
        sc = _sig(gc_ref[...].astype(f32))
        ss = _sig(gs_ref[...].astype(f32))
        mb = (sc * co + ss * so).astype(bf16)
        out = _dot(mb, wout_ref[...])
        r2 = lax.rsqrt(jnp.mean(out * out, axis=-1, keepdims=True) + RMS_EPS)
        on = out * r2
        gp = gp_ref[...]
        e = x_ref[...] + on * gp - t_ref[...]
        loss_ref[...] += (0.5 / D) * jnp.sum(e * e)
        dy = e * (1.0 / D)
        gx_ref[...] = dy
        dgp_ref[...] += jnp.sum(dy * on, axis=0, keepdims=True)
        dn = dy * gp
        dout = (r2 * (dn - on * jnp.mean(dn * on, axis=-1, keepdims=True))).astype(bf16)
        dwout_ref[...] += _dot_tn(mb, dout)
        dm = _dot_nt(dout, wout_ref[...])
        dp_ref[:, 0:D] = (dm * co * sc * (1.0 - sc)).astype(bf16)
        dp_ref[:, D:2 * D] = (dm * so * ss * (1.0 - ss)).astype(bf16)
        dco = (dm * sc).astype(bf16)
        dso = (dm * ss).astype(bf16)
        dwco_ref[...] += _dot_tn(a, dco)
        dbin = None
        for j in range(NCHIP):
            dso_j = dso[:, j * 256:(j + 1) * 256]
            dwso_ref[j] += _dot_tn(b, dso_j)
            t = _dot_nt(dso_j, wso_ref[j])
            dbin = t if dbin is None else dbin + t
        dain_ref[...] = _dot_nt(dco, wco_ref[...]).astype(bf16)
        dbin_ref[...] = dbin.astype(bf16)

    row = lambda w: pl.BlockSpec((tm, w), lambda i: (i, 0))
    one = lambda shape: pl.BlockSpec(shape, lambda i: (0,) * len(shape), pipeline_mode=pl.Buffered(1))
    return pl.pallas_call(
        body, grid=(L // tm,),
        in_specs=[row(CW), row(SW), pl.BlockSpec((tm, D), lambda i: (i, 4)), pl.BlockSpec((tm, D), lambda i: (i, 5)),
                  row(D), row(D), one((CW, D)), one((NCHIP, SW, 256)), one((D, D)), one((1, D))],
        out_specs=[row(D), row(CW), row(SW), pl.BlockSpec((tm, 2 * D), lambda i: (i, 2)),
                   one((D, D)), one((CW, D)), one((NCHIP, SW, 256)), one((1, D)), one((1, 128))],
        out_shape=[jax.ShapeDtypeStruct((L, D), f32), jax.ShapeDtypeStruct((L, CW), bf16),
                   jax.ShapeDtypeStruct((L, SW), bf16), jax.ShapeDtypeStruct((L, IN_W), bf16),
                   jax.ShapeDtypeStruct((D, D), f32), jax.ShapeDtypeStruct((CW, D), f32),
                   jax.ShapeDtypeStruct((NCHIP, SW, 256), f32), jax.ShapeDtypeStruct((1, D), f32),
                   jax.ShapeDtypeStruct((1, 128), f32)],
        name="tail", compiler_params=_cp("arbitrary"))(a_in, b_in, proj, proj, x, tgt, wco, wso, wout, gpost)


def _ssm_bwd(d_bin, y0, proj, sre, sim, cinr, cini, bbt_re, bbt_im, ct_re, ct_im,
             a_re, a_im, apow_re, apow_im, dvec, wglu, bglu, dproj):
    L = y0.shape[0]
    nc = L // TC
    def body(dbin_ref, y0_ref, u_ref, zs_ref, sre_ref, sim_ref, cinr_ref, cini_ref,
             bre_ref, bim_ref, cre_ref, cim_ref, are_ref, aim_ref, pwr_ref, pwi_ref, d_ref, wg_ref, bg_ref, _,
             dp_ref, dbre_ref, dbim_ref, dcre_ref, dcim_ref, dd_ref, dar_ref, dai_ref, dwg_ref, dbg_ref,
             gre, gim, gcr, gci, nxt_re, nxt_im):
        @pl.when(pl.program_id(0) == 0)
        def _():
            for ref in (dbre_ref, dbim_ref, dcre_ref, dcim_ref, dd_ref, dar_ref, dai_ref, dwg_ref, dbg_ref,
                        nxt_re, nxt_im):
                ref[...] = jnp.zeros_like(ref)
        y0 = y0_ref[...]
        u = u_ref[...]
        zs = zs_ref[...].astype(f32)
        dbin = dbin_ref[...].astype(f32)
        t, y1 = _gelu_parts(y0)
        y1b = y1.astype(bf16)
        sg = _sig(_dot(y1b, wg_ref[...]) + bg_ref[...])
        sz = _sig(zs)
        d_y2 = dbin * (zs * sz)
        dp_ref[:, SW:2 * SW] = (dbin * (y1 * sg) * (sz * (1.0 + zs * (1.0 - sz)))).astype(bf16)
        d_glu = d_y2 * y1 * sg * (1.0 - sg)
        d_glub = d_glu.astype(bf16)
        d_y1 = d_y2 * sg + _dot_nt(d_glub, wg_ref[...])
        dwg_ref[...] += _dot_tn(y1b, d_glub)
        dbg_ref[...] += jnp.sum(d_glu, axis=0, keepdims=True)
        dgelu = 0.5 * (1.0 + t) + 0.5 * y0 * (1.0 - t * t) * GELU_K0 * (1.0 + 3.0 * GELU_K1 * y0 * y0)
        d_y0 = d_y1 * dgelu
        dd_ref[...] += jnp.sum(d_y0 * u.astype(f32), axis=0, keepdims=True)
        dyb = d_y0.astype(bf16)
        for blk in range(4):
            dy1 = dyb[:, 128 * blk:128 * (blk + 1)]
            gre[:, 512 * blk:512 * (blk + 1)] = _dot_nt(dy1, cre_ref[blk])
            gim[:, 512 * blk:512 * (blk + 1)] = -_dot_nt(dy1, cim_ref[blk])
        for lb in range(NS // LBW):
            sl = slice(lb * LBW, (lb + 1) * LBW)
            ar = jnp.broadcast_to(are_ref[:, sl], (8, LBW))
            ai = jnp.broadcast_to(aim_ref[:, sl], (8, LBW))
            def step(k, carry, sl=sl, ar=ar, ai=ai):
                gr, gi = carry
                row = _rows8(R - 2 - k)
                nr = ar * gr + ai * gi + gre[row, sl]
                ni = ar * gi - ai * gr + gim[row, sl]
                gre[row, sl] = nr
                gim[row, sl] = ni
                return nr, ni
            lax.fori_loop(0, R - 1, step, (gre[8 * (R - 1):8 * R, sl], gim[8 * (R - 1):8 * R, sl]))
        a_r = pwr_ref[R - 1:R, :]
        a_i = pwi_ref[R - 1:R, :]
        cr = nxt_re[0:1, :]
        ci = nxt_im[0:1, :]
        for seg in range(7, -1, -1):
            gcr[seg:seg + 1, :] = cr
            gci[seg:seg + 1, :] = ci
            er = gre[seg:seg + 1, :]
            ei = gim[seg:seg + 1, :]
            cr, ci = er + a_r * cr + a_i * ci, ei + a_r * ci - a_i * cr
        nxt_re[0:1, :] = cr
        nxt_im[0:1, :] = ci
        for lb in range(NS // LBW):
            sl = slice(lb * LBW, (lb + 1) * LBW)
            kr = gcr[:, sl]
            ki = gci[:, sl]
            def fixed(rows, prow, sl=sl, kr=kr, ki=ki):
                pr = jnp.broadcast_to(pwr_ref[prow, sl], (8, LBW))
                pi = jnp.broadcast_to(pwi_ref[prow, sl], (8, LBW))
                gr = gre[rows, sl] + pr * kr + pi * ki
                gi = gim[rows, sl] + pr * ki - pi * kr
                gre[rows, sl] = gr
                gim[rows, sl] = gi
                return gr, gi
            g0r, g0i = fixed(slice(0, 8), slice(R - 1, R))
            p0r, p0i = cinr_ref[:, sl], cini_ref[:, sl]
            acc0 = (g0r * p0r + g0i * p0i, g0i * p0r - g0r * p0i)
            def dacc(r, carry, sl=sl, fixed=fixed):
                xr, xi = carry
                gr, gi = fixed(_rows8(r), pl.ds(R - 1 - r, 1))
                pr, pi = sre_ref[_rows8(r - 1), sl], sim_ref[_rows8(r - 1), sl]
                return xr + gr * pr + gi * pi, xi + gi * pr - gr * pi
            xr, xi = lax.fori_loop(1, R, dacc, acc0)
            dar_ref[:, sl] += xr
            dai_ref[:, sl] += xi
        dup = []
        for blk in range(4):
            s4 = slice(512 * blk, 512 * (blk + 1))
            s1 = slice(128 * blk, 128 * (blk + 1))
            grb = gre[:, s4].astype(bf16)
            gib = gim[:, s4].astype(bf16)
            dup.append(_dot_nt(grb, bre_ref[blk]) + _dot_nt(gib, bim_ref[blk]))
            dbre_ref[blk] += _dot_tn(u[:, s1], grb)
            dbim_ref[blk] += _dot_tn(u[:, s1], gib)
            dcre_ref[blk] += _dot_tn(dyb[:, s1], sre_ref[:, s4].astype(bf16))
            dcim_ref[blk] -= _dot_tn(dyb[:, s1], sim_ref[:, s4].astype(bf16))
        dp_ref[:, 0:SW] = (jnp.concatenate(dup, axis=1) + d_ref[...] * d_y0).astype(bf16)

    rev = lambda w, cidx: pl.BlockSpec((TC, w), lambda i, cidx=cidx: (nc - 1 - i, cidx))
    one = lambda shape: pl.BlockSpec(shape, lambda i: (0,) * len(shape))
    return pl.pallas_call(
        body, grid=(nc,),
        in_specs=[rev(SW, 0), rev(SW, 0), rev(SW, 6), rev(SW, 7), rev(NS, 0), rev(NS, 0),
                  pl.BlockSpec((8, NS), lambda i: (nc - 1 - i, 0)), pl.BlockSpec((8, NS), lambda i: (nc - 1 - i, 0)),
                  one((4, 128, 512)), one((4, 128, 512)), one((4, 512, 128)), one((4, 512, 128)),
                  one((1, NS)), one((1, NS)), one((R, NS)), one((R, NS)),
                  one((1, SW)), one((SW, SW)), one((1, SW)), _ANY],
        out_specs=[pl.BlockSpec((TC, 2 * SW), lambda i: (nc - 1 - i, 3)),
                   one((4, 128, 512)), one((4, 128, 512)), one((4, 128, 512)), one((4, 128, 512)),
                   one((1, SW)), one((8, NS)), one((8, NS)), one((SW, SW)), one((1, SW))],
        out_shape=[jax.ShapeDtypeStruct((L, IN_W), bf16),
                   jax.ShapeDtypeStruct((4, 128, 512), f32), jax.ShapeDtypeStruct((4, 128, 512), f32),
                   jax.ShapeDtypeStruct((4, 128, 512), f32), jax.ShapeDtypeStruct((4, 128, 512), f32),
                   jax.ShapeDtypeStruct((1, SW), f32), jax.ShapeDtypeStruct((8, NS), f32),
                   jax.ShapeDtypeStruct((8, NS), f32), jax.ShapeDtypeStruct((SW, SW), f32),
                   jax.ShapeDtypeStruct((1, SW), f32)],
        scratch_shapes=[pltpu.VMEM((TC, NS), f32), pltpu.VMEM((TC, NS), f32), pltpu.VMEM((8, NS), f32),
                        pltpu.VMEM((8, NS), f32), pltpu.VMEM((8, NS), f32), pltpu.VMEM((8, NS), f32)],
        input_output_aliases={19: 0},
        name="ssm_bwd", compiler_params=_cp("arbitrary"))(
            d_bin, y0, proj, proj, sre, sim, cinr, cini, bbt_re, bbt_im, ct_re, ct_im,
            a_re, a_im, apow_re, apow_im, dvec, wglu, bglu, dproj)


def _conv_bwd(d_ain, cu1, proj, cw, lng, lnb, dproj):
    L = cu1.shape[0]
    nc = L // TC
    def body(dain_ref, cu1_ref, ca_ref, cb_ref, zc_ref, cah_ref, cbh_ref, w_ref, g_ref, bb_ref, _,
             dp_ref, dw_ref, dbias_ref, dlng_ref, dlnb_ref, dbuf, ebuf, prev, nxt, dcu0):
        i = pl.program_id(0)
        @pl.when(i == 0)
        def _():
            dw_ref[...] = jnp.zeros_like(dw_ref)
            dbias_ref[...] = jnp.zeros_like(dbias_ref)
            dlng_ref[...] = jnp.zeros_like(dlng_ref)
            dlnb_ref[...] = jnp.zeros_like(dlnb_ref)
            nxt[...] = jnp.zeros_like(nxt)
        def lnb(s, carry):
            rows = pl.ds(pl.multiple_of(s * 32, 32), 32)
            dain = dain_ref[rows, :].astype(f32)
            c1 = cu1_ref[rows, :].astype(f32)
            zc = zc_ref[rows, :].astype(f32)
            xc = c1 - jnp.mean(c1, axis=-1, keepdims=True)
            var = jnp.mean(xc * xc, axis=-1, keepdims=True)
            rstd = lax.rsqrt(var + LN_EPS)
            xh = xc * rstd
            ln = xh * g_ref[...] + bb_ref[...]
            sl_ = _sig(ln)
            sz = _sig(zc)
            dp_ref[rows, 2 * CW:3 * CW] = (dain * (ln * sl_) * (sz * (1.0 + zc * (1.0 - sz)))).astype(bf16)
            d_ln = dain * (zc * sz) * (sl_ * (1.0 + ln * (1.0 - sl_)))
            dlng_ref[...] += jnp.sum(d_ln * xh, axis=0, keepdims=True)
            dlnb_ref[...] += jnp.sum(d_ln, axis=0, keepdims=True)
            dxh = d_ln * g_ref[...]
            d_c1 = rstd * (dxh - jnp.mean(dxh, axis=-1, keepdims=True)
                           - xh * jnp.mean(dxh * xh, axis=-1, keepdims=True))
            dbias_ref[...] += jnp.sum(d_c1, axis=0, keepdims=True)
            _put_blocked(dbuf, pl.multiple_of(s * 32, 32), 32, d_c1)
            _put_blocked(ebuf, pl.multiple_of(NH * 8 + s * 32, 32), 32,
                         ca_ref[rows, :].astype(f32) * _sig(cb_ref[rows, :].astype(f32)))
            return carry
        lax.fori_loop(0, TC // 32, lnb, 0, unroll=4)
        sub = lax.broadcasted_iota(jnp.int32, (8, 128), 0)
        def after(p, carry):
            for lb in range(NLB):
                cur = dbuf[lb, _rows8(p), :]
                dbuf[lb, _rows8(R + p), :] = jnp.where(sub == 7, pltpu.roll(nxt[lb, _rows8(p), :], 7, 0),
                                                       pltpu.roll(cur, 7, 0))
            return carry
        lax.fori_loop(0, NH, after, 0)
        nxt[...] = dbuf[:, 0:NH * 8, :]
        def before(s, carry):
            rows = pl.ds(pl.multiple_of(s * 64, 64), 64)
            v = cah_ref[rows, :].astype(f32) * _sig(cbh_ref[rows, :].astype(f32))
            _put_blocked(prev, pl.multiple_of(s * 64, 64), 64, jnp.where(i == nc - 1, jnp.zeros_like(v), v))
            return carry
        lax.fori_loop(0, NH * 8 // 64, before, 0)
        _fill_before(ebuf, prev)
        for lb in range(NLB):
            sl = slice(lb * 128, (lb + 1) * 128)
            wk = [jnp.broadcast_to(w_ref[k:k + 1, sl], (8, 128)) for k in range(KS)]
            def tap(q, carry, lb=lb, wk=wk):
                r = q * RPI
                for j, o in enumerate(_fir(dbuf, lb, r, wk, None, True)):
                    dcu0[lb, _rows8(r + j), :] = o
                return carry
            lax.fori_loop(0, R // RPI, tap, 0)
            def wgrad(q, accs, lb=lb):
                r = q * RPI
                dvs = dbuf[lb, pl.ds(pl.multiple_of(r * 8, 8), RPI * 8), :]
                win = ebuf[lb, pl.ds(pl.multiple_of((r + (NH - KS + 1)) * 8, 8), (KS + RPI - 1) * 8), :]
                accs = list(accs)
                for j in range(RPI):
                    dv = dvs[8 * j:8 * j + 8, :]
                    for k in range(KS):
                        accs[k] = accs[k] + dv * win[8 * (j + k):8 * (j + k) + 8, :]
                return tuple(accs)
            accs = lax.fori_loop(0, R // RPI, wgrad, tuple(jnp.zeros((8, 128), f32) for _ in range(KS)))
            for k in range(KS):
                dw_ref[k, :, sl] += accs[k]
        def glub(s, carry):
            rows = pl.ds(pl.multiple_of(s * 64, 64), 64)
            d0 = _get_blocked(dcu0, pl.multiple_of(s * 64, 64), 64)
            ca = ca_ref[rows, :].astype(f32)
            sb = _sig(cb_ref[rows, :].astype(f32))
            dp_ref[rows, 0:CW] = (d0 * sb).astype(bf16)
            dp_ref[rows, CW:2 * CW] = (d0 * ca * sb * (1.0 - sb)).astype(bf16)
            return carry
        lax.fori_loop(0, TC // 64, glub, 0)

    hrows = NH * 8
    per = TC // hrows
    rev = lambda cidx: pl.BlockSpec((TC, CW), lambda i, cidx=cidx: (nc - 1 - i, cidx))
    halo = lambda cidx: pl.BlockSpec((hrows, CW), lambda i, cidx=cidx: (jnp.maximum((nc - 1 - i) * per - 1, 0), cidx))
    one = lambda shape: pl.BlockSpec(shape, lambda i: (0,) * len(shape))
    return pl.pallas_call(
        body, grid=(nc,),
        in_specs=[rev(0), rev(0), rev(0), rev(1), rev(2), halo(0), halo(1), one((32, CW)), one((1, CW)), one((1, CW)),
                  _ANY],
        out_specs=[pl.BlockSpec((TC, 3 * CW), lambda i: (nc - 1 - i, 0)), one((32, 8, CW)), one((1, CW)), one((1, CW)), one((1, CW))],
        out_shape=[jax.ShapeDtypeStruct((L, IN_W), bf16), jax.ShapeDtypeStruct((32, 8, CW), f32),
                   jax.ShapeDtypeStruct((1, CW), f32), jax.ShapeDtypeStruct((1, CW), f32),
                   jax.ShapeDtypeStruct((1, CW), f32)],
        scratch_shapes=[pltpu.VMEM((NLB, (R + NH) * 8, 128), f32), pltpu.VMEM((NLB, (NH + R) * 8, 128), f32),
                        pltpu.VMEM((NLB, hrows, 128), f32), pltpu.VMEM((NLB, hrows, 128), f32),
                        pltpu.VMEM((NLB, TC, 128), f32)],
        input_output_aliases={10: 0},
        name="conv_bwd", compiler_params=_cp("arbitrary"))(d_ain, cu1, proj, proj, proj, proj, proj, cw, lng, lnb, dproj)


def _win_grad(h, dproj):
    L = h.shape[0]
    tm = min(1024, L)
    nt = L // tm
    def body(h_ref, d_ref, o_ref, acc):
        i = pl.program_id(1)
        @pl.when(i == 0)
        def _():
            acc[...] = jnp.zeros_like(acc)
        acc[...] += _dot_tn(h_ref[...], d_ref[...])
        @pl.when(i == nt - 1)
        def _():
            o_ref[0] = acc[...].astype(bf16)
    return pl.pallas_call(
        body, grid=(NCHIP, nt),
        in_specs=[pl.BlockSpec((tm, D), lambda j, i: (i, 0)), pl.BlockSpec((tm, SHARD_W), lambda j, i: (i, j))],
        out_specs=pl.BlockSpec((1, D, SHARD_W), lambda j, i: (j, 0, 0)),
        out_shape=jax.ShapeDtypeStruct((NCHIP, D, SHARD_W), bf16),
        scratch_shapes=[pltpu.VMEM((D, SHARD_W), f32)],
        name="win_grad", compiler_params=_cp("arbitrary", "arbitrary"))(h, dproj)


def _adamw_math(w, g, m, v):
    m2 = B1 * m + (1.0 - B1) * g
    v2 = B2 * v + (1.0 - B2) * (g * g)
    m_hat = m2 / (1.0 - B1 ** STEP)
    v_hat = v2 / (1.0 - B2 ** STEP)
    delta = -LR * (m_hat / (jnp.sqrt(v_hat) + EPS) + WD * w)
    return delta, m2, v2


def _adamw(name, w, g, m, v):
    rows, cols = w.shape
    tm = rows if rows <= 256 else (256 if rows % 256 == 0 else 128)
    assert rows % tm == 0
    def body(w_ref, g_ref, m_ref, v_ref, d_ref, m2_ref, v2_ref):
        d, m2, v2 = _adamw_math(w_ref[...], g_ref[...], m_ref[...], v_ref[...])
        d_ref[...] = d
        m2_ref[...] = m2
        v2_ref[...] = v2
    spec = pl.BlockSpec((tm, cols), lambda i: (i, 0))
    shp = jax.ShapeDtypeStruct((rows, cols), f32)
    return pl.pallas_call(
        body, grid=(rows // tm,), in_specs=[spec] * 4, out_specs=[spec] * 3, out_shape=[shp] * 3,
        name=name, compiler_params=_cp("arbitrary"))(w, g, m, v)


def _adamw_group(name, ws, gs, ms, vs):
    n = len(ws)
    def body(*refs):
        for i in range(n):
            w_ref, g_ref, m_ref, v_ref = (refs[q * n + i] for q in range(4))
            d, m2, v2 = _adamw_math(w_ref[...], g_ref[...], m_ref[...], v_ref[...])
            for q, val in enumerate((d, m2, v2)):
                refs[(4 + q) * n + i][...] = val
    shapes = [jax.ShapeDtypeStruct(w.shape, f32) for w in ws]
    out = pl.pallas_call(body, out_shape=shapes * 3, name=name,
                         compiler_params=pltpu.CompilerParams(vmem_limit_bytes=VMEM_LIMIT))(*ws, *gs, *ms, *vs)
    return [(out[i], out[n + i], out[2 * n + i]) for i in range(n)]


_ANY = pl.BlockSpec(memory_space=pl.ANY)


def _chunks(rows, parts):
    step = rows // parts
    assert step * parts == rows and step % 16 == 0
    return [(i * step, step) for i in range(parts)]


def _place():
    x, y, c = lax.axis_index("x"), lax.axis_index("y"), lax.axis_index("c")
    chips = [(1 - x, y), (x, 1 - y), (1 - x, 1 - y)]
    return x, y, c, chips


def _nchunks(half, cols, itemsize):
    return 4 if half * cols * itemsize >= (1 << 20) else 1


def _segments(metas):
    segs = []
    for w, (half, cols, dt) in enumerate(metas):
        for r0, n in _chunks(half, _nchunks(half, cols, jnp.dtype(dt).itemsize)):
            segs.append((w, half, r0, n))
    return segs


def _rcopy(i, src, dst, send_sems, recv_sems, to):
    return pltpu.make_async_remote_copy(src_ref=src, dst_ref=dst, send_sem=send_sems.at[i], recv_sem=recv_sems.at[i],
                                        device_id=to, device_id_type=MESH)


def _gather_prep(k_arr, shards, x, tgt, g_pre, perm):
    na = len(shards)
    L = x.shape[0]
    nc = L // TC
    segs = _segments([(a.shape[0] // 2, a.shape[1], a.dtype) for a in shards])
    ns = len(segs)
    def body(_, *refs):
        ins = refs[:na]
        x_ref, t_ref, g_ref, p_ref = refs[na:na + 4]
        outs = refs[na + 4:2 * na + 4]
        h_ref, xi_ref, ti_ref, proj_ref = refs[2 * na + 4:2 * na + 8]
        stages = refs[2 * na + 8:3 * na + 8]
        send_sems, recv_sems, local_sems = refs[3 * na + 8:]
        i = pl.program_id(0)
        x, y, c, chips = _place()
        k = 2 * x + y
        me, sibling = (x, y, c), (x, y, 1 - c)

        def dst(w, half, chip, pc, r0, n):
            return outs[w].at[chip, pl.ds(pc * half + r0, n), :]

        def firsts():
            return [_rcopy(j * ns + s, ins[w].at[pl.ds(c * half + r0, n), :], dst(w, half, k, c, r0, n),
                           send_sems, recv_sems, (*chip, c))
                    for j, chip in enumerate(chips) for s, (w, half, r0, n) in enumerate(segs)]

        def own_out(w):
            return pltpu.make_async_copy(stages[w], outs[w].at[k], local_sems.at[w])

        @pl.when(i == 0)
        def _():
            for cp in firsts():
                cp.start()
            cins = [pltpu.make_async_copy(ins[w], stages[w], local_sems.at[w]) for w in range(na)]
            for cp in cins:
                cp.start()
            for w in range(na):
                cins[w].wait()
                own_out(w).start()

        p = p_ref[...]
        def through(v):
            hi = v.astype(bf16)
            r1 = v - hi.astype(f32)
            mid = r1.astype(bf16)
            lo = (r1 - mid.astype(f32)).astype(bf16)
            return (_dot(p, hi) + _dot(p, mid)) + _dot(p, lo)
        xt = x_ref[...]
        r = lax.rsqrt(jnp.mean(xt * xt, axis=-1, keepdims=True) + RMS_EPS)
        hp = _dot(p, (xt * r * g_ref[...]).astype(bf16)).astype(bf16)
        h_ref[...] = hp
        proj_ref[...] = _dot(hp, stages[0][...]).astype(bf16)
        xi_ref[...] = through(xt)
        ti_ref[...] = through(t_ref[...])

        @pl.when(i == nc - 1)
        def _():
            passed = []
            for j, chip in enumerate(chips):
                cj = 2 * chip[0] + chip[1]
                for s, (w, half, r0, n) in enumerate(segs):
                    landed = dst(w, half, cj, c, r0, n)
                    _rcopy(j * ns + s, landed, landed, send_sems, recv_sems, me).wait_recv()
                    fwd = _rcopy(3 * ns + j * ns + s, landed, landed, send_sems, recv_sems, sibling)
                    fwd.start()
                    passed.append(fwd)
            for j, chip in enumerate(chips):
                cj = 2 * chip[0] + chip[1]
                for s, (w, half, r0, n) in enumerate(segs):
                    theirs = dst(w, half, cj, 1 - c, r0, n)
                    _rcopy(3 * ns + j * ns + s, theirs, theirs, send_sems, recv_sems, me).wait_recv()
            for cp in firsts() + passed:
                cp.wait_send()
            for w in range(na):
                own_out(w).wait()

    row = lambda: pl.BlockSpec((TC, D), lambda i, k: (i, 0))
    grid_spec = pltpu.PrefetchScalarGridSpec(
        num_scalar_prefetch=1, grid=(nc,),
        in_specs=[_ANY] * na + [row(), row(), pl.BlockSpec((1, D), lambda i, k: (0, 0)),
                                pl.BlockSpec((TC, TC), lambda i, k: (0, 0))],
        out_specs=[_ANY] * na + [row(), row(), row(), pl.BlockSpec((TC, SHARD_W), lambda i, k: (i, k[0]))],
        scratch_shapes=[pltpu.VMEM(a.shape, a.dtype) for a in shards]
        + [pltpu.SemaphoreType.DMA((6 * ns,)), pltpu.SemaphoreType.DMA((6 * ns,)), pltpu.SemaphoreType.DMA((na,))])
    return pl.pallas_call(
        body, grid_spec=grid_spec,
        out_shape=[jax.ShapeDtypeStruct((NCHIP,) + a.shape, a.dtype) for a in shards]
        + [jax.ShapeDtypeStruct((L, D), bf16), jax.ShapeDtypeStruct((L, D), f32), jax.ShapeDtypeStruct((L, D), f32),
           jax.ShapeDtypeStruct((L, IN_W), bf16)],
        name="gather_prep", compiler_params=_cp("arbitrary"))(k_arr, *shards, x, tgt, g_pre, perm)


def _x_grad_exchange(dproj, w_in, x, gx0, g_pre, parts, small):
    L = x.shape[0]
    tm = 512
    nt = L // tm
    na = len(parts)
    hs = SMALL_ROWS // 2
    segs = _segments([(p.shape[1], p.shape[2], p.dtype) for p in parts])
    ns = len(segs) + 1
    def body(*refs):
        d_ref, w_ref, x_ref, gx_ref, g_ref = refs[:5]
        ins, s_ref = refs[5:5 + na], refs[5 + na]
        o_ref, dg_ref = refs[6 + na:8 + na]
        outs, qs_ref = refs[8 + na:8 + 2 * na], refs[8 + 2 * na]
        stages = refs[9 + 2 * na:10 + 3 * na]
        send_sems, recv_sems, local_sems = refs[10 + 3 * na:]
        i = pl.program_id(0)
        x, y, c, chips = _place()
        k = 2 * x + y

        def my_small():
            return s_ref.at[pl.ds(c * hs, hs), :]

        def copies():
            out = []
            for j, chip in enumerate(chips):
                cj = 2 * chip[0] + chip[1]
                pieces = [(my_small(), qs_ref.at[k])]
                pieces += [(ins[w].at[cj, pl.ds(r0, n), :], outs[w].at[k, pl.ds(r0, n), :]) for w, _, r0, n in segs]
                out += [_rcopy(ns * j + s, src, d, send_sems, recv_sems, (*chip, c)) for s, (src, d) in enumerate(pieces)]
            return out

        def own_out(w):
            dst = qs_ref.at[k] if w == na else outs[w].at[k]
            return pltpu.make_async_copy(stages[w], dst, local_sems.at[w])

        @pl.when(i == 0)
        def _():
            dg_ref[...] = jnp.zeros_like(dg_ref)
            for cp in copies():
                cp.start()
            cins = [pltpu.make_async_copy(my_small() if w == na else ins[w].at[k], stages[w], local_sems.at[w])
                    for w in range(na + 1)]
            for cp in cins:
                cp.start()
            for w in range(na + 1):
                cins[w].wait()
                own_out(w).start()

        dh = _dot_nt(d_ref[:, 0:SHARD_W], w_ref[0])
        for j in range(1, NCHIP):
            dh = dh + _dot_nt(d_ref[:, j * SHARD_W:(j + 1) * SHARD_W], w_ref[j])
        xt = x_ref[...]
        r = lax.rsqrt(jnp.mean(xt * xt, axis=-1, keepdims=True) + RMS_EPS)
        xn = xt * r
        dg_ref[...] += jnp.sum(dh * xn, axis=0, keepdims=True)
        dxn = dh * g_ref[...]
        o_ref[...] = gx_ref[...] + r * (dxn - xn * jnp.mean(dxn * xn, axis=-1, keepdims=True))

        @pl.when(i == nt - 1)
        def _():
            for cp in copies():
                cp.wait_recv()
            for cp in copies():
                cp.wait_send()
            for w in range(na + 1):
                own_out(w).wait()

    return pl.pallas_call(
        body, grid=(nt,),
        in_specs=[pl.BlockSpec((tm, IN_W), lambda i: (i, 0)),
                  pl.BlockSpec((NCHIP, D, SHARD_W), lambda i: (0, 0, 0), pipeline_mode=pl.Buffered(1)),
                  pl.BlockSpec((tm, D), lambda i: (i, 0)), pl.BlockSpec((tm, D), lambda i: (i, 0)), _full((1, D))]
        + [_ANY] * (na + 1),
        out_specs=[pl.BlockSpec((tm, D), lambda i: (i, 0)), _full((1, D))] + [_ANY] * (na + 1),
        out_shape=[jax.ShapeDtypeStruct((L, D), f32), jax.ShapeDtypeStruct((1, D), f32)]
        + [jax.ShapeDtypeStruct(p.shape, bf16) for p in parts] + [jax.ShapeDtypeStruct((NCHIP, hs, 128), f32)],
        scratch_shapes=[pltpu.VMEM(p.shape[1:], bf16) for p in parts] + [pltpu.VMEM((hs, 128), f32)]
        + [pltpu.SemaphoreType.DMA((3 * ns,)), pltpu.SemaphoreType.DMA((3 * ns,)), pltpu.SemaphoreType.DMA((na + 1,))],
        name="x_grad_exchange", compiler_params=_cp("arbitrary"))(dproj, w_in, x, gx0, g_pre, *parts, small)


def _sibling_join_list(halves):
    na = len(halves)
    segs = _segments([(h.shape[0], h.shape[1], h.dtype) for h in halves])
    def body(*refs):
        ins, outs, stages = refs[:na], refs[na:2 * na], refs[2 * na:3 * na]
        send_sems, recv_sems, local_sems = refs[3 * na:]
        x, y, c, _ = _place()
        copies = [_rcopy(i, ins[w].at[pl.ds(r0, n), :], outs[w].at[pl.ds(c * half + r0, n), :], send_sems, recv_sems,
                         (x, y, 1 - c)) for i, (w, half, r0, n) in enumerate(segs)]
        for cp in copies:
            cp.start()
        cins = [pltpu.make_async_copy(ins[w], stages[w], local_sems.at[w]) for w in range(na)]
        for cp in cins:
            cp.start()
        own = []
        for w in range(na):
            cins[w].wait()
            half = halves[w].shape[0]
            own.append(pltpu.make_async_copy(stages[w], outs[w].at[pl.ds(c * half, half), :], local_sems.at[w]))
            own[-1].start()
        for cp in copies:
            cp.wait_recv()
        for cp in copies:
            cp.wait_send()
        for cp in own:
            cp.wait()

    return pl.pallas_call(
        body, in_specs=[_ANY] * na, out_specs=[_ANY] * na,
        out_shape=[jax.ShapeDtypeStruct((2 * h.shape[0], h.shape[1]), f32) for h in halves],
        scratch_shapes=[pltpu.VMEM(h.shape, f32) for h in halves]
        + [pltpu.SemaphoreType.DMA((len(segs),)), pltpu.SemaphoreType.DMA((len(segs),)), pltpu.SemaphoreType.DMA((na,))],
        name="sibling_join")(*halves)


def _small_join(v, fs_half):
    hs = fs_half.shape[0]
    def body(v_ref, h_ref, o_ref, fs_ref, send_sems, recv_sems):
        x, y, c, _ = _place()
        me = 4 * x + 2 * y + c
        o_ref[me] = v_ref[...]
        mine = pl.ds(pl.multiple_of(c * hs, 8), hs)
        fs_ref[mine, :] = h_ref[...]
        copies = [_rcopy(7, h_ref, fs_ref.at[mine, :], send_sems, recv_sems, (x, y, 1 - c))]
        i = 0
        for dx in range(2):
            for dy in range(2):
                for dc in range(2):
                    if dx + dy + dc:
                        copies.append(_rcopy(i, v_ref, o_ref.at[me], send_sems, recv_sems, (x ^ dx, y ^ dy, c ^ dc)))
                        i += 1
        for cp in copies:
            cp.start()
        for cp in copies:
            cp.wait_recv()
        for cp in copies:
            cp.wait_send()

    vm = pl.BlockSpec(memory_space=pltpu.VMEM)
    return pl.pallas_call(
        body, in_specs=[vm, vm], out_specs=[vm, vm],
        out_shape=[jax.ShapeDtypeStruct((8, 8, 128), f32), jax.ShapeDtypeStruct((2 * hs, 128), f32)],
        scratch_shapes=[pltpu.SemaphoreType.DMA((8,)), pltpu.SemaphoreType.DMA((8,))],
        name="small_join")(v, fs_half)


def _adamw_rows(parts, w, m, v):
    def body(p_ref, w_ref, m_ref, v_ref, g_ref, d_ref, m2_ref, v2_ref):
        g = p_ref[0]
        for dvc in range(1, 8):
            g = g + p_ref[dvc]
        g_ref[...] = g
        d, m2, v2 = _adamw_math(w_ref[...], g, m_ref[...], v_ref[...])
        d_ref[...] = d
        m2_ref[...] = m2
        v2_ref[...] = v2
    return pl.pallas_call(body, out_shape=[jax.ShapeDtypeStruct((8, 128), f32)] * 4, name="adamw_pre_norm_gain")(
        parts, w, m, v)


def _pair_exchange_list(grads, small):
    na = len(grads)
    segs = _segments([(g.shape[1] // 2, g.shape[2], g.dtype) for g in grads])
    n = NCHIP * len(segs) + 1
    def body(*refs):
        ins, s_ref, outs, rs_ref, (send_sems, recv_sems) = (refs[:na], refs[na], refs[na + 1:2 * na + 1],
                                                            refs[2 * na + 1], refs[2 * na + 2:])
        x, y, c, _ = _place()
        pieces = [(s_ref, rs_ref)]
        for j in range(NCHIP):
            for w, half, r0, rows in segs:
                pieces.append((ins[w].at[j, pl.ds((1 - c) * half + r0, rows), :], outs[w].at[j, pl.ds(r0, rows), :]))
        copies = [_rcopy(i, s, d, send_sems, recv_sems, (x, y, 1 - c)) for i, (s, d) in enumerate(pieces)]
        for cp in copies:
            cp.start()
        for cp in copies:
            cp.wait_recv()
        for cp in copies:
            cp.wait_send()

    return pl.pallas_call(
        body, in_specs=[_ANY] * (na + 1), out_specs=[_ANY] * (na + 1),
        out_shape=[jax.ShapeDtypeStruct((NCHIP, g.shape[1] // 2, g.shape[2]), g.dtype) for g in grads]
        + [jax.ShapeDtypeStruct((SMALL_ROWS, 128), f32)],
        scratch_shapes=[pltpu.SemaphoreType.DMA((n,)), pltpu.SemaphoreType.DMA((n,))],
        name="pair_exchange")(*grads, small)


def _pair_sum_list(c_arr, grads, recvs, small, rsmall):
    na = len(grads)
    def body(c_ref, *refs):
        g_refs, r_refs, s_ref, rs_ref = refs[:na], refs[na:2 * na], refs[2 * na], refs[2 * na + 1]
        o_refs, os_ref = refs[2 * na + 2:3 * na + 2], refs[3 * na + 2]
        for g_ref, r_ref, o_ref in zip(g_refs, r_refs, o_refs):
            o_ref[...] = (g_ref[...].astype(f32) + r_ref[...].astype(f32)).astype(bf16)
        os_ref[...] = s_ref[...] + rs_ref[...]
    half = lambda g: pl.BlockSpec((1, g.shape[1] // 2, g.shape[2]), lambda j, c: (j, c[0], 0))
    low = lambda g: pl.BlockSpec((1, g.shape[1] // 2, g.shape[2]), lambda j, c: (j, 0, 0))
    sm = pl.BlockSpec((SMALL_ROWS, 128), lambda j, c: (0, 0))
    grid_spec = pltpu.PrefetchScalarGridSpec(
        num_scalar_prefetch=1, grid=(NCHIP,),
        in_specs=[half(g) for g in grads] + [low(g) for g in grads] + [sm, sm],
        out_specs=[low(g) for g in grads] + [sm])
    return pl.pallas_call(
        body, grid_spec=grid_spec,
        out_shape=[jax.ShapeDtypeStruct((NCHIP, g.shape[1] // 2, g.shape[2]), bf16) for g in grads]
        + [jax.ShapeDtypeStruct((SMALL_ROWS, 128), f32)],
        name="pair_sum", compiler_params=_cp("arbitrary"))(c_arr, *grads, *recvs, small, rsmall)


def _chip_sum_list(parts, small):
    na = len(parts)
    nt = 2
    def body(*refs):
        for q_ref, f_ref in zip(refs[:na + 1], refs[na + 1:]):
            acc = q_ref[0].astype(f32)
            for j in range(1, NCHIP):
                acc = acc + q_ref[j].astype(f32)
            f_ref[...] = acc
    arrs = list(parts) + [small]
    return pl.pallas_call(
        body, grid=(nt,),
        in_specs=[pl.BlockSpec((NCHIP, a.shape[1] // nt, a.shape[2]), lambda i: (0, i, 0)) for a in arrs],
        out_specs=[pl.BlockSpec((a.shape[1] // nt, a.shape[2]), lambda i: (i, 0)) for a in arrs],
        out_shape=[jax.ShapeDtypeStruct(a.shape[1:], f32) for a in arrs],
        name="chip_sum", compiler_params=_cp("arbitrary"))(*arrs)


_SMALL =(("conv_b", (1, 1024)), ("conv_ln_gain", (1, 1024)), ("conv_ln_bias", (1, 1024)),
          ("ssm_lambda_re", (1, 32, 64)), ("ssm_lambda_im", (1, 32, 64)), ("ssm_log_dt", (1, 32)),
          ("ssm_b_re", (1, 32, 64, 16)), ("ssm_b_im", (1, 32, 64, 16)), ("ssm_c_re", (1, 32, 16, 64)),
          ("ssm_c_im", (1, 32, 16, 64)), ("ssm_d", (1, 32, 16)), ("b_ssm_glu", (1, 512)), ("post_norm_gain", (1, 1024)))


def _pack_small(vals, extra=None):
    rows = []
    for v in list(vals) + ([extra] if extra is not None else []):
        flat = v.reshape(-1).astype(f32)
        n = -(-flat.shape[0] // 1024) * 1024
        rows.append(jnp.pad(flat, (0, n - flat.shape[0])).reshape(-1, 128))
    used = sum(r.shape[0] for r in rows)
    rows.append(jnp.zeros((SMALL_ROWS - used, 128), f32))
    return jnp.concatenate(rows, axis=0)


def _unpack_small(p):
    o = 0
    out = []
    for _, shape in _SMALL:
        n = int(np.prod(shape))
        nr = -(-n // 1024) * 8
        out.append(p[o:o + nr].reshape(-1)[:n].reshape(shape))
        o += nr
    return out, p[o, 0]


def _discretize(lam_re, lam_im, log_dt, b_re, b_im):
    dt = jnp.exp(log_dt)[:, None]
    mag = jnp.exp(lam_re * dt)
    ar = mag * jnp.cos(lam_im * dt)
    ai = mag * jnp.sin(lam_im * dt)
    den = lam_re * lam_re + lam_im * lam_im
    zr = ((ar - 1.0) * lam_re + ai * lam_im) / den
    zi = (ai * lam_re - (ar - 1.0) * lam_im) / den
    bbr = zr[..., None] * b_re - zi[..., None] * b_im
    bbi = zr[..., None] * b_im + zi[..., None] * b_re
    return ar, ai, bbr, bbi


_EYE8 = np.eye(8, dtype=np.float32)


def _bbt_blocks(bb):
    v = bb.reshape(4, 8, PST, H).transpose(0, 1, 3, 2)
    return jnp.einsum("bghp,gk->bghkp", v, _EYE8).reshape(4, 128, 512)


def _bbt_unblock(m):
    v = jnp.einsum("bghkp,gk->bghp", m.reshape(4, 8, H, 8, PST), _EYE8)
    return v.transpose(0, 1, 3, 2).reshape(G, PST, H)


def _ct_blocks(cc):
    v = cc.reshape(4, 8, H, PST)
    return jnp.einsum("bghp,gk->bgpkh", v, _EYE8).reshape(4, 512, 128)


def _ct_unblock(m):
    return jnp.einsum("bghkp,gk->bghp", m.reshape(4, 8, H, 8, PST), _EYE8).reshape(G, H, PST)


def _perm_matrix():
    p = np.zeros((TC, TC), np.float32)
    for r in range(R):
        for seg in range(8):
            p[r * 8 + seg, seg * R + r] = 1.0
    return p


def _deinterleave(a):
    L, C = a.shape
    return a.reshape(L // TC, R, 8, C).transpose(0, 2, 1, 3).reshape(L, C)


def _fwd_bwd(h, xi, ti, proj, conv_w, w_co, w_glu, w_so, w_out, small):
    (conv_b, ln_g, ln_b, lam_re, lam_im, log_dt, b_re, b_im, c_re, c_im, dvec, b_glu, g_post) = small
    lam_re, lam_im, log_dt = lam_re[0], lam_im[0], log_dt[0]
    b_re, b_im, c_re, c_im = b_re[0], b_im[0], c_re[0], c_im[0]
    (ar, ai, bbr, bbi), disc_vjp = jax.vjp(_discretize, lam_re, lam_im, log_dt, b_re, b_im)
    a_re = ar.reshape(1, NS)
    a_im = ai.reshape(1, NS)
    dt = jnp.exp(log_dt)[:, None]
    steps = jnp.arange(1, R + 1, dtype=f32)[:, None, None]
    apow_re = (jnp.exp(steps * (lam_re * dt)) * jnp.cos(steps * (lam_im * dt))).reshape(R, NS)
    apow_im = (jnp.exp(steps * (lam_re * dt)) * jnp.sin(steps * (lam_im * dt))).reshape(R, NS)
    bbt_re, bbt_im = _bbt_blocks(bbr).astype(bf16), _bbt_blocks(bbi).astype(bf16)
    ct_re, ct_im = _ct_blocks(c_re).astype(bf16), _ct_blocks(c_im).astype(bf16)
    d_row = dvec.reshape(1, SW)
    cw32 = jnp.pad(conv_w, ((0, 1), (0, 0)))

    cu1, a_in = _conv_fwd(proj, cw32, conv_b, ln_g, ln_b)
    y0, b_in, sre, sim, cinr, cini = _ssm_fwd(proj, bbt_re, bbt_im, ct_re, ct_im, a_re, a_im,
                                              apow_re, apow_im, d_row, w_glu, b_glu)
    gx0, d_ain, d_bin, dproj, dw_out, dw_co, dw_so, dg_post, loss = _tail(
        a_in, b_in, proj, xi, ti, w_co, w_so, w_out, g_post)
    (dproj, dbbt_re, dbbt_im, dct_re, dct_im, dd, dar8, dai8, dw_glu, db_glu) = _ssm_bwd(
        d_bin, y0, proj, sre, sim, cinr, cini, bbt_re, bbt_im, ct_re, ct_im,
        a_re, a_im, apow_re, apow_im, d_row, w_glu, b_glu, dproj)
    dproj, dcw8, d_convb, d_lng, d_lnb = _conv_bwd(d_ain, cu1, proj, cw32, ln_g, ln_b, dproj)
    dw_in = _win_grad(h, dproj)

    d_ar = jnp.sum(dar8, axis=0).reshape(G, PST)
    d_ai = jnp.sum(dai8, axis=0).reshape(G, PST)
    d_lre, d_lim, d_ldt, d_bre, d_bim = disc_vjp((d_ar, d_ai, _bbt_unblock(dbbt_re), _bbt_unblock(dbbt_im)))
    d_conv_w = jnp.sum(dcw8, axis=1)[:KS]
    small_grads = [d_convb, d_lng, d_lnb, d_lre[None], d_lim[None], d_ldt[None], d_bre[None], d_bim[None],
                   _ct_unblock(dct_re)[None], _ct_unblock(dct_im)[None], dd.reshape(1, G, H), db_glu, dg_post]
    return loss[0, 0], gx0, dproj, (dw_in, dw_co, dw_out, dw_glu, dw_so, d_conv_w), small_grads


def kernel(x, pre_norm_gain, w_in, conv_w, conv_b, conv_ln_gain, conv_ln_bias, w_conv_out, ssm_lambda_re, ssm_lambda_im, ssm_log_dt, ssm_b_re, ssm_b_im, ssm_c_re, ssm_c_im, ssm_d, w_ssm_glu, b_ssm_glu, w_ssm_out, w_out, post_norm_gain, loss_target, m_pre_norm_gain, m_w_in, m_conv_w, m_conv_b, m_conv_ln_gain, m_conv_ln_bias, m_w_conv_out, m_ssm_lambda_re, m_ssm_lambda_im, m_ssm_log_dt, m_ssm_b_re, m_ssm_b_im, m_ssm_c_re, m_ssm_c_im, m_ssm_d, m_w_ssm_glu, m_b_ssm_glu, m_w_ssm_out, m_w_out, m_post_norm_gain, v_pre_norm_gain, v_w_in, v_conv_w, v_conv_b, v_conv_ln_gain, v_conv_ln_bias, v_w_conv_out, v_ssm_lambda_re, v_ssm_lambda_im, v_ssm_log_dt, v_ssm_b_re, v_ssm_b_im, v_ssm_c_re, v_ssm_c_im, v_ssm_d, v_w_ssm_glu, v_b_ssm_glu, v_w_ssm_out, v_w_out, v_post_norm_gain):
    c = lax.axis_index("c")
    shards = [w_in[0].astype(bf16), w_conv_out[0].astype(bf16), w_out[0].astype(bf16), w_ssm_glu[0].astype(bf16),
              w_ssm_out[0].astype(bf16), jnp.pad(conv_w[0], ((0, CONV_ROWS - KS), (0, 0)))]
    k_arr = (2 * lax.axis_index("x") + lax.axis_index("y")).astype(jnp.int32).reshape(1)
    w_in_g, w_co_g, w_out_g, w_glu_g, w_so_g, conv_w_g, h, xi, ti, proj = _gather_prep(
        k_arr, shards, x[0], loss_target[0], pre_norm_gain, jnp.asarray(_perm_matrix(), bf16))
    conv_w_f = conv_w_g[:, :KS].transpose(1, 0, 2).reshape(KS, CW)

    small = (conv_b, conv_ln_gain, conv_ln_bias, ssm_lambda_re, ssm_lambda_im, ssm_log_dt, ssm_b_re,
             ssm_b_im, ssm_c_re, ssm_c_im, ssm_d, b_ssm_glu, post_norm_gain)
    loss_part, gx0, dproj, big_grads, small_grads = _fwd_bwd(
        h, xi, ti, _proj_fwd(k_arr, h, w_in_g, proj), conv_w_f, w_co_g.reshape(CW, D), w_glu_g.reshape(SW, SW), w_so_g,
        w_out_g.reshape(D, D), small)

    dw_in, dw_co, dw_out, dw_glu, dw_so, d_conv_w = big_grads
    d_conv_w = jnp.pad(d_conv_w, ((0, CONV_ROWS - KS), (0, 0))).reshape(CONV_ROWS, NCHIP, 256).transpose(1, 0, 2)
    grads = [dw_in] + [g.astype(bf16) for g in (dw_co.reshape(NCHIP, 256, D), dw_out.reshape(NCHIP, 256, D),
                                                  dw_glu.reshape(NCHIP, 128, SW), dw_so, d_conv_w)]
    gs = _pack_small(small_grads, extra=loss_part)
    *recvs, rs = _pair_exchange_list(grads, gs)
    *parts, ps = _pair_sum_list(c.astype(jnp.int32).reshape(1), grads, recvs, gs, rs)
    gxi, dg_pre, *qparts, qs = _x_grad_exchange(dproj, w_in_g, xi, gx0, pre_norm_gain, parts, ps)
    grad_x = _deinterleave(gxi)
    *halves, fs_half = _chip_sum_list(qparts, qs)
    pre_parts, fs = _small_join(dg_pre.reshape(8, 128), fs_half)
    g_big = list(_sibling_join_list(halves))
    g_big[5] = g_big[5][:KS]

    big_w = (w_in[0], w_conv_out[0], w_out[0], w_ssm_glu[0], w_ssm_out[0], conv_w[0])
    big_m = (m_w_in[0], m_w_conv_out[0], m_w_out[0], m_w_ssm_glu[0], m_w_ssm_out[0], m_conv_w[0])
    big_v = (v_w_in[0], v_w_conv_out[0], v_w_out[0], v_w_ssm_glu[0], v_w_ssm_out[0], v_conv_w[0])
    big_names = ("w_in", "w_conv_out", "w_out", "w_ssm_glu", "w_ssm_out", "conv_w")
    res = {}
    upd = [_adamw("adamw_w_in", big_w[0], g_big[0], big_m[0], big_v[0])]
    upd += _adamw_group("adamw_rest", big_w[1:], g_big[1:], big_m[1:], big_v[1:])
    for n, g, (d, m2, v2) in zip(big_names, g_big, upd):
        res[n] = (g[None], d[None], m2[None], v2[None])

    small_m = (m_conv_b, m_conv_ln_gain, m_conv_ln_bias, m_ssm_lambda_re, m_ssm_lambda_im, m_ssm_log_dt,
               m_ssm_b_re, m_ssm_b_im, m_ssm_c_re, m_ssm_c_im, m_ssm_d, m_b_ssm_glu, m_post_norm_gain)
    small_v = (v_conv_b, v_conv_ln_gain, v_conv_ln_bias, v_ssm_lambda_re, v_ssm_lambda_im, v_ssm_log_dt,
               v_ssm_b_re, v_ssm_b_im, v_ssm_c_re, v_ssm_c_im, v_ssm_d, v_b_ssm_glu, v_post_norm_gain)
    sd, sm, sv = _adamw("adamw_small", _pack_small(small), fs, _pack_small(small_m), _pack_small(small_v))
    sg_l, loss = _unpack_small(fs)
    sd_l, _ = _unpack_small(sd)
    sm_l, _ = _unpack_small(sm)
    sv_l, _ = _unpack_small(sv)
    for i, (n, _) in enumerate(_SMALL):
        res[n] = (sg_l[i], sd_l[i], sm_l[i], sv_l[i])
    rows = lambda a: a.reshape(8, 128)
    pre = _adamw_rows(pre_parts, rows(pre_norm_gain), rows(m_pre_norm_gain), rows(v_pre_norm_gain))
    res["pre_norm_gain"] = tuple(a.reshape(1, D) for a in pre)

    order = ("pre_norm_gain", "w_in", "conv_w", "conv_b", "conv_ln_gain", "conv_ln_bias", "w_conv_out", "ssm_lambda_re",
             "ssm_lambda_im", "ssm_log_dt", "ssm_b_re", "ssm_b_im", "ssm_c_re", "ssm_c_im", "ssm_d", "w_ssm_glu",
             "b_ssm_glu", "w_ssm_out", "w_out", "post_norm_gain")
    outs = [loss, grad_x[None]]
    for q in range(4):
        outs.extend(res[n][q] for n in order)
    return tuple(outs)
```

```python
import math

import numpy as np
import jax
import jax.numpy as jnp
from jax import lax
from jax.experimental import pallas as pl
from jax.experimental.pallas import tpu as pltpu

f32 = jnp.float32
bf16 = jnp.bfloat16

D = 1024
CW = 1024
SW = 512
G = 32
H = 16
PST = 64
NS = G * PST
KS = 31
IN_W = 6144
NCHIP = 4
SHARD_W = IN_W // NCHIP
RMS_EPS = 1e-6
LN_EPS = 1e-5
LR, B1, B2, EPS, WD, STEP = 0.001, 0.9, 0.999, 1e-08, 0.01, 10
GELU_K0 = math.sqrt(2.0 / math.pi)
GELU_K1 = 0.044715

TC = 512
R = TC // 8
NH = 32
LBW = 1024
CONV_ROWS = 64
SMALL_ROWS = 1152
VMEM_LIMIT = 56 * 1024 * 1024
MESH = pl.DeviceIdType.MESH


def _cp(*sem):
    return pltpu.CompilerParams(dimension_semantics=tuple(sem), vmem_limit_bytes=VMEM_LIMIT)


def _sig(v):
    return 0.5 * jnp.tanh(0.5 * v) + 0.5


def _dot(a, b):
    return jnp.dot(a, b, preferred_element_type=f32)


def _dot_nt(a, b):
    return lax.dot_general(a, b, (((1,), (1,)), ((), ())), preferred_element_type=f32)


def _dot_tn(a, b):
    return lax.dot_general(a, b, (((0,), (0,)), ((), ())), preferred_element_type=f32)


def _full(shape):
    nd = len(shape)
    return pl.BlockSpec(shape, lambda *_: (0,) * nd)


def _rows8(i):
    return pl.ds(pl.multiple_of(i * 8, 8), 8)


def _proj_fwd(k_arr, h, w_in, proj):
    L = h.shape[0]
    tm = min(1024, L)
    def body(_, h_ref, w_ref, __, o_ref):
        o_ref[...] = _dot(h_ref[...], w_ref[0]).astype(bf16)
    shard = lambda j, k: (k[0] + 1 + j) % NCHIP
    grid_spec = pltpu.PrefetchScalarGridSpec(
        num_scalar_prefetch=1, grid=(NCHIP - 1, L // tm),
        in_specs=[pl.BlockSpec((tm, D), lambda j, i, k: (i, 0)),
                  pl.BlockSpec((1, D, SHARD_W), lambda j, i, k: (shard(j, k), 0, 0)), _ANY],
        out_specs=pl.BlockSpec((tm, SHARD_W), lambda j, i, k: (i, shard(j, k))))
    return pl.pallas_call(
        body, grid_spec=grid_spec, out_shape=jax.ShapeDtypeStruct((L, IN_W), bf16),
        input_output_aliases={3: 0},
        name="proj_fwd", compiler_params=_cp("arbitrary", "arbitrary"))(k_arr, h, w_in, proj)


NLB = CW // 128
RPI = 32


def _put_blocked(buf, row0, nrows, v):
    for lb in range(NLB):
        buf[lb, pl.ds(row0, nrows), :] = v[:, lb * 128:(lb + 1) * 128]


def _get_blocked(buf, row0, nrows):
    return jnp.concatenate([buf[lb, pl.ds(row0, nrows), :] for lb in range(NLB)], axis=1)


def _fill_before(ebuf, prev):
    sub = lax.broadcasted_iota(jnp.int32, (8, 128), 0)
    def halo(p, carry):
        for lb in range(NLB):
            cur = ebuf[lb, _rows8(R + p), :]
            ebuf[lb, _rows8(p), :] = jnp.where(sub == 0, pltpu.roll(prev[lb, _rows8(p), :], 1, 0),
                                               pltpu.roll(cur, 1, 0))
        return carry
    lax.fori_loop(0, NH, halo, 0)


def _fir(buf, lb, r, coef, first, flip):
    win = buf[lb, pl.ds(pl.multiple_of(r * 8, 8), (KS + RPI - 1) * 8), :]
    outs = []
    for i in range(RPI):
        acc = [first, None, None, None]
        for k in range(KS):
            o = i + ((KS - 1 - k) if flip else k)
            t = coef[k] * win[8 * o:8 * o + 8, :]
            acc[k % 4] = t if acc[k % 4] is None else acc[k % 4] + t
        outs.append((acc[0] + acc[1]) + (acc[2] + acc[3]))
    return outs


def _conv_fwd(proj, cw, cbias, lng, lnb):
    L = proj.shape[0]
    nc = L // TC
    def body(ca_ref, cb_ref, zc_ref, w_ref, b_ref, g_ref, bb_ref, cu1_ref, ain_ref, ebuf, prev, cacc):
        @pl.when(pl.program_id(0) == 0)
        def _():
            prev[...] = jnp.zeros_like(prev)
        def glu(s, carry):
            rows = pl.ds(pl.multiple_of(s * 64, 64), 64)
            _put_blocked(ebuf, pl.multiple_of(NH * 8 + s * 64, 64), 64,
                         ca_ref[rows, :].astype(f32) * _sig(cb_ref[rows, :].astype(f32)))
            return carry
        lax.fori_loop(0, TC // 64, glu, 0)
        _fill_before(ebuf, prev)
        prev[...] = ebuf[:, R * 8:(NH + R) * 8, :]
        for lb in range(NLB):
            sl = slice(lb * 128, (lb + 1) * 128)
            wk = [jnp.broadcast_to(w_ref[k:k + 1, sl], (8, 128)) for k in range(KS)]
            bias = jnp.broadcast_to(b_ref[:, sl], (8, 128))
            def tap(q, carry, lb=lb, wk=wk, bias=bias):
                r = q * RPI
                for i, o in enumerate(_fir(ebuf, lb, r + (NH - KS + 1), wk, bias, False)):
                    cacc[lb, _rows8(r + i), :] = o
                return carry
            lax.fori_loop(0, R // RPI, tap, 0)
        def norm(s, carry):
            rows = pl.ds(pl.multiple_of(s * 64, 64), 64)
            c1b = _get_blocked(cacc, pl.multiple_of(s * 64, 64), 64).astype(bf16)
            cu1_ref[rows, :] = c1b
            c1 = c1b.astype(f32)
            xc = c1 - jnp.mean(c1, axis=-1, keepdims=True)
            var = jnp.mean(xc * xc, axis=-1, keepdims=True)
            ln = xc * lax.rsqrt(var + LN_EPS) * g_ref[...] + bb_ref[...]
            zc = zc_ref[rows, :].astype(f32)
            ain_ref[rows, :] = ((ln * _sig(ln)) * (zc * _sig(zc))).astype(bf16)
            return carry
        lax.fori_loop(0, TC // 64, norm, 0, unroll=4)

    col = lambda c: pl.BlockSpec((TC, CW), lambda i, c=c: (i, c))
    return pl.pallas_call(
        body, grid=(nc,),
        in_specs=[col(0), col(1), col(2), _full((32, CW)), _full((1, CW)), _full((1, CW)), _full((1, CW))],
        out_specs=[pl.BlockSpec((TC, CW), lambda i: (i, 0)), pl.BlockSpec((TC, CW), lambda i: (i, 0))],
        out_shape=[jax.ShapeDtypeStruct((L, CW), bf16), jax.ShapeDtypeStruct((L, CW), bf16)],
        scratch_shapes=[pltpu.VMEM((NLB, (NH + R) * 8, 128), f32), pltpu.VMEM((NLB, NH * 8, 128), f32),
                        pltpu.VMEM((NLB, TC, 128), f32)],
        name="conv_fwd", compiler_params=_cp("arbitrary"))(proj, proj, proj, cw, cbias, lng, lnb)


def _gelu_parts(y0):
    t = jnp.tanh(GELU_K0 * (y0 + GELU_K1 * y0 * y0 * y0))
    return t, 0.5 * y0 * (1.0 + t)


def _ssm_fwd(proj, bbt_re, bbt_im, ct_re, ct_im, a_re, a_im, apow_re, apow_im, dvec, wglu, bglu):
    L = proj.shape[0]
    nc = L // TC
    def body(u_ref, zs_ref, bre_ref, bim_ref, cre_ref, cim_ref, are_ref, aim_ref, pwr_ref, pwi_ref,
             d_ref, wg_ref, bg_ref, y0_ref, bin_ref, sre, sim, cinr, cini, prev_re, prev_im):
        c = pl.program_id(0)
        @pl.when(c == 0)
        def _():
            prev_re[...] = jnp.zeros_like(prev_re)
            prev_im[...] = jnp.zeros_like(prev_im)
        u = u_ref[...]
        for blk in range(4):
            ub = u[:, 128 * blk:128 * (blk + 1)]
            sre[:, 512 * blk:512 * (blk + 1)] = _dot(ub, bre_ref[blk])
            sim[:, 512 * blk:512 * (blk + 1)] = _dot(ub, bim_ref[blk])
        for lb in range(NS // LBW):
            sl = slice(lb * LBW, (lb + 1) * LBW)
            ar = jnp.broadcast_to(are_ref[:, sl], (8, LBW))
            ai = jnp.broadcast_to(aim_ref[:, sl], (8, LBW))
            def step(r, carry, sl=sl, ar=ar, ai=ai):
                sr, si = carry
                nr = ar * sr - ai * si + sre[_rows8(r), sl]
                ni = ar * si + ai * sr + sim[_rows8(r), sl]
                sre[_rows8(r), sl] = nr
                sim[_rows8(r), sl] = ni
                return nr, ni
            lax.fori_loop(1, R, step, (sre[0:8, sl], sim[0:8, sl]))
        a_r = pwr_ref[R - 1:R, :]
        a_i = pwi_ref[R - 1:R, :]
        cr = prev_re[0:1, :]
        ci = prev_im[0:1, :]
        for seg in range(8):
            cinr[seg:seg + 1, :] = cr
            cini[seg:seg + 1, :] = ci
            er = sre[8 * (R - 1) + seg:8 * (R - 1) + seg + 1, :]
            ei = sim[8 * (R - 1) + seg:8 * (R - 1) + seg + 1, :]
            cr, ci = er + a_r * cr - a_i * ci, ei + a_r * ci + a_i * cr
        prev_re[0:1, :] = cr
        prev_im[0:1, :] = ci
        for lb in range(NS // LBW):
            sl = slice(lb * LBW, (lb + 1) * LBW)
            kr = cinr[:, sl]
            ki = cini[:, sl]
            def fix(r, carry, sl=sl, kr=kr, ki=ki):
                pr = jnp.broadcast_to(pwr_ref[pl.ds(r, 1), sl], (8, LBW))
                pi = jnp.broadcast_to(pwi_ref[pl.ds(r, 1), sl], (8, LBW))
                sre[_rows8(r), sl] = sre[_rows8(r), sl] + pr * kr - pi * ki
                sim[_rows8(r), sl] = sim[_rows8(r), sl] + pr * ki + pi * kr
                return carry
            lax.fori_loop(0, R, fix, 0, unroll=2)
        yp = []
        for blk in range(4):
            sr = sre[:, 512 * blk:512 * (blk + 1)].astype(bf16)
            si = sim[:, 512 * blk:512 * (blk + 1)].astype(bf16)
            yp.append(_dot(sr, cre_ref[blk]) - _dot(si, cim_ref[blk]))
        y0 = jnp.concatenate(yp, axis=1) + d_ref[...] * u.astype(f32)
        y0_ref[...] = y0
        _, y1 = _gelu_parts(y0)
        glu = _dot(y1.astype(bf16), wg_ref[...]) + bg_ref[...]
        y2 = y1 * _sig(glu)
        zs = zs_ref[...].astype(f32)
        bin_ref[...] = (y2 * (zs * _sig(zs))).astype(bf16)

    return pl.pallas_call(
        body, grid=(nc,),
        in_specs=[pl.BlockSpec((TC, SW), lambda c: (c, 6)), pl.BlockSpec((TC, SW), lambda c: (c, 7)),
                  _full((4, 128, 512)), _full((4, 128, 512)), _full((4, 512, 128)), _full((4, 512, 128)),
                  _full((1, NS)), _full((1, NS)), _full((R, NS)), _full((R, NS)),
                  _full((1, SW)), _full((SW, SW)), _full((1, SW))],
        out_specs=[pl.BlockSpec((TC, SW), lambda c: (c, 0)), pl.BlockSpec((TC, SW), lambda c: (c, 0)),
                   pl.BlockSpec((TC, NS), lambda c: (c, 0)), pl.BlockSpec((TC, NS), lambda c: (c, 0)),
                   pl.BlockSpec((8, NS), lambda c: (c, 0)), pl.BlockSpec((8, NS), lambda c: (c, 0))],
        out_shape=[jax.ShapeDtypeStruct((L, SW), f32), jax.ShapeDtypeStruct((L, SW), bf16),
                   jax.ShapeDtypeStruct((L, NS), f32), jax.ShapeDtypeStruct((L, NS), f32),
                   jax.ShapeDtypeStruct((nc * 8, NS), f32), jax.ShapeDtypeStruct((nc * 8, NS), f32)],
        scratch_shapes=[pltpu.VMEM((8, NS), f32), pltpu.VMEM((8, NS), f32)],
        name="ssm_fwd", compiler_params=_cp("arbitrary"))(
            proj, proj, bbt_re, bbt_im, ct_re, ct_im, a_re, a_im, apow_re, apow_im, dvec, wglu, bglu)


def _tail(a_in, b_in, proj, x, tgt, wco, wso, wout, gpost):
    L = x.shape[0]
    tm = 512
    def body(a_ref, b_ref, gc_ref, gs_ref, x_ref, t_ref, wco_ref, wso_ref, wout_ref, gp_ref,
             gx_ref, dain_ref, dbin_ref, dp_ref, dwout_ref, dwco_ref, dwso_ref, dgp_ref, loss_ref):
        @pl.when(pl.program_id(0) == 0)
        def _():
            dwout_ref[...] = jnp.zeros_like(dwout_ref)
            dwco_ref[...] = jnp.zeros_like(dwco_ref)
            dwso_ref[...] = jnp.zeros_like(dwso_ref)
            dgp_ref[...] = jnp.zeros_like(dgp_ref)
            loss_ref[...] = jnp.zeros_like(loss_ref)
        a = a_ref[...]
        b = b_ref[...]
        co = _dot(a, wco_ref[...])
        so = jnp.concatenate([_dot(b, wso_ref[j]) for j in range(NCHIP)], axis=1)
        sc = _sig(gc_ref[...].astype(f32))
        ss = _sig(gs_ref[...].astype(f32))
        mb = (sc * co + ss * so).astype(bf16)
        out = _dot(mb, wout_ref[...])
        r2 = lax.rsqrt(jnp.mean(out * out, axis=-1, keepdims=True) + RMS_EPS)
        on = out * r2
        gp = gp_ref[...]
        e = x_ref[...] + on * gp - t_ref[...]
        loss_ref[...] += (0.5 / D) * jnp.sum(e * e)
        dy = e * (1.0 / D)
        gx_ref[...] = dy
        dgp_ref[...] += jnp.sum(dy * on, axis=0, keepdims=True)
        dn = dy * gp
        dout = (r2 * (dn - on * jnp.mean(dn * on, axis=-1, keepdims=True))).astype(bf16)
        dwout_ref[...] += _dot_tn(mb, dout)
        dm = _dot_nt(dout, wout_ref[...])
        dp_ref[:, 0:D] = (dm * co * sc * (1.0 - sc)).astype(bf16)
        dp_ref[:, D:2 * D] = (dm * so * ss * (1.0 - ss)).astype(bf16)
        dco = (dm * sc).astype(bf16)
        dso = (dm * ss).astype(bf16)
        dwco_ref[...] += _dot_tn(a, dco)
        dbin = None
        for j in range(NCHIP):
            dso_j = dso[:, j * 256:(j + 1) * 256]
            dwso_ref[j] += _dot_tn(b, dso_j)
            t = _dot_nt(dso_j, wso_ref[j])
            dbin = t if dbin is None else dbin + t
        dain_ref[...] = _dot_nt(dco, wco_ref[...]).astype(bf16)
        dbin_ref[...] = dbin.astype(bf16)

    row = lambda w: pl.BlockSpec((tm, w), lambda i: (i, 0))
    one = lambda shape: pl.BlockSpec(shape, lambda i: (0,) * len(shape), pipeline_mode=pl.Buffered(1))
    return pl.pallas_call(
        body, grid=(L // tm,),
        in_specs=[row(CW), row(SW), pl.BlockSpec((tm, D), lambda i: (i, 4)), pl.BlockSpec((tm, D), lambda i: (i, 5)),
                  row(D), row(D), one((CW, D)), one((NCHIP, SW, 256)), one((D, D)), one((1, D))],
        out_specs=[row(D), row(CW), row(SW), pl.BlockSpec((tm, 2 * D), lambda i: (i, 2)),
                   one((D, D)), one((CW, D)), one((NCHIP, SW, 256)), one((1, D)), one((1, 128))],
        out_shape=[jax.ShapeDtypeStruct((L, D), f32), jax.ShapeDtypeStruct((L, CW), bf16),
                   jax.ShapeDtypeStruct((L, SW), bf16), jax.ShapeDtypeStruct((L, IN_W), bf16),
                   jax.ShapeDtypeStruct((D, D), f32), jax.ShapeDtypeStruct((CW, D), f32),
                   jax.ShapeDtypeStruct((NCHIP, SW, 256), f32), jax.ShapeDtypeStruct((1, D), f32),
                   jax.ShapeDtypeStruct((1, 128), f32)],
        name="tail", compiler_params=_cp("arbitrary"))(a_in, b_in, proj, proj, x, tgt, wco, wso, wout, gpost)


def _ssm_bwd(d_bin, y0, proj, sre, sim, cinr, cini, bbt_re, bbt_im, ct_re, ct_im,
             a_re, a_im, apow_re, apow_im, dvec, wglu, bglu, dproj):
    L = y0.shape[0]
    nc = L // TC
    def body(dbin_ref, y0_ref, u_ref, zs_ref, sre_ref, sim_ref, cinr_ref, cini_ref,
             bre_ref, bim_ref, cre_ref, cim_ref, are_ref, aim_ref, pwr_ref, pwi_ref, d_ref, wg_ref, bg_ref, _,
             dp_ref, dbre_ref, dbim_ref, dcre_ref, dcim_ref, dd_ref, dar_ref, dai_ref, dwg_ref, dbg_ref,
             gre, gim, gcr, gci, nxt_re, nxt_im):
        @pl.when(pl.program_id(0) == 0)
        def _():
            for ref in (dbre_ref, dbim_ref, dcre_ref, dcim_ref, dd_ref, dar_ref, dai_ref, dwg_ref, dbg_ref,
                        nxt_re, nxt_im):
                ref[...] = jnp.zeros_like(ref)
        y0 = y0_ref[...]
        u = u_ref[...]
        zs = zs_ref[...].astype(f32)
        dbin = dbin_ref[...].astype(f32)
        t, y1 = _gelu_parts(y0)
        y1b = y1.astype(bf16)
        sg = _sig(_dot(y1b, wg_ref[...]) + bg_ref[...])
        sz = _sig(zs)
        d_y2 = dbin * (zs * sz)
        dp_ref[:, SW:2 * SW] = (dbin * (y1 * sg) * (sz * (1.0 + zs * (1.0 - sz)))).astype(bf16)
        d_glu = d_y2 * y1 * sg * (1.0 - sg)
        d_glub = d_glu.astype(bf16)
        d_y1 = d_y2 * sg + _dot_nt(d_glub, wg_ref[...])
        dwg_ref[...] += _dot_tn(y1b, d_glub)
        dbg_ref[...] += jnp.sum(d_glu, axis=0, keepdims=True)
        dgelu = 0.5 * (1.0 + t) + 0.5 * y0 * (1.0 - t * t) * GELU_K0 * (1.0 + 3.0 * GELU_K1 * y0 * y0)
        d_y0 = d_y1 * dgelu
        dd_ref[...] += jnp.sum(d_y0 * u.astype(f32), axis=0, keepdims=True)
        dyb = d_y0.astype(bf16)
        for blk in range(4):
            dy1 = dyb[:, 128 * blk:128 * (blk + 1)]
            gre[:, 512 * blk:512 * (blk + 1)] = _dot_nt(dy1, cre_ref[blk])
            gim[:, 512 * blk:512 * (blk + 1)] = -_dot_nt(dy1, cim_ref[blk])
        for lb in range(NS // LBW):
            sl = slice(lb * LBW, (lb + 1) * LBW)
            ar = jnp.broadcast_to(are_ref[:, sl], (8, LBW))
            ai = jnp.broadcast_to(aim_ref[:, sl], (8, LBW))
            def step(k, carry, sl=sl, ar=ar, ai=ai):
                gr, gi = carry
                row = _rows8(R - 2 - k)
                nr = ar * gr + ai * gi + gre[row, sl]
                ni = ar * gi - ai * gr + gim[row, sl]
                gre[row, sl] = nr
                gim[row, sl] = ni
                return nr, ni
            lax.fori_loop(0, R - 1, step, (gre[8 * (R - 1):8 * R, sl], gim[8 * (R - 1):8 * R, sl]))
        a_r = pwr_ref[R - 1:R, :]
        a_i = pwi_ref[R - 1:R, :]
        cr = nxt_re[0:1, :]
        ci = nxt_im[0:1, :]
        for seg in range(7, -1, -1):
            gcr[seg:seg + 1, :] = cr
            gci[seg:seg + 1, :] = ci
            er = gre[seg:seg + 1, :]
            ei = gim[seg:seg + 1, :]
            cr, ci = er + a_r * cr + a_i * ci, ei + a_r * ci - a_i * cr
        nxt_re[0:1, :] = cr
        nxt_im[0:1, :] = ci
        for lb in range(NS // LBW):
            sl = slice(lb * LBW, (lb + 1) * LBW)
            kr = gcr[:, sl]
            ki = gci[:, sl]
            def fixed(rows, prow, sl=sl, kr=kr, ki=ki):
                pr = jnp.broadcast_to(pwr_ref[prow, sl], (8, LBW))
                pi = jnp.broadcast_to(pwi_ref[prow, sl], (8, LBW))
                gr = gre[rows, sl] + pr * kr + pi * ki
                gi = gim[rows, sl] + pr * ki - pi * kr
                gre[rows, sl] = gr
                gim[rows, sl] = gi
                return gr, gi
            g0r, g0i = fixed(slice(0, 8), slice(R - 1, R))
            p0r, p0i = cinr_ref[:, sl], cini_ref[:, sl]
            acc0 = (g0r * p0r + g0i * p0i, g0i * p0r - g0r * p0i)
            def dacc(r, carry, sl=sl, fixed=fixed):
                xr, xi = carry
                gr, gi = fixed(_rows8(r), pl.ds(R - 1 - r, 1))
                pr, pi = sre_ref[_rows8(r - 1), sl], sim_ref[_rows8(r - 1), sl]
                return xr + gr * pr + gi * pi, xi + gi * pr - gr * pi
            xr, xi = lax.fori_loop(1, R, dacc, acc0)
            dar_ref[:, sl] += xr
            dai_ref[:, sl] += xi
        dup = []
        for blk in range(4):
            s4 = slice(512 * blk, 512 * (blk + 1))
            s1 = slice(128 * blk, 128 * (blk + 1))
            grb = gre[:, s4].astype(bf16)
            gib = gim[:, s4].astype(bf16)
            dup.append(_dot_nt(grb, bre_ref[blk]) + _dot_nt(gib, bim_ref[blk]))
            dbre_ref[blk] += _dot_tn(u[:, s1], grb)
            dbim_ref[blk] += _dot_tn(u[:, s1], gib)
            dcre_ref[blk] += _dot_tn(dyb[:, s1], sre_ref[:, s4].astype(bf16))
            dcim_ref[blk] -= _dot_tn(dyb[:, s1], sim_ref[:, s4].astype(bf16))
        dp_ref[:, 0:SW] = (jnp.concatenate(dup, axis=1) + d_ref[...] * d_y0).astype(bf16)

    rev = lambda w, cidx: pl.BlockSpec((TC, w), lambda i, cidx=cidx: (nc - 1 - i, cidx))
    one = lambda shape: pl.BlockSpec(shape, lambda i: (0,) * len(shape))
    return pl.pallas_call(
        body, grid=(nc,),
        in_specs=[rev(SW, 0), rev(SW, 0), rev(SW, 6), rev(SW, 7), rev(NS, 0), rev(NS, 0),
                  pl.BlockSpec((8, NS), lambda i: (nc - 1 - i, 0)), pl.BlockSpec((8, NS), lambda i: (nc - 1 - i, 0)),
                  one((4, 128, 512)), one((4, 128, 512)), one((4, 512, 128)), one((4, 512, 128)),
                  one((1, NS)), one((1, NS)), one((R, NS)), one((R, NS)),
                  one((1, SW)), one((SW, SW)), one((1, SW)), _ANY],
        out_specs=[pl.BlockSpec((TC, 2 * SW), lambda i: (nc - 1 - i, 3)),
                   one((4, 128, 512)), one((4, 128, 512)), one((4, 128, 512)), one((4, 128, 512)),
                   one((1, SW)), one((8, NS)), one((8, NS)), one((SW, SW)), one((1, SW))],
        out_shape=[jax.ShapeDtypeStruct((L, IN_W), bf16),
                   jax.ShapeDtypeStruct((4, 128, 512), f32), jax.ShapeDtypeStruct((4, 128, 512), f32),
                   jax.ShapeDtypeStruct((4, 128, 512), f32), jax.ShapeDtypeStruct((4, 128, 512), f32),
                   jax.ShapeDtypeStruct((1, SW), f32), jax.ShapeDtypeStruct((8, NS), f32),
                   jax.ShapeDtypeStruct((8, NS), f32), jax.ShapeDtypeStruct((SW, SW), f32),
                   jax.ShapeDtypeStruct((1, SW), f32)],
        scratch_shapes=[pltpu.VMEM((TC, NS), f32), pltpu.VMEM((TC, NS), f32), pltpu.VMEM((8, NS), f32),
                        pltpu.VMEM((8, NS), f32), pltpu.VMEM((8, NS), f32), pltpu.VMEM((8, NS), f32)],
        input_output_aliases={19: 0},
        name="ssm_bwd", compiler_params=_cp("arbitrary"))(
            d_bin, y0, proj, proj, sre, sim, cinr, cini, bbt_re, bbt_im, ct_re, ct_im,
            a_re, a_im, apow_re, apow_im, dvec, wglu, bglu, dproj)


def _conv_bwd(d_ain, cu1, proj, cw, lng, lnb, dproj):
    L = cu1.shape[0]
    nc = L // TC
    def body(dain_ref, cu1_ref, ca_ref, cb_ref, zc_ref, cah_ref, cbh_ref, w_ref, g_ref, bb_ref, _,
             dp_ref, dw_ref, dbias_ref, dlng_ref, dlnb_ref, dbuf, ebuf, prev, nxt, dcu0):
        i = pl.program_id(0)
        @pl.when(i == 0)
        def _():
            dw_ref[...] = jnp.zeros_like(dw_ref)
            dbias_ref[...] = jnp.zeros_like(dbias_ref)
            dlng_ref[...] = jnp.zeros_like(dlng_ref)
            dlnb_ref[...] = jnp.zeros_like(dlnb_ref)
            nxt[...] = jnp.zeros_like(nxt)
        def lnb(s, carry):
            rows = pl.ds(pl.multiple_of(s * 32, 32), 32)
            dain = dain_ref[rows, :].astype(f32)
            c1 = cu1_ref[rows, :].astype(f32)
            zc = zc_ref[rows, :].astype(f32)
            xc = c1 - jnp.mean(c1, axis=-1, keepdims=True)
            var = jnp.mean(xc * xc, axis=-1, keepdims=True)
            rstd = lax.rsqrt(var + LN_EPS)
            xh = xc * rstd
            ln = xh * g_ref[...] + bb_ref[...]
            sl_ = _sig(ln)
            sz = _sig(zc)
            dp_ref[rows, 2 * CW:3 * CW] = (dain * (ln * sl_) * (sz * (1.0 + zc * (1.0 - sz)))).astype(bf16)
            d_ln = dain * (zc * sz) * (sl_ * (1.0 + ln * (1.0 - sl_)))
            dlng_ref[...] += jnp.sum(d_ln * xh, axis=0, keepdims=True)
            dlnb_ref[...] += jnp.sum(d_ln, axis=0, keepdims=True)
            dxh = d_ln * g_ref[...]
            d_c1 = rstd * (dxh - jnp.mean(dxh, axis=-1, keepdims=True)
                           - xh * jnp.mean(dxh * xh, axis=-1, keepdims=True))
            dbias_ref[...] += jnp.sum(d_c1, axis=0, keepdims=True)
            _put_blocked(dbuf, pl.multiple_of(s * 32, 32), 32, d_c1)
            _put_blocked(ebuf, pl.multiple_of(NH * 8 + s * 32, 32), 32,
                         ca_ref[rows, :].astype(f32) * _sig(cb_ref[rows, :].astype(f32)))
            return carry
        lax.fori_loop(0, TC // 32, lnb, 0, unroll=4)
        sub = lax.broadcasted_iota(jnp.int32, (8, 128), 0)
        def after(p, carry):
            for lb in range(NLB):
                cur = dbuf[lb, _rows8(p), :]
                dbuf[lb, _rows8(R + p), :] = jnp.where(sub == 7, pltpu.roll(nxt[lb, _rows8(p), :], 7, 0),
                                                       pltpu.roll(cur, 7, 0))
            return carry
        lax.fori_loop(0, NH, after, 0)
        nxt[...] = dbuf[:, 0:NH * 8, :]
        def before(s, carry):
            rows = pl.ds(pl.multiple_of(s * 64, 64), 64)
            v = cah_ref[rows, :].astype(f32) * _sig(cbh_ref[rows, :].astype(f32))
            _put_blocked(prev, pl.multiple_of(s * 64, 64), 64, jnp.where(i == nc - 1, jnp.zeros_like(v), v))
            return carry
        lax.fori_loop(0, NH * 8 // 64, before, 0)
        _fill_before(ebuf, prev)
        for lb in range(NLB):
            sl = slice(lb * 128, (lb + 1) * 128)
            wk = [jnp.broadcast_to(w_ref[k:k + 1, sl], (8, 128)) for k in range(KS)]
            def tap(q, carry, lb=lb, wk=wk):
                r = q * RPI
                for j, o in enumerate(_fir(dbuf, lb, r, wk, None, True)):
                    dcu0[lb, _rows8(r + j), :] = o
                return carry
            lax.fori_loop(0, R // RPI, tap, 0)
            def wgrad(q, accs, lb=lb):
                r = q * RPI
                dvs = dbuf[lb, pl.ds(pl.multiple_of(r * 8, 8), RPI * 8), :]
                win = ebuf[lb, pl.ds(pl.multiple_of((r + (NH - KS + 1)) * 8, 8), (KS + RPI - 1) * 8), :]
                accs = list(accs)
                for j in range(RPI):
                    dv = dvs[8 * j:8 * j + 8, :]
                    for k in range(KS):
                        accs[k] = accs[k] + dv * win[8 * (j + k):8 * (j + k) + 8, :]
                return tuple(accs)
            accs = lax.fori_loop(0, R // RPI, wgrad, tuple(jnp.zeros((8, 128), f32) for _ in range(KS)))
            for k in range(KS):
                dw_ref[k, :, sl] += accs[k]
        def glub(s, carry):
            rows = pl.ds(pl.multiple_of(s * 64, 64), 64)
            d0 = _get_blocked(dcu0, pl.multiple_of(s * 64, 64), 64)
            ca = ca_ref[rows, :].astype(f32)
            sb = _sig(cb_ref[rows, :].astype(f32))
            dp_ref[rows, 0:CW] = (d0 * sb).astype(bf16)
            dp_ref[rows, CW:2 * CW] = (d0 * ca * sb * (1.0 - sb)).astype(bf16)
            return carry
        lax.fori_loop(0, TC // 64, glub, 0)

    hrows = NH * 8
    per = TC // hrows
    rev = lambda cidx: pl.BlockSpec((TC, CW), lambda i, cidx=cidx: (nc - 1 - i, cidx))
    halo = lambda cidx: pl.BlockSpec((hrows, CW), lambda i, cidx=cidx: (jnp.maximum((nc - 1 - i) * per - 1, 0), cidx))
    one = lambda shape: pl.BlockSpec(shape, lambda i: (0,) * len(shape))
    return pl.pallas_call(
        body, grid=(nc,),
        in_specs=[rev(0), rev(0), rev(0), rev(1), rev(2), halo(0), halo(1), one((32, CW)), one((1, CW)), one((1, CW)),
                  _ANY],
        out_specs=[pl.BlockSpec((TC, 3 * CW), lambda i: (nc - 1 - i, 0)), one((32, 8, CW)), one((1, CW)), one((1, CW)), one((1, CW))],
        out_shape=[jax.ShapeDtypeStruct((L, IN_W), bf16), jax.ShapeDtypeStruct((32, 8, CW), f32),
                   jax.ShapeDtypeStruct((1, CW), f32), jax.ShapeDtypeStruct((1, CW), f32),
                   jax.ShapeDtypeStruct((1, CW), f32)],
        scratch_shapes=[pltpu.VMEM((NLB, (R + NH) * 8, 128), f32), pltpu.VMEM((NLB, (NH + R) * 8, 128), f32),
                        pltpu.VMEM((NLB, hrows, 128), f32), pltpu.VMEM((NLB, hrows, 128), f32),
                        pltpu.VMEM((NLB, TC, 128), f32)],
        input_output_aliases={10: 0},
        name="conv_bwd", compiler_params=_cp("arbitrary"))(d_ain, cu1, proj, proj, proj, proj, proj, cw, lng, lnb, dproj)


def _win_grad(h, dproj):
    L = h.shape[0]
    tm = min(1024, L)
    nt = L // tm
    def body(h_ref, d_ref, o_ref, acc):
        i = pl.program_id(1)
        @pl.when(i == 0)
        def _():
            acc[...] = jnp.zeros_like(acc)
        acc[...] += _dot_tn(h_ref[...], d_ref[...])
        @pl.when(i == nt - 1)
        def _():
            o_ref[0] = acc[...].astype(bf16)
    return pl.pallas_call(
        body, grid=(NCHIP, nt),
        in_specs=[pl.BlockSpec((tm, D), lambda j, i: (i, 0)), pl.BlockSpec((tm, SHARD_W), lambda j, i: (i, j))],
        out_specs=pl.BlockSpec((1, D, SHARD_W), lambda j, i: (j, 0, 0)),
        out_shape=jax.ShapeDtypeStruct((NCHIP, D, SHARD_W), bf16),
        scratch_shapes=[pltpu.VMEM((D, SHARD_W), f32)],
        name="win_grad", compiler_params=_cp("arbitrary", "arbitrary"))(h, dproj)


def _adamw_math(w, g, m, v):
    m2 = B1 * m + (1.0 - B1) * g
    v2 = B2 * v + (1.0 - B2) * (g * g)
    m_hat = m2 / (1.0 - B1 ** STEP)
    v_hat = v2 / (1.0 - B2 ** STEP)
    delta = -LR * (m_hat / (jnp.sqrt(v_hat) + EPS) + WD * w)
    return delta, m2, v2


def _adamw(name, w, g, m, v):
    rows, cols = w.shape
    tm = rows if rows <= 256 else (256 if rows % 256 == 0 else 128)
    assert rows % tm == 0
    def body(w_ref, g_ref, m_ref, v_ref, d_ref, m2_ref, v2_ref):
        d, m2, v2 = _adamw_math(w_ref[...], g_ref[...], m_ref[...], v_ref[...])
        d_ref[...] = d
        m2_ref[...] = m2
        v2_ref[...] = v2
    spec = pl.BlockSpec((tm, cols), lambda i: (i, 0))
    shp = jax.ShapeDtypeStruct((rows, cols), f32)
    return pl.pallas_call(
        body, grid=(rows // tm,), in_specs=[spec] * 4, out_specs=[spec] * 3, out_shape=[shp] * 3,
        name=name, compiler_params=_cp("arbitrary"))(w, g, m, v)


def _adamw_group(name, ws, gs, ms, vs):
    n = len(ws)
    def body(*refs):
        for i in range(n):
            w_ref, g_ref, m_ref, v_ref = (refs[q * n + i] for q in range(4))
            d, m2, v2 = _adamw_math(w_ref[...], g_ref[...], m_ref[...], v_ref[...])
            for q, val in enumerate((d, m2, v2)):
                refs[(4 + q) * n + i][...] = val
    shapes = [jax.ShapeDtypeStruct(w.shape, f32) for w in ws]
    out = pl.pallas_call(body, out_shape=shapes * 3, name=name,
                         compiler_params=pltpu.CompilerParams(vmem_limit_bytes=VMEM_LIMIT))(*ws, *gs, *ms, *vs)
    return [(out[i], out[n + i], out[2 * n + i]) for i in range(n)]


_ANY = pl.BlockSpec(memory_space=pl.ANY)


def _chunks(rows, parts):
    step = rows // parts
    assert step * parts == rows and step % 16 == 0
    return [(i * step, step) for i in range(parts)]


def _place():
    x, y, c = lax.axis_index("x"), lax.axis_index("y"), lax.axis_index("c")
    chips = [(1 - x, y), (x, 1 - y), (1 - x, 1 - y)]
    return x, y, c, chips


def _nchunks(half, cols, itemsize):
    return 4 if half * cols * itemsize >= (1 << 20) else 1


def _segments(metas):
    segs = []
    for w, (half, cols, dt) in enumerate(metas):
        for r0, n in _chunks(half, _nchunks(half, cols, jnp.dtype(dt).itemsize)):
            segs.append((w, half, r0, n))
    return segs


def _rcopy(i, src, dst, send_sems, recv_sems, to):
    return pltpu.make_async_remote_copy(src_ref=src, dst_ref=dst, send_sem=send_sems.at[i], recv_sem=recv_sems.at[i],
                                        device_id=to, device_id_type=MESH)


def _gather_prep(k_arr, shards, x, tgt, g_pre, perm):
    na = len(shards)
    L = x.shape[0]
    nc = L // TC
    segs = _segments([(a.shape[0] // 2, a.shape[1], a.dtype) for a in shards])
    ns = len(segs)
    def body(_, *refs):
        ins = refs[:na]
        x_ref, t_ref, g_ref, p_ref = refs[na:na + 4]
        outs = refs[na + 4:2 * na + 4]
        h_ref, xi_ref, ti_ref, proj_ref = refs[2 * na + 4:2 * na + 8]
        stages = refs[2 * na + 8:3 * na + 8]
        send_sems, recv_sems, local_sems = refs[3 * na + 8:]
        i = pl.program_id(0)
        x, y, c, chips = _place()
        k = 2 * x + y
        me, sibling = (x, y, c), (x, y, 1 - c)

        def dst(w, half, chip, pc, r0, n):
            return outs[w].at[chip, pl.ds(pc * half + r0, n), :]

        def firsts():
            return [_rcopy(j * ns + s, ins[w].at[pl.ds(c * half + r0, n), :], dst(w, half, k, c, r0, n),
                           send_sems, recv_sems, (*chip, c))
                    for j, chip in enumerate(chips) for s, (w, half, r0, n) in enumerate(segs)]

        def own_out(w):
            return pltpu.make_async_copy(stages[w], outs[w].at[k], local_sems.at[w])

        @pl.when(i == 0)
        def _():
            for cp in firsts():
                cp.start()
            cins = [pltpu.make_async_copy(ins[w], stages[w], local_sems.at[w]) for w in range(na)]
            for cp in cins:
                cp.start()
            for w in range(na):
                cins[w].wait()
                own_out(w).start()

        p = p_ref[...]
        def through(v):
            hi = v.astype(bf16)
            r1 = v - hi.astype(f32)
            mid = r1.astype(bf16)
            lo = (r1 - mid.astype(f32)).astype(bf16)
            return (_dot(p, hi) + _dot(p, mid)) + _dot(p, lo)
        xt = x_ref[...]
        r = lax.rsqrt(jnp.mean(xt * xt, axis=-1, keepdims=True) + RMS_EPS)
        hp = _dot(p, (xt * r * g_ref[...]).astype(bf16)).astype(bf16)
        h_ref[...] = hp
        proj_ref[...] = _dot(hp, stages[0][...]).astype(bf16)
        xi_ref[...] = through(xt)
        ti_ref[...] = through(t_ref[...])

        @pl.when(i == nc - 1)
        def _():
            passed = []
            for j, chip in enumerate(chips):
                cj = 2 * chip[0] + chip[1]
                for s, (w, half, r0, n) in enumerate(segs):
                    landed = dst(w, half, cj, c, r0, n)
                    _rcopy(j * ns + s, landed, landed, send_sems, recv_sems, me).wait_recv()
                    fwd = _rcopy(3 * ns + j * ns + s, landed, landed, send_sems, recv_sems, sibling)
                    fwd.start()
                    passed.append(fwd)
            for j, chip in enumerate(chips):
                cj = 2 * chip[0] + chip[1]
                for s, (w, half, r0, n) in enumerate(segs):
                    theirs = dst(w, half, cj, 1 - c, r0, n)
                    _rcopy(3 * ns + j * ns + s, theirs, theirs, send_sems, recv_sems, me).wait_recv()
            for cp in firsts() + passed:
                cp.wait_send()
            for w in range(na):
                own_out(w).wait()

    row = lambda: pl.BlockSpec((TC, D), lambda i, k: (i, 0))
    grid_spec = pltpu.PrefetchScalarGridSpec(
        num_scalar_prefetch=1, grid=(nc,),
        in_specs=[_ANY] * na + [row(), row(), pl.BlockSpec((1, D), lambda i, k: (0, 0)),
                                pl.BlockSpec((TC, TC), lambda i, k: (0, 0))],
        out_specs=[_ANY] * na + [row(), row(), row(), pl.BlockSpec((TC, SHARD_W), lambda i, k: (i, k[0]))],
        scratch_shapes=[pltpu.VMEM(a.shape, a.dtype) for a in shards]
        + [pltpu.SemaphoreType.DMA((6 * ns,)), pltpu.SemaphoreType.DMA((6 * ns,)), pltpu.SemaphoreType.DMA((na,))])
    return pl.pallas_call(
        body, grid_spec=grid_spec,
        out_shape=[jax.ShapeDtypeStruct((NCHIP,) + a.shape, a.dtype) for a in shards]
        + [jax.ShapeDtypeStruct((L, D), bf16), jax.ShapeDtypeStruct((L, D), f32), jax.ShapeDtypeStruct((L, D), f32),
           jax.ShapeDtypeStruct((L, IN_W), bf16)],
        name="gather_prep", compiler_params=_cp("arbitrary"))(k_arr, *shards, x, tgt, g_pre, perm)


def _x_grad_exchange(dproj, w_in, x, gx0, g_pre, parts, small):
    L = x.shape[0]
    tm = 512
    nt = L // tm
    na = len(parts)
    hs = SMALL_ROWS // 2
    segs = _segments([(p.shape[1], p.shape[2], p.dtype) for p in parts])
    ns = len(segs) + 1
    def body(*refs):
        d_ref, w_ref, x_ref, gx_ref, g_ref = refs[:5]
        ins, s_ref = refs[5:5 + na], refs[5 + na]
        o_ref, dg_ref = refs[6 + na:8 + na]
        outs, qs_ref = refs[8 + na:8 + 2 * na], refs[8 + 2 * na]
        stages = refs[9 + 2 * na:10 + 3 * na]
        send_sems, recv_sems, local_sems = refs[10 + 3 * na:]
        i = pl.program_id(0)
        x, y, c, chips = _place()
        k = 2 * x + y

        def my_small():
            return s_ref.at[pl.ds(c * hs, hs), :]

        def copies():
            out = []
            for j, chip in enumerate(chips):
                cj = 2 * chip[0] + chip[1]
                pieces = [(my_small(), qs_ref.at[k])]
                pieces += [(ins[w].at[cj, pl.ds(r0, n), :], outs[w].at[k, pl.ds(r0, n), :]) for w, _, r0, n in segs]
                out += [_rcopy(ns * j + s, src, d, send_sems, recv_sems, (*chip, c)) for s, (src, d) in enumerate(pieces)]
            return out

        def own_out(w):
            dst = qs_ref.at[k] if w == na else outs[w].at[k]
            return pltpu.make_async_copy(stages[w], dst, local_sems.at[w])

        @pl.when(i == 0)
        def _():
            dg_ref[...] = jnp.zeros_like(dg_ref)
            for cp in copies():
                cp.start()
            cins = [pltpu.make_async_copy(my_small() if w == na else ins[w].at[k], stages[w], local_sems.at[w])
                    for w in range(na + 1)]
            for cp in cins:
                cp.start()
            for w in range(na + 1):
                cins[w].wait()
                own_out(w).start()

        dh = _dot_nt(d_ref[:, 0:SHARD_W], w_ref[0])
        for j in range(1, NCHIP):
            dh = dh + _dot_nt(d_ref[:, j * SHARD_W:(j + 1) * SHARD_W], w_ref[j])
        xt = x_ref[...]
        r = lax.rsqrt(jnp.mean(xt * xt, axis=-1, keepdims=True) + RMS_EPS)
        xn = xt * r
        dg_ref[...] += jnp.sum(dh * xn, axis=0, keepdims=True)
        dxn = dh * g_ref[...]
        o_ref[...] = gx_ref[...] + r * (dxn - xn * jnp.mean(dxn * xn, axis=-1, keepdims=True))

        @pl.when(i == nt - 1)
        def _():
            for cp in copies():
                cp.wait_recv()
            for cp in copies():
                cp.wait_send()
            for w in range(na + 1):
                own_out(w).wait()

    return pl.pallas_call(
        body, grid=(nt,),
        in_specs=[pl.BlockSpec((tm, IN_W), lambda i: (i, 0)),
                  pl.BlockSpec((NCHIP, D, SHARD_W), lambda i: (0, 0, 0), pipeline_mode=pl.Buffered(1)),
                  pl.BlockSpec((tm, D), lambda i: (i, 0)), pl.BlockSpec((tm, D), lambda i: (i, 0)), _full((1, D))]
        + [_ANY] * (na + 1),
        out_specs=[pl.BlockSpec((tm, D), lambda i: (i, 0)), _full((1, D))] + [_ANY] * (na + 1),
        out_shape=[jax.ShapeDtypeStruct((L, D), f32), jax.ShapeDtypeStruct((1, D), f32)]
        + [jax.ShapeDtypeStruct(p.shape, bf16) for p in parts] + [jax.ShapeDtypeStruct((NCHIP, hs, 128), f32)],
        scratch_shapes=[pltpu.VMEM(p.shape[1:], bf16) for p in parts] + [pltpu.VMEM((hs, 128), f32)]
        + [pltpu.SemaphoreType.DMA((3 * ns,)), pltpu.SemaphoreType.DMA((3 * ns,)), pltpu.SemaphoreType.DMA((na + 1,))],
        name="x_grad_exchange", compiler_params=_cp("arbitrary"))(dproj, w_in, x, gx0, g_pre, *parts, small)


def _sibling_join_list(halves):
    na = len(halves)
    segs = _segments([(h.shape[0], h.shape[1], h.dtype) for h in halves])
    def body(*refs):
        ins, outs, stages = refs[:na], refs[na:2 * na], refs[2 * na:3 * na]
        send_sems, recv_sems, local_sems = refs[3 * na:]
        x, y, c, _ = _place()
        copies = [_rcopy(i, ins[w].at[pl.ds(r0, n), :], outs[w].at[pl.ds(c * half + r0, n), :], send_sems, recv_sems,
                         (x, y, 1 - c)) for i, (w, half, r0, n) in enumerate(segs)]
        for cp in copies:
            cp.start()
        cins = [pltpu.make_async_copy(ins[w], stages[w], local_sems.at[w]) for w in range(na)]
        for cp in cins:
            cp.start()
        own = []
        for w in range(na):
            cins[w].wait()
            half = halves[w].shape[0]
            own.append(pltpu.make_async_copy(stages[w], outs[w].at[pl.ds(c * half, half), :], local_sems.at[w]))
            own[-1].start()
        for cp in copies:
            cp.wait_recv()
        for cp in copies:
            cp.wait_send()
        for cp in own:
            cp.wait()

    return pl.pallas_call(
        body, in_specs=[_ANY] * na, out_specs=[_ANY] * na,
        out_shape=[jax.ShapeDtypeStruct((2 * h.shape[0], h.shape[1]), f32) for h in halves],
        scratch_shapes=[pltpu.VMEM(h.shape, f32) for h in halves]
        + [pltpu.SemaphoreType.DMA((len(segs),)), pltpu.SemaphoreType.DMA((len(segs),)), pltpu.SemaphoreType.DMA((na,))],
        name="sibling_join")(*halves)


def _small_join(v, fs_half):
    hs = fs_half.shape[0]
    def body(v_ref, h_ref, o_ref, fs_ref, send_sems, recv_sems):
        x, y, c, _ = _place()
        me = 4 * x + 2 * y + c
        o_ref[me] = v_ref[...]
        mine = pl.ds(pl.multiple_of(c * hs, 8), hs)
        fs_ref[mine, :] = h_ref[...]
        copies = [_rcopy(7, h_ref, fs_ref.at[mine, :], send_sems, recv_sems, (x, y, 1 - c))]
        i = 0
        for dx in range(2):
            for dy in range(2):
                for dc in range(2):
                    if dx + dy + dc:
                        copies.append(_rcopy(i, v_ref, o_ref.at[me], send_sems, recv_sems, (x ^ dx, y ^ dy, c ^ dc)))
                        i += 1
        for cp in copies:
            cp.start()
        for cp in copies:
            cp.wait_recv()
        for cp in copies:
            cp.wait_send()

    vm = pl.BlockSpec(memory_space=pltpu.VMEM)
    return pl.pallas_call(
        body, in_specs=[vm, vm], out_specs=[vm, vm],
        out_shape=[jax.ShapeDtypeStruct((8, 8, 128), f32), jax.ShapeDtypeStruct((2 * hs, 128), f32)],
        scratch_shapes=[pltpu.SemaphoreType.DMA((8,)), pltpu.SemaphoreType.DMA((8,))],
        name="small_join")(v, fs_half)


def _adamw_rows(parts, w, m, v):
    def body(p_ref, w_ref, m_ref, v_ref, g_ref, d_ref, m2_ref, v2_ref):
        g = p_ref[0]
        for dvc in range(1, 8):
            g = g + p_ref[dvc]
        g_ref[...] = g
        d, m2, v2 = _adamw_math(w_ref[...], g, m_ref[...], v_ref[...])
        d_ref[...] = d
        m2_ref[...] = m2
        v2_ref[...] = v2
    return pl.pallas_call(body, out_shape=[jax.ShapeDtypeStruct((8, 128), f32)] * 4, name="adamw_pre_norm_gain")(
        parts, w, m, v)


def _pair_exchange_list(grads, small):
    na = len(grads)
    segs = _segments([(g.shape[1] // 2, g.shape[2], g.dtype) for g in grads])
    n = NCHIP * len(segs) + 1
    def body(*refs):
        ins, s_ref, outs, rs_ref, (send_sems, recv_sems) = (refs[:na], refs[na], refs[na + 1:2 * na + 1],
                                                            refs[2 * na + 1], refs[2 * na + 2:])
        x, y, c, _ = _place()
        pieces = [(s_ref, rs_ref)]
        for j in range(NCHIP):
            for w, half, r0, rows in segs:
                pieces.append((ins[w].at[j, pl.ds((1 - c) * half + r0, rows), :], outs[w].at[j, pl.ds(r0, rows), :]))
        copies = [_rcopy(i, s, d, send_sems, recv_sems, (x, y, 1 - c)) for i, (s, d) in enumerate(pieces)]
        for cp in copies:
            cp.start()
        for cp in copies:
            cp.wait_recv()
        for cp in copies:
            cp.wait_send()

    return pl.pallas_call(
        body, in_specs=[_ANY] * (na + 1), out_specs=[_ANY] * (na + 1),
        out_shape=[jax.ShapeDtypeStruct((NCHIP, g.shape[1] // 2, g.shape[2]), g.dtype) for g in grads]
        + [jax.ShapeDtypeStruct((SMALL_ROWS, 128), f32)],
        scratch_shapes=[pltpu.SemaphoreType.DMA((n,)), pltpu.SemaphoreType.DMA((n,))],
        name="pair_exchange")(*grads, small)


def _pair_sum_list(c_arr, grads, recvs, small, rsmall):
    na = len(grads)
    def body(c_ref, *refs):
        g_refs, r_refs, s_ref, rs_ref = refs[:na], refs[na:2 * na], refs[2 * na], refs[2 * na + 1]
        o_refs, os_ref = refs[2 * na + 2:3 * na + 2], refs[3 * na + 2]
        for g_ref, r_ref, o_ref in zip(g_refs, r_refs, o_refs):
            o_ref[...] = (g_ref[...].astype(f32) + r_ref[...].astype(f32)).astype(bf16)
        os_ref[...] = s_ref[...] + rs_ref[...]
    half = lambda g: pl.BlockSpec((1, g.shape[1] // 2, g.shape[2]), lambda j, c: (j, c[0], 0))
    low = lambda g: pl.BlockSpec((1, g.shape[1] // 2, g.shape[2]), lambda j, c: (j, 0, 0))
    sm = pl.BlockSpec((SMALL_ROWS, 128), lambda j, c: (0, 0))
    grid_spec = pltpu.PrefetchScalarGridSpec(
        num_scalar_prefetch=1, grid=(NCHIP,),
        in_specs=[half(g) for g in grads] + [low(g) for g in grads] + [sm, sm],
        out_specs=[low(g) for g in grads] + [sm])
    return pl.pallas_call(
        body, grid_spec=grid_spec,
        out_shape=[jax.ShapeDtypeStruct((NCHIP, g.shape[1] // 2, g.shape[2]), bf16) for g in grads]
        + [jax.ShapeDtypeStruct((SMALL_ROWS, 128), f32)],
        name="pair_sum", compiler_params=_cp("arbitrary"))(c_arr, *grads, *recvs, small, rsmall)


def _chip_sum_list(parts, small):
    na = len(parts)
    nt = 2
    def body(*refs):
        for q_ref, f_ref in zip(refs[:na + 1], refs[na + 1:]):
            acc = q_ref[0].astype(f32)
            for j in range(1, NCHIP):
                acc = acc + q_ref[j].astype(f32)
            f_ref[...] = acc
    arrs = list(parts) + [small]
    return pl.pallas_call(
        body, grid=(nt,),
        in_specs=[pl.BlockSpec((NCHIP, a.shape[1] // nt, a.shape[2]), lambda i: (0, i, 0)) for a in arrs],
        out_specs=[pl.BlockSpec((a.shape[1] // nt, a.shape[2]), lambda i: (i, 0)) for a in arrs],
        out_shape=[jax.ShapeDtypeStruct(a.shape[1:], f32) for a in arrs],
        name="chip_sum", compiler_params=_cp("arbitrary"))(*arrs)


_SMALL =(("conv_b", (1, 1024)), ("conv_ln_gain", (1, 1024)), ("conv_ln_bias", (1, 1024)),
          ("ssm_lambda_re", (1, 32, 64)), ("ssm_lambda_im", (1, 32, 64)), ("ssm_log_dt", (1, 32)),
          ("ssm_b_re", (1, 32, 64, 16)), ("ssm_b_im", (1, 32, 64, 16)), ("ssm_c_re", (1, 32, 16, 64)),
          ("ssm_c_im", (1, 32, 16, 64)), ("ssm_d", (1, 32, 16)), ("b_ssm_glu", (1, 512)), ("post_norm_gain", (1, 1024)))


def _pack_small(vals, extra=None):
    rows = []
    for v in list(vals) + ([extra] if extra is not None else []):
        flat = v.reshape(-1).astype(f32)
        n = -(-flat.shape[0] // 1024) * 1024
        rows.append(jnp.pad(flat, (0, n - flat.shape[0])).reshape(-1, 128))
    used = sum(r.shape[0] for r in rows)
    rows.append(jnp.zeros((SMALL_ROWS - used, 128), f32))
    return jnp.concatenate(rows, axis=0)


def _unpack_small(p):
    o = 0
    out = []
    for _, shape in _SMALL:
        n = int(np.prod(shape))
        nr = -(-n // 1024) * 8
        out.append(p[o:o + nr].reshape(-1)[:n].reshape(shape))
        o += nr
    return out, p[o, 0]


def _discretize(lam_re, lam_im, log_dt, b_re, b_im):
    dt = jnp.exp(log_dt)[:, None]
    mag = jnp.exp(lam_re * dt)
    ar = mag * jnp.cos(lam_im * dt)
    ai = mag * jnp.sin(lam_im * dt)
    den = lam_re * lam_re + lam_im * lam_im
    zr = ((ar - 1.0) * lam_re + ai * lam_im) / den
    zi = (ai * lam_re - (ar - 1.0) * lam_im) / den
    bbr = zr[..., None] * b_re - zi[..., None] * b_im
    bbi = zr[..., None] * b_im + zi[..., None] * b_re
    return ar, ai, bbr, bbi


_EYE8 = np.eye(8, dtype=np.float32)


def _bbt_blocks(bb):
    v = bb.reshape(4, 8, PST, H).transpose(0, 1, 3, 2)
    return jnp.einsum("bghp,gk->bghkp", v, _EYE8).reshape(4, 128, 512)


def _bbt_unblock(m):
    v = jnp.einsum("bghkp,gk->bghp", m.reshape(4, 8, H, 8, PST), _EYE8)
    return v.transpose(0, 1, 3, 2).reshape(G, PST, H)


def _ct_blocks(cc):
    v = cc.reshape(4, 8, H, PST)
    return jnp.einsum("bghp,gk->bgpkh", v, _EYE8).reshape(4, 512, 128)


def _ct_unblock(m):
    return jnp.einsum("bghkp,gk->bghp", m.reshape(4, 8, H, 8, PST), _EYE8).reshape(G, H, PST)


def _perm_matrix():
    p = np.zeros((TC, TC), np.float32)
    for r in range(R):
        for seg in range(8):
            p[r * 8 + seg, seg * R + r] = 1.0
    return p


def _deinterleave(a):
    L, C = a.shape
    return a.reshape(L // TC, R, 8, C).transpose(0, 2, 1, 3).reshape(L, C)


def _fwd_bwd(h, xi, ti, proj, conv_w, w_co, w_glu, w_so, w_out, small):
    (conv_b, ln_g, ln_b, lam_re, lam_im, log_dt, b_re, b_im, c_re, c_im, dvec, b_glu, g_post) = small
    lam_re, lam_im, log_dt = lam_re[0], lam_im[0], log_dt[0]
    b_re, b_im, c_re, c_im = b_re[0], b_im[0], c_re[0], c_im[0]
    (ar, ai, bbr, bbi), disc_vjp = jax.vjp(_discretize, lam_re, lam_im, log_dt, b_re, b_im)
    a_re = ar.reshape(1, NS)
    a_im = ai.reshape(1, NS)
    dt = jnp.exp(log_dt)[:, None]
    steps = jnp.arange(1, R + 1, dtype=f32)[:, None, None]
    apow_re = (jnp.exp(steps * (lam_re * dt)) * jnp.cos(steps * (lam_im * dt))).reshape(R, NS)
    apow_im = (jnp.exp(steps * (lam_re * dt)) * jnp.sin(steps * (lam_im * dt))).reshape(R, NS)
    bbt_re, bbt_im = _bbt_blocks(bbr).astype(bf16), _bbt_blocks(bbi).astype(bf16)
    ct_re, ct_im = _ct_blocks(c_re).astype(bf16), _ct_blocks(c_im).astype(bf16)
    d_row = dvec.reshape(1, SW)
    cw32 = jnp.pad(conv_w, ((0, 1), (0, 0)))

    cu1, a_in = _conv_fwd(proj, cw32, conv_b, ln_g, ln_b)
    y0, b_in, sre, sim, cinr, cini = _ssm_fwd(proj, bbt_re, bbt_im, ct_re, ct_im, a_re, a_im,
                                              apow_re, apow_im, d_row, w_glu, b_glu)
    gx0, d_ain, d_bin, dproj, dw_out, dw_co, dw_so, dg_post, loss = _tail(
        a_in, b_in, proj, xi, ti, w_co, w_so, w_out, g_post)
    (dproj, dbbt_re, dbbt_im, dct_re, dct_im, dd, dar8, dai8, dw_glu, db_glu) = _ssm_bwd(
        d_bin, y0, proj, sre, sim, cinr, cini, bbt_re, bbt_im, ct_re, ct_im,
        a_re, a_im, apow_re, apow_im, d_row, w_glu, b_glu, dproj)
    dproj, dcw8, d_convb, d_lng, d_lnb = _conv_bwd(d_ain, cu1, proj, cw32, ln_g, ln_b, dproj)
    dw_in = _win_grad(h, dproj)

    d_ar = jnp.sum(dar8, axis=0).reshape(G, PST)
    d_ai = jnp.sum(dai8, axis=0).reshape(G, PST)
    d_lre, d_lim, d_ldt, d_bre, d_bim = disc_vjp((d_ar, d_ai, _bbt_unblock(dbbt_re), _bbt_unblock(dbbt_im)))
    d_conv_w = jnp.sum(dcw8, axis=1)[:KS]
    small_grads = [d_convb, d_lng, d_lnb, d_lre[None], d_lim[None], d_ldt[None], d_bre[None], d_bim[None],
                   _ct_unblock(dct_re)[None], _ct_unblock(dct_im)[None], dd.reshape(1, G, H), db_glu, dg_post]
    return loss[0, 0], gx0, dproj, (dw_in, dw_co, dw_out, dw_glu, dw_so, d_conv_w), small_grads


def kernel(x, pre_norm_gain, w_in, conv_w, conv_b, conv_ln_gain, conv_ln_bias, w_conv_out, ssm_lambda_re, ssm_lambda_im, ssm_log_dt, ssm_b_re, ssm_b_im, ssm_c_re, ssm_c_im, ssm_d, w_ssm_glu, b_ssm_glu, w_ssm_out, w_out, post_norm_gain, loss_target, m_pre_norm_gain, m_w_in, m_conv_w, m_conv_b, m_conv_ln_gain, m_conv_ln_bias, m_w_conv_out, m_ssm_lambda_re, m_ssm_lambda_im, m_ssm_log_dt, m_ssm_b_re, m_ssm_b_im, m_ssm_c_re, m_ssm_c_im, m_ssm_d, m_w_ssm_glu, m_b_ssm_glu, m_w_ssm_out, m_w_out, m_post_norm_gain, v_pre_norm_gain, v_w_in, v_conv_w, v_conv_b, v_conv_ln_gain, v_conv_ln_bias, v_w_conv_out, v_ssm_lambda_re, v_ssm_lambda_im, v_ssm_log_dt, v_ssm_b_re, v_ssm_b_im, v_ssm_c_re, v_ssm_c_im, v_ssm_d, v_w_ssm_glu, v_b_ssm_glu, v_w_ssm_out, v_w_out, v_post_norm_gain):
    c = lax.axis_index("c")
    shards = [w_in[0].astype(bf16), w_conv_out[0].astype(bf16), w_out[0].astype(bf16), w_ssm_glu[0].astype(bf16),
              w_ssm_out[0].astype(bf16), jnp.pad(conv_w[0], ((0, CONV_ROWS - KS), (0, 0)))]
    k_arr = (2 * lax.axis_index("x") + lax.axis_index("y")).astype(jnp.int32).reshape(1)
    w_in_g, w_co_g, w_out_g, w_glu_g, w_so_g, conv_w_g, h, xi, ti, proj = _gather_prep(
        k_arr, shards, x[0], loss_target[0], pre_norm_gain, jnp.asarray(_perm_matrix(), bf16))
    conv_w_f = conv_w_g[:, :KS].transpose(1, 0, 2).reshape(KS, CW)

    small = (conv_b, conv_ln_gain, conv_ln_bias, ssm_lambda_re, ssm_lambda_im, ssm_log_dt, ssm_b_re,
             ssm_b_im, ssm_c_re, ssm_c_im, ssm_d, b_ssm_glu, post_norm_gain)
    loss_part, gx0, dproj, big_grads, small_grads = _fwd_bwd(
        h, xi, ti, _proj_fwd(k_arr, h, w_in_g, proj), conv_w_f, w_co_g.reshape(CW, D), w_glu_g.reshape(SW, SW), w_so_g,
        w_out_g.reshape(D, D), small)

    dw_in, dw_co, dw_out, dw_glu, dw_so, d_conv_w = big_grads
    d_conv_w = jnp.pad(d_conv_w, ((0, CONV_ROWS - KS), (0, 0))).reshape(CONV_ROWS, NCHIP, 256).transpose(1, 0, 2)
    grads = [dw_in] + [g.astype(bf16) for g in (dw_co.reshape(NCHIP, 256, D), dw_out.reshape(NCHIP, 256, D),
                                                  dw_glu.reshape(NCHIP, 128, SW), dw_so, d_conv_w)]
    gs = _pack_small(small_grads, extra=loss_part)
    *recvs, rs = _pair_exchange_list(grads, gs)
    *parts, ps = _pair_sum_list(c.astype(jnp.int32).reshape(1), grads, recvs, gs, rs)
    gxi, dg_pre, *qparts, qs = _x_grad_exchange(dproj, w_in_g, xi, gx0, pre_norm_gain, parts, ps)
    grad_x = _deinterleave(gxi)
    *halves, fs_half = _chip_sum_list(qparts, qs)
    pre_parts, fs = _small_join(dg_pre.reshape(8, 128), fs_half)
    g_big = list(_sibling_join_list(halves))
    g_big[5] = g_big[5][:KS]

    big_w = (w_in[0], w_conv_out[0], w_out[0], w_ssm_glu[0], w_ssm_out[0], conv_w[0])
    big_m = (m_w_in[0], m_w_conv_out[0], m_w_out[0], m_w_ssm_glu[0], m_w_ssm_out[0], m_conv_w[0])
    big_v = (v_w_in[0], v_w_conv_out[0], v_w_out[0], v_w_ssm_glu[0], v_w_ssm_out[0], v_conv_w[0])
    big_names = ("w_in", "w_conv_out", "w_out", "w_ssm_glu", "w_ssm_out", "conv_w")
    res = {}
    upd = [_adamw("adamw_w_in", big_w[0], g_big[0], big_m[0], big_v[0])]
    upd += _adamw_group("adamw_rest", big_w[1:], g_big[1:], big_m[1:], big_v[1:])
    for n, g, (d, m2, v2) in zip(big_names, g_big, upd):
        res[n] = (g[None], d[None], m2[None], v2[None])

    small_m = (m_conv_b, m_conv_ln_gain, m_conv_ln_bias, m_ssm_lambda_re, m_ssm_lambda_im, m_ssm_log_dt,
               m_ssm_b_re, m_ssm_b_im, m_ssm_c_re, m_ssm_c_im, m_ssm_d, m_b_ssm_glu, m_post_norm_gain)
    small_v = (v_conv_b, v_conv_ln_gain, v_conv_ln_bias, v_ssm_lambda_re, v_ssm_lambda_im, v_ssm_log_dt,
               v_ssm_b_re, v_ssm_b_im, v_ssm_c_re, v_ssm_c_im, v_ssm_d, v_b_ssm_glu, v_post_norm_gain)
    sd, sm, sv = _adamw("adamw_small", _pack_small(small), fs, _pack_small(small_m), _pack_small(small_v))
    sg_l, loss = _unpack_small(fs)
    sd_l, _ = _unpack_small(sd)
    sm_l, _ = _unpack_small(sm)
    sv_l, _ = _unpack_small(sv)
    for i, (n, _) in enumerate(_SMALL):
        res[n] = (sg_l[i], sd_l[i], sm_l[i], sv_l[i])
    rows = lambda a: a.reshape(8, 128)
    pre = _adamw_rows(pre_parts, rows(pre_norm_gain), rows(m_pre_norm_gain), rows(v_pre_norm_gain))
    res["pre_norm_gain"] = tuple(a.reshape(1, D) for a in pre)

    order = ("pre_norm_gain", "w_in", "conv_w", "conv_b", "conv_ln_gain", "conv_ln_bias", "w_conv_out", "ssm_lambda_re",
             "ssm_lambda_im", "ssm_log_dt", "ssm_b_re", "ssm_b_im", "ssm_c_re", "ssm_c_im", "ssm_d", "w_ssm_glu",
             "b_ssm_glu", "w_ssm_out", "w_out", "post_norm_gain")
    outs = [loss, grad_x[None]]
    for q in range(4):
        outs.extend(res[n][q] for n in order)
    return tuple(outs)
```

```python
import math

import numpy as np
import jax
import jax.numpy as jnp
from jax import lax
from jax.experimental import pallas as pl
from jax.experimental.pallas import tpu as pltpu

f32 = jnp.float32
bf16 = jnp.bfloat16

D = 1024
CW = 1024
SW = 512
G = 32
H = 16
PST = 64
NS = G * PST
KS = 31
IN_W = 6144
NCHIP = 4
SHARD_W = IN_W // NCHIP
RMS_EPS = 1e-6
LN_EPS = 1e-5
LR, B1, B2, EPS, WD, STEP = 0.001, 0.9, 0.999, 1e-08, 0.01, 10
GELU_K0 = math.sqrt(2.0 / math.pi)
GELU_K1 = 0.044715

TC = 512
R = TC // 8
NH = 32
LBW = 1024
CONV_ROWS = 64
SMALL_ROWS = 1152
VMEM_LIMIT = 56 * 1024 * 1024
MESH = pl.DeviceIdType.MESH


def _cp(*sem):
    return pltpu.CompilerParams(dimension_semantics=tuple(sem), vmem_limit_bytes=VMEM_LIMIT)


def _sig(v):
    return 0.5 * jnp.tanh(0.5 * v) + 0.5


def _dot(a, b):
    return jnp.dot(a, b, preferred_element_type=f32)


def _dot_nt(a, b):
    return lax.dot_general(a, b, (((1,), (1,)), ((), ())), preferred_element_type=f32)


def _dot_tn(a, b):
    return lax.dot_general(a, b, (((0,), (0,)), ((), ())), preferred_element_type=f32)


def _full(shape):
    nd = len(shape)
    return pl.BlockSpec(shape, lambda *_: (0,) * nd)


def _rows8(i):
    return pl.ds(pl.multiple_of(i * 8, 8), 8)


def _proj_fwd(k_arr, h, w_in, proj):
    L = h.shape[0]
    tm = min(1024, L)
    def body(_, h_ref, w_ref, __, o_ref):
        o_ref[...] = _dot(h_ref[...], w_ref[0]).astype(bf16)
    shard = lambda j, k: (k[0] + 1 + j) % NCHIP
    grid_spec = pltpu.PrefetchScalarGridSpec(
        num_scalar_prefetch=1, grid=(NCHIP - 1, L // tm),
        in_specs=[pl.BlockSpec((tm, D), lambda j, i, k: (i, 0)),
                  pl.BlockSpec((1, D, SHARD_W), lambda j, i, k: (shard(j, k), 0, 0)), _ANY],
        out_specs=pl.BlockSpec((tm, SHARD_W), lambda j, i, k: (i, shard(j, k))))
    return pl.pallas_call(
        body, grid_spec=grid_spec, out_shape=jax.ShapeDtypeStruct((L, IN_W), bf16),
        input_output_aliases={3: 0},
        name="proj_fwd", compiler_params=_cp("arbitrary", "arbitrary"))(k_arr, h, w_in, proj)


NLB = CW // 128
RPI = 32


def _put_blocked(buf, row0, nrows, v):
    for lb in range(NLB):
        buf[lb, pl.ds(row0, nrows), :] = v[:, lb * 128:(lb + 1) * 128]


def _get_blocked(buf, row0, nrows):
    return jnp.concatenate([buf[lb, pl.ds(row0, nrows), :] for lb in range(NLB)], axis=1)


def _fill_before(ebuf, prev):
    sub = lax.broadcasted_iota(jnp.int32, (8, 128), 0)
    def halo(p, carry):
        for lb in range(NLB):
            cur = ebuf[lb, _rows8(R + p), :]
            ebuf[lb, _rows8(p), :] = jnp.where(sub == 0, pltpu.roll(prev[lb, _rows8(p), :], 1, 0),
                                               pltpu.roll(cur, 1, 0))
        return carry
    lax.fori_loop(0, NH, halo, 0)


def _fir(buf, lb, r, coef, first, flip):
    win = buf[lb, pl.ds(pl.multiple_of(r * 8, 8), (KS + RPI - 1) * 8), :]
    outs = []
    for i in range(RPI):
        acc = [first, None, None, None]
        for k in range(KS):
            o = i + ((KS - 1 - k) if flip else k)
            t = coef[k] * win[8 * o:8 * o + 8, :]
            acc[k % 4] = t if acc[k % 4] is None else acc[k % 4] + t
        outs.append((acc[0] + acc[1]) + (acc[2] + acc[3]))
    return outs


def _conv_fwd(proj, cw, cbias, lng, lnb):
    L = proj.shape[0]
    nc = L // TC
    def body(ca_ref, cb_ref, zc_ref, w_ref, b_ref, g_ref, bb_ref, cu1_ref, ain_ref, ebuf, prev, cacc):
        @pl.when(pl.program_id(0) == 0)
        def _():
            prev[...] = jnp.zeros_like(prev)
        def glu(s, carry):
            rows = pl.ds(pl.multiple_of(s * 64, 64), 64)
            _put_blocked(ebuf, pl.multiple_of(NH * 8 + s * 64, 64), 64,
                         ca_ref[rows, :].astype(f32) * _sig(cb_ref[rows, :].astype(f32)))
            return carry
        lax.fori_loop(0, TC // 64, glu, 0)
        _fill_before(ebuf, prev)
        prev[...] = ebuf[:, R * 8:(NH + R) * 8, :]
        for lb in range(NLB):
            sl = slice(lb * 128, (lb + 1) * 128)
            wk = [jnp.broadcast_to(w_ref[k:k + 1, sl], (8, 128)) for k in range(KS)]
            bias = jnp.broadcast_to(b_ref[:, sl], (8, 128))
            def tap(q, carry, lb=lb, wk=wk, bias=bias):
                r = q * RPI
                for i, o in enumerate(_fir(ebuf, lb, r + (NH - KS + 1), wk, bias, False)):
                    cacc[lb, _rows8(r + i), :] = o
                return carry
            lax.fori_loop(0, R // RPI, tap, 0)
        def norm(s, carry):
            rows = pl.ds(pl.multiple_of(s * 64, 64), 64)
            c1b = _get_blocked(cacc, pl.multiple_of(s * 64, 64), 64).astype(bf16)
            cu1_ref[rows, :] = c1b
            c1 = c1b.astype(f32)
            xc = c1 - jnp.mean(c1, axis=-1, keepdims=True)
            var = jnp.mean(xc * xc, axis=-1, keepdims=True)
            ln = xc * lax.rsqrt(var + LN_EPS) * g_ref[...] + bb_ref[...]
            zc = zc_ref[rows, :].astype(f32)
            ain_ref[rows, :] = ((ln * _sig(ln)) * (zc * _sig(zc))).astype(bf16)
            return carry
        lax.fori_loop(0, TC // 64, norm, 0, unroll=4)

    col = lambda c: pl.BlockSpec((TC, CW), lambda i, c=c: (i, c))
    return pl.pallas_call(
        body, grid=(nc,),
        in_specs=[col(0), col(1), col(2), _full((32, CW)), _full((1, CW)), _full((1, CW)), _full((1, CW))],
        out_specs=[pl.BlockSpec((TC, CW), lambda i: (i, 0)), pl.BlockSpec((TC, CW), lambda i: (i, 0))],
        out_shape=[jax.ShapeDtypeStruct((L, CW), bf16), jax.ShapeDtypeStruct((L, CW), bf16)],
        scratch_shapes=[pltpu.VMEM((NLB, (NH + R) * 8, 128), f32), pltpu.VMEM((NLB, NH * 8, 128), f32),
                        pltpu.VMEM((NLB, TC, 128), f32)],
        name="conv_fwd", compiler_params=_cp("arbitrary"))(proj, proj, proj, cw, cbias, lng, lnb)


def _gelu_parts(y0):
    t = jnp.tanh(GELU_K0 * (y0 + GELU_K1 * y0 * y0 * y0))
    return t, 0.5 * y0 * (1.0 + t)


def _ssm_fwd(proj, bbt_re, bbt_im, ct_re, ct_im, a_re, a_im, apow_re, apow_im, dvec, wglu, bglu):
    L = proj.shape[0]
    nc = L // TC
    def body(u_ref, zs_ref, bre_ref, bim_ref, cre_ref, cim_ref, are_ref, aim_ref, pwr_ref, pwi_ref,
             d_ref, wg_ref, bg_ref, y0_ref, bin_ref, sre, sim, cinr, cini, prev_re, prev_im):
        c = pl.program_id(0)
        @pl.when(c == 0)
        def _():
            prev_re[...] = jnp.zeros_like(prev_re)
            prev_im[...] = jnp.zeros_like(prev_im)
        u = u_ref[...]
        for blk in range(4):
            ub = u[:, 128 * blk:128 * (blk + 1)]
            sre[:, 512 * blk:512 * (blk + 1)] = _dot(ub, bre_ref[blk])
            sim[:, 512 * blk:512 * (blk + 1)] = _dot(ub, bim_ref[blk])
        for lb in range(NS // LBW):
            sl = slice(lb * LBW, (lb + 1) * LBW)
            ar = jnp.broadcast_to(are_ref[:, sl], (8, LBW))
            ai = jnp.broadcast_to(aim_ref[:, sl], (8, LBW))
            def step(r, carry, sl=sl, ar=ar, ai=ai):
                sr, si = carry
                nr = ar * sr - ai * si + sre[_rows8(r), sl]
                ni = ar * si + ai * sr + sim[_rows8(r), sl]
                sre[_rows8(r), sl] = nr
                sim[_rows8(r), sl] = ni
                return nr, ni
            lax.fori_loop(1, R, step, (sre[0:8, sl], sim[0:8, sl]))
        a_r = pwr_ref[R - 1:R, :]
        a_i = pwi_ref[R - 1:R, :]
        cr = prev_re[0:1, :]
        ci = prev_im[0:1, :]
        for seg in range(8):
            cinr[seg:seg + 1, :] = cr
            cini[seg:seg + 1, :] = ci
            er = sre[8 * (R - 1) + seg:8 * (R - 1) + seg + 1, :]
            ei = sim[8 * (R - 1) + seg:8 * (R - 1) + seg + 1, :]
            cr, ci = er + a_r * cr - a_i * ci, ei + a_r * ci + a_i * cr
        prev_re[0:1, :] = cr
        prev_im[0:1, :] = ci
        for lb in range(NS // LBW):
            sl = slice(lb * LBW, (lb + 1) * LBW)
            kr = cinr[:, sl]
            ki = cini[:, sl]
            def fix(r, carry, sl=sl, kr=kr, ki=ki):
                pr = jnp.broadcast_to(pwr_ref[pl.ds(r, 1), sl], (8, LBW))
                pi = jnp.broadcast_to(pwi_ref[pl.ds(r, 1), sl], (8, LBW))
                sre[_rows8(r), sl] = sre[_rows8(r), sl] + pr * kr - pi * ki
                sim[_rows8(r), sl] = sim[_rows8(r), sl] + pr * ki + pi * kr
                return carry
            lax.fori_loop(0, R, fix, 0, unroll=4)
        yp = []
        for blk in range(4):
            sr = sre[:, 512 * blk:512 * (blk + 1)].astype(bf16)
            si = sim[:, 512 * blk:512 * (blk + 1)].astype(bf16)
            yp.append(_dot(sr, cre_ref[blk]) - _dot(si, cim_ref[blk]))
        y0 = jnp.concatenate(yp, axis=1) + d_ref[...] * u.astype(f32)
        y0_ref[...] = y0
        _, y1 = _gelu_parts(y0)
        glu = _dot(y1.astype(bf16), wg_ref[...]) + bg_ref[...]
        y2 = y1 * _sig(glu)
        zs = zs_ref[...].astype(f32)
        bin_ref[...] = (y2 * (zs * _sig(zs))).astype(bf16)

    return pl.pallas_call(
        body, grid=(nc,),
        in_specs=[pl.BlockSpec((TC, SW), lambda c: (c, 6)), pl.BlockSpec((TC, SW), lambda c: (c, 7)),
                  _full((4, 128, 512)), _full((4, 128, 512)), _full((4, 512, 128)), _full((4, 512, 128)),
                  _full((1, NS)), _full((1, NS)), _full((R, NS)), _full((R, NS)),
                  _full((1, SW)), _full((SW, SW)), _full((1, SW))],
        out_specs=[pl.BlockSpec((TC, SW), lambda c: (c, 0)), pl.BlockSpec((TC, SW), lambda c: (c, 0)),
                   pl.BlockSpec((TC, NS), lambda c: (c, 0)), pl.BlockSpec((TC, NS), lambda c: (c, 0)),
                   pl.BlockSpec((8, NS), lambda c: (c, 0)), pl.BlockSpec((8, NS), lambda c: (c, 0))],
        out_shape=[jax.ShapeDtypeStruct((L, SW), f32), jax.ShapeDtypeStruct((L, SW), bf16),
                   jax.ShapeDtypeStruct((L, NS), f32), jax.ShapeDtypeStruct((L, NS), f32),
                   jax.ShapeDtypeStruct((nc * 8, NS), f32), jax.ShapeDtypeStruct((nc * 8, NS), f32)],
        scratch_shapes=[pltpu.VMEM((8, NS), f32), pltpu.VMEM((8, NS), f32)],
        name="ssm_fwd", compiler_params=_cp("arbitrary"))(
            proj, proj, bbt_re, bbt_im, ct_re, ct_im, a_re, a_im, apow_re, apow_im, dvec, wglu, bglu)


def _tail(a_in, b_in, proj, x, tgt, wco, wso, wout, gpost):
    L = x.shape[0]
    tm = 512
    def body(a_ref, b_ref, gc_ref, gs_ref, x_ref, t_ref, wco_ref, wso_ref, wout_ref, gp_ref,
             gx_ref, dain_ref, dbin_ref, dp_ref, dwout_ref, dwco_ref, dwso_ref, dgp_ref, loss_ref):
        @pl.when(pl.program_id(0) == 0)
        def _():
            dwout_ref[...] = jnp.zeros_like(dwout_ref)
            dwco_ref[...] = jnp.zeros_like(dwco_ref)
            dwso_ref[...] = jnp.zeros_like(dwso_ref)
            dgp_ref[...] = jnp.zeros_like(dgp_ref)
            loss_ref[...] = jnp.zeros_like(loss_ref)
        a = a_ref[...]
        b = b_ref[...]
        co = _dot(a, wco_ref[...])
        so = jnp.concatenate([_dot(b, wso_ref[j]) for j in range(NCHIP)], axis=1)
        sc = _sig(gc_ref[...].astype(f32))
        ss = _sig(gs_ref[...].astype(f32))
        mb = (sc * co + ss * so).astype(bf16)
        out = _dot(mb, wout_ref[...])
        r2 = lax.rsqrt(jnp.mean(out * out, axis=-1, keepdims=True) + RMS_EPS)
        on = out * r2
        gp = gp_ref[...]
        e = x_ref[...] + on * gp - t_ref[...]
        loss_ref[...] += (0.5 / D) * jnp.sum(e * e)
        dy = e * (1.0 / D)
        gx_ref[...] = dy
        dgp_ref[...] += jnp.sum(dy * on, axis=0, keepdims=True)
        dn = dy * gp
        dout = (r2 * (dn - on * jnp.mean(dn * on, axis=-1, keepdims=True))).astype(bf16)
        dwout_ref[...] += _dot_tn(mb, dout)
        dm = _dot_nt(dout, wout_ref[...])
        dp_ref[:, 0:D] = (dm * co * sc * (1.0 - sc)).astype(bf16)
        dp_ref[:, D:2 * D] = (dm * so * ss * (1.0 - ss)).astype(bf16)
        dco = (dm * sc).astype(bf16)
        dso = (dm * ss).astype(bf16)
        dwco_ref[...] += _dot_tn(a, dco)
        dbin = None
        for j in range(NCHIP):
            dso_j = dso[:, j * 256:(j + 1) * 256]
            dwso_ref[j] += _dot_tn(b, dso_j)
            t = _dot_nt(dso_j, wso_ref[j])
            dbin = t if dbin is None else dbin + t
        dain_ref[...] = _dot_nt(dco, wco_ref[...]).astype(bf16)
        dbin_ref[...] = dbin.astype(bf16)

    row = lambda w: pl.BlockSpec((tm, w), lambda i: (i, 0))
    one = lambda shape: pl.BlockSpec(shape, lambda i: (0,) * len(shape), pipeline_mode=pl.Buffered(1))
    return pl.pallas_call(
        body, grid=(L // tm,),
        in_specs=[row(CW), row(SW), pl.BlockSpec((tm, D), lambda i: (i, 4)), pl.BlockSpec((tm, D), lambda i: (i, 5)),
                  row(D), row(D), one((CW, D)), one((NCHIP, SW, 256)), one((D, D)), one((1, D))],
        out_specs=[row(D), row(CW), row(SW), pl.BlockSpec((tm, 2 * D), lambda i: (i, 2)),
                   one((D, D)), one((CW, D)), one((NCHIP, SW, 256)), one((1, D)), one((1, 128))],
        out_shape=[jax.ShapeDtypeStruct((L, D), f32), jax.ShapeDtypeStruct((L, CW), bf16),
                   jax.ShapeDtypeStruct((L, SW), bf16), jax.ShapeDtypeStruct((L, IN_W), bf16),
                   jax.ShapeDtypeStruct((D, D), f32), jax.ShapeDtypeStruct((CW, D), f32),
                   jax.ShapeDtypeStruct((NCHIP, SW, 256), f32), jax.ShapeDtypeStruct((1, D), f32),
                   jax.ShapeDtypeStruct((1, 128), f32)],
        name="tail", compiler_params=_cp("arbitrary"))(a_in, b_in, proj, proj, x, tgt, wco, wso, wout, gpost)


def _ssm_bwd(d_bin, y0, proj, sre, sim, cinr, cini, bbt_re, bbt_im, ct_re, ct_im,
             a_re, a_im, apow_re, apow_im, dvec, wglu, bglu, dproj):
    L = y0.shape[0]
    nc = L // TC
    def body(dbin_ref, y0_ref, u_ref, zs_ref, sre_ref, sim_ref, cinr_ref, cini_ref,
             bre_ref, bim_ref, cre_ref, cim_ref, are_ref, aim_ref, pwr_ref, pwi_ref, d_ref, wg_ref, bg_ref, _,
             dp_ref, dbre_ref, dbim_ref, dcre_ref, dcim_ref, dd_ref, dar_ref, dai_ref, dwg_ref, dbg_ref,
             gre, gim, gcr, gci, nxt_re, nxt_im):
        @pl.when(pl.program_id(0) == 0)
        def _():
            for ref in (dbre_ref, dbim_ref, dcre_ref, dcim_ref, dd_ref, dar_ref, dai_ref, dwg_ref, dbg_ref,
                        nxt_re, nxt_im):
                ref[...] = jnp.zeros_like(ref)
        y0 = y0_ref[...]
        u = u_ref[...]
        zs = zs_ref[...].astype(f32)
        dbin = dbin_ref[...].astype(f32)
        t, y1 = _gelu_parts(y0)
        y1b = y1.astype(bf16)
        sg = _sig(_dot(y1b, wg_ref[...]) + bg_ref[...])
        sz = _sig(zs)
        d_y2 = dbin * (zs * sz)
        dp_ref[:, SW:2 * SW] = (dbin * (y1 * sg) * (sz * (1.0 + zs * (1.0 - sz)))).astype(bf16)
        d_glu = d_y2 * y1 * sg * (1.0 - sg)
        d_glub = d_glu.astype(bf16)
        d_y1 = d_y2 * sg + _dot_nt(d_glub, wg_ref[...])
        dwg_ref[...] += _dot_tn(y1b, d_glub)
        dbg_ref[...] += jnp.sum(d_glu, axis=0, keepdims=True)
        dgelu = 0.5 * (1.0 + t) + 0.5 * y0 * (1.0 - t * t) * GELU_K0 * (1.0 + 3.0 * GELU_K1 * y0 * y0)
        d_y0 = d_y1 * dgelu
        dd_ref[...] += jnp.sum(d_y0 * u.astype(f32), axis=0, keepdims=True)
        dyb = d_y0.astype(bf16)
        for blk in range(4):
            dy1 = dyb[:, 128 * blk:128 * (blk + 1)]
            gre[:, 512 * blk:512 * (blk + 1)] = _dot_nt(dy1, cre_ref[blk])
            gim[:, 512 * blk:512 * (blk + 1)] = -_dot_nt(dy1, cim_ref[blk])
        for lb in range(NS // LBW):
            sl = slice(lb * LBW, (lb + 1) * LBW)
            ar = jnp.broadcast_to(are_ref[:, sl], (8, LBW))
            ai = jnp.broadcast_to(aim_ref[:, sl], (8, LBW))
            def step(k, carry, sl=sl, ar=ar, ai=ai):
                gr, gi = carry
                row = _rows8(R - 2 - k)
                nr = ar * gr + ai * gi + gre[row, sl]
                ni = ar * gi - ai * gr + gim[row, sl]
                gre[row, sl] = nr
                gim[row, sl] = ni
                return nr, ni
            lax.fori_loop(0, R - 1, step, (gre[8 * (R - 1):8 * R, sl], gim[8 * (R - 1):8 * R, sl]))
        a_r = pwr_ref[R - 1:R, :]
        a_i = pwi_ref[R - 1:R, :]
        cr = nxt_re[0:1, :]
        ci = nxt_im[0:1, :]
        for seg in range(7, -1, -1):
            gcr[seg:seg + 1, :] = cr
            gci[seg:seg + 1, :] = ci
            er = gre[seg:seg + 1, :]
            ei = gim[seg:seg + 1, :]
            cr, ci = er + a_r * cr + a_i * ci, ei + a_r * ci - a_i * cr
        nxt_re[0:1, :] = cr
        nxt_im[0:1, :] = ci
        for lb in range(NS // LBW):
            sl = slice(lb * LBW, (lb + 1) * LBW)
            kr = gcr[:, sl]
            ki = gci[:, sl]
            def fixed(rows, prow, sl=sl, kr=kr, ki=ki):
                pr = jnp.broadcast_to(pwr_ref[prow, sl], (8, LBW))
                pi = jnp.broadcast_to(pwi_ref[prow, sl], (8, LBW))
                gr = gre[rows, sl] + pr * kr + pi * ki
                gi = gim[rows, sl] + pr * ki - pi * kr
                gre[rows, sl] = gr
                gim[rows, sl] = gi
                return gr, gi
            g0r, g0i = fixed(slice(0, 8), slice(R - 1, R))
            p0r, p0i = cinr_ref[:, sl], cini_ref[:, sl]
            acc0 = (g0r * p0r + g0i * p0i, g0i * p0r - g0r * p0i)
            def dacc(r, carry, sl=sl, fixed=fixed):
                xr, xi = carry
                gr, gi = fixed(_rows8(r), pl.ds(R - 1 - r, 1))
                pr, pi = sre_ref[_rows8(r - 1), sl], sim_ref[_rows8(r - 1), sl]
                return xr + gr * pr + gi * pi, xi + gi * pr - gr * pi
            xr, xi = lax.fori_loop(1, R, dacc, acc0)
            dar_ref[:, sl] += xr
            dai_ref[:, sl] += xi
        dup = []
        for blk in range(4):
            s4 = slice(512 * blk, 512 * (blk + 1))
            s1 = slice(128 * blk, 128 * (blk + 1))
            grb = gre[:, s4].astype(bf16)
            gib = gim[:, s4].astype(bf16)
            dup.append(_dot_nt(grb, bre_ref[blk]) + _dot_nt(gib, bim_ref[blk]))
            dbre_ref[blk] += _dot_tn(u[:, s1], grb)
            dbim_ref[blk] += _dot_tn(u[:, s1], gib)
            dcre_ref[blk] += _dot_tn(dyb[:, s1], sre_ref[:, s4].astype(bf16))
            dcim_ref[blk] -= _dot_tn(dyb[:, s1], sim_ref[:, s4].astype(bf16))
        dp_ref[:, 0:SW] = (jnp.concatenate(dup, axis=1) + d_ref[...] * d_y0).astype(bf16)

    rev = lambda w, cidx: pl.BlockSpec((TC, w), lambda i, cidx=cidx: (nc - 1 - i, cidx))
    one = lambda shape: pl.BlockSpec(shape, lambda i: (0,) * len(shape))
    return pl.pallas_call(
        body, grid=(nc,),
        in_specs=[rev(SW, 0), rev(SW, 0), rev(SW, 6), rev(SW, 7), rev(NS, 0), rev(NS, 0),
                  pl.BlockSpec((8, NS), lambda i: (nc - 1 - i, 0)), pl.BlockSpec((8, NS), lambda i: (nc - 1 - i, 0)),
                  one((4, 128, 512)), one((4, 128, 512)), one((4, 512, 128)), one((4, 512, 128)),
                  one((1, NS)), one((1, NS)), one((R, NS)), one((R, NS)),
                  one((1, SW)), one((SW, SW)), one((1, SW)), _ANY],
        out_specs=[pl.BlockSpec((TC, 2 * SW), lambda i: (nc - 1 - i, 3)),
                   one((4, 128, 512)), one((4, 128, 512)), one((4, 128, 512)), one((4, 128, 512)),
                   one((1, SW)), one((8, NS)), one((8, NS)), one((SW, SW)), one((1, SW))],
        out_shape=[jax.ShapeDtypeStruct((L, IN_W), bf16),
                   jax.ShapeDtypeStruct((4, 128, 512), f32), jax.ShapeDtypeStruct((4, 128, 512), f32),
                   jax.ShapeDtypeStruct((4, 128, 512), f32), jax.ShapeDtypeStruct((4, 128, 512), f32),
                   jax.ShapeDtypeStruct((1, SW), f32), jax.ShapeDtypeStruct((8, NS), f32),
                   jax.ShapeDtypeStruct((8, NS), f32), jax.ShapeDtypeStruct((SW, SW), f32),
                   jax.ShapeDtypeStruct((1, SW), f32)],
        scratch_shapes=[pltpu.VMEM((TC, NS), f32), pltpu.VMEM((TC, NS), f32), pltpu.VMEM((8, NS), f32),
                        pltpu.VMEM((8, NS), f32), pltpu.VMEM((8, NS), f32), pltpu.VMEM((8, NS), f32)],
        input_output_aliases={19: 0},
        name="ssm_bwd", compiler_params=_cp("arbitrary"))(
            d_bin, y0, proj, proj, sre, sim, cinr, cini, bbt_re, bbt_im, ct_re, ct_im,
            a_re, a_im, apow_re, apow_im, dvec, wglu, bglu, dproj)


def _conv_bwd(d_ain, cu1, proj, cw, lng, lnb, dproj):
    L = cu1.shape[0]
    nc = L // TC
    def body(dain_ref, cu1_ref, ca_ref, cb_ref, zc_ref, cah_ref, cbh_ref, w_ref, g_ref, bb_ref, _,
             dp_ref, dw_ref, dbias_ref, dlng_ref, dlnb_ref, dbuf, ebuf, prev, nxt, dcu0):
        i = pl.program_id(0)
        @pl.when(i == 0)
        def _():
            dw_ref[...] = jnp.zeros_like(dw_ref)
            dbias_ref[...] = jnp.zeros_like(dbias_ref)
            dlng_ref[...] = jnp.zeros_like(dlng_ref)
            dlnb_ref[...] = jnp.zeros_like(dlnb_ref)
            nxt[...] = jnp.zeros_like(nxt)
        def lnb(s, carry):
            rows = pl.ds(pl.multiple_of(s * 32, 32), 32)
            dain = dain_ref[rows, :].astype(f32)
            c1 = cu1_ref[rows, :].astype(f32)
            zc = zc_ref[rows, :].astype(f32)
            xc = c1 - jnp.mean(c1, axis=-1, keepdims=True)
            var = jnp.mean(xc * xc, axis=-1, keepdims=True)
            rstd = lax.rsqrt(var + LN_EPS)
            xh = xc * rstd
            ln = xh * g_ref[...] + bb_ref[...]
            sl_ = _sig(ln)
            sz = _sig(zc)
            dp_ref[rows, 2 * CW:3 * CW] = (dain * (ln * sl_) * (sz * (1.0 + zc * (1.0 - sz)))).astype(bf16)
            d_ln = dain * (zc * sz) * (sl_ * (1.0 + ln * (1.0 - sl_)))
            dlng_ref[...] += jnp.sum(d_ln * xh, axis=0, keepdims=True)
            dlnb_ref[...] += jnp.sum(d_ln, axis=0, keepdims=True)
            dxh = d_ln * g_ref[...]
            d_c1 = rstd * (dxh - jnp.mean(dxh, axis=-1, keepdims=True)
                           - xh * jnp.mean(dxh * xh, axis=-1, keepdims=True))
            dbias_ref[...] += jnp.sum(d_c1, axis=0, keepdims=True)
            _put_blocked(dbuf, pl.multiple_of(s * 32, 32), 32, d_c1)
            _put_blocked(ebuf, pl.multiple_of(NH * 8 + s * 32, 32), 32,
                         ca_ref[rows, :].astype(f32) * _sig(cb_ref[rows, :].astype(f32)))
            return carry
        lax.fori_loop(0, TC // 32, lnb, 0, unroll=4)
        sub = lax.broadcasted_iota(jnp.int32, (8, 128), 0)
        def after(p, carry):
            for lb in range(NLB):
                cur = dbuf[lb, _rows8(p), :]
                dbuf[lb, _rows8(R + p), :] = jnp.where(sub == 7, pltpu.roll(nxt[lb, _rows8(p), :], 7, 0),
                                                       pltpu.roll(cur, 7, 0))
            return carry
        lax.fori_loop(0, NH, after, 0)
        nxt[...] = dbuf[:, 0:NH * 8, :]
        def before(s, carry):
            rows = pl.ds(pl.multiple_of(s * 64, 64), 64)
            v = cah_ref[rows, :].astype(f32) * _sig(cbh_ref[rows, :].astype(f32))
            _put_blocked(prev, pl.multiple_of(s * 64, 64), 64, jnp.where(i == nc - 1, jnp.zeros_like(v), v))
            return carry
        lax.fori_loop(0, NH * 8 // 64, before, 0)
        _fill_before(ebuf, prev)
        for lb in range(NLB):
            sl = slice(lb * 128, (lb + 1) * 128)
            wk = [jnp.broadcast_to(w_ref[k:k + 1, sl], (8, 128)) for k in range(KS)]
            def tap(q, carry, lb=lb, wk=wk):
                r = q * RPI
                for j, o in enumerate(_fir(dbuf, lb, r, wk, None, True)):
                    dcu0[lb, _rows8(r + j), :] = o
                return carry
            lax.fori_loop(0, R // RPI, tap, 0)
            def wgrad(q, accs, lb=lb):
                r = q * RPI
                dvs = dbuf[lb, pl.ds(pl.multiple_of(r * 8, 8), RPI * 8), :]
                win = ebuf[lb, pl.ds(pl.multiple_of((r + (NH - KS + 1)) * 8, 8), (KS + RPI - 1) * 8), :]
                accs = list(accs)
                for j in range(RPI):
                    dv = dvs[8 * j:8 * j + 8, :]
                    for k in range(KS):
                        accs[k] = accs[k] + dv * win[8 * (j + k):8 * (j + k) + 8, :]
                return tuple(accs)
            accs = lax.fori_loop(0, R // RPI, wgrad, tuple(jnp.zeros((8, 128), f32) for _ in range(KS)))
            for k in range(KS):
                dw_ref[k, :, sl] += accs[k]
        def glub(s, carry):
            rows = pl.ds(pl.multiple_of(s * 64, 64), 64)
            d0 = _get_blocked(dcu0, pl.multiple_of(s * 64, 64), 64)
            ca = ca_ref[rows, :].astype(f32)
            sb = _sig(cb_ref[rows, :].astype(f32))
            dp_ref[rows, 0:CW] = (d0 * sb).astype(bf16)
            dp_ref[rows, CW:2 * CW] = (d0 * ca * sb * (1.0 - sb)).astype(bf16)
            return carry
        lax.fori_loop(0, TC // 64, glub, 0)

    hrows = NH * 8
    per = TC // hrows
    rev = lambda cidx: pl.BlockSpec((TC, CW), lambda i, cidx=cidx: (nc - 1 - i, cidx))
    halo = lambda cidx: pl.BlockSpec((hrows, CW), lambda i, cidx=cidx: (jnp.maximum((nc - 1 - i) * per - 1, 0), cidx))
    one = lambda shape: pl.BlockSpec(shape, lambda i: (0,) * len(shape))
    return pl.pallas_call(
        body, grid=(nc,),
        in_specs=[rev(0), rev(0), rev(0), rev(1), rev(2), halo(0), halo(1), one((32, CW)), one((1, CW)), one((1, CW)),
                  _ANY],
        out_specs=[pl.BlockSpec((TC, 3 * CW), lambda i: (nc - 1 - i, 0)), one((32, 8, CW)), one((1, CW)), one((1, CW)), one((1, CW))],
        out_shape=[jax.ShapeDtypeStruct((L, IN_W), bf16), jax.ShapeDtypeStruct((32, 8, CW), f32),
                   jax.ShapeDtypeStruct((1, CW), f32), jax.ShapeDtypeStruct((1, CW), f32),
                   jax.ShapeDtypeStruct((1, CW), f32)],
        scratch_shapes=[pltpu.VMEM((NLB, (R + NH) * 8, 128), f32), pltpu.VMEM((NLB, (NH + R) * 8, 128), f32),
                        pltpu.VMEM((NLB, hrows, 128), f32), pltpu.VMEM((NLB, hrows, 128), f32),
                        pltpu.VMEM((NLB, TC, 128), f32)],
        input_output_aliases={10: 0},
        name="conv_bwd", compiler_params=_cp("arbitrary"))(d_ain, cu1, proj, proj, proj, proj, proj, cw, lng, lnb, dproj)


def _win_grad(h, dproj):
    L = h.shape[0]
    tm = min(2048, L)
    nt = L // tm
    def body(h_ref, d_ref, o_ref, acc):
        i = pl.program_id(1)
        @pl.when(i == 0)
        def _():
            acc[...] = jnp.zeros_like(acc)
        acc[...] += _dot_tn(h_ref[...], d_ref[...])
        @pl.when(i == nt - 1)
        def _():
            o_ref[0] = acc[...].astype(bf16)
    return pl.pallas_call(
        body, grid=(NCHIP, nt),
        in_specs=[pl.BlockSpec((tm, D), lambda j, i: (i, 0)), pl.BlockSpec((tm, SHARD_W), lambda j, i: (i, j))],
        out_specs=pl.BlockSpec((1, D, SHARD_W), lambda j, i: (j, 0, 0)),
        out_shape=jax.ShapeDtypeStruct((NCHIP, D, SHARD_W), bf16),
        scratch_shapes=[pltpu.VMEM((D, SHARD_W), f32)],
        name="win_grad", compiler_params=_cp("arbitrary", "arbitrary"))(h, dproj)


def _adamw_math(w, g, m, v):
    m2 = B1 * m + (1.0 - B1) * g
    v2 = B2 * v + (1.0 - B2) * (g * g)
    m_hat = m2 / (1.0 - B1 ** STEP)
    v_hat = v2 / (1.0 - B2 ** STEP)
    delta = -LR * (m_hat / (jnp.sqrt(v_hat) + EPS) + WD * w)
    return delta, m2, v2


def _adamw(name, w, g, m, v):
    rows, cols = w.shape
    tm = rows if rows <= 256 else (256 if rows % 256 == 0 else 128)
    assert rows % tm == 0
    def body(w_ref, g_ref, m_ref, v_ref, d_ref, m2_ref, v2_ref):
        d, m2, v2 = _adamw_math(w_ref[...], g_ref[...], m_ref[...], v_ref[...])
        d_ref[...] = d
        m2_ref[...] = m2
        v2_ref[...] = v2
    spec = pl.BlockSpec((tm, cols), lambda i: (i, 0))
    shp = jax.ShapeDtypeStruct((rows, cols), f32)
    return pl.pallas_call(
        body, grid=(rows // tm,), in_specs=[spec] * 4, out_specs=[spec] * 3, out_shape=[shp] * 3,
        name=name, compiler_params=_cp("arbitrary"))(w, g, m, v)


def _adamw_group(name, ws, gs, ms, vs):
    n = len(ws)
    def body(*refs):
        for i in range(n):
            w_ref, g_ref, m_ref, v_ref = (refs[q * n + i] for q in range(4))
            d, m2, v2 = _adamw_math(w_ref[...], g_ref[...], m_ref[...], v_ref[...])
            for q, val in enumerate((d, m2, v2)):
                refs[(4 + q) * n + i][...] = val
    shapes = [jax.ShapeDtypeStruct(w.shape, f32) for w in ws]
    out = pl.pallas_call(body, out_shape=shapes * 3, name=name,
                         compiler_params=pltpu.CompilerParams(vmem_limit_bytes=VMEM_LIMIT))(*ws, *gs, *ms, *vs)
    return [(out[i], out[n + i], out[2 * n + i]) for i in range(n)]


_ANY = pl.BlockSpec(memory_space=pl.ANY)


def _chunks(rows, parts):
    step = rows // parts
    assert step * parts == rows and step % 16 == 0
    return [(i * step, step) for i in range(parts)]


def _place():
    x, y, c = lax.axis_index("x"), lax.axis_index("y"), lax.axis_index("c")
    chips = [(1 - x, y), (x, 1 - y), (1 - x, 1 - y)]
    return x, y, c, chips


def _nchunks(half, cols, itemsize):
    return 4 if half * cols * itemsize >= (1 << 20) else 1


def _segments(metas):
    segs = []
    for w, (half, cols, dt) in enumerate(metas):
        for r0, n in _chunks(half, _nchunks(half, cols, jnp.dtype(dt).itemsize)):
            segs.append((w, half, r0, n))
    return segs


def _rcopy(i, src, dst, send_sems, recv_sems, to):
    return pltpu.make_async_remote_copy(src_ref=src, dst_ref=dst, send_sem=send_sems.at[i], recv_sem=recv_sems.at[i],
                                        device_id=to, device_id_type=MESH)


def _gather_prep(k_arr, shards, x, tgt, g_pre, perm):
    na = len(shards)
    L = x.shape[0]
    nc = L // TC
    segs = _segments([(a.shape[0] // 2, a.shape[1], a.dtype) for a in shards])
    ns = len(segs)
    def body(_, *refs):
        ins = refs[:na]
        x_ref, t_ref, g_ref, p_ref = refs[na:na + 4]
        outs = refs[na + 4:2 * na + 4]
        h_ref, xi_ref, ti_ref, proj_ref = refs[2 * na + 4:2 * na + 8]
        stages = refs[2 * na + 8:3 * na + 8]
        send_sems, recv_sems, local_sems = refs[3 * na + 8:]
        i = pl.program_id(0)
        x, y, c, chips = _place()
        k = 2 * x + y
        me, sibling = (x, y, c), (x, y, 1 - c)

        def dst(w, half, chip, pc, r0, n):
            return outs[w].at[chip, pl.ds(pc * half + r0, n), :]

        def firsts():
            return [_rcopy(j * ns + s, ins[w].at[pl.ds(c * half + r0, n), :], dst(w, half, k, c, r0, n),
                           send_sems, recv_sems, (*chip, c))
                    for j, chip in enumerate(chips) for s, (w, half, r0, n) in enumerate(segs)]

        def own_out(w):
            return pltpu.make_async_copy(stages[w], outs[w].at[k], local_sems.at[w])

        @pl.when(i == 0)
        def _():
            for cp in firsts():
                cp.start()
            cins = [pltpu.make_async_copy(ins[w], stages[w], local_sems.at[w]) for w in range(na)]
            for cp in cins:
                cp.start()
            for w in range(na):
                cins[w].wait()
                own_out(w).start()

        p = p_ref[...]
        def through(v):
            hi = v.astype(bf16)
            r1 = v - hi.astype(f32)
            mid = r1.astype(bf16)
            lo = (r1 - mid.astype(f32)).astype(bf16)
            return (_dot(p, hi) + _dot(p, mid)) + _dot(p, lo)
        xt = x_ref[...]
        r = lax.rsqrt(jnp.mean(xt * xt, axis=-1, keepdims=True) + RMS_EPS)
        hp = _dot(p, (xt * r * g_ref[...]).astype(bf16)).astype(bf16)
        h_ref[...] = hp
        proj_ref[...] = _dot(hp, stages[0][...]).astype(bf16)
        xi_ref[...] = through(xt)
        ti_ref[...] = through(t_ref[...])

        @pl.when(i == nc - 1)
        def _():
            passed = []
            for j, chip in enumerate(chips):
                cj = 2 * chip[0] + chip[1]
                for s, (w, half, r0, n) in enumerate(segs):
                    landed = dst(w, half, cj, c, r0, n)
                    _rcopy(j * ns + s, landed, landed, send_sems, recv_sems, me).wait_recv()
                    fwd = _rcopy(3 * ns + j * ns + s, landed, landed, send_sems, recv_sems, sibling)
                    fwd.start()
                    passed.append(fwd)
            for j, chip in enumerate(chips):
                cj = 2 * chip[0] + chip[1]
                for s, (w, half, r0, n) in enumerate(segs):
                    theirs = dst(w, half, cj, 1 - c, r0, n)
                    _rcopy(3 * ns + j * ns + s, theirs, theirs, send_sems, recv_sems, me).wait_recv()
            for cp in firsts() + passed:
                cp.wait_send()
            for w in range(na):
                own_out(w).wait()

    row = lambda: pl.BlockSpec((TC, D), lambda i, k: (i, 0))
    grid_spec = pltpu.PrefetchScalarGridSpec(
        num_scalar_prefetch=1, grid=(nc,),
        in_specs=[_ANY] * na + [row(), row(), pl.BlockSpec((1, D), lambda i, k: (0, 0)),
                                pl.BlockSpec((TC, TC), lambda i, k: (0, 0))],
        out_specs=[_ANY] * na + [row(), row(), row(), pl.BlockSpec((TC, SHARD_W), lambda i, k: (i, k[0]))],
        scratch_shapes=[pltpu.VMEM(a.shape, a.dtype) for a in shards]
        + [pltpu.SemaphoreType.DMA((6 * ns,)), pltpu.SemaphoreType.DMA((6 * ns,)), pltpu.SemaphoreType.DMA((na,))])
    return pl.pallas_call(
        body, grid_spec=grid_spec,
        out_shape=[jax.ShapeDtypeStruct((NCHIP,) + a.shape, a.dtype) for a in shards]
        + [jax.ShapeDtypeStruct((L, D), bf16), jax.ShapeDtypeStruct((L, D), f32), jax.ShapeDtypeStruct((L, D), f32),
           jax.ShapeDtypeStruct((L, IN_W), bf16)],
        name="gather_prep", compiler_params=_cp("arbitrary"))(k_arr, *shards, x, tgt, g_pre, perm)


def _x_grad_exchange(dproj, w_in, x, gx0, g_pre, parts, small):
    L = x.shape[0]
    tm = 512
    nt = L // tm
    na = len(parts)
    hs = SMALL_ROWS // 2
    segs = _segments([(p.shape[1], p.shape[2], p.dtype) for p in parts])
    ns = len(segs) + 1
    def body(*refs):
        d_ref, w_ref, x_ref, gx_ref, g_ref = refs[:5]
        ins, s_ref = refs[5:5 + na], refs[5 + na]
        o_ref, dg_ref = refs[6 + na:8 + na]
        outs, qs_ref = refs[8 + na:8 + 2 * na], refs[8 + 2 * na]
        stages = refs[9 + 2 * na:10 + 3 * na]
        send_sems, recv_sems, local_sems = refs[10 + 3 * na:]
        i = pl.program_id(0)
        x, y, c, chips = _place()
        k = 2 * x + y

        def my_small():
            return s_ref.at[pl.ds(c * hs, hs), :]

        def copies():
            out = []
            for j, chip in enumerate(chips):
                cj = 2 * chip[0] + chip[1]
                pieces = [(my_small(), qs_ref.at[k])]
                pieces += [(ins[w].at[cj, pl.ds(r0, n), :], outs[w].at[k, pl.ds(r0, n), :]) for w, _, r0, n in segs]
                out += [_rcopy(ns * j + s, src, d, send_sems, recv_sems, (*chip, c)) for s, (src, d) in enumerate(pieces)]
            return out

        def own_out(w):
            dst = qs_ref.at[k] if w == na else outs[w].at[k]
            return pltpu.make_async_copy(stages[w], dst, local_sems.at[w])

        @pl.when(i == 0)
        def _():
            dg_ref[...] = jnp.zeros_like(dg_ref)
            for cp in copies():
                cp.start()
            cins = [pltpu.make_async_copy(my_small() if w == na else ins[w].at[k], stages[w], local_sems.at[w])
                    for w in range(na + 1)]
            for cp in cins:
                cp.start()
            for w in range(na + 1):
                cins[w].wait()
                own_out(w).start()

        dh = _dot_nt(d_ref[:, 0:SHARD_W], w_ref[0])
        for j in range(1, NCHIP):
            dh = dh + _dot_nt(d_ref[:, j * SHARD_W:(j + 1) * SHARD_W], w_ref[j])
        xt = x_ref[...]
        r = lax.rsqrt(jnp.mean(xt * xt, axis=-1, keepdims=True) + RMS_EPS)
        xn = xt * r
        dg_ref[...] += jnp.sum(dh * xn, axis=0, keepdims=True)
        dxn = dh * g_ref[...]
        o_ref[...] = gx_ref[...] + r * (dxn - xn * jnp.mean(dxn * xn, axis=-1, keepdims=True))

        @pl.when(i == nt - 1)
        def _():
            for cp in copies():
                cp.wait_recv()
            for cp in copies():
                cp.wait_send()
            for w in range(na + 1):
                own_out(w).wait()

    return pl.pallas_call(
        body, grid=(nt,),
        in_specs=[pl.BlockSpec((tm, IN_W), lambda i: (i, 0)),
                  pl.BlockSpec((NCHIP, D, SHARD_W), lambda i: (0, 0, 0), pipeline_mode=pl.Buffered(1)),
                  pl.BlockSpec((tm, D), lambda i: (i, 0)), pl.BlockSpec((tm, D), lambda i: (i, 0)), _full((1, D))]
        + [_ANY] * (na + 1),
        out_specs=[pl.BlockSpec((tm, D), lambda i: (i, 0)), _full((1, D))] + [_ANY] * (na + 1),
        out_shape=[jax.ShapeDtypeStruct((L, D), f32), jax.ShapeDtypeStruct((1, D), f32)]
        + [jax.ShapeDtypeStruct(p.shape, bf16) for p in parts] + [jax.ShapeDtypeStruct((NCHIP, hs, 128), f32)],
        scratch_shapes=[pltpu.VMEM(p.shape[1:], bf16) for p in parts] + [pltpu.VMEM((hs, 128), f32)]
        + [pltpu.SemaphoreType.DMA((3 * ns,)), pltpu.SemaphoreType.DMA((3 * ns,)), pltpu.SemaphoreType.DMA((na + 1,))],
        name="x_grad_exchange", compiler_params=_cp("arbitrary"))(dproj, w_in, x, gx0, g_pre, *parts, small)


def _sibling_join_list(halves):
    na = len(halves)
    segs = _segments([(h.shape[0], h.shape[1], h.dtype) for h in halves])
    def body(*refs):
        ins, outs, stages = refs[:na], refs[na:2 * na], refs[2 * na:3 * na]
        send_sems, recv_sems, local_sems = refs[3 * na:]
        x, y, c, _ = _place()
        copies = [_rcopy(i, ins[w].at[pl.ds(r0, n), :], outs[w].at[pl.ds(c * half + r0, n), :], send_sems, recv_sems,
                         (x, y, 1 - c)) for i, (w, half, r0, n) in enumerate(segs)]
        for cp in copies:
            cp.start()
        cins = [pltpu.make_async_copy(ins[w], stages[w], local_sems.at[w]) for w in range(na)]
        for cp in cins:
            cp.start()
        own = []
        for w in range(na):
            cins[w].wait()
            half = halves[w].shape[0]
            own.append(pltpu.make_async_copy(stages[w], outs[w].at[pl.ds(c * half, half), :], local_sems.at[w]))
            own[-1].start()
        for cp in copies:
            cp.wait_recv()
        for cp in copies:
            cp.wait_send()
        for cp in own:
            cp.wait()

    return pl.pallas_call(
        body, in_specs=[_ANY] * na, out_specs=[_ANY] * na,
        out_shape=[jax.ShapeDtypeStruct((2 * h.shape[0], h.shape[1]), f32) for h in halves],
        scratch_shapes=[pltpu.VMEM(h.shape, f32) for h in halves]
        + [pltpu.SemaphoreType.DMA((len(segs),)), pltpu.SemaphoreType.DMA((len(segs),)), pltpu.SemaphoreType.DMA((na,))],
        name="sibling_join")(*halves)


def _small_join(v, fs_half):
    hs = fs_half.shape[0]
    def body(v_ref, h_ref, o_ref, fs_ref, send_sems, recv_sems):
        x, y, c, _ = _place()
        me = 4 * x + 2 * y + c
        o_ref[me] = v_ref[...]
        mine = pl.ds(pl.multiple_of(c * hs, 8), hs)
        fs_ref[mine, :] = h_ref[...]
        copies = [_rcopy(7, h_ref, fs_ref.at[mine, :], send_sems, recv_sems, (x, y, 1 - c))]
        i = 0
        for dx in range(2):
            for dy in range(2):
                for dc in range(2):
                    if dx + dy + dc:
                        copies.append(_rcopy(i, v_ref, o_ref.at[me], send_sems, recv_sems, (x ^ dx, y ^ dy, c ^ dc)))
                        i += 1
        for cp in copies:
            cp.start()
        for cp in copies:
            cp.wait_recv()
        for cp in copies:
            cp.wait_send()

    vm = pl.BlockSpec(memory_space=pltpu.VMEM)
    return pl.pallas_call(
        body, in_specs=[vm, vm], out_specs=[vm, vm],
        out_shape=[jax.ShapeDtypeStruct((8, 8, 128), f32), jax.ShapeDtypeStruct((2 * hs, 128), f32)],
        scratch_shapes=[pltpu.SemaphoreType.DMA((8,)), pltpu.SemaphoreType.DMA((8,))],
        name="small_join")(v, fs_half)


def _adamw_rows(parts, w, m, v):
    def body(p_ref, w_ref, m_ref, v_ref, g_ref, d_ref, m2_ref, v2_ref):
        g = p_ref[0]
        for dvc in range(1, 8):
            g = g + p_ref[dvc]
        g_ref[...] = g
        d, m2, v2 = _adamw_math(w_ref[...], g, m_ref[...], v_ref[...])
        d_ref[...] = d
        m2_ref[...] = m2
        v2_ref[...] = v2
    return pl.pallas_call(body, out_shape=[jax.ShapeDtypeStruct((8, 128), f32)] * 4, name="adamw_pre_norm_gain")(
        parts, w, m, v)


def _pair_exchange_list(grads, small):
    na = len(grads)
    segs = _segments([(g.shape[1] // 2, g.shape[2], g.dtype) for g in grads])
    n = NCHIP * len(segs) + 1
    def body(*refs):
        ins, s_ref, outs, rs_ref, (send_sems, recv_sems) = (refs[:na], refs[na], refs[na + 1:2 * na + 1],
                                                            refs[2 * na + 1], refs[2 * na + 2:])
        x, y, c, _ = _place()
        pieces = [(s_ref, rs_ref)]
        for j in range(NCHIP):
            for w, half, r0, rows in segs:
                pieces.append((ins[w].at[j, pl.ds((1 - c) * half + r0, rows), :], outs[w].at[j, pl.ds(r0, rows), :]))
        copies = [_rcopy(i, s, d, send_sems, recv_sems, (x, y, 1 - c)) for i, (s, d) in enumerate(pieces)]
        for cp in copies:
            cp.start()
        for cp in copies:
            cp.wait_recv()
        for cp in copies:
            cp.wait_send()

    return pl.pallas_call(
        body, in_specs=[_ANY] * (na + 1), out_specs=[_ANY] * (na + 1),
        out_shape=[jax.ShapeDtypeStruct((NCHIP, g.shape[1] // 2, g.shape[2]), g.dtype) for g in grads]
        + [jax.ShapeDtypeStruct((SMALL_ROWS, 128), f32)],
        scratch_shapes=[pltpu.SemaphoreType.DMA((n,)), pltpu.SemaphoreType.DMA((n,))],
        name="pair_exchange")(*grads, small)


def _pair_sum_list(c_arr, grads, recvs, small, rsmall):
    na = len(grads)
    def body(c_ref, *refs):
        g_refs, r_refs, s_ref, rs_ref = refs[:na], refs[na:2 * na], refs[2 * na], refs[2 * na + 1]
        o_refs, os_ref = refs[2 * na + 2:3 * na + 2], refs[3 * na + 2]
        for g_ref, r_ref, o_ref in zip(g_refs, r_refs, o_refs):
            o_ref[...] = (g_ref[...].astype(f32) + r_ref[...].astype(f32)).astype(bf16)
        os_ref[...] = s_ref[...] + rs_ref[...]
    half = lambda g: pl.BlockSpec((1, g.shape[1] // 2, g.shape[2]), lambda j, c: (j, c[0], 0))
    low = lambda g: pl.BlockSpec((1, g.shape[1] // 2, g.shape[2]), lambda j, c: (j, 0, 0))
    sm = pl.BlockSpec((SMALL_ROWS, 128), lambda j, c: (0, 0))
    grid_spec = pltpu.PrefetchScalarGridSpec(
        num_scalar_prefetch=1, grid=(NCHIP,),
        in_specs=[half(g) for g in grads] + [low(g) for g in grads] + [sm, sm],
        out_specs=[low(g) for g in grads] + [sm])
    return pl.pallas_call(
        body, grid_spec=grid_spec,
        out_shape=[jax.ShapeDtypeStruct((NCHIP, g.shape[1] // 2, g.shape[2]), bf16) for g in grads]
        + [jax.ShapeDtypeStruct((SMALL_ROWS, 128), f32)],
        name="pair_sum", compiler_params=_cp("arbitrary"))(c_arr, *grads, *recvs, small, rsmall)


def _chip_sum_list(parts, small):
    na = len(parts)
    nt = 2
    def body(*refs):
        for q_ref, f_ref in zip(refs[:na + 1], refs[na + 1:]):
            acc = q_ref[0].astype(f32)
            for j in range(1, NCHIP):
                acc = acc + q_ref[j].astype(f32)
            f_ref[...] = acc
    arrs = list(parts) + [small]
    return pl.pallas_call(
        body, grid=(nt,),
        in_specs=[pl.BlockSpec((NCHIP, a.shape[1] // nt, a.shape[2]), lambda i: (0, i, 0)) for a in arrs],
        out_specs=[pl.BlockSpec((a.shape[1] // nt, a.shape[2]), lambda i: (i, 0)) for a in arrs],
        out_shape=[jax.ShapeDtypeStruct(a.shape[1:], f32) for a in arrs],
        name="chip_sum", compiler_params=_cp("arbitrary"))(*arrs)


_SMALL =(("conv_b", (1, 1024)), ("conv_ln_gain", (1, 1024)), ("conv_ln_bias", (1, 1024)),
          ("ssm_lambda_re", (1, 32, 64)), ("ssm_lambda_im", (1, 32, 64)), ("ssm_log_dt", (1, 32)),
          ("ssm_b_re", (1, 32, 64, 16)), ("ssm_b_im", (1, 32, 64, 16)), ("ssm_c_re", (1, 32, 16, 64)),
          ("ssm_c_im", (1, 32, 16, 64)), ("ssm_d", (1, 32, 16)), ("b_ssm_glu", (1, 512)), ("post_norm_gain", (1, 1024)))


def _pack_small(vals, extra=None):
    rows = []
    for v in list(vals) + ([extra] if extra is not None else []):
        flat = v.reshape(-1).astype(f32)
        n = -(-flat.shape[0] // 1024) * 1024
        rows.append(jnp.pad(flat, (0, n - flat.shape[0])).reshape(-1, 128))
    used = sum(r.shape[0] for r in rows)
    rows.append(jnp.zeros((SMALL_ROWS - used, 128), f32))
    return jnp.concatenate(rows, axis=0)


def _unpack_small(p):
    o = 0
    out = []
    for _, shape in _SMALL:
        n = int(np.prod(shape))
        nr = -(-n // 1024) * 8
        out.append(p[o:o + nr].reshape(-1)[:n].reshape(shape))
        o += nr
    return out, p[o, 0]


def _discretize(lam_re, lam_im, log_dt, b_re, b_im):
    dt = jnp.exp(log_dt)[:, None]
    mag = jnp.exp(lam_re * dt)
    ar = mag * jnp.cos(lam_im * dt)
    ai = mag * jnp.sin(lam_im * dt)
    den = lam_re * lam_re + lam_im * lam_im
    zr = ((ar - 1.0) * lam_re + ai * lam_im) / den
    zi = (ai * lam_re - (ar - 1.0) * lam_im) / den
    bbr = zr[..., None] * b_re - zi[..., None] * b_im
    bbi = zr[..., None] * b_im + zi[..., None] * b_re
    return ar, ai, bbr, bbi


_EYE8 = np.eye(8, dtype=np.float32)


def _bbt_blocks(bb):
    v = bb.reshape(4, 8, PST, H).transpose(0, 1, 3, 2)
    return jnp.einsum("bghp,gk->bghkp", v, _EYE8).reshape(4, 128, 512)


def _bbt_unblock(m):
    v = jnp.einsum("bghkp,gk->bghp", m.reshape(4, 8, H, 8, PST), _EYE8)
    return v.transpose(0, 1, 3, 2).reshape(G, PST, H)


def _ct_blocks(cc):
    v = cc.reshape(4, 8, H, PST)
    return jnp.einsum("bghp,gk->bgpkh", v, _EYE8).reshape(4, 512, 128)


def _ct_unblock(m):
    return jnp.einsum("bghkp,gk->bghp", m.reshape(4, 8, H, 8, PST), _EYE8).reshape(G, H, PST)


def _perm_matrix():
    p = np.zeros((TC, TC), np.float32)
    for r in range(R):
        for seg in range(8):
            p[r * 8 + seg, seg * R + r] = 1.0
    return p


def _deinterleave(a):
    L, C = a.shape
    return a.reshape(L // TC, R, 8, C).transpose(0, 2, 1, 3).reshape(L, C)


def _fwd_bwd(h, xi, ti, proj, conv_w, w_co, w_glu, w_so, w_out, small):
    (conv_b, ln_g, ln_b, lam_re, lam_im, log_dt, b_re, b_im, c_re, c_im, dvec, b_glu, g_post) = small
    lam_re, lam_im, log_dt = lam_re[0], lam_im[0], log_dt[0]
    b_re, b_im, c_re, c_im = b_re[0], b_im[0], c_re[0], c_im[0]
    (ar, ai, bbr, bbi), disc_vjp = jax.vjp(_discretize, lam_re, lam_im, log_dt, b_re, b_im)
    a_re = ar.reshape(1, NS)
    a_im = ai.reshape(1, NS)
    dt = jnp.exp(log_dt)[:, None]
    steps = jnp.arange(1, R + 1, dtype=f32)[:, None, None]
    apow_re = (jnp.exp(steps * (lam_re * dt)) * jnp.cos(steps * (lam_im * dt))).reshape(R, NS)
    apow_im = (jnp.exp(steps * (lam_re * dt)) * jnp.sin(steps * (lam_im * dt))).reshape(R, NS)
    bbt_re, bbt_im = _bbt_blocks(bbr).astype(bf16), _bbt_blocks(bbi).astype(bf16)
    ct_re, ct_im = _ct_blocks(c_re).astype(bf16), _ct_blocks(c_im).astype(bf16)
    d_row = dvec.reshape(1, SW)
    cw32 = jnp.pad(conv_w, ((0, 1), (0, 0)))

    cu1, a_in = _conv_fwd(proj, cw32, conv_b, ln_g, ln_b)
    y0, b_in, sre, sim, cinr, cini = _ssm_fwd(proj, bbt_re, bbt_im, ct_re, ct_im, a_re, a_im,
                                              apow_re, apow_im, d_row, w_glu, b_glu)
    gx0, d_ain, d_bin, dproj, dw_out, dw_co, dw_so, dg_post, loss = _tail(
        a_in, b_in, proj, xi, ti, w_co, w_so, w_out, g_post)
    (dproj, dbbt_re, dbbt_im, dct_re, dct_im, dd, dar8, dai8, dw_glu, db_glu) = _ssm_bwd(
        d_bin, y0, proj, sre, sim, cinr, cini, bbt_re, bbt_im, ct_re, ct_im,
        a_re, a_im, apow_re, apow_im, d_row, w_glu, b_glu, dproj)
    dproj, dcw8, d_convb, d_lng, d_lnb = _conv_bwd(d_ain, cu1, proj, cw32, ln_g, ln_b, dproj)
    dw_in = _win_grad(h, dproj)

    d_ar = jnp.sum(dar8, axis=0).reshape(G, PST)
    d_ai = jnp.sum(dai8, axis=0).reshape(G, PST)
    d_lre, d_lim, d_ldt, d_bre, d_bim = disc_vjp((d_ar, d_ai, _bbt_unblock(dbbt_re), _bbt_unblock(dbbt_im)))
    d_conv_w = jnp.sum(dcw8, axis=1)[:KS]
    small_grads = [d_convb, d_lng, d_lnb, d_lre[None], d_lim[None], d_ldt[None], d_bre[None], d_bim[None],
                   _ct_unblock(dct_re)[None], _ct_unblock(dct_im)[None], dd.reshape(1, G, H), db_glu, dg_post]
    return loss[0, 0], gx0, dproj, (dw_in, dw_co, dw_out, dw_glu, dw_so, d_conv_w), small_grads


def kernel(x, pre_norm_gain, w_in, conv_w, conv_b, conv_ln_gain, conv_ln_bias, w_conv_out, ssm_lambda_re, ssm_lambda_im, ssm_log_dt, ssm_b_re, ssm_b_im, ssm_c_re, ssm_c_im, ssm_d, w_ssm_glu, b_ssm_glu, w_ssm_out, w_out, post_norm_gain, loss_target, m_pre_norm_gain, m_w_in, m_conv_w, m_conv_b, m_conv_ln_gain, m_conv_ln_bias, m_w_conv_out, m_ssm_lambda_re, m_ssm_lambda_im, m_ssm_log_dt, m_ssm_b_re, m_ssm_b_im, m_ssm_c_re, m_ssm_c_im, m_ssm_d, m_w_ssm_glu, m_b_ssm_glu, m_w_ssm_out, m_w_out, m_post_norm_gain, v_pre_norm_gain, v_w_in, v_conv_w, v_conv_b, v_conv_ln_gain, v_conv_ln_bias, v_w_conv_out, v_ssm_lambda_re, v_ssm_lambda_im, v_ssm_log_dt, v_ssm_b_re, v_ssm_b_im, v_ssm_c_re, v_ssm_c_im, v_ssm_d, v_w_ssm_glu, v_b_ssm_glu, v_w_ssm_out, v_w_out, v_post_norm_gain):
    c = lax.axis_index("c")
    shards = [w_in[0].astype(bf16), w_conv_out[0].astype(bf16), w_out[0].astype(bf16), w_ssm_glu[0].astype(bf16),
              w_ssm_out[0].astype(bf16), jnp.pad(conv_w[0], ((0, CONV_ROWS - KS), (0, 0)))]
    k_arr = (2 * lax.axis_index("x") + lax.axis_index("y")).astype(jnp.int32).reshape(1)
    w_in_g, w_co_g, w_out_g, w_glu_g, w_so_g, conv_w_g, h, xi, ti, proj = _gather_prep(
        k_arr, shards, x[0], loss_target[0], pre_norm_gain, jnp.asarray(_perm_matrix(), bf16))
    conv_w_f = conv_w_g[:, :KS].transpose(1, 0, 2).reshape(KS, CW)

    small = (conv_b, conv_ln_gain, conv_ln_bias, ssm_lambda_re, ssm_lambda_im, ssm_log_dt, ssm_b_re,
             ssm_b_im, ssm_c_re, ssm_c_im, ssm_d, b_ssm_glu, post_norm_gain)
    loss_part, gx0, dproj, big_grads, small_grads = _fwd_bwd(
        h, xi, ti, _proj_fwd(k_arr, h, w_in_g, proj), conv_w_f, w_co_g.reshape(CW, D), w_glu_g.reshape(SW, SW), w_so_g,
        w_out_g.reshape(D, D), small)

    dw_in, dw_co, dw_out, dw_glu, dw_so, d_conv_w = big_grads
    d_conv_w = jnp.pad(d_conv_w, ((0, CONV_ROWS - KS), (0, 0))).reshape(CONV_ROWS, NCHIP, 256).transpose(1, 0, 2)
    grads = [dw_in] + [g.astype(bf16) for g in (dw_co.reshape(NCHIP, 256, D), dw_out.reshape(NCHIP, 256, D),
                                                  dw_glu.reshape(NCHIP, 128, SW), dw_so, d_conv_w)]
    gs = _pack_small(small_grads, extra=loss_part)
    *recvs, rs = _pair_exchange_list(grads, gs)
    *parts, ps = _pair_sum_list(c.astype(jnp.int32).reshape(1), grads, recvs, gs, rs)
    gxi, dg_pre, *qparts, qs = _x_grad_exchange(dproj, w_in_g, xi, gx0, pre_norm_gain, parts, ps)
    grad_x = _deinterleave(gxi)
    *halves, fs_half = _chip_sum_list(qparts, qs)
    pre_parts, fs = _small_join(dg_pre.reshape(8, 128), fs_half)
    g_big = list(_sibling_join_list(halves))
    g_big[5] = g_big[5][:KS]

    big_w = (w_in[0], w_conv_out[0], w_out[0], w_ssm_glu[0], w_ssm_out[0], conv_w[0])
    big_m = (m_w_in[0], m_w_conv_out[0], m_w_out[0], m_w_ssm_glu[0], m_w_ssm_out[0], m_conv_w[0])
    big_v = (v_w_in[0], v_w_conv_out[0], v_w_out[0], v_w_ssm_glu[0], v_w_ssm_out[0], v_conv_w[0])
    big_names = ("w_in", "w_conv_out", "w_out", "w_ssm_glu", "w_ssm_out", "conv_w")
    res = {}
    upd = [_adamw("adamw_w_in", big_w[0], g_big[0], big_m[0], big_v[0])]
    upd += _adamw_group("adamw_rest", big_w[1:], g_big[1:], big_m[1:], big_v[1:])
    for n, g, (d, m2, v2) in zip(big_names, g_big, upd):
        res[n] = (g[None], d[None], m2[None], v2[None])

    small_m = (m_conv_b, m_conv_ln_gain, m_conv_ln_bias, m_ssm_lambda_re, m_ssm_lambda_im, m_ssm_log_dt,
               m_ssm_b_re, m_ssm_b_im, m_ssm_c_re, m_ssm_c_im, m_ssm_d, m_b_ssm_glu, m_post_norm_gain)
    small_v = (v_conv_b, v_conv_ln_gain, v_conv_ln_bias, v_ssm_lambda_re, v_ssm_lambda_im, v_ssm_log_dt,
               v_ssm_b_re, v_ssm_b_im, v_ssm_c_re, v_ssm_c_im, v_ssm_d, v_b_ssm_glu, v_post_norm_gain)
    sd, sm, sv = _adamw("adamw_small", _pack_small(small), fs, _pack_small(small_m), _pack_small(small_v))
    sg_l, loss = _unpack_small(fs)
    sd_l, _ = _unpack_small(sd)
    sm_l, _ = _unpack_small(sm)
    sv_l, _ = _unpack_small(sv)
    for i, (n, _) in enumerate(_SMALL):
        res[n] = (sg_l[i], sd_l[i], sm_l[i], sv_l[i])
    rows = lambda a: a.reshape(8, 128)
    pre = _adamw_rows(pre_parts, rows(pre_norm_gain), rows(m_pre_norm_gain), rows(v_pre_norm_gain))
    res["pre_norm_gain"] = tuple(a.reshape(1, D) for a in pre)

    order = ("pre_norm_gain", "w_in", "conv_w", "conv_b", "conv_ln_gain", "conv_ln_bias", "w_conv_out", "ssm_lambda_re",
             "ssm_lambda_im", "ssm_log_dt", "ssm_b_re", "ssm_b_im", "ssm_c_re", "ssm_c_im", "ssm_d", "w_ssm_glu",
             "b_ssm_glu", "w_ssm_out", "w_out", "post_norm_gain")
    outs = [loss, grad_x[None]]
    for q in range(4):
        outs.extend(res[n][q] for n in order)
    return tuple(outs)
```

```python
import math

import numpy as np
import jax
import jax.numpy as jnp
from jax import lax
from jax.experimental import pallas as pl
from jax.experimental.pallas import tpu as pltpu

f32 = jnp.float32
bf16 = jnp.bfloat16

D = 1024
CW = 1024
SW = 512
G = 32
H = 16
PST = 64
NS = G * PST
KS = 31
IN_W = 6144
NCHIP = 4
SHARD_W = IN_W // NCHIP
RMS_EPS = 1e-6
LN_EPS = 1e-5
LR, B1, B2, EPS, WD, STEP = 0.001, 0.9, 0.999, 1e-08, 0.01, 10
GELU_K0 = math.sqrt(2.0 / math.pi)
GELU_K1 = 0.044715

TC = 512
R = TC // 8
NH = 32
LBW = 1024
CONV_ROWS = 64
SMALL_ROWS = 1152
VMEM_LIMIT = 56 * 1024 * 1024
MESH = pl.DeviceIdType.MESH


def _cp(*sem):
    return pltpu.CompilerParams(dimension_semantics=tuple(sem), vmem_limit_bytes=VMEM_LIMIT)


def _sig(v):
    return 0.5 * jnp.tanh(0.5 * v) + 0.5


def _dot(a, b):
    return jnp.dot(a, b, preferred_element_type=f32)


def _dot_nt(a, b):
    return lax.dot_general(a, b, (((1,), (1,)), ((), ())), preferred_element_type=f32)


def _dot_tn(a, b):
    return lax.dot_general(a, b, (((0,), (0,)), ((), ())), preferred_element_type=f32)


def _full(shape):
    nd = len(shape)
    return pl.BlockSpec(shape, lambda *_: (0,) * nd)


def _rows8(i):
    return pl.ds(pl.multiple_of(i * 8, 8), 8)


def _proj_fwd(k_arr, h, w_in, proj):
    L = h.shape[0]
    tm = min(1024, L)
    def body(_, h_ref, w_ref, __, o_ref):
        o_ref[...] = _dot(h_ref[...], w_ref[0]).astype(bf16)
    shard = lambda j, k: (k[0] + 1 + j) % NCHIP
    grid_spec = pltpu.PrefetchScalarGridSpec(
        num_scalar_prefetch=1, grid=(NCHIP - 1, L // tm),
        in_specs=[pl.BlockSpec((tm, D), lambda j, i, k: (i, 0)),
                  pl.BlockSpec((1, D, SHARD_W), lambda j, i, k: (shard(j, k), 0, 0)), _ANY],
        out_specs=pl.BlockSpec((tm, SHARD_W), lambda j, i, k: (i, shard(j, k))))
    return pl.pallas_call(
        body, grid_spec=grid_spec, out_shape=jax.ShapeDtypeStruct((L, IN_W), bf16),
        input_output_aliases={3: 0},
        name="proj_fwd", compiler_params=_cp("arbitrary", "arbitrary"))(k_arr, h, w_in, proj)


NLB = CW // 128
RPI = 32


def _put_blocked(buf, row0, nrows, v):
    for lb in range(NLB):
        buf[lb, pl.ds(row0, nrows), :] = v[:, lb * 128:(lb + 1) * 128]


def _get_blocked(buf, row0, nrows):
    return jnp.concatenate([buf[lb, pl.ds(row0, nrows), :] for lb in range(NLB)], axis=1)


def _fill_before(ebuf, prev):
    sub = lax.broadcasted_iota(jnp.int32, (8, 128), 0)
    def halo(p, carry):
        for lb in range(NLB):
            cur = ebuf[lb, _rows8(R + p), :]
            ebuf[lb, _rows8(p), :] = jnp.where(sub == 0, pltpu.roll(prev[lb, _rows8(p), :], 1, 0),
                                               pltpu.roll(cur, 1, 0))
        return carry
    lax.fori_loop(0, NH, halo, 0)


def _fir(buf, lb, r, coef, first, flip):
    win = buf[lb, pl.ds(pl.multiple_of(r * 8, 8), (KS + RPI - 1) * 8), :]
    outs = []
    for i in range(RPI):
        acc = [first, None, None, None]
        for k in range(KS):
            o = i + ((KS - 1 - k) if flip else k)
            t = coef[k] * win[8 * o:8 * o + 8, :]
            acc[k % 4] = t if acc[k % 4] is None else acc[k % 4] + t
        outs.append((acc[0] + acc[1]) + (acc[2] + acc[3]))
    return outs


def _conv_fwd(proj, cw, cbias, lng, lnb):
    L = proj.shape[0]
    nc = L // TC
    def body(ca_ref, cb_ref, zc_ref, w_ref, b_ref, g_ref, bb_ref, cu1_ref, ain_ref, ebuf, prev, cacc):
        @pl.when(pl.program_id(0) == 0)
        def _():
            prev[...] = jnp.zeros_like(prev)
        def glu(s, carry):
            rows = pl.ds(pl.multiple_of(s * 64, 64), 64)
            _put_blocked(ebuf, pl.multiple_of(NH * 8 + s * 64, 64), 64,
                         ca_ref[rows, :].astype(f32) * _sig(cb_ref[rows, :].astype(f32)))
            return carry
        lax.fori_loop(0, TC // 64, glu, 0)
        _fill_before(ebuf, prev)
        prev[...] = ebuf[:, R * 8:(NH + R) * 8, :]
        for lb in range(NLB):
            sl = slice(lb * 128, (lb + 1) * 128)
            wk = [jnp.broadcast_to(w_ref[k:k + 1, sl], (8, 128)) for k in range(KS)]
            bias = jnp.broadcast_to(b_ref[:, sl], (8, 128))
            def tap(q, carry, lb=lb, wk=wk, bias=bias):
                r = q * RPI
                for i, o in enumerate(_fir(ebuf, lb, r + (NH - KS + 1), wk, bias, False)):
                    cacc[lb, _rows8(r + i), :] = o
                return carry
            lax.fori_loop(0, R // RPI, tap, 0)
        def norm(s, carry):
            rows = pl.ds(pl.multiple_of(s * 64, 64), 64)
            c1b = _get_blocked(cacc, pl.multiple_of(s * 64, 64), 64).astype(bf16)
            cu1_ref[rows, :] = c1b
            c1 = c1b.astype(f32)
            xc = c1 - jnp.mean(c1, axis=-1, keepdims=True)
            var = jnp.mean(xc * xc, axis=-1, keepdims=True)
            ln = xc * lax.rsqrt(var + LN_EPS) * g_ref[...] + bb_ref[...]
            zc = zc_ref[rows, :].astype(f32)
            ain_ref[rows, :] = ((ln * _sig(ln)) * (zc * _sig(zc))).astype(bf16)
            return carry
        lax.fori_loop(0, TC // 64, norm, 0, unroll=4)

    col = lambda c: pl.BlockSpec((TC, CW), lambda i, c=c: (i, c))
    return pl.pallas_call(
        body, grid=(nc,),
        in_specs=[col(0), col(1), col(2), _full((32, CW)), _full((1, CW)), _full((1, CW)), _full((1, CW))],
        out_specs=[pl.BlockSpec((TC, CW), lambda i: (i, 0)), pl.BlockSpec((TC, CW), lambda i: (i, 0))],
        out_shape=[jax.ShapeDtypeStruct((L, CW), bf16), jax.ShapeDtypeStruct((L, CW), bf16)],
        scratch_shapes=[pltpu.VMEM((NLB, (NH + R) * 8, 128), f32), pltpu.VMEM((NLB, NH * 8, 128), f32),
                        pltpu.VMEM((NLB, TC, 128), f32)],
        name="conv_fwd", compiler_params=_cp("arbitrary"))(proj, proj, proj, cw, cbias, lng, lnb)


def _gelu_parts(y0):
    t = jnp.tanh(GELU_K0 * (y0 + GELU_K1 * y0 * y0 * y0))
    return t, 0.5 * y0 * (1.0 + t)


def _ssm_fwd(proj, bbt_re, bbt_im, ct_re, ct_im, a_re, a_im, apow_re, apow_im, dvec, wglu, bglu):
    L = proj.shape[0]
    nc = L // TC
    def body(u_ref, zs_ref, bre_ref, bim_ref, cre_ref, cim_ref, are_ref, aim_ref, pwr_ref, pwi_ref,
             d_ref, wg_ref, bg_ref, y0_ref, bin_ref, sre, sim, cinr, cini, prev_re, prev_im):
        c = pl.program_id(0)
        @pl.when(c == 0)
        def _():
            prev_re[...] = jnp.zeros_like(prev_re)
            prev_im[...] = jnp.zeros_like(prev_im)
        u = u_ref[...]
        for blk in range(4):
            ub = u[:, 128 * blk:128 * (blk + 1)]
            sre[:, 512 * blk:512 * (blk + 1)] = _dot(ub, bre_ref[blk])
            sim[:, 512 * blk:512 * (blk + 1)] = _dot(ub, bim_ref[blk])
        for lb in range(NS // LBW):
            sl = slice(lb * LBW, (lb + 1) * LBW)
            ar = jnp.broadcast_to(are_ref[:, sl], (8, LBW))
            ai = jnp.broadcast_to(aim_ref[:, sl], (8, LBW))
            def step(r, carry, sl=sl, ar=ar, ai=ai):
                sr, si = carry
                nr = ar * sr - ai * si + sre[_rows8(r), sl]
                ni = ar * si + ai * sr + sim[_rows8(r), sl]
                sre[_rows8(r), sl] = nr
                sim[_rows8(r), sl] = ni
                return nr, ni
            lax.fori_loop(1, R, step, (sre[0:8, sl], sim[0:8, sl]))
        a_r = pwr_ref[R - 1:R, :]
        a_i = pwi_ref[R - 1:R, :]
        cr = prev_re[0:1, :]
        ci = prev_im[0:1, :]
        for seg in range(8):
            cinr[seg:seg + 1, :] = cr
            cini[seg:seg + 1, :] = ci
            er = sre[8 * (R - 1) + seg:8 * (R - 1) + seg + 1, :]
            ei = sim[8 * (R - 1) + seg:8 * (R - 1) + seg + 1, :]
            cr, ci = er + a_r * cr - a_i * ci, ei + a_r * ci + a_i * cr
        prev_re[0:1, :] = cr
        prev_im[0:1, :] = ci
        for lb in range(NS // LBW):
            sl = slice(lb * LBW, (lb + 1) * LBW)
            kr = cinr[:, sl]
            ki = cini[:, sl]
            def fix(r, carry, sl=sl, kr=kr, ki=ki):
                pr = jnp.broadcast_to(pwr_ref[pl.ds(r, 1), sl], (8, LBW))
                pi = jnp.broadcast_to(pwi_ref[pl.ds(r, 1), sl], (8, LBW))
                sre[_rows8(r), sl] = sre[_rows8(r), sl] + pr * kr - pi * ki
                sim[_rows8(r), sl] = sim[_rows8(r), sl] + pr * ki + pi * kr
                return carry
            lax.fori_loop(0, R, fix, 0, unroll=4)
        yp = []
        for blk in range(4):
            sr = sre[:, 512 * blk:512 * (blk + 1)].astype(bf16)
            si = sim[:, 512 * blk:512 * (blk + 1)].astype(bf16)
            yp.append(_dot(sr, cre_ref[blk]) - _dot(si, cim_ref[blk]))
        y0 = jnp.concatenate(yp, axis=1) + d_ref[...] * u.astype(f32)
        y0_ref[...] = y0
        _, y1 = _gelu_parts(y0)
        glu = _dot(y1.astype(bf16), wg_ref[...]) + bg_ref[...]
        y2 = y1 * _sig(glu)
        zs = zs_ref[...].astype(f32)
        bin_ref[...] = (y2 * (zs * _sig(zs))).astype(bf16)

    return pl.pallas_call(
        body, grid=(nc,),
        in_specs=[pl.BlockSpec((TC, SW), lambda c: (c, 6)), pl.BlockSpec((TC, SW), lambda c: (c, 7)),
                  _full((4, 128, 512)), _full((4, 128, 512)), _full((4, 512, 128)), _full((4, 512, 128)),
                  _full((1, NS)), _full((1, NS)), _full((R, NS)), _full((R, NS)),
                  _full((1, SW)), _full((SW, SW)), _full((1, SW))],
        out_specs=[pl.BlockSpec((TC, SW), lambda c: (c, 0)), pl.BlockSpec((TC, SW), lambda c: (c, 0)),
                   pl.BlockSpec((TC, NS), lambda c: (c, 0)), pl.BlockSpec((TC, NS), lambda c: (c, 0)),
                   pl.BlockSpec((8, NS), lambda c: (c, 0)), pl.BlockSpec((8, NS), lambda c: (c, 0))],
        out_shape=[jax.ShapeDtypeStruct((L, SW), f32), jax.ShapeDtypeStruct((L, SW), bf16),
                   jax.ShapeDtypeStruct((L, NS), f32), jax.ShapeDtypeStruct((L, NS), f32),
                   jax.ShapeDtypeStruct((nc * 8, NS), f32), jax.ShapeDtypeStruct((nc * 8, NS), f32)],
        scratch_shapes=[pltpu.VMEM((8, NS), f32), pltpu.VMEM((8, NS), f32)],
        name="ssm_fwd", compiler_params=_cp("arbitrary"))(
            proj, proj, bbt_re, bbt_im, ct_re, ct_im, a_re, a_im, apow_re, apow_im, dvec, wglu, bglu)


def _tail(a_in, b_in, proj, x, tgt, wco, wso, wout, gpost):
    L = x.shape[0]
    tm = 512
    def body(a_ref, b_ref, gc_ref, gs_ref, x_ref, t_ref, wco_ref, wso_ref, wout_ref, gp_ref,
             gx_ref, dain_ref, dbin_ref, dp_ref, dwout_ref, dwco_ref, dwso_ref, dgp_ref, loss_ref):
        @pl.when(pl.program_id(0) == 0)
        def _():
            dwout_ref[...] = jnp.zeros_like(dwout_ref)
            dwco_ref[...] = jnp.zeros_like(dwco_ref)
            dwso_ref[...] = jnp.zeros_like(dwso_ref)
            dgp_ref[...] = jnp.zeros_like(dgp_ref)
            loss_ref[...] = jnp.zeros_like(loss_ref)
        a = a_ref[...]
        b = b_ref[...]
        co = _dot(a, wco_ref[...])
        so = jnp.concatenate([_dot(b, wso_ref[j]) for j in range(NCHIP)], axis=1)
        sc = _sig(gc_ref[...].astype(f32))
        ss = _sig(gs_ref[...].astype(f32))
        mb = (sc * co + ss * so).astype(bf16)
        out = _dot(mb, wout_ref[...])
        r2 = lax.rsqrt(jnp.mean(out * out, axis=-1, keepdims=True) + RMS_EPS)
        on = out * r2
        gp = gp_ref[...]
        e = x_ref[...] + on * gp - t_ref[...]
        loss_ref[...] += (0.5 / D) * jnp.sum(e * e)
        dy = e * (1.0 / D)
        gx_ref[...] = dy
        dgp_ref[...] += jnp.sum(dy * on, axis=0, keepdims=True)
        dn = dy * gp
        dout = (r2 * (dn - on * jnp.mean(dn * on, axis=-1, keepdims=True))).astype(bf16)
        dwout_ref[...] += _dot_tn(mb, dout)
        dm = _dot_nt(dout, wout_ref[...])
        dp_ref[:, 0:D] = (dm * co * sc * (1.0 - sc)).astype(bf16)
        dp_ref[:, D:2 * D] = (dm * so * ss * (1.0 - ss)).astype(bf16)
        dco = (dm * sc).astype(bf16)
        dso = (dm * ss).astype(bf16)
        dwco_ref[...] += _dot_tn(a, dco)
        dbin = None
        for j in range(NCHIP):
            dso_j = dso[:, j * 256:(j + 1) * 256]
            dwso_ref[j] += _dot_tn(b, dso_j)
            t = _dot_nt(dso_j, wso_ref[j])
            dbin = t if dbin is None else dbin + t
        dain_ref[...] = _dot_nt(dco, wco_ref[...]).astype(bf16)
        dbin_ref[...] = dbin.astype(bf16)

    row = lambda w: pl.BlockSpec((tm, w), lambda i: (i, 0))
    one = lambda shape: pl.BlockSpec(shape, lambda i: (0,) * len(shape), pipeline_mode=pl.Buffered(1))
    return pl.pallas_call(
        body, grid=(L // tm,),
        in_specs=[row(CW), row(SW), pl.BlockSpec((tm, D), lambda i: (i, 4)), pl.BlockSpec((tm, D), lambda i: (i, 5)),
                  row(D), row(D), one((CW, D)), one((NCHIP, SW, 256)), one((D, D)), one((1, D))],
        out_specs=[row(D), row(CW), row(SW), pl.BlockSpec((tm, 2 * D), lambda i: (i, 2)),
                   one((D, D)), one((CW, D)), one((NCHIP, SW, 256)), one((1, D)), one((1, 128))],
        out_shape=[jax.ShapeDtypeStruct((L, D), f32), jax.ShapeDtypeStruct((L, CW), bf16),
                   jax.ShapeDtypeStruct((L, SW), bf16), jax.ShapeDtypeStruct((L, IN_W), bf16),
                   jax.ShapeDtypeStruct((D, D), f32), jax.ShapeDtypeStruct((CW, D), f32),
                   jax.ShapeDtypeStruct((NCHIP, SW, 256), f32), jax.ShapeDtypeStruct((1, D), f32),
                   jax.ShapeDtypeStruct((1, 128), f32)],
        name="tail", compiler_params=_cp("arbitrary"))(a_in, b_in, proj, proj, x, tgt, wco, wso, wout, gpost)


def _ssm_bwd(d_bin, y0, proj, sre, sim, cinr, cini, bbt_re, bbt_im, ct_re, ct_im,
             a_re, a_im, apow_re, apow_im, dvec, wglu, bglu, dproj):
    L = y0.shape[0]
    nc = L // TC
    def body(dbin_ref, y0_ref, u_ref, zs_ref, sre_ref, sim_ref, cinr_ref, cini_ref,
             bre_ref, bim_ref, cre_ref, cim_ref, are_ref, aim_ref, pwr_ref, pwi_ref, d_ref, wg_ref, bg_ref, _,
             dp_ref, dbre_ref, dbim_ref, dcre_ref, dcim_ref, dd_ref, dar_ref, dai_ref, dwg_ref, dbg_ref,
             gre, gim, gcr, gci, nxt_re, nxt_im):
        @pl.when(pl.program_id(0) == 0)
        def _():
            for ref in (dbre_ref, dbim_ref, dcre_ref, dcim_ref, dd_ref, dar_ref, dai_ref, dwg_ref, dbg_ref,
                        nxt_re, nxt_im):
                ref[...] = jnp.zeros_like(ref)
        y0 = y0_ref[...]
        u = u_ref[...]
        zs = zs_ref[...].astype(f32)
        dbin = dbin_ref[...].astype(f32)
        t, y1 = _gelu_parts(y0)
        y1b = y1.astype(bf16)
        sg = _sig(_dot(y1b, wg_ref[...]) + bg_ref[...])
        sz = _sig(zs)
        d_y2 = dbin * (zs * sz)
        dp_ref[:, SW:2 * SW] = (dbin * (y1 * sg) * (sz * (1.0 + zs * (1.0 - sz)))).astype(bf16)
        d_glu = d_y2 * y1 * sg * (1.0 - sg)
        d_glub = d_glu.astype(bf16)
        d_y1 = d_y2 * sg + _dot_nt(d_glub, wg_ref[...])
        dwg_ref[...] += _dot_tn(y1b, d_glub)
        dbg_ref[...] += jnp.sum(d_glu, axis=0, keepdims=True)
        dgelu = 0.5 * (1.0 + t) + 0.5 * y0 * (1.0 - t * t) * GELU_K0 * (1.0 + 3.0 * GELU_K1 * y0 * y0)
        d_y0 = d_y1 * dgelu
        dd_ref[...] += jnp.sum(d_y0 * u.astype(f32), axis=0, keepdims=True)
        dyb = d_y0.astype(bf16)
        for blk in range(4):
            dy1 = dyb[:, 128 * blk:128 * (blk + 1)]
            gre[:, 512 * blk:512 * (blk + 1)] = _dot_nt(dy1, cre_ref[blk])
            gim[:, 512 * blk:512 * (blk + 1)] = -_dot_nt(dy1, cim_ref[blk])
        for lb in range(NS // LBW):
            sl = slice(lb * LBW, (lb + 1) * LBW)
            ar = jnp.broadcast_to(are_ref[:, sl], (8, LBW))
            ai = jnp.broadcast_to(aim_ref[:, sl], (8, LBW))
            def step(k, carry, sl=sl, ar=ar, ai=ai):
                gr, gi = carry
                row = _rows8(R - 2 - k)
                nr = ar * gr + ai * gi + gre[row, sl]
                ni = ar * gi - ai * gr + gim[row, sl]
                gre[row, sl] = nr
                gim[row, sl] = ni
                return nr, ni
            lax.fori_loop(0, R - 1, step, (gre[8 * (R - 1):8 * R, sl], gim[8 * (R - 1):8 * R, sl]))
        a_r = pwr_ref[R - 1:R, :]
        a_i = pwi_ref[R - 1:R, :]
        cr = nxt_re[0:1, :]
        ci = nxt_im[0:1, :]
        for seg in range(7, -1, -1):
            gcr[seg:seg + 1, :] = cr
            gci[seg:seg + 1, :] = ci
            er = gre[seg:seg + 1, :]
            ei = gim[seg:seg + 1, :]
            cr, ci = er + a_r * cr + a_i * ci, ei + a_r * ci - a_i * cr
        nxt_re[0:1, :] = cr
        nxt_im[0:1, :] = ci
        for lb in range(NS // LBW):
            sl = slice(lb * LBW, (lb + 1) * LBW)
            kr = gcr[:, sl]
            ki = gci[:, sl]
            def fixed(rows, prow, sl=sl, kr=kr, ki=ki):
                pr = jnp.broadcast_to(pwr_ref[prow, sl], (8, LBW))
                pi = jnp.broadcast_to(pwi_ref[prow, sl], (8, LBW))
                gr = gre[rows, sl] + pr * kr + pi * ki
                gi = gim[rows, sl] + pr * ki - pi * kr
                gre[rows, sl] = gr
                gim[rows, sl] = gi
                return gr, gi
            g0r, g0i = fixed(slice(0, 8), slice(R - 1, R))
            p0r, p0i = cinr_ref[:, sl], cini_ref[:, sl]
            acc0 = (g0r * p0r + g0i * p0i, g0i * p0r - g0r * p0i)
            def dacc(r, carry, sl=sl, fixed=fixed):
                xr, xi = carry
                gr, gi = fixed(_rows8(r), pl.ds(R - 1 - r, 1))
                pr, pi = sre_ref[_rows8(r - 1), sl], sim_ref[_rows8(r - 1), sl]
                return xr + gr * pr + gi * pi, xi + gi * pr - gr * pi
            xr, xi = lax.fori_loop(1, R, dacc, acc0)
            dar_ref[:, sl] += xr
            dai_ref[:, sl] += xi
        dup = []
        for blk in range(4):
            s4 = slice(512 * blk, 512 * (blk + 1))
            s1 = slice(128 * blk, 128 * (blk + 1))
            grb = gre[:, s4].astype(bf16)
            gib = gim[:, s4].astype(bf16)
            dup.append(_dot_nt(grb, bre_ref[blk]) + _dot_nt(gib, bim_ref[blk]))
            dbre_ref[blk] += _dot_tn(u[:, s1], grb)
            dbim_ref[blk] += _dot_tn(u[:, s1], gib)
            dcre_ref[blk] += _dot_tn(dyb[:, s1], sre_ref[:, s4].astype(bf16))
            dcim_ref[blk] -= _dot_tn(dyb[:, s1], sim_ref[:, s4].astype(bf16))
        dp_ref[:, 0:SW] = (jnp.concatenate(dup, axis=1) + d_ref[...] * d_y0).astype(bf16)

    rev = lambda w, cidx: pl.BlockSpec((TC, w), lambda i, cidx=cidx: (nc - 1 - i, cidx))
    one = lambda shape: pl.BlockSpec(shape, lambda i: (0,) * len(shape))
    return pl.pallas_call(
        body, grid=(nc,),
        in_specs=[rev(SW, 0), rev(SW, 0), rev(SW, 6), rev(SW, 7), rev(NS, 0), rev(NS, 0),
                  pl.BlockSpec((8, NS), lambda i: (nc - 1 - i, 0)), pl.BlockSpec((8, NS), lambda i: (nc - 1 - i, 0)),
                  one((4, 128, 512)), one((4, 128, 512)), one((4, 512, 128)), one((4, 512, 128)),
                  one((1, NS)), one((1, NS)), one((R, NS)), one((R, NS)),
                  one((1, SW)), one((SW, SW)), one((1, SW)), _ANY],
        out_specs=[pl.BlockSpec((TC, 2 * SW), lambda i: (nc - 1 - i, 3)),
                   one((4, 128, 512)), one((4, 128, 512)), one((4, 128, 512)), one((4, 128, 512)),
                   one((1, SW)), one((8, NS)), one((8, NS)), one((SW, SW)), one((1, SW))],
        out_shape=[jax.ShapeDtypeStruct((L, IN_W), bf16),
                   jax.ShapeDtypeStruct((4, 128, 512), f32), jax.ShapeDtypeStruct((4, 128, 512), f32),
                   jax.ShapeDtypeStruct((4, 128, 512), f32), jax.ShapeDtypeStruct((4, 128, 512), f32),
                   jax.ShapeDtypeStruct((1, SW), f32), jax.ShapeDtypeStruct((8, NS), f32),
                   jax.ShapeDtypeStruct((8, NS), f32), jax.ShapeDtypeStruct((SW, SW), f32),
                   jax.ShapeDtypeStruct((1, SW), f32)],
        scratch_shapes=[pltpu.VMEM((TC, NS), f32), pltpu.VMEM((TC, NS), f32), pltpu.VMEM((8, NS), f32),
                        pltpu.VMEM((8, NS), f32), pltpu.VMEM((8, NS), f32), pltpu.VMEM((8, NS), f32)],
        input_output_aliases={19: 0},
        name="ssm_bwd", compiler_params=_cp("arbitrary"))(
            d_bin, y0, proj, proj, sre, sim, cinr, cini, bbt_re, bbt_im, ct_re, ct_im,
            a_re, a_im, apow_re, apow_im, dvec, wglu, bglu, dproj)


def _conv_bwd(d_ain, cu1, proj, cw, lng, lnb, dproj):
    L = cu1.shape[0]
    nc = L // TC
    def body(dain_ref, cu1_ref, ca_ref, cb_ref, zc_ref, cah_ref, cbh_ref, w_ref, g_ref, bb_ref, _,
             dp_ref, dw_ref, dbias_ref, dlng_ref, dlnb_ref, dbuf, ebuf, prev, nxt, dcu0):
        i = pl.program_id(0)
        @pl.when(i == 0)
        def _():
            dw_ref[...] = jnp.zeros_like(dw_ref)
            dbias_ref[...] = jnp.zeros_like(dbias_ref)
            dlng_ref[...] = jnp.zeros_like(dlng_ref)
            dlnb_ref[...] = jnp.zeros_like(dlnb_ref)
            nxt[...] = jnp.zeros_like(nxt)
        def lnb(s, carry):
            rows = pl.ds(pl.multiple_of(s * 32, 32), 32)
            dain = dain_ref[rows, :].astype(f32)
            c1 = cu1_ref[rows, :].astype(f32)
            zc = zc_ref[rows, :].astype(f32)
            xc = c1 - jnp.mean(c1, axis=-1, keepdims=True)
            var = jnp.mean(xc * xc, axis=-1, keepdims=True)
            rstd = lax.rsqrt(var + LN_EPS)
            xh = xc * rstd
            ln = xh * g_ref[...] + bb_ref[...]
            sl_ = _sig(ln)
            sz = _sig(zc)
            dp_ref[rows, 2 * CW:3 * CW] = (dain * (ln * sl_) * (sz * (1.0 + zc * (1.0 - sz)))).astype(bf16)
            d_ln = dain * (zc * sz) * (sl_ * (1.0 + ln * (1.0 - sl_)))
            dlng_ref[...] += jnp.sum(d_ln * xh, axis=0, keepdims=True)
            dlnb_ref[...] += jnp.sum(d_ln, axis=0, keepdims=True)
            dxh = d_ln * g_ref[...]
            d_c1 = rstd * (dxh - jnp.mean(dxh, axis=-1, keepdims=True)
                           - xh * jnp.mean(dxh * xh, axis=-1, keepdims=True))
            dbias_ref[...] += jnp.sum(d_c1, axis=0, keepdims=True)
            _put_blocked(dbuf, pl.multiple_of(s * 32, 32), 32, d_c1)
            _put_blocked(ebuf, pl.multiple_of(NH * 8 + s * 32, 32), 32,
                         ca_ref[rows, :].astype(f32) * _sig(cb_ref[rows, :].astype(f32)))
            return carry
        lax.fori_loop(0, TC // 32, lnb, 0, unroll=4)
        sub = lax.broadcasted_iota(jnp.int32, (8, 128), 0)
        def after(p, carry):
            for lb in range(NLB):
                cur = dbuf[lb, _rows8(p), :]
                dbuf[lb, _rows8(R + p), :] = jnp.where(sub == 7, pltpu.roll(nxt[lb, _rows8(p), :], 7, 0),
                                                       pltpu.roll(cur, 7, 0))
            return carry
        lax.fori_loop(0, NH, after, 0)
        nxt[...] = dbuf[:, 0:NH * 8, :]
        def before(s, carry):
            rows = pl.ds(pl.multiple_of(s * 64, 64), 64)
            v = cah_ref[rows, :].astype(f32) * _sig(cbh_ref[rows, :].astype(f32))
            _put_blocked(prev, pl.multiple_of(s * 64, 64), 64, jnp.where(i == nc - 1, jnp.zeros_like(v), v))
            return carry
        lax.fori_loop(0, NH * 8 // 64, before, 0)
        _fill_before(ebuf, prev)
        for lb in range(NLB):
            sl = slice(lb * 128, (lb + 1) * 128)
            wk = [jnp.broadcast_to(w_ref[k:k + 1, sl], (8, 128)) for k in range(KS)]
            def tap(q, carry, lb=lb, wk=wk):
                r = q * RPI
                for j, o in enumerate(_fir(dbuf, lb, r, wk, None, True)):
                    dcu0[lb, _rows8(r + j), :] = o
                return carry
            lax.fori_loop(0, R // RPI, tap, 0)
            def wgrad(q, accs, lb=lb):
                r = q * RPI
                dvs = dbuf[lb, pl.ds(pl.multiple_of(r * 8, 8), RPI * 8), :]
                win = ebuf[lb, pl.ds(pl.multiple_of((r + (NH - KS + 1)) * 8, 8), (KS + RPI - 1) * 8), :]
                accs = list(accs)
                for j in range(RPI):
                    dv = dvs[8 * j:8 * j + 8, :]
                    for k in range(KS):
                        accs[k] = accs[k] + dv * win[8 * (j + k):8 * (j + k) + 8, :]
                return tuple(accs)
            accs = lax.fori_loop(0, R // RPI, wgrad, tuple(jnp.zeros((8, 128), f32) for _ in range(KS)))
            for k in range(KS):
                dw_ref[k, :, sl] += accs[k]
        def glub(s, carry):
            rows = pl.ds(pl.multiple_of(s * 64, 64), 64)
            d0 = _get_blocked(dcu0, pl.multiple_of(s * 64, 64), 64)
            ca = ca_ref[rows, :].astype(f32)
            sb = _sig(cb_ref[rows, :].astype(f32))
            dp_ref[rows, 0:CW] = (d0 * sb).astype(bf16)
            dp_ref[rows, CW:2 * CW] = (d0 * ca * sb * (1.0 - sb)).astype(bf16)
            return carry
        lax.fori_loop(0, TC // 64, glub, 0)

    hrows = NH * 8
    per = TC // hrows
    rev = lambda cidx: pl.BlockSpec((TC, CW), lambda i, cidx=cidx: (nc - 1 - i, cidx))
    halo = lambda cidx: pl.BlockSpec((hrows, CW), lambda i, cidx=cidx: (jnp.maximum((nc - 1 - i) * per - 1, 0), cidx))
    one = lambda shape: pl.BlockSpec(shape, lambda i: (0,) * len(shape))
    return pl.pallas_call(
        body, grid=(nc,),
        in_specs=[rev(0), rev(0), rev(0), rev(1), rev(2), halo(0), halo(1), one((32, CW)), one((1, CW)), one((1, CW)),
                  _ANY],
        out_specs=[pl.BlockSpec((TC, 3 * CW), lambda i: (nc - 1 - i, 0)), one((32, 8, CW)), one((1, CW)), one((1, CW)), one((1, CW))],
        out_shape=[jax.ShapeDtypeStruct((L, IN_W), bf16), jax.ShapeDtypeStruct((32, 8, CW), f32),
                   jax.ShapeDtypeStruct((1, CW), f32), jax.ShapeDtypeStruct((1, CW), f32),
                   jax.ShapeDtypeStruct((1, CW), f32)],
        scratch_shapes=[pltpu.VMEM((NLB, (R + NH) * 8, 128), f32), pltpu.VMEM((NLB, (NH + R) * 8, 128), f32),
                        pltpu.VMEM((NLB, hrows, 128), f32), pltpu.VMEM((NLB, hrows, 128), f32),
                        pltpu.VMEM((NLB, TC, 128), f32)],
        input_output_aliases={10: 0},
        name="conv_bwd", compiler_params=_cp("arbitrary"))(d_ain, cu1, proj, proj, proj, proj, proj, cw, lng, lnb, dproj)


def _win_grad(h, dproj):
    L = h.shape[0]
    tm = min(2048, L)
    nt = L // tm
    def body(h_ref, d_ref, o_ref, acc):
        i = pl.program_id(1)
        @pl.when(i == 0)
        def _():
            acc[...] = jnp.zeros_like(acc)
        acc[...] += _dot_tn(h_ref[...], d_ref[...])
        @pl.when(i == nt - 1)
        def _():
            o_ref[0] = acc[...].astype(bf16)
    return pl.pallas_call(
        body, grid=(NCHIP, nt),
        in_specs=[pl.BlockSpec((tm, D), lambda j, i: (i, 0)), pl.BlockSpec((tm, SHARD_W), lambda j, i: (i, j))],
        out_specs=pl.BlockSpec((1, D, SHARD_W), lambda j, i: (j, 0, 0)),
        out_shape=jax.ShapeDtypeStruct((NCHIP, D, SHARD_W), bf16),
        scratch_shapes=[pltpu.VMEM((D, SHARD_W), f32)],
        name="win_grad", compiler_params=_cp("arbitrary", "arbitrary"))(h, dproj)


def _adamw_math(w, g, m, v):
    m2 = B1 * m + (1.0 - B1) * g
    v2 = B2 * v + (1.0 - B2) * (g * g)
    m_hat = m2 / (1.0 - B1 ** STEP)
    v_hat = v2 / (1.0 - B2 ** STEP)
    delta = -LR * (m_hat / (jnp.sqrt(v_hat) + EPS) + WD * w)
    return delta, m2, v2


def _adamw(name, w, g, m, v):
    rows, cols = w.shape
    tm = rows if rows <= 256 else (256 if rows % 256 == 0 else 128)
    assert rows % tm == 0
    def body(w_ref, g_ref, m_ref, v_ref, d_ref, m2_ref, v2_ref):
        d, m2, v2 = _adamw_math(w_ref[...], g_ref[...], m_ref[...], v_ref[...])
        d_ref[...] = d
        m2_ref[...] = m2
        v2_ref[...] = v2
    spec = pl.BlockSpec((tm, cols), lambda i: (i, 0))
    shp = jax.ShapeDtypeStruct((rows, cols), f32)
    return pl.pallas_call(
        body, grid=(rows // tm,), in_specs=[spec] * 4, out_specs=[spec] * 3, out_shape=[shp] * 3,
        name=name, compiler_params=_cp("arbitrary"))(w, g, m, v)


def _adamw_group(name, ws, gs, ms, vs):
    n = len(ws)
    def body(*refs):
        for i in range(n):
            w_ref, g_ref, m_ref, v_ref = (refs[q * n + i] for q in range(4))
            d, m2, v2 = _adamw_math(w_ref[...], g_ref[...], m_ref[...], v_ref[...])
            for q, val in enumerate((d, m2, v2)):
                refs[(4 + q) * n + i][...] = val
    shapes = [jax.ShapeDtypeStruct(w.shape, f32) for w in ws]
    out = pl.pallas_call(body, out_shape=shapes * 3, name=name,
                         compiler_params=pltpu.CompilerParams(vmem_limit_bytes=VMEM_LIMIT))(*ws, *gs, *ms, *vs)
    return [(out[i], out[n + i], out[2 * n + i]) for i in range(n)]


_ANY = pl.BlockSpec(memory_space=pl.ANY)


def _chunks(rows, parts):
    step = rows // parts
    assert step * parts == rows and step % 16 == 0
    return [(i * step, step) for i in range(parts)]


def _place():
    x, y, c = lax.axis_index("x"), lax.axis_index("y"), lax.axis_index("c")
    chips = [(1 - x, y), (x, 1 - y), (1 - x, 1 - y)]
    return x, y, c, chips


def _nchunks(half, cols, itemsize):
    return 4 if half * cols * itemsize >= (1 << 20) else 1


def _segments(metas):
    segs = []
    for w, (half, cols, dt) in enumerate(metas):
        for r0, n in _chunks(half, _nchunks(half, cols, jnp.dtype(dt).itemsize)):
            segs.append((w, half, r0, n))
    return segs


def _rcopy(i, src, dst, send_sems, recv_sems, to):
    return pltpu.make_async_remote_copy(src_ref=src, dst_ref=dst, send_sem=send_sems.at[i], recv_sem=recv_sems.at[i],
                                        device_id=to, device_id_type=MESH)


def _gather_prep(k_arr, shards, x, tgt, g_pre, perm):
    na = len(shards)
    L = x.shape[0]
    nc = L // TC
    segs = _segments([(a.shape[0] // 2, a.shape[1], a.dtype) for a in shards])
    ns = len(segs)
    def body(_, *refs):
        ins = refs[:na]
        x_ref, t_ref, g_ref, p_ref = refs[na:na + 4]
        outs = refs[na + 4:2 * na + 4]
        h_ref, xi_ref, ti_ref, proj_ref = refs[2 * na + 4:2 * na + 8]
        stages = refs[2 * na + 8:3 * na + 8]
        send_sems, recv_sems, local_sems = refs[3 * na + 8:]
        i = pl.program_id(0)
        x, y, c, chips = _place()
        k = 2 * x + y
        me, sibling = (x, y, c), (x, y, 1 - c)

        def dst(w, half, chip, pc, r0, n):
            return outs[w].at[chip, pl.ds(pc * half + r0, n), :]

        def firsts():
            return [_rcopy(j * ns + s, ins[w].at[pl.ds(c * half + r0, n), :], dst(w, half, k, c, r0, n),
                           send_sems, recv_sems, (*chip, c))
                    for j, chip in enumerate(chips) for s, (w, half, r0, n) in enumerate(segs)]

        def own_out(w):
            return pltpu.make_async_copy(stages[w], outs[w].at[k], local_sems.at[w])

        @pl.when(i == 0)
        def _():
            for cp in firsts():
                cp.start()
            cins = [pltpu.make_async_copy(ins[w], stages[w], local_sems.at[w]) for w in range(na)]
            for cp in cins:
                cp.start()
            for w in range(na):
                cins[w].wait()
                own_out(w).start()

        p = p_ref[...]
        def through(v):
            hi = v.astype(bf16)
            r1 = v - hi.astype(f32)
            mid = r1.astype(bf16)
            lo = (r1 - mid.astype(f32)).astype(bf16)
            return (_dot(p, hi) + _dot(p, mid)) + _dot(p, lo)
        xt = x_ref[...]
        r = lax.rsqrt(jnp.mean(xt * xt, axis=-1, keepdims=True) + RMS_EPS)
        hp = _dot(p, (xt * r * g_ref[...]).astype(bf16)).astype(bf16)
        h_ref[...] = hp
        proj_ref[...] = _dot(hp, stages[0][...]).astype(bf16)
        xi_ref[...] = through(xt)
        ti_ref[...] = through(t_ref[...])

        def forwards():
            out = []
            for j, chip in enumerate(chips):
                cj = 2 * chip[0] + chip[1]
                for s, (w, half, r0, n) in enumerate(segs):
                    landed = dst(w, half, cj, c, r0, n)
                    out.append(_rcopy(3 * ns + j * ns + s, landed, landed, send_sems, recv_sems, sibling))
            return out

        @pl.when(i == max(nc - 2, 0))
        def _():
            for cp, fwd in zip(firsts(), forwards()):
                cp.wait_recv()
                fwd.start()

        @pl.when(i == nc - 1)
        def _():
            passed = forwards()
            for j, chip in enumerate(chips):
                cj = 2 * chip[0] + chip[1]
                for s, (w, half, r0, n) in enumerate(segs):
                    theirs = dst(w, half, cj, 1 - c, r0, n)
                    _rcopy(3 * ns + j * ns + s, theirs, theirs, send_sems, recv_sems, me).wait_recv()
            for cp in firsts() + passed:
                cp.wait_send()
            for w in range(na):
                own_out(w).wait()

    row = lambda: pl.BlockSpec((TC, D), lambda i, k: (i, 0))
    grid_spec = pltpu.PrefetchScalarGridSpec(
        num_scalar_prefetch=1, grid=(nc,),
        in_specs=[_ANY] * na + [row(), row(), pl.BlockSpec((1, D), lambda i, k: (0, 0)),
                                pl.BlockSpec((TC, TC), lambda i, k: (0, 0))],
        out_specs=[_ANY] * na + [row(), row(), row(), pl.BlockSpec((TC, SHARD_W), lambda i, k: (i, k[0]))],
        scratch_shapes=[pltpu.VMEM(a.shape, a.dtype) for a in shards]
        + [pltpu.SemaphoreType.DMA((6 * ns,)), pltpu.SemaphoreType.DMA((6 * ns,)), pltpu.SemaphoreType.DMA((na,))])
    return pl.pallas_call(
        body, grid_spec=grid_spec,
        out_shape=[jax.ShapeDtypeStruct((NCHIP,) + a.shape, a.dtype) for a in shards]
        + [jax.ShapeDtypeStruct((L, D), bf16), jax.ShapeDtypeStruct((L, D), f32), jax.ShapeDtypeStruct((L, D), f32),
           jax.ShapeDtypeStruct((L, IN_W), bf16)],
        name="gather_prep", compiler_params=_cp("arbitrary"))(k_arr, *shards, x, tgt, g_pre, perm)


def _x_grad_exchange(dproj, w_in, x, gx0, g_pre, parts, small):
    L = x.shape[0]
    tm = 512
    nt = L // tm
    na = len(parts)
    hs = SMALL_ROWS // 2
    segs = _segments([(p.shape[1], p.shape[2], p.dtype) for p in parts])
    ns = len(segs) + 1
    def body(*refs):
        d_ref, w_ref, x_ref, gx_ref, g_ref = refs[:5]
        ins, s_ref = refs[5:5 + na], refs[5 + na]
        o_ref, dg_ref = refs[6 + na:8 + na]
        outs, qs_ref = refs[8 + na:8 + 2 * na], refs[8 + 2 * na]
        stages = refs[9 + 2 * na:10 + 3 * na]
        send_sems, recv_sems, local_sems = refs[10 + 3 * na:]
        i = pl.program_id(0)
        x, y, c, chips = _place()
        k = 2 * x + y

        def my_small():
            return s_ref.at[pl.ds(c * hs, hs), :]

        def copies():
            out = []
            for j, chip in enumerate(chips):
                cj = 2 * chip[0] + chip[1]
                pieces = [(my_small(), qs_ref.at[k])]
                pieces += [(ins[w].at[cj, pl.ds(r0, n), :], outs[w].at[k, pl.ds(r0, n), :]) for w, _, r0, n in segs]
                out += [_rcopy(ns * j + s, src, d, send_sems, recv_sems, (*chip, c)) for s, (src, d) in enumerate(pieces)]
            return out

        def own_out(w):
            dst = qs_ref.at[k] if w == na else outs[w].at[k]
            return pltpu.make_async_copy(stages[w], dst, local_sems.at[w])

        @pl.when(i == 0)
        def _():
            dg_ref[...] = jnp.zeros_like(dg_ref)
            for cp in copies():
                cp.start()
            cins = [pltpu.make_async_copy(my_small() if w == na else ins[w].at[k], stages[w], local_sems.at[w])
                    for w in range(na + 1)]
            for cp in cins:
                cp.start()
            for w in range(na + 1):
                cins[w].wait()
                own_out(w).start()

        dh = _dot_nt(d_ref[:, 0:SHARD_W], w_ref[0])
        for j in range(1, NCHIP):
            dh = dh + _dot_nt(d_ref[:, j * SHARD_W:(j + 1) * SHARD_W], w_ref[j])
        xt = x_ref[...]
        r = lax.rsqrt(jnp.mean(xt * xt, axis=-1, keepdims=True) + RMS_EPS)
        xn = xt * r
        dg_ref[...] += jnp.sum(dh * xn, axis=0, keepdims=True)
        dxn = dh * g_ref[...]
        o_ref[...] = gx_ref[...] + r * (dxn - xn * jnp.mean(dxn * xn, axis=-1, keepdims=True))

        @pl.when(i == nt - 1)
        def _():
            for cp in copies():
                cp.wait_recv()
            for cp in copies():
                cp.wait_send()
            for w in range(na + 1):
                own_out(w).wait()

    return pl.pallas_call(
        body, grid=(nt,),
        in_specs=[pl.BlockSpec((tm, IN_W), lambda i: (i, 0)),
                  pl.BlockSpec((NCHIP, D, SHARD_W), lambda i: (0, 0, 0), pipeline_mode=pl.Buffered(1)),
                  pl.BlockSpec((tm, D), lambda i: (i, 0)), pl.BlockSpec((tm, D), lambda i: (i, 0)), _full((1, D))]
        + [_ANY] * (na + 1),
        out_specs=[pl.BlockSpec((tm, D), lambda i: (i, 0)), _full((1, D))] + [_ANY] * (na + 1),
        out_shape=[jax.ShapeDtypeStruct((L, D), f32), jax.ShapeDtypeStruct((1, D), f32)]
        + [jax.ShapeDtypeStruct(p.shape, bf16) for p in parts] + [jax.ShapeDtypeStruct((NCHIP, hs, 128), f32)],
        scratch_shapes=[pltpu.VMEM(p.shape[1:], bf16) for p in parts] + [pltpu.VMEM((hs, 128), f32)]
        + [pltpu.SemaphoreType.DMA((3 * ns,)), pltpu.SemaphoreType.DMA((3 * ns,)), pltpu.SemaphoreType.DMA((na + 1,))],
        name="x_grad_exchange", compiler_params=_cp("arbitrary"))(dproj, w_in, x, gx0, g_pre, *parts, small)


def _sibling_join_list(halves):
    na = len(halves)
    segs = _segments([(h.shape[0], h.shape[1], h.dtype) for h in halves])
    def body(*refs):
        ins, outs, stages = refs[:na], refs[na:2 * na], refs[2 * na:3 * na]
        send_sems, recv_sems, local_sems = refs[3 * na:]
        x, y, c, _ = _place()
        copies = [_rcopy(i, ins[w].at[pl.ds(r0, n), :], outs[w].at[pl.ds(c * half + r0, n), :], send_sems, recv_sems,
                         (x, y, 1 - c)) for i, (w, half, r0, n) in enumerate(segs)]
        for cp in copies:
            cp.start()
        cins = [pltpu.make_async_copy(ins[w], stages[w], local_sems.at[w]) for w in range(na)]
        for cp in cins:
            cp.start()
        own = []
        for w in range(na):
            cins[w].wait()
            half = halves[w].shape[0]
            own.append(pltpu.make_async_copy(stages[w], outs[w].at[pl.ds(c * half, half), :], local_sems.at[w]))
            own[-1].start()
        for cp in copies:
            cp.wait_recv()
        for cp in copies:
            cp.wait_send()
        for cp in own:
            cp.wait()

    return pl.pallas_call(
        body, in_specs=[_ANY] * na, out_specs=[_ANY] * na,
        out_shape=[jax.ShapeDtypeStruct((2 * h.shape[0], h.shape[1]), f32) for h in halves],
        scratch_shapes=[pltpu.VMEM(h.shape, f32) for h in halves]
        + [pltpu.SemaphoreType.DMA((len(segs),)), pltpu.SemaphoreType.DMA((len(segs),)), pltpu.SemaphoreType.DMA((na,))],
        name="sibling_join")(*halves)


def _small_join(v, fs_half):
    hs = fs_half.shape[0]
    def body(v_ref, h_ref, o_ref, fs_ref, send_sems, recv_sems):
        x, y, c, _ = _place()
        me = 4 * x + 2 * y + c
        o_ref[me] = v_ref[...]
        mine = pl.ds(pl.multiple_of(c * hs, 8), hs)
        fs_ref[mine, :] = h_ref[...]
        copies = [_rcopy(7, h_ref, fs_ref.at[mine, :], send_sems, recv_sems, (x, y, 1 - c))]
        i = 0
        for dx in range(2):
            for dy in range(2):
                for dc in range(2):
                    if dx + dy + dc:
                        copies.append(_rcopy(i, v_ref, o_ref.at[me], send_sems, recv_sems, (x ^ dx, y ^ dy, c ^ dc)))
                        i += 1
        for cp in copies:
            cp.start()
        for cp in copies:
            cp.wait_recv()
        for cp in copies:
            cp.wait_send()

    vm = pl.BlockSpec(memory_space=pltpu.VMEM)
    return pl.pallas_call(
        body, in_specs=[vm, vm], out_specs=[vm, vm],
        out_shape=[jax.ShapeDtypeStruct((8, 8, 128), f32), jax.ShapeDtypeStruct((2 * hs, 128), f32)],
        scratch_shapes=[pltpu.SemaphoreType.DMA((8,)), pltpu.SemaphoreType.DMA((8,))],
        name="small_join")(v, fs_half)


def _adamw_rows(parts, w, m, v):
    def body(p_ref, w_ref, m_ref, v_ref, g_ref, d_ref, m2_ref, v2_ref):
        g = p_ref[0]
        for dvc in range(1, 8):
            g = g + p_ref[dvc]
        g_ref[...] = g
        d, m2, v2 = _adamw_math(w_ref[...], g, m_ref[...], v_ref[...])
        d_ref[...] = d
        m2_ref[...] = m2
        v2_ref[...] = v2
    return pl.pallas_call(body, out_shape=[jax.ShapeDtypeStruct((8, 128), f32)] * 4, name="adamw_pre_norm_gain")(
        parts, w, m, v)


def _pair_exchange_list(grads, small):
    na = len(grads)
    segs = _segments([(g.shape[1] // 2, g.shape[2], g.dtype) for g in grads])
    n = NCHIP * len(segs) + 1
    def body(*refs):
        ins, s_ref, outs, rs_ref, (send_sems, recv_sems) = (refs[:na], refs[na], refs[na + 1:2 * na + 1],
                                                            refs[2 * na + 1], refs[2 * na + 2:])
        x, y, c, _ = _place()
        pieces = [(s_ref, rs_ref)]
        for j in range(NCHIP):
            for w, half, r0, rows in segs:
                pieces.append((ins[w].at[j, pl.ds((1 - c) * half + r0, rows), :], outs[w].at[j, pl.ds(r0, rows), :]))
        copies = [_rcopy(i, s, d, send_sems, recv_sems, (x, y, 1 - c)) for i, (s, d) in enumerate(pieces)]
        for cp in copies:
            cp.start()
        for cp in copies:
            cp.wait_recv()
        for cp in copies:
            cp.wait_send()

    return pl.pallas_call(
        body, in_specs=[_ANY] * (na + 1), out_specs=[_ANY] * (na + 1),
        out_shape=[jax.ShapeDtypeStruct((NCHIP, g.shape[1] // 2, g.shape[2]), g.dtype) for g in grads]
        + [jax.ShapeDtypeStruct((SMALL_ROWS, 128), f32)],
        scratch_shapes=[pltpu.SemaphoreType.DMA((n,)), pltpu.SemaphoreType.DMA((n,))],
        name="pair_exchange")(*grads, small)


def _pair_sum_list(c_arr, grads, recvs, small, rsmall):
    na = len(grads)
    def body(c_ref, *refs):
        g_refs, r_refs, s_ref, rs_ref = refs[:na], refs[na:2 * na], refs[2 * na], refs[2 * na + 1]
        o_refs, os_ref = refs[2 * na + 2:3 * na + 2], refs[3 * na + 2]
        for g_ref, r_ref, o_ref in zip(g_refs, r_refs, o_refs):
            o_ref[...] = (g_ref[...].astype(f32) + r_ref[...].astype(f32)).astype(bf16)
        os_ref[...] = s_ref[...] + rs_ref[...]
    half = lambda g: pl.BlockSpec((1, g.shape[1] // 2, g.shape[2]), lambda j, c: (j, c[0], 0))
    low = lambda g: pl.BlockSpec((1, g.shape[1] // 2, g.shape[2]), lambda j, c: (j, 0, 0))
    sm = pl.BlockSpec((SMALL_ROWS, 128), lambda j, c: (0, 0))
    grid_spec = pltpu.PrefetchScalarGridSpec(
        num_scalar_prefetch=1, grid=(NCHIP,),
        in_specs=[half(g) for g in grads] + [low(g) for g in grads] + [sm, sm],
        out_specs=[low(g) for g in grads] + [sm])
    return pl.pallas_call(
        body, grid_spec=grid_spec,
        out_shape=[jax.ShapeDtypeStruct((NCHIP, g.shape[1] // 2, g.shape[2]), bf16) for g in grads]
        + [jax.ShapeDtypeStruct((SMALL_ROWS, 128), f32)],
        name="pair_sum", compiler_params=_cp("arbitrary"))(c_arr, *grads, *recvs, small, rsmall)


def _chip_sum_list(parts, small):
    na = len(parts)
    nt = 2
    def body(*refs):
        for q_ref, f_ref in zip(refs[:na + 1], refs[na + 1:]):
            acc = q_ref[0].astype(f32)
            for j in range(1, NCHIP):
                acc = acc + q_ref[j].astype(f32)
            f_ref[...] = acc
    arrs = list(parts) + [small]
    return pl.pallas_call(
        body, grid=(nt,),
        in_specs=[pl.BlockSpec((NCHIP, a.shape[1] // nt, a.shape[2]), lambda i: (0, i, 0)) for a in arrs],
        out_specs=[pl.BlockSpec((a.shape[1] // nt, a.shape[2]), lambda i: (i, 0)) for a in arrs],
        out_shape=[jax.ShapeDtypeStruct(a.shape[1:], f32) for a in arrs],
        name="chip_sum", compiler_params=_cp("arbitrary"))(*arrs)


_SMALL =(("conv_b", (1, 1024)), ("conv_ln_gain", (1, 1024)), ("conv_ln_bias", (1, 1024)),
          ("ssm_lambda_re", (1, 32, 64)), ("ssm_lambda_im", (1, 32, 64)), ("ssm_log_dt", (1, 32)),
          ("ssm_b_re", (1, 32, 64, 16)), ("ssm_b_im", (1, 32, 64, 16)), ("ssm_c_re", (1, 32, 16, 64)),
          ("ssm_c_im", (1, 32, 16, 64)), ("ssm_d", (1, 32, 16)), ("b_ssm_glu", (1, 512)), ("post_norm_gain", (1, 1024)))


def _pack_small(vals, extra=None):
    rows = []
    for v in list(vals) + ([extra] if extra is not None else []):
        flat = v.reshape(-1).astype(f32)
        n = -(-flat.shape[0] // 1024) * 1024
        rows.append(jnp.pad(flat, (0, n - flat.shape[0])).reshape(-1, 128))
    used = sum(r.shape[0] for r in rows)
    rows.append(jnp.zeros((SMALL_ROWS - used, 128), f32))
    return jnp.concatenate(rows, axis=0)


def _unpack_small(p):
    o = 0
    out = []
    for _, shape in _SMALL:
        n = int(np.prod(shape))
        nr = -(-n // 1024) * 8
        out.append(p[o:o + nr].reshape(-1)[:n].reshape(shape))
        o += nr
    return out, p[o, 0]


def _discretize(lam_re, lam_im, log_dt, b_re, b_im):
    dt = jnp.exp(log_dt)[:, None]
    mag = jnp.exp(lam_re * dt)
    ar = mag * jnp.cos(lam_im * dt)
    ai = mag * jnp.sin(lam_im * dt)
    den = lam_re * lam_re + lam_im * lam_im
    zr = ((ar - 1.0) * lam_re + ai * lam_im) / den
    zi = (ai * lam_re - (ar - 1.0) * lam_im) / den
    bbr = zr[..., None] * b_re - zi[..., None] * b_im
    bbi = zr[..., None] * b_im + zi[..., None] * b_re
    return ar, ai, bbr, bbi


_EYE8 = np.eye(8, dtype=np.float32)


def _bbt_blocks(bb):
    v = bb.reshape(4, 8, PST, H).transpose(0, 1, 3, 2)
    return jnp.einsum("bghp,gk->bghkp", v, _EYE8).reshape(4, 128, 512)


def _bbt_unblock(m):
    v = jnp.einsum("bghkp,gk->bghp", m.reshape(4, 8, H, 8, PST), _EYE8)
    return v.transpose(0, 1, 3, 2).reshape(G, PST, H)


def _ct_blocks(cc):
    v = cc.reshape(4, 8, H, PST)
    return jnp.einsum("bghp,gk->bgpkh", v, _EYE8).reshape(4, 512, 128)


def _ct_unblock(m):
    return jnp.einsum("bghkp,gk->bghp", m.reshape(4, 8, H, 8, PST), _EYE8).reshape(G, H, PST)


def _perm_matrix():
    p = np.zeros((TC, TC), np.float32)
    for r in range(R):
        for seg in range(8):
            p[r * 8 + seg, seg * R + r] = 1.0
    return p


def _deinterleave(a):
    L, C = a.shape
    return a.reshape(L // TC, R, 8, C).transpose(0, 2, 1, 3).reshape(L, C)


def _fwd_bwd(h, xi, ti, proj, conv_w, w_co, w_glu, w_so, w_out, small):
    (conv_b, ln_g, ln_b, lam_re, lam_im, log_dt, b_re, b_im, c_re, c_im, dvec, b_glu, g_post) = small
    lam_re, lam_im, log_dt = lam_re[0], lam_im[0], log_dt[0]
    b_re, b_im, c_re, c_im = b_re[0], b_im[0], c_re[0], c_im[0]
    (ar, ai, bbr, bbi), disc_vjp = jax.vjp(_discretize, lam_re, lam_im, log_dt, b_re, b_im)
    a_re = ar.reshape(1, NS)
    a_im = ai.reshape(1, NS)
    dt = jnp.exp(log_dt)[:, None]
    steps = jnp.arange(1, R + 1, dtype=f32)[:, None, None]
    apow_re = (jnp.exp(steps * (lam_re * dt)) * jnp.cos(steps * (lam_im * dt))).reshape(R, NS)
    apow_im = (jnp.exp(steps * (lam_re * dt)) * jnp.sin(steps * (lam_im * dt))).reshape(R, NS)
    bbt_re, bbt_im = _bbt_blocks(bbr).astype(bf16), _bbt_blocks(bbi).astype(bf16)
    ct_re, ct_im = _ct_blocks(c_re).astype(bf16), _ct_blocks(c_im).astype(bf16)
    d_row = dvec.reshape(1, SW)
    cw32 = jnp.pad(conv_w, ((0, 1), (0, 0)))

    cu1, a_in = _conv_fwd(proj, cw32, conv_b, ln_g, ln_b)
    y0, b_in, sre, sim, cinr, cini = _ssm_fwd(proj, bbt_re, bbt_im, ct_re, ct_im, a_re, a_im,
                                              apow_re, apow_im, d_row, w_glu, b_glu)
    gx0, d_ain, d_bin, dproj, dw_out, dw_co, dw_so, dg_post, loss = _tail(
        a_in, b_in, proj, xi, ti, w_co, w_so, w_out, g_post)
    (dproj, dbbt_re, dbbt_im, dct_re, dct_im, dd, dar8, dai8, dw_glu, db_glu) = _ssm_bwd(
        d_bin, y0, proj, sre, sim, cinr, cini, bbt_re, bbt_im, ct_re, ct_im,
        a_re, a_im, apow_re, apow_im, d_row, w_glu, b_glu, dproj)
    dproj, dcw8, d_convb, d_lng, d_lnb = _conv_bwd(d_ain, cu1, proj, cw32, ln_g, ln_b, dproj)
    dw_in = _win_grad(h, dproj)

    d_ar = jnp.sum(dar8, axis=0).reshape(G, PST)
    d_ai = jnp.sum(dai8, axis=0).reshape(G, PST)
    d_lre, d_lim, d_ldt, d_bre, d_bim = disc_vjp((d_ar, d_ai, _bbt_unblock(dbbt_re), _bbt_unblock(dbbt_im)))
    d_conv_w = jnp.sum(dcw8, axis=1)[:KS]
    small_grads = [d_convb, d_lng, d_lnb, d_lre[None], d_lim[None], d_ldt[None], d_bre[None], d_bim[None],
                   _ct_unblock(dct_re)[None], _ct_unblock(dct_im)[None], dd.reshape(1, G, H), db_glu, dg_post]
    return loss[0, 0], gx0, dproj, (dw_in, dw_co, dw_out, dw_glu, dw_so, d_conv_w), small_grads


def kernel(x, pre_norm_gain, w_in, conv_w, conv_b, conv_ln_gain, conv_ln_bias, w_conv_out, ssm_lambda_re, ssm_lambda_im, ssm_log_dt, ssm_b_re, ssm_b_im, ssm_c_re, ssm_c_im, ssm_d, w_ssm_glu, b_ssm_glu, w_ssm_out, w_out, post_norm_gain, loss_target, m_pre_norm_gain, m_w_in, m_conv_w, m_conv_b, m_conv_ln_gain, m_conv_ln_bias, m_w_conv_out, m_ssm_lambda_re, m_ssm_lambda_im, m_ssm_log_dt, m_ssm_b_re, m_ssm_b_im, m_ssm_c_re, m_ssm_c_im, m_ssm_d, m_w_ssm_glu, m_b_ssm_glu, m_w_ssm_out, m_w_out, m_post_norm_gain, v_pre_norm_gain, v_w_in, v_conv_w, v_conv_b, v_conv_ln_gain, v_conv_ln_bias, v_w_conv_out, v_ssm_lambda_re, v_ssm_lambda_im, v_ssm_log_dt, v_ssm_b_re, v_ssm_b_im, v_ssm_c_re, v_ssm_c_im, v_ssm_d, v_w_ssm_glu, v_b_ssm_glu, v_w_ssm_out, v_w_out, v_post_norm_gain):
    c = lax.axis_index("c")
    shards = [w_in[0].astype(bf16), w_conv_out[0].astype(bf16), w_out[0].astype(bf16), w_ssm_glu[0].astype(bf16),
              w_ssm_out[0].astype(bf16), jnp.pad(conv_w[0], ((0, CONV_ROWS - KS), (0, 0)))]
    k_arr = (2 * lax.axis_index("x") + lax.axis_index("y")).astype(jnp.int32).reshape(1)
    w_in_g, w_co_g, w_out_g, w_glu_g, w_so_g, conv_w_g, h, xi, ti, proj = _gather_prep(
        k_arr, shards, x[0], loss_target[0], pre_norm_gain, jnp.asarray(_perm_matrix(), bf16))
    conv_w_f = conv_w_g[:, :KS].transpose(1, 0, 2).reshape(KS, CW)

    small = (conv_b, conv_ln_gain, conv_ln_bias, ssm_lambda_re, ssm_lambda_im, ssm_log_dt, ssm_b_re,
             ssm_b_im, ssm_c_re, ssm_c_im, ssm_d, b_ssm_glu, post_norm_gain)
    loss_part, gx0, dproj, big_grads, small_grads = _fwd_bwd(
        h, xi, ti, _proj_fwd(k_arr, h, w_in_g, proj), conv_w_f, w_co_g.reshape(CW, D), w_glu_g.reshape(SW, SW), w_so_g,
        w_out_g.reshape(D, D), small)

    dw_in, dw_co, dw_out, dw_glu, dw_so, d_conv_w = big_grads
    d_conv_w = jnp.pad(d_conv_w, ((0, CONV_ROWS - KS), (0, 0))).reshape(CONV_ROWS, NCHIP, 256).transpose(1, 0, 2)
    grads = [dw_in] + [g.astype(bf16) for g in (dw_co.reshape(NCHIP, 256, D), dw_out.reshape(NCHIP, 256, D),
                                                  dw_glu.reshape(NCHIP, 128, SW), dw_so, d_conv_w)]
    gs = _pack_small(small_grads, extra=loss_part)
    *recvs, rs = _pair_exchange_list(grads, gs)
    *parts, ps = _pair_sum_list(c.astype(jnp.int32).reshape(1), grads, recvs, gs, rs)
    gxi, dg_pre, *qparts, qs = _x_grad_exchange(dproj, w_in_g, xi, gx0, pre_norm_gain, parts, ps)
    grad_x = _deinterleave(gxi)
    *halves, fs_half = _chip_sum_list(qparts, qs)
    pre_parts, fs = _small_join(dg_pre.reshape(8, 128), fs_half)
    g_big = list(_sibling_join_list(halves))
    g_big[5] = g_big[5][:KS]

    big_w = (w_in[0], w_conv_out[0], w_out[0], w_ssm_glu[0], w_ssm_out[0], conv_w[0])
    big_m = (m_w_in[0], m_w_conv_out[0], m_w_out[0], m_w_ssm_glu[0], m_w_ssm_out[0], m_conv_w[0])
    big_v = (v_w_in[0], v_w_conv_out[0], v_w_out[0], v_w_ssm_glu[0], v_w_ssm_out[0], v_conv_w[0])
    big_names = ("w_in", "w_conv_out", "w_out", "w_ssm_glu", "w_ssm_out", "conv_w")
    res = {}
    upd = [_adamw("adamw_w_in", big_w[0], g_big[0], big_m[0], big_v[0])]
    upd += _adamw_group("adamw_rest", big_w[1:], g_big[1:], big_m[1:], big_v[1:])
    for n, g, (d, m2, v2) in zip(big_names, g_big, upd):
        res[n] = (g[None], d[None], m2[None], v2[None])

    small_m = (m_conv_b, m_conv_ln_gain, m_conv_ln_bias, m_ssm_lambda_re, m_ssm_lambda_im, m_ssm_log_dt,
               m_ssm_b_re, m_ssm_b_im, m_ssm_c_re, m_ssm_c_im, m_ssm_d, m_b_ssm_glu, m_post_norm_gain)
    small_v = (v_conv_b, v_conv_ln_gain, v_conv_ln_bias, v_ssm_lambda_re, v_ssm_lambda_im, v_ssm_log_dt,
               v_ssm_b_re, v_ssm_b_im, v_ssm_c_re, v_ssm_c_im, v_ssm_d, v_b_ssm_glu, v_post_norm_gain)
    sd, sm, sv = _adamw("adamw_small", _pack_small(small), fs, _pack_small(small_m), _pack_small(small_v))
    sg_l, loss = _unpack_small(fs)
    sd_l, _ = _unpack_small(sd)
    sm_l, _ = _unpack_small(sm)
    sv_l, _ = _unpack_small(sv)
    for i, (n, _) in enumerate(_SMALL):
        res[n] = (sg_l[i], sd_l[i], sm_l[i], sv_l[i])
    rows = lambda a: a.reshape(8, 128)
    pre = _adamw_rows(pre_parts, rows(pre_norm_gain), rows(m_pre_norm_gain), rows(v_pre_norm_gain))
    res["pre_norm_gain"] = tuple(a.reshape(1, D) for a in pre)

    order = ("pre_norm_gain", "w_in", "conv_w", "conv_b", "conv_ln_gain", "conv_ln_bias", "w_conv_out", "ssm_lambda_re",
             "ssm_lambda_im", "ssm_log_dt", "ssm_b_re", "ssm_b_im", "ssm_c_re", "ssm_c_im", "ssm_d", "w_ssm_glu",
             "b_ssm_glu", "w_ssm_out", "w_out", "post_norm_gain")
    outs = [loss, grad_x[None]]
    for q in range(4):
        outs.extend(res[n][q] for n in order)
    return tuple(outs)
```

```python
import math

import numpy as np
import jax
import jax.numpy as jnp
from jax import lax
from jax.experimental import pallas as pl
from jax.experimental.pallas import tpu as pltpu

f32 = jnp.float32
bf16 = jnp.bfloat16

D = 1024
CW = 1024
SW = 512
G = 32
H = 16
PST = 64
NS = G * PST
KS = 31
IN_W = 6144
NCHIP = 4
SHARD_W = IN_W // NCHIP
RMS_EPS = 1e-6
LN_EPS = 1e-5
LR, B1, B2, EPS, WD, STEP = 0.001, 0.9, 0.999, 1e-08, 0.01, 10
GELU_K0 = math.sqrt(2.0 / math.pi)
GELU_K1 = 0.044715

TC = 512
R = TC // 8
NH = 32
LBW = 1024
CONV_ROWS = 64
SMALL_ROWS = 1152
VMEM_LIMIT = 56 * 1024 * 1024
MESH = pl.DeviceIdType.MESH


def _cp(*sem):
    return pltpu.CompilerParams(dimension_semantics=tuple(sem), vmem_limit_bytes=VMEM_LIMIT)


def _sig(v):
    return 0.5 * jnp.tanh(0.5 * v) + 0.5


def _dot(a, b):
    return jnp.dot(a, b, preferred_element_type=f32)


def _dot_nt(a, b):
    return lax.dot_general(a, b, (((1,), (1,)), ((), ())), preferred_element_type=f32)


def _dot_tn(a, b):
    return lax.dot_general(a, b, (((0,), (0,)), ((), ())), preferred_element_type=f32)


def _full(shape):
    nd = len(shape)
    return pl.BlockSpec(shape, lambda *_: (0,) * nd)


def _rows8(i):
    return pl.ds(pl.multiple_of(i * 8, 8), 8)


def _proj_fwd(k_arr, h, w_in, proj):
    L = h.shape[0]
    tm = min(1024, L)
    def body(_, h_ref, w_ref, __, o_ref):
        o_ref[...] = _dot(h_ref[...], w_ref[0]).astype(bf16)
    shard = lambda j, k: (k[0] + 1 + j) % NCHIP
    grid_spec = pltpu.PrefetchScalarGridSpec(
        num_scalar_prefetch=1, grid=(NCHIP - 1, L // tm),
        in_specs=[pl.BlockSpec((tm, D), lambda j, i, k: (i, 0)),
                  pl.BlockSpec((1, D, SHARD_W), lambda j, i, k: (shard(j, k), 0, 0)), _ANY],
        out_specs=pl.BlockSpec((tm, SHARD_W), lambda j, i, k: (i, shard(j, k))))
    return pl.pallas_call(
        body, grid_spec=grid_spec, out_shape=jax.ShapeDtypeStruct((L, IN_W), bf16),
        input_output_aliases={3: 0},
        name="proj_fwd", compiler_params=_cp("arbitrary", "arbitrary"))(k_arr, h, w_in, proj)


NLB = CW // 128
RPI = 32


def _put_blocked(buf, row0, nrows, v):
    for lb in range(NLB):
        buf[lb, pl.ds(row0, nrows), :] = v[:, lb * 128:(lb + 1) * 128]


def _get_blocked(buf, row0, nrows):
    return jnp.concatenate([buf[lb, pl.ds(row0, nrows), :] for lb in range(NLB)], axis=1)


def _fill_before(ebuf, prev):
    sub = lax.broadcasted_iota(jnp.int32, (8, 128), 0)
    def halo(p, carry):
        for lb in range(NLB):
            cur = ebuf[lb, _rows8(R + p), :]
            ebuf[lb, _rows8(p), :] = jnp.where(sub == 0, pltpu.roll(prev[lb, _rows8(p), :], 1, 0),
                                               pltpu.roll(cur, 1, 0))
        return carry
    lax.fori_loop(0, NH, halo, 0)


def _fir(buf, lb, r, coef, first, flip):
    win = buf[lb, pl.ds(pl.multiple_of(r * 8, 8), (KS + RPI - 1) * 8), :]
    outs = []
    for i in range(RPI):
        acc = [first, None, None, None]
        for k in range(KS):
            o = i + ((KS - 1 - k) if flip else k)
            t = coef[k] * win[8 * o:8 * o + 8, :]
            acc[k % 4] = t if acc[k % 4] is None else acc[k % 4] + t
        outs.append((acc[0] + acc[1]) + (acc[2] + acc[3]))
    return outs


def _conv_fwd(proj, cw, cbias, lng, lnb):
    L = proj.shape[0]
    nc = L // TC
    def body(ca_ref, cb_ref, zc_ref, w_ref, b_ref, g_ref, bb_ref, cu1_ref, ain_ref, ebuf, prev, cacc):
        @pl.when(pl.program_id(0) == 0)
        def _():
            prev[...] = jnp.zeros_like(prev)
        def glu(s, carry):
            rows = pl.ds(pl.multiple_of(s * 64, 64), 64)
            _put_blocked(ebuf, pl.multiple_of(NH * 8 + s * 64, 64), 64,
                         ca_ref[rows, :].astype(f32) * _sig(cb_ref[rows, :].astype(f32)))
            return carry
        lax.fori_loop(0, TC // 64, glu, 0)
        _fill_before(ebuf, prev)
        prev[...] = ebuf[:, R * 8:(NH + R) * 8, :]
        for lb in range(NLB):
            sl = slice(lb * 128, (lb + 1) * 128)
            wk = [jnp.broadcast_to(w_ref[k:k + 1, sl], (8, 128)) for k in range(KS)]
            bias = jnp.broadcast_to(b_ref[:, sl], (8, 128))
            def tap(q, carry, lb=lb, wk=wk, bias=bias):
                r = q * RPI
                for i, o in enumerate(_fir(ebuf, lb, r + (NH - KS + 1), wk, bias, False)):
                    cacc[lb, _rows8(r + i), :] = o
                return carry
            lax.fori_loop(0, R // RPI, tap, 0)
        def norm(s, carry):
            rows = pl.ds(pl.multiple_of(s * 64, 64), 64)
            c1b = _get_blocked(cacc, pl.multiple_of(s * 64, 64), 64).astype(bf16)
            cu1_ref[rows, :] = c1b
            c1 = c1b.astype(f32)
            xc = c1 - jnp.mean(c1, axis=-1, keepdims=True)
            var = jnp.mean(xc * xc, axis=-1, keepdims=True)
            ln = xc * lax.rsqrt(var + LN_EPS) * g_ref[...] + bb_ref[...]
            zc = zc_ref[rows, :].astype(f32)
            ain_ref[rows, :] = ((ln * _sig(ln)) * (zc * _sig(zc))).astype(bf16)
            return carry
        lax.fori_loop(0, TC // 64, norm, 0, unroll=4)

    col = lambda c: pl.BlockSpec((TC, CW), lambda i, c=c: (i, c))
    return pl.pallas_call(
        body, grid=(nc,),
        in_specs=[col(0), col(1), col(2), _full((32, CW)), _full((1, CW)), _full((1, CW)), _full((1, CW))],
        out_specs=[pl.BlockSpec((TC, CW), lambda i: (i, 0)), pl.BlockSpec((TC, CW), lambda i: (i, 0))],
        out_shape=[jax.ShapeDtypeStruct((L, CW), bf16), jax.ShapeDtypeStruct((L, CW), bf16)],
        scratch_shapes=[pltpu.VMEM((NLB, (NH + R) * 8, 128), f32), pltpu.VMEM((NLB, NH * 8, 128), f32),
                        pltpu.VMEM((NLB, TC, 128), f32)],
        name="conv_fwd", compiler_params=_cp("arbitrary"))(proj, proj, proj, cw, cbias, lng, lnb)


def _gelu_parts(y0):
    t = jnp.tanh(GELU_K0 * (y0 + GELU_K1 * y0 * y0 * y0))
    return t, 0.5 * y0 * (1.0 + t)


def _ssm_fwd(proj, bbt_re, bbt_im, ct_re, ct_im, a_re, a_im, apow_re, apow_im, dvec, wglu, bglu):
    L = proj.shape[0]
    nc = L // TC
    def body(u_ref, zs_ref, bre_ref, bim_ref, cre_ref, cim_ref, are_ref, aim_ref, pwr_ref, pwi_ref,
             d_ref, wg_ref, bg_ref, y0_ref, bin_ref, sre, sim, cinr, cini, prev_re, prev_im):
        c = pl.program_id(0)
        @pl.when(c == 0)
        def _():
            prev_re[...] = jnp.zeros_like(prev_re)
            prev_im[...] = jnp.zeros_like(prev_im)
        u = u_ref[...]
        for blk in range(4):
            ub = u[:, 128 * blk:128 * (blk + 1)]
            sre[:, 512 * blk:512 * (blk + 1)] = _dot(ub, bre_ref[blk])
            sim[:, 512 * blk:512 * (blk + 1)] = _dot(ub, bim_ref[blk])
        for lb in range(NS // LBW):
            sl = slice(lb * LBW, (lb + 1) * LBW)
            ar = jnp.broadcast_to(are_ref[:, sl], (8, LBW))
            ai = jnp.broadcast_to(aim_ref[:, sl], (8, LBW))
            def step(r, carry, sl=sl, ar=ar, ai=ai):
                sr, si = carry
                nr = ar * sr - ai * si + sre[_rows8(r), sl]
                ni = ar * si + ai * sr + sim[_rows8(r), sl]
                sre[_rows8(r), sl] = nr
                sim[_rows8(r), sl] = ni
                return nr, ni
            lax.fori_loop(1, R, step, (sre[0:8, sl], sim[0:8, sl]))
        a_r = pwr_ref[R - 1:R, :]
        a_i = pwi_ref[R - 1:R, :]
        cr = prev_re[0:1, :]
        ci = prev_im[0:1, :]
        for seg in range(8):
            cinr[seg:seg + 1, :] = cr
            cini[seg:seg + 1, :] = ci
            er = sre[8 * (R - 1) + seg:8 * (R - 1) + seg + 1, :]
            ei = sim[8 * (R - 1) + seg:8 * (R - 1) + seg + 1, :]
            cr, ci = er + a_r * cr - a_i * ci, ei + a_r * ci + a_i * cr
        prev_re[0:1, :] = cr
        prev_im[0:1, :] = ci
        for lb in range(NS // LBW):
            sl = slice(lb * LBW, (lb + 1) * LBW)
            kr = cinr[:, sl]
            ki = cini[:, sl]
            def fix(r, carry, sl=sl, kr=kr, ki=ki):
                pr = jnp.broadcast_to(pwr_ref[pl.ds(r, 1), sl], (8, LBW))
                pi = jnp.broadcast_to(pwi_ref[pl.ds(r, 1), sl], (8, LBW))
                sre[_rows8(r), sl] = sre[_rows8(r), sl] + pr * kr - pi * ki
                sim[_rows8(r), sl] = sim[_rows8(r), sl] + pr * ki + pi * kr
                return carry
            lax.fori_loop(0, R, fix, 0, unroll=4)
        yp = []
        for blk in range(4):
            sr = sre[:, 512 * blk:512 * (blk + 1)].astype(bf16)
            si = sim[:, 512 * blk:512 * (blk + 1)].astype(bf16)
            yp.append(_dot(sr, cre_ref[blk]) - _dot(si, cim_ref[blk]))
        y0 = jnp.concatenate(yp, axis=1) + d_ref[...] * u.astype(f32)
        y0_ref[...] = y0
        _, y1 = _gelu_parts(y0)
        glu = _dot(y1.astype(bf16), wg_ref[...]) + bg_ref[...]
        y2 = y1 * _sig(glu)
        zs = zs_ref[...].astype(f32)
        bin_ref[...] = (y2 * (zs * _sig(zs))).astype(bf16)

    return pl.pallas_call(
        body, grid=(nc,),
        in_specs=[pl.BlockSpec((TC, SW), lambda c: (c, 6)), pl.BlockSpec((TC, SW), lambda c: (c, 7)),
                  _full((4, 128, 512)), _full((4, 128, 512)), _full((4, 512, 128)), _full((4, 512, 128)),
                  _full((1, NS)), _full((1, NS)), _full((R, NS)), _full((R, NS)),
                  _full((1, SW)), _full((SW, SW)), _full((1, SW))],
        out_specs=[pl.BlockSpec((TC, SW), lambda c: (c, 0)), pl.BlockSpec((TC, SW), lambda c: (c, 0)),
                   pl.BlockSpec((TC, NS), lambda c: (c, 0)), pl.BlockSpec((TC, NS), lambda c: (c, 0)),
                   pl.BlockSpec((8, NS), lambda c: (c, 0)), pl.BlockSpec((8, NS), lambda c: (c, 0))],
        out_shape=[jax.ShapeDtypeStruct((L, SW), f32), jax.ShapeDtypeStruct((L, SW), bf16),
                   jax.ShapeDtypeStruct((L, NS), f32), jax.ShapeDtypeStruct((L, NS), f32),
                   jax.ShapeDtypeStruct((nc * 8, NS), f32), jax.ShapeDtypeStruct((nc * 8, NS), f32)],
        scratch_shapes=[pltpu.VMEM((8, NS), f32), pltpu.VMEM((8, NS), f32)],
        name="ssm_fwd", compiler_params=_cp("arbitrary"))(
            proj, proj, bbt_re, bbt_im, ct_re, ct_im, a_re, a_im, apow_re, apow_im, dvec, wglu, bglu)


def _tail(a_in, b_in, proj, x, tgt, wco, wso, wout, gpost):
    L = x.shape[0]
    tm = 512
    def body(a_ref, b_ref, gc_ref, gs_ref, x_ref, t_ref, wco_ref, wso_ref, wout_ref, gp_ref,
             gx_ref, dain_ref, dbin_ref, dp_ref, dwout_ref, dwco_ref, dwso_ref, dgp_ref, loss_ref):
        @pl.when(pl.program_id(0) == 0)
        def _():
            dwout_ref[...] = jnp.zeros_like(dwout_ref)
            dwco_ref[...] = jnp.zeros_like(dwco_ref)
            dwso_ref[...] = jnp.zeros_like(dwso_ref)
            dgp_ref[...] = jnp.zeros_like(dgp_ref)
            loss_ref[...] = jnp.zeros_like(loss_ref)
        a = a_ref[...]
        b = b_ref[...]
        co = _dot(a, wco_ref[...])
        so = jnp.concatenate([_dot(b, wso_ref[j]) for j in range(NCHIP)], axis=1)
        sc = _sig(gc_ref[...].astype(f32))
        ss = _sig(gs_ref[...].astype(f32))
        mb = (sc * co + ss * so).astype(bf16)
        out = _dot(mb, wout_ref[...])
        r2 = lax.rsqrt(jnp.mean(out * out, axis=-1, keepdims=True) + RMS_EPS)
        on = out * r2
        gp = gp_ref[...]
        e = x_ref[...] + on * gp - t_ref[...]
        loss_ref[...] += (0.5 / D) * jnp.sum(e * e)
        dy = e * (1.0 / D)
        gx_ref[...] = dy
        dgp_ref[...] += jnp.sum(dy * on, axis=0, keepdims=True)
        dn = dy * gp
        dout = (r2 * (dn - on * jnp.mean(dn * on, axis=-1, keepdims=True))).astype(bf16)
        dwout_ref[...] += _dot_tn(mb, dout)
        dm = _dot_nt(dout, wout_ref[...])
        dp_ref[:, 0:D] = (dm * co * sc * (1.0 - sc)).astype(bf16)
        dp_ref[:, D:2 * D] = (dm * so * ss * (1.0 - ss)).astype(bf16)
        dco = (dm * sc).astype(bf16)
        dso = (dm * ss).astype(bf16)
        dwco_ref[...] += _dot_tn(a, dco)
        dbin = None
        for j in range(NCHIP):
            dso_j = dso[:, j * 256:(j + 1) * 256]
            dwso_ref[j] += _dot_tn(b, dso_j)
            t = _dot_nt(dso_j, wso_ref[j])
            dbin = t if dbin is None else dbin + t
        dain_ref[...] = _dot_nt(dco, wco_ref[...]).astype(bf16)
        dbin_ref[...] = dbin.astype(bf16)

    row = lambda w: pl.BlockSpec((tm, w), lambda i: (i, 0))
    one = lambda shape: pl.BlockSpec(shape, lambda i: (0,) * len(shape), pipeline_mode=pl.Buffered(1))
    return pl.pallas_call(
        body, grid=(L // tm,),
        in_specs=[row(CW), row(SW), pl.BlockSpec((tm, D), lambda i: (i, 4)), pl.BlockSpec((tm, D), lambda i: (i, 5)),
                  row(D), row(D), one((CW, D)), one((NCHIP, SW, 256)), one((D, D)), one((1, D))],
        out_specs=[row(D), row(CW), row(SW), pl.BlockSpec((tm, 2 * D), lambda i: (i, 2)),
                   one((D, D)), one((CW, D)), one((NCHIP, SW, 256)), one((1, D)), one((1, 128))],
        out_shape=[jax.ShapeDtypeStruct((L, D), f32), jax.ShapeDtypeStruct((L, CW), bf16),
                   jax.ShapeDtypeStruct((L, SW), bf16), jax.ShapeDtypeStruct((L, IN_W), bf16),
                   jax.ShapeDtypeStruct((D, D), f32), jax.ShapeDtypeStruct((CW, D), f32),
                   jax.ShapeDtypeStruct((NCHIP, SW, 256), f32), jax.ShapeDtypeStruct((1, D), f32),
                   jax.ShapeDtypeStruct((1, 128), f32)],
        name="tail", compiler_params=_cp("arbitrary"))(a_in, b_in, proj, proj, x, tgt, wco, wso, wout, gpost)


def _ssm_bwd(d_bin, y0, proj, sre, sim, cinr, cini, bbt_re, bbt_im, ct_re, ct_im,
             a_re, a_im, apow_re, apow_im, dvec, wglu, bglu, dproj):
    L = y0.shape[0]
    nc = L // TC
    def body(dbin_ref, y0_ref, u_ref, zs_ref, sre_ref, sim_ref, cinr_ref, cini_ref,
             bre_ref, bim_ref, cre_ref, cim_ref, are_ref, aim_ref, pwr_ref, pwi_ref, d_ref, wg_ref, bg_ref, _,
             dp_ref, dbre_ref, dbim_ref, dcre_ref, dcim_ref, dd_ref, dar_ref, dai_ref, dwg_ref, dbg_ref,
             gre, gim, gcr, gci, nxt_re, nxt_im):
        @pl.when(pl.program_id(0) == 0)
        def _():
            for ref in (dbre_ref, dbim_ref, dcre_ref, dcim_ref, dd_ref, dar_ref, dai_ref, dwg_ref, dbg_ref,
                        nxt_re, nxt_im):
                ref[...] = jnp.zeros_like(ref)
        y0 = y0_ref[...]
        u = u_ref[...]
        zs = zs_ref[...].astype(f32)
        dbin = dbin_ref[...].astype(f32)
        t, y1 = _gelu_parts(y0)
        y1b = y1.astype(bf16)
        sg = _sig(_dot(y1b, wg_ref[...]) + bg_ref[...])
        sz = _sig(zs)
        d_y2 = dbin * (zs * sz)
        dp_ref[:, SW:2 * SW] = (dbin * (y1 * sg) * (sz * (1.0 + zs * (1.0 - sz)))).astype(bf16)
        d_glu = d_y2 * y1 * sg * (1.0 - sg)
        d_glub = d_glu.astype(bf16)
        d_y1 = d_y2 * sg + _dot_nt(d_glub, wg_ref[...])
        dwg_ref[...] += _dot_tn(y1b, d_glub)
        dbg_ref[...] += jnp.sum(d_glu, axis=0, keepdims=True)
        dgelu = 0.5 * (1.0 + t) + 0.5 * y0 * (1.0 - t * t) * GELU_K0 * (1.0 + 3.0 * GELU_K1 * y0 * y0)
        d_y0 = d_y1 * dgelu
        dd_ref[...] += jnp.sum(d_y0 * u.astype(f32), axis=0, keepdims=True)
        dyb = d_y0.astype(bf16)
        for blk in range(4):
            dy1 = dyb[:, 128 * blk:128 * (blk + 1)]
            gre[:, 512 * blk:512 * (blk + 1)] = _dot_nt(dy1, cre_ref[blk])
            gim[:, 512 * blk:512 * (blk + 1)] = -_dot_nt(dy1, cim_ref[blk])
        for lb in range(NS // LBW):
            sl = slice(lb * LBW, (lb + 1) * LBW)
            ar = jnp.broadcast_to(are_ref[:, sl], (8, LBW))
            ai = jnp.broadcast_to(aim_ref[:, sl], (8, LBW))
            def step(k, carry, sl=sl, ar=ar, ai=ai):
                gr, gi = carry
                row = _rows8(R - 2 - k)
                nr = ar * gr + ai * gi + gre[row, sl]
                ni = ar * gi - ai * gr + gim[row, sl]
                gre[row, sl] = nr
                gim[row, sl] = ni
                return nr, ni
            lax.fori_loop(0, R - 1, step, (gre[8 * (R - 1):8 * R, sl], gim[8 * (R - 1):8 * R, sl]))
        a_r = pwr_ref[R - 1:R, :]
        a_i = pwi_ref[R - 1:R, :]
        cr = nxt_re[0:1, :]
        ci = nxt_im[0:1, :]
        for seg in range(7, -1, -1):
            gcr[seg:seg + 1, :] = cr
            gci[seg:seg + 1, :] = ci
            er = gre[seg:seg + 1, :]
            ei = gim[seg:seg + 1, :]
            cr, ci = er + a_r * cr + a_i * ci, ei + a_r * ci - a_i * cr
        nxt_re[0:1, :] = cr
        nxt_im[0:1, :] = ci
        for lb in range(NS // LBW):
            sl = slice(lb * LBW, (lb + 1) * LBW)
            kr = gcr[:, sl]
            ki = gci[:, sl]
            def fixed(rows, prow, sl=sl, kr=kr, ki=ki):
                pr = jnp.broadcast_to(pwr_ref[prow, sl], (8, LBW))
                pi = jnp.broadcast_to(pwi_ref[prow, sl], (8, LBW))
                gr = gre[rows, sl] + pr * kr + pi * ki
                gi = gim[rows, sl] + pr * ki - pi * kr
                gre[rows, sl] = gr
                gim[rows, sl] = gi
                return gr, gi
            g0r, g0i = fixed(slice(0, 8), slice(R - 1, R))
            p0r, p0i = cinr_ref[:, sl], cini_ref[:, sl]
            acc0 = (g0r * p0r + g0i * p0i, g0i * p0r - g0r * p0i)
            def dacc(r, carry, sl=sl, fixed=fixed):
                xr, xi = carry
                gr, gi = fixed(_rows8(r), pl.ds(R - 1 - r, 1))
                pr, pi = sre_ref[_rows8(r - 1), sl], sim_ref[_rows8(r - 1), sl]
                return xr + gr * pr + gi * pi, xi + gi * pr - gr * pi
            xr, xi = lax.fori_loop(1, R, dacc, acc0)
            dar_ref[:, sl] += xr
            dai_ref[:, sl] += xi
        dup = []
        for blk in range(4):
            s4 = slice(512 * blk, 512 * (blk + 1))
            s1 = slice(128 * blk, 128 * (blk + 1))
            grb = gre[:, s4].astype(bf16)
            gib = gim[:, s4].astype(bf16)
            dup.append(_dot_nt(grb, bre_ref[blk]) + _dot_nt(gib, bim_ref[blk]))
            dbre_ref[blk] += _dot_tn(u[:, s1], grb)
            dbim_ref[blk] += _dot_tn(u[:, s1], gib)
            dcre_ref[blk] += _dot_tn(dyb[:, s1], sre_ref[:, s4].astype(bf16))
            dcim_ref[blk] -= _dot_tn(dyb[:, s1], sim_ref[:, s4].astype(bf16))
        dp_ref[:, 0:SW] = (jnp.concatenate(dup, axis=1) + d_ref[...] * d_y0).astype(bf16)

    rev = lambda w, cidx: pl.BlockSpec((TC, w), lambda i, cidx=cidx: (nc - 1 - i, cidx))
    one = lambda shape: pl.BlockSpec(shape, lambda i: (0,) * len(shape))
    return pl.pallas_call(
        body, grid=(nc,),
        in_specs=[rev(SW, 0), rev(SW, 0), rev(SW, 6), rev(SW, 7), rev(NS, 0), rev(NS, 0),
                  pl.BlockSpec((8, NS), lambda i: (nc - 1 - i, 0)), pl.BlockSpec((8, NS), lambda i: (nc - 1 - i, 0)),
                  one((4, 128, 512)), one((4, 128, 512)), one((4, 512, 128)), one((4, 512, 128)),
                  one((1, NS)), one((1, NS)), one((R, NS)), one((R, NS)),
                  one((1, SW)), one((SW, SW)), one((1, SW)), _ANY],
        out_specs=[pl.BlockSpec((TC, 2 * SW), lambda i: (nc - 1 - i, 3)),
                   one((4, 128, 512)), one((4, 128, 512)), one((4, 128, 512)), one((4, 128, 512)),
                   one((1, SW)), one((8, NS)), one((8, NS)), one((SW, SW)), one((1, SW))],
        out_shape=[jax.ShapeDtypeStruct((L, IN_W), bf16),
                   jax.ShapeDtypeStruct((4, 128, 512), f32), jax.ShapeDtypeStruct((4, 128, 512), f32),
                   jax.ShapeDtypeStruct((4, 128, 512), f32), jax.ShapeDtypeStruct((4, 128, 512), f32),
                   jax.ShapeDtypeStruct((1, SW), f32), jax.ShapeDtypeStruct((8, NS), f32),
                   jax.ShapeDtypeStruct((8, NS), f32), jax.ShapeDtypeStruct((SW, SW), f32),
                   jax.ShapeDtypeStruct((1, SW), f32)],
        scratch_shapes=[pltpu.VMEM((TC, NS), f32), pltpu.VMEM((TC, NS), f32), pltpu.VMEM((8, NS), f32),
                        pltpu.VMEM((8, NS), f32), pltpu.VMEM((8, NS), f32), pltpu.VMEM((8, NS), f32)],
        input_output_aliases={19: 0},
        name="ssm_bwd", compiler_params=_cp("arbitrary"))(
            d_bin, y0, proj, proj, sre, sim, cinr, cini, bbt_re, bbt_im, ct_re, ct_im,
            a_re, a_im, apow_re, apow_im, dvec, wglu, bglu, dproj)


def _conv_bwd(d_ain, cu1, proj, cw, lng, lnb, dproj):
    L = cu1.shape[0]
    nc = L // TC
    def body(dain_ref, cu1_ref, ca_ref, cb_ref, zc_ref, cah_ref, cbh_ref, w_ref, g_ref, bb_ref, _,
             dp_ref, dw_ref, dbias_ref, dlng_ref, dlnb_ref, dbuf, ebuf, prev, nxt, dcu0):
        i = pl.program_id(0)
        @pl.when(i == 0)
        def _():
            dw_ref[...] = jnp.zeros_like(dw_ref)
            dbias_ref[...] = jnp.zeros_like(dbias_ref)
            dlng_ref[...] = jnp.zeros_like(dlng_ref)
            dlnb_ref[...] = jnp.zeros_like(dlnb_ref)
            nxt[...] = jnp.zeros_like(nxt)
        def lnb(s, carry):
            rows = pl.ds(pl.multiple_of(s * 32, 32), 32)
            dain = dain_ref[rows, :].astype(f32)
            c1 = cu1_ref[rows, :].astype(f32)
            zc = zc_ref[rows, :].astype(f32)
            xc = c1 - jnp.mean(c1, axis=-1, keepdims=True)
            var = jnp.mean(xc * xc, axis=-1, keepdims=True)
            rstd = lax.rsqrt(var + LN_EPS)
            xh = xc * rstd
            ln = xh * g_ref[...] + bb_ref[...]
            sl_ = _sig(ln)
            sz = _sig(zc)
            dp_ref[rows, 2 * CW:3 * CW] = (dain * (ln * sl_) * (sz * (1.0 + zc * (1.0 - sz)))).astype(bf16)
            d_ln = dain * (zc * sz) * (sl_ * (1.0 + ln * (1.0 - sl_)))
            dlng_ref[...] += jnp.sum(d_ln * xh, axis=0, keepdims=True)
            dlnb_ref[...] += jnp.sum(d_ln, axis=0, keepdims=True)
            dxh = d_ln * g_ref[...]
            d_c1 = rstd * (dxh - jnp.mean(dxh, axis=-1, keepdims=True)
                           - xh * jnp.mean(dxh * xh, axis=-1, keepdims=True))
            dbias_ref[...] += jnp.sum(d_c1, axis=0, keepdims=True)
            _put_blocked(dbuf, pl.multiple_of(s * 32, 32), 32, d_c1)
            _put_blocked(ebuf, pl.multiple_of(NH * 8 + s * 32, 32), 32,
                         ca_ref[rows, :].astype(f32) * _sig(cb_ref[rows, :].astype(f32)))
            return carry
        lax.fori_loop(0, TC // 32, lnb, 0, unroll=8)
        sub = lax.broadcasted_iota(jnp.int32, (8, 128), 0)
        def after(p, carry):
            for lb in range(NLB):
                cur = dbuf[lb, _rows8(p), :]
                dbuf[lb, _rows8(R + p), :] = jnp.where(sub == 7, pltpu.roll(nxt[lb, _rows8(p), :], 7, 0),
                                                       pltpu.roll(cur, 7, 0))
            return carry
        lax.fori_loop(0, NH, after, 0)
        nxt[...] = dbuf[:, 0:NH * 8, :]
        def before(s, carry):
            rows = pl.ds(pl.multiple_of(s * 64, 64), 64)
            v = cah_ref[rows, :].astype(f32) * _sig(cbh_ref[rows, :].astype(f32))
            _put_blocked(prev, pl.multiple_of(s * 64, 64), 64, jnp.where(i == nc - 1, jnp.zeros_like(v), v))
            return carry
        lax.fori_loop(0, NH * 8 // 64, before, 0)
        _fill_before(ebuf, prev)
        for lb in range(NLB):
            sl = slice(lb * 128, (lb + 1) * 128)
            wk = [jnp.broadcast_to(w_ref[k:k + 1, sl], (8, 128)) for k in range(KS)]
            def tap(q, carry, lb=lb, wk=wk):
                r = q * RPI
                for j, o in enumerate(_fir(dbuf, lb, r, wk, None, True)):
                    dcu0[lb, _rows8(r + j), :] = o
                return carry
            lax.fori_loop(0, R // RPI, tap, 0)
            def wgrad(q, accs, lb=lb):
                r = q * RPI
                dvs = dbuf[lb, pl.ds(pl.multiple_of(r * 8, 8), RPI * 8), :]
                win = ebuf[lb, pl.ds(pl.multiple_of((r + (NH - KS + 1)) * 8, 8), (KS + RPI - 1) * 8), :]
                accs = list(accs)
                for j in range(RPI):
                    dv = dvs[8 * j:8 * j + 8, :]
                    for k in range(KS):
                        accs[k] = accs[k] + dv * win[8 * (j + k):8 * (j + k) + 8, :]
                return tuple(accs)
            accs = lax.fori_loop(0, R // RPI, wgrad, tuple(jnp.zeros((8, 128), f32) for _ in range(KS)))
            for k in range(KS):
                dw_ref[k, :, sl] += accs[k]
        def glub(s, carry):
            rows = pl.ds(pl.multiple_of(s * 64, 64), 64)
            d0 = _get_blocked(dcu0, pl.multiple_of(s * 64, 64), 64)
            ca = ca_ref[rows, :].astype(f32)
            sb = _sig(cb_ref[rows, :].astype(f32))
            dp_ref[rows, 0:CW] = (d0 * sb).astype(bf16)
            dp_ref[rows, CW:2 * CW] = (d0 * ca * sb * (1.0 - sb)).astype(bf16)
            return carry
        lax.fori_loop(0, TC // 64, glub, 0)

    hrows = NH * 8
    per = TC // hrows
    rev = lambda cidx: pl.BlockSpec((TC, CW), lambda i, cidx=cidx: (nc - 1 - i, cidx))
    halo = lambda cidx: pl.BlockSpec((hrows, CW), lambda i, cidx=cidx: (jnp.maximum((nc - 1 - i) * per - 1, 0), cidx))
    one = lambda shape: pl.BlockSpec(shape, lambda i: (0,) * len(shape))
    return pl.pallas_call(
        body, grid=(nc,),
        in_specs=[rev(0), rev(0), rev(0), rev(1), rev(2), halo(0), halo(1), one((32, CW)), one((1, CW)), one((1, CW)),
                  _ANY],
        out_specs=[pl.BlockSpec((TC, 3 * CW), lambda i: (nc - 1 - i, 0)), one((32, 8, CW)), one((1, CW)), one((1, CW)), one((1, CW))],
        out_shape=[jax.ShapeDtypeStruct((L, IN_W), bf16), jax.ShapeDtypeStruct((32, 8, CW), f32),
                   jax.ShapeDtypeStruct((1, CW), f32), jax.ShapeDtypeStruct((1, CW), f32),
                   jax.ShapeDtypeStruct((1, CW), f32)],
        scratch_shapes=[pltpu.VMEM((NLB, (R + NH) * 8, 128), f32), pltpu.VMEM((NLB, (NH + R) * 8, 128), f32),
                        pltpu.VMEM((NLB, hrows, 128), f32), pltpu.VMEM((NLB, hrows, 128), f32),
                        pltpu.VMEM((NLB, TC, 128), f32)],
        input_output_aliases={10: 0},
        name="conv_bwd", compiler_params=_cp("arbitrary"))(d_ain, cu1, proj, proj, proj, proj, proj, cw, lng, lnb, dproj)


def _win_grad(h, dproj):
    L = h.shape[0]
    tm = min(2048, L)
    nt = L // tm
    def body(h_ref, d_ref, o_ref, acc):
        i = pl.program_id(1)
        @pl.when(i == 0)
        def _():
            acc[...] = jnp.zeros_like(acc)
        acc[...] += _dot_tn(h_ref[...], d_ref[...])
        @pl.when(i == nt - 1)
        def _():
            o_ref[0] = acc[...].astype(bf16)
    return pl.pallas_call(
        body, grid=(NCHIP, nt),
        in_specs=[pl.BlockSpec((tm, D), lambda j, i: (i, 0)), pl.BlockSpec((tm, SHARD_W), lambda j, i: (i, j))],
        out_specs=pl.BlockSpec((1, D, SHARD_W), lambda j, i: (j, 0, 0)),
        out_shape=jax.ShapeDtypeStruct((NCHIP, D, SHARD_W), bf16),
        scratch_shapes=[pltpu.VMEM((D, SHARD_W), f32)],
        name="win_grad", compiler_params=_cp("arbitrary", "arbitrary"))(h, dproj)


def _adamw_math(w, g, m, v):
    m2 = B1 * m + (1.0 - B1) * g
    v2 = B2 * v + (1.0 - B2) * (g * g)
    m_hat = m2 / (1.0 - B1 ** STEP)
    v_hat = v2 / (1.0 - B2 ** STEP)
    delta = -LR * (m_hat / (jnp.sqrt(v_hat) + EPS) + WD * w)
    return delta, m2, v2


def _adamw(name, w, g, m, v):
    rows, cols = w.shape
    tm = rows if rows <= 256 else (256 if rows % 256 == 0 else 128)
    assert rows % tm == 0
    def body(w_ref, g_ref, m_ref, v_ref, d_ref, m2_ref, v2_ref):
        d, m2, v2 = _adamw_math(w_ref[...], g_ref[...], m_ref[...], v_ref[...])
        d_ref[...] = d
        m2_ref[...] = m2
        v2_ref[...] = v2
    spec = pl.BlockSpec((tm, cols), lambda i: (i, 0))
    shp = jax.ShapeDtypeStruct((rows, cols), f32)
    return pl.pallas_call(
        body, grid=(rows // tm,), in_specs=[spec] * 4, out_specs=[spec] * 3, out_shape=[shp] * 3,
        name=name, compiler_params=_cp("arbitrary"))(w, g, m, v)


def _adamw_group(name, ws, gs, ms, vs):
    n = len(ws)
    def body(*refs):
        for i in range(n):
            w_ref, g_ref, m_ref, v_ref = (refs[q * n + i] for q in range(4))
            d, m2, v2 = _adamw_math(w_ref[...], g_ref[...], m_ref[...], v_ref[...])
            for q, val in enumerate((d, m2, v2)):
                refs[(4 + q) * n + i][...] = val
    shapes = [jax.ShapeDtypeStruct(w.shape, f32) for w in ws]
    out = pl.pallas_call(body, out_shape=shapes * 3, name=name,
                         compiler_params=pltpu.CompilerParams(vmem_limit_bytes=VMEM_LIMIT))(*ws, *gs, *ms, *vs)
    return [(out[i], out[n + i], out[2 * n + i]) for i in range(n)]


_ANY = pl.BlockSpec(memory_space=pl.ANY)


def _chunks(rows, parts):
    step = rows // parts
    assert step * parts == rows and step % 16 == 0
    return [(i * step, step) for i in range(parts)]


def _place():
    x, y, c = lax.axis_index("x"), lax.axis_index("y"), lax.axis_index("c")
    chips = [(1 - x, y), (x, 1 - y), (1 - x, 1 - y)]
    return x, y, c, chips


def _nchunks(half, cols, itemsize):
    return 4 if half * cols * itemsize >= (1 << 20) else 1


def _segments(metas):
    segs = []
    for w, (half, cols, dt) in enumerate(metas):
        for r0, n in _chunks(half, _nchunks(half, cols, jnp.dtype(dt).itemsize)):
            segs.append((w, half, r0, n))
    return segs


def _rcopy(i, src, dst, send_sems, recv_sems, to):
    return pltpu.make_async_remote_copy(src_ref=src, dst_ref=dst, send_sem=send_sems.at[i], recv_sem=recv_sems.at[i],
                                        device_id=to, device_id_type=MESH)


def _gather_prep(k_arr, shards, x, tgt, g_pre, perm):
    na = len(shards)
    L = x.shape[0]
    nc = L // TC
    segs = _segments([(a.shape[0] // 2, a.shape[1], a.dtype) for a in shards])
    ns = len(segs)
    def body(_, *refs):
        ins = refs[:na]
        x_ref, t_ref, g_ref, p_ref = refs[na:na + 4]
        outs = refs[na + 4:2 * na + 4]
        h_ref, xi_ref, ti_ref, proj_ref = refs[2 * na + 4:2 * na + 8]
        stages = refs[2 * na + 8:3 * na + 8]
        send_sems, recv_sems, local_sems = refs[3 * na + 8:]
        i = pl.program_id(0)
        x, y, c, chips = _place()
        k = 2 * x + y
        me, sibling = (x, y, c), (x, y, 1 - c)

        def dst(w, half, chip, pc, r0, n):
            return outs[w].at[chip, pl.ds(pc * half + r0, n), :]

        def firsts():
            return [_rcopy(j * ns + s, ins[w].at[pl.ds(c * half + r0, n), :], dst(w, half, k, c, r0, n),
                           send_sems, recv_sems, (*chip, c))
                    for j, chip in enumerate(chips) for s, (w, half, r0, n) in enumerate(segs)]

        def own_out(w):
            return pltpu.make_async_copy(stages[w], outs[w].at[k], local_sems.at[w])

        @pl.when(i == 0)
        def _():
            for cp in firsts():
                cp.start()
            cins = [pltpu.make_async_copy(ins[w], stages[w], local_sems.at[w]) for w in range(na)]
            for cp in cins:
                cp.start()
            for w in range(na):
                cins[w].wait()
                own_out(w).start()

        p = p_ref[...]
        def through(v):
            hi = v.astype(bf16)
            r1 = v - hi.astype(f32)
            mid = r1.astype(bf16)
            lo = (r1 - mid.astype(f32)).astype(bf16)
            return (_dot(p, hi) + _dot(p, mid)) + _dot(p, lo)
        xt = x_ref[...]
        r = lax.rsqrt(jnp.mean(xt * xt, axis=-1, keepdims=True) + RMS_EPS)
        hp = _dot(p, (xt * r * g_ref[...]).astype(bf16)).astype(bf16)
        h_ref[...] = hp
        proj_ref[...] = _dot(hp, stages[0][...]).astype(bf16)
        xi_ref[...] = through(xt)
        ti_ref[...] = through(t_ref[...])

        def forwards():
            out = []
            for j, chip in enumerate(chips):
                cj = 2 * chip[0] + chip[1]
                for s, (w, half, r0, n) in enumerate(segs):
                    landed = dst(w, half, cj, c, r0, n)
                    out.append(_rcopy(3 * ns + j * ns + s, landed, landed, send_sems, recv_sems, sibling))
            return out

        @pl.when(i == max(nc - 2, 0))
        def _():
            for cp, fwd in zip(firsts(), forwards()):
                cp.wait_recv()
                fwd.start()

        @pl.when(i == nc - 1)
        def _():
            passed = forwards()
            for j, chip in enumerate(chips):
                cj = 2 * chip[0] + chip[1]
                for s, (w, half, r0, n) in enumerate(segs):
                    theirs = dst(w, half, cj, 1 - c, r0, n)
                    _rcopy(3 * ns + j * ns + s, theirs, theirs, send_sems, recv_sems, me).wait_recv()
            for cp in firsts() + passed:
                cp.wait_send()
            for w in range(na):
                own_out(w).wait()

    row = lambda: pl.BlockSpec((TC, D), lambda i, k: (i, 0))
    grid_spec = pltpu.PrefetchScalarGridSpec(
        num_scalar_prefetch=1, grid=(nc,),
        in_specs=[_ANY] * na + [row(), row(), pl.BlockSpec((1, D), lambda i, k: (0, 0)),
                                pl.BlockSpec((TC, TC), lambda i, k: (0, 0))],
        out_specs=[_ANY] * na + [row(), row(), row(), pl.BlockSpec((TC, SHARD_W), lambda i, k: (i, k[0]))],
        scratch_shapes=[pltpu.VMEM(a.shape, a.dtype) for a in shards]
        + [pltpu.SemaphoreType.DMA((6 * ns,)), pltpu.SemaphoreType.DMA((6 * ns,)), pltpu.SemaphoreType.DMA((na,))])
    return pl.pallas_call(
        body, grid_spec=grid_spec,
        out_shape=[jax.ShapeDtypeStruct((NCHIP,) + a.shape, a.dtype) for a in shards]
        + [jax.ShapeDtypeStruct((L, D), bf16), jax.ShapeDtypeStruct((L, D), f32), jax.ShapeDtypeStruct((L, D), f32),
           jax.ShapeDtypeStruct((L, IN_W), bf16)],
        name="gather_prep", compiler_params=_cp("arbitrary"))(k_arr, *shards, x, tgt, g_pre, perm)


def _x_grad_exchange(dproj, w_in, x, gx0, g_pre, parts, small):
    L = x.shape[0]
    tm = 512
    nt = L // tm
    na = len(parts)
    hs = SMALL_ROWS // 2
    segs = _segments([(p.shape[1], p.shape[2], p.dtype) for p in parts])
    ns = len(segs) + 1
    def body(*refs):
        d_ref, w_ref, x_ref, gx_ref, g_ref = refs[:5]
        ins, s_ref = refs[5:5 + na], refs[5 + na]
        o_ref, dg_ref = refs[6 + na:8 + na]
        outs, qs_ref = refs[8 + na:8 + 2 * na], refs[8 + 2 * na]
        stages = refs[9 + 2 * na:10 + 3 * na]
        send_sems, recv_sems, local_sems = refs[10 + 3 * na:]
        i = pl.program_id(0)
        x, y, c, chips = _place()
        k = 2 * x + y

        def my_small():
            return s_ref.at[pl.ds(c * hs, hs), :]

        def copies():
            out = []
            for j, chip in enumerate(chips):
                cj = 2 * chip[0] + chip[1]
                pieces = [(my_small(), qs_ref.at[k])]
                pieces += [(ins[w].at[cj, pl.ds(r0, n), :], outs[w].at[k, pl.ds(r0, n), :]) for w, _, r0, n in segs]
                out += [_rcopy(ns * j + s, src, d, send_sems, recv_sems, (*chip, c)) for s, (src, d) in enumerate(pieces)]
            return out

        def own_out(w):
            dst = qs_ref.at[k] if w == na else outs[w].at[k]
            return pltpu.make_async_copy(stages[w], dst, local_sems.at[w])

        @pl.when(i == 0)
        def _():
            dg_ref[...] = jnp.zeros_like(dg_ref)
            for cp in copies():
                cp.start()
            cins = [pltpu.make_async_copy(my_small() if w == na else ins[w].at[k], stages[w], local_sems.at[w])
                    for w in range(na + 1)]
            for cp in cins:
                cp.start()
            for w in range(na + 1):
                cins[w].wait()
                own_out(w).start()

        dh = _dot_nt(d_ref[:, 0:SHARD_W], w_ref[0])
        for j in range(1, NCHIP):
            dh = dh + _dot_nt(d_ref[:, j * SHARD_W:(j + 1) * SHARD_W], w_ref[j])
        xt = x_ref[...]
        r = lax.rsqrt(jnp.mean(xt * xt, axis=-1, keepdims=True) + RMS_EPS)
        xn = xt * r
        dg_ref[...] += jnp.sum(dh * xn, axis=0, keepdims=True)
        dxn = dh * g_ref[...]
        o_ref[...] = gx_ref[...] + r * (dxn - xn * jnp.mean(dxn * xn, axis=-1, keepdims=True))

        @pl.when(i == nt - 1)
        def _():
            for cp in copies():
                cp.wait_recv()
            for cp in copies():
                cp.wait_send()
            for w in range(na + 1):
                own_out(w).wait()

    return pl.pallas_call(
        body, grid=(nt,),
        in_specs=[pl.BlockSpec((tm, IN_W), lambda i: (i, 0)),
                  pl.BlockSpec((NCHIP, D, SHARD_W), lambda i: (0, 0, 0), pipeline_mode=pl.Buffered(1)),
                  pl.BlockSpec((tm, D), lambda i: (i, 0)), pl.BlockSpec((tm, D), lambda i: (i, 0)), _full((1, D))]
        + [_ANY] * (na + 1),
        out_specs=[pl.BlockSpec((tm, D), lambda i: (i, 0)), _full((1, D))] + [_ANY] * (na + 1),
        out_shape=[jax.ShapeDtypeStruct((L, D), f32), jax.ShapeDtypeStruct((1, D), f32)]
        + [jax.ShapeDtypeStruct(p.shape, bf16) for p in parts] + [jax.ShapeDtypeStruct((NCHIP, hs, 128), f32)],
        scratch_shapes=[pltpu.VMEM(p.shape[1:], bf16) for p in parts] + [pltpu.VMEM((hs, 128), f32)]
        + [pltpu.SemaphoreType.DMA((3 * ns,)), pltpu.SemaphoreType.DMA((3 * ns,)), pltpu.SemaphoreType.DMA((na + 1,))],
        name="x_grad_exchange", compiler_params=_cp("arbitrary"))(dproj, w_in, x, gx0, g_pre, *parts, small)


def _sibling_join_list(halves):
    na = len(halves)
    segs = _segments([(h.shape[0], h.shape[1], h.dtype) for h in halves])
    def body(*refs):
        ins, outs, stages = refs[:na], refs[na:2 * na], refs[2 * na:3 * na]
        send_sems, recv_sems, local_sems = refs[3 * na:]
        x, y, c, _ = _place()
        copies = [_rcopy(i, ins[w].at[pl.ds(r0, n), :], outs[w].at[pl.ds(c * half + r0, n), :], send_sems, recv_sems,
                         (x, y, 1 - c)) for i, (w, half, r0, n) in enumerate(segs)]
        for cp in copies:
            cp.start()
        cins = [pltpu.make_async_copy(ins[w], stages[w], local_sems.at[w]) for w in range(na)]
        for cp in cins:
            cp.start()
        own = []
        for w in range(na):
            cins[w].wait()
            half = halves[w].shape[0]
            own.append(pltpu.make_async_copy(stages[w], outs[w].at[pl.ds(c * half, half), :], local_sems.at[w]))
            own[-1].start()
        for cp in copies:
            cp.wait_recv()
        for cp in copies:
            cp.wait_send()
        for cp in own:
            cp.wait()

    return pl.pallas_call(
        body, in_specs=[_ANY] * na, out_specs=[_ANY] * na,
        out_shape=[jax.ShapeDtypeStruct((2 * h.shape[0], h.shape[1]), f32) for h in halves],
        scratch_shapes=[pltpu.VMEM(h.shape, f32) for h in halves]
        + [pltpu.SemaphoreType.DMA((len(segs),)), pltpu.SemaphoreType.DMA((len(segs),)), pltpu.SemaphoreType.DMA((na,))],
        name="sibling_join")(*halves)


def _small_join(v, fs_half):
    hs = fs_half.shape[0]
    def body(v_ref, h_ref, o_ref, fs_ref, send_sems, recv_sems):
        x, y, c, _ = _place()
        me = 4 * x + 2 * y + c
        o_ref[me] = v_ref[...]
        mine = pl.ds(pl.multiple_of(c * hs, 8), hs)
        fs_ref[mine, :] = h_ref[...]
        copies = [_rcopy(7, h_ref, fs_ref.at[mine, :], send_sems, recv_sems, (x, y, 1 - c))]
        i = 0
        for dx in range(2):
            for dy in range(2):
                for dc in range(2):
                    if dx + dy + dc:
                        copies.append(_rcopy(i, v_ref, o_ref.at[me], send_sems, recv_sems, (x ^ dx, y ^ dy, c ^ dc)))
                        i += 1
        for cp in copies:
            cp.start()
        for cp in copies:
            cp.wait_recv()
        for cp in copies:
            cp.wait_send()

    vm = pl.BlockSpec(memory_space=pltpu.VMEM)
    return pl.pallas_call(
        body, in_specs=[vm, vm], out_specs=[vm, vm],
        out_shape=[jax.ShapeDtypeStruct((8, 8, 128), f32), jax.ShapeDtypeStruct((2 * hs, 128), f32)],
        scratch_shapes=[pltpu.SemaphoreType.DMA((8,)), pltpu.SemaphoreType.DMA((8,))],
        name="small_join")(v, fs_half)


def _adamw_rows(parts, w, m, v):
    def body(p_ref, w_ref, m_ref, v_ref, g_ref, d_ref, m2_ref, v2_ref):
        g = p_ref[0]
        for dvc in range(1, 8):
            g = g + p_ref[dvc]
        g_ref[...] = g
        d, m2, v2 = _adamw_math(w_ref[...], g, m_ref[...], v_ref[...])
        d_ref[...] = d
        m2_ref[...] = m2
        v2_ref[...] = v2
    return pl.pallas_call(body, out_shape=[jax.ShapeDtypeStruct((8, 128), f32)] * 4, name="adamw_pre_norm_gain")(
        parts, w, m, v)


def _pair_exchange_list(grads, small):
    na = len(grads)
    segs = _segments([(g.shape[1] // 2, g.shape[2], g.dtype) for g in grads])
    n = NCHIP * len(segs) + 1
    def body(*refs):
        ins, s_ref, outs, rs_ref, (send_sems, recv_sems) = (refs[:na], refs[na], refs[na + 1:2 * na + 1],
                                                            refs[2 * na + 1], refs[2 * na + 2:])
        x, y, c, _ = _place()
        pieces = [(s_ref, rs_ref)]
        for j in range(NCHIP):
            for w, half, r0, rows in segs:
                pieces.append((ins[w].at[j, pl.ds((1 - c) * half + r0, rows), :], outs[w].at[j, pl.ds(r0, rows), :]))
        copies = [_rcopy(i, s, d, send_sems, recv_sems, (x, y, 1 - c)) for i, (s, d) in enumerate(pieces)]
        for cp in copies:
            cp.start()
        for cp in copies:
            cp.wait_recv()
        for cp in copies:
            cp.wait_send()

    return pl.pallas_call(
        body, in_specs=[_ANY] * (na + 1), out_specs=[_ANY] * (na + 1),
        out_shape=[jax.ShapeDtypeStruct((NCHIP, g.shape[1] // 2, g.shape[2]), g.dtype) for g in grads]
        + [jax.ShapeDtypeStruct((SMALL_ROWS, 128), f32)],
        scratch_shapes=[pltpu.SemaphoreType.DMA((n,)), pltpu.SemaphoreType.DMA((n,))],
        name="pair_exchange")(*grads, small)


def _pair_sum_list(c_arr, grads, recvs, small, rsmall):
    na = len(grads)
    def body(c_ref, *refs):
        g_refs, r_refs, s_ref, rs_ref = refs[:na], refs[na:2 * na], refs[2 * na], refs[2 * na + 1]
        o_refs, os_ref = refs[2 * na + 2:3 * na + 2], refs[3 * na + 2]
        for g_ref, r_ref, o_ref in zip(g_refs, r_refs, o_refs):
            o_ref[...] = (g_ref[...].astype(f32) + r_ref[...].astype(f32)).astype(bf16)
        os_ref[...] = s_ref[...] + rs_ref[...]
    half = lambda g: pl.BlockSpec((1, g.shape[1] // 2, g.shape[2]), lambda j, c: (j, c[0], 0))
    low = lambda g: pl.BlockSpec((1, g.shape[1] // 2, g.shape[2]), lambda j, c: (j, 0, 0))
    sm = pl.BlockSpec((SMALL_ROWS, 128), lambda j, c: (0, 0))
    grid_spec = pltpu.PrefetchScalarGridSpec(
        num_scalar_prefetch=1, grid=(NCHIP,),
        in_specs=[half(g) for g in grads] + [low(g) for g in grads] + [sm, sm],
        out_specs=[low(g) for g in grads] + [sm])
    return pl.pallas_call(
        body, grid_spec=grid_spec,
        out_shape=[jax.ShapeDtypeStruct((NCHIP, g.shape[1] // 2, g.shape[2]), bf16) for g in grads]
        + [jax.ShapeDtypeStruct((SMALL_ROWS, 128), f32)],
        name="pair_sum", compiler_params=_cp("arbitrary"))(c_arr, *grads, *recvs, small, rsmall)


def _chip_sum_list(parts, small):
    na = len(parts)
    nt = 2
    def body(*refs):
        for q_ref, f_ref in zip(refs[:na + 1], refs[na + 1:]):
            acc = q_ref[0].astype(f32)
            for j in range(1, NCHIP):
                acc = acc + q_ref[j].astype(f32)
            f_ref[...] = acc
    arrs = list(parts) + [small]
    return pl.pallas_call(
        body, grid=(nt,),
        in_specs=[pl.BlockSpec((NCHIP, a.shape[1] // nt, a.shape[2]), lambda i: (0, i, 0)) for a in arrs],
        out_specs=[pl.BlockSpec((a.shape[1] // nt, a.shape[2]), lambda i: (i, 0)) for a in arrs],
        out_shape=[jax.ShapeDtypeStruct(a.shape[1:], f32) for a in arrs],
        name="chip_sum", compiler_params=_cp("arbitrary"))(*arrs)


_SMALL =(("conv_b", (1, 1024)), ("conv_ln_gain", (1, 1024)), ("conv_ln_bias", (1, 1024)),
          ("ssm_lambda_re", (1, 32, 64)), ("ssm_lambda_im", (1, 32, 64)), ("ssm_log_dt", (1, 32)),
          ("ssm_b_re", (1, 32, 64, 16)), ("ssm_b_im", (1, 32, 64, 16)), ("ssm_c_re", (1, 32, 16, 64)),
          ("ssm_c_im", (1, 32, 16, 64)), ("ssm_d", (1, 32, 16)), ("b_ssm_glu", (1, 512)), ("post_norm_gain", (1, 1024)))


def _pack_small(vals, extra=None):
    rows = []
    for v in list(vals) + ([extra] if extra is not None else []):
        flat = v.reshape(-1).astype(f32)
        n = -(-flat.shape[0] // 1024) * 1024
        rows.append(jnp.pad(flat, (0, n - flat.shape[0])).reshape(-1, 128))
    used = sum(r.shape[0] for r in rows)
    rows.append(jnp.zeros((SMALL_ROWS - used, 128), f32))
    return jnp.concatenate(rows, axis=0)


def _unpack_small(p):
    o = 0
    out = []
    for _, shape in _SMALL:
        n = int(np.prod(shape))
        nr = -(-n // 1024) * 8
        out.append(p[o:o + nr].reshape(-1)[:n].reshape(shape))
        o += nr
    return out, p[o, 0]


def _discretize(lam_re, lam_im, log_dt, b_re, b_im):
    dt = jnp.exp(log_dt)[:, None]
    mag = jnp.exp(lam_re * dt)
    ar = mag * jnp.cos(lam_im * dt)
    ai = mag * jnp.sin(lam_im * dt)
    den = lam_re * lam_re + lam_im * lam_im
    zr = ((ar - 1.0) * lam_re + ai * lam_im) / den
    zi = (ai * lam_re - (ar - 1.0) * lam_im) / den
    bbr = zr[..., None] * b_re - zi[..., None] * b_im
    bbi = zr[..., None] * b_im + zi[..., None] * b_re
    return ar, ai, bbr, bbi


_EYE8 = np.eye(8, dtype=np.float32)


def _bbt_blocks(bb):
    v = bb.reshape(4, 8, PST, H).transpose(0, 1, 3, 2)
    return jnp.einsum("bghp,gk->bghkp", v, _EYE8).reshape(4, 128, 512)


def _bbt_unblock(m):
    v = jnp.einsum("bghkp,gk->bghp", m.reshape(4, 8, H, 8, PST), _EYE8)
    return v.transpose(0, 1, 3, 2).reshape(G, PST, H)


def _ct_blocks(cc):
    v = cc.reshape(4, 8, H, PST)
    return jnp.einsum("bghp,gk->bgpkh", v, _EYE8).reshape(4, 512, 128)


def _ct_unblock(m):
    return jnp.einsum("bghkp,gk->bghp", m.reshape(4, 8, H, 8, PST), _EYE8).reshape(G, H, PST)


def _perm_matrix():
    p = np.zeros((TC, TC), np.float32)
    for r in range(R):
        for seg in range(8):
            p[r * 8 + seg, seg * R + r] = 1.0
    return p


def _deinterleave(a):
    L, C = a.shape
    return a.reshape(L // TC, R, 8, C).transpose(0, 2, 1, 3).reshape(L, C)


def _fwd_bwd(h, xi, ti, proj, conv_w, w_co, w_glu, w_so, w_out, small):
    (conv_b, ln_g, ln_b, lam_re, lam_im, log_dt, b_re, b_im, c_re, c_im, dvec, b_glu, g_post) = small
    lam_re, lam_im, log_dt = lam_re[0], lam_im[0], log_dt[0]
    b_re, b_im, c_re, c_im = b_re[0], b_im[0], c_re[0], c_im[0]
    (ar, ai, bbr, bbi), disc_vjp = jax.vjp(_discretize, lam_re, lam_im, log_dt, b_re, b_im)
    a_re = ar.reshape(1, NS)
    a_im = ai.reshape(1, NS)
    dt = jnp.exp(log_dt)[:, None]
    steps = jnp.arange(1, R + 1, dtype=f32)[:, None, None]
    apow_re = (jnp.exp(steps * (lam_re * dt)) * jnp.cos(steps * (lam_im * dt))).reshape(R, NS)
    apow_im = (jnp.exp(steps * (lam_re * dt)) * jnp.sin(steps * (lam_im * dt))).reshape(R, NS)
    bbt_re, bbt_im = _bbt_blocks(bbr).astype(bf16), _bbt_blocks(bbi).astype(bf16)
    ct_re, ct_im = _ct_blocks(c_re).astype(bf16), _ct_blocks(c_im).astype(bf16)
    d_row = dvec.reshape(1, SW)
    cw32 = jnp.pad(conv_w, ((0, 1), (0, 0)))

    cu1, a_in = _conv_fwd(proj, cw32, conv_b, ln_g, ln_b)
    y0, b_in, sre, sim, cinr, cini = _ssm_fwd(proj, bbt_re, bbt_im, ct_re, ct_im, a_re, a_im,
                                              apow_re, apow_im, d_row, w_glu, b_glu)
    gx0, d_ain, d_bin, dproj, dw_out, dw_co, dw_so, dg_post, loss = _tail(
        a_in, b_in, proj, xi, ti, w_co, w_so, w_out, g_post)
    (dproj, dbbt_re, dbbt_im, dct_re, dct_im, dd, dar8, dai8, dw_glu, db_glu) = _ssm_bwd(
        d_bin, y0, proj, sre, sim, cinr, cini, bbt_re, bbt_im, ct_re, ct_im,
        a_re, a_im, apow_re, apow_im, d_row, w_glu, b_glu, dproj)
    dproj, dcw8, d_convb, d_lng, d_lnb = _conv_bwd(d_ain, cu1, proj, cw32, ln_g, ln_b, dproj)
    dw_in = _win_grad(h, dproj)

    d_ar = jnp.sum(dar8, axis=0).reshape(G, PST)
    d_ai = jnp.sum(dai8, axis=0).reshape(G, PST)
    d_lre, d_lim, d_ldt, d_bre, d_bim = disc_vjp((d_ar, d_ai, _bbt_unblock(dbbt_re), _bbt_unblock(dbbt_im)))
    d_conv_w = jnp.sum(dcw8, axis=1)[:KS]
    small_grads = [d_convb, d_lng, d_lnb, d_lre[None], d_lim[None], d_ldt[None], d_bre[None], d_bim[None],
                   _ct_unblock(dct_re)[None], _ct_unblock(dct_im)[None], dd.reshape(1, G, H), db_glu, dg_post]
    return loss[0, 0], gx0, dproj, (dw_in, dw_co, dw_out, dw_glu, dw_so, d_conv_w), small_grads


def kernel(x, pre_norm_gain, w_in, conv_w, conv_b, conv_ln_gain, conv_ln_bias, w_conv_out, ssm_lambda_re, ssm_lambda_im, ssm_log_dt, ssm_b_re, ssm_b_im, ssm_c_re, ssm_c_im, ssm_d, w_ssm_glu, b_ssm_glu, w_ssm_out, w_out, post_norm_gain, loss_target, m_pre_norm_gain, m_w_in, m_conv_w, m_conv_b, m_conv_ln_gain, m_conv_ln_bias, m_w_conv_out, m_ssm_lambda_re, m_ssm_lambda_im, m_ssm_log_dt, m_ssm_b_re, m_ssm_b_im, m_ssm_c_re, m_ssm_c_im, m_ssm_d, m_w_ssm_glu, m_b_ssm_glu, m_w_ssm_out, m_w_out, m_post_norm_gain, v_pre_norm_gain, v_w_in, v_conv_w, v_conv_b, v_conv_ln_gain, v_conv_ln_bias, v_w_conv_out, v_ssm_lambda_re, v_ssm_lambda_im, v_ssm_log_dt, v_ssm_b_re, v_ssm_b_im, v_ssm_c_re, v_ssm_c_im, v_ssm_d, v_w_ssm_glu, v_b_ssm_glu, v_w_ssm_out, v_w_out, v_post_norm_gain):
    c = lax.axis_index("c")
    shards = [w_in[0].astype(bf16), w_conv_out[0].astype(bf16), w_out[0].astype(bf16), w_ssm_glu[0].astype(bf16),
              w_ssm_out[0].astype(bf16), jnp.pad(conv_w[0], ((0, CONV_ROWS - KS), (0, 0)))]
    k_arr = (2 * lax.axis_index("x") + lax.axis_index("y")).astype(jnp.int32).reshape(1)
    w_in_g, w_co_g, w_out_g, w_glu_g, w_so_g, conv_w_g, h, xi, ti, proj = _gather_prep(
        k_arr, shards, x[0], loss_target[0], pre_norm_gain, jnp.asarray(_perm_matrix(), bf16))
    conv_w_f = conv_w_g[:, :KS].transpose(1, 0, 2).reshape(KS, CW)

    small = (conv_b, conv_ln_gain, conv_ln_bias, ssm_lambda_re, ssm_lambda_im, ssm_log_dt, ssm_b_re,
             ssm_b_im, ssm_c_re, ssm_c_im, ssm_d, b_ssm_glu, post_norm_gain)
    loss_part, gx0, dproj, big_grads, small_grads = _fwd_bwd(
        h, xi, ti, _proj_fwd(k_arr, h, w_in_g, proj), conv_w_f, w_co_g.reshape(CW, D), w_glu_g.reshape(SW, SW), w_so_g,
        w_out_g.reshape(D, D), small)

    dw_in, dw_co, dw_out, dw_glu, dw_so, d_conv_w = big_grads
    d_conv_w = jnp.pad(d_conv_w, ((0, CONV_ROWS - KS), (0, 0))).reshape(CONV_ROWS, NCHIP, 256).transpose(1, 0, 2)
    grads = [dw_in] + [g.astype(bf16) for g in (dw_co.reshape(NCHIP, 256, D), dw_out.reshape(NCHIP, 256, D),
                                                  dw_glu.reshape(NCHIP, 128, SW), dw_so, d_conv_w)]
    gs = _pack_small(small_grads, extra=loss_part)
    *recvs, rs = _pair_exchange_list(grads, gs)
    *parts, ps = _pair_sum_list(c.astype(jnp.int32).reshape(1), grads, recvs, gs, rs)
    gxi, dg_pre, *qparts, qs = _x_grad_exchange(dproj, w_in_g, xi, gx0, pre_norm_gain, parts, ps)
    grad_x = _deinterleave(gxi)
    *halves, fs_half = _chip_sum_list(qparts, qs)
    pre_parts, fs = _small_join(dg_pre.reshape(8, 128), fs_half)
    g_big = list(_sibling_join_list(halves))
    g_big[5] = g_big[5][:KS]

    big_w = (w_in[0], w_conv_out[0], w_out[0], w_ssm_glu[0], w_ssm_out[0], conv_w[0])
    big_m = (m_w_in[0], m_w_conv_out[0], m_w_out[0], m_w_ssm_glu[0], m_w_ssm_out[0], m_conv_w[0])
    big_v = (v_w_in[0], v_w_conv_out[0], v_w_out[0], v_w_ssm_glu[0], v_w_ssm_out[0], v_conv_w[0])
    big_names = ("w_in", "w_conv_out", "w_out", "w_ssm_glu", "w_ssm_out", "conv_w")
    res = {}
    upd = [_adamw("adamw_w_in", big_w[0], g_big[0], big_m[0], big_v[0])]
    upd += _adamw_group("adamw_rest", big_w[1:], g_big[1:], big_m[1:], big_v[1:])
    for n, g, (d, m2, v2) in zip(big_names, g_big, upd):
        res[n] = (g[None], d[None], m2[None], v2[None])

    small_m = (m_conv_b, m_conv_ln_gain, m_conv_ln_bias, m_ssm_lambda_re, m_ssm_lambda_im, m_ssm_log_dt,
               m_ssm_b_re, m_ssm_b_im, m_ssm_c_re, m_ssm_c_im, m_ssm_d, m_b_ssm_glu, m_post_norm_gain)
    small_v = (v_conv_b, v_conv_ln_gain, v_conv_ln_bias, v_ssm_lambda_re, v_ssm_lambda_im, v_ssm_log_dt,
               v_ssm_b_re, v_ssm_b_im, v_ssm_c_re, v_ssm_c_im, v_ssm_d, v_b_ssm_glu, v_post_norm_gain)
    sd, sm, sv = _adamw("adamw_small", _pack_small(small), fs, _pack_small(small_m), _pack_small(small_v))
    sg_l, loss = _unpack_small(fs)
    sd_l, _ = _unpack_small(sd)
    sm_l, _ = _unpack_small(sm)
    sv_l, _ = _unpack_small(sv)
    for i, (n, _) in enumerate(_SMALL):
        res[n] = (sg_l[i], sd_l[i], sm_l[i], sv_l[i])
    rows = lambda a: a.reshape(8, 128)
    pre = _adamw_rows(pre_parts, rows(pre_norm_gain), rows(m_pre_norm_gain), rows(v_pre_norm_gain))
    res["pre_norm_gain"] = tuple(a.reshape(1, D) for a in pre)

    order = ("pre_norm_gain", "w_in", "conv_w", "conv_b", "conv_ln_gain", "conv_ln_bias", "w_conv_out", "ssm_lambda_re",
             "ssm_lambda_im", "ssm_log_dt", "ssm_b_re", "ssm_b_im", "ssm_c_re", "ssm_c_im", "ssm_d", "w_ssm_glu",
             "b_ssm_glu", "w_ssm_out", "w_out", "post_norm_gain")
    outs = [loss, grad_x[None]]
    for q in range(4):
        outs.extend(res[n][q] for n in order)
    return tuple(outs)
```

```python
import math

import numpy as np
import jax
import jax.numpy as jnp
from jax import lax
from jax.experimental import pallas as pl
from jax.experimental.pallas import tpu as pltpu

f32 = jnp.float32
bf16 = jnp.bfloat16

D = 1024
CW = 1024
SW = 512
G = 32
H = 16
PST = 64
NS = G * PST
KS = 31
IN_W = 6144
NCHIP = 4
SHARD_W = IN_W // NCHIP
RMS_EPS = 1e-6
LN_EPS = 1e-5
LR, B1, B2, EPS, WD, STEP = 0.001, 0.9, 0.999, 1e-08, 0.01, 10
GELU_K0 = math.sqrt(2.0 / math.pi)
GELU_K1 = 0.044715

TC = 512
R = TC // 8
NH = 32
LBW = 1024
CONV_ROWS = 64
SMALL_ROWS = 1152
VMEM_LIMIT = 56 * 1024 * 1024
MESH = pl.DeviceIdType.MESH


def _cp(*sem):
    return pltpu.CompilerParams(dimension_semantics=tuple(sem), vmem_limit_bytes=VMEM_LIMIT)


def _sig(v):
    return 0.5 * jnp.tanh(0.5 * v) + 0.5


def _dot(a, b):
    return jnp.dot(a, b, preferred_element_type=f32)


def _dot_nt(a, b):
    return lax.dot_general(a, b, (((1,), (1,)), ((), ())), preferred_element_type=f32)


def _dot_tn(a, b):
    return lax.dot_general(a, b, (((0,), (0,)), ((), ())), preferred_element_type=f32)


def _full(shape):
    nd = len(shape)
    return pl.BlockSpec(shape, lambda *_: (0,) * nd)


def _rows8(i):
    return pl.ds(pl.multiple_of(i * 8, 8), 8)


def _proj_fwd(k_arr, h, w_in, proj):
    L = h.shape[0]
    tm = min(1024, L)
    def body(_, h_ref, w_ref, __, o_ref):
        o_ref[...] = _dot(h_ref[...], w_ref[0]).astype(bf16)
    shard = lambda j, k: (k[0] + 1 + j) % NCHIP
    grid_spec = pltpu.PrefetchScalarGridSpec(
        num_scalar_prefetch=1, grid=(NCHIP - 1, L // tm),
        in_specs=[pl.BlockSpec((tm, D), lambda j, i, k: (i, 0)),
                  pl.BlockSpec((1, D, SHARD_W), lambda j, i, k: (shard(j, k), 0, 0)), _ANY],
        out_specs=pl.BlockSpec((tm, SHARD_W), lambda j, i, k: (i, shard(j, k))))
    return pl.pallas_call(
        body, grid_spec=grid_spec, out_shape=jax.ShapeDtypeStruct((L, IN_W), bf16),
        input_output_aliases={3: 0},
        name="proj_fwd", compiler_params=_cp("arbitrary", "arbitrary"))(k_arr, h, w_in, proj)


NLB = CW // 128
RPI = 32


def _put_blocked(buf, row0, nrows, v):
    for lb in range(NLB):
        buf[lb, pl.ds(row0, nrows), :] = v[:, lb * 128:(lb + 1) * 128]


def _get_blocked(buf, row0, nrows):
    return jnp.concatenate([buf[lb, pl.ds(row0, nrows), :] for lb in range(NLB)], axis=1)


def _fill_before(ebuf, prev):
    sub = lax.broadcasted_iota(jnp.int32, (8, 128), 0)
    def halo(p, carry):
        for lb in range(NLB):
            cur = ebuf[lb, _rows8(R + p), :]
            ebuf[lb, _rows8(p), :] = jnp.where(sub == 0, pltpu.roll(prev[lb, _rows8(p), :], 1, 0),
                                               pltpu.roll(cur, 1, 0))
        return carry
    lax.fori_loop(0, NH, halo, 0)


def _fir(buf, lb, r, coef, first, flip):
    win = buf[lb, pl.ds(pl.multiple_of(r * 8, 8), (KS + RPI - 1) * 8), :]
    outs = []
    for i in range(RPI):
        acc = [first, None, None, None]
        for k in range(KS):
            o = i + ((KS - 1 - k) if flip else k)
            t = coef[k] * win[8 * o:8 * o + 8, :]
            acc[k % 4] = t if acc[k % 4] is None else acc[k % 4] + t
        outs.append((acc[0] + acc[1]) + (acc[2] + acc[3]))
    return outs


def _conv_fwd(proj, cw, cbias, lng, lnb):
    L = proj.shape[0]
    nc = L // TC
    def body(ca_ref, cb_ref, zc_ref, w_ref, b_ref, g_ref, bb_ref, cu1_ref, ain_ref, ebuf, prev, cacc):
        @pl.when(pl.program_id(0) == 0)
        def _():
            prev[...] = jnp.zeros_like(prev)
        def glu(s, carry):
            rows = pl.ds(pl.multiple_of(s * 64, 64), 64)
            _put_blocked(ebuf, pl.multiple_of(NH * 8 + s * 64, 64), 64,
                         ca_ref[rows, :].astype(f32) * _sig(cb_ref[rows, :].astype(f32)))
            return carry
        lax.fori_loop(0, TC // 64, glu, 0)
        _fill_before(ebuf, prev)
        prev[...] = ebuf[:, R * 8:(NH + R) * 8, :]
        for lb in range(NLB):
            sl = slice(lb * 128, (lb + 1) * 128)
            wk = [jnp.broadcast_to(w_ref[k:k + 1, sl], (8, 128)) for k in range(KS)]
            bias = jnp.broadcast_to(b_ref[:, sl], (8, 128))
            def tap(q, carry, lb=lb, wk=wk, bias=bias):
                r = q * RPI
                for i, o in enumerate(_fir(ebuf, lb, r + (NH - KS + 1), wk, bias, False)):
                    cacc[lb, _rows8(r + i), :] = o
                return carry
            lax.fori_loop(0, R // RPI, tap, 0)
        def norm(s, carry):
            rows = pl.ds(pl.multiple_of(s * 64, 64), 64)
            c1b = _get_blocked(cacc, pl.multiple_of(s * 64, 64), 64).astype(bf16)
            cu1_ref[rows, :] = c1b
            c1 = c1b.astype(f32)
            xc = c1 - jnp.mean(c1, axis=-1, keepdims=True)
            var = jnp.mean(xc * xc, axis=-1, keepdims=True)
            ln = xc * lax.rsqrt(var + LN_EPS) * g_ref[...] + bb_ref[...]
            zc = zc_ref[rows, :].astype(f32)
            ain_ref[rows, :] = ((ln * _sig(ln)) * (zc * _sig(zc))).astype(bf16)
            return carry
        lax.fori_loop(0, TC // 64, norm, 0, unroll=4)

    col = lambda c: pl.BlockSpec((TC, CW), lambda i, c=c: (i, c))
    return pl.pallas_call(
        body, grid=(nc,),
        in_specs=[col(0), col(1), col(2), _full((32, CW)), _full((1, CW)), _full((1, CW)), _full((1, CW))],
        out_specs=[pl.BlockSpec((TC, CW), lambda i: (i, 0)), pl.BlockSpec((TC, CW), lambda i: (i, 0))],
        out_shape=[jax.ShapeDtypeStruct((L, CW), bf16), jax.ShapeDtypeStruct((L, CW), bf16)],
        scratch_shapes=[pltpu.VMEM((NLB, (NH + R) * 8, 128), f32), pltpu.VMEM((NLB, NH * 8, 128), f32),
                        pltpu.VMEM((NLB, TC, 128), f32)],
        name="conv_fwd", compiler_params=_cp("arbitrary"))(proj, proj, proj, cw, cbias, lng, lnb)


def _gelu_parts(y0):
    t = jnp.tanh(GELU_K0 * (y0 + GELU_K1 * y0 * y0 * y0))
    return t, 0.5 * y0 * (1.0 + t)


def _ssm_fwd(proj, bbt_re, bbt_im, ct_re, ct_im, a_re, a_im, apow_re, apow_im, dvec, wglu, bglu):
    L = proj.shape[0]
    nc = L // TC
    def body(u_ref, zs_ref, bre_ref, bim_ref, cre_ref, cim_ref, are_ref, aim_ref, pwr_ref, pwi_ref,
             d_ref, wg_ref, bg_ref, y0_ref, bin_ref, sre, sim, cinr, cini, prev_re, prev_im):
        c = pl.program_id(0)
        @pl.when(c == 0)
        def _():
            prev_re[...] = jnp.zeros_like(prev_re)
            prev_im[...] = jnp.zeros_like(prev_im)
        u = u_ref[...]
        for blk in range(4):
            ub = u[:, 128 * blk:128 * (blk + 1)]
            sre[:, 512 * blk:512 * (blk + 1)] = _dot(ub, bre_ref[blk])
            sim[:, 512 * blk:512 * (blk + 1)] = _dot(ub, bim_ref[blk])
        for lb in range(NS // LBW):
            sl = slice(lb * LBW, (lb + 1) * LBW)
            ar = jnp.broadcast_to(are_ref[:, sl], (8, LBW))
            ai = jnp.broadcast_to(aim_ref[:, sl], (8, LBW))
            def step(r, carry, sl=sl, ar=ar, ai=ai):
                sr, si = carry
                nr = ar * sr - ai * si + sre[_rows8(r), sl]
                ni = ar * si + ai * sr + sim[_rows8(r), sl]
                sre[_rows8(r), sl] = nr
                sim[_rows8(r), sl] = ni
                return nr, ni
            lax.fori_loop(1, R, step, (sre[0:8, sl], sim[0:8, sl]))
        a_r = pwr_ref[R - 1:R, :]
        a_i = pwi_ref[R - 1:R, :]
        cr = prev_re[0:1, :]
        ci = prev_im[0:1, :]
        for seg in range(8):
            cinr[seg:seg + 1, :] = cr
            cini[seg:seg + 1, :] = ci
            er = sre[8 * (R - 1) + seg:8 * (R - 1) + seg + 1, :]
            ei = sim[8 * (R - 1) + seg:8 * (R - 1) + seg + 1, :]
            cr, ci = er + a_r * cr - a_i * ci, ei + a_r * ci + a_i * cr
        prev_re[0:1, :] = cr
        prev_im[0:1, :] = ci
        for lb in range(NS // LBW):
            sl = slice(lb * LBW, (lb + 1) * LBW)
            kr = cinr[:, sl]
            ki = cini[:, sl]
            def fix(r, carry, sl=sl, kr=kr, ki=ki):
                pr = jnp.broadcast_to(pwr_ref[pl.ds(r, 1), sl], (8, LBW))
                pi = jnp.broadcast_to(pwi_ref[pl.ds(r, 1), sl], (8, LBW))
                sre[_rows8(r), sl] = sre[_rows8(r), sl] + pr * kr - pi * ki
                sim[_rows8(r), sl] = sim[_rows8(r), sl] + pr * ki + pi * kr
                return carry
            lax.fori_loop(0, R, fix, 0, unroll=4)
        yp = []
        for blk in range(4):
            sr = sre[:, 512 * blk:512 * (blk + 1)].astype(bf16)
            si = sim[:, 512 * blk:512 * (blk + 1)].astype(bf16)
            yp.append(_dot(sr, cre_ref[blk]) - _dot(si, cim_ref[blk]))
        y0 = jnp.concatenate(yp, axis=1) + d_ref[...] * u.astype(f32)
        y0_ref[...] = y0
        _, y1 = _gelu_parts(y0)
        glu = _dot(y1.astype(bf16), wg_ref[...]) + bg_ref[...]
        y2 = y1 * _sig(glu)
        zs = zs_ref[...].astype(f32)
        bin_ref[...] = (y2 * (zs * _sig(zs))).astype(bf16)

    return pl.pallas_call(
        body, grid=(nc,),
        in_specs=[pl.BlockSpec((TC, SW), lambda c: (c, 6)), pl.BlockSpec((TC, SW), lambda c: (c, 7)),
                  _full((4, 128, 512)), _full((4, 128, 512)), _full((4, 512, 128)), _full((4, 512, 128)),
                  _full((1, NS)), _full((1, NS)), _full((R, NS)), _full((R, NS)),
                  _full((1, SW)), _full((SW, SW)), _full((1, SW))],
        out_specs=[pl.BlockSpec((TC, SW), lambda c: (c, 0)), pl.BlockSpec((TC, SW), lambda c: (c, 0)),
                   pl.BlockSpec((TC, NS), lambda c: (c, 0)), pl.BlockSpec((TC, NS), lambda c: (c, 0)),
                   pl.BlockSpec((8, NS), lambda c: (c, 0)), pl.BlockSpec((8, NS), lambda c: (c, 0))],
        out_shape=[jax.ShapeDtypeStruct((L, SW), f32), jax.ShapeDtypeStruct((L, SW), bf16),
                   jax.ShapeDtypeStruct((L, NS), f32), jax.ShapeDtypeStruct((L, NS), f32),
                   jax.ShapeDtypeStruct((nc * 8, NS), f32), jax.ShapeDtypeStruct((nc * 8, NS), f32)],
        scratch_shapes=[pltpu.VMEM((8, NS), f32), pltpu.VMEM((8, NS), f32)],
        name="ssm_fwd", compiler_params=_cp("arbitrary"))(
            proj, proj, bbt_re, bbt_im, ct_re, ct_im, a_re, a_im, apow_re, apow_im, dvec, wglu, bglu)


def _tail(a_in, b_in, proj, x, tgt, wco, wso, wout, gpost):
    L = x.shape[0]
    tm = 512
    def body(a_ref, b_ref, gc_ref, gs_ref, x_ref, t_ref, wco_ref, wso_ref, wout_ref, gp_ref,
             gx_ref, dain_ref, dbin_ref, dp_ref, dwout_ref, dwco_ref, dwso_ref, dgp_ref, loss_ref):
        @pl.when(pl.program_id(0) == 0)
        def _():
            dwout_ref[...] = jnp.zeros_like(dwout_ref)
            dwco_ref[...] = jnp.zeros_like(dwco_ref)
            dwso_ref[...] = jnp.zeros_like(dwso_ref)
            dgp_ref[...] = jnp.zeros_like(dgp_ref)
            loss_ref[...] = jnp.zeros_like(loss_ref)
        a = a_ref[...]
        b = b_ref[...]
        co = _dot(a, wco_ref[...])
        so = jnp.concatenate([_dot(b, wso_ref[j]) for j in range(NCHIP)], axis=1)
        sc = _sig(gc_ref[...].astype(f32))
        ss = _sig(gs_ref[...].astype(f32))
        mb = (sc * co + ss * so).astype(bf16)
        out = _dot(mb, wout_ref[...])
        gp = gp_ref[...]
        douts = []
        for half in range(2):
            rows = slice(half * (tm // 2), (half + 1) * (tm // 2))
            oh = out[rows, :]
            r2 = lax.rsqrt(jnp.mean(oh * oh, axis=-1, keepdims=True) + RMS_EPS)
            on = oh * r2
            e = x_ref[rows, :] + on * gp - t_ref[rows, :]
            loss_ref[...] += (0.5 / D) * jnp.sum(e * e)
            dy = e * (1.0 / D)
            gx_ref[rows, :] = dy
            dgp_ref[...] += jnp.sum(dy * on, axis=0, keepdims=True)
            dn = dy * gp
            douts.append((r2 * (dn - on * jnp.mean(dn * on, axis=-1, keepdims=True))).astype(bf16))
        dout = jnp.concatenate(douts, axis=0)
        dwout_ref[...] += _dot_tn(mb, dout)
        dm = _dot_nt(dout, wout_ref[...])
        dp_ref[:, 0:D] = (dm * co * sc * (1.0 - sc)).astype(bf16)
        dp_ref[:, D:2 * D] = (dm * so * ss * (1.0 - ss)).astype(bf16)
        dco = (dm * sc).astype(bf16)
        dso = (dm * ss).astype(bf16)
        dwco_ref[...] += _dot_tn(a, dco)
        dbin = None
        for j in range(NCHIP):
            dso_j = dso[:, j * 256:(j + 1) * 256]
            dwso_ref[j] += _dot_tn(b, dso_j)
            t = _dot_nt(dso_j, wso_ref[j])
            dbin = t if dbin is None else dbin + t
        dain_ref[...] = _dot_nt(dco, wco_ref[...]).astype(bf16)
        dbin_ref[...] = dbin.astype(bf16)

    row = lambda w: pl.BlockSpec((tm, w), lambda i: (i, 0))
    one = lambda shape: pl.BlockSpec(shape, lambda i: (0,) * len(shape), pipeline_mode=pl.Buffered(1))
    return pl.pallas_call(
        body, grid=(L // tm,),
        in_specs=[row(CW), row(SW), pl.BlockSpec((tm, D), lambda i: (i, 4)), pl.BlockSpec((tm, D), lambda i: (i, 5)),
                  row(D), row(D), one((CW, D)), one((NCHIP, SW, 256)), one((D, D)), one((1, D))],
        out_specs=[row(D), row(CW), row(SW), pl.BlockSpec((tm, 2 * D), lambda i: (i, 2)),
                   one((D, D)), one((CW, D)), one((NCHIP, SW, 256)), one((1, D)), one((1, 128))],
        out_shape=[jax.ShapeDtypeStruct((L, D), f32), jax.ShapeDtypeStruct((L, CW), bf16),
                   jax.ShapeDtypeStruct((L, SW), bf16), jax.ShapeDtypeStruct((L, IN_W), bf16),
                   jax.ShapeDtypeStruct((D, D), f32), jax.ShapeDtypeStruct((CW, D), f32),
                   jax.ShapeDtypeStruct((NCHIP, SW, 256), f32), jax.ShapeDtypeStruct((1, D), f32),
                   jax.ShapeDtypeStruct((1, 128), f32)],
        name="tail", compiler_params=_cp("arbitrary"))(a_in, b_in, proj, proj, x, tgt, wco, wso, wout, gpost)


def _ssm_bwd(d_bin, y0, proj, sre, sim, cinr, cini, bbt_re, bbt_im, ct_re, ct_im,
             a_re, a_im, apow_re, apow_im, dvec, wglu, bglu, dproj):
    L = y0.shape[0]
    nc = L // TC
    def body(dbin_ref, y0_ref, u_ref, zs_ref, sre_ref, sim_ref, cinr_ref, cini_ref,
             bre_ref, bim_ref, cre_ref, cim_ref, are_ref, aim_ref, pwr_ref, pwi_ref, d_ref, wg_ref, bg_ref, _,
             dp_ref, dbre_ref, dbim_ref, dcre_ref, dcim_ref, dd_ref, dar_ref, dai_ref, dwg_ref, dbg_ref,
             gre, gim, gcr, gci, nxt_re, nxt_im):
        @pl.when(pl.program_id(0) == 0)
        def _():
            for ref in (dbre_ref, dbim_ref, dcre_ref, dcim_ref, dd_ref, dar_ref, dai_ref, dwg_ref, dbg_ref,
                        nxt_re, nxt_im):
                ref[...] = jnp.zeros_like(ref)
        y0 = y0_ref[...]
        u = u_ref[...]
        zs = zs_ref[...].astype(f32)
        dbin = dbin_ref[...].astype(f32)
        t, y1 = _gelu_parts(y0)
        y1b = y1.astype(bf16)
        sg = _sig(_dot(y1b, wg_ref[...]) + bg_ref[...])
        sz = _sig(zs)
        d_y2 = dbin * (zs * sz)
        dp_ref[:, SW:2 * SW] = (dbin * (y1 * sg) * (sz * (1.0 + zs * (1.0 - sz)))).astype(bf16)
        d_glu = d_y2 * y1 * sg * (1.0 - sg)
        d_glub = d_glu.astype(bf16)
        d_y1 = d_y2 * sg + _dot_nt(d_glub, wg_ref[...])
        dwg_ref[...] += _dot_tn(y1b, d_glub)
        dbg_ref[...] += jnp.sum(d_glu, axis=0, keepdims=True)
        dgelu = 0.5 * (1.0 + t) + 0.5 * y0 * (1.0 - t * t) * GELU_K0 * (1.0 + 3.0 * GELU_K1 * y0 * y0)
        d_y0 = d_y1 * dgelu
        dd_ref[...] += jnp.sum(d_y0 * u.astype(f32), axis=0, keepdims=True)
        dyb = d_y0.astype(bf16)
        for blk in range(4):
            dy1 = dyb[:, 128 * blk:128 * (blk + 1)]
            gre[:, 512 * blk:512 * (blk + 1)] = _dot_nt(dy1, cre_ref[blk])
            gim[:, 512 * blk:512 * (blk + 1)] = -_dot_nt(dy1, cim_ref[blk])
        for lb in range(NS // LBW):
            sl = slice(lb * LBW, (lb + 1) * LBW)
            ar = jnp.broadcast_to(are_ref[:, sl], (8, LBW))
            ai = jnp.broadcast_to(aim_ref[:, sl], (8, LBW))
            def step(k, carry, sl=sl, ar=ar, ai=ai):
                gr, gi = carry
                row = _rows8(R - 2 - k)
                nr = ar * gr + ai * gi + gre[row, sl]
                ni = ar * gi - ai * gr + gim[row, sl]
                gre[row, sl] = nr
                gim[row, sl] = ni
                return nr, ni
            lax.fori_loop(0, R - 1, step, (gre[8 * (R - 1):8 * R, sl], gim[8 * (R - 1):8 * R, sl]))
        a_r = pwr_ref[R - 1:R, :]
        a_i = pwi_ref[R - 1:R, :]
        cr = nxt_re[0:1, :]
        ci = nxt_im[0:1, :]
        for seg in range(7, -1, -1):
            gcr[seg:seg + 1, :] = cr
            gci[seg:seg + 1, :] = ci
            er = gre[seg:seg + 1, :]
            ei = gim[seg:seg + 1, :]
            cr, ci = er + a_r * cr + a_i * ci, ei + a_r * ci - a_i * cr
        nxt_re[0:1, :] = cr
        nxt_im[0:1, :] = ci
        for lb in range(NS // LBW):
            sl = slice(lb * LBW, (lb + 1) * LBW)
            kr = gcr[:, sl]
            ki = gci[:, sl]
            def fixed(rows, prow, sl=sl, kr=kr, ki=ki):
                pr = jnp.broadcast_to(pwr_ref[prow, sl], (8, LBW))
                pi = jnp.broadcast_to(pwi_ref[prow, sl], (8, LBW))
                gr = gre[rows, sl] + pr * kr + pi * ki
                gi = gim[rows, sl] + pr * ki - pi * kr
                gre[rows, sl] = gr
                gim[rows, sl] = gi
                return gr, gi
            g0r, g0i = fixed(slice(0, 8), slice(R - 1, R))
            p0r, p0i = cinr_ref[:, sl], cini_ref[:, sl]
            acc0 = (g0r * p0r + g0i * p0i, g0i * p0r - g0r * p0i)
            def dacc(r, carry, sl=sl, fixed=fixed):
                xr, xi = carry
                gr, gi = fixed(_rows8(r), pl.ds(R - 1 - r, 1))
                pr, pi = sre_ref[_rows8(r - 1), sl], sim_ref[_rows8(r - 1), sl]
                return xr + gr * pr + gi * pi, xi + gi * pr - gr * pi
            xr, xi = lax.fori_loop(1, R, dacc, acc0)
            dar_ref[:, sl] += xr
            dai_ref[:, sl] += xi
        dup = []
        for blk in range(4):
            s4 = slice(512 * blk, 512 * (blk + 1))
            s1 = slice(128 * blk, 128 * (blk + 1))
            grb = gre[:, s4].astype(bf16)
            gib = gim[:, s4].astype(bf16)
            dup.append(_dot_nt(grb, bre_ref[blk]) + _dot_nt(gib, bim_ref[blk]))
            dbre_ref[blk] += _dot_tn(u[:, s1], grb)
            dbim_ref[blk] += _dot_tn(u[:, s1], gib)
            dcre_ref[blk] += _dot_tn(dyb[:, s1], sre_ref[:, s4].astype(bf16))
            dcim_ref[blk] -= _dot_tn(dyb[:, s1], sim_ref[:, s4].astype(bf16))
        dp_ref[:, 0:SW] = (jnp.concatenate(dup, axis=1) + d_ref[...] * d_y0).astype(bf16)

    rev = lambda w, cidx: pl.BlockSpec((TC, w), lambda i, cidx=cidx: (nc - 1 - i, cidx))
    one = lambda shape: pl.BlockSpec(shape, lambda i: (0,) * len(shape))
    return pl.pallas_call(
        body, grid=(nc,),
        in_specs=[rev(SW, 0), rev(SW, 0), rev(SW, 6), rev(SW, 7), rev(NS, 0), rev(NS, 0),
                  pl.BlockSpec((8, NS), lambda i: (nc - 1 - i, 0)), pl.BlockSpec((8, NS), lambda i: (nc - 1 - i, 0)),
                  one((4, 128, 512)), one((4, 128, 512)), one((4, 512, 128)), one((4, 512, 128)),
                  one((1, NS)), one((1, NS)), one((R, NS)), one((R, NS)),
                  one((1, SW)), one((SW, SW)), one((1, SW)), _ANY],
        out_specs=[pl.BlockSpec((TC, 2 * SW), lambda i: (nc - 1 - i, 3)),
                   one((4, 128, 512)), one((4, 128, 512)), one((4, 128, 512)), one((4, 128, 512)),
                   one((1, SW)), one((8, NS)), one((8, NS)), one((SW, SW)), one((1, SW))],
        out_shape=[jax.ShapeDtypeStruct((L, IN_W), bf16),
                   jax.ShapeDtypeStruct((4, 128, 512), f32), jax.ShapeDtypeStruct((4, 128, 512), f32),
                   jax.ShapeDtypeStruct((4, 128, 512), f32), jax.ShapeDtypeStruct((4, 128, 512), f32),
                   jax.ShapeDtypeStruct((1, SW), f32), jax.ShapeDtypeStruct((8, NS), f32),
                   jax.ShapeDtypeStruct((8, NS), f32), jax.ShapeDtypeStruct((SW, SW), f32),
                   jax.ShapeDtypeStruct((1, SW), f32)],
        scratch_shapes=[pltpu.VMEM((TC, NS), f32), pltpu.VMEM((TC, NS), f32), pltpu.VMEM((8, NS), f32),
                        pltpu.VMEM((8, NS), f32), pltpu.VMEM((8, NS), f32), pltpu.VMEM((8, NS), f32)],
        input_output_aliases={19: 0},
        name="ssm_bwd", compiler_params=_cp("arbitrary"))(
            d_bin, y0, proj, proj, sre, sim, cinr, cini, bbt_re, bbt_im, ct_re, ct_im,
            a_re, a_im, apow_re, apow_im, dvec, wglu, bglu, dproj)


def _conv_bwd(d_ain, cu1, proj, cw, lng, lnb, dproj):
    L = cu1.shape[0]
    nc = L // TC
    def body(dain_ref, cu1_ref, ca_ref, cb_ref, zc_ref, cah_ref, cbh_ref, w_ref, g_ref, bb_ref, _,
             dp_ref, dw_ref, dbias_ref, dlng_ref, dlnb_ref, dbuf, ebuf, prev, nxt, dcu0):
        i = pl.program_id(0)
        @pl.when(i == 0)
        def _():
            dw_ref[...] = jnp.zeros_like(dw_ref)
            dbias_ref[...] = jnp.zeros_like(dbias_ref)
            dlng_ref[...] = jnp.zeros_like(dlng_ref)
            dlnb_ref[...] = jnp.zeros_like(dlnb_ref)
            nxt[...] = jnp.zeros_like(nxt)
        def lnb(s, carry):
            rows = pl.ds(pl.multiple_of(s * 32, 32), 32)
            dain = dain_ref[rows, :].astype(f32)
            c1 = cu1_ref[rows, :].astype(f32)
            zc = zc_ref[rows, :].astype(f32)
            xc = c1 - jnp.mean(c1, axis=-1, keepdims=True)
            var = jnp.mean(xc * xc, axis=-1, keepdims=True)
            rstd = lax.rsqrt(var + LN_EPS)
            xh = xc * rstd
            ln = xh * g_ref[...] + bb_ref[...]
            sl_ = _sig(ln)
            sz = _sig(zc)
            dp_ref[rows, 2 * CW:3 * CW] = (dain * (ln * sl_) * (sz * (1.0 + zc * (1.0 - sz)))).astype(bf16)
            d_ln = dain * (zc * sz) * (sl_ * (1.0 + ln * (1.0 - sl_)))
            dlng_ref[...] += jnp.sum(d_ln * xh, axis=0, keepdims=True)
            dlnb_ref[...] += jnp.sum(d_ln, axis=0, keepdims=True)
            dxh = d_ln * g_ref[...]
            d_c1 = rstd * (dxh - jnp.mean(dxh, axis=-1, keepdims=True)
                           - xh * jnp.mean(dxh * xh, axis=-1, keepdims=True))
            dbias_ref[...] += jnp.sum(d_c1, axis=0, keepdims=True)
            _put_blocked(dbuf, pl.multiple_of(s * 32, 32), 32, d_c1)
            _put_blocked(ebuf, pl.multiple_of(NH * 8 + s * 32, 32), 32,
                         ca_ref[rows, :].astype(f32) * _sig(cb_ref[rows, :].astype(f32)))
            return carry
        lax.fori_loop(0, TC // 32, lnb, 0, unroll=8)
        sub = lax.broadcasted_iota(jnp.int32, (8, 128), 0)
        def after(p, carry):
            for lb in range(NLB):
                cur = dbuf[lb, _rows8(p), :]
                dbuf[lb, _rows8(R + p), :] = jnp.where(sub == 7, pltpu.roll(nxt[lb, _rows8(p), :], 7, 0),
                                                       pltpu.roll(cur, 7, 0))
            return carry
        lax.fori_loop(0, NH, after, 0)
        nxt[...] = dbuf[:, 0:NH * 8, :]
        def before(s, carry):
            rows = pl.ds(pl.multiple_of(s * 64, 64), 64)
            v = cah_ref[rows, :].astype(f32) * _sig(cbh_ref[rows, :].astype(f32))
            _put_blocked(prev, pl.multiple_of(s * 64, 64), 64, jnp.where(i == nc - 1, jnp.zeros_like(v), v))
            return carry
        lax.fori_loop(0, NH * 8 // 64, before, 0)
        _fill_before(ebuf, prev)
        for lb in range(NLB):
            sl = slice(lb * 128, (lb + 1) * 128)
            wk = [jnp.broadcast_to(w_ref[k:k + 1, sl], (8, 128)) for k in range(KS)]
            def tap(q, carry, lb=lb, wk=wk):
                r = q * RPI
                for j, o in enumerate(_fir(dbuf, lb, r, wk, None, True)):
                    dcu0[lb, _rows8(r + j), :] = o
                return carry
            lax.fori_loop(0, R // RPI, tap, 0)
            def wgrad(q, accs, lb=lb):
                r = q * RPI
                dvs = dbuf[lb, pl.ds(pl.multiple_of(r * 8, 8), RPI * 8), :]
                win = ebuf[lb, pl.ds(pl.multiple_of((r + (NH - KS + 1)) * 8, 8), (KS + RPI - 1) * 8), :]
                accs = list(accs)
                for j in range(RPI):
                    dv = dvs[8 * j:8 * j + 8, :]
                    for k in range(KS):
                        accs[k] = accs[k] + dv * win[8 * (j + k):8 * (j + k) + 8, :]
                return tuple(accs)
            accs = lax.fori_loop(0, R // RPI, wgrad, tuple(jnp.zeros((8, 128), f32) for _ in range(KS)))
            for k in range(KS):
                dw_ref[k, :, sl] += accs[k]
        def glub(s, carry):
            rows = pl.ds(pl.multiple_of(s * 64, 64), 64)
            d0 = _get_blocked(dcu0, pl.multiple_of(s * 64, 64), 64)
            ca = ca_ref[rows, :].astype(f32)
            sb = _sig(cb_ref[rows, :].astype(f32))
            dp_ref[rows, 0:CW] = (d0 * sb).astype(bf16)
            dp_ref[rows, CW:2 * CW] = (d0 * ca * sb * (1.0 - sb)).astype(bf16)
            return carry
        lax.fori_loop(0, TC // 64, glub, 0)

    hrows = NH * 8
    per = TC // hrows
    rev = lambda cidx: pl.BlockSpec((TC, CW), lambda i, cidx=cidx: (nc - 1 - i, cidx))
    halo = lambda cidx: pl.BlockSpec((hrows, CW), lambda i, cidx=cidx: (jnp.maximum((nc - 1 - i) * per - 1, 0), cidx))
    one = lambda shape: pl.BlockSpec(shape, lambda i: (0,) * len(shape))
    return pl.pallas_call(
        body, grid=(nc,),
        in_specs=[rev(0), rev(0), rev(0), rev(1), rev(2), halo(0), halo(1), one((32, CW)), one((1, CW)), one((1, CW)),
                  _ANY],
        out_specs=[pl.BlockSpec((TC, 3 * CW), lambda i: (nc - 1 - i, 0)), one((32, 8, CW)), one((1, CW)), one((1, CW)), one((1, CW))],
        out_shape=[jax.ShapeDtypeStruct((L, IN_W), bf16), jax.ShapeDtypeStruct((32, 8, CW), f32),
                   jax.ShapeDtypeStruct((1, CW), f32), jax.ShapeDtypeStruct((1, CW), f32),
                   jax.ShapeDtypeStruct((1, CW), f32)],
        scratch_shapes=[pltpu.VMEM((NLB, (R + NH) * 8, 128), f32), pltpu.VMEM((NLB, (NH + R) * 8, 128), f32),
                        pltpu.VMEM((NLB, hrows, 128), f32), pltpu.VMEM((NLB, hrows, 128), f32),
                        pltpu.VMEM((NLB, TC, 128), f32)],
        input_output_aliases={10: 0},
        name="conv_bwd", compiler_params=_cp("arbitrary"))(d_ain, cu1, proj, proj, proj, proj, proj, cw, lng, lnb, dproj)


def _win_grad(h, dproj):
    L = h.shape[0]
    tm = min(2048, L)
    nt = L // tm
    def body(h_ref, d_ref, o_ref, acc):
        i = pl.program_id(1)
        @pl.when(i == 0)
        def _():
            acc[...] = jnp.zeros_like(acc)
        acc[...] += _dot_tn(h_ref[...], d_ref[...])
        @pl.when(i == nt - 1)
        def _():
            o_ref[0] = acc[...].astype(bf16)
    return pl.pallas_call(
        body, grid=(NCHIP, nt),
        in_specs=[pl.BlockSpec((tm, D), lambda j, i: (i, 0)), pl.BlockSpec((tm, SHARD_W), lambda j, i: (i, j))],
        out_specs=pl.BlockSpec((1, D, SHARD_W), lambda j, i: (j, 0, 0)),
        out_shape=jax.ShapeDtypeStruct((NCHIP, D, SHARD_W), bf16),
        scratch_shapes=[pltpu.VMEM((D, SHARD_W), f32)],
        name="win_grad", compiler_params=_cp("arbitrary", "arbitrary"))(h, dproj)


def _adamw_math(w, g, m, v):
    m2 = B1 * m + (1.0 - B1) * g
    v2 = B2 * v + (1.0 - B2) * (g * g)
    m_hat = m2 / (1.0 - B1 ** STEP)
    v_hat = v2 / (1.0 - B2 ** STEP)
    delta = -LR * (m_hat / (jnp.sqrt(v_hat) + EPS) + WD * w)
    return delta, m2, v2


def _adamw(name, w, g, m, v):
    rows, cols = w.shape
    tm = rows if rows <= 256 else (256 if rows % 256 == 0 else 128)
    assert rows % tm == 0
    def body(w_ref, g_ref, m_ref, v_ref, d_ref, m2_ref, v2_ref):
        d, m2, v2 = _adamw_math(w_ref[...], g_ref[...], m_ref[...], v_ref[...])
        d_ref[...] = d
        m2_ref[...] = m2
        v2_ref[...] = v2
    spec = pl.BlockSpec((tm, cols), lambda i: (i, 0))
    shp = jax.ShapeDtypeStruct((rows, cols), f32)
    return pl.pallas_call(
        body, grid=(rows // tm,), in_specs=[spec] * 4, out_specs=[spec] * 3, out_shape=[shp] * 3,
        name=name, compiler_params=_cp("arbitrary"))(w, g, m, v)


def _adamw_group(name, ws, gs, ms, vs):
    n = len(ws)
    def body(*refs):
        for i in range(n):
            w_ref, g_ref, m_ref, v_ref = (refs[q * n + i] for q in range(4))
            d, m2, v2 = _adamw_math(w_ref[...], g_ref[...], m_ref[...], v_ref[...])
            for q, val in enumerate((d, m2, v2)):
                refs[(4 + q) * n + i][...] = val
    shapes = [jax.ShapeDtypeStruct(w.shape, f32) for w in ws]
    out = pl.pallas_call(body, out_shape=shapes * 3, name=name,
                         compiler_params=pltpu.CompilerParams(vmem_limit_bytes=VMEM_LIMIT))(*ws, *gs, *ms, *vs)
    return [(out[i], out[n + i], out[2 * n + i]) for i in range(n)]


_ANY = pl.BlockSpec(memory_space=pl.ANY)


def _chunks(rows, parts):
    step = rows // parts
    assert step * parts == rows and step % 16 == 0
    return [(i * step, step) for i in range(parts)]


def _place():
    x, y, c = lax.axis_index("x"), lax.axis_index("y"), lax.axis_index("c")
    chips = [(1 - x, y), (x, 1 - y), (1 - x, 1 - y)]
    return x, y, c, chips


def _nchunks(half, cols, itemsize):
    return 4 if half * cols * itemsize >= (1 << 20) else 1


def _segments(metas):
    segs = []
    for w, (half, cols, dt) in enumerate(metas):
        for r0, n in _chunks(half, _nchunks(half, cols, jnp.dtype(dt).itemsize)):
            segs.append((w, half, r0, n))
    return segs


def _rcopy(i, src, dst, send_sems, recv_sems, to):
    return pltpu.make_async_remote_copy(src_ref=src, dst_ref=dst, send_sem=send_sems.at[i], recv_sem=recv_sems.at[i],
                                        device_id=to, device_id_type=MESH)


def _gather_prep(k_arr, shards, x, tgt, g_pre, perm):
    na = len(shards)
    L = x.shape[0]
    nc = L // TC
    segs = _segments([(a.shape[0] // 2, a.shape[1], a.dtype) for a in shards])
    ns = len(segs)
    def body(_, *refs):
        ins = refs[:na]
        x_ref, t_ref, g_ref, p_ref = refs[na:na + 4]
        outs = refs[na + 4:2 * na + 4]
        h_ref, xi_ref, ti_ref, proj_ref = refs[2 * na + 4:2 * na + 8]
        stages = refs[2 * na + 8:3 * na + 8]
        send_sems, recv_sems, local_sems = refs[3 * na + 8:]
        i = pl.program_id(0)
        x, y, c, chips = _place()
        k = 2 * x + y
        me, sibling = (x, y, c), (x, y, 1 - c)

        def dst(w, half, chip, pc, r0, n):
            return outs[w].at[chip, pl.ds(pc * half + r0, n), :]

        def firsts():
            return [_rcopy(j * ns + s, ins[w].at[pl.ds(c * half + r0, n), :], dst(w, half, k, c, r0, n),
                           send_sems, recv_sems, (*chip, c))
                    for j, chip in enumerate(chips) for s, (w, half, r0, n) in enumerate(segs)]

        def own_out(w):
            return pltpu.make_async_copy(stages[w], outs[w].at[k], local_sems.at[w])

        @pl.when(i == 0)
        def _():
            for cp in firsts():
                cp.start()
            cins = [pltpu.make_async_copy(ins[w], stages[w], local_sems.at[w]) for w in range(na)]
            for cp in cins:
                cp.start()
            for w in range(na):
                cins[w].wait()
                own_out(w).start()

        p = p_ref[...]
        def through(v):
            hi = v.astype(bf16)
            r1 = v - hi.astype(f32)
            mid = r1.astype(bf16)
            lo = (r1 - mid.astype(f32)).astype(bf16)
            return (_dot(p, hi) + _dot(p, mid)) + _dot(p, lo)
        xt = x_ref[...]
        r = lax.rsqrt(jnp.mean(xt * xt, axis=-1, keepdims=True) + RMS_EPS)
        hp = _dot(p, (xt * r * g_ref[...]).astype(bf16)).astype(bf16)
        h_ref[...] = hp
        proj_ref[...] = _dot(hp, stages[0][...]).astype(bf16)
        xi_ref[...] = through(xt)
        ti_ref[...] = through(t_ref[...])

        def forwards():
            out = []
            for j, chip in enumerate(chips):
                cj = 2 * chip[0] + chip[1]
                for s, (w, half, r0, n) in enumerate(segs):
                    landed = dst(w, half, cj, c, r0, n)
                    out.append(_rcopy(3 * ns + j * ns + s, landed, landed, send_sems, recv_sems, sibling))
            return out

        @pl.when(i == max(nc - 2, 0))
        def _():
            for cp, fwd in zip(firsts(), forwards()):
                cp.wait_recv()
                fwd.start()

        @pl.when(i == nc - 1)
        def _():
            passed = forwards()
            for j, chip in enumerate(chips):
                cj = 2 * chip[0] + chip[1]
                for s, (w, half, r0, n) in enumerate(segs):
                    theirs = dst(w, half, cj, 1 - c, r0, n)
                    _rcopy(3 * ns + j * ns + s, theirs, theirs, send_sems, recv_sems, me).wait_recv()
            for cp in firsts() + passed:
                cp.wait_send()
            for w in range(na):
                own_out(w).wait()

    row = lambda: pl.BlockSpec((TC, D), lambda i, k: (i, 0))
    grid_spec = pltpu.PrefetchScalarGridSpec(
        num_scalar_prefetch=1, grid=(nc,),
        in_specs=[_ANY] * na + [row(), row(), pl.BlockSpec((1, D), lambda i, k: (0, 0)),
                                pl.BlockSpec((TC, TC), lambda i, k: (0, 0))],
        out_specs=[_ANY] * na + [row(), row(), row(), pl.BlockSpec((TC, SHARD_W), lambda i, k: (i, k[0]))],
        scratch_shapes=[pltpu.VMEM(a.shape, a.dtype) for a in shards]
        + [pltpu.SemaphoreType.DMA((6 * ns,)), pltpu.SemaphoreType.DMA((6 * ns,)), pltpu.SemaphoreType.DMA((na,))])
    return pl.pallas_call(
        body, grid_spec=grid_spec,
        out_shape=[jax.ShapeDtypeStruct((NCHIP,) + a.shape, a.dtype) for a in shards]
        + [jax.ShapeDtypeStruct((L, D), bf16), jax.ShapeDtypeStruct((L, D), f32), jax.ShapeDtypeStruct((L, D), f32),
           jax.ShapeDtypeStruct((L, IN_W), bf16)],
        name="gather_prep", compiler_params=_cp("arbitrary"))(k_arr, *shards, x, tgt, g_pre, perm)


def _x_grad_exchange(dproj, w_in, x, gx0, g_pre, parts, small):
    L = x.shape[0]
    tm = 512
    nt = L // tm
    na = len(parts)
    hs = SMALL_ROWS // 2
    segs = _segments([(p.shape[1], p.shape[2], p.dtype) for p in parts])
    ns = len(segs) + 1
    def body(*refs):
        d_ref, w_ref, x_ref, gx_ref, g_ref = refs[:5]
        ins, s_ref = refs[5:5 + na], refs[5 + na]
        o_ref, dg_ref = refs[6 + na:8 + na]
        outs, qs_ref = refs[8 + na:8 + 2 * na], refs[8 + 2 * na]
        stages = refs[9 + 2 * na:10 + 3 * na]
        send_sems, recv_sems, local_sems = refs[10 + 3 * na:]
        i = pl.program_id(0)
        x, y, c, chips = _place()
        k = 2 * x + y

        def my_small():
            return s_ref.at[pl.ds(c * hs, hs), :]

        def copies():
            out = []
            for j, chip in enumerate(chips):
                cj = 2 * chip[0] + chip[1]
                pieces = [(my_small(), qs_ref.at[k])]
                pieces += [(ins[w].at[cj, pl.ds(r0, n), :], outs[w].at[k, pl.ds(r0, n), :]) for w, _, r0, n in segs]
                out += [_rcopy(ns * j + s, src, d, send_sems, recv_sems, (*chip, c)) for s, (src, d) in enumerate(pieces)]
            return out

        def own_out(w):
            dst = qs_ref.at[k] if w == na else outs[w].at[k]
            return pltpu.make_async_copy(stages[w], dst, local_sems.at[w])

        @pl.when(i == 0)
        def _():
            dg_ref[...] = jnp.zeros_like(dg_ref)
            for cp in copies():
                cp.start()
            cins = [pltpu.make_async_copy(my_small() if w == na else ins[w].at[k], stages[w], local_sems.at[w])
                    for w in range(na + 1)]
            for cp in cins:
                cp.start()
            for w in range(na + 1):
                cins[w].wait()
                own_out(w).start()

        dh = _dot_nt(d_ref[:, 0:SHARD_W], w_ref[0])
        for j in range(1, NCHIP):
            dh = dh + _dot_nt(d_ref[:, j * SHARD_W:(j + 1) * SHARD_W], w_ref[j])
        xt = x_ref[...]
        r = lax.rsqrt(jnp.mean(xt * xt, axis=-1, keepdims=True) + RMS_EPS)
        xn = xt * r
        dg_ref[...] += jnp.sum(dh * xn, axis=0, keepdims=True)
        dxn = dh * g_ref[...]
        o_ref[...] = gx_ref[...] + r * (dxn - xn * jnp.mean(dxn * xn, axis=-1, keepdims=True))

        @pl.when(i == nt - 1)
        def _():
            for cp in copies():
                cp.wait_recv()
            for cp in copies():
                cp.wait_send()
            for w in range(na + 1):
                own_out(w).wait()

    return pl.pallas_call(
        body, grid=(nt,),
        in_specs=[pl.BlockSpec((tm, IN_W), lambda i: (i, 0)),
                  pl.BlockSpec((NCHIP, D, SHARD_W), lambda i: (0, 0, 0), pipeline_mode=pl.Buffered(1)),
                  pl.BlockSpec((tm, D), lambda i: (i, 0)), pl.BlockSpec((tm, D), lambda i: (i, 0)), _full((1, D))]
        + [_ANY] * (na + 1),
        out_specs=[pl.BlockSpec((tm, D), lambda i: (i, 0)), _full((1, D))] + [_ANY] * (na + 1),
        out_shape=[jax.ShapeDtypeStruct((L, D), f32), jax.ShapeDtypeStruct((1, D), f32)]
        + [jax.ShapeDtypeStruct(p.shape, bf16) for p in parts] + [jax.ShapeDtypeStruct((NCHIP, hs, 128), f32)],
        scratch_shapes=[pltpu.VMEM(p.shape[1:], bf16) for p in parts] + [pltpu.VMEM((hs, 128), f32)]
        + [pltpu.SemaphoreType.DMA((3 * ns,)), pltpu.SemaphoreType.DMA((3 * ns,)), pltpu.SemaphoreType.DMA((na + 1,))],
        name="x_grad_exchange", compiler_params=_cp("arbitrary"))(dproj, w_in, x, gx0, g_pre, *parts, small)


def _sibling_join_list(halves):
    na = len(halves)
    segs = _segments([(h.shape[0], h.shape[1], h.dtype) for h in halves])
    def body(*refs):
        ins, outs, stages = refs[:na], refs[na:2 * na], refs[2 * na:3 * na]
        send_sems, recv_sems, local_sems = refs[3 * na:]
        x, y, c, _ = _place()
        copies = [_rcopy(i, ins[w].at[pl.ds(r0, n), :], outs[w].at[pl.ds(c * half + r0, n), :], send_sems, recv_sems,
                         (x, y, 1 - c)) for i, (w, half, r0, n) in enumerate(segs)]
        for cp in copies:
            cp.start()
        cins = [pltpu.make_async_copy(ins[w], stages[w], local_sems.at[w]) for w in range(na)]
        for cp in cins:
            cp.start()
        own = []
        for w in range(na):
            cins[w].wait()
            half = halves[w].shape[0]
            own.append(pltpu.make_async_copy(stages[w], outs[w].at[pl.ds(c * half, half), :], local_sems.at[w]))
            own[-1].start()
        for cp in copies:
            cp.wait_recv()
        for cp in copies:
            cp.wait_send()
        for cp in own:
            cp.wait()

    return pl.pallas_call(
        body, in_specs=[_ANY] * na, out_specs=[_ANY] * na,
        out_shape=[jax.ShapeDtypeStruct((2 * h.shape[0], h.shape[1]), f32) for h in halves],
        scratch_shapes=[pltpu.VMEM(h.shape, f32) for h in halves]
        + [pltpu.SemaphoreType.DMA((len(segs),)), pltpu.SemaphoreType.DMA((len(segs),)), pltpu.SemaphoreType.DMA((na,))],
        name="sibling_join")(*halves)


def _small_join(v, fs_half):
    hs = fs_half.shape[0]
    def body(v_ref, h_ref, o_ref, fs_ref, send_sems, recv_sems):
        x, y, c, _ = _place()
        me = 4 * x + 2 * y + c
        o_ref[me] = v_ref[...]
        mine = pl.ds(pl.multiple_of(c * hs, 8), hs)
        fs_ref[mine, :] = h_ref[...]
        copies = [_rcopy(7, h_ref, fs_ref.at[mine, :], send_sems, recv_sems, (x, y, 1 - c))]
        i = 0
        for dx in range(2):
            for dy in range(2):
                for dc in range(2):
                    if dx + dy + dc:
                        copies.append(_rcopy(i, v_ref, o_ref.at[me], send_sems, recv_sems, (x ^ dx, y ^ dy, c ^ dc)))
                        i += 1
        for cp in copies:
            cp.start()
        for cp in copies:
            cp.wait_recv()
        for cp in copies:
            cp.wait_send()

    vm = pl.BlockSpec(memory_space=pltpu.VMEM)
    return pl.pallas_call(
        body, in_specs=[vm, vm], out_specs=[vm, vm],
        out_shape=[jax.ShapeDtypeStruct((8, 8, 128), f32), jax.ShapeDtypeStruct((2 * hs, 128), f32)],
        scratch_shapes=[pltpu.SemaphoreType.DMA((8,)), pltpu.SemaphoreType.DMA((8,))],
        name="small_join")(v, fs_half)


def _adamw_rows(parts, w, m, v):
    def body(p_ref, w_ref, m_ref, v_ref, g_ref, d_ref, m2_ref, v2_ref):
        g = p_ref[0]
        for dvc in range(1, 8):
            g = g + p_ref[dvc]
        g_ref[...] = g
        d, m2, v2 = _adamw_math(w_ref[...], g, m_ref[...], v_ref[...])
        d_ref[...] = d
        m2_ref[...] = m2
        v2_ref[...] = v2
    return pl.pallas_call(body, out_shape=[jax.ShapeDtypeStruct((8, 128), f32)] * 4, name="adamw_pre_norm_gain")(
        parts, w, m, v)


def _pair_exchange_list(grads, small):
    na = len(grads)
    segs = _segments([(g.shape[1] // 2, g.shape[2], g.dtype) for g in grads])
    n = NCHIP * len(segs) + 1
    def body(*refs):
        ins, s_ref, outs, rs_ref, (send_sems, recv_sems) = (refs[:na], refs[na], refs[na + 1:2 * na + 1],
                                                            refs[2 * na + 1], refs[2 * na + 2:])
        x, y, c, _ = _place()
        pieces = [(s_ref, rs_ref)]
        for j in range(NCHIP):
            for w, half, r0, rows in segs:
                pieces.append((ins[w].at[j, pl.ds((1 - c) * half + r0, rows), :], outs[w].at[j, pl.ds(r0, rows), :]))
        copies = [_rcopy(i, s, d, send_sems, recv_sems, (x, y, 1 - c)) for i, (s, d) in enumerate(pieces)]
        for cp in copies:
            cp.start()
        for cp in copies:
            cp.wait_recv()
        for cp in copies:
            cp.wait_send()

    return pl.pallas_call(
        body, in_specs=[_ANY] * (na + 1), out_specs=[_ANY] * (na + 1),
        out_shape=[jax.ShapeDtypeStruct((NCHIP, g.shape[1] // 2, g.shape[2]), g.dtype) for g in grads]
        + [jax.ShapeDtypeStruct((SMALL_ROWS, 128), f32)],
        scratch_shapes=[pltpu.SemaphoreType.DMA((n,)), pltpu.SemaphoreType.DMA((n,))],
        name="pair_exchange")(*grads, small)


def _pair_sum_list(c_arr, grads, recvs, small, rsmall):
    na = len(grads)
    def body(c_ref, *refs):
        g_refs, r_refs, s_ref, rs_ref = refs[:na], refs[na:2 * na], refs[2 * na], refs[2 * na + 1]
        o_refs, os_ref = refs[2 * na + 2:3 * na + 2], refs[3 * na + 2]
        for g_ref, r_ref, o_ref in zip(g_refs, r_refs, o_refs):
            o_ref[...] = (g_ref[...].astype(f32) + r_ref[...].astype(f32)).astype(bf16)
        os_ref[...] = s_ref[...] + rs_ref[...]
    half = lambda g: pl.BlockSpec((1, g.shape[1] // 2, g.shape[2]), lambda j, c: (j, c[0], 0))
    low = lambda g: pl.BlockSpec((1, g.shape[1] // 2, g.shape[2]), lambda j, c: (j, 0, 0))
    sm = pl.BlockSpec((SMALL_ROWS, 128), lambda j, c: (0, 0))
    grid_spec = pltpu.PrefetchScalarGridSpec(
        num_scalar_prefetch=1, grid=(NCHIP,),
        in_specs=[half(g) for g in grads] + [low(g) for g in grads] + [sm, sm],
        out_specs=[low(g) for g in grads] + [sm])
    return pl.pallas_call(
        body, grid_spec=grid_spec,
        out_shape=[jax.ShapeDtypeStruct((NCHIP, g.shape[1] // 2, g.shape[2]), bf16) for g in grads]
        + [jax.ShapeDtypeStruct((SMALL_ROWS, 128), f32)],
        name="pair_sum", compiler_params=_cp("arbitrary"))(c_arr, *grads, *recvs, small, rsmall)


def _chip_sum_list(parts, small):
    na = len(parts)
    nt = 2
    def body(*refs):
        for q_ref, f_ref in zip(refs[:na + 1], refs[na + 1:]):
            acc = q_ref[0].astype(f32)
            for j in range(1, NCHIP):
                acc = acc + q_ref[j].astype(f32)
            f_ref[...] = acc
    arrs = list(parts) + [small]
    return pl.pallas_call(
        body, grid=(nt,),
        in_specs=[pl.BlockSpec((NCHIP, a.shape[1] // nt, a.shape[2]), lambda i: (0, i, 0)) for a in arrs],
        out_specs=[pl.BlockSpec((a.shape[1] // nt, a.shape[2]), lambda i: (i, 0)) for a in arrs],
        out_shape=[jax.ShapeDtypeStruct(a.shape[1:], f32) for a in arrs],
        name="chip_sum", compiler_params=_cp("arbitrary"))(*arrs)


_SMALL =(("conv_b", (1, 1024)), ("conv_ln_gain", (1, 1024)), ("conv_ln_bias", (1, 1024)),
          ("ssm_lambda_re", (1, 32, 64)), ("ssm_lambda_im", (1, 32, 64)), ("ssm_log_dt", (1, 32)),
          ("ssm_b_re", (1, 32, 64, 16)), ("ssm_b_im", (1, 32, 64, 16)), ("ssm_c_re", (1, 32, 16, 64)),
          ("ssm_c_im", (1, 32, 16, 64)), ("ssm_d", (1, 32, 16)), ("b_ssm_glu", (1, 512)), ("post_norm_gain", (1, 1024)))


def _pack_small(vals, extra=None):
    rows = []
    for v in list(vals) + ([extra] if extra is not None else []):
        flat = v.reshape(-1).astype(f32)
        n = -(-flat.shape[0] // 1024) * 1024
        rows.append(jnp.pad(flat, (0, n - flat.shape[0])).reshape(-1, 128))
    used = sum(r.shape[0] for r in rows)
    rows.append(jnp.zeros((SMALL_ROWS - used, 128), f32))
    return jnp.concatenate(rows, axis=0)


def _unpack_small(p):
    o = 0
    out = []
    for _, shape in _SMALL:
        n = int(np.prod(shape))
        nr = -(-n // 1024) * 8
        out.append(p[o:o + nr].reshape(-1)[:n].reshape(shape))
        o += nr
    return out, p[o, 0]


def _discretize(lam_re, lam_im, log_dt, b_re, b_im):
    dt = jnp.exp(log_dt)[:, None]
    mag = jnp.exp(lam_re * dt)
    ar = mag * jnp.cos(lam_im * dt)
    ai = mag * jnp.sin(lam_im * dt)
    den = lam_re * lam_re + lam_im * lam_im
    zr = ((ar - 1.0) * lam_re + ai * lam_im) / den
    zi = (ai * lam_re - (ar - 1.0) * lam_im) / den
    bbr = zr[..., None] * b_re - zi[..., None] * b_im
    bbi = zr[..., None] * b_im + zi[..., None] * b_re
    return ar, ai, bbr, bbi


_EYE8 = np.eye(8, dtype=np.float32)


def _bbt_blocks(bb):
    v = bb.reshape(4, 8, PST, H).transpose(0, 1, 3, 2)
    return jnp.einsum("bghp,gk->bghkp", v, _EYE8).reshape(4, 128, 512)


def _bbt_unblock(m):
    v = jnp.einsum("bghkp,gk->bghp", m.reshape(4, 8, H, 8, PST), _EYE8)
    return v.transpose(0, 1, 3, 2).reshape(G, PST, H)


def _ct_blocks(cc):
    v = cc.reshape(4, 8, H, PST)
    return jnp.einsum("bghp,gk->bgpkh", v, _EYE8).reshape(4, 512, 128)


def _ct_unblock(m):
    return jnp.einsum("bghkp,gk->bghp", m.reshape(4, 8, H, 8, PST), _EYE8).reshape(G, H, PST)


def _perm_matrix():
    p = np.zeros((TC, TC), np.float32)
    for r in range(R):
        for seg in range(8):
            p[r * 8 + seg, seg * R + r] = 1.0
    return p


def _deinterleave(a):
    L, C = a.shape
    return a.reshape(L // TC, R, 8, C).transpose(0, 2, 1, 3).reshape(L, C)


def _fwd_bwd(h, xi, ti, proj, conv_w, w_co, w_glu, w_so, w_out, small):
    (conv_b, ln_g, ln_b, lam_re, lam_im, log_dt, b_re, b_im, c_re, c_im, dvec, b_glu, g_post) = small
    lam_re, lam_im, log_dt = lam_re[0], lam_im[0], log_dt[0]
    b_re, b_im, c_re, c_im = b_re[0], b_im[0], c_re[0], c_im[0]
    (ar, ai, bbr, bbi), disc_vjp = jax.vjp(_discretize, lam_re, lam_im, log_dt, b_re, b_im)
    a_re = ar.reshape(1, NS)
    a_im = ai.reshape(1, NS)
    dt = jnp.exp(log_dt)[:, None]
    steps = jnp.arange(1, R + 1, dtype=f32)[:, None, None]
    apow_re = (jnp.exp(steps * (lam_re * dt)) * jnp.cos(steps * (lam_im * dt))).reshape(R, NS)
    apow_im = (jnp.exp(steps * (lam_re * dt)) * jnp.sin(steps * (lam_im * dt))).reshape(R, NS)
    bbt_re, bbt_im = _bbt_blocks(bbr).astype(bf16), _bbt_blocks(bbi).astype(bf16)
    ct_re, ct_im = _ct_blocks(c_re).astype(bf16), _ct_blocks(c_im).astype(bf16)
    d_row = dvec.reshape(1, SW)
    cw32 = jnp.pad(conv_w, ((0, 1), (0, 0)))

    cu1, a_in = _conv_fwd(proj, cw32, conv_b, ln_g, ln_b)
    y0, b_in, sre, sim, cinr, cini = _ssm_fwd(proj, bbt_re, bbt_im, ct_re, ct_im, a_re, a_im,
                                              apow_re, apow_im, d_row, w_glu, b_glu)
    gx0, d_ain, d_bin, dproj, dw_out, dw_co, dw_so, dg_post, loss = _tail(
        a_in, b_in, proj, xi, ti, w_co, w_so, w_out, g_post)
    (dproj, dbbt_re, dbbt_im, dct_re, dct_im, dd, dar8, dai8, dw_glu, db_glu) = _ssm_bwd(
        d_bin, y0, proj, sre, sim, cinr, cini, bbt_re, bbt_im, ct_re, ct_im,
        a_re, a_im, apow_re, apow_im, d_row, w_glu, b_glu, dproj)
    dproj, dcw8, d_convb, d_lng, d_lnb = _conv_bwd(d_ain, cu1, proj, cw32, ln_g, ln_b, dproj)
    dw_in = _win_grad(h, dproj)

    d_ar = jnp.sum(dar8, axis=0).reshape(G, PST)
    d_ai = jnp.sum(dai8, axis=0).reshape(G, PST)
    d_lre, d_lim, d_ldt, d_bre, d_bim = disc_vjp((d_ar, d_ai, _bbt_unblock(dbbt_re), _bbt_unblock(dbbt_im)))
    d_conv_w = jnp.sum(dcw8, axis=1)[:KS]
    small_grads = [d_convb, d_lng, d_lnb, d_lre[None], d_lim[None], d_ldt[None], d_bre[None], d_bim[None],
                   _ct_unblock(dct_re)[None], _ct_unblock(dct_im)[None], dd.reshape(1, G, H), db_glu, dg_post]
    return loss[0, 0], gx0, dproj, (dw_in, dw_co, dw_out, dw_glu, dw_so, d_conv_w), small_grads


def kernel(x, pre_norm_gain, w_in, conv_w, conv_b, conv_ln_gain, conv_ln_bias, w_conv_out, ssm_lambda_re, ssm_lambda_im, ssm_log_dt, ssm_b_re, ssm_b_im, ssm_c_re, ssm_c_im, ssm_d, w_ssm_glu, b_ssm_glu, w_ssm_out, w_out, post_norm_gain, loss_target, m_pre_norm_gain, m_w_in, m_conv_w, m_conv_b, m_conv_ln_gain, m_conv_ln_bias, m_w_conv_out, m_ssm_lambda_re, m_ssm_lambda_im, m_ssm_log_dt, m_ssm_b_re, m_ssm_b_im, m_ssm_c_re, m_ssm_c_im, m_ssm_d, m_w_ssm_glu, m_b_ssm_glu, m_w_ssm_out, m_w_out, m_post_norm_gain, v_pre_norm_gain, v_w_in, v_conv_w, v_conv_b, v_conv_ln_gain, v_conv_ln_bias, v_w_conv_out, v_ssm_lambda_re, v_ssm_lambda_im, v_ssm_log_dt, v_ssm_b_re, v_ssm_b_im, v_ssm_c_re, v_ssm_c_im, v_ssm_d, v_w_ssm_glu, v_b_ssm_glu, v_w_ssm_out, v_w_out, v_post_norm_gain):
    c = lax.axis_index("c")
    shards = [w_in[0].astype(bf16), w_conv_out[0].astype(bf16), w_out[0].astype(bf16), w_ssm_glu[0].astype(bf16),
              w_ssm_out[0].astype(bf16), jnp.pad(conv_w[0], ((0, CONV_ROWS - KS), (0, 0)))]
    k_arr = (2 * lax.axis_index("x") + lax.axis_index("y")).astype(jnp.int32).reshape(1)
    w_in_g, w_co_g, w_out_g, w_glu_g, w_so_g, conv_w_g, h, xi, ti, proj = _gather_prep(
        k_arr, shards, x[0], loss_target[0], pre_norm_gain, jnp.asarray(_perm_matrix(), bf16))
    conv_w_f = conv_w_g[:, :KS].transpose(1, 0, 2).reshape(KS, CW)

    small = (conv_b, conv_ln_gain, conv_ln_bias, ssm_lambda_re, ssm_lambda_im, ssm_log_dt, ssm_b_re,
             ssm_b_im, ssm_c_re, ssm_c_im, ssm_d, b_ssm_glu, post_norm_gain)
    loss_part, gx0, dproj, big_grads, small_grads = _fwd_bwd(
        h, xi, ti, _proj_fwd(k_arr, h, w_in_g, proj), conv_w_f, w_co_g.reshape(CW, D), w_glu_g.reshape(SW, SW), w_so_g,
        w_out_g.reshape(D, D), small)

    dw_in, dw_co, dw_out, dw_glu, dw_so, d_conv_w = big_grads
    d_conv_w = jnp.pad(d_conv_w, ((0, CONV_ROWS - KS), (0, 0))).reshape(CONV_ROWS, NCHIP, 256).transpose(1, 0, 2)
    grads = [dw_in] + [g.astype(bf16) for g in (dw_co.reshape(NCHIP, 256, D), dw_out.reshape(NCHIP, 256, D),
                                                  dw_glu.reshape(NCHIP, 128, SW), dw_so, d_conv_w)]
    gs = _pack_small(small_grads, extra=loss_part)
    *recvs, rs = _pair_exchange_list(grads, gs)
    *parts, ps = _pair_sum_list(c.astype(jnp.int32).reshape(1), grads, recvs, gs, rs)
    gxi, dg_pre, *qparts, qs = _x_grad_exchange(dproj, w_in_g, xi, gx0, pre_norm_gain, parts, ps)
    grad_x = _deinterleave(gxi)
    *halves, fs_half = _chip_sum_list(qparts, qs)
    pre_parts, fs = _small_join(dg_pre.reshape(8, 128), fs_half)
    g_big = list(_sibling_join_list(halves))
    g_big[5] = g_big[5][:KS]

    big_w = (w_in[0], w_conv_out[0], w_out[0], w_ssm_glu[0], w_ssm_out[0], conv_w[0])
    big_m = (m_w_in[0], m_w_conv_out[0], m_w_out[0], m_w_ssm_glu[0], m_w_ssm_out[0], m_conv_w[0])
    big_v = (v_w_in[0], v_w_conv_out[0], v_w_out[0], v_w_ssm_glu[0], v_w_ssm_out[0], v_conv_w[0])
    big_names = ("w_in", "w_conv_out", "w_out", "w_ssm_glu", "w_ssm_out", "conv_w")
    res = {}
    upd = [_adamw("adamw_w_in", big_w[0], g_big[0], big_m[0], big_v[0])]
    upd += _adamw_group("adamw_rest", big_w[1:], g_big[1:], big_m[1:], big_v[1:])
    for n, g, (d, m2, v2) in zip(big_names, g_big, upd):
        res[n] = (g[None], d[None], m2[None], v2[None])

    small_m = (m_conv_b, m_conv_ln_gain, m_conv_ln_bias, m_ssm_lambda_re, m_ssm_lambda_im, m_ssm_log_dt,
               m_ssm_b_re, m_ssm_b_im, m_ssm_c_re, m_ssm_c_im, m_ssm_d, m_b_ssm_glu, m_post_norm_gain)
    small_v = (v_conv_b, v_conv_ln_gain, v_conv_ln_bias, v_ssm_lambda_re, v_ssm_lambda_im, v_ssm_log_dt,
               v_ssm_b_re, v_ssm_b_im, v_ssm_c_re, v_ssm_c_im, v_ssm_d, v_b_ssm_glu, v_post_norm_gain)
    sd, sm, sv = _adamw("adamw_small", _pack_small(small), fs, _pack_small(small_m), _pack_small(small_v))
    sg_l, loss = _unpack_small(fs)
    sd_l, _ = _unpack_small(sd)
    sm_l, _ = _unpack_small(sm)
    sv_l, _ = _unpack_small(sv)
    for i, (n, _) in enumerate(_SMALL):
        res[n] = (sg_l[i], sd_l[i], sm_l[i], sv_l[i])
    rows = lambda a: a.reshape(8, 128)
    pre = _adamw_rows(pre_parts, rows(pre_norm_gain), rows(m_pre_norm_gain), rows(v_pre_norm_gain))
    res["pre_norm_gain"] = tuple(a.reshape(1, D) for a in pre)

    order = ("pre_norm_gain", "w_in", "conv_w", "conv_b", "conv_ln_gain", "conv_ln_bias", "w_conv_out", "ssm_lambda_re",
             "ssm_lambda_im", "ssm_log_dt", "ssm_b_re", "ssm_b_im", "ssm_c_re", "ssm_c_im", "ssm_d", "w_ssm_glu",
             "b_ssm_glu", "w_ssm_out", "w_out", "post_norm_gain")
    outs = [loss, grad_x[None]]
    for q in range(4):
        outs.extend(res[n][q] for n in order)
    return tuple(outs)
```
